```python
import math
import jax, jax.numpy as jnp
from jax import lax
import numpy as np

D_MODEL = 1024
BATCH = 1
SEQ = 16384
DEPTH = 1

CHUNK = 64
Q_BLOCK = 128
EPS = 1e-6

DA_HEADS = 4
DA_HEAD_DIM = 64
DA_V_DIM = 2 * DA_HEAD_DIM
DA_WIDTH = DA_HEADS * DA_V_DIM
ALIBI_MAX = 8.0

RET_HEADS = 4
RET_QK_DIM = 64
RET_V_DIM = 128
RET_WIDTH = RET_HEADS * RET_V_DIM

MIX_WIDTH = DA_WIDTH + RET_WIDTH

DA_Q_COLS = DA_HEADS * 2 * DA_HEAD_DIM
DA_K_COLS = DA_HEADS * 2 * DA_HEAD_DIM
DA_V_COLS = DA_WIDTH
RET_Q_COLS = RET_HEADS * RET_QK_DIM
RET_K_COLS = RET_HEADS * RET_QK_DIM
RET_V_COLS = RET_WIDTH
RET_G_COLS = RET_WIDTH
IN_COLS = DA_Q_COLS + DA_K_COLS + DA_V_COLS + RET_Q_COLS + RET_K_COLS + RET_V_COLS + RET_G_COLS
IN_SPLITS = (DA_Q_COLS,
             DA_Q_COLS + DA_K_COLS,
             DA_Q_COLS + DA_K_COLS + DA_V_COLS,
             DA_Q_COLS + DA_K_COLS + DA_V_COLS + RET_Q_COLS,
             DA_Q_COLS + DA_K_COLS + DA_V_COLS + RET_Q_COLS + RET_K_COLS,
             DA_Q_COLS + DA_K_COLS + DA_V_COLS + RET_Q_COLS + RET_K_COLS + RET_V_COLS)

MOE_GROUPS = 4
MOE_EXPERTS_PER_GROUP = 8
MOE_EXPERTS = MOE_GROUPS * MOE_EXPERTS_PER_GROUP
MOE_TOP_K = 2
MOE_HIDDEN = 512
MOE_BLOCK = 128

kernel_name = "hybrid_diffattn_retention_hmoe"


def rmsnorm(x, g):
    xf = x.astype(jnp.float32)
    y = xf * lax.rsqrt(jnp.mean(xf * xf, axis=-1, keepdims=True) + EPS)
    return (y * g.astype(jnp.float32)).astype(x.dtype)


def head_rmsnorm(x):
    xf = x.astype(jnp.float32)
    return xf * lax.rsqrt(jnp.mean(xf * xf, axis=-1, keepdims=True) + EPS)


def alibi_slopes(n_heads):
    return jnp.exp2(-ALIBI_MAX * jnp.arange(1, n_heads + 1, dtype=jnp.float32) / n_heads)


def diff_attention(q, k, v, lam, subln_g, lambda_init):
    B, S = q.shape[0], q.shape[1]
    nb = S // Q_BLOCK
    scale = DA_HEAD_DIM ** -0.5
    slopes = alibi_slopes(DA_HEADS)
    k_pos = jnp.arange(S)
    k_chunk = k_pos // CHUNK
    qb = q.reshape(B, nb, Q_BLOCK, DA_HEADS, 2, DA_HEAD_DIM).transpose(1, 0, 2, 3, 4, 5)

    def one_block(args):
        i, q_blk = args
        q_pos = i * Q_BLOCK + jnp.arange(Q_BLOCK)
        s = jnp.einsum('bqhmd,bkhmd->bhmqk', q_blk, k,
                       preferred_element_type=jnp.float32) * scale
        dist = jnp.abs(q_pos[:, None] - k_pos[None, :]).astype(jnp.float32)
        bias = -slopes[:, None, None, None] * dist[None, None]
        allowed = k_chunk[None, :] <= (q_pos // CHUNK)[:, None]
        s = jnp.where(allowed, s + bias, -jnp.inf)
        p = jax.nn.softmax(s, axis=-1)
        a = p[:, :, 0] - lam * p[:, :, 1]
        return jnp.einsum('bhqk,bkhe->bqhe', a.astype(v.dtype), v)

    o = lax.map(one_block, (jnp.arange(nb), qb))
    o = o.transpose(1, 0, 2, 3, 4).reshape(B, S, DA_HEADS, DA_V_DIM)
    o = rmsnorm(o, subln_g) * (1.0 - lambda_init)
    return o.reshape(B, S, DA_WIDTH)


def retention(q, k, v, g):
    B, S = q.shape[0], q.shape[1]
    nc = S // CHUNK
    f32 = jnp.float32
    log_gamma = jnp.log1p(-jnp.exp2(-5.0 - jnp.arange(RET_HEADS, dtype=f32)))
    qc = q.astype(f32).reshape(B, nc, CHUNK, RET_HEADS, RET_QK_DIM)
    kc = (k.astype(f32) * RET_QK_DIM ** -0.5).reshape(B, nc, CHUNK, RET_HEADS, RET_QK_DIM)
    vc = v.astype(f32).reshape(B, nc, CHUNK, RET_HEADS, RET_V_DIM)
    pos = jnp.arange(CHUNK, dtype=f32)
    rel = pos[:, None] - pos[None, :]
    intra_decay = jnp.where(rel >= 0, jnp.exp(log_gamma[:, None, None] * jnp.maximum(rel, 0.0)), 0.0)
    s = jnp.einsum('bnchd,bnshd->bnhcs', qc, kc) * intra_decay
    intra = jnp.einsum('bnhcs,bnshe->bnche', s, vc)
    k_decay = jnp.exp(log_gamma[:, None] * (CHUNK - 1 - pos)[None, :])
    kv = jnp.einsum('bnshd,hs,bnshe->bnhde', kc, k_decay, vc)
    chunk_decay = jnp.exp(log_gamma * CHUNK)[:, None, None]

    def step(state, kv_n):
        return state * chunk_decay + kv_n, state

    init = jnp.zeros((B, RET_HEADS, RET_QK_DIM, RET_V_DIM), f32)
    _, prev = lax.scan(step, init, kv.transpose(1, 0, 2, 3, 4))
    prev = prev.transpose(1, 0, 2, 3, 4)
    q_decay = jnp.exp(log_gamma[:, None] * (pos + 1.0)[None, :])
    cross = jnp.einsum('bnchd,hc,bnhde->bnche', qc, q_decay, prev)
    o = head_rmsnorm((intra + cross).reshape(B, S, RET_HEADS, RET_V_DIM))
    o = jax.nn.silu(g.astype(f32)) * o
    return o.reshape(B, S, RET_WIDTH).astype(q.dtype)


def hier_moe(xn, wg, bg, we, be, w_gate, w_up, w_down):
    B, S, D = xn.shape
    T = B * S
    f32 = jnp.float32
    xt = xn.reshape(T, D)
    g_prob = jax.nn.softmax((xt @ wg + bg).astype(f32), axis=-1)
    g_top_p, g_top_i = lax.top_k(g_prob, 1)
    grp = g_top_i[:, 0]
    e_logits = (xt @ we + be).astype(f32).reshape(T, MOE_GROUPS, MOE_EXPERTS_PER_GROUP)
    idx = jnp.broadcast_to(grp[:, None, None], (T, 1, MOE_EXPERTS_PER_GROUP))
    e_logits = jnp.take_along_axis(e_logits, idx, axis=1)[:, 0]
    top_val, top_idx = lax.top_k(e_logits, MOE_TOP_K)
    weights = g_top_p * jax.nn.softmax(top_val, axis=-1)
    expert_id = grp[:, None] * MOE_EXPERTS_PER_GROUP + top_idx

    N = T * MOE_TOP_K
    e_flat = expert_id.reshape(N).astype(jnp.int32)
    w_flat = weights.reshape(N)
    tok_flat = jnp.repeat(jnp.arange(T, dtype=jnp.int32), MOE_TOP_K)
    order = jnp.argsort(e_flat)
    sorted_e = e_flat[order]
    counts = jnp.zeros((MOE_EXPERTS,), jnp.int32).at[e_flat].add(1)
    start = jnp.cumsum(counts) - counts
    padded = (counts + MOE_BLOCK - 1) // MOE_BLOCK * MOE_BLOCK
    pad_end = jnp.cumsum(padded)
    pad_start = pad_end - padded
    dest = pad_start[sorted_e] + jnp.arange(N, dtype=jnp.int32) - start[sorted_e]
    n_blocks = (N + MOE_BLOCK - 1) // MOE_BLOCK + MOE_EXPERTS
    nbuf = n_blocks * MOE_BLOCK
    buf_tok = jnp.full((nbuf,), T, jnp.int32).at[dest].set(tok_flat[order])
    buf_w = jnp.zeros((nbuf,), f32).at[dest].set(w_flat[order])
    block_e = jnp.minimum(jnp.searchsorted(pad_end, jnp.arange(n_blocks, dtype=jnp.int32) * MOE_BLOCK,
                                           side='right'), MOE_EXPERTS - 1)
    x_pad = jnp.concatenate([xt, jnp.zeros((1, D), xt.dtype)], axis=0)
    xb = x_pad[buf_tok].reshape(n_blocks, MOE_BLOCK, D)

    def expert_block(args):
        xblk, e = args
        h = jax.nn.silu(xblk @ w_gate[e]) * (xblk @ w_up[e])
        return h @ w_down[e]

    yb = lax.map(expert_block, (xb, block_e)).reshape(nbuf, D)
    y = jnp.zeros((T + 1, D), f32).at[buf_tok].add(yb.astype(f32) * buf_w[:, None])[:T]
    return y.reshape(B, S, D).astype(xn.dtype)


def setup_inputs(seed: int = 0) -> dict:
    key = jax.random.key(seed)
    ks = jax.random.split(key, 20)
    f32 = jnp.float32
    L, D = DEPTH, D_MODEL

    def nrm(k, shape, scale):
        return jax.random.normal(k, shape, f32) * scale

    return {
        "x": nrm(ks[0], (BATCH, SEQ, D), 1.0),
        "attn_norm_g": 1.0 + nrm(ks[1], (L, D), 0.02),
        "w_in": nrm(ks[2], (L, D, IN_COLS), D ** -0.5),
        "da_lambda_q1": nrm(ks[3], (L, DA_HEAD_DIM), 0.1),
        "da_lambda_k1": nrm(ks[4], (L, DA_HEAD_DIM), 0.1),
        "da_lambda_q2": nrm(ks[5], (L, DA_HEAD_DIM), 0.1),
        "da_lambda_k2": nrm(ks[6], (L, DA_HEAD_DIM), 0.1),
        "da_subln_g": 1.0 + nrm(ks[7], (L, DA_V_DIM), 0.02),
        "w_out": nrm(ks[8], (L, MIX_WIDTH, D), MIX_WIDTH ** -0.5),
        "ffn_norm_g": 1.0 + nrm(ks[9], (L, D), 0.02),
        "router_group_w": nrm(ks[10], (L, D, MOE_GROUPS), D ** -0.5),
        "router_group_b": nrm(ks[11], (L, MOE_GROUPS), 0.01),
        "router_expert_w": nrm(ks[12], (L, D, MOE_EXPERTS), D ** -0.5),
        "router_expert_b": nrm(ks[13], (L, MOE_EXPERTS), 0.01),
        "expert_w_gate": nrm(ks[14], (L, MOE_EXPERTS, D, MOE_HIDDEN), D ** -0.5),
        "expert_w_up": nrm(ks[15], (L, MOE_EXPERTS, D, MOE_HIDDEN), D ** -0.5),
        "expert_w_down": nrm(ks[16], (L, MOE_EXPERTS, MOE_HIDDEN, D), MOE_HIDDEN ** -0.5),
        "final_norm_g": 1.0 + nrm(ks[17], (D,), 0.02),
    }


def reference(x, attn_norm_g, w_in, da_lambda_q1, da_lambda_k1, da_lambda_q2, da_lambda_k2,
              da_subln_g, w_out, ffn_norm_g, router_group_w, router_group_b, router_expert_w,
              router_expert_b, expert_w_gate, expert_w_up, expert_w_down, final_norm_g):
    B, S, _ = x.shape
    h = x
    for l in range(DEPTH):
        xn = rmsnorm(h, attn_norm_g[l])
        proj = xn @ w_in[l]
        q_da, k_da, v_da, q_r, k_r, v_r, g_r = jnp.split(proj, IN_SPLITS, axis=-1)
        lambda_init = 0.8 - 0.6 * math.exp(-0.3 * l)
        lam = (jnp.exp(jnp.sum(da_lambda_q1[l].astype(jnp.float32) * da_lambda_k1[l].astype(jnp.float32)))
               - jnp.exp(jnp.sum(da_lambda_q2[l].astype(jnp.float32) * da_lambda_k2[l].astype(jnp.float32)))
               + lambda_init)
        o_da = diff_attention(q_da.reshape(B, S, DA_HEADS, 2, DA_HEAD_DIM),
                              k_da.reshape(B, S, DA_HEADS, 2, DA_HEAD_DIM),
                              v_da.reshape(B, S, DA_HEADS, DA_V_DIM),
                              lam, da_subln_g[l], lambda_init)
        o_r = retention(q_r.reshape(B, S, RET_HEADS, RET_QK_DIM),
                        k_r.reshape(B, S, RET_HEADS, RET_QK_DIM),
                        v_r.reshape(B, S, RET_HEADS, RET_V_DIM),
                        g_r.reshape(B, S, RET_HEADS, RET_V_DIM))
        mix = jnp.concatenate([o_da.astype(h.dtype), o_r.astype(h.dtype)], axis=-1)
        h = h + mix @ w_out[l]
        xn = rmsnorm(h, ffn_norm_g[l])
        h = h + hier_moe(xn, router_group_w[l], router_group_b[l], router_expert_w[l],
                         router_expert_b[l], expert_w_gate[l], expert_w_up[l], expert_w_down[l])
    return rmsnorm(h, final_norm_g)
```

```python
import functools
import math

import jax
import jax.numpy as jnp
from jax import lax
from jax.experimental import pallas as pl
from jax.experimental.pallas import tpu as pltpu

F32 = jnp.float32
BF16 = jnp.bfloat16
I32 = jnp.int32

D_MODEL = 1024
SEQ = 16384
CHUNK = 64
EPS = 1e-6

DA_HEADS = 4
DA_HEAD_DIM = 64
DA_V_DIM = 128
DA_WIDTH = 512
ALIBI_MAX = 8.0
RET_HEADS = 4
RET_QK_DIM = 64
RET_V_DIM = 128
RET_WIDTH = 512
IN_COLS = 3072
DA_COLS = 1536
RQ_OFF = 1536
RK_OFF = 1792
RV_OFF = 2048
RG_OFF = 2560

MOE_GROUPS = 4
MOE_EXPERTS_PER_GROUP = 8
MOE_EXPERTS = 32
MOE_HIDDEN = 512
LAMBDA_INIT = 0.8 - 0.6 * math.exp(-0.3 * 0)

LANES = 128
VMEM_LIMIT = 56 * 1024 * 1024

PROJ_TM = 512
ATT_T = 256
RET_C = 256
PLAN_T = 512
FFN_B = 256
N_ASSIGN = 2 * SEQ
N_BLOCKS = N_ASSIGN // FFN_B + MOE_EXPERTS
N_BLOCKS_PAD = (N_BLOCKS + 7) // 8 * 8
N_BUF = N_BLOCKS * FFN_B
COMB_TM = 256


def _params(sem):
    return pltpu.CompilerParams(dimension_semantics=sem, vmem_limit_bytes=VMEM_LIMIT)


def _inproj_kernel(x_ref, g_ref, w_ref, o_ref):
    x = x_ref[...]
    var = jnp.mean(x * x, axis=-1, keepdims=True)
    xn = (x * lax.rsqrt(var + EPS) * g_ref[...]).astype(BF16)
    for c in range(IN_COLS // 512):
        sl = slice(c * 512, (c + 1) * 512)
        o_ref[:, sl] = jnp.dot(xn, w_ref[:, sl], preferred_element_type=F32).astype(BF16)


def _inproj(x2, g, w_bf):
    return pl.pallas_call(
        _inproj_kernel,
        out_shape=jax.ShapeDtypeStruct((SEQ, IN_COLS), BF16),
        grid=(SEQ // PROJ_TM,),
        in_specs=[
            pl.BlockSpec((PROJ_TM, D_MODEL), lambda i: (i, 0)),
            pl.BlockSpec((1, D_MODEL), lambda i: (0, 0)),
            pl.BlockSpec((D_MODEL, IN_COLS), lambda i: (0, 0)),
        ],
        out_specs=pl.BlockSpec((PROJ_TM, IN_COLS), lambda i: (i, 0)),
        compiler_params=_params(("arbitrary",)),
        name="inproj",
    )(x2, g, w_bf)


def _attn_kernel(slope_ref, q_ref, k_ref, v_ref, boff_ref, bdiag_ref, lq1_ref, lk1_ref,
                 lq2_ref, lk2_ref, g_ref, o_ref, m_sc, l_sc, acc_sc):
    T = ATT_T
    h = pl.program_id(0)
    i = pl.program_id(1)
    slope = slope_ref[h]
    q = q_ref[...]
    lane = lax.broadcasted_iota(I32, q.shape, 1)
    zero = jnp.zeros_like(q)
    qs = (jnp.where(lane < DA_HEAD_DIM, q, zero), jnp.where(lane >= DA_HEAD_DIM, q, zero))

    def scores(mp, kt):
        return lax.dot_general(qs[mp], kt, (((1,), (1,)), ((), ())),
                               preferred_element_type=F32)

    d0 = pl.multiple_of(i * T, T)
    kt = k_ref[pl.ds(d0, T), :]
    vt = v_ref[pl.ds(d0, T), :]
    for mp in range(2):
        s = scores(mp, kt) + bdiag_ref[0]
        m = jnp.max(s, axis=1, keepdims=True)
        p = jnp.exp(s - m)
        m_sc[mp] = m
        l_sc[mp] = jnp.sum(p, axis=1, keepdims=True)
        acc_sc[mp] = jnp.dot(p.astype(BF16), vt, preferred_element_type=F32)

    def body(j, carry):
        j0 = pl.multiple_of(j * T, T)
        kt = k_ref[pl.ds(j0, T), :]
        vt = v_ref[pl.ds(j0, T), :]
        c = slope * ((i - j) * T).astype(F32)
        for mp in range(2):
            s = scores(mp, kt) + boff_ref[0]
            m_prev = m_sc[mp]
            m_new = jnp.maximum(m_prev, jnp.max(s, axis=1, keepdims=True) - c)
            p = jnp.exp(s - (m_new + c))
            alpha = jnp.exp(m_prev - m_new)
            l_sc[mp] = alpha * l_sc[mp] + jnp.sum(p, axis=1, keepdims=True)
            acc_sc[mp] = alpha * acc_sc[mp] + jnp.dot(p.astype(BF16), vt,
                                                      preferred_element_type=F32)
            m_sc[mp] = m_new
        return carry

    lax.fori_loop(0, i, body, 0)

    lam = (jnp.exp(jnp.sum(lq1_ref[...] * lk1_ref[...], axis=1, keepdims=True))
           - jnp.exp(jnp.sum(lq2_ref[...] * lk2_ref[...], axis=1, keepdims=True))
           + LAMBDA_INIT)
    o = acc_sc[0] / l_sc[0] - lam * (acc_sc[1] / l_sc[1])
    var = jnp.mean(o * o, axis=-1, keepdims=True)
    o = (o * lax.rsqrt(var + EPS) * g_ref[...]) * (1.0 - LAMBDA_INIT)
    o_ref[...] = o.astype(BF16)


def _attention(proj, slopes, boff, bdiag, lq1, lk1, lq2, lk2, subln_g):
    T = ATT_T
    vec64 = pl.BlockSpec((1, DA_HEAD_DIM), lambda h, i: (0, 0))
    return pl.pallas_call(
        _attn_kernel,
        out_shape=jax.ShapeDtypeStruct((SEQ, DA_WIDTH), BF16),
        grid=(DA_HEADS, SEQ // T),
        in_specs=[
            pl.BlockSpec(memory_space=pltpu.SMEM),
            pl.BlockSpec((T, LANES), lambda h, i: (i, h)),
            pl.BlockSpec((SEQ, LANES), lambda h, i: (0, DA_HEADS + h)),
            pl.BlockSpec((SEQ, LANES), lambda h, i: (0, 2 * DA_HEADS + h)),
            pl.BlockSpec((1, T, T), lambda h, i: (h, 0, 0)),
            pl.BlockSpec((1, T, T), lambda h, i: (h, 0, 0)),
            vec64, vec64, vec64, vec64,
            pl.BlockSpec((1, DA_V_DIM), lambda h, i: (0, 0)),
        ],
        out_specs=pl.BlockSpec((T, LANES), lambda h, i: (i, h)),
        scratch_shapes=[
            pltpu.VMEM((2, T, 1), F32),
            pltpu.VMEM((2, T, 1), F32),
            pltpu.VMEM((2, T, DA_V_DIM), F32),
        ],
        compiler_params=_params(("arbitrary", "arbitrary")),
        name="diff_attention",
    )(slopes, proj, proj, proj, boff, bdiag, lq1, lk1, lq2, lk2, subln_g)


def _ret_kernel(cd_ref, q_ref, k_ref, v_ref, g_ref, dec_ref, qdec_ref, kdec_ref, o_ref, st_sc):
    @pl.when(pl.program_id(0) == 0)
    def _():
        st_sc[...] = jnp.zeros_like(st_sc)

    for h in range(RET_HEADS):
        qk = slice(h * RET_QK_DIM, (h + 1) * RET_QK_DIM)
        vv = slice(h * RET_V_DIM, (h + 1) * RET_V_DIM)
        q = q_ref[:, qk]
        k = k_ref[:, qk]
        v = v_ref[:, vv]
        g = g_ref[:, vv].astype(F32)
        s = lax.dot_general(q, k, (((1,), (1,)), ((), ())),
                            preferred_element_type=F32) * dec_ref[h]
        intra = jnp.dot(s.astype(BF16), v, preferred_element_type=F32)
        st = st_sc[h]
        cross = jnp.dot(q, st.astype(BF16), preferred_element_type=F32) * qdec_ref[h]
        kd = (k.astype(F32) * kdec_ref[h]).astype(BF16)
        st_sc[h] = st * cd_ref[h] + lax.dot_general(kd, v, (((0,), (0,)), ((), ())),
                                                    preferred_element_type=F32)
        o = intra + cross
        o = o * lax.rsqrt(jnp.mean(o * o, axis=-1, keepdims=True) + EPS)
        o = (g / (1.0 + jnp.exp(-g))) * o
        o_ref[:, vv] = o.astype(BF16)


def _retention(proj, cd, dec, qdec, kdec):
    C = RET_C
    return pl.pallas_call(
        _ret_kernel,
        out_shape=jax.ShapeDtypeStruct((SEQ, RET_WIDTH), BF16),
        grid=(SEQ // C,),
        in_specs=[
            pl.BlockSpec(memory_space=pltpu.SMEM),
            pl.BlockSpec((C, 256), lambda i: (i, RQ_OFF // 256)),
            pl.BlockSpec((C, 256), lambda i: (i, RK_OFF // 256)),
            pl.BlockSpec((C, 512), lambda i: (i, RV_OFF // 512)),
            pl.BlockSpec((C, 512), lambda i: (i, RG_OFF // 512)),
            pl.BlockSpec((RET_HEADS, C, C), lambda i: (0, 0, 0)),
            pl.BlockSpec((RET_HEADS, C, 1), lambda i: (0, 0, 0)),
            pl.BlockSpec((RET_HEADS, C, 1), lambda i: (0, 0, 0)),
        ],
        out_specs=pl.BlockSpec((C, RET_WIDTH), lambda i: (i, 0)),
        scratch_shapes=[pltpu.VMEM((RET_HEADS, RET_QK_DIM, RET_V_DIM), F32)],
        compiler_params=_params(("arbitrary",)),
        name="retention",
    )(cd, proj, proj, proj, proj, dec, qdec, kdec)


def _outproj_router_kernel(x_ref, oda_ref, or_ref, wo_ref, g_ref, wr_ref, br_ref,
                           h1_ref, xn_ref, ri_ref, rw_ref):
    h1 = (x_ref[...]
          + jnp.dot(oda_ref[...], wo_ref[0:DA_WIDTH, :], preferred_element_type=F32)
          + jnp.dot(or_ref[...], wo_ref[DA_WIDTH:, :], preferred_element_type=F32))
    h1_ref[...] = h1
    var = jnp.mean(h1 * h1, axis=-1, keepdims=True)
    xn = h1 * lax.rsqrt(var + EPS) * g_ref[...]
    xn_ref[...] = xn
    logits = jnp.dot(xn, wr_ref[...], preferred_element_type=F32,
                     precision=lax.Precision.HIGHEST) + br_ref[...]
    lane = lax.broadcasted_iota(I32, logits.shape, 1)
    neg = jnp.float32(-jnp.inf)
    big = jnp.int32(1 << 20)
    gl = jnp.where(lane < MOE_GROUPS, logits, neg)
    gmax = jnp.max(gl, axis=1, keepdims=True)
    gidx = jnp.min(jnp.where(gl == gmax, lane, big), axis=1, keepdims=True)
    gsum = jnp.sum(jnp.exp(gl - gmax), axis=1, keepdims=True)
    gp = 1.0 / gsum
    lo = MOE_GROUPS + gidx * MOE_EXPERTS_PER_GROUP
    el = jnp.where((lane >= lo) & (lane < lo + MOE_EXPERTS_PER_GROUP), logits, neg)
    v1 = jnp.max(el, axis=1, keepdims=True)
    i1 = jnp.min(jnp.where(el == v1, lane, big), axis=1, keepdims=True)
    el2 = jnp.where(lane == i1, neg, el)
    v2 = jnp.max(el2, axis=1, keepdims=True)
    i2 = jnp.min(jnp.where(el2 == v2, lane, big), axis=1, keepdims=True)
    t = jnp.exp(v2 - v1)
    w1 = gp / (1.0 + t)
    w2 = gp * t / (1.0 + t)
    ri_ref[...] = jnp.where(lane == 0, i1 - MOE_GROUPS,
                            jnp.where(lane == 1, i2 - MOE_GROUPS, 0))
    rw_ref[...] = jnp.where(lane == 0, w1, jnp.where(lane == 1, w2, 0.0))


def _outproj_router(x2, o_da, o_r, wo_bf, g, wr, br):
    tm = PROJ_TM
    row = lambda w: pl.BlockSpec((tm, w), lambda i: (i, 0))
    full = lambda a, b: pl.BlockSpec((a, b), lambda i: (0, 0))
    return pl.pallas_call(
        _outproj_router_kernel,
        out_shape=(
            jax.ShapeDtypeStruct((SEQ, D_MODEL), F32),
            jax.ShapeDtypeStruct((SEQ, D_MODEL), F32),
            jax.ShapeDtypeStruct((SEQ, LANES), I32),
            jax.ShapeDtypeStruct((SEQ, LANES), F32),
        ),
        grid=(SEQ // tm,),
        in_specs=[row(D_MODEL), row(DA_WIDTH), row(RET_WIDTH), full(D_MODEL, D_MODEL),
                  full(1, D_MODEL), full(D_MODEL, LANES), full(1, LANES)],
        out_specs=(row(D_MODEL), row(D_MODEL), row(LANES), row(LANES)),
        compiler_params=_params(("arbitrary",)),
        name="outproj_router",
    )(x2, o_da, o_r, wo_bf, g, wr, br)


def _plan_kernel(ri_ref, dest_ref, blk_ref, used_ref):
    TT = PLAN_T
    lane = lax.broadcasted_iota(I32, (TT, LANES), 1)

    def onehots(t):
        r = ri_ref[pl.ds(pl.multiple_of(t * TT, TT), TT), :]
        return lane == r[:, 0:1], lane == r[:, 1:2]

    def count_body(t, acc):
        oh1, oh2 = onehots(t)
        return acc + jnp.sum((oh1 | oh2).astype(F32), axis=0, keepdims=True)

    counts = lax.fori_loop(0, SEQ // TT, count_body, jnp.zeros((1, LANES), F32))
    counts8 = jnp.broadcast_to(counts, (8, LANES)).astype(I32)
    shift = FFN_B.bit_length() - 1
    padded = ((counts8 + (FFN_B - 1)) >> shift) << shift
    lane8 = lax.broadcasted_iota(I32, (8, LANES), 1)
    pad_end = padded
    sh = 1
    while sh < LANES:
        pad_end = pad_end + jnp.where(lane8 >= sh, pltpu.roll(pad_end, sh, axis=1), 0)
        sh *= 2
    pad_start = pad_end - padded

    ltri = (lax.broadcasted_iota(I32, (TT, TT), 0)
            > lax.broadcasted_iota(I32, (TT, TT), 1)).astype(BF16)

    def dest_body(t, carry):
        oh1, oh2 = onehots(t)
        a = (oh1 | oh2).astype(F32)
        base = jnp.dot(ltri, a.astype(BF16), preferred_element_type=F32) + carry
        d1 = jnp.sum(jnp.where(oh1, base, 0.0), axis=1, keepdims=True)
        d2 = jnp.sum(jnp.where(oh2, base, 0.0), axis=1, keepdims=True)
        dest_ref[pl.ds(pl.multiple_of(t * TT, TT), TT), :] = jnp.where(
            lane == 0, d1, jnp.where(lane == 1, d2, 0.0)).astype(I32)
        return carry + jnp.sum(a, axis=0, keepdims=True)

    lax.fori_loop(0, SEQ // TT, dest_body, pad_start[0:1].astype(F32))

    rows = lax.broadcasted_iota(I32, (N_BLOCKS_PAD, LANES), 0) * FFN_B
    lanes = lax.broadcasted_iota(I32, (N_BLOCKS_PAD, LANES), 1)
    ended = (jnp.broadcast_to(pad_end[0:1], (N_BLOCKS_PAD, LANES)) <= rows) & (lanes < MOE_EXPERTS)
    be = jnp.sum(ended.astype(I32), axis=1, keepdims=True)
    blk_ref[...] = jnp.broadcast_to(jnp.minimum(be, MOE_EXPERTS - 1), (N_BLOCKS_PAD, LANES))
    total = jnp.max(pad_end, axis=1, keepdims=True)
    used_ref[...] = jnp.broadcast_to(total >> shift, (8, LANES))


def _plan(ri):
    return pl.pallas_call(
        _plan_kernel,
        out_shape=(
            jax.ShapeDtypeStruct((SEQ, LANES), I32),
            jax.ShapeDtypeStruct((N_BLOCKS_PAD, LANES), I32),
            jax.ShapeDtypeStruct((8, LANES), I32),
        ),
        compiler_params=pltpu.CompilerParams(vmem_limit_bytes=VMEM_LIMIT),
        name="route_plan",
    )(ri)


def _inverse_kernel(dest_ref, tok_ref):
    U = 8

    def fill(it, carry):
        for u in range(U):
            tok_ref[it * U + u] = 0
        return carry

    lax.fori_loop(0, N_BUF // U, fill, 0)

    def body(it, carry):
        for u in range(U):
            a = it * U + u
            tok_ref[dest_ref[a]] = a >> 1
        return carry

    lax.fori_loop(0, N_ASSIGN // U, body, 0)


def _inverse(dest_flat):
    smem = pl.BlockSpec(memory_space=pltpu.SMEM)
    return pl.pallas_call(
        _inverse_kernel,
        out_shape=jax.ShapeDtypeStruct((N_BUF,), I32),
        in_specs=[smem],
        out_specs=smem,
        name="slot_tokens",
    )(dest_flat)


def _ffn_gather_copy(xn_hbm, xbuf, sem, tok, slot, r):
    return pltpu.make_async_copy(xn_hbm.at[pl.ds(tok, 1)], xbuf.at[slot, pl.ds(r, 1)],
                                 sem.at[slot])


def _ffn_kernel(blk_ref, used_ref, tok_ref, xn_hbm, wg_ref, wu_ref, wd_ref, y_ref,
                xbuf, sem, wg_bf, wu_bf, wd_bf):
    B = FFN_B
    b = pl.program_id(0)
    used = used_ref[0]

    def start_gather(blk, slot):
        def issue(r, carry):
            _ffn_gather_copy(xn_hbm, xbuf, sem, tok_ref[blk * B + r], slot, r).start()
            return carry
        lax.fori_loop(0, B, issue, 0)

    @pl.when(b == 0)
    def _():
        start_gather(0, 0)

    @pl.when(b + 1 < used)
    def _():
        start_gather(b + 1, (b + 1) % 2)

    @pl.when(b < used)
    def _():
        slot = b % 2
        pltpu.make_async_copy(xn_hbm.at[pl.ds(0, B)], xbuf.at[slot], sem.at[slot]).wait()
        prev = blk_ref[jnp.maximum(b - 1, 0)]

        @pl.when((b == 0) | (blk_ref[b] != prev))
        def _():
            wg_bf[...] = wg_ref[0].astype(BF16)
            wu_bf[...] = wu_ref[0].astype(BF16)
            wd_bf[...] = wd_ref[0].astype(BF16)

        x = xbuf[slot].astype(BF16)
        hg = jnp.dot(x, wg_bf[...], preferred_element_type=F32)
        hu = jnp.dot(x, wu_bf[...], preferred_element_type=F32)
        hh = ((hg / (1.0 + jnp.exp(-hg))) * hu).astype(BF16)
        y_ref[...] = jnp.dot(hh, wd_bf[...], preferred_element_type=F32)

    @pl.when(b >= used)
    def _():
        y_ref[...] = jnp.zeros_like(y_ref)


def _ffn(blk_e, used, slot_tok, xn, w_gate, w_up, w_down):
    B = FFN_B
    wspec = lambda a, c: pl.BlockSpec((1, a, c), lambda b, blk, used, tok: (blk[b], 0, 0))
    return pl.pallas_call(
        _ffn_kernel,
        out_shape=jax.ShapeDtypeStruct((N_BUF, D_MODEL), F32),
        grid_spec=pltpu.PrefetchScalarGridSpec(
            num_scalar_prefetch=3,
            grid=(N_BLOCKS,),
            in_specs=[
                pl.BlockSpec(memory_space=pl.ANY),
                wspec(D_MODEL, MOE_HIDDEN),
                wspec(D_MODEL, MOE_HIDDEN),
                wspec(MOE_HIDDEN, D_MODEL),
            ],
            out_specs=pl.BlockSpec((B, D_MODEL), lambda b, blk, used, tok: (b, 0)),
            scratch_shapes=[
                pltpu.VMEM((2, B, D_MODEL), F32),
                pltpu.SemaphoreType.DMA((2,)),
                pltpu.VMEM((D_MODEL, MOE_HIDDEN), BF16),
                pltpu.VMEM((D_MODEL, MOE_HIDDEN), BF16),
                pltpu.VMEM((MOE_HIDDEN, D_MODEL), BF16),
            ],
        ),
        compiler_params=_params(("arbitrary",)),
        name="expert_ffn",
    )(blk_e, used, slot_tok, xn, w_gate, w_up, w_down)


def _combine_kernel(dest_ref, h1_ref, rw_ref, g_ref, y_hbm, o_ref, ybuf, sem):
    tm = COMB_TM
    i = pl.program_id(0)

    def issue(r, carry):
        a = (i * tm + r) * 2
        for kk in range(2):
            pltpu.make_async_copy(y_hbm.at[pl.ds(dest_ref[a + kk], 1)],
                                  ybuf.at[kk, pl.ds(r, 1)], sem.at[kk]).start()
        return carry

    lax.fori_loop(0, tm, issue, 0)
    for kk in range(2):
        pltpu.make_async_copy(y_hbm.at[pl.ds(0, tm)], ybuf.at[kk], sem.at[kk]).wait()
    w = rw_ref[...]
    h = h1_ref[...] + w[:, 0:1] * ybuf[0] + w[:, 1:2] * ybuf[1]
    var = jnp.mean(h * h, axis=-1, keepdims=True)
    o_ref[...] = h * lax.rsqrt(var + EPS) * g_ref[...]


def _combine(dest_flat, h1, rw, g, y):
    tm = COMB_TM
    return pl.pallas_call(
        _combine_kernel,
        out_shape=jax.ShapeDtypeStruct((SEQ, D_MODEL), F32),
        grid_spec=pltpu.PrefetchScalarGridSpec(
            num_scalar_prefetch=1,
            grid=(SEQ // tm,),
            in_specs=[
                pl.BlockSpec((tm, D_MODEL), lambda i, d: (i, 0)),
                pl.BlockSpec((tm, LANES), lambda i, d: (i, 0)),
                pl.BlockSpec((1, D_MODEL), lambda i, d: (0, 0)),
                pl.BlockSpec(memory_space=pl.ANY),
            ],
            out_specs=pl.BlockSpec((tm, D_MODEL), lambda i, d: (i, 0)),
            scratch_shapes=[
                pltpu.VMEM((2, tm, D_MODEL), F32),
                pltpu.SemaphoreType.DMA((2,)),
            ],
        ),
        compiler_params=_params(("arbitrary",)),
        name="moe_combine",
    )(dest_flat, h1, rw, g, y)


def _attention_tables():
    T = ATT_T
    slopes = jnp.exp2(-ALIBI_MAX * jnp.arange(1, DA_HEADS + 1, dtype=F32) / DA_HEADS)
    r = jnp.arange(T)
    rel = (r[:, None] - r[None, :]).astype(F32)
    boff = -slopes[:, None, None] * rel[None]
    allowed = (r[None, :] // CHUNK) <= (r[:, None] // CHUNK)
    bdiag = jnp.where(allowed[None], -slopes[:, None, None] * jnp.abs(rel)[None], -jnp.inf)
    return slopes, boff, bdiag


def _retention_tables():
    C = RET_C
    log_gamma = jnp.log1p(-jnp.exp2(-5.0 - jnp.arange(RET_HEADS, dtype=F32)))
    pos = jnp.arange(C, dtype=F32)
    rel = pos[:, None] - pos[None, :]
    dec = jnp.where(rel >= 0, jnp.exp(log_gamma[:, None, None] * jnp.maximum(rel, 0.0)), 0.0)
    qdec = jnp.exp(log_gamma[:, None] * (pos + 1.0)[None, :])[:, :, None]
    kdec = jnp.exp(log_gamma[:, None] * (C - 1 - pos)[None, :])[:, :, None]
    cd = jnp.exp(log_gamma * C)
    return cd, dec, qdec, kdec


def kernel(x, attn_norm_g, w_in, da_lambda_q1, da_lambda_k1, da_lambda_q2, da_lambda_k2,
           da_subln_g, w_out, ffn_norm_g, router_group_w, router_group_b, router_expert_w,
           router_expert_b, expert_w_gate, expert_w_up, expert_w_down, final_norm_g):
    B, S, D = x.shape
    assert (B, S, D) == (1, SEQ, D_MODEL)
    x2 = x.reshape(S, D)

    col_scale = jnp.concatenate([
        jnp.full((512,), DA_HEAD_DIM ** -0.5, F32), jnp.ones((1024,), F32),
        jnp.ones((256,), F32), jnp.full((256,), RET_QK_DIM ** -0.5, F32),
        jnp.ones((1024,), F32)])
    w_in_bf = (w_in[0] * col_scale[None, :]).astype(BF16)
    proj = _inproj(x2, attn_norm_g[0][None, :], w_in_bf)

    slopes, boff, bdiag = _attention_tables()
    o_da = _attention(proj, slopes, boff, bdiag, da_lambda_q1, da_lambda_k1, da_lambda_q2,
                      da_lambda_k2, da_subln_g)
    cd, dec, qdec, kdec = _retention_tables()
    o_r = _retention(proj, cd, dec, qdec, kdec)

    wr = jnp.zeros((D, LANES), F32)
    wr = wr.at[:, :MOE_GROUPS].set(router_group_w[0])
    wr = wr.at[:, MOE_GROUPS:MOE_GROUPS + MOE_EXPERTS].set(router_expert_w[0])
    br = jnp.zeros((1, LANES), F32)
    br = br.at[0, :MOE_GROUPS].set(router_group_b[0])
    br = br.at[0, MOE_GROUPS:MOE_GROUPS + MOE_EXPERTS].set(router_expert_b[0])
    h1, xn, ri, rw = _outproj_router(x2, o_da, o_r, w_out[0].astype(BF16),
                                     ffn_norm_g[0][None, :], wr, br)

    dest, blk, used = _plan(ri)
    dest_flat = dest[:, :2].reshape(N_ASSIGN)
    used1 = used[0, :1]
    blk_e = blk[:N_BLOCKS, 0]
    blk_e = blk_e[jnp.minimum(jnp.arange(N_BLOCKS, dtype=I32), used1[0] - 1)]
    slot_tok = _inverse(dest_flat)

    y = _ffn(blk_e, used1, slot_tok, xn, expert_w_gate[0], expert_w_up[0], expert_w_down[0])
    out = _combine(dest_flat, h1, rw, final_norm_g[None, :], y)
    return out.reshape(B, S, D)
```

```python
import functools
import math

import jax
import jax.numpy as jnp
from jax import lax
from jax.experimental import pallas as pl
from jax.experimental.pallas import tpu as pltpu

F32 = jnp.float32
BF16 = jnp.bfloat16
I32 = jnp.int32

D_MODEL = 1024
SEQ = 16384
CHUNK = 64
EPS = 1e-6

DA_HEADS = 4
DA_HEAD_DIM = 64
DA_V_DIM = 128
DA_WIDTH = 512
ALIBI_MAX = 8.0
RET_HEADS = 4
RET_QK_DIM = 64
RET_V_DIM = 128
RET_WIDTH = 512
KT_ROWS = 512
MAIN_COLS = 2560
DQ_OFF = 0
DV_OFF = 512
RQ_OFF = 1024
RK_OFF = 1280
RV_OFF = 1536
RG_OFF = 2048

MOE_GROUPS = 4
MOE_EXPERTS_PER_GROUP = 8
MOE_EXPERTS = 32
MOE_HIDDEN = 512
LAMBDA_INIT = 0.8 - 0.6 * math.exp(-0.3 * 0)

LANES = 128
VMEM_LIMIT = 56 * 1024 * 1024

PROJ_TM = 512
ATT_T = 512
RET_C = 256
PLAN_T = 512
FFN_B = 256
N_ASSIGN = 2 * SEQ
N_BLOCKS = N_ASSIGN // FFN_B + MOE_EXPERTS
N_BLOCKS_PAD = (N_BLOCKS + 7) // 8 * 8
N_BUF = N_BLOCKS * FFN_B
COMB_TM = 256


def _params(sem):
    return pltpu.CompilerParams(dimension_semantics=sem, vmem_limit_bytes=VMEM_LIMIT)


def _inproj_kernel(x_ref, g_ref, w_ref, wk_ref, o_ref, kt_ref):
    x = x_ref[...]
    var = jnp.mean(x * x, axis=-1, keepdims=True)
    xn = (x * lax.rsqrt(var + EPS) * g_ref[...]).astype(BF16)
    for c in range(MAIN_COLS // 512):
        sl = slice(c * 512, (c + 1) * 512)
        o_ref[:, sl] = jnp.dot(xn, w_ref[:, sl], preferred_element_type=F32).astype(BF16)
    kt = lax.dot_general(wk_ref[...], xn, (((1,), (1,)), ((), ())), preferred_element_type=F32)
    kt_ref[...] = kt.astype(BF16).reshape(DA_HEADS, 1, 2 * DA_HEAD_DIM, PROJ_TM)


def _inproj(x2, g, w_bf, wk_bf):
    return pl.pallas_call(
        _inproj_kernel,
        out_shape=(
            jax.ShapeDtypeStruct((SEQ, MAIN_COLS), BF16),
            jax.ShapeDtypeStruct((DA_HEADS, SEQ // PROJ_TM, 2 * DA_HEAD_DIM, PROJ_TM), BF16),
        ),
        grid=(SEQ // PROJ_TM,),
        in_specs=[
            pl.BlockSpec((PROJ_TM, D_MODEL), lambda i: (i, 0)),
            pl.BlockSpec((1, D_MODEL), lambda i: (0, 0)),
            pl.BlockSpec((D_MODEL, MAIN_COLS), lambda i: (0, 0)),
            pl.BlockSpec((KT_ROWS, D_MODEL), lambda i: (0, 0)),
        ],
        out_specs=(
            pl.BlockSpec((PROJ_TM, MAIN_COLS), lambda i: (i, 0)),
            pl.BlockSpec((DA_HEADS, 1, 2 * DA_HEAD_DIM, PROJ_TM), lambda i: (0, i, 0, 0)),
        ),
        compiler_params=_params(("arbitrary",)),
        name="inproj",
    )(x2, g, w_bf, wk_bf)


def _attn_kernel(slope_ref, q_ref, kt_ref, v_ref, qa1_ref, qa2_ref, ka_ref, dtab_ref,
                 lq1_ref, lk1_ref, lq2_ref, lk2_ref, g_ref, o_ref, m_sc, l_sc, acc_sc):
    T = ATT_T
    NL = T // LANES
    h = pl.program_id(0)
    i = pl.program_id(1)
    slope = slope_ref[h]
    q = q_ref[...]
    lane = lax.broadcasted_iota(I32, q.shape, 1)
    qs = (jnp.where(lane < DA_HEAD_DIM, q, qa1_ref[0]),
          jnp.where(lane >= DA_HEAD_DIM, q, qa2_ref[0]))
    ka = ka_ref[0]

    def k_aug(j, mp):
        if mp == 0:
            return jnp.concatenate([kt_ref[0, j, 0:DA_HEAD_DIM, :], ka], axis=0)
        return jnp.concatenate([ka, kt_ref[0, j, DA_HEAD_DIM:, :]], axis=0)

    def tile(j, c, first, extra):
        vt = v_ref[pl.ds(pl.multiple_of(j * T, T), T), :]
        for mp in range(2):
            s = jnp.dot(qs[mp], k_aug(j, mp), preferred_element_type=F32)
            if extra is not None:
                s = s + extra
            cols = [s[:, n * LANES:(n + 1) * LANES] for n in range(NL)]
            mx = cols[0]
            for n in range(1, NL):
                mx = jnp.maximum(mx, cols[n])
            rm = jnp.max(mx, axis=1, keepdims=True) - c
            if first:
                m_new = jnp.broadcast_to(rm, (T, LANES))
            else:
                m_prev = m_sc[mp]
                m_new = jnp.maximum(m_prev, rm)
            u = m_new + c
            ps = [jnp.exp(cols[n] - u) for n in range(NL)]
            lsum = ps[0]
            for n in range(1, NL):
                lsum = lsum + ps[n]
            p = jnp.concatenate(ps, axis=1).astype(BF16)
            pv = jnp.dot(p, vt, preferred_element_type=F32)
            if first:
                l_sc[mp] = lsum
                acc_sc[mp] = pv
            else:
                alpha = jnp.exp(m_prev - m_new)
                l_sc[mp] = alpha * l_sc[mp] + lsum
                acc_sc[mp] = alpha * acc_sc[mp] + pv
            m_sc[mp] = m_new

    tile(i, 0.0, True, dtab_ref[0])

    def body(j, carry):
        tile(j, slope * ((i - j) * T).astype(F32), False, None)
        return carry

    lax.fori_loop(0, i, body, 0)

    lam = (jnp.exp(jnp.sum(lq1_ref[...] * lk1_ref[...], axis=1, keepdims=True))
           - jnp.exp(jnp.sum(lq2_ref[...] * lk2_ref[...], axis=1, keepdims=True))
           + LAMBDA_INIT)
    l1 = jnp.sum(l_sc[0], axis=1, keepdims=True)
    l2 = jnp.sum(l_sc[1], axis=1, keepdims=True)
    o = acc_sc[0] / l1 - lam * (acc_sc[1] / l2)
    var = jnp.mean(o * o, axis=-1, keepdims=True)
    o = (o * lax.rsqrt(var + EPS) * g_ref[...]) * (1.0 - LAMBDA_INIT)
    o_ref[...] = o.astype(BF16)


def _attention(proj, kt4, slopes, qa1, qa2, ka, dtab, lq1, lk1, lq2, lk2, subln_g):
    T = ATT_T
    vec64 = pl.BlockSpec((1, DA_HEAD_DIM), lambda h, i: (0, 0))
    per_head = lambda a, b: pl.BlockSpec((1, a, b), lambda h, i: (h, 0, 0))
    return pl.pallas_call(
        _attn_kernel,
        out_shape=jax.ShapeDtypeStruct((SEQ, DA_WIDTH), BF16),
        grid=(DA_HEADS, SEQ // T),
        in_specs=[
            pl.BlockSpec(memory_space=pltpu.SMEM),
            pl.BlockSpec((T, LANES), lambda h, i: (i, DQ_OFF // LANES + h)),
            pl.BlockSpec((1, SEQ // T, 2 * DA_HEAD_DIM, T), lambda h, i: (h, 0, 0, 0)),
            pl.BlockSpec((SEQ, LANES), lambda h, i: (0, DV_OFF // LANES + h)),
            per_head(T, LANES), per_head(T, LANES), per_head(DA_HEAD_DIM, T), per_head(T, T),
            vec64, vec64, vec64, vec64,
            pl.BlockSpec((1, DA_V_DIM), lambda h, i: (0, 0)),
        ],
        out_specs=pl.BlockSpec((T, LANES), lambda h, i: (i, h)),
        scratch_shapes=[
            pltpu.VMEM((2, T, LANES), F32),
            pltpu.VMEM((2, T, LANES), F32),
            pltpu.VMEM((2, T, DA_V_DIM), F32),
        ],
        compiler_params=_params(("arbitrary", "arbitrary")),
        name="diff_attention",
    )(slopes, proj, kt4, proj, qa1, qa2, ka, dtab, lq1, lk1, lq2, lk2, subln_g)


def _ret_kernel(cd_ref, q_ref, k_ref, v_ref, g_ref, dec_ref, qdec_ref, kdec_ref, o_ref, st_sc):
    @pl.when(pl.program_id(0) == 0)
    def _():
        st_sc[...] = jnp.zeros_like(st_sc)

    for h in range(RET_HEADS):
        qk = slice(h * RET_QK_DIM, (h + 1) * RET_QK_DIM)
        vv = slice(h * RET_V_DIM, (h + 1) * RET_V_DIM)
        q = q_ref[:, qk]
        k = k_ref[:, qk]
        v = v_ref[:, vv]
        g = g_ref[:, vv].astype(F32)
        s = lax.dot_general(q, k, (((1,), (1,)), ((), ())),
                            preferred_element_type=F32) * dec_ref[h]
        intra = jnp.dot(s.astype(BF16), v, preferred_element_type=F32)
        st = st_sc[h]
        cross = jnp.dot(q, st.astype(BF16), preferred_element_type=F32) * qdec_ref[h]
        kd = (k.astype(F32) * kdec_ref[h]).astype(BF16)
        st_sc[h] = st * cd_ref[h] + lax.dot_general(kd, v, (((0,), (0,)), ((), ())),
                                                    preferred_element_type=F32)
        o = intra + cross
        o = o * lax.rsqrt(jnp.mean(o * o, axis=-1, keepdims=True) + EPS)
        o = (g / (1.0 + jnp.exp(-g))) * o
        o_ref[:, vv] = o.astype(BF16)


def _retention(proj, cd, dec, qdec, kdec):
    C = RET_C
    return pl.pallas_call(
        _ret_kernel,
        out_shape=jax.ShapeDtypeStruct((SEQ, RET_WIDTH), BF16),
        grid=(SEQ // C,),
        in_specs=[
            pl.BlockSpec(memory_space=pltpu.SMEM),
            pl.BlockSpec((C, 256), lambda i: (i, RQ_OFF // 256)),
            pl.BlockSpec((C, 256), lambda i: (i, RK_OFF // 256)),
            pl.BlockSpec((C, 512), lambda i: (i, RV_OFF // 512)),
            pl.BlockSpec((C, 512), lambda i: (i, RG_OFF // 512)),
            pl.BlockSpec((RET_HEADS, C, C), lambda i: (0, 0, 0)),
            pl.BlockSpec((RET_HEADS, C, 1), lambda i: (0, 0, 0)),
            pl.BlockSpec((RET_HEADS, C, 1), lambda i: (0, 0, 0)),
        ],
        out_specs=pl.BlockSpec((C, RET_WIDTH), lambda i: (i, 0)),
        scratch_shapes=[pltpu.VMEM((RET_HEADS, RET_QK_DIM, RET_V_DIM), F32)],
        compiler_params=_params(("arbitrary",)),
        name="retention",
    )(cd, proj, proj, proj, proj, dec, qdec, kdec)


def _outproj_router_kernel(x_ref, oda_ref, or_ref, wo_ref, g_ref, wr_ref, br_ref,
                           h1_ref, xn_ref, ri_ref, rw_ref):
    h1 = (x_ref[...]
          + jnp.dot(oda_ref[...], wo_ref[0:DA_WIDTH, :], preferred_element_type=F32)
          + jnp.dot(or_ref[...], wo_ref[DA_WIDTH:, :], preferred_element_type=F32))
    h1_ref[...] = h1
    var = jnp.mean(h1 * h1, axis=-1, keepdims=True)
    xn = h1 * lax.rsqrt(var + EPS) * g_ref[...]
    xn_ref[...] = xn
    logits = jnp.dot(xn, wr_ref[...], preferred_element_type=F32,
                     precision=lax.Precision.HIGHEST) + br_ref[...]
    lane = lax.broadcasted_iota(I32, logits.shape, 1)
    neg = jnp.float32(-jnp.inf)
    big = jnp.int32(1 << 20)
    gl = jnp.where(lane < MOE_GROUPS, logits, neg)
    gmax = jnp.max(gl, axis=1, keepdims=True)
    gidx = jnp.min(jnp.where(gl == gmax, lane, big), axis=1, keepdims=True)
    gsum = jnp.sum(jnp.exp(gl - gmax), axis=1, keepdims=True)
    gp = 1.0 / gsum
    lo = MOE_GROUPS + gidx * MOE_EXPERTS_PER_GROUP
    el = jnp.where((lane >= lo) & (lane < lo + MOE_EXPERTS_PER_GROUP), logits, neg)
    v1 = jnp.max(el, axis=1, keepdims=True)
    i1 = jnp.min(jnp.where(el == v1, lane, big), axis=1, keepdims=True)
    el2 = jnp.where(lane == i1, neg, el)
    v2 = jnp.max(el2, axis=1, keepdims=True)
    i2 = jnp.min(jnp.where(el2 == v2, lane, big), axis=1, keepdims=True)
    t = jnp.exp(v2 - v1)
    w1 = gp / (1.0 + t)
    w2 = gp * t / (1.0 + t)
    ri_ref[...] = jnp.where(lane == 0, i1 - MOE_GROUPS,
                            jnp.where(lane == 1, i2 - MOE_GROUPS, 0))
    rw_ref[...] = jnp.where(lane == 0, w1, jnp.where(lane == 1, w2, 0.0))


def _outproj_router(x2, o_da, o_r, wo_bf, g, wr, br):
    tm = PROJ_TM
    row = lambda w: pl.BlockSpec((tm, w), lambda i: (i, 0))
    full = lambda a, b: pl.BlockSpec((a, b), lambda i: (0, 0))
    return pl.pallas_call(
        _outproj_router_kernel,
        out_shape=(
            jax.ShapeDtypeStruct((SEQ, D_MODEL), F32),
            jax.ShapeDtypeStruct((SEQ, D_MODEL), F32),
            jax.ShapeDtypeStruct((SEQ, LANES), I32),
            jax.ShapeDtypeStruct((SEQ, LANES), F32),
        ),
        grid=(SEQ // tm,),
        in_specs=[row(D_MODEL), row(DA_WIDTH), row(RET_WIDTH), full(D_MODEL, D_MODEL),
                  full(1, D_MODEL), full(D_MODEL, LANES), full(1, LANES)],
        out_specs=(row(D_MODEL), row(D_MODEL), row(LANES), row(LANES)),
        compiler_params=_params(("arbitrary",)),
        name="outproj_router",
    )(x2, o_da, o_r, wo_bf, g, wr, br)


def _plan_kernel(ri_ref, dest_ref, blk_ref, used_ref):
    TT = PLAN_T
    lane = lax.broadcasted_iota(I32, (TT, LANES), 1)

    def onehots(t):
        r = ri_ref[pl.ds(pl.multiple_of(t * TT, TT), TT), :]
        return lane == r[:, 0:1], lane == r[:, 1:2]

    def count_body(t, acc):
        oh1, oh2 = onehots(t)
        return acc + jnp.sum((oh1 | oh2).astype(F32), axis=0, keepdims=True)

    counts = lax.fori_loop(0, SEQ // TT, count_body, jnp.zeros((1, LANES), F32))
    counts8 = jnp.broadcast_to(counts, (8, LANES)).astype(I32)
    shift = FFN_B.bit_length() - 1
    padded = ((counts8 + (FFN_B - 1)) >> shift) << shift
    lane8 = lax.broadcasted_iota(I32, (8, LANES), 1)
    pad_end = padded
    sh = 1
    while sh < LANES:
        pad_end = pad_end + jnp.where(lane8 >= sh, pltpu.roll(pad_end, sh, axis=1), 0)
        sh *= 2
    pad_start = pad_end - padded

    ltri = (lax.broadcasted_iota(I32, (TT, TT), 0)
            > lax.broadcasted_iota(I32, (TT, TT), 1)).astype(BF16)

    def dest_body(t, carry):
        oh1, oh2 = onehots(t)
        a = (oh1 | oh2).astype(F32)
        base = jnp.dot(ltri, a.astype(BF16), preferred_element_type=F32) + carry
        d1 = jnp.sum(jnp.where(oh1, base, 0.0), axis=1, keepdims=True)
        d2 = jnp.sum(jnp.where(oh2, base, 0.0), axis=1, keepdims=True)
        dest_ref[pl.ds(pl.multiple_of(t * TT, TT), TT), :] = jnp.where(
            lane == 0, d1, jnp.where(lane == 1, d2, 0.0)).astype(I32)
        return carry + jnp.sum(a, axis=0, keepdims=True)

    lax.fori_loop(0, SEQ // TT, dest_body, pad_start[0:1].astype(F32))

    rows = lax.broadcasted_iota(I32, (N_BLOCKS_PAD, LANES), 0) * FFN_B
    lanes = lax.broadcasted_iota(I32, (N_BLOCKS_PAD, LANES), 1)
    ended = (jnp.broadcast_to(pad_end[0:1], (N_BLOCKS_PAD, LANES)) <= rows) & (lanes < MOE_EXPERTS)
    be = jnp.sum(ended.astype(I32), axis=1, keepdims=True)
    blk_ref[...] = jnp.broadcast_to(jnp.minimum(be, MOE_EXPERTS - 1), (N_BLOCKS_PAD, LANES))
    total = jnp.max(pad_end, axis=1, keepdims=True)
    used_ref[...] = jnp.broadcast_to(total >> shift, (8, LANES))


def _plan(ri):
    return pl.pallas_call(
        _plan_kernel,
        out_shape=(
            jax.ShapeDtypeStruct((SEQ, LANES), I32),
            jax.ShapeDtypeStruct((N_BLOCKS_PAD, LANES), I32),
            jax.ShapeDtypeStruct((8, LANES), I32),
        ),
        compiler_params=pltpu.CompilerParams(vmem_limit_bytes=VMEM_LIMIT),
        name="route_plan",
    )(ri)


def _inverse_kernel(dest_ref, tok_ref):
    U = 8

    def fill(it, carry):
        for u in range(U):
            tok_ref[it * U + u] = 0
        return carry

    lax.fori_loop(0, N_BUF // U, fill, 0)

    def body(it, carry):
        for u in range(U):
            a = it * U + u
            tok_ref[dest_ref[a]] = a >> 1
        return carry

    lax.fori_loop(0, N_ASSIGN // U, body, 0)


def _inverse(dest_flat):
    smem = pl.BlockSpec(memory_space=pltpu.SMEM)
    return pl.pallas_call(
        _inverse_kernel,
        out_shape=jax.ShapeDtypeStruct((N_BUF,), I32),
        in_specs=[smem],
        out_specs=smem,
        name="slot_tokens",
    )(dest_flat)


def _ffn_gather_copy(xn_hbm, xbuf, sem, tok, slot, r):
    return pltpu.make_async_copy(xn_hbm.at[pl.ds(tok, 1)], xbuf.at[slot, pl.ds(r, 1)],
                                 sem.at[slot])


def _ffn_kernel(blk_ref, used_ref, tok_ref, xn_hbm, wg_ref, wu_ref, wd_ref, y_ref,
                xbuf, sem, wg_bf, wu_bf, wd_bf):
    B = FFN_B
    b = pl.program_id(0)
    used = used_ref[0]

    def start_gather(blk, slot):
        def issue(r, carry):
            _ffn_gather_copy(xn_hbm, xbuf, sem, tok_ref[blk * B + r], slot, r).start()
            return carry
        lax.fori_loop(0, B, issue, 0)

    @pl.when(b == 0)
    def _():
        start_gather(0, 0)

    @pl.when(b + 1 < used)
    def _():
        start_gather(b + 1, (b + 1) % 2)

    @pl.when(b < used)
    def _():
        slot = b % 2
        pltpu.make_async_copy(xn_hbm.at[pl.ds(0, B)], xbuf.at[slot], sem.at[slot]).wait()
        prev = blk_ref[jnp.maximum(b - 1, 0)]

        @pl.when((b == 0) | (blk_ref[b] != prev))
        def _():
            wg_bf[...] = wg_ref[0].astype(BF16)
            wu_bf[...] = wu_ref[0].astype(BF16)
            wd_bf[...] = wd_ref[0].astype(BF16)

        x = xbuf[slot].astype(BF16)
        hg = jnp.dot(x, wg_bf[...], preferred_element_type=F32)
        hu = jnp.dot(x, wu_bf[...], preferred_element_type=F32)
        hh = ((hg / (1.0 + jnp.exp(-hg))) * hu).astype(BF16)
        y_ref[...] = jnp.dot(hh, wd_bf[...], preferred_element_type=F32)

    @pl.when(b >= used)
    def _():
        y_ref[...] = jnp.zeros_like(y_ref)


def _ffn(blk_e, used, slot_tok, xn, w_gate, w_up, w_down):
    B = FFN_B
    wspec = lambda a, c: pl.BlockSpec((1, a, c), lambda b, blk, used, tok: (blk[b], 0, 0))
    return pl.pallas_call(
        _ffn_kernel,
        out_shape=jax.ShapeDtypeStruct((N_BUF, D_MODEL), F32),
        grid_spec=pltpu.PrefetchScalarGridSpec(
            num_scalar_prefetch=3,
            grid=(N_BLOCKS,),
            in_specs=[
                pl.BlockSpec(memory_space=pl.ANY),
                wspec(D_MODEL, MOE_HIDDEN),
                wspec(D_MODEL, MOE_HIDDEN),
                wspec(MOE_HIDDEN, D_MODEL),
            ],
            out_specs=pl.BlockSpec((B, D_MODEL), lambda b, blk, used, tok: (b, 0)),
            scratch_shapes=[
                pltpu.VMEM((2, B, D_MODEL), F32),
                pltpu.SemaphoreType.DMA((2,)),
                pltpu.VMEM((D_MODEL, MOE_HIDDEN), BF16),
                pltpu.VMEM((D_MODEL, MOE_HIDDEN), BF16),
                pltpu.VMEM((MOE_HIDDEN, D_MODEL), BF16),
            ],
        ),
        compiler_params=_params(("arbitrary",)),
        name="expert_ffn",
    )(blk_e, used, slot_tok, xn, w_gate, w_up, w_down)


def _combine_kernel(dest_ref, h1_ref, rw_ref, g_ref, y_hbm, o_ref, ybuf, sem):
    tm = COMB_TM
    i = pl.program_id(0)

    def issue(r, carry):
        a = (i * tm + r) * 2
        for kk in range(2):
            pltpu.make_async_copy(y_hbm.at[pl.ds(dest_ref[a + kk], 1)],
                                  ybuf.at[kk, pl.ds(r, 1)], sem.at[kk]).start()
        return carry

    lax.fori_loop(0, tm, issue, 0)
    for kk in range(2):
        pltpu.make_async_copy(y_hbm.at[pl.ds(0, tm)], ybuf.at[kk], sem.at[kk]).wait()
    w = rw_ref[...]
    h = h1_ref[...] + w[:, 0:1] * ybuf[0] + w[:, 1:2] * ybuf[1]
    var = jnp.mean(h * h, axis=-1, keepdims=True)
    o_ref[...] = h * lax.rsqrt(var + EPS) * g_ref[...]


def _combine(dest_flat, h1, rw, g, y):
    tm = COMB_TM
    return pl.pallas_call(
        _combine_kernel,
        out_shape=jax.ShapeDtypeStruct((SEQ, D_MODEL), F32),
        grid_spec=pltpu.PrefetchScalarGridSpec(
            num_scalar_prefetch=1,
            grid=(SEQ // tm,),
            in_specs=[
                pl.BlockSpec((tm, D_MODEL), lambda i, d: (i, 0)),
                pl.BlockSpec((tm, LANES), lambda i, d: (i, 0)),
                pl.BlockSpec((1, D_MODEL), lambda i, d: (0, 0)),
                pl.BlockSpec(memory_space=pl.ANY),
            ],
            out_specs=pl.BlockSpec((tm, D_MODEL), lambda i, d: (i, 0)),
            scratch_shapes=[
                pltpu.VMEM((2, tm, D_MODEL), F32),
                pltpu.SemaphoreType.DMA((2,)),
            ],
        ),
        compiler_params=_params(("arbitrary",)),
        name="moe_combine",
    )(dest_flat, h1, rw, g, y)


def _attention_tables():
    T = ATT_T
    slopes = jnp.exp2(-ALIBI_MAX * jnp.arange(1, DA_HEADS + 1, dtype=F32) / DA_HEADS)
    r = jnp.arange(T)
    hi = ((r // CHUNK) * CHUNK).astype(F32)
    lo = (r % CHUNK).astype(F32)
    ones = jnp.ones((T,), F32)
    sl = slopes[:, None]
    q_cols = jnp.stack([jnp.broadcast_to(ones, (DA_HEADS, T)), jnp.broadcast_to(ones, (DA_HEADS, T)),
                        -sl * hi[None], -sl * lo[None]], axis=-1)
    k_rows = jnp.stack([sl * hi[None], sl * lo[None], jnp.broadcast_to(ones, (DA_HEADS, T)),
                        jnp.broadcast_to(ones, (DA_HEADS, T))], axis=1)
    qa1 = jnp.zeros((DA_HEADS, T, LANES), F32).at[:, :, DA_HEAD_DIM:DA_HEAD_DIM + 4].set(q_cols)
    qa2 = jnp.zeros((DA_HEADS, T, LANES), F32).at[:, :, 0:4].set(q_cols)
    ka = jnp.zeros((DA_HEADS, DA_HEAD_DIM, T), F32).at[:, 0:4, :].set(k_rows)
    rel = (r[None, :] - r[:, None]).astype(F32)
    allowed = (r[None, :] // CHUNK) <= (r[:, None] // CHUNK)
    fix = jnp.where(rel > 0, -2.0 * slopes[:, None, None] * rel[None], 0.0)
    dtab = jnp.where(allowed[None], fix, -jnp.inf)
    return slopes, qa1.astype(BF16), qa2.astype(BF16), ka.astype(BF16), dtab


def _retention_tables():
    C = RET_C
    log_gamma = jnp.log1p(-jnp.exp2(-5.0 - jnp.arange(RET_HEADS, dtype=F32)))
    pos = jnp.arange(C, dtype=F32)
    rel = pos[:, None] - pos[None, :]
    dec = jnp.where(rel >= 0, jnp.exp(log_gamma[:, None, None] * jnp.maximum(rel, 0.0)), 0.0)
    qdec = jnp.exp(log_gamma[:, None] * (pos + 1.0)[None, :])[:, :, None]
    kdec = jnp.exp(log_gamma[:, None] * (C - 1 - pos)[None, :])[:, :, None]
    cd = jnp.exp(log_gamma * C)
    return cd, dec, qdec, kdec


def kernel(x, attn_norm_g, w_in, da_lambda_q1, da_lambda_k1, da_lambda_q2, da_lambda_k2,
           da_subln_g, w_out, ffn_norm_g, router_group_w, router_group_b, router_expert_w,
           router_expert_b, expert_w_gate, expert_w_up, expert_w_down, final_norm_g):
    B, S, D = x.shape
    assert (B, S, D) == (1, SEQ, D_MODEL)
    x2 = x.reshape(S, D)

    w = w_in[0]
    w_main = jnp.concatenate([
        w[:, 0:512] * (DA_HEAD_DIM ** -0.5),
        w[:, 1024:1536],
        w[:, 1536:1792],
        w[:, 1792:2048] * (RET_QK_DIM ** -0.5),
        w[:, 2048:3072]], axis=1).astype(BF16)
    wk_t = w[:, 512:1024].T.astype(BF16)
    proj, kt4 = _inproj(x2, attn_norm_g[0][None, :], w_main, wk_t)

    slopes, qa1, qa2, ka, dtab = _attention_tables()
    o_da = _attention(proj, kt4, slopes, qa1, qa2, ka, dtab, da_lambda_q1, da_lambda_k1,
                      da_lambda_q2, da_lambda_k2, da_subln_g)
    cd, dec, qdec, kdec = _retention_tables()
    o_r = _retention(proj, cd, dec, qdec, kdec)

    wr = jnp.zeros((D, LANES), F32)
    wr = wr.at[:, :MOE_GROUPS].set(router_group_w[0])
    wr = wr.at[:, MOE_GROUPS:MOE_GROUPS + MOE_EXPERTS].set(router_expert_w[0])
    br = jnp.zeros((1, LANES), F32)
    br = br.at[0, :MOE_GROUPS].set(router_group_b[0])
    br = br.at[0, MOE_GROUPS:MOE_GROUPS + MOE_EXPERTS].set(router_expert_b[0])
    h1, xn, ri, rw = _outproj_router(x2, o_da, o_r, w_out[0].astype(BF16),
                                     ffn_norm_g[0][None, :], wr, br)

    dest, blk, used = _plan(ri)
    dest_flat = dest[:, :2].reshape(N_ASSIGN)
    used1 = used[0, :1]
    blk_e = blk[:N_BLOCKS, 0]
    blk_e = blk_e[jnp.minimum(jnp.arange(N_BLOCKS, dtype=I32), used1[0] - 1)]
    slot_tok = _inverse(dest_flat)

    y = _ffn(blk_e, used1, slot_tok, xn, expert_w_gate[0], expert_w_up[0], expert_w_down[0])
    out = _combine(dest_flat, h1, rw, final_norm_g[None, :], y)
    return out.reshape(B, S, D)
```

```python
import functools
import math

import jax
import jax.numpy as jnp
from jax import lax
from jax.experimental import pallas as pl
from jax.experimental.pallas import tpu as pltpu

F32 = jnp.float32
BF16 = jnp.bfloat16
I32 = jnp.int32

D_MODEL = 1024
SEQ = 16384
CHUNK = 64
EPS = 1e-6

DA_HEADS = 4
DA_HEAD_DIM = 64
DA_V_DIM = 128
DA_WIDTH = 512
ALIBI_MAX = 8.0
RET_HEADS = 4
RET_QK_DIM = 64
RET_V_DIM = 128
RET_WIDTH = 512
T_ROWS = 512
MAIN_COLS = 2048
DK_OFF = 0
RQ_OFF = 512
RK_OFF = 768
RV_OFF = 1024
RG_OFF = 1536

MOE_GROUPS = 4
MOE_EXPERTS_PER_GROUP = 8
MOE_EXPERTS = 32
MOE_HIDDEN = 512
LAMBDA_INIT = 0.8 - 0.6 * math.exp(-0.3 * 0)

LANES = 128
VMEM_LIMIT = 56 * 1024 * 1024

PROJ_TM = 512
ATT_T = 512
RET_C = 256
PLAN_T = 512
FFN_B = 256
N_ASSIGN = 2 * SEQ
N_BLOCKS = N_ASSIGN // FFN_B + MOE_EXPERTS
N_BLOCKS_PAD = (N_BLOCKS + 7) // 8 * 8
N_BUF = N_BLOCKS * FFN_B
COMB_TM = 256


def _params(sem):
    return pltpu.CompilerParams(dimension_semantics=sem, vmem_limit_bytes=VMEM_LIMIT)


def _inproj_kernel(x_ref, g_ref, w_ref, wq_ref, wv_ref, o_ref, qt_ref, vt_ref):
    x = x_ref[...]
    var = jnp.mean(x * x, axis=-1, keepdims=True)
    xn = (x * lax.rsqrt(var + EPS) * g_ref[...]).astype(BF16)
    for c in range(MAIN_COLS // 512):
        sl = slice(c * 512, (c + 1) * 512)
        o_ref[:, sl] = jnp.dot(xn, w_ref[:, sl], preferred_element_type=F32).astype(BF16)
    nt = (((1,), (1,)), ((), ()))
    qt = lax.dot_general(wq_ref[...], xn, nt, preferred_element_type=F32)
    qt_ref[...] = qt.astype(BF16).reshape(DA_HEADS, 1, 2 * DA_HEAD_DIM, PROJ_TM)
    vt = lax.dot_general(wv_ref[...], xn, nt, preferred_element_type=F32)
    vt_ref[...] = vt.astype(BF16).reshape(DA_HEADS, 1, DA_V_DIM, PROJ_TM)


def _inproj(x2, g, w_bf, wq_t, wv_t):
    t_shape = jax.ShapeDtypeStruct((DA_HEADS, SEQ // PROJ_TM, LANES, PROJ_TM), BF16)
    t_spec = pl.BlockSpec((DA_HEADS, 1, LANES, PROJ_TM), lambda i: (0, i, 0, 0))
    w_t_spec = pl.BlockSpec((T_ROWS, D_MODEL), lambda i: (0, 0))
    return pl.pallas_call(
        _inproj_kernel,
        out_shape=(jax.ShapeDtypeStruct((SEQ, MAIN_COLS), BF16), t_shape, t_shape),
        grid=(SEQ // PROJ_TM,),
        in_specs=[
            pl.BlockSpec((PROJ_TM, D_MODEL), lambda i: (i, 0)),
            pl.BlockSpec((1, D_MODEL), lambda i: (0, 0)),
            pl.BlockSpec((D_MODEL, MAIN_COLS), lambda i: (0, 0)),
            w_t_spec, w_t_spec,
        ],
        out_specs=(pl.BlockSpec((PROJ_TM, MAIN_COLS), lambda i: (i, 0)), t_spec, t_spec),
        compiler_params=_params(("arbitrary",)),
        name="inproj",
    )(x2, g, w_bf, wq_t, wv_t)


ACC_ROWS = DA_V_DIM + 16


def _attn_kernel(slope_ref, qt_ref, k_ref, vt_ref, qa_ref, ka1_ref, ka2_ref, dtab_ref,
                 lq1_ref, lk1_ref, lq2_ref, lk2_ref, g_ref, o_ref,
                 m_sc, acc_sc, s0_sc, s1_sc, mx0_sc, mx1_sc):
    T = ATT_T
    h = pl.program_id(0)
    i = pl.program_id(1)
    slope = slope_ref[h]
    qt = qt_ref[0, 0]
    qa = qa_ref[0]
    qw = (jnp.concatenate([qt[0:DA_HEAD_DIM], qa], axis=0),
          jnp.concatenate([qa, qt[DA_HEAD_DIM:]], axis=0))
    lane = lax.broadcasted_iota(I32, (T, LANES), 1)
    sums_row = (lax.broadcasted_iota(I32, (16, T), 0) == 0).astype(BF16)
    s_bufs = (s0_sc, s1_sc)
    mx_bufs = (mx0_sc, mx1_sc)

    def scores(j, buf, extra):
        kt = k_ref[pl.ds(pl.multiple_of(j * T, T), T), :]
        ks = (jnp.where(lane < DA_HEAD_DIM, kt, ka1_ref[0]),
              jnp.where(lane >= DA_HEAD_DIM, kt, ka2_ref[0]))
        for mp in range(2):
            s = jnp.dot(ks[mp], qw[mp], preferred_element_type=F32)
            if extra is not None:
                s = s + extra
            s_bufs[buf][mp] = s
            mx_bufs[buf][mp] = jnp.max(s, axis=0, keepdims=True)

    def accumulate(j, buf):
        c = slope * ((i - j) * T).astype(F32)
        vte = jnp.concatenate([vt_ref[0, j], sums_row], axis=0)
        for mp in range(2):
            m_prev = m_sc[mp]
            m_new = jnp.maximum(m_prev, mx_bufs[buf][mp] - c)
            p = jnp.exp(s_bufs[buf][mp] - (m_new + c)).astype(BF16)
            pv = jnp.dot(vte, p, preferred_element_type=F32)
            acc_sc[mp] = jnp.exp(m_prev - m_new) * acc_sc[mp] + pv
            m_sc[mp] = m_new

    m_sc[...] = jnp.full_like(m_sc, -jnp.inf)
    acc_sc[...] = jnp.zeros_like(acc_sc)
    scores(i, 0, dtab_ref[0])

    def pair(k, carry):
        prev = jnp.where(k == 0, i, 2 * k - 1)
        accumulate(prev, 0)
        scores(2 * k, 1, None)
        accumulate(2 * k, 1)
        scores(2 * k + 1, 0, None)
        return carry

    npairs = i // 2
    lax.fori_loop(0, npairs, pair, 0)
    last = jnp.where(npairs == 0, i, 2 * npairs - 1)

    @pl.when(i % 2 == 1)
    def _():
        accumulate(last, 0)
        scores(i - 1, 1, None)
        accumulate(i - 1, 1)

    @pl.when(i % 2 == 0)
    def _():
        accumulate(last, 0)

    lam = (jnp.exp(jnp.sum(lq1_ref[...] * lk1_ref[...], axis=1, keepdims=True))
           - jnp.exp(jnp.sum(lq2_ref[...] * lk2_ref[...], axis=1, keepdims=True))
           + LAMBDA_INIT)
    a1 = acc_sc[0]
    a2 = acc_sc[1]
    ot = (a1[0:DA_V_DIM] / a1[DA_V_DIM:DA_V_DIM + 1]
          - lam * (a2[0:DA_V_DIM] / a2[DA_V_DIM:DA_V_DIM + 1]))
    o = ot.T
    var = jnp.mean(o * o, axis=-1, keepdims=True)
    o = (o * lax.rsqrt(var + EPS) * g_ref[...]) * (1.0 - LAMBDA_INIT)
    o_ref[...] = o.astype(BF16)


def _attention(proj, qt4, vt4, slopes, qa, ka1, ka2, dtab, lq1, lk1, lq2, lk2, subln_g):
    T = ATT_T
    vec64 = pl.BlockSpec((1, DA_HEAD_DIM), lambda h, i: (0, 0))
    per_head = lambda a, b: pl.BlockSpec((1, a, b), lambda h, i: (h, 0, 0))
    return pl.pallas_call(
        _attn_kernel,
        out_shape=jax.ShapeDtypeStruct((SEQ, DA_WIDTH), BF16),
        grid=(DA_HEADS, SEQ // T),
        in_specs=[
            pl.BlockSpec(memory_space=pltpu.SMEM),
            pl.BlockSpec((1, 1, LANES, T), lambda h, i: (h, i, 0, 0)),
            pl.BlockSpec((SEQ, LANES), lambda h, i: (0, DK_OFF // LANES + h)),
            pl.BlockSpec((1, SEQ // T, LANES, T), lambda h, i: (h, 0, 0, 0)),
            per_head(DA_HEAD_DIM, T), per_head(T, LANES), per_head(T, LANES), per_head(T, T),
            vec64, vec64, vec64, vec64,
            pl.BlockSpec((1, DA_V_DIM), lambda h, i: (0, 0)),
        ],
        out_specs=pl.BlockSpec((T, LANES), lambda h, i: (i, h)),
        scratch_shapes=[
            pltpu.VMEM((2, 1, T), F32),
            pltpu.VMEM((2, ACC_ROWS, T), F32),
            pltpu.VMEM((2, T, T), F32),
            pltpu.VMEM((2, T, T), F32),
            pltpu.VMEM((2, 1, T), F32),
            pltpu.VMEM((2, 1, T), F32),
        ],
        compiler_params=_params(("arbitrary", "arbitrary")),
        name="diff_attention",
    )(slopes, qt4, proj, vt4, qa, ka1, ka2, dtab, lq1, lk1, lq2, lk2, subln_g)


def _ret_kernel(cd_ref, q_ref, k_ref, v_ref, g_ref, dec_ref, qdec_ref, kdec_ref, o_ref, st_sc):
    @pl.when(pl.program_id(0) == 0)
    def _():
        st_sc[...] = jnp.zeros_like(st_sc)

    for h in range(RET_HEADS):
        qk = slice(h * RET_QK_DIM, (h + 1) * RET_QK_DIM)
        vv = slice(h * RET_V_DIM, (h + 1) * RET_V_DIM)
        q = q_ref[:, qk]
        k = k_ref[:, qk]
        v = v_ref[:, vv]
        g = g_ref[:, vv].astype(F32)
        s = lax.dot_general(q, k, (((1,), (1,)), ((), ())),
                            preferred_element_type=F32) * dec_ref[h]
        intra = jnp.dot(s.astype(BF16), v, preferred_element_type=F32)
        st = st_sc[h]
        cross = jnp.dot(q, st.astype(BF16), preferred_element_type=F32) * qdec_ref[h]
        kd = (k.astype(F32) * kdec_ref[h]).astype(BF16)
        st_sc[h] = st * cd_ref[h] + lax.dot_general(kd, v, (((0,), (0,)), ((), ())),
                                                    preferred_element_type=F32)
        o = intra + cross
        o = o * lax.rsqrt(jnp.mean(o * o, axis=-1, keepdims=True) + EPS)
        o = (g / (1.0 + jnp.exp(-g))) * o
        o_ref[:, vv] = o.astype(BF16)


def _retention(proj, cd, dec, qdec, kdec):
    C = RET_C
    return pl.pallas_call(
        _ret_kernel,
        out_shape=jax.ShapeDtypeStruct((SEQ, RET_WIDTH), BF16),
        grid=(SEQ // C,),
        in_specs=[
            pl.BlockSpec(memory_space=pltpu.SMEM),
            pl.BlockSpec((C, 256), lambda i: (i, RQ_OFF // 256)),
            pl.BlockSpec((C, 256), lambda i: (i, RK_OFF // 256)),
            pl.BlockSpec((C, 512), lambda i: (i, RV_OFF // 512)),
            pl.BlockSpec((C, 512), lambda i: (i, RG_OFF // 512)),
            pl.BlockSpec((RET_HEADS, C, C), lambda i: (0, 0, 0)),
            pl.BlockSpec((RET_HEADS, C, 1), lambda i: (0, 0, 0)),
            pl.BlockSpec((RET_HEADS, C, 1), lambda i: (0, 0, 0)),
        ],
        out_specs=pl.BlockSpec((C, RET_WIDTH), lambda i: (i, 0)),
        scratch_shapes=[pltpu.VMEM((RET_HEADS, RET_QK_DIM, RET_V_DIM), F32)],
        compiler_params=_params(("arbitrary",)),
        name="retention",
    )(cd, proj, proj, proj, proj, dec, qdec, kdec)


def _outproj_router_kernel(x_ref, oda_ref, or_ref, wo_ref, g_ref, wr_ref, br_ref,
                           h1_ref, xn_ref, ri_ref, rw_ref):
    h1 = (x_ref[...]
          + jnp.dot(oda_ref[...], wo_ref[0:DA_WIDTH, :], preferred_element_type=F32)
          + jnp.dot(or_ref[...], wo_ref[DA_WIDTH:, :], preferred_element_type=F32))
    h1_ref[...] = h1
    var = jnp.mean(h1 * h1, axis=-1, keepdims=True)
    xn = h1 * lax.rsqrt(var + EPS) * g_ref[...]
    xn_ref[...] = xn
    logits = jnp.dot(xn, wr_ref[...], preferred_element_type=F32,
                     precision=lax.Precision.HIGHEST) + br_ref[...]
    lane = lax.broadcasted_iota(I32, logits.shape, 1)
    neg = jnp.float32(-jnp.inf)
    big = jnp.int32(1 << 20)
    gl = jnp.where(lane < MOE_GROUPS, logits, neg)
    gmax = jnp.max(gl, axis=1, keepdims=True)
    gidx = jnp.min(jnp.where(gl == gmax, lane, big), axis=1, keepdims=True)
    gsum = jnp.sum(jnp.exp(gl - gmax), axis=1, keepdims=True)
    gp = 1.0 / gsum
    lo = MOE_GROUPS + gidx * MOE_EXPERTS_PER_GROUP
    el = jnp.where((lane >= lo) & (lane < lo + MOE_EXPERTS_PER_GROUP), logits, neg)
    v1 = jnp.max(el, axis=1, keepdims=True)
    i1 = jnp.min(jnp.where(el == v1, lane, big), axis=1, keepdims=True)
    el2 = jnp.where(lane == i1, neg, el)
    v2 = jnp.max(el2, axis=1, keepdims=True)
    i2 = jnp.min(jnp.where(el2 == v2, lane, big), axis=1, keepdims=True)
    t = jnp.exp(v2 - v1)
    w1 = gp / (1.0 + t)
    w2 = gp * t / (1.0 + t)
    ri_ref[...] = jnp.where(lane == 0, i1 - MOE_GROUPS,
                            jnp.where(lane == 1, i2 - MOE_GROUPS, 0))
    rw_ref[...] = jnp.where(lane == 0, w1, jnp.where(lane == 1, w2, 0.0))


def _outproj_router(x2, o_da, o_r, wo_bf, g, wr, br):
    tm = PROJ_TM
    row = lambda w: pl.BlockSpec((tm, w), lambda i: (i, 0))
    full = lambda a, b: pl.BlockSpec((a, b), lambda i: (0, 0))
    return pl.pallas_call(
        _outproj_router_kernel,
        out_shape=(
            jax.ShapeDtypeStruct((SEQ, D_MODEL), F32),
            jax.ShapeDtypeStruct((SEQ, D_MODEL), F32),
            jax.ShapeDtypeStruct((SEQ, LANES), I32),
            jax.ShapeDtypeStruct((SEQ, LANES), F32),
        ),
        grid=(SEQ // tm,),
        in_specs=[row(D_MODEL), row(DA_WIDTH), row(RET_WIDTH), full(D_MODEL, D_MODEL),
                  full(1, D_MODEL), full(D_MODEL, LANES), full(1, LANES)],
        out_specs=(row(D_MODEL), row(D_MODEL), row(LANES), row(LANES)),
        compiler_params=_params(("arbitrary",)),
        name="outproj_router",
    )(x2, o_da, o_r, wo_bf, g, wr, br)


def _plan_kernel(ri_ref, dest_ref, blk_ref, used_ref):
    TT = PLAN_T
    lane = lax.broadcasted_iota(I32, (TT, LANES), 1)

    def onehots(t):
        r = ri_ref[pl.ds(pl.multiple_of(t * TT, TT), TT), :]
        return lane == r[:, 0:1], lane == r[:, 1:2]

    def count_body(t, acc):
        oh1, oh2 = onehots(t)
        return acc + jnp.sum((oh1 | oh2).astype(F32), axis=0, keepdims=True)

    counts = lax.fori_loop(0, SEQ // TT, count_body, jnp.zeros((1, LANES), F32))
    counts8 = jnp.broadcast_to(counts, (8, LANES)).astype(I32)
    shift = FFN_B.bit_length() - 1
    padded = ((counts8 + (FFN_B - 1)) >> shift) << shift
    lane8 = lax.broadcasted_iota(I32, (8, LANES), 1)
    pad_end = padded
    sh = 1
    while sh < LANES:
        pad_end = pad_end + jnp.where(lane8 >= sh, pltpu.roll(pad_end, sh, axis=1), 0)
        sh *= 2
    pad_start = pad_end - padded

    ltri = (lax.broadcasted_iota(I32, (TT, TT), 0)
            > lax.broadcasted_iota(I32, (TT, TT), 1)).astype(BF16)

    def dest_body(t, carry):
        oh1, oh2 = onehots(t)
        a = (oh1 | oh2).astype(F32)
        base = jnp.dot(ltri, a.astype(BF16), preferred_element_type=F32) + carry
        d1 = jnp.sum(jnp.where(oh1, base, 0.0), axis=1, keepdims=True)
        d2 = jnp.sum(jnp.where(oh2, base, 0.0), axis=1, keepdims=True)
        dest_ref[pl.ds(pl.multiple_of(t * TT, TT), TT), :] = jnp.where(
            lane == 0, d1, jnp.where(lane == 1, d2, 0.0)).astype(I32)
        return carry + jnp.sum(a, axis=0, keepdims=True)

    lax.fori_loop(0, SEQ // TT, dest_body, pad_start[0:1].astype(F32))

    rows = lax.broadcasted_iota(I32, (N_BLOCKS_PAD, LANES), 0) * FFN_B
    lanes = lax.broadcasted_iota(I32, (N_BLOCKS_PAD, LANES), 1)
    ended = (jnp.broadcast_to(pad_end[0:1], (N_BLOCKS_PAD, LANES)) <= rows) & (lanes < MOE_EXPERTS)
    be = jnp.sum(ended.astype(I32), axis=1, keepdims=True)
    blk_ref[...] = jnp.broadcast_to(jnp.minimum(be, MOE_EXPERTS - 1), (N_BLOCKS_PAD, LANES))
    total = jnp.max(pad_end, axis=1, keepdims=True)
    used_ref[...] = jnp.broadcast_to(total >> shift, (8, LANES))


def _plan(ri):
    return pl.pallas_call(
        _plan_kernel,
        out_shape=(
            jax.ShapeDtypeStruct((SEQ, LANES), I32),
            jax.ShapeDtypeStruct((N_BLOCKS_PAD, LANES), I32),
            jax.ShapeDtypeStruct((8, LANES), I32),
        ),
        compiler_params=pltpu.CompilerParams(vmem_limit_bytes=VMEM_LIMIT),
        name="route_plan",
    )(ri)


def _inverse_kernel(dest_ref, tok_ref):
    U = 8

    def fill(it, carry):
        for u in range(U):
            tok_ref[it * U + u] = 0
        return carry

    lax.fori_loop(0, N_BUF // U, fill, 0)

    def body(it, carry):
        for u in range(U):
            a = it * U + u
            tok_ref[dest_ref[a]] = a >> 1
        return carry

    lax.fori_loop(0, N_ASSIGN // U, body, 0)


def _inverse(dest_flat):
    smem = pl.BlockSpec(memory_space=pltpu.SMEM)
    return pl.pallas_call(
        _inverse_kernel,
        out_shape=jax.ShapeDtypeStruct((N_BUF,), I32),
        in_specs=[smem],
        out_specs=smem,
        name="slot_tokens",
    )(dest_flat)


def _ffn_gather_copy(xn_hbm, xbuf, sem, tok, slot, r):
    return pltpu.make_async_copy(xn_hbm.at[pl.ds(tok, 1)], xbuf.at[slot, pl.ds(r, 1)],
                                 sem.at[slot])


def _ffn_kernel(blk_ref, used_ref, tok_ref, xn_hbm, wg_ref, wu_ref, wd_ref, y_ref,
                xbuf, sem, wg_bf, wu_bf, wd_bf):
    B = FFN_B
    b = pl.program_id(0)
    used = used_ref[0]

    def start_gather(blk, slot):
        def issue(r, carry):
            _ffn_gather_copy(xn_hbm, xbuf, sem, tok_ref[blk * B + r], slot, r).start()
            return carry
        lax.fori_loop(0, B, issue, 0)

    @pl.when(b == 0)
    def _():
        start_gather(0, 0)

    @pl.when(b + 1 < used)
    def _():
        start_gather(b + 1, (b + 1) % 2)

    @pl.when(b < used)
    def _():
        slot = b % 2
        pltpu.make_async_copy(xn_hbm.at[pl.ds(0, B)], xbuf.at[slot], sem.at[slot]).wait()
        prev = blk_ref[jnp.maximum(b - 1, 0)]

        @pl.when((b == 0) | (blk_ref[b] != prev))
        def _():
            wg_bf[...] = wg_ref[0].astype(BF16)
            wu_bf[...] = wu_ref[0].astype(BF16)
            wd_bf[...] = wd_ref[0].astype(BF16)

        x = xbuf[slot].astype(BF16)
        hg = jnp.dot(x, wg_bf[...], preferred_element_type=F32)
        hu = jnp.dot(x, wu_bf[...], preferred_element_type=F32)
        hh = ((hg / (1.0 + jnp.exp(-hg))) * hu).astype(BF16)
        y_ref[...] = jnp.dot(hh, wd_bf[...], preferred_element_type=F32)

    @pl.when(b >= used)
    def _():
        y_ref[...] = jnp.zeros_like(y_ref)


def _ffn(blk_e, used, slot_tok, xn, w_gate, w_up, w_down):
    B = FFN_B
    wspec = lambda a, c: pl.BlockSpec((1, a, c), lambda b, blk, used, tok: (blk[b], 0, 0))
    return pl.pallas_call(
        _ffn_kernel,
        out_shape=jax.ShapeDtypeStruct((N_BUF, D_MODEL), F32),
        grid_spec=pltpu.PrefetchScalarGridSpec(
            num_scalar_prefetch=3,
            grid=(N_BLOCKS,),
            in_specs=[
                pl.BlockSpec(memory_space=pl.ANY),
                wspec(D_MODEL, MOE_HIDDEN),
                wspec(D_MODEL, MOE_HIDDEN),
                wspec(MOE_HIDDEN, D_MODEL),
            ],
            out_specs=pl.BlockSpec((B, D_MODEL), lambda b, blk, used, tok: (b, 0)),
            scratch_shapes=[
                pltpu.VMEM((2, B, D_MODEL), F32),
                pltpu.SemaphoreType.DMA((2,)),
                pltpu.VMEM((D_MODEL, MOE_HIDDEN), BF16),
                pltpu.VMEM((D_MODEL, MOE_HIDDEN), BF16),
                pltpu.VMEM((MOE_HIDDEN, D_MODEL), BF16),
            ],
        ),
        compiler_params=_params(("arbitrary",)),
        name="expert_ffn",
    )(blk_e, used, slot_tok, xn, w_gate, w_up, w_down)


def _combine_kernel(dest_ref, h1_ref, rw_ref, g_ref, y_hbm, o_ref, ybuf, sem):
    tm = COMB_TM
    i = pl.program_id(0)

    def issue(r, carry):
        a = (i * tm + r) * 2
        for kk in range(2):
            pltpu.make_async_copy(y_hbm.at[pl.ds(dest_ref[a + kk], 1)],
                                  ybuf.at[kk, pl.ds(r, 1)], sem.at[kk]).start()
        return carry

    lax.fori_loop(0, tm, issue, 0)
    for kk in range(2):
        pltpu.make_async_copy(y_hbm.at[pl.ds(0, tm)], ybuf.at[kk], sem.at[kk]).wait()
    w = rw_ref[...]
    h = h1_ref[...] + w[:, 0:1] * ybuf[0] + w[:, 1:2] * ybuf[1]
    var = jnp.mean(h * h, axis=-1, keepdims=True)
    o_ref[...] = h * lax.rsqrt(var + EPS) * g_ref[...]


def _combine(dest_flat, h1, rw, g, y):
    tm = COMB_TM
    return pl.pallas_call(
        _combine_kernel,
        out_shape=jax.ShapeDtypeStruct((SEQ, D_MODEL), F32),
        grid_spec=pltpu.PrefetchScalarGridSpec(
            num_scalar_prefetch=1,
            grid=(SEQ // tm,),
            in_specs=[
                pl.BlockSpec((tm, D_MODEL), lambda i, d: (i, 0)),
                pl.BlockSpec((tm, LANES), lambda i, d: (i, 0)),
                pl.BlockSpec((1, D_MODEL), lambda i, d: (0, 0)),
                pl.BlockSpec(memory_space=pl.ANY),
            ],
            out_specs=pl.BlockSpec((tm, D_MODEL), lambda i, d: (i, 0)),
            scratch_shapes=[
                pltpu.VMEM((2, tm, D_MODEL), F32),
                pltpu.SemaphoreType.DMA((2,)),
            ],
        ),
        compiler_params=_params(("arbitrary",)),
        name="moe_combine",
    )(dest_flat, h1, rw, g, y)


def _attention_tables():
    T = ATT_T
    slopes = jnp.exp2(-ALIBI_MAX * jnp.arange(1, DA_HEADS + 1, dtype=F32) / DA_HEADS)
    r = jnp.arange(T)
    hi = ((r // CHUNK) * CHUNK).astype(F32)
    lo = (r % CHUNK).astype(F32)
    ones = jnp.ones((T,), F32)
    sl = slopes[:, None]
    one_h = jnp.broadcast_to(ones, (DA_HEADS, T))
    q_rows = jnp.stack([one_h, one_h, -sl * hi[None], -sl * lo[None]], axis=1)
    k_cols = jnp.stack([sl * hi[None], sl * lo[None], one_h, one_h], axis=-1)
    qa = jnp.zeros((DA_HEADS, DA_HEAD_DIM, T), F32).at[:, 0:4, :].set(q_rows)
    ka1 = jnp.zeros((DA_HEADS, T, LANES), F32).at[:, :, DA_HEAD_DIM:DA_HEAD_DIM + 4].set(k_cols)
    ka2 = jnp.zeros((DA_HEADS, T, LANES), F32).at[:, :, 0:4].set(k_cols)
    rel = (r[:, None] - r[None, :]).astype(F32)
    allowed = (r[:, None] // CHUNK) <= (r[None, :] // CHUNK)
    fix = jnp.where(rel > 0, -2.0 * slopes[:, None, None] * rel[None], 0.0)
    dtab = jnp.where(allowed[None], fix, -jnp.inf)
    return slopes, qa.astype(BF16), ka1.astype(BF16), ka2.astype(BF16), dtab


def _retention_tables():
    C = RET_C
    log_gamma = jnp.log1p(-jnp.exp2(-5.0 - jnp.arange(RET_HEADS, dtype=F32)))
    pos = jnp.arange(C, dtype=F32)
    rel = pos[:, None] - pos[None, :]
    dec = jnp.where(rel >= 0, jnp.exp(log_gamma[:, None, None] * jnp.maximum(rel, 0.0)), 0.0)
    qdec = jnp.exp(log_gamma[:, None] * (pos + 1.0)[None, :])[:, :, None]
    kdec = jnp.exp(log_gamma[:, None] * (C - 1 - pos)[None, :])[:, :, None]
    cd = jnp.exp(log_gamma * C)
    return cd, dec, qdec, kdec


def kernel(x, attn_norm_g, w_in, da_lambda_q1, da_lambda_k1, da_lambda_q2, da_lambda_k2,
           da_subln_g, w_out, ffn_norm_g, router_group_w, router_group_b, router_expert_w,
           router_expert_b, expert_w_gate, expert_w_up, expert_w_down, final_norm_g):
    B, S, D = x.shape
    assert (B, S, D) == (1, SEQ, D_MODEL)
    x2 = x.reshape(S, D)

    w = w_in[0]
    w_main = jnp.concatenate([
        w[:, 512:1024],
        w[:, 1536:1792],
        w[:, 1792:2048] * (RET_QK_DIM ** -0.5),
        w[:, 2048:3072]], axis=1).astype(BF16)
    wq_t = (w[:, 0:512] * (DA_HEAD_DIM ** -0.5)).T.astype(BF16)
    wv_t = w[:, 1024:1536].T.astype(BF16)
    proj, qt4, vt4 = _inproj(x2, attn_norm_g[0][None, :], w_main, wq_t, wv_t)

    slopes, qa, ka1, ka2, dtab = _attention_tables()
    o_da = _attention(proj, qt4, vt4, slopes, qa, ka1, ka2, dtab, da_lambda_q1, da_lambda_k1,
                      da_lambda_q2, da_lambda_k2, da_subln_g)
    cd, dec, qdec, kdec = _retention_tables()
    o_r = _retention(proj, cd, dec, qdec, kdec)

    wr = jnp.zeros((D, LANES), F32)
    wr = wr.at[:, :MOE_GROUPS].set(router_group_w[0])
    wr = wr.at[:, MOE_GROUPS:MOE_GROUPS + MOE_EXPERTS].set(router_expert_w[0])
    br = jnp.zeros((1, LANES), F32)
    br = br.at[0, :MOE_GROUPS].set(router_group_b[0])
    br = br.at[0, MOE_GROUPS:MOE_GROUPS + MOE_EXPERTS].set(router_expert_b[0])
    h1, xn, ri, rw = _outproj_router(x2, o_da, o_r, w_out[0].astype(BF16),
                                     ffn_norm_g[0][None, :], wr, br)

    dest, blk, used = _plan(ri)
    dest_flat = dest[:, :2].reshape(N_ASSIGN)
    used1 = used[0, :1]
    blk_e = blk[:N_BLOCKS, 0]
    blk_e = blk_e[jnp.minimum(jnp.arange(N_BLOCKS, dtype=I32), used1[0] - 1)]
    slot_tok = _inverse(dest_flat)

    y = _ffn(blk_e, used1, slot_tok, xn, expert_w_gate[0], expert_w_up[0], expert_w_down[0])
    out = _combine(dest_flat, h1, rw, final_norm_g[None, :], y)
    return out.reshape(B, S, D)
```

```python
import functools
import math

import jax
import jax.numpy as jnp
from jax import lax
from jax.experimental import pallas as pl
from jax.experimental.pallas import tpu as pltpu

F32 = jnp.float32
BF16 = jnp.bfloat16
I32 = jnp.int32

D_MODEL = 1024
SEQ = 16384
CHUNK = 64
EPS = 1e-6

DA_HEADS = 4
DA_HEAD_DIM = 64
DA_V_DIM = 128
DA_WIDTH = 512
ALIBI_MAX = 8.0
RET_HEADS = 4
RET_QK_DIM = 64
RET_V_DIM = 128
RET_WIDTH = 512
T_ROWS = 512
MAIN_COLS = 2048
DK_OFF = 0
RQ_OFF = 512
RK_OFF = 768
RV_OFF = 1024
RG_OFF = 1536

MOE_GROUPS = 4
MOE_EXPERTS_PER_GROUP = 8
MOE_EXPERTS = 32
MOE_HIDDEN = 512
LAMBDA_INIT = 0.8 - 0.6 * math.exp(-0.3 * 0)

LANES = 128
ROW_TILE = 8
VMEM_LIMIT = 56 * 1024 * 1024

PROJ_TM = 512
ATT_T = 512
RET_C = 256
PLAN_T = 512
FFN_B = 256
N_ASSIGN = 2 * SEQ
N_BLOCKS = N_ASSIGN // FFN_B + MOE_EXPERTS
N_BLOCKS_PAD = (N_BLOCKS + 7) // 8 * 8
N_BUF = N_BLOCKS * FFN_B
COMB_TM = 256
DISP_TM = 256
DMA_UNROLL = 8


def _params(sem):
    return pltpu.CompilerParams(dimension_semantics=sem, vmem_limit_bytes=VMEM_LIMIT)


def _inproj_kernel(x_ref, g_ref, w_ref, wq_ref, wv_ref, o_ref, qt_ref, vt_ref):
    x = x_ref[...]
    var = jnp.mean(x * x, axis=-1, keepdims=True)
    xn = (x * lax.rsqrt(var + EPS) * g_ref[...]).astype(BF16)
    for c in range(MAIN_COLS // 512):
        sl = slice(c * 512, (c + 1) * 512)
        o_ref[:, sl] = jnp.dot(xn, w_ref[:, sl], preferred_element_type=F32).astype(BF16)
    nt = (((1,), (1,)), ((), ()))
    qt = lax.dot_general(wq_ref[...], xn, nt, preferred_element_type=F32)
    qt_ref[...] = qt.astype(BF16).reshape(DA_HEADS, 1, 2 * DA_HEAD_DIM, PROJ_TM)
    vt = lax.dot_general(wv_ref[...], xn, nt, preferred_element_type=F32)
    vt_ref[...] = vt.astype(BF16).reshape(DA_HEADS, 1, DA_V_DIM, PROJ_TM)


def _inproj(x2, g, w_bf, wq_t, wv_t):
    t_shape = jax.ShapeDtypeStruct((DA_HEADS, SEQ // PROJ_TM, LANES, PROJ_TM), BF16)
    t_spec = pl.BlockSpec((DA_HEADS, 1, LANES, PROJ_TM), lambda i: (0, i, 0, 0))
    w_t_spec = pl.BlockSpec((T_ROWS, D_MODEL), lambda i: (0, 0))
    return pl.pallas_call(
        _inproj_kernel,
        out_shape=(jax.ShapeDtypeStruct((SEQ, MAIN_COLS), BF16), t_shape, t_shape),
        grid=(SEQ // PROJ_TM,),
        in_specs=[
            pl.BlockSpec((PROJ_TM, D_MODEL), lambda i: (i, 0)),
            pl.BlockSpec((1, D_MODEL), lambda i: (0, 0)),
            pl.BlockSpec((D_MODEL, MAIN_COLS), lambda i: (0, 0)),
            w_t_spec, w_t_spec,
        ],
        out_specs=(pl.BlockSpec((PROJ_TM, MAIN_COLS), lambda i: (i, 0)), t_spec, t_spec),
        compiler_params=_params(("arbitrary",)),
        name="inproj",
    )(x2, g, w_bf, wq_t, wv_t)


ACC_ROWS = DA_V_DIM + 16


def _attn_kernel(slope_ref, qt_ref, k_ref, vt_ref, qa_ref, ka1_ref, ka2_ref, dtab_ref,
                 lq1_ref, lk1_ref, lq2_ref, lk2_ref, g_ref, o_ref,
                 m_sc, acc_sc, s0_sc, s1_sc, mx0_sc, mx1_sc):
    T = ATT_T
    h = pl.program_id(0)
    i = pl.program_id(1)
    slope = slope_ref[h]
    qt = qt_ref[0, 0]
    qa = qa_ref[0]
    qw = (jnp.concatenate([qt[0:DA_HEAD_DIM], qa], axis=0),
          jnp.concatenate([qa, qt[DA_HEAD_DIM:]], axis=0))
    lane = lax.broadcasted_iota(I32, (T, LANES), 1)
    sums_row = (lax.broadcasted_iota(I32, (16, T), 0) == 0).astype(BF16)
    s_bufs = (s0_sc, s1_sc)
    mx_bufs = (mx0_sc, mx1_sc)

    def scores(j, buf, extra):
        kt = k_ref[pl.ds(pl.multiple_of(j * T, T), T), :]
        ks = (jnp.where(lane < DA_HEAD_DIM, kt, ka1_ref[0]),
              jnp.where(lane >= DA_HEAD_DIM, kt, ka2_ref[0]))
        for mp in range(2):
            s = jnp.dot(ks[mp], qw[mp], preferred_element_type=F32)
            if extra is not None:
                s = s + extra
            s_bufs[buf][mp] = s
            mx_bufs[buf][mp] = jnp.max(s, axis=0, keepdims=True)

    def accumulate(j, buf):
        c = slope * ((i - j) * T).astype(F32)
        vte = jnp.concatenate([vt_ref[0, j], sums_row], axis=0)
        for mp in range(2):
            m_prev = m_sc[mp]
            m_new = jnp.maximum(m_prev, mx_bufs[buf][mp] - c)
            p = jnp.exp(s_bufs[buf][mp] - (m_new + c)).astype(BF16)
            pv = jnp.dot(vte, p, preferred_element_type=F32)
            acc_sc[mp] = jnp.exp(m_prev - m_new) * acc_sc[mp] + pv
            m_sc[mp] = m_new

    m_sc[...] = jnp.full_like(m_sc, -jnp.inf)
    acc_sc[...] = jnp.zeros_like(acc_sc)
    scores(i, 0, dtab_ref[0])

    def pair(k, carry):
        prev = jnp.where(k == 0, i, 2 * k - 1)
        accumulate(prev, 0)
        scores(2 * k, 1, None)
        accumulate(2 * k, 1)
        scores(2 * k + 1, 0, None)
        return carry

    npairs = i // 2
    lax.fori_loop(0, npairs, pair, 0)
    last = jnp.where(npairs == 0, i, 2 * npairs - 1)

    @pl.when(i % 2 == 1)
    def _():
        accumulate(last, 0)
        scores(i - 1, 1, None)
        accumulate(i - 1, 1)

    @pl.when(i % 2 == 0)
    def _():
        accumulate(last, 0)

    lam = (jnp.exp(jnp.sum(lq1_ref[...] * lk1_ref[...], axis=1, keepdims=True))
           - jnp.exp(jnp.sum(lq2_ref[...] * lk2_ref[...], axis=1, keepdims=True))
           + LAMBDA_INIT)
    a1 = acc_sc[0]
    a2 = acc_sc[1]
    ot = (a1[0:DA_V_DIM] / a1[DA_V_DIM:DA_V_DIM + 1]
          - lam * (a2[0:DA_V_DIM] / a2[DA_V_DIM:DA_V_DIM + 1]))
    o = ot.T
    var = jnp.mean(o * o, axis=-1, keepdims=True)
    o = (o * lax.rsqrt(var + EPS) * g_ref[...]) * (1.0 - LAMBDA_INIT)
    o_ref[...] = o.astype(BF16)


def _attention(proj, qt4, vt4, slopes, qa, ka1, ka2, dtab, lq1, lk1, lq2, lk2, subln_g):
    T = ATT_T
    vec64 = pl.BlockSpec((1, DA_HEAD_DIM), lambda h, i: (0, 0))
    per_head = lambda a, b: pl.BlockSpec((1, a, b), lambda h, i: (h, 0, 0))
    return pl.pallas_call(
        _attn_kernel,
        out_shape=jax.ShapeDtypeStruct((SEQ, DA_WIDTH), BF16),
        grid=(DA_HEADS, SEQ // T),
        in_specs=[
            pl.BlockSpec(memory_space=pltpu.SMEM),
            pl.BlockSpec((1, 1, LANES, T), lambda h, i: (h, i, 0, 0)),
            pl.BlockSpec((SEQ, LANES), lambda h, i: (0, DK_OFF // LANES + h)),
            pl.BlockSpec((1, SEQ // T, LANES, T), lambda h, i: (h, 0, 0, 0)),
            per_head(DA_HEAD_DIM, T), per_head(T, LANES), per_head(T, LANES), per_head(T, T),
            vec64, vec64, vec64, vec64,
            pl.BlockSpec((1, DA_V_DIM), lambda h, i: (0, 0)),
        ],
        out_specs=pl.BlockSpec((T, LANES), lambda h, i: (i, h)),
        scratch_shapes=[
            pltpu.VMEM((2, 1, T), F32),
            pltpu.VMEM((2, ACC_ROWS, T), F32),
            pltpu.VMEM((2, T, T), F32),
            pltpu.VMEM((2, T, T), F32),
            pltpu.VMEM((2, 1, T), F32),
            pltpu.VMEM((2, 1, T), F32),
        ],
        compiler_params=_params(("arbitrary", "arbitrary")),
        name="diff_attention",
    )(slopes, qt4, proj, vt4, qa, ka1, ka2, dtab, lq1, lk1, lq2, lk2, subln_g)


def _ret_kernel(cd_ref, q_ref, k_ref, v_ref, g_ref, dec_ref, qdec_ref, kdec_ref, o_ref, st_sc):
    @pl.when(pl.program_id(0) == 0)
    def _():
        st_sc[...] = jnp.zeros_like(st_sc)

    for h in range(RET_HEADS):
        qk = slice(h * RET_QK_DIM, (h + 1) * RET_QK_DIM)
        vv = slice(h * RET_V_DIM, (h + 1) * RET_V_DIM)
        q = q_ref[:, qk]
        k = k_ref[:, qk]
        v = v_ref[:, vv]
        g = g_ref[:, vv].astype(F32)
        s = lax.dot_general(q, k, (((1,), (1,)), ((), ())),
                            preferred_element_type=F32) * dec_ref[h]
        intra = jnp.dot(s.astype(BF16), v, preferred_element_type=F32)
        st = st_sc[h]
        cross = jnp.dot(q, st.astype(BF16), preferred_element_type=F32) * qdec_ref[h]
        kd = (k.astype(F32) * kdec_ref[h]).astype(BF16)
        st_sc[h] = st * cd_ref[h] + lax.dot_general(kd, v, (((0,), (0,)), ((), ())),
                                                    preferred_element_type=F32)
        o = intra + cross
        o = o * lax.rsqrt(jnp.mean(o * o, axis=-1, keepdims=True) + EPS)
        o = (g / (1.0 + jnp.exp(-g))) * o
        o_ref[:, vv] = o.astype(BF16)


def _retention(proj, cd, dec, qdec, kdec):
    C = RET_C
    return pl.pallas_call(
        _ret_kernel,
        out_shape=jax.ShapeDtypeStruct((SEQ, RET_WIDTH), BF16),
        grid=(SEQ // C,),
        in_specs=[
            pl.BlockSpec(memory_space=pltpu.SMEM),
            pl.BlockSpec((C, 256), lambda i: (i, RQ_OFF // 256)),
            pl.BlockSpec((C, 256), lambda i: (i, RK_OFF // 256)),
            pl.BlockSpec((C, 512), lambda i: (i, RV_OFF // 512)),
            pl.BlockSpec((C, 512), lambda i: (i, RG_OFF // 512)),
            pl.BlockSpec((RET_HEADS, C, C), lambda i: (0, 0, 0)),
            pl.BlockSpec((RET_HEADS, C, 1), lambda i: (0, 0, 0)),
            pl.BlockSpec((RET_HEADS, C, 1), lambda i: (0, 0, 0)),
        ],
        out_specs=pl.BlockSpec((C, RET_WIDTH), lambda i: (i, 0)),
        scratch_shapes=[pltpu.VMEM((RET_HEADS, RET_QK_DIM, RET_V_DIM), F32)],
        compiler_params=_params(("arbitrary",)),
        name="retention",
    )(cd, proj, proj, proj, proj, dec, qdec, kdec)


def _outproj_router_kernel(x_ref, oda_ref, or_ref, wo_ref, g_ref, wr_ref, br_ref,
                           h1_ref, xn_ref, ri_ref, rw_ref):
    h1 = (x_ref[...]
          + jnp.dot(oda_ref[...], wo_ref[0:DA_WIDTH, :], preferred_element_type=F32)
          + jnp.dot(or_ref[...], wo_ref[DA_WIDTH:, :], preferred_element_type=F32))
    h1_ref[...] = h1
    var = jnp.mean(h1 * h1, axis=-1, keepdims=True)
    xn = h1 * lax.rsqrt(var + EPS) * g_ref[...]
    for c in range(ROW_TILE):
        xn_ref[pl.ds(c, PROJ_TM, stride=ROW_TILE), :] = xn[:, c * LANES:(c + 1) * LANES]
    logits = jnp.dot(xn, wr_ref[...], preferred_element_type=F32,
                     precision=lax.Precision.HIGHEST) + br_ref[...]
    lane = lax.broadcasted_iota(I32, logits.shape, 1)
    neg = jnp.float32(-jnp.inf)
    big = jnp.int32(1 << 20)
    gl = jnp.where(lane < MOE_GROUPS, logits, neg)
    gmax = jnp.max(gl, axis=1, keepdims=True)
    gidx = jnp.min(jnp.where(gl == gmax, lane, big), axis=1, keepdims=True)
    gsum = jnp.sum(jnp.exp(gl - gmax), axis=1, keepdims=True)
    gp = 1.0 / gsum
    lo = MOE_GROUPS + gidx * MOE_EXPERTS_PER_GROUP
    el = jnp.where((lane >= lo) & (lane < lo + MOE_EXPERTS_PER_GROUP), logits, neg)
    v1 = jnp.max(el, axis=1, keepdims=True)
    i1 = jnp.min(jnp.where(el == v1, lane, big), axis=1, keepdims=True)
    el2 = jnp.where(lane == i1, neg, el)
    v2 = jnp.max(el2, axis=1, keepdims=True)
    i2 = jnp.min(jnp.where(el2 == v2, lane, big), axis=1, keepdims=True)
    t = jnp.exp(v2 - v1)
    w1 = gp / (1.0 + t)
    w2 = gp * t / (1.0 + t)
    ri_ref[...] = jnp.where(lane == 0, i1 - MOE_GROUPS,
                            jnp.where(lane == 1, i2 - MOE_GROUPS, 0))
    rw_ref[...] = jnp.where(lane == 0, w1, jnp.where(lane == 1, w2, 0.0))


def _outproj_router(x2, o_da, o_r, wo_bf, g, wr, br):
    tm = PROJ_TM
    row = lambda w: pl.BlockSpec((tm, w), lambda i: (i, 0))
    full = lambda a, b: pl.BlockSpec((a, b), lambda i: (0, 0))
    return pl.pallas_call(
        _outproj_router_kernel,
        out_shape=(
            jax.ShapeDtypeStruct((SEQ, D_MODEL), F32),
            jax.ShapeDtypeStruct((SEQ * ROW_TILE, LANES), F32),
            jax.ShapeDtypeStruct((SEQ, LANES), I32),
            jax.ShapeDtypeStruct((SEQ, LANES), F32),
        ),
        grid=(SEQ // tm,),
        in_specs=[row(D_MODEL), row(DA_WIDTH), row(RET_WIDTH), full(D_MODEL, D_MODEL),
                  full(1, D_MODEL), full(D_MODEL, LANES), full(1, LANES)],
        out_specs=(row(D_MODEL), pl.BlockSpec((tm * ROW_TILE, LANES), lambda i: (i, 0)),
                   row(LANES), row(LANES)),
        compiler_params=_params(("arbitrary",)),
        name="outproj_router",
    )(x2, o_da, o_r, wo_bf, g, wr, br)


def _plan_kernel(ri_ref, dest_ref, blk_ref, used_ref):
    TT = PLAN_T
    lane = lax.broadcasted_iota(I32, (TT, LANES), 1)

    def onehots(t):
        r = ri_ref[pl.ds(pl.multiple_of(t * TT, TT), TT), :]
        return lane == r[:, 0:1], lane == r[:, 1:2]

    def count_body(t, acc):
        oh1, oh2 = onehots(t)
        return acc + jnp.sum((oh1 | oh2).astype(F32), axis=0, keepdims=True)

    counts = lax.fori_loop(0, SEQ // TT, count_body, jnp.zeros((1, LANES), F32))
    counts8 = jnp.broadcast_to(counts, (8, LANES)).astype(I32)
    shift = FFN_B.bit_length() - 1
    padded = ((counts8 + (FFN_B - 1)) >> shift) << shift
    lane8 = lax.broadcasted_iota(I32, (8, LANES), 1)
    pad_end = padded
    sh = 1
    while sh < LANES:
        pad_end = pad_end + jnp.where(lane8 >= sh, pltpu.roll(pad_end, sh, axis=1), 0)
        sh *= 2
    pad_start = pad_end - padded

    ltri = (lax.broadcasted_iota(I32, (TT, TT), 0)
            > lax.broadcasted_iota(I32, (TT, TT), 1)).astype(BF16)

    def dest_body(t, carry):
        oh1, oh2 = onehots(t)
        a = (oh1 | oh2).astype(F32)
        base = jnp.dot(ltri, a.astype(BF16), preferred_element_type=F32) + carry
        d1 = jnp.sum(jnp.where(oh1, base, 0.0), axis=1, keepdims=True)
        d2 = jnp.sum(jnp.where(oh2, base, 0.0), axis=1, keepdims=True)
        dest_ref[pl.ds(pl.multiple_of(t * TT, TT), TT), :] = jnp.where(
            lane == 0, d1, jnp.where(lane == 1, d2, 0.0)).astype(I32)
        return carry + jnp.sum(a, axis=0, keepdims=True)

    lax.fori_loop(0, SEQ // TT, dest_body, pad_start[0:1].astype(F32))

    rows = lax.broadcasted_iota(I32, (N_BLOCKS_PAD, LANES), 0) * FFN_B
    lanes = lax.broadcasted_iota(I32, (N_BLOCKS_PAD, LANES), 1)
    ended = (jnp.broadcast_to(pad_end[0:1], (N_BLOCKS_PAD, LANES)) <= rows) & (lanes < MOE_EXPERTS)
    be = jnp.sum(ended.astype(I32), axis=1, keepdims=True)
    blk_ref[...] = jnp.broadcast_to(jnp.minimum(be, MOE_EXPERTS - 1), (N_BLOCKS_PAD, LANES))
    total = jnp.max(pad_end, axis=1, keepdims=True)
    row8 = lax.broadcasted_iota(I32, (8, LANES), 0)
    used_ref[...] = jnp.where(row8 == 0, jnp.broadcast_to(total >> shift, (8, LANES)),
                              jnp.where(row8 == 1, counts8, pad_start))


def _plan(ri):
    return pl.pallas_call(
        _plan_kernel,
        out_shape=(
            jax.ShapeDtypeStruct((SEQ, LANES), I32),
            jax.ShapeDtypeStruct((N_BLOCKS_PAD, LANES), I32),
            jax.ShapeDtypeStruct((8, LANES), I32),
        ),
        compiler_params=pltpu.CompilerParams(vmem_limit_bytes=VMEM_LIMIT),
        name="route_plan",
    )(ri)


PAD_BITS = FFN_B.bit_length() - 1


def _pad_fill_copies(e, cnt_ref, pst_ref, zero_sc, xs_hbm, zsem):
    cnt = cnt_ref[e]
    pad = (-cnt) & (FFN_B - 1)
    row = pst_ref[e] + cnt
    out = []
    for bit in reversed(range(PAD_BITS)):
        n = 1 << bit
        start = row + ((pad >> (bit + 1)) << (bit + 1))
        copy = pltpu.make_async_copy(
            zero_sc.at[pl.ds(0, n * ROW_TILE)],
            xs_hbm.at[pl.ds(pl.multiple_of(start * ROW_TILE, ROW_TILE), n * ROW_TILE)], zsem)
        out.append(((pad & n) != 0, copy))
    return out


def _unused_block_copies(b, zero_sc, xs_hbm, zsem):
    half = FFN_B // 2 * ROW_TILE
    return [pltpu.make_async_copy(
        zero_sc, xs_hbm.at[pl.ds(pl.multiple_of((2 * b + k) * half, half), half)], zsem)
        for k in range(2)]


def _dispatch_kernel(dest_ref, cnt_ref, pst_ref, used_ref, xn_ref, xs_hbm, zero_sc, sem, zsem):
    tm = DISP_TM
    i = pl.program_id(0)

    @pl.when(i == 0)
    def _():
        zero_sc[...] = jnp.zeros_like(zero_sc)

        def fill(e, carry):
            for cond, copy in _pad_fill_copies(e, cnt_ref, pst_ref, zero_sc, xs_hbm, zsem):
                pl.when(cond)(copy.start)
            return carry

        lax.fori_loop(0, MOE_EXPERTS, fill, 0)

        def fill_block(b, carry):
            for copy in _unused_block_copies(b, zero_sc, xs_hbm, zsem):
                copy.start()
            return carry

        lax.fori_loop(used_ref[0], N_BLOCKS, fill_block, 0)

    def issue(it, carry):
        for u in range(DMA_UNROLL):
            r = it * DMA_UNROLL + u
            a = (i * tm + r) * 2
            src = xn_ref.at[pl.ds(pl.multiple_of(r * ROW_TILE, ROW_TILE), ROW_TILE)]
            for kk in range(2):
                d = pl.multiple_of(dest_ref[a + kk] * ROW_TILE, ROW_TILE)
                pltpu.make_async_copy(src, xs_hbm.at[pl.ds(d, ROW_TILE)], sem).start()
        return carry

    lax.fori_loop(0, tm // DMA_UNROLL, issue, 0)
    for _ in range(2):
        pltpu.make_async_copy(xn_ref, xs_hbm.at[pl.ds(0, tm * ROW_TILE)], sem).wait()

    @pl.when(i == 0)
    def _():
        def drain(e, carry):
            for cond, copy in _pad_fill_copies(e, cnt_ref, pst_ref, zero_sc, xs_hbm, zsem):
                pl.when(cond)(copy.wait)
            return carry

        lax.fori_loop(0, MOE_EXPERTS, drain, 0)

        def drain_block(b, carry):
            for copy in _unused_block_copies(b, zero_sc, xs_hbm, zsem):
                copy.wait()
            return carry

        lax.fori_loop(used_ref[0], N_BLOCKS, drain_block, 0)


def _dispatch(dest_flat, counts, pad_start, used, xn3):
    tm = DISP_TM
    return pl.pallas_call(
        _dispatch_kernel,
        out_shape=jax.ShapeDtypeStruct((N_BUF * ROW_TILE, LANES), F32),
        grid_spec=pltpu.PrefetchScalarGridSpec(
            num_scalar_prefetch=4,
            grid=(SEQ // tm,),
            in_specs=[pl.BlockSpec((tm * ROW_TILE, LANES), lambda i, d, c, p, u: (i, 0))],
            out_specs=pl.BlockSpec(memory_space=pl.ANY),
            scratch_shapes=[
                pltpu.VMEM((FFN_B // 2 * ROW_TILE, LANES), F32),
                pltpu.SemaphoreType.DMA(()),
                pltpu.SemaphoreType.DMA(()),
            ],
        ),
        compiler_params=_params(("arbitrary",)),
        name="moe_dispatch",
    )(dest_flat, counts, pad_start, used, xn3)


def _ffn_kernel(blk_ref, used_ref, xs_ref, wg_ref, wu_ref, wd_ref, y_ref, wg_bf, wu_bf, wd_bf):
    B = FFN_B
    b = pl.program_id(0)

    @pl.when(b < used_ref[0])
    def _():
        prev = blk_ref[jnp.maximum(b - 1, 0)]

        @pl.when((b == 0) | (blk_ref[b] != prev))
        def _():
            wg_bf[...] = wg_ref[0].astype(BF16)
            wu_bf[...] = wu_ref[0].astype(BF16)
            wd_bf[...] = wd_ref[0].astype(BF16)

        x = jnp.concatenate([xs_ref[pl.ds(c, B, stride=ROW_TILE), :] for c in range(ROW_TILE)],
                            axis=1).astype(BF16)
        hg = jnp.dot(x, wg_bf[...], preferred_element_type=F32)
        hu = jnp.dot(x, wu_bf[...], preferred_element_type=F32)
        hh = ((hg / (1.0 + jnp.exp(-hg))) * hu).astype(BF16)
        y = jnp.dot(hh, wd_bf[...], preferred_element_type=F32)
        for c in range(ROW_TILE):
            y_ref[pl.ds(c, B, stride=ROW_TILE), :] = y[:, c * LANES:(c + 1) * LANES]

    @pl.when(b >= used_ref[0])
    def _():
        y_ref[...] = jnp.zeros_like(y_ref)


def _ffn(blk_e, used, xs, w_gate, w_up, w_down):
    B = FFN_B
    wspec = lambda a, c: pl.BlockSpec((1, a, c), lambda b, blk, used: (blk[b], 0, 0))
    return pl.pallas_call(
        _ffn_kernel,
        out_shape=jax.ShapeDtypeStruct((N_BUF * ROW_TILE, LANES), F32),
        grid_spec=pltpu.PrefetchScalarGridSpec(
            num_scalar_prefetch=2,
            grid=(N_BLOCKS,),
            in_specs=[
                pl.BlockSpec((B * ROW_TILE, LANES),
                             lambda b, blk, used: (jnp.minimum(b, used[0] - 1), 0)),
                wspec(D_MODEL, MOE_HIDDEN),
                wspec(D_MODEL, MOE_HIDDEN),
                wspec(MOE_HIDDEN, D_MODEL),
            ],
            out_specs=pl.BlockSpec((B * ROW_TILE, LANES), lambda b, blk, used: (b, 0)),
            scratch_shapes=[
                pltpu.VMEM((D_MODEL, MOE_HIDDEN), BF16),
                pltpu.VMEM((D_MODEL, MOE_HIDDEN), BF16),
                pltpu.VMEM((MOE_HIDDEN, D_MODEL), BF16),
            ],
        ),
        compiler_params=_params(("arbitrary",)),
        name="expert_ffn",
    )(blk_e, used, xs, w_gate, w_up, w_down)


def _combine_kernel(dest_ref, h1_ref, rw_ref, g_ref, y_hbm, o_ref, ybuf, sem):
    tm = COMB_TM
    i = pl.program_id(0)

    def issue(it, carry):
        for u in range(DMA_UNROLL):
            r = it * DMA_UNROLL + u
            a = (i * tm + r) * 2
            for kk in range(2):
                d = pl.multiple_of(dest_ref[a + kk] * ROW_TILE, ROW_TILE)
                pltpu.make_async_copy(
                    y_hbm.at[pl.ds(d, ROW_TILE)],
                    ybuf.at[kk, pl.ds(pl.multiple_of(r * ROW_TILE, ROW_TILE), ROW_TILE)],
                    sem.at[kk]).start()
        return carry

    lax.fori_loop(0, tm // DMA_UNROLL, issue, 0)
    for kk in range(2):
        pltpu.make_async_copy(y_hbm.at[pl.ds(0, tm * ROW_TILE)], ybuf.at[kk], sem.at[kk]).wait()
    w = rw_ref[...]
    ys = [jnp.concatenate([ybuf[kk, pl.ds(c, tm, stride=ROW_TILE), :] for c in range(ROW_TILE)],
                          axis=1) for kk in range(2)]
    h = h1_ref[...] + w[:, 0:1] * ys[0] + w[:, 1:2] * ys[1]
    var = jnp.mean(h * h, axis=-1, keepdims=True)
    o_ref[...] = h * lax.rsqrt(var + EPS) * g_ref[...]


def _combine(dest_flat, h1, rw, g, y):
    tm = COMB_TM
    return pl.pallas_call(
        _combine_kernel,
        out_shape=jax.ShapeDtypeStruct((SEQ, D_MODEL), F32),
        grid_spec=pltpu.PrefetchScalarGridSpec(
            num_scalar_prefetch=1,
            grid=(SEQ // tm,),
            in_specs=[
                pl.BlockSpec((tm, D_MODEL), lambda i, d: (i, 0)),
                pl.BlockSpec((tm, LANES), lambda i, d: (i, 0)),
                pl.BlockSpec((1, D_MODEL), lambda i, d: (0, 0)),
                pl.BlockSpec(memory_space=pl.ANY),
            ],
            out_specs=pl.BlockSpec((tm, D_MODEL), lambda i, d: (i, 0)),
            scratch_shapes=[
                pltpu.VMEM((2, tm * ROW_TILE, LANES), F32),
                pltpu.SemaphoreType.DMA((2,)),
            ],
        ),
        compiler_params=_params(("arbitrary",)),
        name="moe_combine",
    )(dest_flat, h1, rw, g, y)


def _attention_tables():
    T = ATT_T
    slopes = jnp.exp2(-ALIBI_MAX * jnp.arange(1, DA_HEADS + 1, dtype=F32) / DA_HEADS)
    r = jnp.arange(T)
    hi = ((r // CHUNK) * CHUNK).astype(F32)
    lo = (r % CHUNK).astype(F32)
    ones = jnp.ones((T,), F32)
    sl = slopes[:, None]
    one_h = jnp.broadcast_to(ones, (DA_HEADS, T))
    q_rows = jnp.stack([one_h, one_h, -sl * hi[None], -sl * lo[None]], axis=1)
    k_cols = jnp.stack([sl * hi[None], sl * lo[None], one_h, one_h], axis=-1)
    qa = jnp.zeros((DA_HEADS, DA_HEAD_DIM, T), F32).at[:, 0:4, :].set(q_rows)
    ka1 = jnp.zeros((DA_HEADS, T, LANES), F32).at[:, :, DA_HEAD_DIM:DA_HEAD_DIM + 4].set(k_cols)
    ka2 = jnp.zeros((DA_HEADS, T, LANES), F32).at[:, :, 0:4].set(k_cols)
    rel = (r[:, None] - r[None, :]).astype(F32)
    allowed = (r[:, None] // CHUNK) <= (r[None, :] // CHUNK)
    fix = jnp.where(rel > 0, -2.0 * slopes[:, None, None] * rel[None], 0.0)
    dtab = jnp.where(allowed[None], fix, -jnp.inf)
    return slopes, qa.astype(BF16), ka1.astype(BF16), ka2.astype(BF16), dtab


def _retention_tables():
    C = RET_C
    log_gamma = jnp.log1p(-jnp.exp2(-5.0 - jnp.arange(RET_HEADS, dtype=F32)))
    pos = jnp.arange(C, dtype=F32)
    rel = pos[:, None] - pos[None, :]
    dec = jnp.where(rel >= 0, jnp.exp(log_gamma[:, None, None] * jnp.maximum(rel, 0.0)), 0.0)
    qdec = jnp.exp(log_gamma[:, None] * (pos + 1.0)[None, :])[:, :, None]
    kdec = jnp.exp(log_gamma[:, None] * (C - 1 - pos)[None, :])[:, :, None]
    cd = jnp.exp(log_gamma * C)
    return cd, dec, qdec, kdec


def kernel(x, attn_norm_g, w_in, da_lambda_q1, da_lambda_k1, da_lambda_q2, da_lambda_k2,
           da_subln_g, w_out, ffn_norm_g, router_group_w, router_group_b, router_expert_w,
           router_expert_b, expert_w_gate, expert_w_up, expert_w_down, final_norm_g):
    B, S, D = x.shape
    assert (B, S, D) == (1, SEQ, D_MODEL)
    x2 = x.reshape(S, D)

    w = w_in[0]
    w_main = jnp.concatenate([
        w[:, 512:1024],
        w[:, 1536:1792],
        w[:, 1792:2048] * (RET_QK_DIM ** -0.5),
        w[:, 2048:3072]], axis=1).astype(BF16)
    wq_t = (w[:, 0:512] * (DA_HEAD_DIM ** -0.5)).T.astype(BF16)
    wv_t = w[:, 1024:1536].T.astype(BF16)
    proj, qt4, vt4 = _inproj(x2, attn_norm_g[0][None, :], w_main, wq_t, wv_t)

    slopes, qa, ka1, ka2, dtab = _attention_tables()
    o_da = _attention(proj, qt4, vt4, slopes, qa, ka1, ka2, dtab, da_lambda_q1, da_lambda_k1,
                      da_lambda_q2, da_lambda_k2, da_subln_g)
    cd, dec, qdec, kdec = _retention_tables()
    o_r = _retention(proj, cd, dec, qdec, kdec)

    wr = jnp.zeros((D, LANES), F32)
    wr = wr.at[:, :MOE_GROUPS].set(router_group_w[0])
    wr = wr.at[:, MOE_GROUPS:MOE_GROUPS + MOE_EXPERTS].set(router_expert_w[0])
    br = jnp.zeros((1, LANES), F32)
    br = br.at[0, :MOE_GROUPS].set(router_group_b[0])
    br = br.at[0, MOE_GROUPS:MOE_GROUPS + MOE_EXPERTS].set(router_expert_b[0])
    h1, xn, ri, rw = _outproj_router(x2, o_da, o_r, w_out[0].astype(BF16),
                                     ffn_norm_g[0][None, :], wr, br)

    dest, blk, used = _plan(ri)
    dest_flat = dest[:, :2].reshape(N_ASSIGN)
    used1 = used[0, :1]
    blk_e = blk[:N_BLOCKS, 0]
    blk_e = blk_e[jnp.minimum(jnp.arange(N_BLOCKS, dtype=I32), used1[0] - 1)]
    xs = _dispatch(dest_flat, used[1, :MOE_EXPERTS], used[2, :MOE_EXPERTS], used1, xn)

    y = _ffn(blk_e, used1, xs, expert_w_gate[0], expert_w_up[0], expert_w_down[0])
    out = _combine(dest_flat, h1, rw, final_norm_g[None, :], y)
    return out.reshape(B, S, D)
```

```python
import functools
import math

import jax
import jax.numpy as jnp
from jax import lax
from jax.experimental import pallas as pl
from jax.experimental.pallas import tpu as pltpu

F32 = jnp.float32
BF16 = jnp.bfloat16
I32 = jnp.int32

D_MODEL = 1024
SEQ = 16384
CHUNK = 64
EPS = 1e-6

DA_HEADS = 4
DA_HEAD_DIM = 64
DA_V_DIM = 128
DA_WIDTH = 512
ALIBI_MAX = 8.0
RET_HEADS = 4
RET_QK_DIM = 64
RET_V_DIM = 128
RET_WIDTH = 512
T_ROWS = 512
MAIN_COLS = 2048
DK_OFF = 0
RQ_OFF = 512
RK_OFF = 768
RV_OFF = 1024
RG_OFF = 1536

MOE_GROUPS = 4
MOE_EXPERTS_PER_GROUP = 8
MOE_EXPERTS = 32
MOE_HIDDEN = 512
LAMBDA_INIT = 0.8 - 0.6 * math.exp(-0.3 * 0)

LANES = 128
ROW_TILE = 8
VMEM_LIMIT = 56 * 1024 * 1024

PROJ_TM = 512
ATT_T = 512
RET_C = 256
PLAN_T = 512
FFN_B = 256
N_ASSIGN = 2 * SEQ
N_BLOCKS = N_ASSIGN // FFN_B + MOE_EXPERTS
N_BLOCKS_PAD = (N_BLOCKS + 7) // 8 * 8
N_BUF = N_BLOCKS * FFN_B
COMB_TM = 256
DISP_TM = 256
DMA_UNROLL = 8


def _params(sem):
    return pltpu.CompilerParams(dimension_semantics=sem, vmem_limit_bytes=VMEM_LIMIT)


def _inproj_kernel(x_ref, g_ref, w_ref, wq_ref, wv_ref, o_ref, qt_ref, vt_ref):
    x = x_ref[...]
    var = jnp.mean(x * x, axis=-1, keepdims=True)
    xn = (x * lax.rsqrt(var + EPS) * g_ref[...]).astype(BF16)
    for c in range(MAIN_COLS // 512):
        sl = slice(c * 512, (c + 1) * 512)
        o_ref[:, sl] = jnp.dot(xn, w_ref[:, sl], preferred_element_type=F32).astype(BF16)
    nt = (((1,), (1,)), ((), ()))
    qt = lax.dot_general(wq_ref[...], xn, nt, preferred_element_type=F32)
    qt_ref[...] = qt.astype(BF16).reshape(DA_HEADS, 1, 2 * DA_HEAD_DIM, PROJ_TM)
    vt = lax.dot_general(wv_ref[...], xn, nt, preferred_element_type=F32)
    vt_ref[...] = vt.astype(BF16).reshape(DA_HEADS, 1, DA_V_DIM, PROJ_TM)


def _inproj(x2, g, w_bf, wq_t, wv_t):
    t_shape = jax.ShapeDtypeStruct((DA_HEADS, SEQ // PROJ_TM, LANES, PROJ_TM), BF16)
    t_spec = pl.BlockSpec((DA_HEADS, 1, LANES, PROJ_TM), lambda i: (0, i, 0, 0))
    w_t_spec = pl.BlockSpec((T_ROWS, D_MODEL), lambda i: (0, 0))
    return pl.pallas_call(
        _inproj_kernel,
        out_shape=(jax.ShapeDtypeStruct((SEQ, MAIN_COLS), BF16), t_shape, t_shape),
        grid=(SEQ // PROJ_TM,),
        in_specs=[
            pl.BlockSpec((PROJ_TM, D_MODEL), lambda i: (i, 0)),
            pl.BlockSpec((1, D_MODEL), lambda i: (0, 0)),
            pl.BlockSpec((D_MODEL, MAIN_COLS), lambda i: (0, 0)),
            w_t_spec, w_t_spec,
        ],
        out_specs=(pl.BlockSpec((PROJ_TM, MAIN_COLS), lambda i: (i, 0)), t_spec, t_spec),
        compiler_params=_params(("arbitrary",)),
        name="inproj",
    )(x2, g, w_bf, wq_t, wv_t)


ACC_ROWS = DA_V_DIM + 16


def _attn_kernel(slope_ref, qt_ref, k_ref, vt_ref, qa_ref, ka1_ref, ka2_ref, dtab_ref,
                 lq1_ref, lk1_ref, lq2_ref, lk2_ref, g_ref, o_ref,
                 m_sc, acc_sc, s0_sc, s1_sc, mx0_sc, mx1_sc):
    T = ATT_T
    h = pl.program_id(0)
    i = pl.program_id(1)
    slope = slope_ref[h]
    qt = qt_ref[0, 0]
    qa = qa_ref[0]
    qw = (jnp.concatenate([qt[0:DA_HEAD_DIM], qa], axis=0),
          jnp.concatenate([qa, qt[DA_HEAD_DIM:]], axis=0))
    lane = lax.broadcasted_iota(I32, (T, LANES), 1)
    sums_row = (lax.broadcasted_iota(I32, (16, T), 0) == 0).astype(BF16)
    s_bufs = (s0_sc, s1_sc)
    mx_bufs = (mx0_sc, mx1_sc)

    def scores(j, buf, extra):
        kt = k_ref[pl.ds(pl.multiple_of(j * T, T), T), :]
        ks = (jnp.where(lane < DA_HEAD_DIM, kt, ka1_ref[0]),
              jnp.where(lane >= DA_HEAD_DIM, kt, ka2_ref[0]))
        for mp in range(2):
            s = jnp.dot(ks[mp], qw[mp], preferred_element_type=F32)
            if extra is not None:
                s = s + extra
            s_bufs[buf][mp] = s
            mx_bufs[buf][mp] = jnp.max(s, axis=0, keepdims=True)

    def accumulate(j, buf):
        c = slope * ((i - j) * T).astype(F32)
        vte = jnp.concatenate([vt_ref[0, j], sums_row], axis=0)
        for mp in range(2):
            m_prev = m_sc[mp]
            m_new = jnp.maximum(m_prev, mx_bufs[buf][mp] - c)
            p = jnp.exp(s_bufs[buf][mp] - (m_new + c)).astype(BF16)
            pv = jnp.dot(vte, p, preferred_element_type=F32)
            acc_sc[mp] = jnp.exp(m_prev - m_new) * acc_sc[mp] + pv
            m_sc[mp] = m_new

    m_sc[...] = jnp.full_like(m_sc, -jnp.inf)
    acc_sc[...] = jnp.zeros_like(acc_sc)
    scores(i, 0, dtab_ref[0])

    def pair(k, carry):
        prev = jnp.where(k == 0, i, 2 * k - 1)
        accumulate(prev, 0)
        scores(2 * k, 1, None)
        accumulate(2 * k, 1)
        scores(2 * k + 1, 0, None)
        return carry

    npairs = i // 2
    lax.fori_loop(0, npairs, pair, 0)
    last = jnp.where(npairs == 0, i, 2 * npairs - 1)

    @pl.when(i % 2 == 1)
    def _():
        accumulate(last, 0)
        scores(i - 1, 1, None)
        accumulate(i - 1, 1)

    @pl.when(i % 2 == 0)
    def _():
        accumulate(last, 0)

    lam = (jnp.exp(jnp.sum(lq1_ref[...] * lk1_ref[...], axis=1, keepdims=True))
           - jnp.exp(jnp.sum(lq2_ref[...] * lk2_ref[...], axis=1, keepdims=True))
           + LAMBDA_INIT)
    a1 = acc_sc[0]
    a2 = acc_sc[1]
    ot = (a1[0:DA_V_DIM] / a1[DA_V_DIM:DA_V_DIM + 1]
          - lam * (a2[0:DA_V_DIM] / a2[DA_V_DIM:DA_V_DIM + 1]))
    o = ot.T
    var = jnp.mean(o * o, axis=-1, keepdims=True)
    o = (o * lax.rsqrt(var + EPS) * g_ref[...]) * (1.0 - LAMBDA_INIT)
    o_ref[...] = o.astype(BF16)


def _attention(proj, qt4, vt4, slopes, qa, ka1, ka2, dtab, lq1, lk1, lq2, lk2, subln_g):
    T = ATT_T
    vec64 = pl.BlockSpec((1, DA_HEAD_DIM), lambda h, i: (0, 0))
    per_head = lambda a, b: pl.BlockSpec((1, a, b), lambda h, i: (h, 0, 0))
    return pl.pallas_call(
        _attn_kernel,
        out_shape=jax.ShapeDtypeStruct((SEQ, DA_WIDTH), BF16),
        grid=(DA_HEADS, SEQ // T),
        in_specs=[
            pl.BlockSpec(memory_space=pltpu.SMEM),
            pl.BlockSpec((1, 1, LANES, T), lambda h, i: (h, i, 0, 0)),
            pl.BlockSpec((SEQ, LANES), lambda h, i: (0, DK_OFF // LANES + h)),
            pl.BlockSpec((1, SEQ // T, LANES, T), lambda h, i: (h, 0, 0, 0)),
            per_head(DA_HEAD_DIM, T), per_head(T, LANES), per_head(T, LANES), per_head(T, T),
            vec64, vec64, vec64, vec64,
            pl.BlockSpec((1, DA_V_DIM), lambda h, i: (0, 0)),
        ],
        out_specs=pl.BlockSpec((T, LANES), lambda h, i: (i, h)),
        scratch_shapes=[
            pltpu.VMEM((2, 1, T), F32),
            pltpu.VMEM((2, ACC_ROWS, T), F32),
            pltpu.VMEM((2, T, T), F32),
            pltpu.VMEM((2, T, T), F32),
            pltpu.VMEM((2, 1, T), F32),
            pltpu.VMEM((2, 1, T), F32),
        ],
        compiler_params=_params(("arbitrary", "arbitrary")),
        name="diff_attention",
    )(slopes, qt4, proj, vt4, qa, ka1, ka2, dtab, lq1, lk1, lq2, lk2, subln_g)


def _ret_kernel(cd_ref, q_ref, k_ref, v_ref, g_ref, dec_ref, qdec_ref, kdec_ref, o_ref, st_sc):
    @pl.when(pl.program_id(0) == 0)
    def _():
        st_sc[...] = jnp.zeros_like(st_sc)

    for h in range(RET_HEADS):
        qk = slice(h * RET_QK_DIM, (h + 1) * RET_QK_DIM)
        vv = slice(h * RET_V_DIM, (h + 1) * RET_V_DIM)
        q = q_ref[:, qk]
        k = k_ref[:, qk]
        v = v_ref[:, vv]
        g = g_ref[:, vv].astype(F32)
        s = lax.dot_general(q, k, (((1,), (1,)), ((), ())),
                            preferred_element_type=F32) * dec_ref[h]
        intra = jnp.dot(s.astype(BF16), v, preferred_element_type=F32)
        st = st_sc[h]
        cross = jnp.dot(q, st.astype(BF16), preferred_element_type=F32) * qdec_ref[h]
        kd = (k.astype(F32) * kdec_ref[h]).astype(BF16)
        st_sc[h] = st * cd_ref[h] + lax.dot_general(kd, v, (((0,), (0,)), ((), ())),
                                                    preferred_element_type=F32)
        o = intra + cross
        o = o * lax.rsqrt(jnp.mean(o * o, axis=-1, keepdims=True) + EPS)
        o = (g / (1.0 + jnp.exp(-g))) * o
        o_ref[:, vv] = o.astype(BF16)


def _retention(proj, cd, dec, qdec, kdec):
    C = RET_C
    return pl.pallas_call(
        _ret_kernel,
        out_shape=jax.ShapeDtypeStruct((SEQ, RET_WIDTH), BF16),
        grid=(SEQ // C,),
        in_specs=[
            pl.BlockSpec(memory_space=pltpu.SMEM),
            pl.BlockSpec((C, 256), lambda i: (i, RQ_OFF // 256)),
            pl.BlockSpec((C, 256), lambda i: (i, RK_OFF // 256)),
            pl.BlockSpec((C, 512), lambda i: (i, RV_OFF // 512)),
            pl.BlockSpec((C, 512), lambda i: (i, RG_OFF // 512)),
            pl.BlockSpec((RET_HEADS, C, C), lambda i: (0, 0, 0)),
            pl.BlockSpec((RET_HEADS, C, 1), lambda i: (0, 0, 0)),
            pl.BlockSpec((RET_HEADS, C, 1), lambda i: (0, 0, 0)),
        ],
        out_specs=pl.BlockSpec((C, RET_WIDTH), lambda i: (i, 0)),
        scratch_shapes=[pltpu.VMEM((RET_HEADS, RET_QK_DIM, RET_V_DIM), F32)],
        compiler_params=_params(("arbitrary",)),
        name="retention",
    )(cd, proj, proj, proj, proj, dec, qdec, kdec)


def _outproj_router_kernel(x_ref, oda_ref, or_ref, wo_ref, g_ref, wr_ref, br_ref,
                           h1_ref, xn_ref, ri_ref, rw_ref):
    h1 = (x_ref[...]
          + jnp.dot(oda_ref[...], wo_ref[0:DA_WIDTH, :], preferred_element_type=F32)
          + jnp.dot(or_ref[...], wo_ref[DA_WIDTH:, :], preferred_element_type=F32))
    h1_ref[...] = h1
    var = jnp.mean(h1 * h1, axis=-1, keepdims=True)
    xn = h1 * lax.rsqrt(var + EPS) * g_ref[...]
    for c in range(ROW_TILE):
        xn_ref[pl.ds(c, PROJ_TM, stride=ROW_TILE), :] = xn[:, c * LANES:(c + 1) * LANES]
    x_hi = xn.astype(BF16)
    x_lo = (xn - x_hi.astype(F32)).astype(BF16)
    both = jnp.dot(x_hi, wr_ref[...], preferred_element_type=F32)
    logits = (both[:, :LANES] + both[:, LANES:]
              + jnp.dot(x_lo, wr_ref[:, :LANES], preferred_element_type=F32)) + br_ref[...]
    lane = lax.broadcasted_iota(I32, logits.shape, 1)
    neg = jnp.float32(-jnp.inf)
    big = jnp.int32(1 << 20)
    gl = jnp.where(lane < MOE_GROUPS, logits, neg)
    gmax = jnp.max(gl, axis=1, keepdims=True)
    gidx = jnp.min(jnp.where(gl == gmax, lane, big), axis=1, keepdims=True)
    gsum = jnp.sum(jnp.exp(gl - gmax), axis=1, keepdims=True)
    gp = 1.0 / gsum
    lo = MOE_GROUPS + gidx * MOE_EXPERTS_PER_GROUP
    el = jnp.where((lane >= lo) & (lane < lo + MOE_EXPERTS_PER_GROUP), logits, neg)
    v1 = jnp.max(el, axis=1, keepdims=True)
    i1 = jnp.min(jnp.where(el == v1, lane, big), axis=1, keepdims=True)
    el2 = jnp.where(lane == i1, neg, el)
    v2 = jnp.max(el2, axis=1, keepdims=True)
    i2 = jnp.min(jnp.where(el2 == v2, lane, big), axis=1, keepdims=True)
    t = jnp.exp(v2 - v1)
    w1 = gp / (1.0 + t)
    w2 = gp * t / (1.0 + t)
    ri_ref[...] = jnp.where(lane == 0, i1 - MOE_GROUPS,
                            jnp.where(lane == 1, i2 - MOE_GROUPS, 0))
    rw_ref[...] = jnp.where(lane == 0, w1, jnp.where(lane == 1, w2, 0.0))


def _outproj_router(x2, o_da, o_r, wo_bf, g, wr, br):
    tm = PROJ_TM
    row = lambda w: pl.BlockSpec((tm, w), lambda i: (i, 0))
    full = lambda a, b: pl.BlockSpec((a, b), lambda i: (0, 0))
    return pl.pallas_call(
        _outproj_router_kernel,
        out_shape=(
            jax.ShapeDtypeStruct((SEQ, D_MODEL), F32),
            jax.ShapeDtypeStruct((SEQ * ROW_TILE, LANES), F32),
            jax.ShapeDtypeStruct((SEQ, LANES), I32),
            jax.ShapeDtypeStruct((SEQ, LANES), F32),
        ),
        grid=(SEQ // tm,),
        in_specs=[row(D_MODEL), row(DA_WIDTH), row(RET_WIDTH), full(D_MODEL, D_MODEL),
                  full(1, D_MODEL), full(D_MODEL, 2 * LANES), full(1, LANES)],
        out_specs=(row(D_MODEL), pl.BlockSpec((tm * ROW_TILE, LANES), lambda i: (i, 0)),
                   row(LANES), row(LANES)),
        compiler_params=_params(("arbitrary",)),
        name="outproj_router",
    )(x2, o_da, o_r, wo_bf, g, wr, br)


def _plan_kernel(ri_ref, dest_ref, blk_ref, used_ref):
    TT = PLAN_T
    lane = lax.broadcasted_iota(I32, (TT, LANES), 1)

    def onehots(t):
        r = ri_ref[pl.ds(pl.multiple_of(t * TT, TT), TT), :]
        return lane == r[:, 0:1], lane == r[:, 1:2]

    def count_body(t, acc):
        oh1, oh2 = onehots(t)
        return acc + jnp.sum((oh1 | oh2).astype(F32), axis=0, keepdims=True)

    counts = lax.fori_loop(0, SEQ // TT, count_body, jnp.zeros((1, LANES), F32))
    counts8 = jnp.broadcast_to(counts, (8, LANES)).astype(I32)
    shift = FFN_B.bit_length() - 1
    padded = ((counts8 + (FFN_B - 1)) >> shift) << shift
    lane8 = lax.broadcasted_iota(I32, (8, LANES), 1)
    pad_end = padded
    sh = 1
    while sh < LANES:
        pad_end = pad_end + jnp.where(lane8 >= sh, pltpu.roll(pad_end, sh, axis=1), 0)
        sh *= 2
    pad_start = pad_end - padded

    ltri = (lax.broadcasted_iota(I32, (TT, TT), 0)
            > lax.broadcasted_iota(I32, (TT, TT), 1)).astype(BF16)

    def dest_body(t, carry):
        oh1, oh2 = onehots(t)
        a = (oh1 | oh2).astype(F32)
        base = jnp.dot(ltri, a.astype(BF16), preferred_element_type=F32) + carry
        d1 = jnp.sum(jnp.where(oh1, base, 0.0), axis=1, keepdims=True)
        d2 = jnp.sum(jnp.where(oh2, base, 0.0), axis=1, keepdims=True)
        dest_ref[pl.ds(pl.multiple_of(t * TT, TT), TT), :] = jnp.where(
            lane == 0, d1, jnp.where(lane == 1, d2, 0.0)).astype(I32)
        return carry + jnp.sum(a, axis=0, keepdims=True)

    lax.fori_loop(0, SEQ // TT, dest_body, pad_start[0:1].astype(F32))

    rows = lax.broadcasted_iota(I32, (N_BLOCKS_PAD, LANES), 0) * FFN_B
    lanes = lax.broadcasted_iota(I32, (N_BLOCKS_PAD, LANES), 1)
    ended = (jnp.broadcast_to(pad_end[0:1], (N_BLOCKS_PAD, LANES)) <= rows) & (lanes < MOE_EXPERTS)
    be = jnp.sum(ended.astype(I32), axis=1, keepdims=True)
    blk_ref[...] = jnp.broadcast_to(jnp.minimum(be, MOE_EXPERTS - 1), (N_BLOCKS_PAD, LANES))
    total = jnp.max(pad_end, axis=1, keepdims=True)
    row8 = lax.broadcasted_iota(I32, (8, LANES), 0)
    used_ref[...] = jnp.where(row8 == 0, jnp.broadcast_to(total >> shift, (8, LANES)),
                              jnp.where(row8 == 1, counts8, pad_start))


def _plan(ri):
    return pl.pallas_call(
        _plan_kernel,
        out_shape=(
            jax.ShapeDtypeStruct((SEQ, LANES), I32),
            jax.ShapeDtypeStruct((N_BLOCKS_PAD, LANES), I32),
            jax.ShapeDtypeStruct((8, LANES), I32),
        ),
        compiler_params=pltpu.CompilerParams(vmem_limit_bytes=VMEM_LIMIT),
        name="route_plan",
    )(ri)


PAD_BITS = FFN_B.bit_length() - 1


def _pad_fill_copies(e, cnt_ref, pst_ref, zero_sc, xs_hbm, zsem):
    cnt = cnt_ref[e]
    pad = (-cnt) & (FFN_B - 1)
    row = pst_ref[e] + cnt
    out = []
    for bit in reversed(range(PAD_BITS)):
        n = 1 << bit
        start = row + ((pad >> (bit + 1)) << (bit + 1))
        copy = pltpu.make_async_copy(
            zero_sc.at[pl.ds(0, n * ROW_TILE)],
            xs_hbm.at[pl.ds(pl.multiple_of(start * ROW_TILE, ROW_TILE), n * ROW_TILE)], zsem)
        out.append(((pad & n) != 0, copy))
    return out


def _unused_block_copies(b, zero_sc, xs_hbm, zsem):
    half = FFN_B // 2 * ROW_TILE
    return [pltpu.make_async_copy(
        zero_sc, xs_hbm.at[pl.ds(pl.multiple_of((2 * b + k) * half, half), half)], zsem)
        for k in range(2)]


def _dispatch_kernel(dest_ref, cnt_ref, pst_ref, used_ref, xn_hbm, xs_hbm, zero_sc, sem, zsem):
    tm = DISP_TM
    i = pl.program_id(0)

    @pl.when(i == 0)
    def _():
        zero_sc[...] = jnp.zeros_like(zero_sc)

        def fill(e, carry):
            for cond, copy in _pad_fill_copies(e, cnt_ref, pst_ref, zero_sc, xs_hbm, zsem):
                pl.when(cond)(copy.start)
            return carry

        lax.fori_loop(0, MOE_EXPERTS, fill, 0)

        def fill_block(b, carry):
            for copy in _unused_block_copies(b, zero_sc, xs_hbm, zsem):
                copy.start()
            return carry

        lax.fori_loop(used_ref[0], N_BLOCKS, fill_block, 0)

    def issue(it, carry):
        for u in range(DMA_UNROLL):
            t = i * tm + it * DMA_UNROLL + u
            src = xn_hbm.at[pl.ds(pl.multiple_of(t * ROW_TILE, ROW_TILE), ROW_TILE)]
            for kk in range(2):
                d = pl.multiple_of(dest_ref[2 * t + kk] * ROW_TILE, ROW_TILE)
                pltpu.make_async_copy(src, xs_hbm.at[pl.ds(d, ROW_TILE)], sem.at[i % 2]).start()
        return carry

    lax.fori_loop(0, tm // DMA_UNROLL, issue, 0)

    def wait_tile(slot):
        whole = pl.ds(0, tm * ROW_TILE)
        for _ in range(2):
            pltpu.make_async_copy(xn_hbm.at[whole], xs_hbm.at[whole], sem.at[slot]).wait()

    @pl.when(i > 0)
    def _():
        wait_tile((i + 1) % 2)

    @pl.when(i == pl.num_programs(0) - 1)
    def _():
        wait_tile(i % 2)

    @pl.when(i == 0)
    def _():
        def drain(e, carry):
            for cond, copy in _pad_fill_copies(e, cnt_ref, pst_ref, zero_sc, xs_hbm, zsem):
                pl.when(cond)(copy.wait)
            return carry

        lax.fori_loop(0, MOE_EXPERTS, drain, 0)

        def drain_block(b, carry):
            for copy in _unused_block_copies(b, zero_sc, xs_hbm, zsem):
                copy.wait()
            return carry

        lax.fori_loop(used_ref[0], N_BLOCKS, drain_block, 0)


def _dispatch(dest_flat, counts, pad_start, used, xn3):
    tm = DISP_TM
    return pl.pallas_call(
        _dispatch_kernel,
        out_shape=jax.ShapeDtypeStruct((N_BUF * ROW_TILE, LANES), F32),
        grid_spec=pltpu.PrefetchScalarGridSpec(
            num_scalar_prefetch=4,
            grid=(SEQ // tm,),
            in_specs=[pl.BlockSpec(memory_space=pl.ANY)],
            out_specs=pl.BlockSpec(memory_space=pl.ANY),
            scratch_shapes=[
                pltpu.VMEM((FFN_B // 2 * ROW_TILE, LANES), F32),
                pltpu.SemaphoreType.DMA((2,)),
                pltpu.SemaphoreType.DMA(()),
            ],
        ),
        compiler_params=_params(("arbitrary",)),
        name="moe_dispatch",
    )(dest_flat, counts, pad_start, used, xn3)


def _ffn_kernel(blk_ref, used_ref, xs_ref, wg_ref, wu_ref, wd_ref, y_ref, wg_bf, wu_bf, wd_bf):
    B = FFN_B
    b = pl.program_id(0)

    @pl.when(b < used_ref[0])
    def _():
        prev = blk_ref[jnp.maximum(b - 1, 0)]

        @pl.when((b == 0) | (blk_ref[b] != prev))
        def _():
            wg_bf[...] = wg_ref[0].astype(BF16)
            wu_bf[...] = wu_ref[0].astype(BF16)
            wd_bf[...] = wd_ref[0].astype(BF16)

        x = jnp.concatenate([xs_ref[pl.ds(c, B, stride=ROW_TILE), :] for c in range(ROW_TILE)],
                            axis=1).astype(BF16)
        hg = jnp.dot(x, wg_bf[...], preferred_element_type=F32)
        hu = jnp.dot(x, wu_bf[...], preferred_element_type=F32)
        hh = ((hg / (1.0 + jnp.exp(-hg))) * hu).astype(BF16)
        y = jnp.dot(hh, wd_bf[...], preferred_element_type=F32)
        for c in range(ROW_TILE):
            y_ref[pl.ds(c, B, stride=ROW_TILE), :] = y[:, c * LANES:(c + 1) * LANES]

    @pl.when(b >= used_ref[0])
    def _():
        y_ref[...] = jnp.zeros_like(y_ref)


def _ffn(blk_e, used, xs, w_gate, w_up, w_down):
    B = FFN_B
    wspec = lambda a, c: pl.BlockSpec((1, a, c), lambda b, blk, used: (blk[b], 0, 0))
    return pl.pallas_call(
        _ffn_kernel,
        out_shape=jax.ShapeDtypeStruct((N_BUF * ROW_TILE, LANES), F32),
        grid_spec=pltpu.PrefetchScalarGridSpec(
            num_scalar_prefetch=2,
            grid=(N_BLOCKS,),
            in_specs=[
                pl.BlockSpec((B * ROW_TILE, LANES),
                             lambda b, blk, used: (jnp.minimum(b, used[0] - 1), 0)),
                wspec(D_MODEL, MOE_HIDDEN),
                wspec(D_MODEL, MOE_HIDDEN),
                wspec(MOE_HIDDEN, D_MODEL),
            ],
            out_specs=pl.BlockSpec((B * ROW_TILE, LANES), lambda b, blk, used: (b, 0)),
            scratch_shapes=[
                pltpu.VMEM((D_MODEL, MOE_HIDDEN), BF16),
                pltpu.VMEM((D_MODEL, MOE_HIDDEN), BF16),
                pltpu.VMEM((MOE_HIDDEN, D_MODEL), BF16),
            ],
        ),
        compiler_params=_params(("arbitrary",)),
        name="expert_ffn",
    )(blk_e, used, xs, w_gate, w_up, w_down)


def _combine_kernel(dest_ref, h1_ref, rw_ref, g_ref, y_hbm, o_ref, ybuf, sem):
    tm = COMB_TM
    i = pl.program_id(0)

    def gather(tile, slot):
        def issue(it, carry):
            for u in range(DMA_UNROLL):
                r = it * DMA_UNROLL + u
                a = (tile * tm + r) * 2
                for kk in range(2):
                    d = pl.multiple_of(dest_ref[a + kk] * ROW_TILE, ROW_TILE)
                    pltpu.make_async_copy(
                        y_hbm.at[pl.ds(d, ROW_TILE)],
                        ybuf.at[slot, kk, pl.ds(pl.multiple_of(r * ROW_TILE, ROW_TILE), ROW_TILE)],
                        sem.at[slot, kk]).start()
            return carry

        lax.fori_loop(0, tm // DMA_UNROLL, issue, 0)

    @pl.when(i == 0)
    def _():
        gather(0, 0)

    @pl.when(i + 1 < pl.num_programs(0))
    def _():
        gather(i + 1, (i + 1) % 2)

    slot = i % 2
    for kk in range(2):
        pltpu.make_async_copy(y_hbm.at[pl.ds(0, tm * ROW_TILE)], ybuf.at[slot, kk],
                              sem.at[slot, kk]).wait()
    w = rw_ref[...]
    ys = [jnp.concatenate([ybuf[slot, kk, pl.ds(c, tm, stride=ROW_TILE), :]
                           for c in range(ROW_TILE)], axis=1) for kk in range(2)]
    h = h1_ref[...] + w[:, 0:1] * ys[0] + w[:, 1:2] * ys[1]
    var = jnp.mean(h * h, axis=-1, keepdims=True)
    o_ref[...] = h * lax.rsqrt(var + EPS) * g_ref[...]


def _combine(dest_flat, h1, rw, g, y):
    tm = COMB_TM
    return pl.pallas_call(
        _combine_kernel,
        out_shape=jax.ShapeDtypeStruct((SEQ, D_MODEL), F32),
        grid_spec=pltpu.PrefetchScalarGridSpec(
            num_scalar_prefetch=1,
            grid=(SEQ // tm,),
            in_specs=[
                pl.BlockSpec((tm, D_MODEL), lambda i, d: (i, 0)),
                pl.BlockSpec((tm, LANES), lambda i, d: (i, 0)),
                pl.BlockSpec((1, D_MODEL), lambda i, d: (0, 0)),
                pl.BlockSpec(memory_space=pl.ANY),
            ],
            out_specs=pl.BlockSpec((tm, D_MODEL), lambda i, d: (i, 0)),
            scratch_shapes=[
                pltpu.VMEM((2, 2, tm * ROW_TILE, LANES), F32),
                pltpu.SemaphoreType.DMA((2, 2)),
            ],
        ),
        compiler_params=_params(("arbitrary",)),
        name="moe_combine",
    )(dest_flat, h1, rw, g, y)


def _attention_tables():
    T = ATT_T
    slopes = jnp.exp2(-ALIBI_MAX * jnp.arange(1, DA_HEADS + 1, dtype=F32) / DA_HEADS)
    r = jnp.arange(T)
    hi = ((r // CHUNK) * CHUNK).astype(F32)
    lo = (r % CHUNK).astype(F32)
    ones = jnp.ones((T,), F32)
    sl = slopes[:, None]
    one_h = jnp.broadcast_to(ones, (DA_HEADS, T))
    q_rows = jnp.stack([one_h, one_h, -sl * hi[None], -sl * lo[None]], axis=1)
    k_cols = jnp.stack([sl * hi[None], sl * lo[None], one_h, one_h], axis=-1)
    qa = jnp.zeros((DA_HEADS, DA_HEAD_DIM, T), F32).at[:, 0:4, :].set(q_rows)
    ka1 = jnp.zeros((DA_HEADS, T, LANES), F32).at[:, :, DA_HEAD_DIM:DA_HEAD_DIM + 4].set(k_cols)
    ka2 = jnp.zeros((DA_HEADS, T, LANES), F32).at[:, :, 0:4].set(k_cols)
    rel = (r[:, None] - r[None, :]).astype(F32)
    allowed = (r[:, None] // CHUNK) <= (r[None, :] // CHUNK)
    fix = jnp.where(rel > 0, -2.0 * slopes[:, None, None] * rel[None], 0.0)
    dtab = jnp.where(allowed[None], fix, -jnp.inf)
    return slopes, qa.astype(BF16), ka1.astype(BF16), ka2.astype(BF16), dtab


def _retention_tables():
    C = RET_C
    log_gamma = jnp.log1p(-jnp.exp2(-5.0 - jnp.arange(RET_HEADS, dtype=F32)))
    pos = jnp.arange(C, dtype=F32)
    rel = pos[:, None] - pos[None, :]
    dec = jnp.where(rel >= 0, jnp.exp(log_gamma[:, None, None] * jnp.maximum(rel, 0.0)), 0.0)
    qdec = jnp.exp(log_gamma[:, None] * (pos + 1.0)[None, :])[:, :, None]
    kdec = jnp.exp(log_gamma[:, None] * (C - 1 - pos)[None, :])[:, :, None]
    cd = jnp.exp(log_gamma * C)
    return cd, dec, qdec, kdec


def kernel(x, attn_norm_g, w_in, da_lambda_q1, da_lambda_k1, da_lambda_q2, da_lambda_k2,
           da_subln_g, w_out, ffn_norm_g, router_group_w, router_group_b, router_expert_w,
           router_expert_b, expert_w_gate, expert_w_up, expert_w_down, final_norm_g):
    B, S, D = x.shape
    assert (B, S, D) == (1, SEQ, D_MODEL)
    x2 = x.reshape(S, D)

    w = w_in[0]
    w_main = jnp.concatenate([
        w[:, 512:1024],
        w[:, 1536:1792],
        w[:, 1792:2048] * (RET_QK_DIM ** -0.5),
        w[:, 2048:3072]], axis=1).astype(BF16)
    wq_t = (w[:, 0:512] * (DA_HEAD_DIM ** -0.5)).T.astype(BF16)
    wv_t = w[:, 1024:1536].T.astype(BF16)
    proj, qt4, vt4 = _inproj(x2, attn_norm_g[0][None, :], w_main, wq_t, wv_t)

    slopes, qa, ka1, ka2, dtab = _attention_tables()
    o_da = _attention(proj, qt4, vt4, slopes, qa, ka1, ka2, dtab, da_lambda_q1, da_lambda_k1,
                      da_lambda_q2, da_lambda_k2, da_subln_g)
    cd, dec, qdec, kdec = _retention_tables()
    o_r = _retention(proj, cd, dec, qdec, kdec)

    wr = jnp.zeros((D, LANES), F32)
    wr = wr.at[:, :MOE_GROUPS].set(router_group_w[0])
    wr = wr.at[:, MOE_GROUPS:MOE_GROUPS + MOE_EXPERTS].set(router_expert_w[0])
    br = jnp.zeros((1, LANES), F32)
    br = br.at[0, :MOE_GROUPS].set(router_group_b[0])
    br = br.at[0, MOE_GROUPS:MOE_GROUPS + MOE_EXPERTS].set(router_expert_b[0])
    wr_hi = wr.astype(BF16)
    wr_lo = (wr - wr_hi.astype(F32)).astype(BF16)
    h1, xn, ri, rw = _outproj_router(x2, o_da, o_r, w_out[0].astype(BF16), ffn_norm_g[0][None, :],
                                     jnp.concatenate([wr_hi, wr_lo], axis=1), br)

    dest, blk, used = _plan(ri)
    dest_flat = dest[:, :2].reshape(N_ASSIGN)
    used1 = used[0, :1]
    blk_e = blk[:N_BLOCKS, 0]
    blk_e = blk_e[jnp.minimum(jnp.arange(N_BLOCKS, dtype=I32), used1[0] - 1)]
    xs = _dispatch(dest_flat, used[1, :MOE_EXPERTS], used[2, :MOE_EXPERTS], used1, xn)

    y = _ffn(blk_e, used1, xs, expert_w_gate[0], expert_w_up[0], expert_w_down[0])
    out = _combine(dest_flat, h1, rw, final_norm_g[None, :], y)
    return out.reshape(B, S, D)
```

```python
import functools
import math

import jax
import jax.numpy as jnp
from jax import lax
from jax.experimental import pallas as pl
from jax.experimental.pallas import tpu as pltpu

F32 = jnp.float32
BF16 = jnp.bfloat16
I32 = jnp.int32

D_MODEL = 1024
SEQ = 16384
CHUNK = 64
EPS = 1e-6

DA_HEADS = 4
DA_HEAD_DIM = 64
DA_V_DIM = 128
DA_WIDTH = 512
ALIBI_MAX = 8.0
RET_HEADS = 4
RET_QK_DIM = 64
RET_V_DIM = 128
RET_WIDTH = 512
T_ROWS = 512
MAIN_COLS = 2048
DK_OFF = 0
RQ_OFF = 512
RK_OFF = 768
RV_OFF = 1024
RG_OFF = 1536

MOE_GROUPS = 4
MOE_EXPERTS_PER_GROUP = 8
MOE_EXPERTS = 32
MOE_HIDDEN = 512
LAMBDA_INIT = 0.8 - 0.6 * math.exp(-0.3 * 0)

LANES = 128
ROW_TILE = 8
VMEM_LIMIT = 56 * 1024 * 1024

PROJ_TM = 512
ATT_T = 512
RET_C = 256
PLAN_T = 512
FFN_B = 256
N_ASSIGN = 2 * SEQ
N_BLOCKS = N_ASSIGN // FFN_B + MOE_EXPERTS
N_BLOCKS_PAD = (N_BLOCKS + 7) // 8 * 8
N_BUF = N_BLOCKS * FFN_B
COMB_TM = 256
DISP_TM = 512
DMA_UNROLL = 8


def _params(sem):
    return pltpu.CompilerParams(dimension_semantics=sem, vmem_limit_bytes=VMEM_LIMIT)


def _inproj_kernel(x_ref, g_ref, w_ref, wq_ref, wv_ref, o_ref, qt_ref, vt_ref):
    x = x_ref[...]
    var = jnp.mean(x * x, axis=-1, keepdims=True)
    xn = (x * lax.rsqrt(var + EPS) * g_ref[...]).astype(BF16)
    for c in range(MAIN_COLS // 512):
        sl = slice(c * 512, (c + 1) * 512)
        o_ref[:, sl] = jnp.dot(xn, w_ref[:, sl], preferred_element_type=F32).astype(BF16)
    nt = (((1,), (1,)), ((), ()))
    qt = lax.dot_general(wq_ref[...], xn, nt, preferred_element_type=F32)
    qt_ref[...] = qt.astype(BF16).reshape(DA_HEADS, 1, 2 * DA_HEAD_DIM, PROJ_TM)
    vt = lax.dot_general(wv_ref[...], xn, nt, preferred_element_type=F32)
    vt_ref[...] = vt.astype(BF16).reshape(DA_HEADS, 1, DA_V_DIM, PROJ_TM)


def _inproj(x2, g, w_bf, wq_t, wv_t):
    t_shape = jax.ShapeDtypeStruct((DA_HEADS, SEQ // PROJ_TM, LANES, PROJ_TM), BF16)
    t_spec = pl.BlockSpec((DA_HEADS, 1, LANES, PROJ_TM), lambda i: (0, i, 0, 0))
    w_t_spec = pl.BlockSpec((T_ROWS, D_MODEL), lambda i: (0, 0))
    return pl.pallas_call(
        _inproj_kernel,
        out_shape=(jax.ShapeDtypeStruct((SEQ, MAIN_COLS), BF16), t_shape, t_shape),
        grid=(SEQ // PROJ_TM,),
        in_specs=[
            pl.BlockSpec((PROJ_TM, D_MODEL), lambda i: (i, 0)),
            pl.BlockSpec((1, D_MODEL), lambda i: (0, 0)),
            pl.BlockSpec((D_MODEL, MAIN_COLS), lambda i: (0, 0)),
            w_t_spec, w_t_spec,
        ],
        out_specs=(pl.BlockSpec((PROJ_TM, MAIN_COLS), lambda i: (i, 0)), t_spec, t_spec),
        compiler_params=_params(("arbitrary",)),
        name="inproj",
    )(x2, g, w_bf, wq_t, wv_t)


ACC_ROWS = DA_V_DIM + 16


def _attn_kernel(slope_ref, qt_ref, k_ref, vt_ref, qa_ref, ka1_ref, ka2_ref, dtab_ref,
                 lq1_ref, lk1_ref, lq2_ref, lk2_ref, g_ref, o_ref,
                 m_sc, acc_sc, s0_sc, s1_sc, mx0_sc, mx1_sc):
    T = ATT_T
    h = pl.program_id(0)
    i = pl.program_id(1)
    slope = slope_ref[h]
    qt = qt_ref[0, 0]
    qa = qa_ref[0]
    qw = (jnp.concatenate([qt[0:DA_HEAD_DIM], qa], axis=0),
          jnp.concatenate([qa, qt[DA_HEAD_DIM:]], axis=0))
    lane = lax.broadcasted_iota(I32, (T, LANES), 1)
    sums_row = (lax.broadcasted_iota(I32, (16, T), 0) == 0).astype(BF16)
    s_bufs = (s0_sc, s1_sc)
    mx_bufs = (mx0_sc, mx1_sc)

    def scores(j, buf, extra):
        kt = k_ref[pl.ds(pl.multiple_of(j * T, T), T), :]
        ks = (jnp.where(lane < DA_HEAD_DIM, kt, ka1_ref[0]),
              jnp.where(lane >= DA_HEAD_DIM, kt, ka2_ref[0]))
        for mp in range(2):
            s = jnp.dot(ks[mp], qw[mp], preferred_element_type=F32)
            if extra is not None:
                s = s + extra
            s_bufs[buf][mp] = s
            mx_bufs[buf][mp] = jnp.max(s, axis=0, keepdims=True)

    def accumulate(j, buf):
        c = slope * ((i - j) * T).astype(F32)
        vte = jnp.concatenate([vt_ref[0, j], sums_row], axis=0)
        for mp in range(2):
            m_prev = m_sc[mp]
            m_new = jnp.maximum(m_prev, mx_bufs[buf][mp] - c)
            p = jnp.exp(s_bufs[buf][mp] - (m_new + c)).astype(BF16)
            pv = jnp.dot(vte, p, preferred_element_type=F32)
            acc_sc[mp] = jnp.exp(m_prev - m_new) * acc_sc[mp] + pv
            m_sc[mp] = m_new

    m_sc[...] = jnp.full_like(m_sc, -jnp.inf)
    acc_sc[...] = jnp.zeros_like(acc_sc)
    scores(i, 0, dtab_ref[0])

    def pair(k, carry):
        prev = jnp.where(k == 0, i, 2 * k - 1)
        accumulate(prev, 0)
        scores(2 * k, 1, None)
        accumulate(2 * k, 1)
        scores(2 * k + 1, 0, None)
        return carry

    npairs = i // 2
    lax.fori_loop(0, npairs, pair, 0)
    last = jnp.where(npairs == 0, i, 2 * npairs - 1)

    @pl.when(i % 2 == 1)
    def _():
        accumulate(last, 0)
        scores(i - 1, 1, None)
        accumulate(i - 1, 1)

    @pl.when(i % 2 == 0)
    def _():
        accumulate(last, 0)

    lam = (jnp.exp(jnp.sum(lq1_ref[...] * lk1_ref[...], axis=1, keepdims=True))
           - jnp.exp(jnp.sum(lq2_ref[...] * lk2_ref[...], axis=1, keepdims=True))
           + LAMBDA_INIT)
    a1 = acc_sc[0]
    a2 = acc_sc[1]
    ot = (a1[0:DA_V_DIM] / a1[DA_V_DIM:DA_V_DIM + 1]
          - lam * (a2[0:DA_V_DIM] / a2[DA_V_DIM:DA_V_DIM + 1]))
    o = ot.T
    var = jnp.mean(o * o, axis=-1, keepdims=True)
    o = (o * lax.rsqrt(var + EPS) * g_ref[...]) * (1.0 - LAMBDA_INIT)
    o_ref[...] = o.astype(BF16)


def _attention(proj, qt4, vt4, slopes, qa, ka1, ka2, dtab, lq1, lk1, lq2, lk2, subln_g):
    T = ATT_T
    vec64 = pl.BlockSpec((1, DA_HEAD_DIM), lambda h, i: (0, 0))
    per_head = lambda a, b: pl.BlockSpec((1, a, b), lambda h, i: (h, 0, 0))
    return pl.pallas_call(
        _attn_kernel,
        out_shape=jax.ShapeDtypeStruct((SEQ, DA_WIDTH), BF16),
        grid=(DA_HEADS, SEQ // T),
        in_specs=[
            pl.BlockSpec(memory_space=pltpu.SMEM),
            pl.BlockSpec((1, 1, LANES, T), lambda h, i: (h, i, 0, 0)),
            pl.BlockSpec((SEQ, LANES), lambda h, i: (0, DK_OFF // LANES + h)),
            pl.BlockSpec((1, SEQ // T, LANES, T), lambda h, i: (h, 0, 0, 0)),
            per_head(DA_HEAD_DIM, T), per_head(T, LANES), per_head(T, LANES), per_head(T, T),
            vec64, vec64, vec64, vec64,
            pl.BlockSpec((1, DA_V_DIM), lambda h, i: (0, 0)),
        ],
        out_specs=pl.BlockSpec((T, LANES), lambda h, i: (i, h)),
        scratch_shapes=[
            pltpu.VMEM((2, 1, T), F32),
            pltpu.VMEM((2, ACC_ROWS, T), F32),
            pltpu.VMEM((2, T, T), F32),
            pltpu.VMEM((2, T, T), F32),
            pltpu.VMEM((2, 1, T), F32),
            pltpu.VMEM((2, 1, T), F32),
        ],
        compiler_params=_params(("arbitrary", "arbitrary")),
        name="diff_attention",
    )(slopes, qt4, proj, vt4, qa, ka1, ka2, dtab, lq1, lk1, lq2, lk2, subln_g)


def _ret_kernel(cd_ref, q_ref, k_ref, v_ref, g_ref, dec_ref, qdec_ref, kdec_ref, o_ref, st_sc):
    @pl.when(pl.program_id(0) == 0)
    def _():
        st_sc[...] = jnp.zeros_like(st_sc)

    for h in range(RET_HEADS):
        qk = slice(h * RET_QK_DIM, (h + 1) * RET_QK_DIM)
        vv = slice(h * RET_V_DIM, (h + 1) * RET_V_DIM)
        q = q_ref[:, qk]
        k = k_ref[:, qk]
        v = v_ref[:, vv]
        g = g_ref[:, vv].astype(F32)
        s = lax.dot_general(q, k, (((1,), (1,)), ((), ())),
                            preferred_element_type=F32) * dec_ref[h]
        intra = jnp.dot(s.astype(BF16), v, preferred_element_type=F32)
        st = st_sc[h]
        cross = jnp.dot(q, st.astype(BF16), preferred_element_type=F32) * qdec_ref[h]
        kd = (k.astype(F32) * kdec_ref[h]).astype(BF16)
        st_sc[h] = st * cd_ref[h] + lax.dot_general(kd, v, (((0,), (0,)), ((), ())),
                                                    preferred_element_type=F32)
        o = intra + cross
        o = o * lax.rsqrt(jnp.mean(o * o, axis=-1, keepdims=True) + EPS)
        o = (g / (1.0 + jnp.exp(-g))) * o
        o_ref[:, vv] = o.astype(BF16)


def _retention(proj, cd, dec, qdec, kdec):
    C = RET_C
    return pl.pallas_call(
        _ret_kernel,
        out_shape=jax.ShapeDtypeStruct((SEQ, RET_WIDTH), BF16),
        grid=(SEQ // C,),
        in_specs=[
            pl.BlockSpec(memory_space=pltpu.SMEM),
            pl.BlockSpec((C, 256), lambda i: (i, RQ_OFF // 256)),
            pl.BlockSpec((C, 256), lambda i: (i, RK_OFF // 256)),
            pl.BlockSpec((C, 512), lambda i: (i, RV_OFF // 512)),
            pl.BlockSpec((C, 512), lambda i: (i, RG_OFF // 512)),
            pl.BlockSpec((RET_HEADS, C, C), lambda i: (0, 0, 0)),
            pl.BlockSpec((RET_HEADS, C, 1), lambda i: (0, 0, 0)),
            pl.BlockSpec((RET_HEADS, C, 1), lambda i: (0, 0, 0)),
        ],
        out_specs=pl.BlockSpec((C, RET_WIDTH), lambda i: (i, 0)),
        scratch_shapes=[pltpu.VMEM((RET_HEADS, RET_QK_DIM, RET_V_DIM), F32)],
        compiler_params=_params(("arbitrary",)),
        name="retention",
    )(cd, proj, proj, proj, proj, dec, qdec, kdec)


def _outproj_router_kernel(x_ref, oda_ref, or_ref, wo_ref, g_ref, wr_ref, br_ref,
                           h1_ref, xn_ref, ri_ref, rw_ref):
    h1 = (x_ref[...]
          + jnp.dot(oda_ref[...], wo_ref[0:DA_WIDTH, :], preferred_element_type=F32)
          + jnp.dot(or_ref[...], wo_ref[DA_WIDTH:, :], preferred_element_type=F32))
    h1_ref[...] = h1
    var = jnp.mean(h1 * h1, axis=-1, keepdims=True)
    xn = h1 * lax.rsqrt(var + EPS) * g_ref[...]
    for c in range(ROW_TILE):
        xn_ref[pl.ds(c, PROJ_TM, stride=ROW_TILE), :] = xn[:, c * LANES:(c + 1) * LANES]
    x_hi = xn.astype(BF16)
    x_lo = (xn - x_hi.astype(F32)).astype(BF16)
    both = jnp.dot(x_hi, wr_ref[...], preferred_element_type=F32)
    logits = (both[:, :LANES] + both[:, LANES:]
              + jnp.dot(x_lo, wr_ref[:, :LANES], preferred_element_type=F32)) + br_ref[...]
    lane = lax.broadcasted_iota(I32, logits.shape, 1)
    neg = jnp.float32(-jnp.inf)
    big = jnp.int32(1 << 20)
    gl = jnp.where(lane < MOE_GROUPS, logits, neg)
    gmax = jnp.max(gl, axis=1, keepdims=True)
    gidx = jnp.min(jnp.where(gl == gmax, lane, big), axis=1, keepdims=True)
    gsum = jnp.sum(jnp.exp(gl - gmax), axis=1, keepdims=True)
    gp = 1.0 / gsum
    lo = MOE_GROUPS + gidx * MOE_EXPERTS_PER_GROUP
    el = jnp.where((lane >= lo) & (lane < lo + MOE_EXPERTS_PER_GROUP), logits, neg)
    v1 = jnp.max(el, axis=1, keepdims=True)
    i1 = jnp.min(jnp.where(el == v1, lane, big), axis=1, keepdims=True)
    el2 = jnp.where(lane == i1, neg, el)
    v2 = jnp.max(el2, axis=1, keepdims=True)
    i2 = jnp.min(jnp.where(el2 == v2, lane, big), axis=1, keepdims=True)
    t = jnp.exp(v2 - v1)
    w1 = gp / (1.0 + t)
    w2 = gp * t / (1.0 + t)
    ri_ref[...] = jnp.where(lane == 0, i1 - MOE_GROUPS,
                            jnp.where(lane == 1, i2 - MOE_GROUPS, 0))
    rw_ref[...] = jnp.where(lane == 0, w1, jnp.where(lane == 1, w2, 0.0))


def _outproj_router(x2, o_da, o_r, wo_bf, g, wr, br):
    tm = PROJ_TM
    row = lambda w: pl.BlockSpec((tm, w), lambda i: (i, 0))
    full = lambda a, b: pl.BlockSpec((a, b), lambda i: (0, 0))
    return pl.pallas_call(
        _outproj_router_kernel,
        out_shape=(
            jax.ShapeDtypeStruct((SEQ, D_MODEL), F32),
            jax.ShapeDtypeStruct((SEQ * ROW_TILE, LANES), F32),
            jax.ShapeDtypeStruct((SEQ, LANES), I32),
            jax.ShapeDtypeStruct((SEQ, LANES), F32),
        ),
        grid=(SEQ // tm,),
        in_specs=[row(D_MODEL), row(DA_WIDTH), row(RET_WIDTH), full(D_MODEL, D_MODEL),
                  full(1, D_MODEL), full(D_MODEL, 2 * LANES), full(1, LANES)],
        out_specs=(row(D_MODEL), pl.BlockSpec((tm * ROW_TILE, LANES), lambda i: (i, 0)),
                   row(LANES), row(LANES)),
        compiler_params=_params(("arbitrary",)),
        name="outproj_router",
    )(x2, o_da, o_r, wo_bf, g, wr, br)


def _plan_kernel(ri_ref, dest_ref, blk_ref, used_ref):
    TT = PLAN_T
    lane = lax.broadcasted_iota(I32, (TT, LANES), 1)

    def onehots(t):
        r = ri_ref[pl.ds(pl.multiple_of(t * TT, TT), TT), :]
        return lane == r[:, 0:1], lane == r[:, 1:2]

    def count_body(t, acc):
        oh1, oh2 = onehots(t)
        return acc + jnp.sum((oh1 | oh2).astype(F32), axis=0, keepdims=True)

    counts = lax.fori_loop(0, SEQ // TT, count_body, jnp.zeros((1, LANES), F32))
    counts8 = jnp.broadcast_to(counts, (8, LANES)).astype(I32)
    shift = FFN_B.bit_length() - 1
    padded = ((counts8 + (FFN_B - 1)) >> shift) << shift
    lane8 = lax.broadcasted_iota(I32, (8, LANES), 1)
    pad_end = padded
    sh = 1
    while sh < LANES:
        pad_end = pad_end + jnp.where(lane8 >= sh, pltpu.roll(pad_end, sh, axis=1), 0)
        sh *= 2
    pad_start = pad_end - padded

    ltri = (lax.broadcasted_iota(I32, (TT, TT), 0)
            > lax.broadcasted_iota(I32, (TT, TT), 1)).astype(BF16)

    def dest_body(t, carry):
        oh1, oh2 = onehots(t)
        a = (oh1 | oh2).astype(F32)
        base = jnp.dot(ltri, a.astype(BF16), preferred_element_type=F32) + carry
        d1 = jnp.sum(jnp.where(oh1, base, 0.0), axis=1, keepdims=True)
        d2 = jnp.sum(jnp.where(oh2, base, 0.0), axis=1, keepdims=True)
        dest_ref[pl.ds(pl.multiple_of(t * TT, TT), TT), :] = jnp.where(
            lane == 0, d1, jnp.where(lane == 1, d2, 0.0)).astype(I32)
        return carry + jnp.sum(a, axis=0, keepdims=True)

    lax.fori_loop(0, SEQ // TT, dest_body, pad_start[0:1].astype(F32))

    rows = lax.broadcasted_iota(I32, (N_BLOCKS_PAD, LANES), 0) * FFN_B
    lanes = lax.broadcasted_iota(I32, (N_BLOCKS_PAD, LANES), 1)
    ended = (jnp.broadcast_to(pad_end[0:1], (N_BLOCKS_PAD, LANES)) <= rows) & (lanes < MOE_EXPERTS)
    be = jnp.sum(ended.astype(I32), axis=1, keepdims=True)
    blk_ref[...] = jnp.broadcast_to(jnp.minimum(be, MOE_EXPERTS - 1), (N_BLOCKS_PAD, LANES))
    total = jnp.max(pad_end, axis=1, keepdims=True)
    row8 = lax.broadcasted_iota(I32, (8, LANES), 0)
    used_ref[...] = jnp.where(row8 == 0, jnp.broadcast_to(total >> shift, (8, LANES)),
                              jnp.where(row8 == 1, counts8, pad_start))


def _plan(ri):
    return pl.pallas_call(
        _plan_kernel,
        out_shape=(
            jax.ShapeDtypeStruct((SEQ, LANES), I32),
            jax.ShapeDtypeStruct((N_BLOCKS_PAD, LANES), I32),
            jax.ShapeDtypeStruct((8, LANES), I32),
        ),
        compiler_params=pltpu.CompilerParams(vmem_limit_bytes=VMEM_LIMIT),
        name="route_plan",
    )(ri)


PAD_BITS = FFN_B.bit_length() - 1


def _pad_fill_copies(e, cnt_ref, pst_ref, zero_sc, xs_hbm, zsem):
    cnt = cnt_ref[e]
    pad = (-cnt) & (FFN_B - 1)
    row = pst_ref[e] + cnt
    out = []
    for bit in reversed(range(PAD_BITS)):
        n = 1 << bit
        start = row + ((pad >> (bit + 1)) << (bit + 1))
        copy = pltpu.make_async_copy(
            zero_sc.at[pl.ds(0, n * ROW_TILE)],
            xs_hbm.at[pl.ds(pl.multiple_of(start * ROW_TILE, ROW_TILE), n * ROW_TILE)], zsem)
        out.append(((pad & n) != 0, copy))
    return out


def _unused_block_copies(b, zero_sc, xs_hbm, zsem):
    half = FFN_B // 2 * ROW_TILE
    return [pltpu.make_async_copy(
        zero_sc, xs_hbm.at[pl.ds(pl.multiple_of((2 * b + k) * half, half), half)], zsem)
        for k in range(2)]


def _dispatch_kernel(dest_ref, cnt_ref, pst_ref, used_ref, xn_ref, xs_hbm, zero_sc, sem, zsem):
    tm = DISP_TM
    i = pl.program_id(0)

    @pl.when(i == 0)
    def _():
        zero_sc[...] = jnp.zeros_like(zero_sc)

        def fill(e, carry):
            for cond, copy in _pad_fill_copies(e, cnt_ref, pst_ref, zero_sc, xs_hbm, zsem):
                pl.when(cond)(copy.start)
            return carry

        lax.fori_loop(0, MOE_EXPERTS, fill, 0)

        def fill_block(b, carry):
            for copy in _unused_block_copies(b, zero_sc, xs_hbm, zsem):
                copy.start()
            return carry

        lax.fori_loop(used_ref[0], N_BLOCKS, fill_block, 0)

    def issue(it, carry):
        for u in range(DMA_UNROLL):
            r = it * DMA_UNROLL + u
            a = (i * tm + r) * 2
            src = xn_ref.at[pl.ds(pl.multiple_of(r * ROW_TILE, ROW_TILE), ROW_TILE)]
            for kk in range(2):
                d = pl.multiple_of(dest_ref[a + kk] * ROW_TILE, ROW_TILE)
                pltpu.make_async_copy(src, xs_hbm.at[pl.ds(d, ROW_TILE)], sem).start()
        return carry

    lax.fori_loop(0, tm // DMA_UNROLL, issue, 0)
    for _ in range(2):
        pltpu.make_async_copy(xn_ref, xs_hbm.at[pl.ds(0, tm * ROW_TILE)], sem).wait()

    @pl.when(i == 0)
    def _():
        def drain(e, carry):
            for cond, copy in _pad_fill_copies(e, cnt_ref, pst_ref, zero_sc, xs_hbm, zsem):
                pl.when(cond)(copy.wait)
            return carry

        lax.fori_loop(0, MOE_EXPERTS, drain, 0)

        def drain_block(b, carry):
            for copy in _unused_block_copies(b, zero_sc, xs_hbm, zsem):
                copy.wait()
            return carry

        lax.fori_loop(used_ref[0], N_BLOCKS, drain_block, 0)


def _dispatch(dest_flat, counts, pad_start, used, xn3):
    tm = DISP_TM
    return pl.pallas_call(
        _dispatch_kernel,
        out_shape=jax.ShapeDtypeStruct((N_BUF * ROW_TILE, LANES), F32),
        grid_spec=pltpu.PrefetchScalarGridSpec(
            num_scalar_prefetch=4,
            grid=(SEQ // tm,),
            in_specs=[pl.BlockSpec((tm * ROW_TILE, LANES), lambda i, d, c, p, u: (i, 0))],
            out_specs=pl.BlockSpec(memory_space=pl.ANY),
            scratch_shapes=[
                pltpu.VMEM((FFN_B // 2 * ROW_TILE, LANES), F32),
                pltpu.SemaphoreType.DMA(()),
                pltpu.SemaphoreType.DMA(()),
            ],
        ),
        compiler_params=_params(("arbitrary",)),
        name="moe_dispatch",
    )(dest_flat, counts, pad_start, used, xn3)


def _ffn_kernel(blk_ref, used_ref, xs_ref, wg_ref, wu_ref, wd_ref, y_ref, wg_bf, wu_bf, wd_bf):
    B = FFN_B
    b = pl.program_id(0)

    @pl.when(b < used_ref[0])
    def _():
        prev = blk_ref[jnp.maximum(b - 1, 0)]

        @pl.when((b == 0) | (blk_ref[b] != prev))
        def _():
            wg_bf[...] = wg_ref[0].astype(BF16)
            wu_bf[...] = wu_ref[0].astype(BF16)
            wd_bf[...] = wd_ref[0].astype(BF16)

        x = jnp.concatenate([xs_ref[pl.ds(c, B, stride=ROW_TILE), :] for c in range(ROW_TILE)],
                            axis=1).astype(BF16)
        hg = jnp.dot(x, wg_bf[...], preferred_element_type=F32)
        hu = jnp.dot(x, wu_bf[...], preferred_element_type=F32)
        hh = ((hg / (1.0 + jnp.exp(-hg))) * hu).astype(BF16)
        y = jnp.dot(hh, wd_bf[...], preferred_element_type=F32)
        for c in range(ROW_TILE):
            y_ref[pl.ds(c, B, stride=ROW_TILE), :] = y[:, c * LANES:(c + 1) * LANES]

    @pl.when(b >= used_ref[0])
    def _():
        y_ref[...] = jnp.zeros_like(y_ref)


def _ffn(blk_e, used, xs, w_gate, w_up, w_down):
    B = FFN_B
    wspec = lambda a, c: pl.BlockSpec((1, a, c), lambda b, blk, used: (blk[b], 0, 0))
    return pl.pallas_call(
        _ffn_kernel,
        out_shape=jax.ShapeDtypeStruct((N_BUF * ROW_TILE, LANES), F32),
        grid_spec=pltpu.PrefetchScalarGridSpec(
            num_scalar_prefetch=2,
            grid=(N_BLOCKS,),
            in_specs=[
                pl.BlockSpec((B * ROW_TILE, LANES),
                             lambda b, blk, used: (jnp.minimum(b, used[0] - 1), 0)),
                wspec(D_MODEL, MOE_HIDDEN),
                wspec(D_MODEL, MOE_HIDDEN),
                wspec(MOE_HIDDEN, D_MODEL),
            ],
            out_specs=pl.BlockSpec((B * ROW_TILE, LANES), lambda b, blk, used: (b, 0)),
            scratch_shapes=[
                pltpu.VMEM((D_MODEL, MOE_HIDDEN), BF16),
                pltpu.VMEM((D_MODEL, MOE_HIDDEN), BF16),
                pltpu.VMEM((MOE_HIDDEN, D_MODEL), BF16),
            ],
        ),
        compiler_params=_params(("arbitrary",)),
        name="expert_ffn",
    )(blk_e, used, xs, w_gate, w_up, w_down)


def _combine_kernel(dest_ref, h1_ref, rw_ref, g_ref, y_hbm, o_ref, ybuf, sem):
    tm = COMB_TM
    i = pl.program_id(0)

    def gather(tile, slot):
        def issue(it, carry):
            for u in range(DMA_UNROLL):
                r = it * DMA_UNROLL + u
                a = (tile * tm + r) * 2
                for kk in range(2):
                    d = pl.multiple_of(dest_ref[a + kk] * ROW_TILE, ROW_TILE)
                    pltpu.make_async_copy(
                        y_hbm.at[pl.ds(d, ROW_TILE)],
                        ybuf.at[slot, kk, pl.ds(pl.multiple_of(r * ROW_TILE, ROW_TILE), ROW_TILE)],
                        sem.at[slot, kk]).start()
            return carry

        lax.fori_loop(0, tm // DMA_UNROLL, issue, 0)

    @pl.when(i == 0)
    def _():
        gather(0, 0)

    @pl.when(i + 1 < pl.num_programs(0))
    def _():
        gather(i + 1, (i + 1) % 2)

    slot = i % 2
    for kk in range(2):
        pltpu.make_async_copy(y_hbm.at[pl.ds(0, tm * ROW_TILE)], ybuf.at[slot, kk],
                              sem.at[slot, kk]).wait()
    w = rw_ref[...]
    ys = [jnp.concatenate([ybuf[slot, kk, pl.ds(c, tm, stride=ROW_TILE), :]
                           for c in range(ROW_TILE)], axis=1) for kk in range(2)]
    h = h1_ref[...] + w[:, 0:1] * ys[0] + w[:, 1:2] * ys[1]
    var = jnp.mean(h * h, axis=-1, keepdims=True)
    o_ref[...] = h * lax.rsqrt(var + EPS) * g_ref[...]


def _combine(dest_flat, h1, rw, g, y):
    tm = COMB_TM
    return pl.pallas_call(
        _combine_kernel,
        out_shape=jax.ShapeDtypeStruct((SEQ, D_MODEL), F32),
        grid_spec=pltpu.PrefetchScalarGridSpec(
            num_scalar_prefetch=1,
            grid=(SEQ // tm,),
            in_specs=[
                pl.BlockSpec((tm, D_MODEL), lambda i, d: (i, 0)),
                pl.BlockSpec((tm, LANES), lambda i, d: (i, 0)),
                pl.BlockSpec((1, D_MODEL), lambda i, d: (0, 0)),
                pl.BlockSpec(memory_space=pl.ANY),
            ],
            out_specs=pl.BlockSpec((tm, D_MODEL), lambda i, d: (i, 0)),
            scratch_shapes=[
                pltpu.VMEM((2, 2, tm * ROW_TILE, LANES), F32),
                pltpu.SemaphoreType.DMA((2, 2)),
            ],
        ),
        compiler_params=_params(("arbitrary",)),
        name="moe_combine",
    )(dest_flat, h1, rw, g, y)


def _attention_tables():
    T = ATT_T
    slopes = jnp.exp2(-ALIBI_MAX * jnp.arange(1, DA_HEADS + 1, dtype=F32) / DA_HEADS)
    r = jnp.arange(T)
    hi = ((r // CHUNK) * CHUNK).astype(F32)
    lo = (r % CHUNK).astype(F32)
    ones = jnp.ones((T,), F32)
    sl = slopes[:, None]
    one_h = jnp.broadcast_to(ones, (DA_HEADS, T))
    q_rows = jnp.stack([one_h, one_h, -sl * hi[None], -sl * lo[None]], axis=1)
    k_cols = jnp.stack([sl * hi[None], sl * lo[None], one_h, one_h], axis=-1)
    qa = jnp.zeros((DA_HEADS, DA_HEAD_DIM, T), F32).at[:, 0:4, :].set(q_rows)
    ka1 = jnp.zeros((DA_HEADS, T, LANES), F32).at[:, :, DA_HEAD_DIM:DA_HEAD_DIM + 4].set(k_cols)
    ka2 = jnp.zeros((DA_HEADS, T, LANES), F32).at[:, :, 0:4].set(k_cols)
    rel = (r[:, None] - r[None, :]).astype(F32)
    allowed = (r[:, None] // CHUNK) <= (r[None, :] // CHUNK)
    fix = jnp.where(rel > 0, -2.0 * slopes[:, None, None] * rel[None], 0.0)
    dtab = jnp.where(allowed[None], fix, -jnp.inf)
    return slopes, qa.astype(BF16), ka1.astype(BF16), ka2.astype(BF16), dtab


def _retention_tables():
    C = RET_C
    log_gamma = jnp.log1p(-jnp.exp2(-5.0 - jnp.arange(RET_HEADS, dtype=F32)))
    pos = jnp.arange(C, dtype=F32)
    rel = pos[:, None] - pos[None, :]
    dec = jnp.where(rel >= 0, jnp.exp(log_gamma[:, None, None] * jnp.maximum(rel, 0.0)), 0.0)
    qdec = jnp.exp(log_gamma[:, None] * (pos + 1.0)[None, :])[:, :, None]
    kdec = jnp.exp(log_gamma[:, None] * (C - 1 - pos)[None, :])[:, :, None]
    cd = jnp.exp(log_gamma * C)
    return cd, dec, qdec, kdec


def kernel(x, attn_norm_g, w_in, da_lambda_q1, da_lambda_k1, da_lambda_q2, da_lambda_k2,
           da_subln_g, w_out, ffn_norm_g, router_group_w, router_group_b, router_expert_w,
           router_expert_b, expert_w_gate, expert_w_up, expert_w_down, final_norm_g):
    B, S, D = x.shape
    assert (B, S, D) == (1, SEQ, D_MODEL)
    x2 = x.reshape(S, D)

    w = w_in[0]
    w_main = jnp.concatenate([
        w[:, 512:1024],
        w[:, 1536:1792],
        w[:, 1792:2048] * (RET_QK_DIM ** -0.5),
        w[:, 2048:3072]], axis=1).astype(BF16)
    wq_t = (w[:, 0:512] * (DA_HEAD_DIM ** -0.5)).T.astype(BF16)
    wv_t = w[:, 1024:1536].T.astype(BF16)
    proj, qt4, vt4 = _inproj(x2, attn_norm_g[0][None, :], w_main, wq_t, wv_t)

    slopes, qa, ka1, ka2, dtab = _attention_tables()
    o_da = _attention(proj, qt4, vt4, slopes, qa, ka1, ka2, dtab, da_lambda_q1, da_lambda_k1,
                      da_lambda_q2, da_lambda_k2, da_subln_g)
    cd, dec, qdec, kdec = _retention_tables()
    o_r = _retention(proj, cd, dec, qdec, kdec)

    wr = jnp.zeros((D, LANES), F32)
    wr = wr.at[:, :MOE_GROUPS].set(router_group_w[0])
    wr = wr.at[:, MOE_GROUPS:MOE_GROUPS + MOE_EXPERTS].set(router_expert_w[0])
    br = jnp.zeros((1, LANES), F32)
    br = br.at[0, :MOE_GROUPS].set(router_group_b[0])
    br = br.at[0, MOE_GROUPS:MOE_GROUPS + MOE_EXPERTS].set(router_expert_b[0])
    wr_hi = wr.astype(BF16)
    wr_lo = (wr - wr_hi.astype(F32)).astype(BF16)
    h1, xn, ri, rw = _outproj_router(x2, o_da, o_r, w_out[0].astype(BF16), ffn_norm_g[0][None, :],
                                     jnp.concatenate([wr_hi, wr_lo], axis=1), br)

    dest, blk, used = _plan(ri)
    dest_flat = dest[:, :2].reshape(N_ASSIGN)
    used1 = used[0, :1]
    blk_e = blk[:N_BLOCKS, 0]
    blk_e = blk_e[jnp.minimum(jnp.arange(N_BLOCKS, dtype=I32), used1[0] - 1)]
    xs = _dispatch(dest_flat, used[1, :MOE_EXPERTS], used[2, :MOE_EXPERTS], used1, xn)

    y = _ffn(blk_e, used1, xs, expert_w_gate[0], expert_w_up[0], expert_w_down[0])
    out = _combine(dest_flat, h1, rw, final_norm_g[None, :], y)
    return out.reshape(B, S, D)
```

```python
import functools
import math

import jax
import jax.numpy as jnp
from jax import lax
from jax.experimental import pallas as pl
from jax.experimental.pallas import tpu as pltpu

F32 = jnp.float32
BF16 = jnp.bfloat16
I32 = jnp.int32

D_MODEL = 1024
SEQ = 16384
CHUNK = 64
EPS = 1e-6

DA_HEADS = 4
DA_HEAD_DIM = 64
DA_V_DIM = 128
DA_WIDTH = 512
ALIBI_MAX = 8.0
RET_HEADS = 4
RET_QK_DIM = 64
RET_V_DIM = 128
RET_WIDTH = 512
T_ROWS = 512
MAIN_COLS = 2048
DK_OFF = 0
RQ_OFF = 512
RK_OFF = 768
RV_OFF = 1024
RG_OFF = 1536

MOE_GROUPS = 4
MOE_EXPERTS_PER_GROUP = 8
MOE_EXPERTS = 32
MOE_HIDDEN = 512
LAMBDA_INIT = 0.8 - 0.6 * math.exp(-0.3 * 0)

LANES = 128
ROW_TILE = 8
VMEM_LIMIT = 56 * 1024 * 1024

PROJ_TM = 512
ATT_T = 512
RET_C = 256
PLAN_T = 512
FFN_B = 256
N_ASSIGN = 2 * SEQ
N_BLOCKS = N_ASSIGN // FFN_B + MOE_EXPERTS
N_BLOCKS_PAD = (N_BLOCKS + 7) // 8 * 8
N_BUF = N_BLOCKS * FFN_B
COMB_TM = 256
DISP_TM = 512
DMA_UNROLL = 8


def _params(sem):
    return pltpu.CompilerParams(dimension_semantics=sem, vmem_limit_bytes=VMEM_LIMIT)


def _inproj_kernel(x_ref, g_ref, w_ref, wq_ref, wv_ref, o_ref, qt_ref, vt_ref):
    x = x_ref[...]
    var = jnp.mean(x * x, axis=-1, keepdims=True)
    xn = (x * lax.rsqrt(var + EPS) * g_ref[...]).astype(BF16)
    for c in range(MAIN_COLS // 512):
        sl = slice(c * 512, (c + 1) * 512)
        o_ref[:, sl] = jnp.dot(xn, w_ref[:, sl], preferred_element_type=F32).astype(BF16)
    nt = (((1,), (1,)), ((), ()))
    qt = lax.dot_general(wq_ref[...], xn, nt, preferred_element_type=F32)
    qt_ref[...] = qt.astype(BF16).reshape(DA_HEADS, 1, 2 * DA_HEAD_DIM, PROJ_TM)
    vt = lax.dot_general(wv_ref[...], xn, nt, preferred_element_type=F32)
    vt_ref[...] = vt.astype(BF16).reshape(DA_HEADS, 1, DA_V_DIM, PROJ_TM)


def _inproj(x2, g, w_bf, wq_t, wv_t):
    t_shape = jax.ShapeDtypeStruct((DA_HEADS, SEQ // PROJ_TM, LANES, PROJ_TM), BF16)
    t_spec = pl.BlockSpec((DA_HEADS, 1, LANES, PROJ_TM), lambda i: (0, i, 0, 0))
    w_t_spec = pl.BlockSpec((T_ROWS, D_MODEL), lambda i: (0, 0))
    return pl.pallas_call(
        _inproj_kernel,
        out_shape=(jax.ShapeDtypeStruct((SEQ, MAIN_COLS), BF16), t_shape, t_shape),
        grid=(SEQ // PROJ_TM,),
        in_specs=[
            pl.BlockSpec((PROJ_TM, D_MODEL), lambda i: (i, 0)),
            pl.BlockSpec((1, D_MODEL), lambda i: (0, 0)),
            pl.BlockSpec((D_MODEL, MAIN_COLS), lambda i: (0, 0)),
            w_t_spec, w_t_spec,
        ],
        out_specs=(pl.BlockSpec((PROJ_TM, MAIN_COLS), lambda i: (i, 0)), t_spec, t_spec),
        compiler_params=_params(("arbitrary",)),
        name="inproj",
    )(x2, g, w_bf, wq_t, wv_t)


ACC_ROWS = DA_V_DIM + 16


def _attn_kernel(slope_ref, qt_ref, k_ref, vt_ref, qa_ref, ka1_ref, ka2_ref, dtab_ref,
                 lq1_ref, lk1_ref, lq2_ref, lk2_ref, g_ref, o_ref,
                 m_sc, acc_sc, s0_sc, s1_sc, s2_sc, mx0_sc, mx1_sc, mx2_sc):
    T = ATT_T
    h = pl.program_id(0)
    i = pl.program_id(1)
    slope = slope_ref[h]
    qt = qt_ref[0, 0]
    qa = qa_ref[0]
    qw = (jnp.concatenate([qt[0:DA_HEAD_DIM], qa], axis=0),
          jnp.concatenate([qa, qt[DA_HEAD_DIM:]], axis=0))
    lane = lax.broadcasted_iota(I32, (T, LANES), 1)
    sums_row = (lax.broadcasted_iota(I32, (16, T), 0) == 0).astype(BF16)
    s_bufs = (s0_sc, s1_sc, s2_sc)
    mx_bufs = (mx0_sc, mx1_sc, mx2_sc)

    def scores(j, buf, extra):
        kt = k_ref[pl.ds(pl.multiple_of(j * T, T), T), :]
        ks = (jnp.where(lane < DA_HEAD_DIM, kt, ka1_ref[0]),
              jnp.where(lane >= DA_HEAD_DIM, kt, ka2_ref[0]))
        for mp in range(2):
            s = jnp.dot(ks[mp], qw[mp], preferred_element_type=F32)
            if extra is not None:
                s = s + extra
            s_bufs[buf][mp] = s
            mx_bufs[buf][mp] = jnp.max(s, axis=0, keepdims=True)

    def accumulate(j, buf):
        c = slope * ((i - j) * T).astype(F32)
        vte = jnp.concatenate([vt_ref[0, j], sums_row], axis=0)
        for mp in range(2):
            m_prev = m_sc[mp]
            m_new = jnp.maximum(m_prev, mx_bufs[buf][mp] - c)
            p = jnp.exp(s_bufs[buf][mp] - (m_new + c)).astype(BF16)
            pv = jnp.dot(vte, p, preferred_element_type=F32)
            acc_sc[mp] = jnp.exp(m_prev - m_new) * acc_sc[mp] + pv
            m_sc[mp] = m_new

    m_sc[...] = jnp.full_like(m_sc, -jnp.inf)
    acc_sc[...] = jnp.zeros_like(acc_sc)
    def first_tile(t):
        return jnp.where(t == 0, i, t - 1)

    scores(i, 0, dtab_ref[0])

    @pl.when(i == 0)
    def _():
        accumulate(i, 0)

    @pl.when(i >= 1)
    def _():
        scores(0, 1, None)

        def triple(k, carry):
            t = 3 * k
            accumulate(first_tile(t), 0)
            scores(t + 1, 2, None)
            accumulate(t, 1)
            scores(t + 2, 0, None)
            accumulate(t + 1, 2)
            scores(t + 3, 1, None)
            return carry

        steady = i - 1
        lax.fori_loop(0, steady // 3, triple, 0)
        t0 = steady // 3 * 3
        rem = steady - t0

        @pl.when(rem == 0)
        def _():
            accumulate(first_tile(t0), 0)
            accumulate(t0, 1)

        @pl.when(rem == 1)
        def _():
            accumulate(first_tile(t0), 0)
            scores(t0 + 1, 2, None)
            accumulate(t0, 1)
            accumulate(t0 + 1, 2)

        @pl.when(rem == 2)
        def _():
            accumulate(first_tile(t0), 0)
            scores(t0 + 1, 2, None)
            accumulate(t0, 1)
            scores(t0 + 2, 0, None)
            accumulate(t0 + 1, 2)
            accumulate(t0 + 2, 0)

    lam = (jnp.exp(jnp.sum(lq1_ref[...] * lk1_ref[...], axis=1, keepdims=True))
           - jnp.exp(jnp.sum(lq2_ref[...] * lk2_ref[...], axis=1, keepdims=True))
           + LAMBDA_INIT)
    a1 = acc_sc[0]
    a2 = acc_sc[1]
    ot = (a1[0:DA_V_DIM] / a1[DA_V_DIM:DA_V_DIM + 1]
          - lam * (a2[0:DA_V_DIM] / a2[DA_V_DIM:DA_V_DIM + 1]))
    o = ot.T
    var = jnp.mean(o * o, axis=-1, keepdims=True)
    o = (o * lax.rsqrt(var + EPS) * g_ref[...]) * (1.0 - LAMBDA_INIT)
    o_ref[...] = o.astype(BF16)


def _attention(proj, qt4, vt4, slopes, qa, ka1, ka2, dtab, lq1, lk1, lq2, lk2, subln_g):
    T = ATT_T
    vec64 = pl.BlockSpec((1, DA_HEAD_DIM), lambda h, i: (0, 0))
    per_head = lambda a, b: pl.BlockSpec((1, a, b), lambda h, i: (h, 0, 0))
    return pl.pallas_call(
        _attn_kernel,
        out_shape=jax.ShapeDtypeStruct((SEQ, DA_WIDTH), BF16),
        grid=(DA_HEADS, SEQ // T),
        in_specs=[
            pl.BlockSpec(memory_space=pltpu.SMEM),
            pl.BlockSpec((1, 1, LANES, T), lambda h, i: (h, i, 0, 0)),
            pl.BlockSpec((SEQ, LANES), lambda h, i: (0, DK_OFF // LANES + h)),
            pl.BlockSpec((1, SEQ // T, LANES, T), lambda h, i: (h, 0, 0, 0)),
            per_head(DA_HEAD_DIM, T), per_head(T, LANES), per_head(T, LANES), per_head(T, T),
            vec64, vec64, vec64, vec64,
            pl.BlockSpec((1, DA_V_DIM), lambda h, i: (0, 0)),
        ],
        out_specs=pl.BlockSpec((T, LANES), lambda h, i: (i, h)),
        scratch_shapes=[
            pltpu.VMEM((2, 1, T), F32),
            pltpu.VMEM((2, ACC_ROWS, T), F32),
            pltpu.VMEM((2, T, T), F32),
            pltpu.VMEM((2, T, T), F32),
            pltpu.VMEM((2, T, T), F32),
            pltpu.VMEM((2, 1, T), F32),
            pltpu.VMEM((2, 1, T), F32),
            pltpu.VMEM((2, 1, T), F32),
        ],
        compiler_params=_params(("arbitrary", "arbitrary")),
        name="diff_attention",
    )(slopes, qt4, proj, vt4, qa, ka1, ka2, dtab, lq1, lk1, lq2, lk2, subln_g)


def _ret_kernel(cd_ref, q_ref, k_ref, v_ref, g_ref, dec_ref, qdec_ref, kdec_ref, o_ref, st_sc):
    @pl.when(pl.program_id(0) == 0)
    def _():
        st_sc[...] = jnp.zeros_like(st_sc)

    for h in range(RET_HEADS):
        qk = slice(h * RET_QK_DIM, (h + 1) * RET_QK_DIM)
        vv = slice(h * RET_V_DIM, (h + 1) * RET_V_DIM)
        q = q_ref[:, qk]
        k = k_ref[:, qk]
        v = v_ref[:, vv]
        g = g_ref[:, vv].astype(F32)
        s = lax.dot_general(q, k, (((1,), (1,)), ((), ())),
                            preferred_element_type=F32) * dec_ref[h]
        intra = jnp.dot(s.astype(BF16), v, preferred_element_type=F32)
        st = st_sc[h]
        cross = jnp.dot(q, st.astype(BF16), preferred_element_type=F32) * qdec_ref[h]
        kd = (k.astype(F32) * kdec_ref[h]).astype(BF16)
        st_sc[h] = st * cd_ref[h] + lax.dot_general(kd, v, (((0,), (0,)), ((), ())),
                                                    preferred_element_type=F32)
        o = intra + cross
        o = o * lax.rsqrt(jnp.mean(o * o, axis=-1, keepdims=True) + EPS)
        o = (g / (1.0 + jnp.exp(-g))) * o
        o_ref[:, vv] = o.astype(BF16)


def _retention(proj, cd, dec, qdec, kdec):
    C = RET_C
    return pl.pallas_call(
        _ret_kernel,
        out_shape=jax.ShapeDtypeStruct((SEQ, RET_WIDTH), BF16),
        grid=(SEQ // C,),
        in_specs=[
            pl.BlockSpec(memory_space=pltpu.SMEM),
            pl.BlockSpec((C, 256), lambda i: (i, RQ_OFF // 256)),
            pl.BlockSpec((C, 256), lambda i: (i, RK_OFF // 256)),
            pl.BlockSpec((C, 512), lambda i: (i, RV_OFF // 512)),
            pl.BlockSpec((C, 512), lambda i: (i, RG_OFF // 512)),
            pl.BlockSpec((RET_HEADS, C, C), lambda i: (0, 0, 0)),
            pl.BlockSpec((RET_HEADS, C, 1), lambda i: (0, 0, 0)),
            pl.BlockSpec((RET_HEADS, C, 1), lambda i: (0, 0, 0)),
        ],
        out_specs=pl.BlockSpec((C, RET_WIDTH), lambda i: (i, 0)),
        scratch_shapes=[pltpu.VMEM((RET_HEADS, RET_QK_DIM, RET_V_DIM), F32)],
        compiler_params=_params(("arbitrary",)),
        name="retention",
    )(cd, proj, proj, proj, proj, dec, qdec, kdec)


def _outproj_router_kernel(x_ref, oda_ref, or_ref, wo_ref, g_ref, wr_ref, br_ref,
                           h1_ref, xn_ref, ri_ref, rw_ref):
    h1 = (x_ref[...]
          + jnp.dot(oda_ref[...], wo_ref[0:DA_WIDTH, :], preferred_element_type=F32)
          + jnp.dot(or_ref[...], wo_ref[DA_WIDTH:, :], preferred_element_type=F32))
    h1_ref[...] = h1
    var = jnp.mean(h1 * h1, axis=-1, keepdims=True)
    xn = h1 * lax.rsqrt(var + EPS) * g_ref[...]
    for c in range(ROW_TILE):
        xn_ref[pl.ds(c, PROJ_TM, stride=ROW_TILE), :] = xn[:, c * LANES:(c + 1) * LANES]
    x_hi = xn.astype(BF16)
    x_lo = (xn - x_hi.astype(F32)).astype(BF16)
    both = jnp.dot(x_hi, wr_ref[...], preferred_element_type=F32)
    logits = (both[:, :LANES] + both[:, LANES:]
              + jnp.dot(x_lo, wr_ref[:, :LANES], preferred_element_type=F32)) + br_ref[...]
    lane = lax.broadcasted_iota(I32, logits.shape, 1)
    neg = jnp.float32(-jnp.inf)
    big = jnp.int32(1 << 20)
    gl = jnp.where(lane < MOE_GROUPS, logits, neg)
    gmax = jnp.max(gl, axis=1, keepdims=True)
    gidx = jnp.min(jnp.where(gl == gmax, lane, big), axis=1, keepdims=True)
    gsum = jnp.sum(jnp.exp(gl - gmax), axis=1, keepdims=True)
    gp = 1.0 / gsum
    lo = MOE_GROUPS + gidx * MOE_EXPERTS_PER_GROUP
    el = jnp.where((lane >= lo) & (lane < lo + MOE_EXPERTS_PER_GROUP), logits, neg)
    v1 = jnp.max(el, axis=1, keepdims=True)
    i1 = jnp.min(jnp.where(el == v1, lane, big), axis=1, keepdims=True)
    el2 = jnp.where(lane == i1, neg, el)
    v2 = jnp.max(el2, axis=1, keepdims=True)
    i2 = jnp.min(jnp.where(el2 == v2, lane, big), axis=1, keepdims=True)
    t = jnp.exp(v2 - v1)
    w1 = gp / (1.0 + t)
    w2 = gp * t / (1.0 + t)
    ri_ref[...] = jnp.where(lane == 0, i1 - MOE_GROUPS,
                            jnp.where(lane == 1, i2 - MOE_GROUPS, 0))
    rw_ref[...] = jnp.where(lane == 0, w1, jnp.where(lane == 1, w2, 0.0))


def _outproj_router(x2, o_da, o_r, wo_bf, g, wr, br):
    tm = PROJ_TM
    row = lambda w: pl.BlockSpec((tm, w), lambda i: (i, 0))
    full = lambda a, b: pl.BlockSpec((a, b), lambda i: (0, 0))
    return pl.pallas_call(
        _outproj_router_kernel,
        out_shape=(
            jax.ShapeDtypeStruct((SEQ, D_MODEL), F32),
            jax.ShapeDtypeStruct((SEQ * ROW_TILE, LANES), F32),
            jax.ShapeDtypeStruct((SEQ, LANES), I32),
            jax.ShapeDtypeStruct((SEQ, LANES), F32),
        ),
        grid=(SEQ // tm,),
        in_specs=[row(D_MODEL), row(DA_WIDTH), row(RET_WIDTH), full(D_MODEL, D_MODEL),
                  full(1, D_MODEL), full(D_MODEL, 2 * LANES), full(1, LANES)],
        out_specs=(row(D_MODEL), pl.BlockSpec((tm * ROW_TILE, LANES), lambda i: (i, 0)),
                   row(LANES), row(LANES)),
        compiler_params=_params(("arbitrary",)),
        name="outproj_router",
    )(x2, o_da, o_r, wo_bf, g, wr, br)


def _plan_kernel(ri_ref, dest_ref, blk_ref, used_ref):
    TT = PLAN_T
    lane = lax.broadcasted_iota(I32, (TT, LANES), 1)

    def onehots(t):
        r = ri_ref[pl.ds(pl.multiple_of(t * TT, TT), TT), :]
        return lane == r[:, 0:1], lane == r[:, 1:2]

    def count_body(t, acc):
        oh1, oh2 = onehots(t)
        return acc + jnp.sum((oh1 | oh2).astype(F32), axis=0, keepdims=True)

    counts = lax.fori_loop(0, SEQ // TT, count_body, jnp.zeros((1, LANES), F32))
    counts8 = jnp.broadcast_to(counts, (8, LANES)).astype(I32)
    shift = FFN_B.bit_length() - 1
    padded = ((counts8 + (FFN_B - 1)) >> shift) << shift
    lane8 = lax.broadcasted_iota(I32, (8, LANES), 1)
    pad_end = padded
    sh = 1
    while sh < LANES:
        pad_end = pad_end + jnp.where(lane8 >= sh, pltpu.roll(pad_end, sh, axis=1), 0)
        sh *= 2
    pad_start = pad_end - padded

    ltri = (lax.broadcasted_iota(I32, (TT, TT), 0)
            > lax.broadcasted_iota(I32, (TT, TT), 1)).astype(BF16)

    def dest_body(t, carry):
        oh1, oh2 = onehots(t)
        a = (oh1 | oh2).astype(F32)
        base = jnp.dot(ltri, a.astype(BF16), preferred_element_type=F32) + carry
        d1 = jnp.sum(jnp.where(oh1, base, 0.0), axis=1, keepdims=True)
        d2 = jnp.sum(jnp.where(oh2, base, 0.0), axis=1, keepdims=True)
        dest_ref[pl.ds(pl.multiple_of(t * TT, TT), TT), :] = jnp.where(
            lane == 0, d1, jnp.where(lane == 1, d2, 0.0)).astype(I32)
        return carry + jnp.sum(a, axis=0, keepdims=True)

    lax.fori_loop(0, SEQ // TT, dest_body, pad_start[0:1].astype(F32))

    rows = lax.broadcasted_iota(I32, (N_BLOCKS_PAD, LANES), 0) * FFN_B
    lanes = lax.broadcasted_iota(I32, (N_BLOCKS_PAD, LANES), 1)
    ended = (jnp.broadcast_to(pad_end[0:1], (N_BLOCKS_PAD, LANES)) <= rows) & (lanes < MOE_EXPERTS)
    be = jnp.sum(ended.astype(I32), axis=1, keepdims=True)
    blk_ref[...] = jnp.broadcast_to(jnp.minimum(be, MOE_EXPERTS - 1), (N_BLOCKS_PAD, LANES))
    total = jnp.max(pad_end, axis=1, keepdims=True)
    row8 = lax.broadcasted_iota(I32, (8, LANES), 0)
    used_ref[...] = jnp.where(row8 == 0, jnp.broadcast_to(total >> shift, (8, LANES)),
                              jnp.where(row8 == 1, counts8, pad_start))


def _plan(ri):
    return pl.pallas_call(
        _plan_kernel,
        out_shape=(
            jax.ShapeDtypeStruct((SEQ, LANES), I32),
            jax.ShapeDtypeStruct((N_BLOCKS_PAD, LANES), I32),
            jax.ShapeDtypeStruct((8, LANES), I32),
        ),
        compiler_params=pltpu.CompilerParams(vmem_limit_bytes=VMEM_LIMIT),
        name="route_plan",
    )(ri)


PAD_BITS = FFN_B.bit_length() - 1


def _pad_fill_copies(e, cnt_ref, pst_ref, zero_sc, xs_hbm, zsem):
    cnt = cnt_ref[e]
    pad = (-cnt) & (FFN_B - 1)
    row = pst_ref[e] + cnt
    out = []
    for bit in reversed(range(PAD_BITS)):
        n = 1 << bit
        start = row + ((pad >> (bit + 1)) << (bit + 1))
        copy = pltpu.make_async_copy(
            zero_sc.at[pl.ds(0, n * ROW_TILE)],
            xs_hbm.at[pl.ds(pl.multiple_of(start * ROW_TILE, ROW_TILE), n * ROW_TILE)], zsem)
        out.append(((pad & n) != 0, copy))
    return out


def _unused_block_copies(b, zero_sc, xs_hbm, zsem):
    half = FFN_B // 2 * ROW_TILE
    return [pltpu.make_async_copy(
        zero_sc, xs_hbm.at[pl.ds(pl.multiple_of((2 * b + k) * half, half), half)], zsem)
        for k in range(2)]


def _dispatch_kernel(dest_ref, cnt_ref, pst_ref, used_ref, xn_ref, xs_hbm, zero_sc, sem, zsem):
    tm = DISP_TM
    i = pl.program_id(0)

    @pl.when(i == 0)
    def _():
        zero_sc[...] = jnp.zeros_like(zero_sc)

        def fill(e, carry):
            for cond, copy in _pad_fill_copies(e, cnt_ref, pst_ref, zero_sc, xs_hbm, zsem):
                pl.when(cond)(copy.start)
            return carry

        lax.fori_loop(0, MOE_EXPERTS, fill, 0)

        def fill_block(b, carry):
            for copy in _unused_block_copies(b, zero_sc, xs_hbm, zsem):
                copy.start()
            return carry

        lax.fori_loop(used_ref[0], N_BLOCKS, fill_block, 0)

    def issue(it, carry):
        for u in range(DMA_UNROLL):
            r = it * DMA_UNROLL + u
            a = (i * tm + r) * 2
            src = xn_ref.at[pl.ds(pl.multiple_of(r * ROW_TILE, ROW_TILE), ROW_TILE)]
            for kk in range(2):
                d = pl.multiple_of(dest_ref[a + kk] * ROW_TILE, ROW_TILE)
                pltpu.make_async_copy(src, xs_hbm.at[pl.ds(d, ROW_TILE)], sem).start()
        return carry

    lax.fori_loop(0, tm // DMA_UNROLL, issue, 0)
    for _ in range(2):
        pltpu.make_async_copy(xn_ref, xs_hbm.at[pl.ds(0, tm * ROW_TILE)], sem).wait()

    @pl.when(i == 0)
    def _():
        def drain(e, carry):
            for cond, copy in _pad_fill_copies(e, cnt_ref, pst_ref, zero_sc, xs_hbm, zsem):
                pl.when(cond)(copy.wait)
            return carry

        lax.fori_loop(0, MOE_EXPERTS, drain, 0)

        def drain_block(b, carry):
            for copy in _unused_block_copies(b, zero_sc, xs_hbm, zsem):
                copy.wait()
            return carry

        lax.fori_loop(used_ref[0], N_BLOCKS, drain_block, 0)


def _dispatch(dest_flat, counts, pad_start, used, xn3):
    tm = DISP_TM
    return pl.pallas_call(
        _dispatch_kernel,
        out_shape=jax.ShapeDtypeStruct((N_BUF * ROW_TILE, LANES), F32),
        grid_spec=pltpu.PrefetchScalarGridSpec(
            num_scalar_prefetch=4,
            grid=(SEQ // tm,),
            in_specs=[pl.BlockSpec((tm * ROW_TILE, LANES), lambda i, d, c, p, u: (i, 0))],
            out_specs=pl.BlockSpec(memory_space=pl.ANY),
            scratch_shapes=[
                pltpu.VMEM((FFN_B // 2 * ROW_TILE, LANES), F32),
                pltpu.SemaphoreType.DMA(()),
                pltpu.SemaphoreType.DMA(()),
            ],
        ),
        compiler_params=_params(("arbitrary",)),
        name="moe_dispatch",
    )(dest_flat, counts, pad_start, used, xn3)


def _ffn_kernel(blk_ref, used_ref, xs_ref, wg_ref, wu_ref, wd_ref, y_ref, wg_bf, wu_bf, wd_bf):
    B = FFN_B
    b = pl.program_id(0)

    @pl.when(b < used_ref[0])
    def _():
        prev = blk_ref[jnp.maximum(b - 1, 0)]

        @pl.when((b == 0) | (blk_ref[b] != prev))
        def _():
            wg_bf[...] = wg_ref[0].astype(BF16)
            wu_bf[...] = wu_ref[0].astype(BF16)
            wd_bf[...] = wd_ref[0].astype(BF16)

        x = jnp.concatenate([xs_ref[pl.ds(c, B, stride=ROW_TILE), :] for c in range(ROW_TILE)],
                            axis=1).astype(BF16)
        hg = jnp.dot(x, wg_bf[...], preferred_element_type=F32)
        hu = jnp.dot(x, wu_bf[...], preferred_element_type=F32)
        hh = ((hg / (1.0 + jnp.exp(-hg))) * hu).astype(BF16)
        y = jnp.dot(hh, wd_bf[...], preferred_element_type=F32)
        for c in range(ROW_TILE):
            y_ref[pl.ds(c, B, stride=ROW_TILE), :] = y[:, c * LANES:(c + 1) * LANES]

    @pl.when(b >= used_ref[0])
    def _():
        y_ref[...] = jnp.zeros_like(y_ref)


def _ffn(blk_e, used, xs, w_gate, w_up, w_down):
    B = FFN_B
    wspec = lambda a, c: pl.BlockSpec((1, a, c), lambda b, blk, used: (blk[b], 0, 0))
    return pl.pallas_call(
        _ffn_kernel,
        out_shape=jax.ShapeDtypeStruct((N_BUF * ROW_TILE, LANES), F32),
        grid_spec=pltpu.PrefetchScalarGridSpec(
            num_scalar_prefetch=2,
            grid=(N_BLOCKS,),
            in_specs=[
                pl.BlockSpec((B * ROW_TILE, LANES),
                             lambda b, blk, used: (jnp.minimum(b, used[0] - 1), 0)),
                wspec(D_MODEL, MOE_HIDDEN),
                wspec(D_MODEL, MOE_HIDDEN),
                wspec(MOE_HIDDEN, D_MODEL),
            ],
            out_specs=pl.BlockSpec((B * ROW_TILE, LANES), lambda b, blk, used: (b, 0)),
            scratch_shapes=[
                pltpu.VMEM((D_MODEL, MOE_HIDDEN), BF16),
                pltpu.VMEM((D_MODEL, MOE_HIDDEN), BF16),
                pltpu.VMEM((MOE_HIDDEN, D_MODEL), BF16),
            ],
        ),
        compiler_params=_params(("arbitrary",)),
        name="expert_ffn",
    )(blk_e, used, xs, w_gate, w_up, w_down)


def _combine_kernel(dest_ref, h1_ref, rw_ref, g_ref, y_hbm, o_ref, ybuf, sem):
    tm = COMB_TM
    i = pl.program_id(0)

    def gather(tile, slot):
        def issue(it, carry):
            for u in range(DMA_UNROLL):
                r = it * DMA_UNROLL + u
                a = (tile * tm + r) * 2
                for kk in range(2):
                    d = pl.multiple_of(dest_ref[a + kk] * ROW_TILE, ROW_TILE)
                    pltpu.make_async_copy(
                        y_hbm.at[pl.ds(d, ROW_TILE)],
                        ybuf.at[slot, kk, pl.ds(pl.multiple_of(r * ROW_TILE, ROW_TILE), ROW_TILE)],
                        sem.at[slot, kk]).start()
            return carry

        lax.fori_loop(0, tm // DMA_UNROLL, issue, 0)

    @pl.when(i == 0)
    def _():
        gather(0, 0)

    @pl.when(i + 1 < pl.num_programs(0))
    def _():
        gather(i + 1, (i + 1) % 2)

    slot = i % 2
    for kk in range(2):
        pltpu.make_async_copy(y_hbm.at[pl.ds(0, tm * ROW_TILE)], ybuf.at[slot, kk],
                              sem.at[slot, kk]).wait()
    w = rw_ref[...]
    ys = [jnp.concatenate([ybuf[slot, kk, pl.ds(c, tm, stride=ROW_TILE), :]
                           for c in range(ROW_TILE)], axis=1) for kk in range(2)]
    h = h1_ref[...] + w[:, 0:1] * ys[0] + w[:, 1:2] * ys[1]
    var = jnp.mean(h * h, axis=-1, keepdims=True)
    o_ref[...] = h * lax.rsqrt(var + EPS) * g_ref[...]


def _combine(dest_flat, h1, rw, g, y):
    tm = COMB_TM
    return pl.pallas_call(
        _combine_kernel,
        out_shape=jax.ShapeDtypeStruct((SEQ, D_MODEL), F32),
        grid_spec=pltpu.PrefetchScalarGridSpec(
            num_scalar_prefetch=1,
            grid=(SEQ // tm,),
            in_specs=[
                pl.BlockSpec((tm, D_MODEL), lambda i, d: (i, 0)),
                pl.BlockSpec((tm, LANES), lambda i, d: (i, 0)),
                pl.BlockSpec((1, D_MODEL), lambda i, d: (0, 0)),
                pl.BlockSpec(memory_space=pl.ANY),
            ],
            out_specs=pl.BlockSpec((tm, D_MODEL), lambda i, d: (i, 0)),
            scratch_shapes=[
                pltpu.VMEM((2, 2, tm * ROW_TILE, LANES), F32),
                pltpu.SemaphoreType.DMA((2, 2)),
            ],
        ),
        compiler_params=_params(("arbitrary",)),
        name="moe_combine",
    )(dest_flat, h1, rw, g, y)


def _attention_tables():
    T = ATT_T
    slopes = jnp.exp2(-ALIBI_MAX * jnp.arange(1, DA_HEADS + 1, dtype=F32) / DA_HEADS)
    r = jnp.arange(T)
    hi = ((r // CHUNK) * CHUNK).astype(F32)
    lo = (r % CHUNK).astype(F32)
    ones = jnp.ones((T,), F32)
    sl = slopes[:, None]
    one_h = jnp.broadcast_to(ones, (DA_HEADS, T))
    q_rows = jnp.stack([one_h, one_h, -sl * hi[None], -sl * lo[None]], axis=1)
    k_cols = jnp.stack([sl * hi[None], sl * lo[None], one_h, one_h], axis=-1)
    qa = jnp.zeros((DA_HEADS, DA_HEAD_DIM, T), F32).at[:, 0:4, :].set(q_rows)
    ka1 = jnp.zeros((DA_HEADS, T, LANES), F32).at[:, :, DA_HEAD_DIM:DA_HEAD_DIM + 4].set(k_cols)
    ka2 = jnp.zeros((DA_HEADS, T, LANES), F32).at[:, :, 0:4].set(k_cols)
    rel = (r[:, None] - r[None, :]).astype(F32)
    allowed = (r[:, None] // CHUNK) <= (r[None, :] // CHUNK)
    fix = jnp.where(rel > 0, -2.0 * slopes[:, None, None] * rel[None], 0.0)
    dtab = jnp.where(allowed[None], fix, -jnp.inf)
    return slopes, qa.astype(BF16), ka1.astype(BF16), ka2.astype(BF16), dtab


def _retention_tables():
    C = RET_C
    log_gamma = jnp.log1p(-jnp.exp2(-5.0 - jnp.arange(RET_HEADS, dtype=F32)))
    pos = jnp.arange(C, dtype=F32)
    rel = pos[:, None] - pos[None, :]
    dec = jnp.where(rel >= 0, jnp.exp(log_gamma[:, None, None] * jnp.maximum(rel, 0.0)), 0.0)
    qdec = jnp.exp(log_gamma[:, None] * (pos + 1.0)[None, :])[:, :, None]
    kdec = jnp.exp(log_gamma[:, None] * (C - 1 - pos)[None, :])[:, :, None]
    cd = jnp.exp(log_gamma * C)
    return cd, dec, qdec, kdec


def kernel(x, attn_norm_g, w_in, da_lambda_q1, da_lambda_k1, da_lambda_q2, da_lambda_k2,
           da_subln_g, w_out, ffn_norm_g, router_group_w, router_group_b, router_expert_w,
           router_expert_b, expert_w_gate, expert_w_up, expert_w_down, final_norm_g):
    B, S, D = x.shape
    assert (B, S, D) == (1, SEQ, D_MODEL)
    x2 = x.reshape(S, D)

    w = w_in[0]
    w_main = jnp.concatenate([
        w[:, 512:1024],
        w[:, 1536:1792],
        w[:, 1792:2048] * (RET_QK_DIM ** -0.5),
        w[:, 2048:3072]], axis=1).astype(BF16)
    wq_t = (w[:, 0:512] * (DA_HEAD_DIM ** -0.5)).T.astype(BF16)
    wv_t = w[:, 1024:1536].T.astype(BF16)
    proj, qt4, vt4 = _inproj(x2, attn_norm_g[0][None, :], w_main, wq_t, wv_t)

    slopes, qa, ka1, ka2, dtab = _attention_tables()
    o_da = _attention(proj, qt4, vt4, slopes, qa, ka1, ka2, dtab, da_lambda_q1, da_lambda_k1,
                      da_lambda_q2, da_lambda_k2, da_subln_g)
    cd, dec, qdec, kdec = _retention_tables()
    o_r = _retention(proj, cd, dec, qdec, kdec)

    wr = jnp.zeros((D, LANES), F32)
    wr = wr.at[:, :MOE_GROUPS].set(router_group_w[0])
    wr = wr.at[:, MOE_GROUPS:MOE_GROUPS + MOE_EXPERTS].set(router_expert_w[0])
    br = jnp.zeros((1, LANES), F32)
    br = br.at[0, :MOE_GROUPS].set(router_group_b[0])
    br = br.at[0, MOE_GROUPS:MOE_GROUPS + MOE_EXPERTS].set(router_expert_b[0])
    wr_hi = wr.astype(BF16)
    wr_lo = (wr - wr_hi.astype(F32)).astype(BF16)
    h1, xn, ri, rw = _outproj_router(x2, o_da, o_r, w_out[0].astype(BF16), ffn_norm_g[0][None, :],
                                     jnp.concatenate([wr_hi, wr_lo], axis=1), br)

    dest, blk, used = _plan(ri)
    dest_flat = dest[:, :2].reshape(N_ASSIGN)
    used1 = used[0, :1]
    blk_e = blk[:N_BLOCKS, 0]
    blk_e = blk_e[jnp.minimum(jnp.arange(N_BLOCKS, dtype=I32), used1[0] - 1)]
    xs = _dispatch(dest_flat, used[1, :MOE_EXPERTS], used[2, :MOE_EXPERTS], used1, xn)

    y = _ffn(blk_e, used1, xs, expert_w_gate[0], expert_w_up[0], expert_w_down[0])
    out = _combine(dest_flat, h1, rw, final_norm_g[None, :], y)
    return out.reshape(B, S, D)
```

```python
import functools
import math

import jax
import jax.numpy as jnp
import numpy as np
from jax import lax
from jax.experimental import pallas as pl
from jax.experimental.pallas import tpu as pltpu

F32 = jnp.float32
BF16 = jnp.bfloat16
I32 = jnp.int32

D_MODEL = 1024
SEQ = 16384
CHUNK = 64
EPS = 1e-6

DA_HEADS = 4
DA_HEAD_DIM = 64
DA_V_DIM = 128
DA_WIDTH = 512
ALIBI_MAX = 8.0
RET_HEADS = 4
RET_QK_DIM = 64
RET_V_DIM = 128
RET_WIDTH = 512
T_ROWS = 512
MAIN_COLS = 2048
DK_OFF = 0
RQ_OFF = 512
RK_OFF = 768
RV_OFF = 1024
RG_OFF = 1536

MOE_GROUPS = 4
MOE_EXPERTS_PER_GROUP = 8
MOE_EXPERTS = 32
MOE_HIDDEN = 512
LAMBDA_INIT = 0.8 - 0.6 * math.exp(-0.3 * 0)

LANES = 128
ROW_TILE = 8
VMEM_LIMIT = 56 * 1024 * 1024

PROJ_TM = 512
ATT_T = 512
RET_C = 256
PLAN_T = 512
FFN_B = 256
N_ASSIGN = 2 * SEQ
N_BLOCKS = N_ASSIGN // FFN_B + MOE_EXPERTS
N_BLOCKS_PAD = (N_BLOCKS + 7) // 8 * 8
N_BUF = N_BLOCKS * FFN_B
COMB_TM = 256
DISP_TM = 512
DMA_UNROLL = 8


def _params(sem):
    return pltpu.CompilerParams(dimension_semantics=sem, vmem_limit_bytes=VMEM_LIMIT)


def _inproj_kernel(x_ref, g_ref, w_ref, wq_ref, wv_ref, o_ref, qt_ref, vt_ref):
    x = x_ref[...]
    var = jnp.mean(x * x, axis=-1, keepdims=True)
    xn = (x * lax.rsqrt(var + EPS) * g_ref[...]).astype(BF16)
    for c in range(MAIN_COLS // 512):
        sl = slice(c * 512, (c + 1) * 512)
        o_ref[:, sl] = jnp.dot(xn, w_ref[:, sl], preferred_element_type=F32).astype(BF16)
    nt = (((1,), (1,)), ((), ()))
    qt = lax.dot_general(wq_ref[...], xn, nt, preferred_element_type=F32)
    qt_ref[...] = qt.astype(BF16).reshape(DA_HEADS, 1, 2 * DA_HEAD_DIM, PROJ_TM)
    vt = lax.dot_general(wv_ref[...], xn, nt, preferred_element_type=F32)
    vt_ref[...] = vt.astype(BF16).reshape(DA_HEADS, 1, DA_V_DIM, PROJ_TM)


def _inproj(x2, g, w_bf, wq_t, wv_t):
    t_shape = jax.ShapeDtypeStruct((DA_HEADS, SEQ // PROJ_TM, LANES, PROJ_TM), BF16)
    t_spec = pl.BlockSpec((DA_HEADS, 1, LANES, PROJ_TM), lambda i: (0, i, 0, 0))
    w_t_spec = pl.BlockSpec((T_ROWS, D_MODEL), lambda i: (0, 0))
    return pl.pallas_call(
        _inproj_kernel,
        out_shape=(jax.ShapeDtypeStruct((SEQ, MAIN_COLS), BF16), t_shape, t_shape),
        grid=(SEQ // PROJ_TM,),
        in_specs=[
            pl.BlockSpec((PROJ_TM, D_MODEL), lambda i: (i, 0)),
            pl.BlockSpec((1, D_MODEL), lambda i: (0, 0)),
            pl.BlockSpec((D_MODEL, MAIN_COLS), lambda i: (0, 0)),
            w_t_spec, w_t_spec,
        ],
        out_specs=(pl.BlockSpec((PROJ_TM, MAIN_COLS), lambda i: (i, 0)), t_spec, t_spec),
        compiler_params=_params(("arbitrary",)),
        name="inproj",
    )(x2, g, w_bf, wq_t, wv_t)


ACC_ROWS = DA_V_DIM + 16


N_QT = SEQ // ATT_T
N_OFF = N_QT * (N_QT - 1) // 2


def _pipeline3(n_pos, scores, accumulate):
    scores(0, 0)
    scores(1, 1)
    steady = n_pos - 2

    def triple(k, carry):
        t = 3 * k
        accumulate(t, 0)
        scores(t + 2, 2)
        accumulate(t + 1, 1)
        scores(t + 3, 0)
        accumulate(t + 2, 2)
        scores(t + 4, 1)
        return carry

    lax.fori_loop(0, steady // 3, triple, 0)
    t0 = steady // 3 * 3
    rem = steady - t0
    accumulate(t0, 0)
    if rem >= 1:
        scores(t0 + 2, 2)
    accumulate(t0 + 1, 1)
    if rem == 2:
        scores(t0 + 3, 0)
    if rem >= 1:
        accumulate(t0 + 2, 2)
    if rem == 2:
        accumulate(t0 + 3, 0)


def _attn_kernel(slope_ref, jt_ref, it_ref, qt_ref, k_ref, vt_ref, qa_ref, ka1_ref, ka2_ref,
                 dtab_ref, lq1_ref, lk1_ref, lq2_ref, lk2_ref, g_ref, o_ref,
                 m_sc, acc_sc, s0_sc, s1_sc, s2_sc, mx0_sc, mx1_sc, mx2_sc):
    T = ATT_T
    h = pl.program_id(0)
    slope = slope_ref[h]
    qa = qa_ref[0]
    lane = lax.broadcasted_iota(I32, (T, LANES), 1)
    sums_row = (lax.broadcasted_iota(I32, (16, T), 0) == 0).astype(BF16)
    s_bufs = (s0_sc, s1_sc, s2_sc)
    mx_bufs = (mx0_sc, mx1_sc, mx2_sc)

    def scores(j, i, buf, extra):
        kt = k_ref[pl.ds(pl.multiple_of(j * T, T), T), :]
        ks = (jnp.where(lane < DA_HEAD_DIM, kt, ka1_ref[0]),
              jnp.where(lane >= DA_HEAD_DIM, kt, ka2_ref[0]))
        qt = qt_ref[0, i]
        qw = (jnp.concatenate([qt[0:DA_HEAD_DIM], qa], axis=0),
              jnp.concatenate([qa, qt[DA_HEAD_DIM:]], axis=0))
        for mp in range(2):
            s = jnp.dot(ks[mp], qw[mp], preferred_element_type=F32)
            if extra is not None:
                s = s + extra[0]
            s_bufs[buf][mp] = s
            mx_bufs[buf][mp] = jnp.max(s, axis=0, keepdims=True)

    def accumulate(j, i, buf):
        c = slope * lax.convert_element_type((i - j) * T, F32)
        vte = jnp.concatenate([vt_ref[0, j], sums_row], axis=0)
        for mp in range(2):
            m_prev = m_sc[i, mp]
            m_new = jnp.maximum(m_prev, mx_bufs[buf][mp] - c)
            p = jnp.exp(s_bufs[buf][mp] - (m_new + c)).astype(BF16)
            pv = jnp.dot(vte, p, preferred_element_type=F32)
            acc_sc[i, mp] = jnp.exp(m_prev - m_new) * acc_sc[i, mp] + pv
            m_sc[i, mp] = m_new

    m_sc[...] = jnp.full_like(m_sc, -jnp.inf)
    acc_sc[...] = jnp.zeros_like(acc_sc)
    _pipeline3(N_QT,
               lambda pos, buf: scores(pos, pos, buf, dtab_ref),
               lambda pos, buf: accumulate(pos, pos, buf))
    _pipeline3(N_OFF,
               lambda pos, buf: scores(jt_ref[pos], it_ref[pos], buf, None),
               lambda pos, buf: accumulate(jt_ref[pos], it_ref[pos], buf))

    lam = (jnp.exp(jnp.sum(lq1_ref[...] * lk1_ref[...], axis=1, keepdims=True))
           - jnp.exp(jnp.sum(lq2_ref[...] * lk2_ref[...], axis=1, keepdims=True))
           + LAMBDA_INIT)

    def finish(i, carry):
        a1 = acc_sc[i, 0]
        a2 = acc_sc[i, 1]
        ot = (a1[0:DA_V_DIM] / a1[DA_V_DIM:DA_V_DIM + 1]
              - lam * (a2[0:DA_V_DIM] / a2[DA_V_DIM:DA_V_DIM + 1]))
        o = ot.T
        var = jnp.mean(o * o, axis=-1, keepdims=True)
        o = (o * lax.rsqrt(var + EPS) * g_ref[...]) * (1.0 - LAMBDA_INIT)
        o_ref[pl.ds(pl.multiple_of(i * T, T), T), :] = o.astype(BF16)
        return carry

    lax.fori_loop(0, N_QT, finish, 0)


def _attention(proj, qt4, vt4, slopes, qa, ka1, ka2, dtab, lq1, lk1, lq2, lk2, subln_g):
    T = ATT_T
    vec64 = pl.BlockSpec((1, DA_HEAD_DIM), lambda h: (0, 0))
    per_head = lambda a, b: pl.BlockSpec((1, a, b), lambda h: (h, 0, 0))
    slab = lambda shape, imap: pl.BlockSpec(shape, imap, pipeline_mode=pl.Buffered(1))
    smem = pl.BlockSpec(memory_space=pltpu.SMEM)
    it_tab, jt_tab = np.tril_indices(N_QT, -1)
    return pl.pallas_call(
        _attn_kernel,
        out_shape=jax.ShapeDtypeStruct((SEQ, DA_WIDTH), BF16),
        grid=(DA_HEADS,),
        in_specs=[
            smem, smem, smem,
            slab((1, N_QT, LANES, T), lambda h: (h, 0, 0, 0)),
            slab((SEQ, LANES), lambda h: (0, DK_OFF // LANES + h)),
            slab((1, N_QT, LANES, T), lambda h: (h, 0, 0, 0)),
            per_head(DA_HEAD_DIM, T), per_head(T, LANES), per_head(T, LANES), per_head(T, T),
            vec64, vec64, vec64, vec64,
            pl.BlockSpec((1, DA_V_DIM), lambda h: (0, 0)),
        ],
        out_specs=slab((SEQ, LANES), lambda h: (0, h)),
        scratch_shapes=[
            pltpu.VMEM((N_QT, 2, 1, T), F32),
            pltpu.VMEM((N_QT, 2, ACC_ROWS, T), F32),
            pltpu.VMEM((2, T, T), F32),
            pltpu.VMEM((2, T, T), F32),
            pltpu.VMEM((2, T, T), F32),
            pltpu.VMEM((2, 1, T), F32),
            pltpu.VMEM((2, 1, T), F32),
            pltpu.VMEM((2, 1, T), F32),
        ],
        compiler_params=_params(("arbitrary",)),
        name="diff_attention",
    )(slopes, jnp.asarray(jt_tab, I32), jnp.asarray(it_tab, I32), qt4, proj, vt4, qa, ka1, ka2,
      dtab, lq1, lk1, lq2, lk2, subln_g)


def _ret_kernel(cd_ref, q_ref, k_ref, v_ref, g_ref, dec_ref, qdec_ref, kdec_ref, o_ref, st_sc):
    @pl.when(pl.program_id(0) == 0)
    def _():
        st_sc[...] = jnp.zeros_like(st_sc)

    for h in range(RET_HEADS):
        qk = slice(h * RET_QK_DIM, (h + 1) * RET_QK_DIM)
        vv = slice(h * RET_V_DIM, (h + 1) * RET_V_DIM)
        q = q_ref[:, qk]
        k = k_ref[:, qk]
        v = v_ref[:, vv]
        g = g_ref[:, vv].astype(F32)
        s = lax.dot_general(q, k, (((1,), (1,)), ((), ())),
                            preferred_element_type=F32) * dec_ref[h]
        intra = jnp.dot(s.astype(BF16), v, preferred_element_type=F32)
        st = st_sc[h]
        cross = jnp.dot(q, st.astype(BF16), preferred_element_type=F32) * qdec_ref[h]
        kd = (k.astype(F32) * kdec_ref[h]).astype(BF16)
        st_sc[h] = st * cd_ref[h] + lax.dot_general(kd, v, (((0,), (0,)), ((), ())),
                                                    preferred_element_type=F32)
        o = intra + cross
        o = o * lax.rsqrt(jnp.mean(o * o, axis=-1, keepdims=True) + EPS)
        o = (g / (1.0 + jnp.exp(-g))) * o
        o_ref[:, vv] = o.astype(BF16)


def _retention(proj, cd, dec, qdec, kdec):
    C = RET_C
    return pl.pallas_call(
        _ret_kernel,
        out_shape=jax.ShapeDtypeStruct((SEQ, RET_WIDTH), BF16),
        grid=(SEQ // C,),
        in_specs=[
            pl.BlockSpec(memory_space=pltpu.SMEM),
            pl.BlockSpec((C, 256), lambda i: (i, RQ_OFF // 256)),
            pl.BlockSpec((C, 256), lambda i: (i, RK_OFF // 256)),
            pl.BlockSpec((C, 512), lambda i: (i, RV_OFF // 512)),
            pl.BlockSpec((C, 512), lambda i: (i, RG_OFF // 512)),
            pl.BlockSpec((RET_HEADS, C, C), lambda i: (0, 0, 0)),
            pl.BlockSpec((RET_HEADS, C, 1), lambda i: (0, 0, 0)),
            pl.BlockSpec((RET_HEADS, C, 1), lambda i: (0, 0, 0)),
        ],
        out_specs=pl.BlockSpec((C, RET_WIDTH), lambda i: (i, 0)),
        scratch_shapes=[pltpu.VMEM((RET_HEADS, RET_QK_DIM, RET_V_DIM), F32)],
        compiler_params=_params(("arbitrary",)),
        name="retention",
    )(cd, proj, proj, proj, proj, dec, qdec, kdec)


def _outproj_router_kernel(x_ref, oda_ref, or_ref, wo_ref, g_ref, wr_ref, br_ref,
                           h1_ref, xn_ref, ri_ref, rw_ref):
    h1 = (x_ref[...]
          + jnp.dot(oda_ref[...], wo_ref[0:DA_WIDTH, :], preferred_element_type=F32)
          + jnp.dot(or_ref[...], wo_ref[DA_WIDTH:, :], preferred_element_type=F32))
    h1_ref[...] = h1
    var = jnp.mean(h1 * h1, axis=-1, keepdims=True)
    xn = h1 * lax.rsqrt(var + EPS) * g_ref[...]
    for c in range(ROW_TILE):
        xn_ref[pl.ds(c, PROJ_TM, stride=ROW_TILE), :] = xn[:, c * LANES:(c + 1) * LANES]
    x_hi = xn.astype(BF16)
    x_lo = (xn - x_hi.astype(F32)).astype(BF16)
    both = jnp.dot(x_hi, wr_ref[...], preferred_element_type=F32)
    logits = (both[:, :LANES] + both[:, LANES:]
              + jnp.dot(x_lo, wr_ref[:, :LANES], preferred_element_type=F32)) + br_ref[...]
    lane = lax.broadcasted_iota(I32, logits.shape, 1)
    neg = jnp.float32(-jnp.inf)
    big = jnp.int32(1 << 20)
    gl = jnp.where(lane < MOE_GROUPS, logits, neg)
    gmax = jnp.max(gl, axis=1, keepdims=True)
    gidx = jnp.min(jnp.where(gl == gmax, lane, big), axis=1, keepdims=True)
    gsum = jnp.sum(jnp.exp(gl - gmax), axis=1, keepdims=True)
    gp = 1.0 / gsum
    lo = MOE_GROUPS + gidx * MOE_EXPERTS_PER_GROUP
    el = jnp.where((lane >= lo) & (lane < lo + MOE_EXPERTS_PER_GROUP), logits, neg)
    v1 = jnp.max(el, axis=1, keepdims=True)
    i1 = jnp.min(jnp.where(el == v1, lane, big), axis=1, keepdims=True)
    el2 = jnp.where(lane == i1, neg, el)
    v2 = jnp.max(el2, axis=1, keepdims=True)
    i2 = jnp.min(jnp.where(el2 == v2, lane, big), axis=1, keepdims=True)
    t = jnp.exp(v2 - v1)
    w1 = gp / (1.0 + t)
    w2 = gp * t / (1.0 + t)
    ri_ref[...] = jnp.where(lane == 0, i1 - MOE_GROUPS,
                            jnp.where(lane == 1, i2 - MOE_GROUPS, 0))
    rw_ref[...] = jnp.where(lane == 0, w1, jnp.where(lane == 1, w2, 0.0))


def _outproj_router(x2, o_da, o_r, wo_bf, g, wr, br):
    tm = PROJ_TM
    row = lambda w: pl.BlockSpec((tm, w), lambda i: (i, 0))
    full = lambda a, b: pl.BlockSpec((a, b), lambda i: (0, 0))
    return pl.pallas_call(
        _outproj_router_kernel,
        out_shape=(
            jax.ShapeDtypeStruct((SEQ, D_MODEL), F32),
            jax.ShapeDtypeStruct((SEQ * ROW_TILE, LANES), F32),
            jax.ShapeDtypeStruct((SEQ, LANES), I32),
            jax.ShapeDtypeStruct((SEQ, LANES), F32),
        ),
        grid=(SEQ // tm,),
        in_specs=[row(D_MODEL), row(DA_WIDTH), row(RET_WIDTH), full(D_MODEL, D_MODEL),
                  full(1, D_MODEL), full(D_MODEL, 2 * LANES), full(1, LANES)],
        out_specs=(row(D_MODEL), pl.BlockSpec((tm * ROW_TILE, LANES), lambda i: (i, 0)),
                   row(LANES), row(LANES)),
        compiler_params=_params(("arbitrary",)),
        name="outproj_router",
    )(x2, o_da, o_r, wo_bf, g, wr, br)


def _plan_kernel(ri_ref, dest_ref, blk_ref, used_ref):
    TT = PLAN_T
    lane = lax.broadcasted_iota(I32, (TT, LANES), 1)

    def onehots(t):
        r = ri_ref[pl.ds(pl.multiple_of(t * TT, TT), TT), :]
        return lane == r[:, 0:1], lane == r[:, 1:2]

    def count_body(t, acc):
        oh1, oh2 = onehots(t)
        return acc + jnp.sum((oh1 | oh2).astype(F32), axis=0, keepdims=True)

    counts = lax.fori_loop(0, SEQ // TT, count_body, jnp.zeros((1, LANES), F32))
    counts8 = jnp.broadcast_to(counts, (8, LANES)).astype(I32)
    shift = FFN_B.bit_length() - 1
    padded = ((counts8 + (FFN_B - 1)) >> shift) << shift
    lane8 = lax.broadcasted_iota(I32, (8, LANES), 1)
    pad_end = padded
    sh = 1
    while sh < LANES:
        pad_end = pad_end + jnp.where(lane8 >= sh, pltpu.roll(pad_end, sh, axis=1), 0)
        sh *= 2
    pad_start = pad_end - padded

    ltri = (lax.broadcasted_iota(I32, (TT, TT), 0)
            > lax.broadcasted_iota(I32, (TT, TT), 1)).astype(BF16)

    def dest_body(t, carry):
        oh1, oh2 = onehots(t)
        a = (oh1 | oh2).astype(F32)
        base = jnp.dot(ltri, a.astype(BF16), preferred_element_type=F32) + carry
        d1 = jnp.sum(jnp.where(oh1, base, 0.0), axis=1, keepdims=True)
        d2 = jnp.sum(jnp.where(oh2, base, 0.0), axis=1, keepdims=True)
        dest_ref[pl.ds(pl.multiple_of(t * TT, TT), TT), :] = jnp.where(
            lane == 0, d1, jnp.where(lane == 1, d2, 0.0)).astype(I32)
        return carry + jnp.sum(a, axis=0, keepdims=True)

    lax.fori_loop(0, SEQ // TT, dest_body, pad_start[0:1].astype(F32))

    rows = lax.broadcasted_iota(I32, (N_BLOCKS_PAD, LANES), 0) * FFN_B
    lanes = lax.broadcasted_iota(I32, (N_BLOCKS_PAD, LANES), 1)
    ended = (jnp.broadcast_to(pad_end[0:1], (N_BLOCKS_PAD, LANES)) <= rows) & (lanes < MOE_EXPERTS)
    be = jnp.sum(ended.astype(I32), axis=1, keepdims=True)
    blk_ref[...] = jnp.broadcast_to(jnp.minimum(be, MOE_EXPERTS - 1), (N_BLOCKS_PAD, LANES))
    total = jnp.max(pad_end, axis=1, keepdims=True)
    row8 = lax.broadcasted_iota(I32, (8, LANES), 0)
    used_ref[...] = jnp.where(row8 == 0, jnp.broadcast_to(total >> shift, (8, LANES)),
                              jnp.where(row8 == 1, counts8, pad_start))


def _plan(ri):
    return pl.pallas_call(
        _plan_kernel,
        out_shape=(
            jax.ShapeDtypeStruct((SEQ, LANES), I32),
            jax.ShapeDtypeStruct((N_BLOCKS_PAD, LANES), I32),
            jax.ShapeDtypeStruct((8, LANES), I32),
        ),
        compiler_params=pltpu.CompilerParams(vmem_limit_bytes=VMEM_LIMIT),
        name="route_plan",
    )(ri)


PAD_BITS = FFN_B.bit_length() - 1


def _pad_fill_copies(e, cnt_ref, pst_ref, zero_sc, xs_hbm, zsem):
    cnt = cnt_ref[e]
    pad = (-cnt) & (FFN_B - 1)
    row = pst_ref[e] + cnt
    out = []
    for bit in reversed(range(PAD_BITS)):
        n = 1 << bit
        start = row + ((pad >> (bit + 1)) << (bit + 1))
        copy = pltpu.make_async_copy(
            zero_sc.at[pl.ds(0, n * ROW_TILE)],
            xs_hbm.at[pl.ds(pl.multiple_of(start * ROW_TILE, ROW_TILE), n * ROW_TILE)], zsem)
        out.append(((pad & n) != 0, copy))
    return out


def _unused_block_copies(b, zero_sc, xs_hbm, zsem):
    half = FFN_B // 2 * ROW_TILE
    return [pltpu.make_async_copy(
        zero_sc, xs_hbm.at[pl.ds(pl.multiple_of((2 * b + k) * half, half), half)], zsem)
        for k in range(2)]


def _dispatch_kernel(dest_ref, cnt_ref, pst_ref, used_ref, xn_ref, xs_hbm, zero_sc, sem, zsem):
    tm = DISP_TM
    i = pl.program_id(0)

    @pl.when(i == 0)
    def _():
        zero_sc[...] = jnp.zeros_like(zero_sc)

        def fill(e, carry):
            for cond, copy in _pad_fill_copies(e, cnt_ref, pst_ref, zero_sc, xs_hbm, zsem):
                pl.when(cond)(copy.start)
            return carry

        lax.fori_loop(0, MOE_EXPERTS, fill, 0)

        def fill_block(b, carry):
            for copy in _unused_block_copies(b, zero_sc, xs_hbm, zsem):
                copy.start()
            return carry

        lax.fori_loop(used_ref[0], N_BLOCKS, fill_block, 0)

    def issue(it, carry):
        for u in range(DMA_UNROLL):
            r = it * DMA_UNROLL + u
            a = (i * tm + r) * 2
            src = xn_ref.at[pl.ds(pl.multiple_of(r * ROW_TILE, ROW_TILE), ROW_TILE)]
            for kk in range(2):
                d = pl.multiple_of(dest_ref[a + kk] * ROW_TILE, ROW_TILE)
                pltpu.make_async_copy(src, xs_hbm.at[pl.ds(d, ROW_TILE)], sem).start()
        return carry

    lax.fori_loop(0, tm // DMA_UNROLL, issue, 0)
    for _ in range(2):
        pltpu.make_async_copy(xn_ref, xs_hbm.at[pl.ds(0, tm * ROW_TILE)], sem).wait()

    @pl.when(i == 0)
    def _():
        def drain(e, carry):
            for cond, copy in _pad_fill_copies(e, cnt_ref, pst_ref, zero_sc, xs_hbm, zsem):
                pl.when(cond)(copy.wait)
            return carry

        lax.fori_loop(0, MOE_EXPERTS, drain, 0)

        def drain_block(b, carry):
            for copy in _unused_block_copies(b, zero_sc, xs_hbm, zsem):
                copy.wait()
            return carry

        lax.fori_loop(used_ref[0], N_BLOCKS, drain_block, 0)


def _dispatch(dest_flat, counts, pad_start, used, xn3):
    tm = DISP_TM
    return pl.pallas_call(
        _dispatch_kernel,
        out_shape=jax.ShapeDtypeStruct((N_BUF * ROW_TILE, LANES), F32),
        grid_spec=pltpu.PrefetchScalarGridSpec(
            num_scalar_prefetch=4,
            grid=(SEQ // tm,),
            in_specs=[pl.BlockSpec((tm * ROW_TILE, LANES), lambda i, d, c, p, u: (i, 0))],
            out_specs=pl.BlockSpec(memory_space=pl.ANY),
            scratch_shapes=[
                pltpu.VMEM((FFN_B // 2 * ROW_TILE, LANES), F32),
                pltpu.SemaphoreType.DMA(()),
                pltpu.SemaphoreType.DMA(()),
            ],
        ),
        compiler_params=_params(("arbitrary",)),
        name="moe_dispatch",
    )(dest_flat, counts, pad_start, used, xn3)


def _ffn_kernel(blk_ref, used_ref, xs_ref, wg_ref, wu_ref, wd_ref, y_ref, wg_bf, wu_bf, wd_bf):
    B = FFN_B
    b = pl.program_id(0)

    @pl.when(b < used_ref[0])
    def _():
        prev = blk_ref[jnp.maximum(b - 1, 0)]

        @pl.when((b == 0) | (blk_ref[b] != prev))
        def _():
            wg_bf[...] = wg_ref[0].astype(BF16)
            wu_bf[...] = wu_ref[0].astype(BF16)
            wd_bf[...] = wd_ref[0].astype(BF16)

        x = jnp.concatenate([xs_ref[pl.ds(c, B, stride=ROW_TILE), :] for c in range(ROW_TILE)],
                            axis=1).astype(BF16)
        hg = jnp.dot(x, wg_bf[...], preferred_element_type=F32)
        hu = jnp.dot(x, wu_bf[...], preferred_element_type=F32)
        hh = ((hg / (1.0 + jnp.exp(-hg))) * hu).astype(BF16)
        y = jnp.dot(hh, wd_bf[...], preferred_element_type=F32)
        for c in range(ROW_TILE):
            y_ref[pl.ds(c, B, stride=ROW_TILE), :] = y[:, c * LANES:(c + 1) * LANES]

    @pl.when(b >= used_ref[0])
    def _():
        y_ref[...] = jnp.zeros_like(y_ref)


def _ffn(blk_e, used, xs, w_gate, w_up, w_down):
    B = FFN_B
    wspec = lambda a, c: pl.BlockSpec((1, a, c), lambda b, blk, used: (blk[b], 0, 0))
    return pl.pallas_call(
        _ffn_kernel,
        out_shape=jax.ShapeDtypeStruct((N_BUF * ROW_TILE, LANES), F32),
        grid_spec=pltpu.PrefetchScalarGridSpec(
            num_scalar_prefetch=2,
            grid=(N_BLOCKS,),
            in_specs=[
                pl.BlockSpec((B * ROW_TILE, LANES),
                             lambda b, blk, used: (jnp.minimum(b, used[0] - 1), 0)),
                wspec(D_MODEL, MOE_HIDDEN),
                wspec(D_MODEL, MOE_HIDDEN),
                wspec(MOE_HIDDEN, D_MODEL),
            ],
            out_specs=pl.BlockSpec((B * ROW_TILE, LANES), lambda b, blk, used: (b, 0)),
            scratch_shapes=[
                pltpu.VMEM((D_MODEL, MOE_HIDDEN), BF16),
                pltpu.VMEM((D_MODEL, MOE_HIDDEN), BF16),
                pltpu.VMEM((MOE_HIDDEN, D_MODEL), BF16),
            ],
        ),
        compiler_params=_params(("arbitrary",)),
        name="expert_ffn",
    )(blk_e, used, xs, w_gate, w_up, w_down)


def _combine_kernel(dest_ref, h1_ref, rw_ref, g_ref, y_hbm, o_ref, ybuf, sem):
    tm = COMB_TM
    i = pl.program_id(0)

    def gather(tile, slot):
        def issue(it, carry):
            for u in range(DMA_UNROLL):
                r = it * DMA_UNROLL + u
                a = (tile * tm + r) * 2
                for kk in range(2):
                    d = pl.multiple_of(dest_ref[a + kk] * ROW_TILE, ROW_TILE)
                    pltpu.make_async_copy(
                        y_hbm.at[pl.ds(d, ROW_TILE)],
                        ybuf.at[slot, kk, pl.ds(pl.multiple_of(r * ROW_TILE, ROW_TILE), ROW_TILE)],
                        sem.at[slot, kk]).start()
            return carry

        lax.fori_loop(0, tm // DMA_UNROLL, issue, 0)

    @pl.when(i == 0)
    def _():
        gather(0, 0)

    @pl.when(i + 1 < pl.num_programs(0))
    def _():
        gather(i + 1, (i + 1) % 2)

    slot = i % 2
    for kk in range(2):
        pltpu.make_async_copy(y_hbm.at[pl.ds(0, tm * ROW_TILE)], ybuf.at[slot, kk],
                              sem.at[slot, kk]).wait()
    w = rw_ref[...]
    ys = [jnp.concatenate([ybuf[slot, kk, pl.ds(c, tm, stride=ROW_TILE), :]
                           for c in range(ROW_TILE)], axis=1) for kk in range(2)]
    h = h1_ref[...] + w[:, 0:1] * ys[0] + w[:, 1:2] * ys[1]
    var = jnp.mean(h * h, axis=-1, keepdims=True)
    o_ref[...] = h * lax.rsqrt(var + EPS) * g_ref[...]


def _combine(dest_flat, h1, rw, g, y):
    tm = COMB_TM
    return pl.pallas_call(
        _combine_kernel,
        out_shape=jax.ShapeDtypeStruct((SEQ, D_MODEL), F32),
        grid_spec=pltpu.PrefetchScalarGridSpec(
            num_scalar_prefetch=1,
            grid=(SEQ // tm,),
            in_specs=[
                pl.BlockSpec((tm, D_MODEL), lambda i, d: (i, 0)),
                pl.BlockSpec((tm, LANES), lambda i, d: (i, 0)),
                pl.BlockSpec((1, D_MODEL), lambda i, d: (0, 0)),
                pl.BlockSpec(memory_space=pl.ANY),
            ],
            out_specs=pl.BlockSpec((tm, D_MODEL), lambda i, d: (i, 0)),
            scratch_shapes=[
                pltpu.VMEM((2, 2, tm * ROW_TILE, LANES), F32),
                pltpu.SemaphoreType.DMA((2, 2)),
            ],
        ),
        compiler_params=_params(("arbitrary",)),
        name="moe_combine",
    )(dest_flat, h1, rw, g, y)


def _attention_tables():
    T = ATT_T
    slopes = jnp.exp2(-ALIBI_MAX * jnp.arange(1, DA_HEADS + 1, dtype=F32) / DA_HEADS)
    r = jnp.arange(T)
    hi = ((r // CHUNK) * CHUNK).astype(F32)
    lo = (r % CHUNK).astype(F32)
    ones = jnp.ones((T,), F32)
    sl = slopes[:, None]
    one_h = jnp.broadcast_to(ones, (DA_HEADS, T))
    q_rows = jnp.stack([one_h, one_h, -sl * hi[None], -sl * lo[None]], axis=1)
    k_cols = jnp.stack([sl * hi[None], sl * lo[None], one_h, one_h], axis=-1)
    qa = jnp.zeros((DA_HEADS, DA_HEAD_DIM, T), F32).at[:, 0:4, :].set(q_rows)
    ka1 = jnp.zeros((DA_HEADS, T, LANES), F32).at[:, :, DA_HEAD_DIM:DA_HEAD_DIM + 4].set(k_cols)
    ka2 = jnp.zeros((DA_HEADS, T, LANES), F32).at[:, :, 0:4].set(k_cols)
    rel = (r[:, None] - r[None, :]).astype(F32)
    allowed = (r[:, None] // CHUNK) <= (r[None, :] // CHUNK)
    fix = jnp.where(rel > 0, -2.0 * slopes[:, None, None] * rel[None], 0.0)
    dtab = jnp.where(allowed[None], fix, -jnp.inf)
    return slopes, qa.astype(BF16), ka1.astype(BF16), ka2.astype(BF16), dtab


def _retention_tables():
    C = RET_C
    log_gamma = jnp.log1p(-jnp.exp2(-5.0 - jnp.arange(RET_HEADS, dtype=F32)))
    pos = jnp.arange(C, dtype=F32)
    rel = pos[:, None] - pos[None, :]
    dec = jnp.where(rel >= 0, jnp.exp(log_gamma[:, None, None] * jnp.maximum(rel, 0.0)), 0.0)
    qdec = jnp.exp(log_gamma[:, None] * (pos + 1.0)[None, :])[:, :, None]
    kdec = jnp.exp(log_gamma[:, None] * (C - 1 - pos)[None, :])[:, :, None]
    cd = jnp.exp(log_gamma * C)
    return cd, dec, qdec, kdec


def kernel(x, attn_norm_g, w_in, da_lambda_q1, da_lambda_k1, da_lambda_q2, da_lambda_k2,
           da_subln_g, w_out, ffn_norm_g, router_group_w, router_group_b, router_expert_w,
           router_expert_b, expert_w_gate, expert_w_up, expert_w_down, final_norm_g):
    B, S, D = x.shape
    assert (B, S, D) == (1, SEQ, D_MODEL)
    x2 = x.reshape(S, D)

    w = w_in[0]
    w_main = jnp.concatenate([
        w[:, 512:1024],
        w[:, 1536:1792],
        w[:, 1792:2048] * (RET_QK_DIM ** -0.5),
        w[:, 2048:3072]], axis=1).astype(BF16)
    wq_t = (w[:, 0:512] * (DA_HEAD_DIM ** -0.5)).T.astype(BF16)
    wv_t = w[:, 1024:1536].T.astype(BF16)
    proj, qt4, vt4 = _inproj(x2, attn_norm_g[0][None, :], w_main, wq_t, wv_t)

    slopes, qa, ka1, ka2, dtab = _attention_tables()
    o_da = _attention(proj, qt4, vt4, slopes, qa, ka1, ka2, dtab, da_lambda_q1, da_lambda_k1,
                      da_lambda_q2, da_lambda_k2, da_subln_g)
    cd, dec, qdec, kdec = _retention_tables()
    o_r = _retention(proj, cd, dec, qdec, kdec)

    wr = jnp.zeros((D, LANES), F32)
    wr = wr.at[:, :MOE_GROUPS].set(router_group_w[0])
    wr = wr.at[:, MOE_GROUPS:MOE_GROUPS + MOE_EXPERTS].set(router_expert_w[0])
    br = jnp.zeros((1, LANES), F32)
    br = br.at[0, :MOE_GROUPS].set(router_group_b[0])
    br = br.at[0, MOE_GROUPS:MOE_GROUPS + MOE_EXPERTS].set(router_expert_b[0])
    wr_hi = wr.astype(BF16)
    wr_lo = (wr - wr_hi.astype(F32)).astype(BF16)
    h1, xn, ri, rw = _outproj_router(x2, o_da, o_r, w_out[0].astype(BF16), ffn_norm_g[0][None, :],
                                     jnp.concatenate([wr_hi, wr_lo], axis=1), br)

    dest, blk, used = _plan(ri)
    dest_flat = dest[:, :2].reshape(N_ASSIGN)
    used1 = used[0, :1]
    blk_e = blk[:N_BLOCKS, 0]
    blk_e = blk_e[jnp.minimum(jnp.arange(N_BLOCKS, dtype=I32), used1[0] - 1)]
    xs = _dispatch(dest_flat, used[1, :MOE_EXPERTS], used[2, :MOE_EXPERTS], used1, xn)

    y = _ffn(blk_e, used1, xs, expert_w_gate[0], expert_w_up[0], expert_w_down[0])
    out = _combine(dest_flat, h1, rw, final_norm_g[None, :], y)
    return out.reshape(B, S, D)
```

```python
import functools
import math

import jax
import jax.numpy as jnp
import numpy as np
from jax import lax
from jax.experimental import pallas as pl
from jax.experimental.pallas import tpu as pltpu

F32 = jnp.float32
BF16 = jnp.bfloat16
I32 = jnp.int32

D_MODEL = 1024
SEQ = 16384
CHUNK = 64
EPS = 1e-6

DA_HEADS = 4
DA_HEAD_DIM = 64
DA_V_DIM = 128
DA_WIDTH = 512
ALIBI_MAX = 8.0
RET_HEADS = 4
RET_QK_DIM = 64
RET_V_DIM = 128
RET_WIDTH = 512
T_ROWS = 512
MAIN_COLS = 2048
DK_OFF = 0
RQ_OFF = 512
RK_OFF = 768
RV_OFF = 1024
RG_OFF = 1536

MOE_GROUPS = 4
MOE_EXPERTS_PER_GROUP = 8
MOE_EXPERTS = 32
MOE_HIDDEN = 512
LAMBDA_INIT = 0.8 - 0.6 * math.exp(-0.3 * 0)

LANES = 128
ROW_TILE = 8
VMEM_LIMIT = 56 * 1024 * 1024

PROJ_TM = 512
ATT_T = 512
RET_C = 256
PLAN_T = 512
FFN_B = 256
N_ASSIGN = 2 * SEQ
N_BLOCKS = N_ASSIGN // FFN_B + MOE_EXPERTS
N_BLOCKS_PAD = (N_BLOCKS + 7) // 8 * 8
N_BUF = N_BLOCKS * FFN_B
COMB_TM = 256
DISP_TM = 512
DMA_UNROLL = 8


def _params(sem):
    return pltpu.CompilerParams(dimension_semantics=sem, vmem_limit_bytes=VMEM_LIMIT)


def _retention_block(q_all, k_all, v_all, g_all, cd_ref, dec_ref, qdec_ref, kdec_ref, st_sc):
    outs = []
    for h in range(RET_HEADS):
        qk = slice(h * RET_QK_DIM, (h + 1) * RET_QK_DIM)
        vv = slice(h * RET_V_DIM, (h + 1) * RET_V_DIM)
        q = q_all[:, qk]
        k = k_all[:, qk]
        v = v_all[:, vv]
        g = g_all[:, vv]
        s = lax.dot_general(q, k, (((1,), (1,)), ((), ())),
                            preferred_element_type=F32) * dec_ref[h]
        intra = jnp.dot(s.astype(BF16), v, preferred_element_type=F32)
        st = st_sc[h]
        cross = jnp.dot(q, st.astype(BF16), preferred_element_type=F32) * qdec_ref[h]
        kd = (k.astype(F32) * kdec_ref[h]).astype(BF16)
        st_sc[h] = st * cd_ref[h] + lax.dot_general(kd, v, (((0,), (0,)), ((), ())),
                                                    preferred_element_type=F32)
        o = intra + cross
        o = o * lax.rsqrt(jnp.mean(o * o, axis=-1, keepdims=True) + EPS)
        outs.append(((g / (1.0 + jnp.exp(-g))) * o).astype(BF16))
    return outs


def _inproj_kernel(cd_ref, x_ref, g_ref, w_ref, wq_ref, wv_ref, dec_ref, qdec_ref, kdec_ref,
                   k_ref, qt_ref, vt_ref, or_ref, st_sc):
    @pl.when(pl.program_id(0) == 0)
    def _():
        st_sc[...] = jnp.zeros_like(st_sc)

    x = x_ref[...]
    var = jnp.mean(x * x, axis=-1, keepdims=True)
    xn = (x * lax.rsqrt(var + EPS) * g_ref[...]).astype(BF16)

    def proj(lo, hi):
        return jnp.dot(xn, w_ref[:, lo:hi], preferred_element_type=F32)

    k_ref[...] = proj(DK_OFF, DK_OFF + DA_WIDTH).astype(BF16)
    nt = (((1,), (1,)), ((), ()))
    qt = lax.dot_general(wq_ref[...], xn, nt, preferred_element_type=F32)
    qt_ref[...] = qt.astype(BF16).reshape(DA_HEADS, 1, 2 * DA_HEAD_DIM, PROJ_TM)
    vt = lax.dot_general(wv_ref[...], xn, nt, preferred_element_type=F32)
    vt_ref[...] = vt.astype(BF16).reshape(DA_HEADS, 1, DA_V_DIM, PROJ_TM)

    rq = proj(RQ_OFF, RK_OFF).astype(BF16)
    rk = proj(RK_OFF, RV_OFF).astype(BF16)
    rv = proj(RV_OFF, RG_OFF).astype(BF16)
    rg = proj(RG_OFF, MAIN_COLS)
    for blk in range(PROJ_TM // RET_C):
        rows = slice(blk * RET_C, (blk + 1) * RET_C)
        outs = _retention_block(rq[rows], rk[rows], rv[rows], rg[rows],
                                cd_ref, dec_ref, qdec_ref, kdec_ref, st_sc)
        for h in range(RET_HEADS):
            or_ref[rows, h * RET_V_DIM:(h + 1) * RET_V_DIM] = outs[h]


def _inproj(x2, g, w_bf, wq_t, wv_t, cd, dec, qdec, kdec):
    C = RET_C
    t_shape = jax.ShapeDtypeStruct((DA_HEADS, SEQ // PROJ_TM, LANES, PROJ_TM), BF16)
    t_spec = pl.BlockSpec((DA_HEADS, 1, LANES, PROJ_TM), lambda i: (0, i, 0, 0))
    w_t_spec = pl.BlockSpec((T_ROWS, D_MODEL), lambda i: (0, 0))
    return pl.pallas_call(
        _inproj_kernel,
        out_shape=(jax.ShapeDtypeStruct((SEQ, DA_WIDTH), BF16), t_shape, t_shape,
                   jax.ShapeDtypeStruct((SEQ, RET_WIDTH), BF16)),
        grid=(SEQ // PROJ_TM,),
        in_specs=[
            pl.BlockSpec(memory_space=pltpu.SMEM),
            pl.BlockSpec((PROJ_TM, D_MODEL), lambda i: (i, 0)),
            pl.BlockSpec((1, D_MODEL), lambda i: (0, 0)),
            pl.BlockSpec((D_MODEL, MAIN_COLS), lambda i: (0, 0)),
            w_t_spec, w_t_spec,
            pl.BlockSpec((RET_HEADS, C, C), lambda i: (0, 0, 0)),
            pl.BlockSpec((RET_HEADS, C, 1), lambda i: (0, 0, 0)),
            pl.BlockSpec((RET_HEADS, C, 1), lambda i: (0, 0, 0)),
        ],
        out_specs=(pl.BlockSpec((PROJ_TM, DA_WIDTH), lambda i: (i, 0)), t_spec, t_spec,
                   pl.BlockSpec((PROJ_TM, RET_WIDTH), lambda i: (i, 0))),
        scratch_shapes=[pltpu.VMEM((RET_HEADS, RET_QK_DIM, RET_V_DIM), F32)],
        compiler_params=_params(("arbitrary",)),
        name="inproj_retention",
    )(cd, x2, g, w_bf, wq_t, wv_t, dec, qdec, kdec)


ACC_ROWS = DA_V_DIM + 16


N_QT = SEQ // ATT_T
N_OFF = N_QT * (N_QT - 1) // 2


def _pipeline3(n_pos, scores, accumulate):
    scores(0, 0)
    scores(1, 1)
    steady = n_pos - 2

    def triple(k, carry):
        t = 3 * k
        accumulate(t, 0)
        scores(t + 2, 2)
        accumulate(t + 1, 1)
        scores(t + 3, 0)
        accumulate(t + 2, 2)
        scores(t + 4, 1)
        return carry

    lax.fori_loop(0, steady // 3, triple, 0)
    t0 = steady // 3 * 3
    rem = steady - t0
    accumulate(t0, 0)
    if rem >= 1:
        scores(t0 + 2, 2)
    accumulate(t0 + 1, 1)
    if rem == 2:
        scores(t0 + 3, 0)
    if rem >= 1:
        accumulate(t0 + 2, 2)
    if rem == 2:
        accumulate(t0 + 3, 0)


def _attn_kernel(slope_ref, jt_ref, it_ref, qt_ref, k_ref, vt_ref, qa_ref, ka1_ref, ka2_ref,
                 dtab_ref, lq1_ref, lk1_ref, lq2_ref, lk2_ref, g_ref, o_ref,
                 m_sc, acc_sc, s0_sc, s1_sc, s2_sc, mx0_sc, mx1_sc, mx2_sc):
    T = ATT_T
    h = pl.program_id(0)
    slope = slope_ref[h]
    qa = qa_ref[0]
    lane = lax.broadcasted_iota(I32, (T, LANES), 1)
    sums_row = (lax.broadcasted_iota(I32, (16, T), 0) == 0).astype(BF16)
    s_bufs = (s0_sc, s1_sc, s2_sc)
    mx_bufs = (mx0_sc, mx1_sc, mx2_sc)

    def scores(j, i, buf, extra):
        kt = k_ref[pl.ds(pl.multiple_of(j * T, T), T), :]
        ks = (jnp.where(lane < DA_HEAD_DIM, kt, ka1_ref[0]),
              jnp.where(lane >= DA_HEAD_DIM, kt, ka2_ref[0]))
        qt = qt_ref[0, i]
        qw = (jnp.concatenate([qt[0:DA_HEAD_DIM], qa], axis=0),
              jnp.concatenate([qa, qt[DA_HEAD_DIM:]], axis=0))
        for mp in range(2):
            s = jnp.dot(ks[mp], qw[mp], preferred_element_type=F32)
            if extra is not None:
                s = s + extra[0]
            s_bufs[buf][mp] = s
            mx_bufs[buf][mp] = jnp.max(s, axis=0, keepdims=True)

    def accumulate(j, i, buf):
        c = slope * lax.convert_element_type((i - j) * T, F32)
        vte = jnp.concatenate([vt_ref[0, j], sums_row], axis=0)
        for mp in range(2):
            m_prev = m_sc[i, mp]
            m_new = jnp.maximum(m_prev, mx_bufs[buf][mp] - c)
            p = jnp.exp(s_bufs[buf][mp] - (m_new + c)).astype(BF16)
            pv = jnp.dot(vte, p, preferred_element_type=F32)
            acc_sc[i, mp] = jnp.exp(m_prev - m_new) * acc_sc[i, mp] + pv
            m_sc[i, mp] = m_new

    m_sc[...] = jnp.full_like(m_sc, -jnp.inf)
    acc_sc[...] = jnp.zeros_like(acc_sc)
    _pipeline3(N_QT,
               lambda pos, buf: scores(pos, pos, buf, dtab_ref),
               lambda pos, buf: accumulate(pos, pos, buf))
    _pipeline3(N_OFF,
               lambda pos, buf: scores(jt_ref[pos], it_ref[pos], buf, None),
               lambda pos, buf: accumulate(jt_ref[pos], it_ref[pos], buf))

    lam = (jnp.exp(jnp.sum(lq1_ref[...] * lk1_ref[...], axis=1, keepdims=True))
           - jnp.exp(jnp.sum(lq2_ref[...] * lk2_ref[...], axis=1, keepdims=True))
           + LAMBDA_INIT)

    def finish(i, carry):
        a1 = acc_sc[i, 0]
        a2 = acc_sc[i, 1]
        ot = (a1[0:DA_V_DIM] / a1[DA_V_DIM:DA_V_DIM + 1]
              - lam * (a2[0:DA_V_DIM] / a2[DA_V_DIM:DA_V_DIM + 1]))
        o = ot.T
        var = jnp.mean(o * o, axis=-1, keepdims=True)
        o = (o * lax.rsqrt(var + EPS) * g_ref[...]) * (1.0 - LAMBDA_INIT)
        o_ref[pl.ds(pl.multiple_of(i * T, T), T), :] = o.astype(BF16)
        return carry

    lax.fori_loop(0, N_QT, finish, 0)


def _attention(proj, qt4, vt4, slopes, qa, ka1, ka2, dtab, lq1, lk1, lq2, lk2, subln_g):
    T = ATT_T
    vec64 = pl.BlockSpec((1, DA_HEAD_DIM), lambda h: (0, 0))
    per_head = lambda a, b: pl.BlockSpec((1, a, b), lambda h: (h, 0, 0))
    slab = lambda shape, imap: pl.BlockSpec(shape, imap, pipeline_mode=pl.Buffered(1))
    smem = pl.BlockSpec(memory_space=pltpu.SMEM)
    it_tab, jt_tab = np.tril_indices(N_QT, -1)
    return pl.pallas_call(
        _attn_kernel,
        out_shape=jax.ShapeDtypeStruct((SEQ, DA_WIDTH), BF16),
        grid=(DA_HEADS,),
        in_specs=[
            smem, smem, smem,
            slab((1, N_QT, LANES, T), lambda h: (h, 0, 0, 0)),
            slab((SEQ, LANES), lambda h: (0, DK_OFF // LANES + h)),
            slab((1, N_QT, LANES, T), lambda h: (h, 0, 0, 0)),
            per_head(DA_HEAD_DIM, T), per_head(T, LANES), per_head(T, LANES), per_head(T, T),
            vec64, vec64, vec64, vec64,
            pl.BlockSpec((1, DA_V_DIM), lambda h: (0, 0)),
        ],
        out_specs=slab((SEQ, LANES), lambda h: (0, h)),
        scratch_shapes=[
            pltpu.VMEM((N_QT, 2, 1, T), F32),
            pltpu.VMEM((N_QT, 2, ACC_ROWS, T), F32),
            pltpu.VMEM((2, T, T), F32),
            pltpu.VMEM((2, T, T), F32),
            pltpu.VMEM((2, T, T), F32),
            pltpu.VMEM((2, 1, T), F32),
            pltpu.VMEM((2, 1, T), F32),
            pltpu.VMEM((2, 1, T), F32),
        ],
        compiler_params=_params(("arbitrary",)),
        name="diff_attention",
    )(slopes, jnp.asarray(jt_tab, I32), jnp.asarray(it_tab, I32), qt4, proj, vt4, qa, ka1, ka2,
      dtab, lq1, lk1, lq2, lk2, subln_g)


def _outproj_router_kernel(x_ref, oda_ref, or_ref, wo_ref, g_ref, wr_ref, br_ref,
                           h1_ref, xn_ref, ri_ref, rw_ref):
    h1 = (x_ref[...]
          + jnp.dot(oda_ref[...], wo_ref[0:DA_WIDTH, :], preferred_element_type=F32)
          + jnp.dot(or_ref[...], wo_ref[DA_WIDTH:, :], preferred_element_type=F32))
    h1_ref[...] = h1
    var = jnp.mean(h1 * h1, axis=-1, keepdims=True)
    xn = h1 * lax.rsqrt(var + EPS) * g_ref[...]
    for c in range(ROW_TILE):
        xn_ref[pl.ds(c, PROJ_TM, stride=ROW_TILE), :] = xn[:, c * LANES:(c + 1) * LANES]
    x_hi = xn.astype(BF16)
    x_lo = (xn - x_hi.astype(F32)).astype(BF16)
    both = jnp.dot(x_hi, wr_ref[...], preferred_element_type=F32)
    logits = (both[:, :LANES] + both[:, LANES:]
              + jnp.dot(x_lo, wr_ref[:, :LANES], preferred_element_type=F32)) + br_ref[...]
    lane = lax.broadcasted_iota(I32, logits.shape, 1)
    neg = jnp.float32(-jnp.inf)
    big = jnp.int32(1 << 20)
    gl = jnp.where(lane < MOE_GROUPS, logits, neg)
    gmax = jnp.max(gl, axis=1, keepdims=True)
    gidx = jnp.min(jnp.where(gl == gmax, lane, big), axis=1, keepdims=True)
    gsum = jnp.sum(jnp.exp(gl - gmax), axis=1, keepdims=True)
    gp = 1.0 / gsum
    lo = MOE_GROUPS + gidx * MOE_EXPERTS_PER_GROUP
    el = jnp.where((lane >= lo) & (lane < lo + MOE_EXPERTS_PER_GROUP), logits, neg)
    v1 = jnp.max(el, axis=1, keepdims=True)
    i1 = jnp.min(jnp.where(el == v1, lane, big), axis=1, keepdims=True)
    el2 = jnp.where(lane == i1, neg, el)
    v2 = jnp.max(el2, axis=1, keepdims=True)
    i2 = jnp.min(jnp.where(el2 == v2, lane, big), axis=1, keepdims=True)
    t = jnp.exp(v2 - v1)
    w1 = gp / (1.0 + t)
    w2 = gp * t / (1.0 + t)
    ri_ref[...] = jnp.where(lane == 0, i1 - MOE_GROUPS,
                            jnp.where(lane == 1, i2 - MOE_GROUPS, 0))
    rw_ref[...] = jnp.where(lane == 0, w1, jnp.where(lane == 1, w2, 0.0))


def _outproj_router(x2, o_da, o_r, wo_bf, g, wr, br):
    tm = PROJ_TM
    row = lambda w: pl.BlockSpec((tm, w), lambda i: (i, 0))
    full = lambda a, b: pl.BlockSpec((a, b), lambda i: (0, 0))
    return pl.pallas_call(
        _outproj_router_kernel,
        out_shape=(
            jax.ShapeDtypeStruct((SEQ, D_MODEL), F32),
            jax.ShapeDtypeStruct((SEQ * ROW_TILE, LANES), F32),
            jax.ShapeDtypeStruct((SEQ, LANES), I32),
            jax.ShapeDtypeStruct((SEQ, LANES), F32),
        ),
        grid=(SEQ // tm,),
        in_specs=[row(D_MODEL), row(DA_WIDTH), row(RET_WIDTH), full(D_MODEL, D_MODEL),
                  full(1, D_MODEL), full(D_MODEL, 2 * LANES), full(1, LANES)],
        out_specs=(row(D_MODEL), pl.BlockSpec((tm * ROW_TILE, LANES), lambda i: (i, 0)),
                   row(LANES), row(LANES)),
        compiler_params=_params(("arbitrary",)),
        name="outproj_router",
    )(x2, o_da, o_r, wo_bf, g, wr, br)


def _plan_kernel(ri_ref, dest_ref, blk_ref, used_ref):
    TT = PLAN_T
    lane = lax.broadcasted_iota(I32, (TT, LANES), 1)

    def onehots(t):
        r = ri_ref[pl.ds(pl.multiple_of(t * TT, TT), TT), :]
        return lane == r[:, 0:1], lane == r[:, 1:2]

    def count_body(t, acc):
        oh1, oh2 = onehots(t)
        return acc + jnp.sum((oh1 | oh2).astype(F32), axis=0, keepdims=True)

    counts = lax.fori_loop(0, SEQ // TT, count_body, jnp.zeros((1, LANES), F32))
    counts8 = jnp.broadcast_to(counts, (8, LANES)).astype(I32)
    shift = FFN_B.bit_length() - 1
    padded = ((counts8 + (FFN_B - 1)) >> shift) << shift
    lane8 = lax.broadcasted_iota(I32, (8, LANES), 1)
    pad_end = padded
    sh = 1
    while sh < LANES:
        pad_end = pad_end + jnp.where(lane8 >= sh, pltpu.roll(pad_end, sh, axis=1), 0)
        sh *= 2
    pad_start = pad_end - padded

    ltri = (lax.broadcasted_iota(I32, (TT, TT), 0)
            > lax.broadcasted_iota(I32, (TT, TT), 1)).astype(BF16)

    def dest_body(t, carry):
        oh1, oh2 = onehots(t)
        a = (oh1 | oh2).astype(F32)
        base = jnp.dot(ltri, a.astype(BF16), preferred_element_type=F32) + carry
        d1 = jnp.sum(jnp.where(oh1, base, 0.0), axis=1, keepdims=True)
        d2 = jnp.sum(jnp.where(oh2, base, 0.0), axis=1, keepdims=True)
        dest_ref[pl.ds(pl.multiple_of(t * TT, TT), TT), :] = jnp.where(
            lane == 0, d1, jnp.where(lane == 1, d2, 0.0)).astype(I32)
        return carry + jnp.sum(a, axis=0, keepdims=True)

    lax.fori_loop(0, SEQ // TT, dest_body, pad_start[0:1].astype(F32))

    rows = lax.broadcasted_iota(I32, (N_BLOCKS_PAD, LANES), 0) * FFN_B
    lanes = lax.broadcasted_iota(I32, (N_BLOCKS_PAD, LANES), 1)
    ended = (jnp.broadcast_to(pad_end[0:1], (N_BLOCKS_PAD, LANES)) <= rows) & (lanes < MOE_EXPERTS)
    be = jnp.sum(ended.astype(I32), axis=1, keepdims=True)
    blk_ref[...] = jnp.broadcast_to(jnp.minimum(be, MOE_EXPERTS - 1), (N_BLOCKS_PAD, LANES))
    total = jnp.max(pad_end, axis=1, keepdims=True)
    row8 = lax.broadcasted_iota(I32, (8, LANES), 0)
    used_ref[...] = jnp.where(row8 == 0, jnp.broadcast_to(total >> shift, (8, LANES)),
                              jnp.where(row8 == 1, counts8, pad_start))


def _plan(ri):
    return pl.pallas_call(
        _plan_kernel,
        out_shape=(
            jax.ShapeDtypeStruct((SEQ, LANES), I32),
            jax.ShapeDtypeStruct((N_BLOCKS_PAD, LANES), I32),
            jax.ShapeDtypeStruct((8, LANES), I32),
        ),
        compiler_params=pltpu.CompilerParams(vmem_limit_bytes=VMEM_LIMIT),
        name="route_plan",
    )(ri)


PAD_BITS = FFN_B.bit_length() - 1


def _pad_fill_copies(e, cnt_ref, pst_ref, zero_sc, xs_hbm, zsem):
    cnt = cnt_ref[e]
    pad = (-cnt) & (FFN_B - 1)
    row = pst_ref[e] + cnt
    out = []
    for bit in reversed(range(PAD_BITS)):
        n = 1 << bit
        start = row + ((pad >> (bit + 1)) << (bit + 1))
        copy = pltpu.make_async_copy(
            zero_sc.at[pl.ds(0, n * ROW_TILE)],
            xs_hbm.at[pl.ds(pl.multiple_of(start * ROW_TILE, ROW_TILE), n * ROW_TILE)], zsem)
        out.append(((pad & n) != 0, copy))
    return out


def _unused_block_copies(b, zero_sc, xs_hbm, zsem):
    half = FFN_B // 2 * ROW_TILE
    return [pltpu.make_async_copy(
        zero_sc, xs_hbm.at[pl.ds(pl.multiple_of((2 * b + k) * half, half), half)], zsem)
        for k in range(2)]


def _dispatch_kernel(dest_ref, cnt_ref, pst_ref, used_ref, xn_ref, xs_hbm, zero_sc, sem, zsem):
    tm = DISP_TM
    i = pl.program_id(0)

    @pl.when(i == 0)
    def _():
        zero_sc[...] = jnp.zeros_like(zero_sc)

        def fill(e, carry):
            for cond, copy in _pad_fill_copies(e, cnt_ref, pst_ref, zero_sc, xs_hbm, zsem):
                pl.when(cond)(copy.start)
            return carry

        lax.fori_loop(0, MOE_EXPERTS, fill, 0)

        def fill_block(b, carry):
            for copy in _unused_block_copies(b, zero_sc, xs_hbm, zsem):
                copy.start()
            return carry

        lax.fori_loop(used_ref[0], N_BLOCKS, fill_block, 0)

    def issue(it, carry):
        for u in range(DMA_UNROLL):
            r = it * DMA_UNROLL + u
            a = (i * tm + r) * 2
            src = xn_ref.at[pl.ds(pl.multiple_of(r * ROW_TILE, ROW_TILE), ROW_TILE)]
            for kk in range(2):
                d = pl.multiple_of(dest_ref[a + kk] * ROW_TILE, ROW_TILE)
                pltpu.make_async_copy(src, xs_hbm.at[pl.ds(d, ROW_TILE)], sem).start()
        return carry

    lax.fori_loop(0, tm // DMA_UNROLL, issue, 0)
    for _ in range(2):
        pltpu.make_async_copy(xn_ref, xs_hbm.at[pl.ds(0, tm * ROW_TILE)], sem).wait()

    @pl.when(i == 0)
    def _():
        def drain(e, carry):
            for cond, copy in _pad_fill_copies(e, cnt_ref, pst_ref, zero_sc, xs_hbm, zsem):
                pl.when(cond)(copy.wait)
            return carry

        lax.fori_loop(0, MOE_EXPERTS, drain, 0)

        def drain_block(b, carry):
            for copy in _unused_block_copies(b, zero_sc, xs_hbm, zsem):
                copy.wait()
            return carry

        lax.fori_loop(used_ref[0], N_BLOCKS, drain_block, 0)


def _dispatch(dest_flat, counts, pad_start, used, xn3):
    tm = DISP_TM
    return pl.pallas_call(
        _dispatch_kernel,
        out_shape=jax.ShapeDtypeStruct((N_BUF * ROW_TILE, LANES), F32),
        grid_spec=pltpu.PrefetchScalarGridSpec(
            num_scalar_prefetch=4,
            grid=(SEQ // tm,),
            in_specs=[pl.BlockSpec((tm * ROW_TILE, LANES), lambda i, d, c, p, u: (i, 0))],
            out_specs=pl.BlockSpec(memory_space=pl.ANY),
            scratch_shapes=[
                pltpu.VMEM((FFN_B // 2 * ROW_TILE, LANES), F32),
                pltpu.SemaphoreType.DMA(()),
                pltpu.SemaphoreType.DMA(()),
            ],
        ),
        compiler_params=_params(("arbitrary",)),
        name="moe_dispatch",
    )(dest_flat, counts, pad_start, used, xn3)


def _ffn_kernel(blk_ref, used_ref, xs_ref, wg_ref, wu_ref, wd_ref, y_ref, wg_bf, wu_bf, wd_bf):
    B = FFN_B
    b = pl.program_id(0)

    @pl.when(b < used_ref[0])
    def _():
        prev = blk_ref[jnp.maximum(b - 1, 0)]

        @pl.when((b == 0) | (blk_ref[b] != prev))
        def _():
            wg_bf[...] = wg_ref[0].astype(BF16)
            wu_bf[...] = wu_ref[0].astype(BF16)
            wd_bf[...] = wd_ref[0].astype(BF16)

        x = jnp.concatenate([xs_ref[pl.ds(c, B, stride=ROW_TILE), :] for c in range(ROW_TILE)],
                            axis=1).astype(BF16)
        hg = jnp.dot(x, wg_bf[...], preferred_element_type=F32)
        hu = jnp.dot(x, wu_bf[...], preferred_element_type=F32)
        hh = ((hg / (1.0 + jnp.exp(-hg))) * hu).astype(BF16)
        y = jnp.dot(hh, wd_bf[...], preferred_element_type=F32)
        for c in range(ROW_TILE):
            y_ref[pl.ds(c, B, stride=ROW_TILE), :] = y[:, c * LANES:(c + 1) * LANES]

    @pl.when(b >= used_ref[0])
    def _():
        y_ref[...] = jnp.zeros_like(y_ref)


def _ffn(blk_e, used, xs, w_gate, w_up, w_down):
    B = FFN_B
    wspec = lambda a, c: pl.BlockSpec((1, a, c), lambda b, blk, used: (blk[b], 0, 0))
    return pl.pallas_call(
        _ffn_kernel,
        out_shape=jax.ShapeDtypeStruct((N_BUF * ROW_TILE, LANES), F32),
        grid_spec=pltpu.PrefetchScalarGridSpec(
            num_scalar_prefetch=2,
            grid=(N_BLOCKS,),
            in_specs=[
                pl.BlockSpec((B * ROW_TILE, LANES),
                             lambda b, blk, used: (jnp.minimum(b, used[0] - 1), 0)),
                wspec(D_MODEL, MOE_HIDDEN),
                wspec(D_MODEL, MOE_HIDDEN),
                wspec(MOE_HIDDEN, D_MODEL),
            ],
            out_specs=pl.BlockSpec((B * ROW_TILE, LANES), lambda b, blk, used: (b, 0)),
            scratch_shapes=[
                pltpu.VMEM((D_MODEL, MOE_HIDDEN), BF16),
                pltpu.VMEM((D_MODEL, MOE_HIDDEN), BF16),
                pltpu.VMEM((MOE_HIDDEN, D_MODEL), BF16),
            ],
        ),
        compiler_params=_params(("arbitrary",)),
        name="expert_ffn",
    )(blk_e, used, xs, w_gate, w_up, w_down)


def _combine_kernel(dest_ref, h1_ref, rw_ref, g_ref, y_hbm, o_ref, ybuf, sem):
    tm = COMB_TM
    i = pl.program_id(0)

    def gather(tile, slot):
        def issue(it, carry):
            for u in range(DMA_UNROLL):
                r = it * DMA_UNROLL + u
                a = (tile * tm + r) * 2
                for kk in range(2):
                    d = pl.multiple_of(dest_ref[a + kk] * ROW_TILE, ROW_TILE)
                    pltpu.make_async_copy(
                        y_hbm.at[pl.ds(d, ROW_TILE)],
                        ybuf.at[slot, kk, pl.ds(pl.multiple_of(r * ROW_TILE, ROW_TILE), ROW_TILE)],
                        sem.at[slot, kk]).start()
            return carry

        lax.fori_loop(0, tm // DMA_UNROLL, issue, 0)

    @pl.when(i == 0)
    def _():
        gather(0, 0)

    @pl.when(i + 1 < pl.num_programs(0))
    def _():
        gather(i + 1, (i + 1) % 2)

    slot = i % 2
    for kk in range(2):
        pltpu.make_async_copy(y_hbm.at[pl.ds(0, tm * ROW_TILE)], ybuf.at[slot, kk],
                              sem.at[slot, kk]).wait()
    w = rw_ref[...]
    ys = [jnp.concatenate([ybuf[slot, kk, pl.ds(c, tm, stride=ROW_TILE), :]
                           for c in range(ROW_TILE)], axis=1) for kk in range(2)]
    h = h1_ref[...] + w[:, 0:1] * ys[0] + w[:, 1:2] * ys[1]
    var = jnp.mean(h * h, axis=-1, keepdims=True)
    o_ref[...] = h * lax.rsqrt(var + EPS) * g_ref[...]


def _combine(dest_flat, h1, rw, g, y):
    tm = COMB_TM
    return pl.pallas_call(
        _combine_kernel,
        out_shape=jax.ShapeDtypeStruct((SEQ, D_MODEL), F32),
        grid_spec=pltpu.PrefetchScalarGridSpec(
            num_scalar_prefetch=1,
            grid=(SEQ // tm,),
            in_specs=[
                pl.BlockSpec((tm, D_MODEL), lambda i, d: (i, 0)),
                pl.BlockSpec((tm, LANES), lambda i, d: (i, 0)),
                pl.BlockSpec((1, D_MODEL), lambda i, d: (0, 0)),
                pl.BlockSpec(memory_space=pl.ANY),
            ],
            out_specs=pl.BlockSpec((tm, D_MODEL), lambda i, d: (i, 0)),
            scratch_shapes=[
                pltpu.VMEM((2, 2, tm * ROW_TILE, LANES), F32),
                pltpu.SemaphoreType.DMA((2, 2)),
            ],
        ),
        compiler_params=_params(("arbitrary",)),
        name="moe_combine",
    )(dest_flat, h1, rw, g, y)


def _attention_tables():
    T = ATT_T
    slopes = jnp.exp2(-ALIBI_MAX * jnp.arange(1, DA_HEADS + 1, dtype=F32) / DA_HEADS)
    r = jnp.arange(T)
    hi = ((r // CHUNK) * CHUNK).astype(F32)
    lo = (r % CHUNK).astype(F32)
    ones = jnp.ones((T,), F32)
    sl = slopes[:, None]
    one_h = jnp.broadcast_to(ones, (DA_HEADS, T))
    q_rows = jnp.stack([one_h, one_h, -sl * hi[None], -sl * lo[None]], axis=1)
    k_cols = jnp.stack([sl * hi[None], sl * lo[None], one_h, one_h], axis=-1)
    qa = jnp.zeros((DA_HEADS, DA_HEAD_DIM, T), F32).at[:, 0:4, :].set(q_rows)
    ka1 = jnp.zeros((DA_HEADS, T, LANES), F32).at[:, :, DA_HEAD_DIM:DA_HEAD_DIM + 4].set(k_cols)
    ka2 = jnp.zeros((DA_HEADS, T, LANES), F32).at[:, :, 0:4].set(k_cols)
    rel = (r[:, None] - r[None, :]).astype(F32)
    allowed = (r[:, None] // CHUNK) <= (r[None, :] // CHUNK)
    fix = jnp.where(rel > 0, -2.0 * slopes[:, None, None] * rel[None], 0.0)
    dtab = jnp.where(allowed[None], fix, -jnp.inf)
    return slopes, qa.astype(BF16), ka1.astype(BF16), ka2.astype(BF16), dtab


def _retention_tables():
    C = RET_C
    log_gamma = jnp.log1p(-jnp.exp2(-5.0 - jnp.arange(RET_HEADS, dtype=F32)))
    pos = jnp.arange(C, dtype=F32)
    rel = pos[:, None] - pos[None, :]
    dec = jnp.where(rel >= 0, jnp.exp(log_gamma[:, None, None] * jnp.maximum(rel, 0.0)), 0.0)
    qdec = jnp.exp(log_gamma[:, None] * (pos + 1.0)[None, :])[:, :, None]
    kdec = jnp.exp(log_gamma[:, None] * (C - 1 - pos)[None, :])[:, :, None]
    cd = jnp.exp(log_gamma * C)
    return cd, dec, qdec, kdec


def kernel(x, attn_norm_g, w_in, da_lambda_q1, da_lambda_k1, da_lambda_q2, da_lambda_k2,
           da_subln_g, w_out, ffn_norm_g, router_group_w, router_group_b, router_expert_w,
           router_expert_b, expert_w_gate, expert_w_up, expert_w_down, final_norm_g):
    B, S, D = x.shape
    assert (B, S, D) == (1, SEQ, D_MODEL)
    x2 = x.reshape(S, D)

    w = w_in[0]
    w_main = jnp.concatenate([
        w[:, 512:1024],
        w[:, 1536:1792],
        w[:, 1792:2048] * (RET_QK_DIM ** -0.5),
        w[:, 2048:3072]], axis=1).astype(BF16)
    wq_t = (w[:, 0:512] * (DA_HEAD_DIM ** -0.5)).T.astype(BF16)
    wv_t = w[:, 1024:1536].T.astype(BF16)
    cd, dec, qdec, kdec = _retention_tables()
    k_da, qt4, vt4, o_r = _inproj(x2, attn_norm_g[0][None, :], w_main, wq_t, wv_t,
                                  cd, dec, qdec, kdec)

    slopes, qa, ka1, ka2, dtab = _attention_tables()
    o_da = _attention(k_da, qt4, vt4, slopes, qa, ka1, ka2, dtab, da_lambda_q1, da_lambda_k1,
                      da_lambda_q2, da_lambda_k2, da_subln_g)

    wr = jnp.zeros((D, LANES), F32)
    wr = wr.at[:, :MOE_GROUPS].set(router_group_w[0])
    wr = wr.at[:, MOE_GROUPS:MOE_GROUPS + MOE_EXPERTS].set(router_expert_w[0])
    br = jnp.zeros((1, LANES), F32)
    br = br.at[0, :MOE_GROUPS].set(router_group_b[0])
    br = br.at[0, MOE_GROUPS:MOE_GROUPS + MOE_EXPERTS].set(router_expert_b[0])
    wr_hi = wr.astype(BF16)
    wr_lo = (wr - wr_hi.astype(F32)).astype(BF16)
    h1, xn, ri, rw = _outproj_router(x2, o_da, o_r, w_out[0].astype(BF16), ffn_norm_g[0][None, :],
                                     jnp.concatenate([wr_hi, wr_lo], axis=1), br)

    dest, blk, used = _plan(ri)
    dest_flat = dest[:, :2].reshape(N_ASSIGN)
    used1 = used[0, :1]
    blk_e = blk[:N_BLOCKS, 0]
    blk_e = blk_e[jnp.minimum(jnp.arange(N_BLOCKS, dtype=I32), used1[0] - 1)]
    xs = _dispatch(dest_flat, used[1, :MOE_EXPERTS], used[2, :MOE_EXPERTS], used1, xn)

    y = _ffn(blk_e, used1, xs, expert_w_gate[0], expert_w_up[0], expert_w_down[0])
    out = _combine(dest_flat, h1, rw, final_norm_g[None, :], y)
    return out.reshape(B, S, D)
```

```python
import functools
import math

import jax
import jax.numpy as jnp
import numpy as np
from jax import lax
from jax.experimental import pallas as pl
from jax.experimental.pallas import tpu as pltpu

F32 = jnp.float32
BF16 = jnp.bfloat16
I32 = jnp.int32

D_MODEL = 1024
SEQ = 16384
CHUNK = 64
EPS = 1e-6

DA_HEADS = 4
DA_HEAD_DIM = 64
DA_V_DIM = 128
DA_WIDTH = 512
ALIBI_MAX = 8.0
RET_HEADS = 4
RET_QK_DIM = 64
RET_V_DIM = 128
RET_WIDTH = 512
T_ROWS = 512
MAIN_COLS = 2048
DK_OFF = 0
RQ_OFF = 512
RK_OFF = 768
RV_OFF = 1024
RG_OFF = 1536

MOE_GROUPS = 4
MOE_EXPERTS_PER_GROUP = 8
MOE_EXPERTS = 32
MOE_HIDDEN = 512
LAMBDA_INIT = 0.8 - 0.6 * math.exp(-0.3 * 0)

LANES = 128
ROW_TILE = 8
VMEM_LIMIT = 56 * 1024 * 1024

PROJ_TM = 512
ATT_T = 512
RET_C = 256
PLAN_T = 512
FFN_B = 256
N_ASSIGN = 2 * SEQ
N_BLOCKS = N_ASSIGN // FFN_B + MOE_EXPERTS
N_BUF = N_BLOCKS * FFN_B
COMB_TM = 256
DISP_TM = 512
DMA_UNROLL = 8


def _params(sem):
    return pltpu.CompilerParams(dimension_semantics=sem, vmem_limit_bytes=VMEM_LIMIT)


def _retention_block(q_all, k_all, v_all, g_all, cd_ref, dec_ref, qdec_ref, kdec_ref, st_sc):
    outs = []
    for h in range(RET_HEADS):
        qk = slice(h * RET_QK_DIM, (h + 1) * RET_QK_DIM)
        vv = slice(h * RET_V_DIM, (h + 1) * RET_V_DIM)
        q = q_all[:, qk]
        k = k_all[:, qk]
        v = v_all[:, vv]
        g = g_all[:, vv]
        s = lax.dot_general(q, k, (((1,), (1,)), ((), ())),
                            preferred_element_type=F32) * dec_ref[h]
        intra = jnp.dot(s.astype(BF16), v, preferred_element_type=F32)
        st = st_sc[h]
        cross = jnp.dot(q, st.astype(BF16), preferred_element_type=F32) * qdec_ref[h]
        kd = (k.astype(F32) * kdec_ref[h]).astype(BF16)
        st_sc[h] = st * cd_ref[h] + lax.dot_general(kd, v, (((0,), (0,)), ((), ())),
                                                    preferred_element_type=F32)
        o = intra + cross
        o = o * lax.rsqrt(jnp.mean(o * o, axis=-1, keepdims=True) + EPS)
        outs.append(((g / (1.0 + jnp.exp(-g))) * o).astype(BF16))
    return outs


def _inproj_kernel(cd_ref, x_ref, g_ref, w_ref, wq_ref, wv_ref, dec_ref, qdec_ref, kdec_ref,
                   k_ref, qt_ref, vt_ref, or_ref, st_sc):
    @pl.when(pl.program_id(0) == 0)
    def _():
        st_sc[...] = jnp.zeros_like(st_sc)

    x = x_ref[...]
    var = jnp.mean(x * x, axis=-1, keepdims=True)
    xn = (x * lax.rsqrt(var + EPS) * g_ref[...]).astype(BF16)

    def proj(lo, hi):
        return jnp.dot(xn, w_ref[:, lo:hi], preferred_element_type=F32)

    k_ref[...] = proj(DK_OFF, DK_OFF + DA_WIDTH).astype(BF16)
    nt = (((1,), (1,)), ((), ()))
    qt = lax.dot_general(wq_ref[...], xn, nt, preferred_element_type=F32)
    qt_ref[...] = qt.astype(BF16).reshape(DA_HEADS, 1, 2 * DA_HEAD_DIM, PROJ_TM)
    vt = lax.dot_general(wv_ref[...], xn, nt, preferred_element_type=F32)
    vt_ref[...] = vt.astype(BF16).reshape(DA_HEADS, 1, DA_V_DIM, PROJ_TM)

    rq = proj(RQ_OFF, RK_OFF).astype(BF16)
    rk = proj(RK_OFF, RV_OFF).astype(BF16)
    rv = proj(RV_OFF, RG_OFF).astype(BF16)
    rg = proj(RG_OFF, MAIN_COLS)
    for blk in range(PROJ_TM // RET_C):
        rows = slice(blk * RET_C, (blk + 1) * RET_C)
        outs = _retention_block(rq[rows], rk[rows], rv[rows], rg[rows],
                                cd_ref, dec_ref, qdec_ref, kdec_ref, st_sc)
        for h in range(RET_HEADS):
            or_ref[rows, h * RET_V_DIM:(h + 1) * RET_V_DIM] = outs[h]


def _inproj(x2, g, w_bf, wq_t, wv_t, cd, dec, qdec, kdec):
    C = RET_C
    t_shape = jax.ShapeDtypeStruct((DA_HEADS, SEQ // PROJ_TM, LANES, PROJ_TM), BF16)
    t_spec = pl.BlockSpec((DA_HEADS, 1, LANES, PROJ_TM), lambda i: (0, i, 0, 0))
    w_t_spec = pl.BlockSpec((T_ROWS, D_MODEL), lambda i: (0, 0))
    return pl.pallas_call(
        _inproj_kernel,
        out_shape=(jax.ShapeDtypeStruct((SEQ, DA_WIDTH), BF16), t_shape, t_shape,
                   jax.ShapeDtypeStruct((SEQ, RET_WIDTH), BF16)),
        grid=(SEQ // PROJ_TM,),
        in_specs=[
            pl.BlockSpec(memory_space=pltpu.SMEM),
            pl.BlockSpec((PROJ_TM, D_MODEL), lambda i: (i, 0)),
            pl.BlockSpec((1, D_MODEL), lambda i: (0, 0)),
            pl.BlockSpec((D_MODEL, MAIN_COLS), lambda i: (0, 0)),
            w_t_spec, w_t_spec,
            pl.BlockSpec((RET_HEADS, C, C), lambda i: (0, 0, 0)),
            pl.BlockSpec((RET_HEADS, C, 1), lambda i: (0, 0, 0)),
            pl.BlockSpec((RET_HEADS, C, 1), lambda i: (0, 0, 0)),
        ],
        out_specs=(pl.BlockSpec((PROJ_TM, DA_WIDTH), lambda i: (i, 0)), t_spec, t_spec,
                   pl.BlockSpec((PROJ_TM, RET_WIDTH), lambda i: (i, 0))),
        scratch_shapes=[pltpu.VMEM((RET_HEADS, RET_QK_DIM, RET_V_DIM), F32)],
        compiler_params=_params(("arbitrary",)),
        name="inproj_retention",
    )(cd, x2, g, w_bf, wq_t, wv_t, dec, qdec, kdec)


ACC_ROWS = DA_V_DIM + 16


N_QT = SEQ // ATT_T
N_OFF = N_QT * (N_QT - 1) // 2


def _pipeline3(n_pos, scores, accumulate):
    scores(0, 0)
    scores(1, 1)
    steady = n_pos - 2

    def triple(k, carry):
        t = 3 * k
        accumulate(t, 0)
        scores(t + 2, 2)
        accumulate(t + 1, 1)
        scores(t + 3, 0)
        accumulate(t + 2, 2)
        scores(t + 4, 1)
        return carry

    lax.fori_loop(0, steady // 3, triple, 0)
    t0 = steady // 3 * 3
    rem = steady - t0
    accumulate(t0, 0)
    if rem >= 1:
        scores(t0 + 2, 2)
    accumulate(t0 + 1, 1)
    if rem == 2:
        scores(t0 + 3, 0)
    if rem >= 1:
        accumulate(t0 + 2, 2)
    if rem == 2:
        accumulate(t0 + 3, 0)


def _attn_kernel(slope_ref, jt_ref, it_ref, qt_ref, k_ref, vt_ref, qa_ref, ka1_ref, ka2_ref,
                 dtab_ref, lq1_ref, lk1_ref, lq2_ref, lk2_ref, g_ref, o_ref,
                 m_sc, acc_sc, s0_sc, s1_sc, s2_sc, mx0_sc, mx1_sc, mx2_sc):
    T = ATT_T
    h = pl.program_id(0)
    slope = slope_ref[h]
    qa = qa_ref[0]
    lane = lax.broadcasted_iota(I32, (T, LANES), 1)
    sums_row = (lax.broadcasted_iota(I32, (16, T), 0) == 0).astype(BF16)
    s_bufs = (s0_sc, s1_sc, s2_sc)
    mx_bufs = (mx0_sc, mx1_sc, mx2_sc)

    def scores(j, i, buf, extra):
        kt = k_ref[pl.ds(pl.multiple_of(j * T, T), T), :]
        ks = (jnp.where(lane < DA_HEAD_DIM, kt, ka1_ref[0]),
              jnp.where(lane >= DA_HEAD_DIM, kt, ka2_ref[0]))
        qt = qt_ref[0, i]
        qw = (jnp.concatenate([qt[0:DA_HEAD_DIM], qa], axis=0),
              jnp.concatenate([qa, qt[DA_HEAD_DIM:]], axis=0))
        for mp in range(2):
            s = jnp.dot(ks[mp], qw[mp], preferred_element_type=F32)
            if extra is not None:
                s = s + extra[0]
            s_bufs[buf][mp] = s
            mx_bufs[buf][mp] = jnp.max(s, axis=0, keepdims=True)

    def accumulate(j, i, buf):
        c = slope * lax.convert_element_type((i - j) * T, F32)
        vte = jnp.concatenate([vt_ref[0, j], sums_row], axis=0)
        for mp in range(2):
            m_prev = m_sc[i, mp]
            m_new = jnp.maximum(m_prev, mx_bufs[buf][mp] - c)
            p = jnp.exp(s_bufs[buf][mp] - (m_new + c)).astype(BF16)
            pv = jnp.dot(vte, p, preferred_element_type=F32)
            acc_sc[i, mp] = jnp.exp(m_prev - m_new) * acc_sc[i, mp] + pv
            m_sc[i, mp] = m_new

    m_sc[...] = jnp.full_like(m_sc, -jnp.inf)
    acc_sc[...] = jnp.zeros_like(acc_sc)
    _pipeline3(N_QT,
               lambda pos, buf: scores(pos, pos, buf, dtab_ref),
               lambda pos, buf: accumulate(pos, pos, buf))
    _pipeline3(N_OFF,
               lambda pos, buf: scores(jt_ref[pos], it_ref[pos], buf, None),
               lambda pos, buf: accumulate(jt_ref[pos], it_ref[pos], buf))

    lam = (jnp.exp(jnp.sum(lq1_ref[...] * lk1_ref[...], axis=1, keepdims=True))
           - jnp.exp(jnp.sum(lq2_ref[...] * lk2_ref[...], axis=1, keepdims=True))
           + LAMBDA_INIT)

    def finish(i, carry):
        a1 = acc_sc[i, 0]
        a2 = acc_sc[i, 1]
        ot = (a1[0:DA_V_DIM] / a1[DA_V_DIM:DA_V_DIM + 1]
              - lam * (a2[0:DA_V_DIM] / a2[DA_V_DIM:DA_V_DIM + 1]))
        o = ot.T
        var = jnp.mean(o * o, axis=-1, keepdims=True)
        o = (o * lax.rsqrt(var + EPS) * g_ref[...]) * (1.0 - LAMBDA_INIT)
        o_ref[pl.ds(pl.multiple_of(i * T, T), T), :] = o.astype(BF16)
        return carry

    lax.fori_loop(0, N_QT, finish, 0)


def _attention(proj, qt4, vt4, slopes, qa, ka1, ka2, dtab, lq1, lk1, lq2, lk2, subln_g):
    T = ATT_T
    vec64 = pl.BlockSpec((1, DA_HEAD_DIM), lambda h: (0, 0))
    per_head = lambda a, b: pl.BlockSpec((1, a, b), lambda h: (h, 0, 0))
    slab = lambda shape, imap: pl.BlockSpec(shape, imap, pipeline_mode=pl.Buffered(1))
    smem = pl.BlockSpec(memory_space=pltpu.SMEM)
    it_tab, jt_tab = np.tril_indices(N_QT, -1)
    return pl.pallas_call(
        _attn_kernel,
        out_shape=jax.ShapeDtypeStruct((SEQ, DA_WIDTH), BF16),
        grid=(DA_HEADS,),
        in_specs=[
            smem, smem, smem,
            slab((1, N_QT, LANES, T), lambda h: (h, 0, 0, 0)),
            slab((SEQ, LANES), lambda h: (0, DK_OFF // LANES + h)),
            slab((1, N_QT, LANES, T), lambda h: (h, 0, 0, 0)),
            per_head(DA_HEAD_DIM, T), per_head(T, LANES), per_head(T, LANES), per_head(T, T),
            vec64, vec64, vec64, vec64,
            pl.BlockSpec((1, DA_V_DIM), lambda h: (0, 0)),
        ],
        out_specs=slab((SEQ, LANES), lambda h: (0, h)),
        scratch_shapes=[
            pltpu.VMEM((N_QT, 2, 1, T), F32),
            pltpu.VMEM((N_QT, 2, ACC_ROWS, T), F32),
            pltpu.VMEM((2, T, T), F32),
            pltpu.VMEM((2, T, T), F32),
            pltpu.VMEM((2, T, T), F32),
            pltpu.VMEM((2, 1, T), F32),
            pltpu.VMEM((2, 1, T), F32),
            pltpu.VMEM((2, 1, T), F32),
        ],
        compiler_params=_params(("arbitrary",)),
        name="diff_attention",
    )(slopes, jnp.asarray(jt_tab, I32), jnp.asarray(it_tab, I32), qt4, proj, vt4, qa, ka1, ka2,
      dtab, lq1, lk1, lq2, lk2, subln_g)


def _outproj_router_kernel(x_ref, oda_ref, or_ref, wo_ref, g_ref, wr_ref, br_ref,
                           h1_ref, xn_ref, ri_ref, rw_ref):
    h1 = (x_ref[...]
          + jnp.dot(oda_ref[...], wo_ref[0:DA_WIDTH, :], preferred_element_type=F32)
          + jnp.dot(or_ref[...], wo_ref[DA_WIDTH:, :], preferred_element_type=F32))
    h1_ref[...] = h1
    var = jnp.mean(h1 * h1, axis=-1, keepdims=True)
    xn = h1 * lax.rsqrt(var + EPS) * g_ref[...]
    for c in range(ROW_TILE):
        xn_ref[pl.ds(c, PROJ_TM, stride=ROW_TILE), :] = xn[:, c * LANES:(c + 1) * LANES]
    x_hi = xn.astype(BF16)
    x_lo = (xn - x_hi.astype(F32)).astype(BF16)
    both = jnp.dot(x_hi, wr_ref[...], preferred_element_type=F32)
    logits = (both[:, :LANES] + both[:, LANES:]
              + jnp.dot(x_lo, wr_ref[:, :LANES], preferred_element_type=F32)) + br_ref[...]
    lane = lax.broadcasted_iota(I32, logits.shape, 1)
    neg = jnp.float32(-jnp.inf)
    big = jnp.int32(1 << 20)
    gl = jnp.where(lane < MOE_GROUPS, logits, neg)
    gmax = jnp.max(gl, axis=1, keepdims=True)
    gidx = jnp.min(jnp.where(gl == gmax, lane, big), axis=1, keepdims=True)
    gsum = jnp.sum(jnp.exp(gl - gmax), axis=1, keepdims=True)
    gp = 1.0 / gsum
    lo = MOE_GROUPS + gidx * MOE_EXPERTS_PER_GROUP
    el = jnp.where((lane >= lo) & (lane < lo + MOE_EXPERTS_PER_GROUP), logits, neg)
    v1 = jnp.max(el, axis=1, keepdims=True)
    i1 = jnp.min(jnp.where(el == v1, lane, big), axis=1, keepdims=True)
    el2 = jnp.where(lane == i1, neg, el)
    v2 = jnp.max(el2, axis=1, keepdims=True)
    i2 = jnp.min(jnp.where(el2 == v2, lane, big), axis=1, keepdims=True)
    t = jnp.exp(v2 - v1)
    w1 = gp / (1.0 + t)
    w2 = gp * t / (1.0 + t)
    ri_ref[...] = jnp.where(lane == 0, i1 - MOE_GROUPS,
                            jnp.where(lane == 1, i2 - MOE_GROUPS, 0))
    rw_ref[...] = jnp.where(lane == 0, w1, jnp.where(lane == 1, w2, 0.0))


def _outproj_router(x2, o_da, o_r, wo_bf, g, wr, br):
    tm = PROJ_TM
    row = lambda w: pl.BlockSpec((tm, w), lambda i: (i, 0))
    full = lambda a, b: pl.BlockSpec((a, b), lambda i: (0, 0))
    return pl.pallas_call(
        _outproj_router_kernel,
        out_shape=(
            jax.ShapeDtypeStruct((SEQ, D_MODEL), F32),
            jax.ShapeDtypeStruct((SEQ * ROW_TILE, LANES), F32),
            jax.ShapeDtypeStruct((SEQ, LANES), I32),
            jax.ShapeDtypeStruct((SEQ, LANES), F32),
        ),
        grid=(SEQ // tm,),
        in_specs=[row(D_MODEL), row(DA_WIDTH), row(RET_WIDTH), full(D_MODEL, D_MODEL),
                  full(1, D_MODEL), full(D_MODEL, 2 * LANES), full(1, LANES)],
        out_specs=(row(D_MODEL), pl.BlockSpec((tm * ROW_TILE, LANES), lambda i: (i, 0)),
                   row(LANES), row(LANES)),
        compiler_params=_params(("arbitrary",)),
        name="outproj_router",
    )(x2, o_da, o_r, wo_bf, g, wr, br)


def _plan_kernel(ri_ref, dest_ref, used_ref):
    TT = PLAN_T
    lane = lax.broadcasted_iota(I32, (TT, LANES), 1)

    def onehots(t):
        r = ri_ref[pl.ds(pl.multiple_of(t * TT, TT), TT), :]
        return lane == r[:, 0:1], lane == r[:, 1:2]

    def count_body(t, acc):
        oh1, oh2 = onehots(t)
        return acc + jnp.sum((oh1 | oh2).astype(F32), axis=0, keepdims=True)

    counts = lax.fori_loop(0, SEQ // TT, count_body, jnp.zeros((1, LANES), F32))
    counts8 = jnp.broadcast_to(counts, (8, LANES)).astype(I32)
    shift = FFN_B.bit_length() - 1
    padded = ((counts8 + (FFN_B - 1)) >> shift) << shift
    lane8 = lax.broadcasted_iota(I32, (8, LANES), 1)
    pad_end = padded
    sh = 1
    while sh < LANES:
        pad_end = pad_end + jnp.where(lane8 >= sh, pltpu.roll(pad_end, sh, axis=1), 0)
        sh *= 2
    pad_start = pad_end - padded

    ltri = (lax.broadcasted_iota(I32, (TT, TT), 0)
            > lax.broadcasted_iota(I32, (TT, TT), 1)).astype(BF16)

    def dest_body(t, carry):
        oh1, oh2 = onehots(t)
        a = (oh1 | oh2).astype(F32)
        base = jnp.dot(ltri, a.astype(BF16), preferred_element_type=F32) + carry
        d1 = jnp.sum(jnp.where(oh1, base, 0.0), axis=1, keepdims=True)
        d2 = jnp.sum(jnp.where(oh2, base, 0.0), axis=1, keepdims=True)
        dest_ref[pl.ds(pl.multiple_of(t * TT, TT), TT), :] = jnp.where(
            lane == 0, d1, jnp.where(lane == 1, d2, 0.0)).astype(I32)
        return carry + jnp.sum(a, axis=0, keepdims=True)

    lax.fori_loop(0, SEQ // TT, dest_body, pad_start[0:1].astype(F32))

    total = jnp.max(pad_end, axis=1, keepdims=True)
    row8 = lax.broadcasted_iota(I32, (8, LANES), 0)
    used_ref[...] = jnp.where(row8 == 0, jnp.broadcast_to(total >> shift, (8, LANES)),
                              jnp.where(row8 == 1, counts8, pad_start))


def _plan(ri):
    return pl.pallas_call(
        _plan_kernel,
        out_shape=(
            jax.ShapeDtypeStruct((SEQ, LANES), I32),
            jax.ShapeDtypeStruct((8, LANES), I32),
        ),
        compiler_params=pltpu.CompilerParams(vmem_limit_bytes=VMEM_LIMIT),
        name="route_plan",
    )(ri)


PAD_BITS = FFN_B.bit_length() - 1


def _pad_fill_copies(e, cnt_ref, pst_ref, zero_sc, xs_hbm, zsem):
    cnt = cnt_ref[e]
    pad = (-cnt) & (FFN_B - 1)
    row = pst_ref[e] + cnt
    out = []
    for bit in reversed(range(PAD_BITS)):
        n = 1 << bit
        start = row + ((pad >> (bit + 1)) << (bit + 1))
        copy = pltpu.make_async_copy(
            zero_sc.at[pl.ds(0, n * ROW_TILE)],
            xs_hbm.at[pl.ds(pl.multiple_of(start * ROW_TILE, ROW_TILE), n * ROW_TILE)], zsem)
        out.append(((pad & n) != 0, copy))
    return out


def _unused_block_copies(b, zero_sc, xs_hbm, zsem):
    half = FFN_B // 2 * ROW_TILE
    return [pltpu.make_async_copy(
        zero_sc, xs_hbm.at[pl.ds(pl.multiple_of((2 * b + k) * half, half), half)], zsem)
        for k in range(2)]


def _dispatch_kernel(dest_ref, cnt_ref, pst_ref, used_ref, xn_ref, xs_hbm, zero_sc, sem, zsem):
    tm = DISP_TM
    i = pl.program_id(0)

    @pl.when(i == 0)
    def _():
        zero_sc[...] = jnp.zeros_like(zero_sc)

        def fill(e, carry):
            for cond, copy in _pad_fill_copies(e, cnt_ref, pst_ref, zero_sc, xs_hbm, zsem):
                pl.when(cond)(copy.start)
            return carry

        lax.fori_loop(0, MOE_EXPERTS, fill, 0)

        def fill_block(b, carry):
            for copy in _unused_block_copies(b, zero_sc, xs_hbm, zsem):
                copy.start()
            return carry

        lax.fori_loop(used_ref[0], N_BLOCKS, fill_block, 0)

    def issue(it, carry):
        for u in range(DMA_UNROLL):
            r = it * DMA_UNROLL + u
            a = (i * tm + r) * 2
            src = xn_ref.at[pl.ds(pl.multiple_of(r * ROW_TILE, ROW_TILE), ROW_TILE)]
            for kk in range(2):
                d = pl.multiple_of(dest_ref[a + kk] * ROW_TILE, ROW_TILE)
                pltpu.make_async_copy(src, xs_hbm.at[pl.ds(d, ROW_TILE)], sem).start()
        return carry

    lax.fori_loop(0, tm // DMA_UNROLL, issue, 0)
    for _ in range(2):
        pltpu.make_async_copy(xn_ref, xs_hbm.at[pl.ds(0, tm * ROW_TILE)], sem).wait()

    @pl.when(i == 0)
    def _():
        def drain(e, carry):
            for cond, copy in _pad_fill_copies(e, cnt_ref, pst_ref, zero_sc, xs_hbm, zsem):
                pl.when(cond)(copy.wait)
            return carry

        lax.fori_loop(0, MOE_EXPERTS, drain, 0)

        def drain_block(b, carry):
            for copy in _unused_block_copies(b, zero_sc, xs_hbm, zsem):
                copy.wait()
            return carry

        lax.fori_loop(used_ref[0], N_BLOCKS, drain_block, 0)


def _dispatch(dest_flat, counts, pad_start, used, xn3):
    tm = DISP_TM
    return pl.pallas_call(
        _dispatch_kernel,
        out_shape=jax.ShapeDtypeStruct((N_BUF * ROW_TILE, LANES), F32),
        grid_spec=pltpu.PrefetchScalarGridSpec(
            num_scalar_prefetch=4,
            grid=(SEQ // tm,),
            in_specs=[pl.BlockSpec((tm * ROW_TILE, LANES), lambda i, d, c, p, u: (i, 0))],
            out_specs=pl.BlockSpec(memory_space=pl.ANY),
            scratch_shapes=[
                pltpu.VMEM((FFN_B // 2 * ROW_TILE, LANES), F32),
                pltpu.SemaphoreType.DMA(()),
                pltpu.SemaphoreType.DMA(()),
            ],
        ),
        compiler_params=_params(("arbitrary",)),
        name="moe_dispatch",
    )(dest_flat, counts, pad_start, used, xn3)


def _ffn_kernel(cnt_ref, pst_ref, used_ref, xs_hbm, wg_ref, wu_ref, wd_ref, y_hbm,
                xbuf, ybuf, zero_sc, wg_bf, wu_bf, wd_bf, sem_in, sem_out, zsem):
    B = FFN_B
    R = B * ROW_TILE
    e = pl.program_id(0)
    n = (cnt_ref[e] + (B - 1)) >> PAD_BITS
    s0 = pst_ref[e] >> PAD_BITS

    def rows(blk):
        return pl.ds(pl.multiple_of(blk * R, R), R)

    def fetch(blk, slot):
        return pltpu.make_async_copy(xs_hbm.at[rows(blk)], xbuf.at[slot], sem_in.at[slot])

    def flush(blk, slot):
        return pltpu.make_async_copy(ybuf.at[slot], y_hbm.at[rows(blk)], sem_out.at[slot])

    @pl.when(n > 0)
    def _():
        fetch(s0, 0).start()
        wg_bf[...] = wg_ref[0].astype(BF16)
        wu_bf[...] = wu_ref[0].astype(BF16)
        wd_bf[...] = wd_ref[0].astype(BF16)

    def body(j, carry):
        slot = j % 2

        @pl.when(j + 1 < n)
        def _():
            fetch(s0 + j + 1, 1 - slot).start()

        fetch(s0 + j, slot).wait()

        @pl.when(j >= 2)
        def _():
            flush(s0 + j - 2, slot).wait()

        x = jnp.concatenate([xbuf[slot, pl.ds(c, B, stride=ROW_TILE), :]
                             for c in range(ROW_TILE)], axis=1).astype(BF16)
        hg = jnp.dot(x, wg_bf[...], preferred_element_type=F32)
        hu = jnp.dot(x, wu_bf[...], preferred_element_type=F32)
        hh = ((hg / (1.0 + jnp.exp(-hg))) * hu).astype(BF16)
        y = jnp.dot(hh, wd_bf[...], preferred_element_type=F32)
        for c in range(ROW_TILE):
            ybuf[slot, pl.ds(c, B, stride=ROW_TILE), :] = y[:, c * LANES:(c + 1) * LANES]
        flush(s0 + j, slot).start()
        return carry

    lax.fori_loop(0, n, body, 0)

    @pl.when(n >= 1)
    def _():
        flush(s0 + n - 1, (n + 1) % 2).wait()

    @pl.when(n >= 2)
    def _():
        flush(s0 + n - 2, n % 2).wait()

    @pl.when(e == MOE_EXPERTS - 1)
    def _():
        zero_sc[...] = jnp.zeros_like(zero_sc)

        def fill(b, carry):
            pltpu.make_async_copy(zero_sc, y_hbm.at[rows(b)], zsem).start()
            return carry

        def drain(b, carry):
            pltpu.make_async_copy(zero_sc, y_hbm.at[rows(b)], zsem).wait()
            return carry

        lax.fori_loop(used_ref[0], N_BLOCKS, fill, 0)
        lax.fori_loop(used_ref[0], N_BLOCKS, drain, 0)


def _ffn(counts, pad_start, used, xs, w_gate, w_up, w_down):
    B = FFN_B
    wspec = lambda a, c: pl.BlockSpec((1, a, c), lambda e, cnt, pst, used: (e, 0, 0))
    return pl.pallas_call(
        _ffn_kernel,
        out_shape=jax.ShapeDtypeStruct((N_BUF * ROW_TILE, LANES), F32),
        grid_spec=pltpu.PrefetchScalarGridSpec(
            num_scalar_prefetch=3,
            grid=(MOE_EXPERTS,),
            in_specs=[
                pl.BlockSpec(memory_space=pl.ANY),
                wspec(D_MODEL, MOE_HIDDEN),
                wspec(D_MODEL, MOE_HIDDEN),
                wspec(MOE_HIDDEN, D_MODEL),
            ],
            out_specs=pl.BlockSpec(memory_space=pl.ANY),
            scratch_shapes=[
                pltpu.VMEM((2, B * ROW_TILE, LANES), F32),
                pltpu.VMEM((2, B * ROW_TILE, LANES), F32),
                pltpu.VMEM((B * ROW_TILE, LANES), F32),
                pltpu.VMEM((D_MODEL, MOE_HIDDEN), BF16),
                pltpu.VMEM((D_MODEL, MOE_HIDDEN), BF16),
                pltpu.VMEM((MOE_HIDDEN, D_MODEL), BF16),
                pltpu.SemaphoreType.DMA((2,)),
                pltpu.SemaphoreType.DMA((2,)),
                pltpu.SemaphoreType.DMA(()),
            ],
        ),
        compiler_params=_params(("arbitrary",)),
        name="expert_ffn",
    )(counts, pad_start, used, xs, w_gate, w_up, w_down)


def _combine_kernel(dest_ref, h1_ref, rw_ref, g_ref, y_hbm, o_ref, ybuf, sem):
    tm = COMB_TM
    i = pl.program_id(0)

    def gather(tile, slot):
        def issue(it, carry):
            for u in range(DMA_UNROLL):
                r = it * DMA_UNROLL + u
                a = (tile * tm + r) * 2
                for kk in range(2):
                    d = pl.multiple_of(dest_ref[a + kk] * ROW_TILE, ROW_TILE)
                    pltpu.make_async_copy(
                        y_hbm.at[pl.ds(d, ROW_TILE)],
                        ybuf.at[slot, kk, pl.ds(pl.multiple_of(r * ROW_TILE, ROW_TILE), ROW_TILE)],
                        sem.at[slot, kk]).start()
            return carry

        lax.fori_loop(0, tm // DMA_UNROLL, issue, 0)

    @pl.when(i == 0)
    def _():
        gather(0, 0)

    @pl.when(i + 1 < pl.num_programs(0))
    def _():
        gather(i + 1, (i + 1) % 2)

    slot = i % 2
    for kk in range(2):
        pltpu.make_async_copy(y_hbm.at[pl.ds(0, tm * ROW_TILE)], ybuf.at[slot, kk],
                              sem.at[slot, kk]).wait()
    w = rw_ref[...]
    ys = [jnp.concatenate([ybuf[slot, kk, pl.ds(c, tm, stride=ROW_TILE), :]
                           for c in range(ROW_TILE)], axis=1) for kk in range(2)]
    h = h1_ref[...] + w[:, 0:1] * ys[0] + w[:, 1:2] * ys[1]
    var = jnp.mean(h * h, axis=-1, keepdims=True)
    o_ref[...] = h * lax.rsqrt(var + EPS) * g_ref[...]


def _combine(dest_flat, h1, rw, g, y):
    tm = COMB_TM
    return pl.pallas_call(
        _combine_kernel,
        out_shape=jax.ShapeDtypeStruct((SEQ, D_MODEL), F32),
        grid_spec=pltpu.PrefetchScalarGridSpec(
            num_scalar_prefetch=1,
            grid=(SEQ // tm,),
            in_specs=[
                pl.BlockSpec((tm, D_MODEL), lambda i, d: (i, 0)),
                pl.BlockSpec((tm, LANES), lambda i, d: (i, 0)),
                pl.BlockSpec((1, D_MODEL), lambda i, d: (0, 0)),
                pl.BlockSpec(memory_space=pl.ANY),
            ],
            out_specs=pl.BlockSpec((tm, D_MODEL), lambda i, d: (i, 0)),
            scratch_shapes=[
                pltpu.VMEM((2, 2, tm * ROW_TILE, LANES), F32),
                pltpu.SemaphoreType.DMA((2, 2)),
            ],
        ),
        compiler_params=_params(("arbitrary",)),
        name="moe_combine",
    )(dest_flat, h1, rw, g, y)


def _attention_tables():
    T = ATT_T
    slopes = jnp.exp2(-ALIBI_MAX * jnp.arange(1, DA_HEADS + 1, dtype=F32) / DA_HEADS)
    r = jnp.arange(T)
    hi = ((r // CHUNK) * CHUNK).astype(F32)
    lo = (r % CHUNK).astype(F32)
    ones = jnp.ones((T,), F32)
    sl = slopes[:, None]
    one_h = jnp.broadcast_to(ones, (DA_HEADS, T))
    q_rows = jnp.stack([one_h, one_h, -sl * hi[None], -sl * lo[None]], axis=1)
    k_cols = jnp.stack([sl * hi[None], sl * lo[None], one_h, one_h], axis=-1)
    qa = jnp.zeros((DA_HEADS, DA_HEAD_DIM, T), F32).at[:, 0:4, :].set(q_rows)
    ka1 = jnp.zeros((DA_HEADS, T, LANES), F32).at[:, :, DA_HEAD_DIM:DA_HEAD_DIM + 4].set(k_cols)
    ka2 = jnp.zeros((DA_HEADS, T, LANES), F32).at[:, :, 0:4].set(k_cols)
    rel = (r[:, None] - r[None, :]).astype(F32)
    allowed = (r[:, None] // CHUNK) <= (r[None, :] // CHUNK)
    fix = jnp.where(rel > 0, -2.0 * slopes[:, None, None] * rel[None], 0.0)
    dtab = jnp.where(allowed[None], fix, -jnp.inf)
    return slopes, qa.astype(BF16), ka1.astype(BF16), ka2.astype(BF16), dtab


def _retention_tables():
    C = RET_C
    log_gamma = jnp.log1p(-jnp.exp2(-5.0 - jnp.arange(RET_HEADS, dtype=F32)))
    pos = jnp.arange(C, dtype=F32)
    rel = pos[:, None] - pos[None, :]
    dec = jnp.where(rel >= 0, jnp.exp(log_gamma[:, None, None] * jnp.maximum(rel, 0.0)), 0.0)
    qdec = jnp.exp(log_gamma[:, None] * (pos + 1.0)[None, :])[:, :, None]
    kdec = jnp.exp(log_gamma[:, None] * (C - 1 - pos)[None, :])[:, :, None]
    cd = jnp.exp(log_gamma * C)
    return cd, dec, qdec, kdec


def kernel(x, attn_norm_g, w_in, da_lambda_q1, da_lambda_k1, da_lambda_q2, da_lambda_k2,
           da_subln_g, w_out, ffn_norm_g, router_group_w, router_group_b, router_expert_w,
           router_expert_b, expert_w_gate, expert_w_up, expert_w_down, final_norm_g):
    B, S, D = x.shape
    assert (B, S, D) == (1, SEQ, D_MODEL)
    x2 = x.reshape(S, D)

    w = w_in[0]
    w_main = jnp.concatenate([
        w[:, 512:1024],
        w[:, 1536:1792],
        w[:, 1792:2048] * (RET_QK_DIM ** -0.5),
        w[:, 2048:3072]], axis=1).astype(BF16)
    wq_t = (w[:, 0:512] * (DA_HEAD_DIM ** -0.5)).T.astype(BF16)
    wv_t = w[:, 1024:1536].T.astype(BF16)
    cd, dec, qdec, kdec = _retention_tables()
    k_da, qt4, vt4, o_r = _inproj(x2, attn_norm_g[0][None, :], w_main, wq_t, wv_t,
                                  cd, dec, qdec, kdec)

    slopes, qa, ka1, ka2, dtab = _attention_tables()
    o_da = _attention(k_da, qt4, vt4, slopes, qa, ka1, ka2, dtab, da_lambda_q1, da_lambda_k1,
                      da_lambda_q2, da_lambda_k2, da_subln_g)

    wr = jnp.zeros((D, LANES), F32)
    wr = wr.at[:, :MOE_GROUPS].set(router_group_w[0])
    wr = wr.at[:, MOE_GROUPS:MOE_GROUPS + MOE_EXPERTS].set(router_expert_w[0])
    br = jnp.zeros((1, LANES), F32)
    br = br.at[0, :MOE_GROUPS].set(router_group_b[0])
    br = br.at[0, MOE_GROUPS:MOE_GROUPS + MOE_EXPERTS].set(router_expert_b[0])
    wr_hi = wr.astype(BF16)
    wr_lo = (wr - wr_hi.astype(F32)).astype(BF16)
    h1, xn, ri, rw = _outproj_router(x2, o_da, o_r, w_out[0].astype(BF16), ffn_norm_g[0][None, :],
                                     jnp.concatenate([wr_hi, wr_lo], axis=1), br)

    dest, meta = _plan(ri)
    dest_flat = dest[:, :2].reshape(N_ASSIGN)
    used1 = meta[0, :1]
    counts = meta[1, :MOE_EXPERTS]
    pad_start = meta[2, :MOE_EXPERTS]
    xs = _dispatch(dest_flat, counts, pad_start, used1, xn)

    y = _ffn(counts, pad_start, used1, xs, expert_w_gate[0], expert_w_up[0], expert_w_down[0])
    out = _combine(dest_flat, h1, rw, final_norm_g[None, :], y)
    return out.reshape(B, S, D)
```

```python
import functools
import math

import jax
import jax.numpy as jnp
import numpy as np
from jax import lax
from jax.experimental import pallas as pl
from jax.experimental.pallas import tpu as pltpu

F32 = jnp.float32
BF16 = jnp.bfloat16
I32 = jnp.int32

D_MODEL = 1024
SEQ = 16384
CHUNK = 64
EPS = 1e-6

DA_HEADS = 4
DA_HEAD_DIM = 64
DA_V_DIM = 128
DA_WIDTH = 512
ALIBI_MAX = 8.0
RET_HEADS = 4
RET_QK_DIM = 64
RET_V_DIM = 128
RET_WIDTH = 512
T_ROWS = 512
MAIN_COLS = 2048
DK_OFF = 0
RQ_OFF = 512
RK_OFF = 768
RV_OFF = 1024
RG_OFF = 1536

MOE_GROUPS = 4
MOE_EXPERTS_PER_GROUP = 8
MOE_EXPERTS = 32
MOE_HIDDEN = 512
LAMBDA_INIT = 0.8 - 0.6 * math.exp(-0.3 * 0)

LANES = 128
ROW_TILE = 8
VMEM_LIMIT = 56 * 1024 * 1024

PROJ_TM = 512
ATT_T = 512
RET_C = 256
PLAN_T = 512
FFN_B = 256
N_ASSIGN = 2 * SEQ
N_BLOCKS = N_ASSIGN // FFN_B + MOE_EXPERTS
N_BUF = N_BLOCKS * FFN_B
COMB_TM = 256
DISP_TM = 512
DMA_UNROLL = 8
W_CHUNKS = 4


def _params(sem):
    return pltpu.CompilerParams(dimension_semantics=sem, vmem_limit_bytes=VMEM_LIMIT)


def _retention_block(q_all, k_all, v_all, g_all, cd_ref, dec_ref, qdec_ref, kdec_ref, st_sc):
    outs = []
    for h in range(RET_HEADS):
        qk = slice(h * RET_QK_DIM, (h + 1) * RET_QK_DIM)
        vv = slice(h * RET_V_DIM, (h + 1) * RET_V_DIM)
        q = q_all[:, qk]
        k = k_all[:, qk]
        v = v_all[:, vv]
        g = g_all[:, vv]
        s = lax.dot_general(q, k, (((1,), (1,)), ((), ())),
                            preferred_element_type=F32) * dec_ref[h]
        intra = jnp.dot(s.astype(BF16), v, preferred_element_type=F32)
        st = st_sc[h]
        cross = jnp.dot(q, st.astype(BF16), preferred_element_type=F32) * qdec_ref[h]
        kd = (k.astype(F32) * kdec_ref[h]).astype(BF16)
        st_sc[h] = st * cd_ref[h] + lax.dot_general(kd, v, (((0,), (0,)), ((), ())),
                                                    preferred_element_type=F32)
        o = intra + cross
        o = o * lax.rsqrt(jnp.mean(o * o, axis=-1, keepdims=True) + EPS)
        outs.append(((g / (1.0 + jnp.exp(-g))) * o).astype(BF16))
    return outs


def _inproj_kernel(cd_ref, x_ref, g_ref, w_ref, wq_ref, wv_ref, dec_ref, qdec_ref, kdec_ref,
                   k_ref, qt_ref, vt_ref, or_ref, st_sc):
    @pl.when(pl.program_id(0) == 0)
    def _():
        st_sc[...] = jnp.zeros_like(st_sc)

    x = x_ref[...]
    var = jnp.mean(x * x, axis=-1, keepdims=True)
    xn = (x * lax.rsqrt(var + EPS) * g_ref[...]).astype(BF16)

    def proj(lo, hi):
        return jnp.dot(xn, w_ref[:, lo:hi], preferred_element_type=F32)

    k_ref[...] = proj(DK_OFF, DK_OFF + DA_WIDTH).astype(BF16)
    nt = (((1,), (1,)), ((), ()))
    qt = lax.dot_general(wq_ref[...], xn, nt, preferred_element_type=F32)
    qt_ref[...] = qt.astype(BF16).reshape(DA_HEADS, 1, 2 * DA_HEAD_DIM, PROJ_TM)
    vt = lax.dot_general(wv_ref[...], xn, nt, preferred_element_type=F32)
    vt_ref[...] = vt.astype(BF16).reshape(DA_HEADS, 1, DA_V_DIM, PROJ_TM)

    rq = proj(RQ_OFF, RK_OFF).astype(BF16)
    rk = proj(RK_OFF, RV_OFF).astype(BF16)
    rv = proj(RV_OFF, RG_OFF).astype(BF16)
    rg = proj(RG_OFF, MAIN_COLS)
    for blk in range(PROJ_TM // RET_C):
        rows = slice(blk * RET_C, (blk + 1) * RET_C)
        outs = _retention_block(rq[rows], rk[rows], rv[rows], rg[rows],
                                cd_ref, dec_ref, qdec_ref, kdec_ref, st_sc)
        for h in range(RET_HEADS):
            or_ref[rows, h * RET_V_DIM:(h + 1) * RET_V_DIM] = outs[h]


def _inproj(x2, g, w_bf, wq_t, wv_t, cd, dec, qdec, kdec):
    C = RET_C
    t_shape = jax.ShapeDtypeStruct((DA_HEADS, SEQ // PROJ_TM, LANES, PROJ_TM), BF16)
    t_spec = pl.BlockSpec((DA_HEADS, 1, LANES, PROJ_TM), lambda i: (0, i, 0, 0))
    w_t_spec = pl.BlockSpec((T_ROWS, D_MODEL), lambda i: (0, 0))
    return pl.pallas_call(
        _inproj_kernel,
        out_shape=(jax.ShapeDtypeStruct((SEQ, DA_WIDTH), BF16), t_shape, t_shape,
                   jax.ShapeDtypeStruct((SEQ, RET_WIDTH), BF16)),
        grid=(SEQ // PROJ_TM,),
        in_specs=[
            pl.BlockSpec(memory_space=pltpu.SMEM),
            pl.BlockSpec((PROJ_TM, D_MODEL), lambda i: (i, 0)),
            pl.BlockSpec((1, D_MODEL), lambda i: (0, 0)),
            pl.BlockSpec((D_MODEL, MAIN_COLS), lambda i: (0, 0)),
            w_t_spec, w_t_spec,
            pl.BlockSpec((RET_HEADS, C, C), lambda i: (0, 0, 0)),
            pl.BlockSpec((RET_HEADS, C, 1), lambda i: (0, 0, 0)),
            pl.BlockSpec((RET_HEADS, C, 1), lambda i: (0, 0, 0)),
        ],
        out_specs=(pl.BlockSpec((PROJ_TM, DA_WIDTH), lambda i: (i, 0)), t_spec, t_spec,
                   pl.BlockSpec((PROJ_TM, RET_WIDTH), lambda i: (i, 0))),
        scratch_shapes=[pltpu.VMEM((RET_HEADS, RET_QK_DIM, RET_V_DIM), F32)],
        compiler_params=_params(("arbitrary",)),
        name="inproj_retention",
    )(cd, x2, g, w_bf, wq_t, wv_t, dec, qdec, kdec)


ACC_ROWS = DA_V_DIM + 16


N_QT = SEQ // ATT_T
N_OFF = N_QT * (N_QT - 1) // 2


def _pipeline3(n_pos, scores, accumulate):
    scores(0, 0)
    scores(1, 1)
    steady = n_pos - 2

    def triple(k, carry):
        t = 3 * k
        accumulate(t, 0)
        scores(t + 2, 2)
        accumulate(t + 1, 1)
        scores(t + 3, 0)
        accumulate(t + 2, 2)
        scores(t + 4, 1)
        return carry

    lax.fori_loop(0, steady // 3, triple, 0)
    t0 = steady // 3 * 3
    rem = steady - t0
    accumulate(t0, 0)
    if rem >= 1:
        scores(t0 + 2, 2)
    accumulate(t0 + 1, 1)
    if rem == 2:
        scores(t0 + 3, 0)
    if rem >= 1:
        accumulate(t0 + 2, 2)
    if rem == 2:
        accumulate(t0 + 3, 0)


def _attn_kernel(slope_ref, jt_ref, it_ref, qt_ref, k_ref, vt_ref, qa_ref, ka1_ref, ka2_ref,
                 dtab_ref, lq1_ref, lk1_ref, lq2_ref, lk2_ref, g_ref, o_ref,
                 m_sc, acc_sc, s0_sc, s1_sc, s2_sc, mx0_sc, mx1_sc, mx2_sc):
    T = ATT_T
    h = pl.program_id(0)
    slope = slope_ref[h]
    qa = qa_ref[0]
    lane = lax.broadcasted_iota(I32, (T, LANES), 1)
    sums_row = (lax.broadcasted_iota(I32, (16, T), 0) == 0).astype(BF16)
    s_bufs = (s0_sc, s1_sc, s2_sc)
    mx_bufs = (mx0_sc, mx1_sc, mx2_sc)

    def scores(j, i, buf, extra):
        kt = k_ref[pl.ds(pl.multiple_of(j * T, T), T), :]
        ks = (jnp.where(lane < DA_HEAD_DIM, kt, ka1_ref[0]),
              jnp.where(lane >= DA_HEAD_DIM, kt, ka2_ref[0]))
        qt = qt_ref[0, i]
        qw = (jnp.concatenate([qt[0:DA_HEAD_DIM], qa], axis=0),
              jnp.concatenate([qa, qt[DA_HEAD_DIM:]], axis=0))
        for mp in range(2):
            s = jnp.dot(ks[mp], qw[mp], preferred_element_type=F32)
            if extra is not None:
                s = s + extra[0]
            s_bufs[buf][mp] = s
            mx_bufs[buf][mp] = jnp.max(s, axis=0, keepdims=True)

    def accumulate(j, i, buf):
        c = slope * lax.convert_element_type((i - j) * T, F32)
        vte = jnp.concatenate([vt_ref[0, j], sums_row], axis=0)
        for mp in range(2):
            m_prev = m_sc[i, mp]
            m_new = jnp.maximum(m_prev, mx_bufs[buf][mp] - c)
            p = jnp.exp(s_bufs[buf][mp] - (m_new + c)).astype(BF16)
            pv = jnp.dot(vte, p, preferred_element_type=F32)
            acc_sc[i, mp] = jnp.exp(m_prev - m_new) * acc_sc[i, mp] + pv
            m_sc[i, mp] = m_new

    m_sc[...] = jnp.full_like(m_sc, -jnp.inf)
    acc_sc[...] = jnp.zeros_like(acc_sc)
    _pipeline3(N_QT,
               lambda pos, buf: scores(pos, pos, buf, dtab_ref),
               lambda pos, buf: accumulate(pos, pos, buf))
    _pipeline3(N_OFF,
               lambda pos, buf: scores(jt_ref[pos], it_ref[pos], buf, None),
               lambda pos, buf: accumulate(jt_ref[pos], it_ref[pos], buf))

    lam = (jnp.exp(jnp.sum(lq1_ref[...] * lk1_ref[...], axis=1, keepdims=True))
           - jnp.exp(jnp.sum(lq2_ref[...] * lk2_ref[...], axis=1, keepdims=True))
           + LAMBDA_INIT)

    def finish(i, carry):
        a1 = acc_sc[i, 0]
        a2 = acc_sc[i, 1]
        ot = (a1[0:DA_V_DIM] / a1[DA_V_DIM:DA_V_DIM + 1]
              - lam * (a2[0:DA_V_DIM] / a2[DA_V_DIM:DA_V_DIM + 1]))
        o = ot.T
        var = jnp.mean(o * o, axis=-1, keepdims=True)
        o = (o * lax.rsqrt(var + EPS) * g_ref[...]) * (1.0 - LAMBDA_INIT)
        o_ref[pl.ds(pl.multiple_of(i * T, T), T), :] = o.astype(BF16)
        return carry

    lax.fori_loop(0, N_QT, finish, 0)


def _attention(proj, qt4, vt4, slopes, qa, ka1, ka2, dtab, lq1, lk1, lq2, lk2, subln_g):
    T = ATT_T
    vec64 = pl.BlockSpec((1, DA_HEAD_DIM), lambda h: (0, 0))
    per_head = lambda a, b: pl.BlockSpec((1, a, b), lambda h: (h, 0, 0))
    slab = lambda shape, imap: pl.BlockSpec(shape, imap, pipeline_mode=pl.Buffered(1))
    smem = pl.BlockSpec(memory_space=pltpu.SMEM)
    it_tab, jt_tab = np.tril_indices(N_QT, -1)
    return pl.pallas_call(
        _attn_kernel,
        out_shape=jax.ShapeDtypeStruct((SEQ, DA_WIDTH), BF16),
        grid=(DA_HEADS,),
        in_specs=[
            smem, smem, smem,
            slab((1, N_QT, LANES, T), lambda h: (h, 0, 0, 0)),
            slab((SEQ, LANES), lambda h: (0, DK_OFF // LANES + h)),
            slab((1, N_QT, LANES, T), lambda h: (h, 0, 0, 0)),
            per_head(DA_HEAD_DIM, T), per_head(T, LANES), per_head(T, LANES), per_head(T, T),
            vec64, vec64, vec64, vec64,
            pl.BlockSpec((1, DA_V_DIM), lambda h: (0, 0)),
        ],
        out_specs=slab((SEQ, LANES), lambda h: (0, h)),
        scratch_shapes=[
            pltpu.VMEM((N_QT, 2, 1, T), F32),
            pltpu.VMEM((N_QT, 2, ACC_ROWS, T), F32),
            pltpu.VMEM((2, T, T), F32),
            pltpu.VMEM((2, T, T), F32),
            pltpu.VMEM((2, T, T), F32),
            pltpu.VMEM((2, 1, T), F32),
            pltpu.VMEM((2, 1, T), F32),
            pltpu.VMEM((2, 1, T), F32),
        ],
        compiler_params=_params(("arbitrary",)),
        name="diff_attention",
    )(slopes, jnp.asarray(jt_tab, I32), jnp.asarray(it_tab, I32), qt4, proj, vt4, qa, ka1, ka2,
      dtab, lq1, lk1, lq2, lk2, subln_g)


def _outproj_router_kernel(x_ref, oda_ref, or_ref, wo_ref, g_ref, wr_ref, br_ref,
                           h1_ref, xn_ref, ri_ref, rw_ref):
    h1 = (x_ref[...]
          + jnp.dot(oda_ref[...], wo_ref[0:DA_WIDTH, :], preferred_element_type=F32)
          + jnp.dot(or_ref[...], wo_ref[DA_WIDTH:, :], preferred_element_type=F32))
    h1_ref[...] = h1
    var = jnp.mean(h1 * h1, axis=-1, keepdims=True)
    xn = h1 * lax.rsqrt(var + EPS) * g_ref[...]
    for c in range(ROW_TILE):
        xn_ref[pl.ds(c, PROJ_TM, stride=ROW_TILE), :] = xn[:, c * LANES:(c + 1) * LANES]
    x_hi = xn.astype(BF16)
    x_lo = (xn - x_hi.astype(F32)).astype(BF16)
    both = jnp.dot(x_hi, wr_ref[...], preferred_element_type=F32)
    logits = (both[:, :LANES] + both[:, LANES:]
              + jnp.dot(x_lo, wr_ref[:, :LANES], preferred_element_type=F32)) + br_ref[...]
    lane = lax.broadcasted_iota(I32, logits.shape, 1)
    neg = jnp.float32(-jnp.inf)
    big = jnp.int32(1 << 20)
    gl = jnp.where(lane < MOE_GROUPS, logits, neg)
    gmax = jnp.max(gl, axis=1, keepdims=True)
    gidx = jnp.min(jnp.where(gl == gmax, lane, big), axis=1, keepdims=True)
    gsum = jnp.sum(jnp.exp(gl - gmax), axis=1, keepdims=True)
    gp = 1.0 / gsum
    lo = MOE_GROUPS + gidx * MOE_EXPERTS_PER_GROUP
    el = jnp.where((lane >= lo) & (lane < lo + MOE_EXPERTS_PER_GROUP), logits, neg)
    v1 = jnp.max(el, axis=1, keepdims=True)
    i1 = jnp.min(jnp.where(el == v1, lane, big), axis=1, keepdims=True)
    el2 = jnp.where(lane == i1, neg, el)
    v2 = jnp.max(el2, axis=1, keepdims=True)
    i2 = jnp.min(jnp.where(el2 == v2, lane, big), axis=1, keepdims=True)
    t = jnp.exp(v2 - v1)
    w1 = gp / (1.0 + t)
    w2 = gp * t / (1.0 + t)
    ri_ref[...] = jnp.where(lane == 0, i1 - MOE_GROUPS,
                            jnp.where(lane == 1, i2 - MOE_GROUPS, 0))
    rw_ref[...] = jnp.where(lane == 0, w1, jnp.where(lane == 1, w2, 0.0))


def _outproj_router(x2, o_da, o_r, wo_bf, g, wr, br):
    tm = PROJ_TM
    row = lambda w: pl.BlockSpec((tm, w), lambda i: (i, 0))
    full = lambda a, b: pl.BlockSpec((a, b), lambda i: (0, 0))
    return pl.pallas_call(
        _outproj_router_kernel,
        out_shape=(
            jax.ShapeDtypeStruct((SEQ, D_MODEL), F32),
            jax.ShapeDtypeStruct((SEQ * ROW_TILE, LANES), F32),
            jax.ShapeDtypeStruct((SEQ, LANES), I32),
            jax.ShapeDtypeStruct((SEQ, LANES), F32),
        ),
        grid=(SEQ // tm,),
        in_specs=[row(D_MODEL), row(DA_WIDTH), row(RET_WIDTH), full(D_MODEL, D_MODEL),
                  full(1, D_MODEL), full(D_MODEL, 2 * LANES), full(1, LANES)],
        out_specs=(row(D_MODEL), pl.BlockSpec((tm * ROW_TILE, LANES), lambda i: (i, 0)),
                   row(LANES), row(LANES)),
        compiler_params=_params(("arbitrary",)),
        name="outproj_router",
    )(x2, o_da, o_r, wo_bf, g, wr, br)


def _plan_kernel(ri_ref, dest_ref, used_ref):
    TT = PLAN_T
    lane = lax.broadcasted_iota(I32, (TT, LANES), 1)

    def onehots(t):
        r = ri_ref[pl.ds(pl.multiple_of(t * TT, TT), TT), :]
        return lane == r[:, 0:1], lane == r[:, 1:2]

    def count_body(t, acc):
        oh1, oh2 = onehots(t)
        return acc + jnp.sum((oh1 | oh2).astype(F32), axis=0, keepdims=True)

    counts = lax.fori_loop(0, SEQ // TT, count_body, jnp.zeros((1, LANES), F32))
    counts8 = jnp.broadcast_to(counts, (8, LANES)).astype(I32)
    shift = FFN_B.bit_length() - 1
    padded = ((counts8 + (FFN_B - 1)) >> shift) << shift
    lane8 = lax.broadcasted_iota(I32, (8, LANES), 1)
    pad_end = padded
    sh = 1
    while sh < LANES:
        pad_end = pad_end + jnp.where(lane8 >= sh, pltpu.roll(pad_end, sh, axis=1), 0)
        sh *= 2
    pad_start = pad_end - padded

    ltri = (lax.broadcasted_iota(I32, (TT, TT), 0)
            > lax.broadcasted_iota(I32, (TT, TT), 1)).astype(BF16)

    def dest_body(t, carry):
        oh1, oh2 = onehots(t)
        a = (oh1 | oh2).astype(F32)
        base = jnp.dot(ltri, a.astype(BF16), preferred_element_type=F32) + carry
        d1 = jnp.sum(jnp.where(oh1, base, 0.0), axis=1, keepdims=True)
        d2 = jnp.sum(jnp.where(oh2, base, 0.0), axis=1, keepdims=True)
        dest_ref[pl.ds(pl.multiple_of(t * TT, TT), TT), :] = jnp.where(
            lane == 0, d1, jnp.where(lane == 1, d2, 0.0)).astype(I32)
        return carry + jnp.sum(a, axis=0, keepdims=True)

    lax.fori_loop(0, SEQ // TT, dest_body, pad_start[0:1].astype(F32))

    total = jnp.max(pad_end, axis=1, keepdims=True)
    row8 = lax.broadcasted_iota(I32, (8, LANES), 0)
    used_ref[...] = jnp.where(row8 == 0, jnp.broadcast_to(total >> shift, (8, LANES)),
                              jnp.where(row8 == 1, counts8, pad_start))


def _plan(ri):
    return pl.pallas_call(
        _plan_kernel,
        out_shape=(
            jax.ShapeDtypeStruct((SEQ, LANES), I32),
            jax.ShapeDtypeStruct((8, LANES), I32),
        ),
        compiler_params=pltpu.CompilerParams(vmem_limit_bytes=VMEM_LIMIT),
        name="route_plan",
    )(ri)


PAD_BITS = FFN_B.bit_length() - 1


def _pad_fill_copies(e, cnt_ref, pst_ref, zero_sc, xs_hbm, zsem):
    cnt = cnt_ref[e]
    pad = (-cnt) & (FFN_B - 1)
    row = pst_ref[e] + cnt
    out = []
    for bit in reversed(range(PAD_BITS)):
        n = 1 << bit
        start = row + ((pad >> (bit + 1)) << (bit + 1))
        copy = pltpu.make_async_copy(
            zero_sc.at[pl.ds(0, n * ROW_TILE)],
            xs_hbm.at[pl.ds(pl.multiple_of(start * ROW_TILE, ROW_TILE), n * ROW_TILE)], zsem)
        out.append(((pad & n) != 0, copy))
    return out


def _unused_block_copies(b, zero_sc, xs_hbm, zsem):
    half = FFN_B // 2 * ROW_TILE
    return [pltpu.make_async_copy(
        zero_sc, xs_hbm.at[pl.ds(pl.multiple_of((2 * b + k) * half, half), half)], zsem)
        for k in range(2)]


def _dispatch_kernel(dest_ref, cnt_ref, pst_ref, used_ref, xn_ref, xs_hbm, zero_sc, sem, zsem):
    tm = DISP_TM
    i = pl.program_id(0)

    @pl.when(i == 0)
    def _():
        zero_sc[...] = jnp.zeros_like(zero_sc)

        def fill(e, carry):
            for cond, copy in _pad_fill_copies(e, cnt_ref, pst_ref, zero_sc, xs_hbm, zsem):
                pl.when(cond)(copy.start)
            return carry

        lax.fori_loop(0, MOE_EXPERTS, fill, 0)

        def fill_block(b, carry):
            for copy in _unused_block_copies(b, zero_sc, xs_hbm, zsem):
                copy.start()
            return carry

        lax.fori_loop(used_ref[0], N_BLOCKS, fill_block, 0)

    def issue(it, carry):
        for u in range(DMA_UNROLL):
            r = it * DMA_UNROLL + u
            a = (i * tm + r) * 2
            src = xn_ref.at[pl.ds(pl.multiple_of(r * ROW_TILE, ROW_TILE), ROW_TILE)]
            for kk in range(2):
                d = pl.multiple_of(dest_ref[a + kk] * ROW_TILE, ROW_TILE)
                pltpu.make_async_copy(src, xs_hbm.at[pl.ds(d, ROW_TILE)], sem).start()
        return carry

    lax.fori_loop(0, tm // DMA_UNROLL, issue, 0)
    for _ in range(2):
        pltpu.make_async_copy(xn_ref, xs_hbm.at[pl.ds(0, tm * ROW_TILE)], sem).wait()

    @pl.when(i == 0)
    def _():
        def drain(e, carry):
            for cond, copy in _pad_fill_copies(e, cnt_ref, pst_ref, zero_sc, xs_hbm, zsem):
                pl.when(cond)(copy.wait)
            return carry

        lax.fori_loop(0, MOE_EXPERTS, drain, 0)

        def drain_block(b, carry):
            for copy in _unused_block_copies(b, zero_sc, xs_hbm, zsem):
                copy.wait()
            return carry

        lax.fori_loop(used_ref[0], N_BLOCKS, drain_block, 0)


def _dispatch(dest_flat, counts, pad_start, used, xn3):
    tm = DISP_TM
    return pl.pallas_call(
        _dispatch_kernel,
        out_shape=jax.ShapeDtypeStruct((N_BUF * ROW_TILE, LANES), F32),
        grid_spec=pltpu.PrefetchScalarGridSpec(
            num_scalar_prefetch=4,
            grid=(SEQ // tm,),
            in_specs=[pl.BlockSpec((tm * ROW_TILE, LANES), lambda i, d, c, p, u: (i, 0))],
            out_specs=pl.BlockSpec(memory_space=pl.ANY),
            scratch_shapes=[
                pltpu.VMEM((FFN_B // 2 * ROW_TILE, LANES), F32),
                pltpu.SemaphoreType.DMA(()),
                pltpu.SemaphoreType.DMA(()),
            ],
        ),
        compiler_params=_params(("arbitrary",)),
        name="moe_dispatch",
    )(dest_flat, counts, pad_start, used, xn3)


def _ffn_kernel(cnt_ref, pst_ref, used_ref, xs_hbm, *refs):
    wg_refs = refs[0:W_CHUNKS]
    wu_refs = refs[W_CHUNKS:2 * W_CHUNKS]
    wd_refs = refs[2 * W_CHUNKS:3 * W_CHUNKS]
    (y_hbm, xbuf, ybuf, zero_sc, wg_bf, wu_bf, wd_bf, sem_in, sem_out,
     zsem) = refs[3 * W_CHUNKS:]
    B = FFN_B
    R = B * ROW_TILE
    e = pl.program_id(0)
    n = (cnt_ref[e] + (B - 1)) >> PAD_BITS
    s0 = pst_ref[e] >> PAD_BITS

    def rows(blk):
        return pl.ds(pl.multiple_of(blk * R, R), R)

    def fetch(blk, slot):
        return pltpu.make_async_copy(xs_hbm.at[rows(blk)], xbuf.at[slot], sem_in.at[slot])

    def flush(blk, slot):
        return pltpu.make_async_copy(ybuf.at[slot], y_hbm.at[rows(blk)], sem_out.at[slot])

    @pl.when(n > 0)
    def _():
        fetch(s0, 0).start()
        for dst, chunks in ((wg_bf, wg_refs), (wu_bf, wu_refs), (wd_bf, wd_refs)):
            rc = dst.shape[0] // W_CHUNKS
            for c in range(W_CHUNKS):
                dst[c * rc:(c + 1) * rc, :] = chunks[c][0, 0].astype(BF16)

    def body(j, carry):
        slot = j % 2

        @pl.when(j + 1 < n)
        def _():
            fetch(s0 + j + 1, 1 - slot).start()

        fetch(s0 + j, slot).wait()

        @pl.when(j >= 2)
        def _():
            flush(s0 + j - 2, slot).wait()

        x = jnp.concatenate([xbuf[slot, pl.ds(c, B, stride=ROW_TILE), :]
                             for c in range(ROW_TILE)], axis=1).astype(BF16)
        hg = jnp.dot(x, wg_bf[...], preferred_element_type=F32)
        hu = jnp.dot(x, wu_bf[...], preferred_element_type=F32)
        hh = ((hg / (1.0 + jnp.exp(-hg))) * hu).astype(BF16)
        y = jnp.dot(hh, wd_bf[...], preferred_element_type=F32)
        for c in range(ROW_TILE):
            ybuf[slot, pl.ds(c, B, stride=ROW_TILE), :] = y[:, c * LANES:(c + 1) * LANES]
        flush(s0 + j, slot).start()
        return carry

    lax.fori_loop(0, n, body, 0)

    @pl.when(n >= 1)
    def _():
        flush(s0 + n - 1, (n + 1) % 2).wait()

    @pl.when(n >= 2)
    def _():
        flush(s0 + n - 2, n % 2).wait()

    @pl.when(e == MOE_EXPERTS - 1)
    def _():
        zero_sc[...] = jnp.zeros_like(zero_sc)

        def fill(b, carry):
            pltpu.make_async_copy(zero_sc, y_hbm.at[rows(b)], zsem).start()
            return carry

        def drain(b, carry):
            pltpu.make_async_copy(zero_sc, y_hbm.at[rows(b)], zsem).wait()
            return carry

        lax.fori_loop(used_ref[0], N_BLOCKS, fill, 0)
        lax.fori_loop(used_ref[0], N_BLOCKS, drain, 0)


def _ffn(counts, pad_start, used, xs, w_gate, w_up, w_down):
    B = FFN_B
    def wspecs(rows, cols):
        return [pl.BlockSpec((1, 1, rows // W_CHUNKS, cols),
                             lambda e, cnt, pst, used, c=c: (e, c, 0, 0)) for c in range(W_CHUNKS)]

    def chunked(w):
        return w.reshape(MOE_EXPERTS, W_CHUNKS, w.shape[1] // W_CHUNKS, w.shape[2])

    return pl.pallas_call(
        _ffn_kernel,
        out_shape=jax.ShapeDtypeStruct((N_BUF * ROW_TILE, LANES), F32),
        grid_spec=pltpu.PrefetchScalarGridSpec(
            num_scalar_prefetch=3,
            grid=(MOE_EXPERTS,),
            in_specs=[pl.BlockSpec(memory_space=pl.ANY)]
            + wspecs(D_MODEL, MOE_HIDDEN) + wspecs(D_MODEL, MOE_HIDDEN)
            + wspecs(MOE_HIDDEN, D_MODEL),
            out_specs=pl.BlockSpec(memory_space=pl.ANY),
            scratch_shapes=[
                pltpu.VMEM((2, B * ROW_TILE, LANES), F32),
                pltpu.VMEM((2, B * ROW_TILE, LANES), F32),
                pltpu.VMEM((B * ROW_TILE, LANES), F32),
                pltpu.VMEM((D_MODEL, MOE_HIDDEN), BF16),
                pltpu.VMEM((D_MODEL, MOE_HIDDEN), BF16),
                pltpu.VMEM((MOE_HIDDEN, D_MODEL), BF16),
                pltpu.SemaphoreType.DMA((2,)),
                pltpu.SemaphoreType.DMA((2,)),
                pltpu.SemaphoreType.DMA(()),
            ],
        ),
        compiler_params=_params(("arbitrary",)),
        name="expert_ffn",
    )(counts, pad_start, used, xs, *([chunked(w_gate)] * W_CHUNKS), *([chunked(w_up)] * W_CHUNKS),
      *([chunked(w_down)] * W_CHUNKS))


def _combine_kernel(dest_ref, h1_ref, rw_ref, g_ref, y_hbm, o_ref, ybuf, sem):
    tm = COMB_TM
    i = pl.program_id(0)

    def gather(tile, slot):
        def issue(it, carry):
            for u in range(DMA_UNROLL):
                r = it * DMA_UNROLL + u
                a = (tile * tm + r) * 2
                for kk in range(2):
                    d = pl.multiple_of(dest_ref[a + kk] * ROW_TILE, ROW_TILE)
                    pltpu.make_async_copy(
                        y_hbm.at[pl.ds(d, ROW_TILE)],
                        ybuf.at[slot, kk, pl.ds(pl.multiple_of(r * ROW_TILE, ROW_TILE), ROW_TILE)],
                        sem.at[slot, kk]).start()
            return carry

        lax.fori_loop(0, tm // DMA_UNROLL, issue, 0)

    @pl.when(i == 0)
    def _():
        gather(0, 0)

    @pl.when(i + 1 < pl.num_programs(0))
    def _():
        gather(i + 1, (i + 1) % 2)

    slot = i % 2
    for kk in range(2):
        pltpu.make_async_copy(y_hbm.at[pl.ds(0, tm * ROW_TILE)], ybuf.at[slot, kk],
                              sem.at[slot, kk]).wait()
    w = rw_ref[...]
    ys = [jnp.concatenate([ybuf[slot, kk, pl.ds(c, tm, stride=ROW_TILE), :]
                           for c in range(ROW_TILE)], axis=1) for kk in range(2)]
    h = h1_ref[...] + w[:, 0:1] * ys[0] + w[:, 1:2] * ys[1]
    var = jnp.mean(h * h, axis=-1, keepdims=True)
    o_ref[...] = h * lax.rsqrt(var + EPS) * g_ref[...]


def _combine(dest_flat, h1, rw, g, y):
    tm = COMB_TM
    return pl.pallas_call(
        _combine_kernel,
        out_shape=jax.ShapeDtypeStruct((SEQ, D_MODEL), F32),
        grid_spec=pltpu.PrefetchScalarGridSpec(
            num_scalar_prefetch=1,
            grid=(SEQ // tm,),
            in_specs=[
                pl.BlockSpec((tm, D_MODEL), lambda i, d: (i, 0)),
                pl.BlockSpec((tm, LANES), lambda i, d: (i, 0)),
                pl.BlockSpec((1, D_MODEL), lambda i, d: (0, 0)),
                pl.BlockSpec(memory_space=pl.ANY),
            ],
            out_specs=pl.BlockSpec((tm, D_MODEL), lambda i, d: (i, 0)),
            scratch_shapes=[
                pltpu.VMEM((2, 2, tm * ROW_TILE, LANES), F32),
                pltpu.SemaphoreType.DMA((2, 2)),
            ],
        ),
        compiler_params=_params(("arbitrary",)),
        name="moe_combine",
    )(dest_flat, h1, rw, g, y)


def _attention_tables():
    T = ATT_T
    slopes = jnp.exp2(-ALIBI_MAX * jnp.arange(1, DA_HEADS + 1, dtype=F32) / DA_HEADS)
    r = jnp.arange(T)
    hi = ((r // CHUNK) * CHUNK).astype(F32)
    lo = (r % CHUNK).astype(F32)
    ones = jnp.ones((T,), F32)
    sl = slopes[:, None]
    one_h = jnp.broadcast_to(ones, (DA_HEADS, T))
    q_rows = jnp.stack([one_h, one_h, -sl * hi[None], -sl * lo[None]], axis=1)
    k_cols = jnp.stack([sl * hi[None], sl * lo[None], one_h, one_h], axis=-1)
    qa = jnp.zeros((DA_HEADS, DA_HEAD_DIM, T), F32).at[:, 0:4, :].set(q_rows)
    ka1 = jnp.zeros((DA_HEADS, T, LANES), F32).at[:, :, DA_HEAD_DIM:DA_HEAD_DIM + 4].set(k_cols)
    ka2 = jnp.zeros((DA_HEADS, T, LANES), F32).at[:, :, 0:4].set(k_cols)
    rel = (r[:, None] - r[None, :]).astype(F32)
    allowed = (r[:, None] // CHUNK) <= (r[None, :] // CHUNK)
    fix = jnp.where(rel > 0, -2.0 * slopes[:, None, None] * rel[None], 0.0)
    dtab = jnp.where(allowed[None], fix, -jnp.inf)
    return slopes, qa.astype(BF16), ka1.astype(BF16), ka2.astype(BF16), dtab


def _retention_tables():
    C = RET_C
    log_gamma = jnp.log1p(-jnp.exp2(-5.0 - jnp.arange(RET_HEADS, dtype=F32)))
    pos = jnp.arange(C, dtype=F32)
    rel = pos[:, None] - pos[None, :]
    dec = jnp.where(rel >= 0, jnp.exp(log_gamma[:, None, None] * jnp.maximum(rel, 0.0)), 0.0)
    qdec = jnp.exp(log_gamma[:, None] * (pos + 1.0)[None, :])[:, :, None]
    kdec = jnp.exp(log_gamma[:, None] * (C - 1 - pos)[None, :])[:, :, None]
    cd = jnp.exp(log_gamma * C)
    return cd, dec, qdec, kdec


def kernel(x, attn_norm_g, w_in, da_lambda_q1, da_lambda_k1, da_lambda_q2, da_lambda_k2,
           da_subln_g, w_out, ffn_norm_g, router_group_w, router_group_b, router_expert_w,
           router_expert_b, expert_w_gate, expert_w_up, expert_w_down, final_norm_g):
    B, S, D = x.shape
    assert (B, S, D) == (1, SEQ, D_MODEL)
    x2 = x.reshape(S, D)

    w = w_in[0]
    w_main = jnp.concatenate([
        w[:, 512:1024],
        w[:, 1536:1792],
        w[:, 1792:2048] * (RET_QK_DIM ** -0.5),
        w[:, 2048:3072]], axis=1).astype(BF16)
    wq_t = (w[:, 0:512] * (DA_HEAD_DIM ** -0.5)).T.astype(BF16)
    wv_t = w[:, 1024:1536].T.astype(BF16)
    cd, dec, qdec, kdec = _retention_tables()
    k_da, qt4, vt4, o_r = _inproj(x2, attn_norm_g[0][None, :], w_main, wq_t, wv_t,
                                  cd, dec, qdec, kdec)

    slopes, qa, ka1, ka2, dtab = _attention_tables()
    o_da = _attention(k_da, qt4, vt4, slopes, qa, ka1, ka2, dtab, da_lambda_q1, da_lambda_k1,
                      da_lambda_q2, da_lambda_k2, da_subln_g)

    wr = jnp.zeros((D, LANES), F32)
    wr = wr.at[:, :MOE_GROUPS].set(router_group_w[0])
    wr = wr.at[:, MOE_GROUPS:MOE_GROUPS + MOE_EXPERTS].set(router_expert_w[0])
    br = jnp.zeros((1, LANES), F32)
    br = br.at[0, :MOE_GROUPS].set(router_group_b[0])
    br = br.at[0, MOE_GROUPS:MOE_GROUPS + MOE_EXPERTS].set(router_expert_b[0])
    wr_hi = wr.astype(BF16)
    wr_lo = (wr - wr_hi.astype(F32)).astype(BF16)
    h1, xn, ri, rw = _outproj_router(x2, o_da, o_r, w_out[0].astype(BF16), ffn_norm_g[0][None, :],
                                     jnp.concatenate([wr_hi, wr_lo], axis=1), br)

    dest, meta = _plan(ri)
    dest_flat = dest[:, :2].reshape(N_ASSIGN)
    used1 = meta[0, :1]
    counts = meta[1, :MOE_EXPERTS]
    pad_start = meta[2, :MOE_EXPERTS]
    xs = _dispatch(dest_flat, counts, pad_start, used1, xn)

    y = _ffn(counts, pad_start, used1, xs, expert_w_gate[0], expert_w_up[0], expert_w_down[0])
    out = _combine(dest_flat, h1, rw, final_norm_g[None, :], y)
    return out.reshape(B, S, D)
```

```python
import functools
import math

import jax
import jax.numpy as jnp
import numpy as np
from jax import lax
from jax.experimental import pallas as pl
from jax.experimental.pallas import tpu as pltpu

F32 = jnp.float32
BF16 = jnp.bfloat16
I32 = jnp.int32

D_MODEL = 1024
SEQ = 16384
CHUNK = 64
EPS = 1e-6

DA_HEADS = 4
DA_HEAD_DIM = 64
DA_V_DIM = 128
DA_WIDTH = 512
ALIBI_MAX = 8.0
RET_HEADS = 4
RET_QK_DIM = 64
RET_V_DIM = 128
RET_WIDTH = 512
T_ROWS = 512
MAIN_COLS = 2048
DK_OFF = 0
RQ_OFF = 512
RK_OFF = 768
RV_OFF = 1024
RG_OFF = 1536

MOE_GROUPS = 4
MOE_EXPERTS_PER_GROUP = 8
MOE_EXPERTS = 32
MOE_HIDDEN = 512
LAMBDA_INIT = 0.8 - 0.6 * math.exp(-0.3 * 0)

LANES = 128
ROW_TILE = 8
VMEM_LIMIT = 56 * 1024 * 1024

PROJ_TM = 512
ATT_T = 512
RET_C = 256
PLAN_T = 512
FFN_B = 256
N_ASSIGN = 2 * SEQ
N_BLOCKS = N_ASSIGN // FFN_B + MOE_EXPERTS
N_BUF = N_BLOCKS * FFN_B
COMB_TM = 256
DISP_TM = 512
DMA_UNROLL = 8
W_CHUNKS = 4


def _params(sem):
    return pltpu.CompilerParams(dimension_semantics=sem, vmem_limit_bytes=VMEM_LIMIT)


def _retention_block(q_all, k_all, v_all, g_all, cd_ref, dec_ref, qdec_ref, kdec_ref, st_sc):
    outs = []
    for h in range(RET_HEADS):
        qk = slice(h * RET_QK_DIM, (h + 1) * RET_QK_DIM)
        vv = slice(h * RET_V_DIM, (h + 1) * RET_V_DIM)
        q = q_all[:, qk]
        k = k_all[:, qk]
        v = v_all[:, vv]
        g = g_all[:, vv]
        s = lax.dot_general(q, k, (((1,), (1,)), ((), ())),
                            preferred_element_type=F32) * dec_ref[h]
        intra = jnp.dot(s.astype(BF16), v, preferred_element_type=F32)
        st = st_sc[h]
        cross = jnp.dot(q, st.astype(BF16), preferred_element_type=F32) * qdec_ref[h]
        kd = (k.astype(F32) * kdec_ref[h]).astype(BF16)
        st_sc[h] = st * cd_ref[h] + lax.dot_general(kd, v, (((0,), (0,)), ((), ())),
                                                    preferred_element_type=F32)
        o = intra + cross
        o = o * lax.rsqrt(jnp.mean(o * o, axis=-1, keepdims=True) + EPS)
        outs.append(((g / (1.0 + jnp.exp(-g))) * o).astype(BF16))
    return outs


def _inproj_kernel(cd_ref, x_ref, g_ref, w_ref, wq_ref, wv_ref, dec_ref, qdec_ref, kdec_ref,
                   k_ref, qt_ref, vt_ref, or_ref, st_sc):
    @pl.when(pl.program_id(0) == 0)
    def _():
        st_sc[...] = jnp.zeros_like(st_sc)

    x = x_ref[...]
    var = jnp.mean(x * x, axis=-1, keepdims=True)
    xn = (x * lax.rsqrt(var + EPS) * g_ref[...]).astype(BF16)

    def proj(lo, hi):
        return jnp.dot(xn, w_ref[:, lo:hi], preferred_element_type=F32)

    k_ref[...] = proj(DK_OFF, DK_OFF + DA_WIDTH).astype(BF16)
    nt = (((1,), (1,)), ((), ()))
    qt = lax.dot_general(wq_ref[...], xn, nt, preferred_element_type=F32)
    qt_ref[...] = qt.astype(BF16).reshape(DA_HEADS, 1, 2 * DA_HEAD_DIM, PROJ_TM)
    vt = lax.dot_general(wv_ref[...], xn, nt, preferred_element_type=F32)
    vt_ref[...] = vt.astype(BF16).reshape(DA_HEADS, 1, DA_V_DIM, PROJ_TM)

    rq = proj(RQ_OFF, RK_OFF).astype(BF16)
    rk = proj(RK_OFF, RV_OFF).astype(BF16)
    rv = proj(RV_OFF, RG_OFF).astype(BF16)
    rg = proj(RG_OFF, MAIN_COLS)
    for blk in range(PROJ_TM // RET_C):
        rows = slice(blk * RET_C, (blk + 1) * RET_C)
        outs = _retention_block(rq[rows], rk[rows], rv[rows], rg[rows],
                                cd_ref, dec_ref, qdec_ref, kdec_ref, st_sc)
        for h in range(RET_HEADS):
            or_ref[rows, h * RET_V_DIM:(h + 1) * RET_V_DIM] = outs[h]


def _inproj(x2, g, w_bf, wq_t, wv_t, cd, dec, qdec, kdec):
    C = RET_C
    t_shape = jax.ShapeDtypeStruct((DA_HEADS, SEQ // PROJ_TM, LANES, PROJ_TM), BF16)
    t_spec = pl.BlockSpec((DA_HEADS, 1, LANES, PROJ_TM), lambda i: (0, i, 0, 0))
    w_t_spec = pl.BlockSpec((T_ROWS, D_MODEL), lambda i: (0, 0))
    return pl.pallas_call(
        _inproj_kernel,
        out_shape=(jax.ShapeDtypeStruct((SEQ, DA_WIDTH), BF16), t_shape, t_shape,
                   jax.ShapeDtypeStruct((SEQ, RET_WIDTH), BF16)),
        grid=(SEQ // PROJ_TM,),
        in_specs=[
            pl.BlockSpec(memory_space=pltpu.SMEM),
            pl.BlockSpec((PROJ_TM, D_MODEL), lambda i: (i, 0)),
            pl.BlockSpec((1, D_MODEL), lambda i: (0, 0)),
            pl.BlockSpec((D_MODEL, MAIN_COLS), lambda i: (0, 0)),
            w_t_spec, w_t_spec,
            pl.BlockSpec((RET_HEADS, C, C), lambda i: (0, 0, 0)),
            pl.BlockSpec((RET_HEADS, C, 1), lambda i: (0, 0, 0)),
            pl.BlockSpec((RET_HEADS, C, 1), lambda i: (0, 0, 0)),
        ],
        out_specs=(pl.BlockSpec((PROJ_TM, DA_WIDTH), lambda i: (i, 0)), t_spec, t_spec,
                   pl.BlockSpec((PROJ_TM, RET_WIDTH), lambda i: (i, 0))),
        scratch_shapes=[pltpu.VMEM((RET_HEADS, RET_QK_DIM, RET_V_DIM), F32)],
        compiler_params=_params(("arbitrary",)),
        name="inproj_retention",
    )(cd, x2, g, w_bf, wq_t, wv_t, dec, qdec, kdec)


ACC_ROWS = DA_V_DIM + 16


N_QT = SEQ // ATT_T
N_OFF = N_QT * (N_QT - 1) // 2


def _pipeline3(n_pos, scores, accumulate):
    scores(0, 0)
    scores(1, 1)
    steady = n_pos - 2

    def triple(k, carry):
        t = 3 * k
        accumulate(t, 0)
        scores(t + 2, 2)
        accumulate(t + 1, 1)
        scores(t + 3, 0)
        accumulate(t + 2, 2)
        scores(t + 4, 1)
        return carry

    lax.fori_loop(0, steady // 3, triple, 0)
    t0 = steady // 3 * 3
    rem = steady - t0
    accumulate(t0, 0)
    if rem >= 1:
        scores(t0 + 2, 2)
    accumulate(t0 + 1, 1)
    if rem == 2:
        scores(t0 + 3, 0)
    if rem >= 1:
        accumulate(t0 + 2, 2)
    if rem == 2:
        accumulate(t0 + 3, 0)


def _attn_kernel(slope_ref, jt_ref, it_ref, qt_ref, k_ref, vt_ref, qa_ref, ka1_ref, ka2_ref,
                 dtab_ref, lq1_ref, lk1_ref, lq2_ref, lk2_ref, g_ref, o_ref,
                 m_sc, acc_sc, s0_sc, s1_sc, s2_sc, mx0_sc, mx1_sc, mx2_sc):
    T = ATT_T
    h = pl.program_id(0)
    slope = slope_ref[h]
    qa = qa_ref[0]
    lane = lax.broadcasted_iota(I32, (T, LANES), 1)
    sums_row = (lax.broadcasted_iota(I32, (16, T), 0) == 0).astype(BF16)
    s_bufs = (s0_sc, s1_sc, s2_sc)
    mx_bufs = (mx0_sc, mx1_sc, mx2_sc)

    def scores(j, i, buf, extra):
        kt = k_ref[pl.ds(pl.multiple_of(j * T, T), T), :]
        ks = (jnp.where(lane < DA_HEAD_DIM, kt, ka1_ref[0]),
              jnp.where(lane >= DA_HEAD_DIM, kt, ka2_ref[0]))
        qt = qt_ref[0, i]
        qw = (jnp.concatenate([qt[0:DA_HEAD_DIM], qa], axis=0),
              jnp.concatenate([qa, qt[DA_HEAD_DIM:]], axis=0))
        for mp in range(2):
            s = jnp.dot(ks[mp], qw[mp], preferred_element_type=F32)
            if extra is not None:
                s = s + extra[0]
            s_bufs[buf][mp] = s
            mx_bufs[buf][mp] = jnp.max(s, axis=0, keepdims=True)

    def accumulate(j, i, buf):
        c = slope * lax.convert_element_type((i - j) * T, F32)
        vte = jnp.concatenate([vt_ref[0, j], sums_row], axis=0)
        for mp in range(2):
            m_prev = m_sc[i, mp]
            m_new = jnp.maximum(m_prev, mx_bufs[buf][mp] - c)
            p = jnp.exp(s_bufs[buf][mp] - (m_new + c)).astype(BF16)
            pv = jnp.dot(vte, p, preferred_element_type=F32)
            acc_sc[i, mp] = jnp.exp(m_prev - m_new) * acc_sc[i, mp] + pv
            m_sc[i, mp] = m_new

    m_sc[...] = jnp.full_like(m_sc, -jnp.inf)
    acc_sc[...] = jnp.zeros_like(acc_sc)
    _pipeline3(N_QT,
               lambda pos, buf: scores(pos, pos, buf, dtab_ref),
               lambda pos, buf: accumulate(pos, pos, buf))
    _pipeline3(N_OFF,
               lambda pos, buf: scores(jt_ref[pos], it_ref[pos], buf, None),
               lambda pos, buf: accumulate(jt_ref[pos], it_ref[pos], buf))

    lam = (jnp.exp(jnp.sum(lq1_ref[...] * lk1_ref[...], axis=1, keepdims=True))
           - jnp.exp(jnp.sum(lq2_ref[...] * lk2_ref[...], axis=1, keepdims=True))
           + LAMBDA_INIT)

    def finish(i, carry):
        a1 = acc_sc[i, 0]
        a2 = acc_sc[i, 1]
        ot = (a1[0:DA_V_DIM] / a1[DA_V_DIM:DA_V_DIM + 1]
              - lam * (a2[0:DA_V_DIM] / a2[DA_V_DIM:DA_V_DIM + 1]))
        o = ot.T
        var = jnp.mean(o * o, axis=-1, keepdims=True)
        o = (o * lax.rsqrt(var + EPS) * g_ref[...]) * (1.0 - LAMBDA_INIT)
        o_ref[pl.ds(pl.multiple_of(i * T, T), T), :] = o.astype(BF16)
        return carry

    lax.fori_loop(0, N_QT, finish, 0)


def _attention(proj, qt4, vt4, slopes, qa, ka1, ka2, dtab, lq1, lk1, lq2, lk2, subln_g):
    T = ATT_T
    vec64 = pl.BlockSpec((1, DA_HEAD_DIM), lambda h: (0, 0))
    per_head = lambda a, b: pl.BlockSpec((1, a, b), lambda h: (h, 0, 0))
    slab = lambda shape, imap: pl.BlockSpec(shape, imap, pipeline_mode=pl.Buffered(1))
    smem = pl.BlockSpec(memory_space=pltpu.SMEM)
    it_tab, jt_tab = np.tril_indices(N_QT, -1)
    return pl.pallas_call(
        _attn_kernel,
        out_shape=jax.ShapeDtypeStruct((SEQ, DA_WIDTH), BF16),
        grid=(DA_HEADS,),
        in_specs=[
            smem, smem, smem,
            slab((1, N_QT, LANES, T), lambda h: (h, 0, 0, 0)),
            slab((SEQ, LANES), lambda h: (0, DK_OFF // LANES + h)),
            slab((1, N_QT, LANES, T), lambda h: (h, 0, 0, 0)),
            per_head(DA_HEAD_DIM, T), per_head(T, LANES), per_head(T, LANES), per_head(T, T),
            vec64, vec64, vec64, vec64,
            pl.BlockSpec((1, DA_V_DIM), lambda h: (0, 0)),
        ],
        out_specs=slab((SEQ, LANES), lambda h: (0, h)),
        scratch_shapes=[
            pltpu.VMEM((N_QT, 2, 1, T), F32),
            pltpu.VMEM((N_QT, 2, ACC_ROWS, T), F32),
            pltpu.VMEM((2, T, T), F32),
            pltpu.VMEM((2, T, T), F32),
            pltpu.VMEM((2, T, T), F32),
            pltpu.VMEM((2, 1, T), F32),
            pltpu.VMEM((2, 1, T), F32),
            pltpu.VMEM((2, 1, T), F32),
        ],
        compiler_params=_params(("arbitrary",)),
        name="diff_attention",
    )(slopes, jnp.asarray(jt_tab, I32), jnp.asarray(it_tab, I32), qt4, proj, vt4, qa, ka1, ka2,
      dtab, lq1, lk1, lq2, lk2, subln_g)


def _outproj_router_kernel(x_ref, oda_ref, or_ref, wo_ref, g_ref, wr_ref, br_ref,
                           h1_ref, xn_ref, ri_ref, rw_ref):
    h1 = (x_ref[...]
          + jnp.dot(oda_ref[...], wo_ref[0:DA_WIDTH, :], preferred_element_type=F32)
          + jnp.dot(or_ref[...], wo_ref[DA_WIDTH:, :], preferred_element_type=F32))
    h1_ref[...] = h1
    var = jnp.mean(h1 * h1, axis=-1, keepdims=True)
    xn = h1 * lax.rsqrt(var + EPS) * g_ref[...]
    for c in range(ROW_TILE):
        xn_ref[pl.ds(c, PROJ_TM, stride=ROW_TILE), :] = xn[:, c * LANES:(c + 1) * LANES]
    x_hi = xn.astype(BF16)
    x_lo = (xn - x_hi.astype(F32)).astype(BF16)
    both = jnp.dot(x_hi, wr_ref[...], preferred_element_type=F32)
    logits = (both[:, :LANES] + both[:, LANES:]
              + jnp.dot(x_lo, wr_ref[:, :LANES], preferred_element_type=F32)) + br_ref[...]
    lane = lax.broadcasted_iota(I32, logits.shape, 1)
    neg = jnp.float32(-jnp.inf)
    big = jnp.int32(1 << 20)
    gl = jnp.where(lane < MOE_GROUPS, logits, neg)
    gmax = jnp.max(gl, axis=1, keepdims=True)
    gidx = jnp.min(jnp.where(gl == gmax, lane, big), axis=1, keepdims=True)
    gsum = jnp.sum(jnp.exp(gl - gmax), axis=1, keepdims=True)
    gp = 1.0 / gsum
    lo = MOE_GROUPS + gidx * MOE_EXPERTS_PER_GROUP
    el = jnp.where((lane >= lo) & (lane < lo + MOE_EXPERTS_PER_GROUP), logits, neg)
    v1 = jnp.max(el, axis=1, keepdims=True)
    i1 = jnp.min(jnp.where(el == v1, lane, big), axis=1, keepdims=True)
    el2 = jnp.where(lane == i1, neg, el)
    v2 = jnp.max(el2, axis=1, keepdims=True)
    i2 = jnp.min(jnp.where(el2 == v2, lane, big), axis=1, keepdims=True)
    t = jnp.exp(v2 - v1)
    w1 = gp / (1.0 + t)
    w2 = gp * t / (1.0 + t)
    ri_ref[...] = jnp.where(lane == 0, i1 - MOE_GROUPS,
                            jnp.where(lane == 1, i2 - MOE_GROUPS, 0))
    rw_ref[...] = jnp.where(lane == 0, w1, jnp.where(lane == 1, w2, 0.0))


def _outproj_router(x2, o_da, o_r, wo_bf, g, wr, br):
    tm = PROJ_TM
    row = lambda w: pl.BlockSpec((tm, w), lambda i: (i, 0))
    full = lambda a, b: pl.BlockSpec((a, b), lambda i: (0, 0))
    return pl.pallas_call(
        _outproj_router_kernel,
        out_shape=(
            jax.ShapeDtypeStruct((SEQ, D_MODEL), F32),
            jax.ShapeDtypeStruct((SEQ * ROW_TILE, LANES), F32),
            jax.ShapeDtypeStruct((SEQ, LANES), I32),
            jax.ShapeDtypeStruct((SEQ, LANES), F32),
        ),
        grid=(SEQ // tm,),
        in_specs=[row(D_MODEL), row(DA_WIDTH), row(RET_WIDTH), full(D_MODEL, D_MODEL),
                  full(1, D_MODEL), full(D_MODEL, 2 * LANES), full(1, LANES)],
        out_specs=(row(D_MODEL), pl.BlockSpec((tm * ROW_TILE, LANES), lambda i: (i, 0)),
                   row(LANES), row(LANES)),
        compiler_params=_params(("arbitrary",)),
        name="outproj_router",
    )(x2, o_da, o_r, wo_bf, g, wr, br)


def _plan_kernel(ri_ref, dest_ref, used_ref):
    TT = PLAN_T
    lane = lax.broadcasted_iota(I32, (TT, LANES), 1)

    def onehots(t):
        r = ri_ref[pl.ds(pl.multiple_of(t * TT, TT), TT), :]
        return lane == r[:, 0:1], lane == r[:, 1:2]

    def count_body(t, acc):
        oh1, oh2 = onehots(t)
        return acc + jnp.sum((oh1 | oh2).astype(F32), axis=0, keepdims=True)

    counts = lax.fori_loop(0, SEQ // TT, count_body, jnp.zeros((1, LANES), F32))
    counts8 = jnp.broadcast_to(counts, (8, LANES)).astype(I32)
    shift = FFN_B.bit_length() - 1
    padded = ((counts8 + (FFN_B - 1)) >> shift) << shift
    lane8 = lax.broadcasted_iota(I32, (8, LANES), 1)
    pad_end = padded
    sh = 1
    while sh < LANES:
        pad_end = pad_end + jnp.where(lane8 >= sh, pltpu.roll(pad_end, sh, axis=1), 0)
        sh *= 2
    pad_start = pad_end - padded

    ltri = (lax.broadcasted_iota(I32, (TT, TT), 0)
            > lax.broadcasted_iota(I32, (TT, TT), 1)).astype(BF16)

    def dest_body(t, carry):
        oh1, oh2 = onehots(t)
        a = (oh1 | oh2).astype(F32)
        base = jnp.dot(ltri, a.astype(BF16), preferred_element_type=F32) + carry
        d1 = jnp.sum(jnp.where(oh1, base, 0.0), axis=1, keepdims=True)
        d2 = jnp.sum(jnp.where(oh2, base, 0.0), axis=1, keepdims=True)
        dest_ref[pl.ds(pl.multiple_of(t * TT, TT), TT), :] = jnp.where(
            lane == 0, d1, jnp.where(lane == 1, d2, 0.0)).astype(I32)
        return carry + jnp.sum(a, axis=0, keepdims=True)

    lax.fori_loop(0, SEQ // TT, dest_body, pad_start[0:1].astype(F32))

    total = jnp.max(pad_end, axis=1, keepdims=True)
    row8 = lax.broadcasted_iota(I32, (8, LANES), 0)
    used_ref[...] = jnp.where(row8 == 0, jnp.broadcast_to(total >> shift, (8, LANES)),
                              jnp.where(row8 == 1, counts8, pad_start))


def _plan(ri):
    return pl.pallas_call(
        _plan_kernel,
        out_shape=(
            jax.ShapeDtypeStruct((SEQ, LANES), I32),
            jax.ShapeDtypeStruct((8, LANES), I32),
        ),
        compiler_params=pltpu.CompilerParams(vmem_limit_bytes=VMEM_LIMIT),
        name="route_plan",
    )(ri)


PAD_BITS = FFN_B.bit_length() - 1


def _pad_fill_copies(e, cnt_ref, pst_ref, zero_sc, xs_hbm, zsem):
    cnt = cnt_ref[e]
    pad = (-cnt) & (FFN_B - 1)
    row = pst_ref[e] + cnt
    out = []
    for bit in reversed(range(PAD_BITS)):
        n = 1 << bit
        start = row + ((pad >> (bit + 1)) << (bit + 1))
        copy = pltpu.make_async_copy(
            zero_sc.at[pl.ds(0, n * ROW_TILE)],
            xs_hbm.at[pl.ds(pl.multiple_of(start * ROW_TILE, ROW_TILE), n * ROW_TILE)], zsem)
        out.append(((pad & n) != 0, copy))
    return out


def _unused_block_copies(b, zero_sc, xs_hbm, zsem):
    half = FFN_B // 2 * ROW_TILE
    return [pltpu.make_async_copy(
        zero_sc, xs_hbm.at[pl.ds(pl.multiple_of((2 * b + k) * half, half), half)], zsem)
        for k in range(2)]


def _dispatch_kernel(dest_ref, cnt_ref, pst_ref, used_ref, xn_ref, xs_hbm, zero_sc, sem, zsem):
    tm = DISP_TM
    i = pl.program_id(0)

    @pl.when(i == 0)
    def _():
        zero_sc[...] = jnp.zeros_like(zero_sc)

        def fill(e, carry):
            for cond, copy in _pad_fill_copies(e, cnt_ref, pst_ref, zero_sc, xs_hbm, zsem):
                pl.when(cond)(copy.start)
            return carry

        lax.fori_loop(0, MOE_EXPERTS, fill, 0)

        def fill_block(b, carry):
            for copy in _unused_block_copies(b, zero_sc, xs_hbm, zsem):
                copy.start()
            return carry

        lax.fori_loop(used_ref[0], N_BLOCKS, fill_block, 0)

    def issue(it, carry):
        for u in range(DMA_UNROLL):
            r = it * DMA_UNROLL + u
            a = (i * tm + r) * 2
            src = xn_ref.at[pl.ds(pl.multiple_of(r * ROW_TILE, ROW_TILE), ROW_TILE)]
            for kk in range(2):
                d = pl.multiple_of(dest_ref[a + kk] * ROW_TILE, ROW_TILE)
                pltpu.make_async_copy(src, xs_hbm.at[pl.ds(d, ROW_TILE)], sem).start()
        return carry

    lax.fori_loop(0, tm // DMA_UNROLL, issue, 0)
    for _ in range(2):
        pltpu.make_async_copy(xn_ref, xs_hbm.at[pl.ds(0, tm * ROW_TILE)], sem).wait()

    @pl.when(i == 0)
    def _():
        def drain(e, carry):
            for cond, copy in _pad_fill_copies(e, cnt_ref, pst_ref, zero_sc, xs_hbm, zsem):
                pl.when(cond)(copy.wait)
            return carry

        lax.fori_loop(0, MOE_EXPERTS, drain, 0)

        def drain_block(b, carry):
            for copy in _unused_block_copies(b, zero_sc, xs_hbm, zsem):
                copy.wait()
            return carry

        lax.fori_loop(used_ref[0], N_BLOCKS, drain_block, 0)


def _dispatch(dest_flat, counts, pad_start, used, xn3):
    tm = DISP_TM
    return pl.pallas_call(
        _dispatch_kernel,
        out_shape=jax.ShapeDtypeStruct((N_BUF * ROW_TILE, LANES), F32),
        grid_spec=pltpu.PrefetchScalarGridSpec(
            num_scalar_prefetch=4,
            grid=(SEQ // tm,),
            in_specs=[pl.BlockSpec((tm * ROW_TILE, LANES), lambda i, d, c, p, u: (i, 0))],
            out_specs=pl.BlockSpec(memory_space=pl.ANY),
            scratch_shapes=[
                pltpu.VMEM((FFN_B // 2 * ROW_TILE, LANES), F32),
                pltpu.SemaphoreType.DMA(()),
                pltpu.SemaphoreType.DMA(()),
            ],
        ),
        compiler_params=_params(("arbitrary",)),
        name="moe_dispatch",
    )(dest_flat, counts, pad_start, used, xn3)


def _ffn_kernel(cnt_ref, pst_ref, used_ref, xs_hbm, *refs):
    wg_refs = refs[0:W_CHUNKS]
    wu_refs = refs[W_CHUNKS:2 * W_CHUNKS]
    wd_refs = refs[2 * W_CHUNKS:3 * W_CHUNKS]
    (y_hbm, xbuf, ybuf, zero_sc, wg_bf, wu_bf, wd_bf, sem_in, sem_out,
     zsem) = refs[3 * W_CHUNKS:]
    B = FFN_B
    R = B * ROW_TILE
    e = pl.program_id(0)
    n = (cnt_ref[e] + (B - 1)) >> PAD_BITS
    s0 = pst_ref[e] >> PAD_BITS

    def rows(blk):
        return pl.ds(pl.multiple_of(blk * R, R), R)

    def fetch(blk, slot):
        return pltpu.make_async_copy(xs_hbm.at[rows(blk)], xbuf.at[slot], sem_in.at[slot])

    def flush(blk, slot):
        return pltpu.make_async_copy(ybuf.at[slot], y_hbm.at[rows(blk)], sem_out.at[slot])

    used = used_ref[0]

    @pl.when(e == 0)
    def _():
        fetch(0, 0).start()

    @pl.when(n > 0)
    def _():
        for dst, chunks in ((wg_bf, wg_refs), (wu_bf, wu_refs), (wd_bf, wd_refs)):
            rc = dst.shape[0] // W_CHUNKS
            for c in range(W_CHUNKS):
                dst[c * rc:(c + 1) * rc, :] = chunks[c][0, 0].astype(BF16)

    def body(j, carry):
        blk = s0 + j
        slot = blk % 2

        @pl.when(blk + 1 < used)
        def _():
            fetch(blk + 1, 1 - slot).start()

        fetch(blk, slot).wait()

        @pl.when(blk >= 2)
        def _():
            flush(blk - 2, slot).wait()

        x = jnp.concatenate([xbuf[slot, pl.ds(c, B, stride=ROW_TILE), :]
                             for c in range(ROW_TILE)], axis=1).astype(BF16)
        hg = jnp.dot(x, wg_bf[...], preferred_element_type=F32)
        hu = jnp.dot(x, wu_bf[...], preferred_element_type=F32)
        hh = ((hg / (1.0 + jnp.exp(-hg))) * hu).astype(BF16)
        y = jnp.dot(hh, wd_bf[...], preferred_element_type=F32)
        for c in range(ROW_TILE):
            ybuf[slot, pl.ds(c, B, stride=ROW_TILE), :] = y[:, c * LANES:(c + 1) * LANES]
        flush(blk, slot).start()
        return carry

    lax.fori_loop(0, n, body, 0)

    @pl.when(e == MOE_EXPERTS - 1)
    def _():
        flush(used - 1, (used + 1) % 2).wait()
        flush(used - 2, used % 2).wait()
        zero_sc[...] = jnp.zeros_like(zero_sc)

        def fill(b, carry):
            pltpu.make_async_copy(zero_sc, y_hbm.at[rows(b)], zsem).start()
            return carry

        def drain(b, carry):
            pltpu.make_async_copy(zero_sc, y_hbm.at[rows(b)], zsem).wait()
            return carry

        lax.fori_loop(used_ref[0], N_BLOCKS, fill, 0)
        lax.fori_loop(used_ref[0], N_BLOCKS, drain, 0)


def _ffn(counts, pad_start, used, xs, w_gate, w_up, w_down):
    B = FFN_B
    def wspecs(rows, cols):
        return [pl.BlockSpec((1, 1, rows // W_CHUNKS, cols),
                             lambda e, cnt, pst, used, c=c: (e, c, 0, 0)) for c in range(W_CHUNKS)]

    def chunked(w):
        return w.reshape(MOE_EXPERTS, W_CHUNKS, w.shape[1] // W_CHUNKS, w.shape[2])

    return pl.pallas_call(
        _ffn_kernel,
        out_shape=jax.ShapeDtypeStruct((N_BUF * ROW_TILE, LANES), F32),
        grid_spec=pltpu.PrefetchScalarGridSpec(
            num_scalar_prefetch=3,
            grid=(MOE_EXPERTS,),
            in_specs=[pl.BlockSpec(memory_space=pl.ANY)]
            + wspecs(D_MODEL, MOE_HIDDEN) + wspecs(D_MODEL, MOE_HIDDEN)
            + wspecs(MOE_HIDDEN, D_MODEL),
            out_specs=pl.BlockSpec(memory_space=pl.ANY),
            scratch_shapes=[
                pltpu.VMEM((2, B * ROW_TILE, LANES), F32),
                pltpu.VMEM((2, B * ROW_TILE, LANES), F32),
                pltpu.VMEM((B * ROW_TILE, LANES), F32),
                pltpu.VMEM((D_MODEL, MOE_HIDDEN), BF16),
                pltpu.VMEM((D_MODEL, MOE_HIDDEN), BF16),
                pltpu.VMEM((MOE_HIDDEN, D_MODEL), BF16),
                pltpu.SemaphoreType.DMA((2,)),
                pltpu.SemaphoreType.DMA((2,)),
                pltpu.SemaphoreType.DMA(()),
            ],
        ),
        compiler_params=_params(("arbitrary",)),
        name="expert_ffn",
    )(counts, pad_start, used, xs, *([chunked(w_gate)] * W_CHUNKS), *([chunked(w_up)] * W_CHUNKS),
      *([chunked(w_down)] * W_CHUNKS))


def _combine_kernel(dest_ref, h1_ref, rw_ref, g_ref, y_hbm, o_ref, ybuf, sem):
    tm = COMB_TM
    i = pl.program_id(0)

    def gather(tile, slot):
        def issue(it, carry):
            for u in range(DMA_UNROLL):
                r = it * DMA_UNROLL + u
                a = (tile * tm + r) * 2
                for kk in range(2):
                    d = pl.multiple_of(dest_ref[a + kk] * ROW_TILE, ROW_TILE)
                    pltpu.make_async_copy(
                        y_hbm.at[pl.ds(d, ROW_TILE)],
                        ybuf.at[slot, kk, pl.ds(pl.multiple_of(r * ROW_TILE, ROW_TILE), ROW_TILE)],
                        sem.at[slot, kk]).start()
            return carry

        lax.fori_loop(0, tm // DMA_UNROLL, issue, 0)

    @pl.when(i == 0)
    def _():
        gather(0, 0)

    @pl.when(i + 1 < pl.num_programs(0))
    def _():
        gather(i + 1, (i + 1) % 2)

    slot = i % 2
    for kk in range(2):
        pltpu.make_async_copy(y_hbm.at[pl.ds(0, tm * ROW_TILE)], ybuf.at[slot, kk],
                              sem.at[slot, kk]).wait()
    w = rw_ref[...]
    ys = [jnp.concatenate([ybuf[slot, kk, pl.ds(c, tm, stride=ROW_TILE), :]
                           for c in range(ROW_TILE)], axis=1) for kk in range(2)]
    h = h1_ref[...] + w[:, 0:1] * ys[0] + w[:, 1:2] * ys[1]
    var = jnp.mean(h * h, axis=-1, keepdims=True)
    o_ref[...] = h * lax.rsqrt(var + EPS) * g_ref[...]


def _combine(dest_flat, h1, rw, g, y):
    tm = COMB_TM
    return pl.pallas_call(
        _combine_kernel,
        out_shape=jax.ShapeDtypeStruct((SEQ, D_MODEL), F32),
        grid_spec=pltpu.PrefetchScalarGridSpec(
            num_scalar_prefetch=1,
            grid=(SEQ // tm,),
            in_specs=[
                pl.BlockSpec((tm, D_MODEL), lambda i, d: (i, 0)),
                pl.BlockSpec((tm, LANES), lambda i, d: (i, 0)),
                pl.BlockSpec((1, D_MODEL), lambda i, d: (0, 0)),
                pl.BlockSpec(memory_space=pl.ANY),
            ],
            out_specs=pl.BlockSpec((tm, D_MODEL), lambda i, d: (i, 0)),
            scratch_shapes=[
                pltpu.VMEM((2, 2, tm * ROW_TILE, LANES), F32),
                pltpu.SemaphoreType.DMA((2, 2)),
            ],
        ),
        compiler_params=_params(("arbitrary",)),
        name="moe_combine",
    )(dest_flat, h1, rw, g, y)


def _attention_tables():
    T = ATT_T
    slopes = jnp.exp2(-ALIBI_MAX * jnp.arange(1, DA_HEADS + 1, dtype=F32) / DA_HEADS)
    r = jnp.arange(T)
    hi = ((r // CHUNK) * CHUNK).astype(F32)
    lo = (r % CHUNK).astype(F32)
    ones = jnp.ones((T,), F32)
    sl = slopes[:, None]
    one_h = jnp.broadcast_to(ones, (DA_HEADS, T))
    q_rows = jnp.stack([one_h, one_h, -sl * hi[None], -sl * lo[None]], axis=1)
    k_cols = jnp.stack([sl * hi[None], sl * lo[None], one_h, one_h], axis=-1)
    qa = jnp.zeros((DA_HEADS, DA_HEAD_DIM, T), F32).at[:, 0:4, :].set(q_rows)
    ka1 = jnp.zeros((DA_HEADS, T, LANES), F32).at[:, :, DA_HEAD_DIM:DA_HEAD_DIM + 4].set(k_cols)
    ka2 = jnp.zeros((DA_HEADS, T, LANES), F32).at[:, :, 0:4].set(k_cols)
    rel = (r[:, None] - r[None, :]).astype(F32)
    allowed = (r[:, None] // CHUNK) <= (r[None, :] // CHUNK)
    fix = jnp.where(rel > 0, -2.0 * slopes[:, None, None] * rel[None], 0.0)
    dtab = jnp.where(allowed[None], fix, -jnp.inf)
    return slopes, qa.astype(BF16), ka1.astype(BF16), ka2.astype(BF16), dtab


def _retention_tables():
    C = RET_C
    log_gamma = jnp.log1p(-jnp.exp2(-5.0 - jnp.arange(RET_HEADS, dtype=F32)))
    pos = jnp.arange(C, dtype=F32)
    rel = pos[:, None] - pos[None, :]
    dec = jnp.where(rel >= 0, jnp.exp(log_gamma[:, None, None] * jnp.maximum(rel, 0.0)), 0.0)
    qdec = jnp.exp(log_gamma[:, None] * (pos + 1.0)[None, :])[:, :, None]
    kdec = jnp.exp(log_gamma[:, None] * (C - 1 - pos)[None, :])[:, :, None]
    cd = jnp.exp(log_gamma * C)
    return cd, dec, qdec, kdec


def kernel(x, attn_norm_g, w_in, da_lambda_q1, da_lambda_k1, da_lambda_q2, da_lambda_k2,
           da_subln_g, w_out, ffn_norm_g, router_group_w, router_group_b, router_expert_w,
           router_expert_b, expert_w_gate, expert_w_up, expert_w_down, final_norm_g):
    B, S, D = x.shape
    assert (B, S, D) == (1, SEQ, D_MODEL)
    x2 = x.reshape(S, D)

    w = w_in[0]
    w_main = jnp.concatenate([
        w[:, 512:1024],
        w[:, 1536:1792],
        w[:, 1792:2048] * (RET_QK_DIM ** -0.5),
        w[:, 2048:3072]], axis=1).astype(BF16)
    wq_t = (w[:, 0:512] * (DA_HEAD_DIM ** -0.5)).T.astype(BF16)
    wv_t = w[:, 1024:1536].T.astype(BF16)
    cd, dec, qdec, kdec = _retention_tables()
    k_da, qt4, vt4, o_r = _inproj(x2, attn_norm_g[0][None, :], w_main, wq_t, wv_t,
                                  cd, dec, qdec, kdec)

    slopes, qa, ka1, ka2, dtab = _attention_tables()
    o_da = _attention(k_da, qt4, vt4, slopes, qa, ka1, ka2, dtab, da_lambda_q1, da_lambda_k1,
                      da_lambda_q2, da_lambda_k2, da_subln_g)

    wr = jnp.zeros((D, LANES), F32)
    wr = wr.at[:, :MOE_GROUPS].set(router_group_w[0])
    wr = wr.at[:, MOE_GROUPS:MOE_GROUPS + MOE_EXPERTS].set(router_expert_w[0])
    br = jnp.zeros((1, LANES), F32)
    br = br.at[0, :MOE_GROUPS].set(router_group_b[0])
    br = br.at[0, MOE_GROUPS:MOE_GROUPS + MOE_EXPERTS].set(router_expert_b[0])
    wr_hi = wr.astype(BF16)
    wr_lo = (wr - wr_hi.astype(F32)).astype(BF16)
    h1, xn, ri, rw = _outproj_router(x2, o_da, o_r, w_out[0].astype(BF16), ffn_norm_g[0][None, :],
                                     jnp.concatenate([wr_hi, wr_lo], axis=1), br)

    dest, meta = _plan(ri)
    dest_flat = dest[:, :2].reshape(N_ASSIGN)
    used1 = meta[0, :1]
    counts = meta[1, :MOE_EXPERTS]
    pad_start = meta[2, :MOE_EXPERTS]
    xs = _dispatch(dest_flat, counts, pad_start, used1, xn)

    y = _ffn(counts, pad_start, used1, xs, expert_w_gate[0], expert_w_up[0], expert_w_down[0])
    out = _combine(dest_flat, h1, rw, final_norm_g[None, :], y)
    return out.reshape(B, S, D)
```

```python
import functools
import math

import jax
import jax.numpy as jnp
import numpy as np
from jax import lax
from jax.experimental import pallas as pl
from jax.experimental.pallas import tpu as pltpu

F32 = jnp.float32
BF16 = jnp.bfloat16
I32 = jnp.int32

D_MODEL = 1024
SEQ = 16384
CHUNK = 64
EPS = 1e-6

DA_HEADS = 4
DA_HEAD_DIM = 64
DA_V_DIM = 128
DA_WIDTH = 512
ALIBI_MAX = 8.0
RET_HEADS = 4
RET_QK_DIM = 64
RET_V_DIM = 128
RET_WIDTH = 512
T_ROWS = 512
MAIN_COLS = 2048
DK_OFF = 0
RQ_OFF = 512
RK_OFF = 768
RV_OFF = 1024
RG_OFF = 1536

MOE_GROUPS = 4
MOE_EXPERTS_PER_GROUP = 8
MOE_EXPERTS = 32
MOE_HIDDEN = 512
LAMBDA_INIT = 0.8 - 0.6 * math.exp(-0.3 * 0)

LANES = 128
ROW_TILE = 8
VMEM_LIMIT = 56 * 1024 * 1024

PROJ_TM = 512
ATT_T = 512
RET_C = 256
PLAN_T = 512
FFN_B = 256
N_ASSIGN = 2 * SEQ
N_BLOCKS = N_ASSIGN // FFN_B + MOE_EXPERTS
N_BUF = N_BLOCKS * FFN_B
COMB_TM = 256
DISP_TM = 512
DMA_UNROLL = 8


def _params(sem):
    return pltpu.CompilerParams(dimension_semantics=sem, vmem_limit_bytes=VMEM_LIMIT)


def _retention_block(q_all, k_all, v_all, g_all, cd_ref, dec_ref, qdec_ref, kdec_ref, st_sc):
    outs = []
    for h in range(RET_HEADS):
        qk = slice(h * RET_QK_DIM, (h + 1) * RET_QK_DIM)
        vv = slice(h * RET_V_DIM, (h + 1) * RET_V_DIM)
        q = q_all[:, qk]
        k = k_all[:, qk]
        v = v_all[:, vv]
        g = g_all[:, vv]
        s = lax.dot_general(q, k, (((1,), (1,)), ((), ())),
                            preferred_element_type=F32) * dec_ref[h]
        intra = jnp.dot(s.astype(BF16), v, preferred_element_type=F32)
        st = st_sc[h]
        cross = jnp.dot(q, st.astype(BF16), preferred_element_type=F32) * qdec_ref[h]
        kd = (k.astype(F32) * kdec_ref[h]).astype(BF16)
        st_sc[h] = st * cd_ref[h] + lax.dot_general(kd, v, (((0,), (0,)), ((), ())),
                                                    preferred_element_type=F32)
        o = intra + cross
        o = o * lax.rsqrt(jnp.mean(o * o, axis=-1, keepdims=True) + EPS)
        outs.append(((g / (1.0 + jnp.exp(-g))) * o).astype(BF16))
    return outs


def _inproj_kernel(cd_ref, x_ref, g_ref, w_ref, wq_ref, wv_ref, dec_ref, qdec_ref, kdec_ref,
                   k_ref, qt_ref, vt_ref, or_ref, st_sc):
    @pl.when(pl.program_id(0) == 0)
    def _():
        st_sc[...] = jnp.zeros_like(st_sc)

    x = x_ref[...]
    var = jnp.mean(x * x, axis=-1, keepdims=True)
    xn = (x * lax.rsqrt(var + EPS) * g_ref[...]).astype(BF16)

    def proj(lo, hi):
        return jnp.dot(xn, w_ref[:, lo:hi], preferred_element_type=F32)

    k_ref[...] = proj(DK_OFF, DK_OFF + DA_WIDTH).astype(BF16)
    nt = (((1,), (1,)), ((), ()))
    qt = lax.dot_general(wq_ref[...], xn, nt, preferred_element_type=F32)
    qt_ref[...] = qt.astype(BF16).reshape(DA_HEADS, 1, 2 * DA_HEAD_DIM, PROJ_TM)
    vt = lax.dot_general(wv_ref[...], xn, nt, preferred_element_type=F32)
    vt_ref[...] = vt.astype(BF16).reshape(DA_HEADS, 1, DA_V_DIM, PROJ_TM)

    rq = proj(RQ_OFF, RK_OFF).astype(BF16)
    rk = proj(RK_OFF, RV_OFF).astype(BF16)
    rv = proj(RV_OFF, RG_OFF).astype(BF16)
    rg = proj(RG_OFF, MAIN_COLS)
    for blk in range(PROJ_TM // RET_C):
        rows = slice(blk * RET_C, (blk + 1) * RET_C)
        outs = _retention_block(rq[rows], rk[rows], rv[rows], rg[rows],
                                cd_ref, dec_ref, qdec_ref, kdec_ref, st_sc)
        for h in range(RET_HEADS):
            or_ref[rows, h * RET_V_DIM:(h + 1) * RET_V_DIM] = outs[h]


def _inproj(x2, g, w_bf, wq_t, wv_t, cd, dec, qdec, kdec):
    C = RET_C
    t_shape = jax.ShapeDtypeStruct((DA_HEADS, SEQ // PROJ_TM, LANES, PROJ_TM), BF16)
    t_spec = pl.BlockSpec((DA_HEADS, 1, LANES, PROJ_TM), lambda i: (0, i, 0, 0))
    w_t_spec = pl.BlockSpec((T_ROWS, D_MODEL), lambda i: (0, 0))
    return pl.pallas_call(
        _inproj_kernel,
        out_shape=(jax.ShapeDtypeStruct((SEQ, DA_WIDTH), BF16), t_shape, t_shape,
                   jax.ShapeDtypeStruct((SEQ, RET_WIDTH), BF16)),
        grid=(SEQ // PROJ_TM,),
        in_specs=[
            pl.BlockSpec(memory_space=pltpu.SMEM),
            pl.BlockSpec((PROJ_TM, D_MODEL), lambda i: (i, 0)),
            pl.BlockSpec((1, D_MODEL), lambda i: (0, 0)),
            pl.BlockSpec((D_MODEL, MAIN_COLS), lambda i: (0, 0)),
            w_t_spec, w_t_spec,
            pl.BlockSpec((RET_HEADS, C, C), lambda i: (0, 0, 0)),
            pl.BlockSpec((RET_HEADS, C, 1), lambda i: (0, 0, 0)),
            pl.BlockSpec((RET_HEADS, C, 1), lambda i: (0, 0, 0)),
        ],
        out_specs=(pl.BlockSpec((PROJ_TM, DA_WIDTH), lambda i: (i, 0)), t_spec, t_spec,
                   pl.BlockSpec((PROJ_TM, RET_WIDTH), lambda i: (i, 0))),
        scratch_shapes=[pltpu.VMEM((RET_HEADS, RET_QK_DIM, RET_V_DIM), F32)],
        compiler_params=_params(("arbitrary",)),
        name="inproj_retention",
    )(cd, x2, g, w_bf, wq_t, wv_t, dec, qdec, kdec)


ACC_ROWS = DA_V_DIM + 16


N_QT = SEQ // ATT_T
N_OFF = N_QT * (N_QT - 1) // 2


def _pipeline3(n_pos, scores, accumulate):
    scores(0, 0)
    scores(1, 1)
    steady = n_pos - 2

    def triple(k, carry):
        t = 3 * k
        accumulate(t, 0)
        scores(t + 2, 2)
        accumulate(t + 1, 1)
        scores(t + 3, 0)
        accumulate(t + 2, 2)
        scores(t + 4, 1)
        return carry

    lax.fori_loop(0, steady // 3, triple, 0)
    t0 = steady // 3 * 3
    rem = steady - t0
    accumulate(t0, 0)
    if rem >= 1:
        scores(t0 + 2, 2)
    accumulate(t0 + 1, 1)
    if rem == 2:
        scores(t0 + 3, 0)
    if rem >= 1:
        accumulate(t0 + 2, 2)
    if rem == 2:
        accumulate(t0 + 3, 0)


def _attn_kernel(slope_ref, jt_ref, it_ref, qt_ref, k_ref, vt_ref, qa_ref, ka1_ref, ka2_ref,
                 dtab_ref, lq1_ref, lk1_ref, lq2_ref, lk2_ref, g_ref, o_ref,
                 m_sc, acc_sc, s0_sc, s1_sc, s2_sc, mx0_sc, mx1_sc, mx2_sc):
    T = ATT_T
    h = pl.program_id(0)
    slope = slope_ref[h]
    qa = qa_ref[0]
    lane = lax.broadcasted_iota(I32, (T, LANES), 1)
    sums_row = (lax.broadcasted_iota(I32, (16, T), 0) == 0).astype(BF16)
    s_bufs = (s0_sc, s1_sc, s2_sc)
    mx_bufs = (mx0_sc, mx1_sc, mx2_sc)

    def scores(j, i, buf, extra):
        kt = k_ref[pl.ds(pl.multiple_of(j * T, T), T), :]
        ks = (jnp.where(lane < DA_HEAD_DIM, kt, ka1_ref[0]),
              jnp.where(lane >= DA_HEAD_DIM, kt, ka2_ref[0]))
        qt = qt_ref[0, i]
        qw = (jnp.concatenate([qt[0:DA_HEAD_DIM], qa], axis=0),
              jnp.concatenate([qa, qt[DA_HEAD_DIM:]], axis=0))
        for mp in range(2):
            s = jnp.dot(ks[mp], qw[mp], preferred_element_type=F32)
            if extra is not None:
                s = s + extra[0]
            s_bufs[buf][mp] = s
            mx_bufs[buf][mp] = jnp.max(s, axis=0, keepdims=True)

    def accumulate(j, i, buf):
        c = slope * lax.convert_element_type((i - j) * T, F32)
        vte = jnp.concatenate([vt_ref[0, j], sums_row], axis=0)
        for mp in range(2):
            m_prev = m_sc[i, mp]
            m_new = jnp.maximum(m_prev, mx_bufs[buf][mp] - c)
            p = jnp.exp(s_bufs[buf][mp] - (m_new + c)).astype(BF16)
            pv = jnp.dot(vte, p, preferred_element_type=F32)
            acc_sc[i, mp] = jnp.exp(m_prev - m_new) * acc_sc[i, mp] + pv
            m_sc[i, mp] = m_new

    m_sc[...] = jnp.full_like(m_sc, -jnp.inf)
    acc_sc[...] = jnp.zeros_like(acc_sc)
    _pipeline3(N_QT,
               lambda pos, buf: scores(pos, pos, buf, dtab_ref),
               lambda pos, buf: accumulate(pos, pos, buf))
    _pipeline3(N_OFF,
               lambda pos, buf: scores(jt_ref[pos], it_ref[pos], buf, None),
               lambda pos, buf: accumulate(jt_ref[pos], it_ref[pos], buf))

    lam = (jnp.exp(jnp.sum(lq1_ref[...] * lk1_ref[...], axis=1, keepdims=True))
           - jnp.exp(jnp.sum(lq2_ref[...] * lk2_ref[...], axis=1, keepdims=True))
           + LAMBDA_INIT)

    def finish(i, carry):
        a1 = acc_sc[i, 0]
        a2 = acc_sc[i, 1]
        ot = (a1[0:DA_V_DIM] / a1[DA_V_DIM:DA_V_DIM + 1]
              - lam * (a2[0:DA_V_DIM] / a2[DA_V_DIM:DA_V_DIM + 1]))
        o = ot.T
        var = jnp.mean(o * o, axis=-1, keepdims=True)
        o = (o * lax.rsqrt(var + EPS) * g_ref[...]) * (1.0 - LAMBDA_INIT)
        o_ref[pl.ds(pl.multiple_of(i * T, T), T), :] = o.astype(BF16)
        return carry

    lax.fori_loop(0, N_QT, finish, 0)


def _attention(proj, qt4, vt4, slopes, qa, ka1, ka2, dtab, lq1, lk1, lq2, lk2, subln_g):
    T = ATT_T
    vec64 = pl.BlockSpec((1, DA_HEAD_DIM), lambda h: (0, 0))
    per_head = lambda a, b: pl.BlockSpec((1, a, b), lambda h: (h, 0, 0))
    slab = lambda shape, imap: pl.BlockSpec(shape, imap, pipeline_mode=pl.Buffered(1))
    smem = pl.BlockSpec(memory_space=pltpu.SMEM)
    it_tab, jt_tab = np.tril_indices(N_QT, -1)
    return pl.pallas_call(
        _attn_kernel,
        out_shape=jax.ShapeDtypeStruct((SEQ, DA_WIDTH), BF16),
        grid=(DA_HEADS,),
        in_specs=[
            smem, smem, smem,
            slab((1, N_QT, LANES, T), lambda h: (h, 0, 0, 0)),
            slab((SEQ, LANES), lambda h: (0, DK_OFF // LANES + h)),
            slab((1, N_QT, LANES, T), lambda h: (h, 0, 0, 0)),
            per_head(DA_HEAD_DIM, T), per_head(T, LANES), per_head(T, LANES), per_head(T, T),
            vec64, vec64, vec64, vec64,
            pl.BlockSpec((1, DA_V_DIM), lambda h: (0, 0)),
        ],
        out_specs=slab((SEQ, LANES), lambda h: (0, h)),
        scratch_shapes=[
            pltpu.VMEM((N_QT, 2, 1, T), F32),
            pltpu.VMEM((N_QT, 2, ACC_ROWS, T), F32),
            pltpu.VMEM((2, T, T), F32),
            pltpu.VMEM((2, T, T), F32),
            pltpu.VMEM((2, T, T), F32),
            pltpu.VMEM((2, 1, T), F32),
            pltpu.VMEM((2, 1, T), F32),
            pltpu.VMEM((2, 1, T), F32),
        ],
        compiler_params=_params(("arbitrary",)),
        name="diff_attention",
    )(slopes, jnp.asarray(jt_tab, I32), jnp.asarray(it_tab, I32), qt4, proj, vt4, qa, ka1, ka2,
      dtab, lq1, lk1, lq2, lk2, subln_g)


def _outproj_router_kernel(x_ref, oda_ref, or_ref, wo_ref, g_ref, wr_ref, br_ref,
                           h1_ref, xn_ref, ri_ref, rw_ref):
    h1 = (x_ref[...]
          + jnp.dot(oda_ref[...], wo_ref[0:DA_WIDTH, :], preferred_element_type=F32)
          + jnp.dot(or_ref[...], wo_ref[DA_WIDTH:, :], preferred_element_type=F32))
    h1_ref[...] = h1
    var = jnp.mean(h1 * h1, axis=-1, keepdims=True)
    xn = h1 * lax.rsqrt(var + EPS) * g_ref[...]
    for c in range(ROW_TILE):
        xn_ref[pl.ds(c, PROJ_TM, stride=ROW_TILE), :] = xn[:, c * LANES:(c + 1) * LANES]
    x_hi = xn.astype(BF16)
    x_lo = (xn - x_hi.astype(F32)).astype(BF16)
    both = jnp.dot(x_hi, wr_ref[...], preferred_element_type=F32)
    logits = (both[:, :LANES] + both[:, LANES:]
              + jnp.dot(x_lo, wr_ref[:, :LANES], preferred_element_type=F32)) + br_ref[...]
    lane = lax.broadcasted_iota(I32, logits.shape, 1)
    neg = jnp.float32(-jnp.inf)
    big = jnp.int32(1 << 20)
    gl = jnp.where(lane < MOE_GROUPS, logits, neg)
    gmax = jnp.max(gl, axis=1, keepdims=True)
    gidx = jnp.min(jnp.where(gl == gmax, lane, big), axis=1, keepdims=True)
    gsum = jnp.sum(jnp.exp(gl - gmax), axis=1, keepdims=True)
    gp = 1.0 / gsum
    lo = MOE_GROUPS + gidx * MOE_EXPERTS_PER_GROUP
    el = jnp.where((lane >= lo) & (lane < lo + MOE_EXPERTS_PER_GROUP), logits, neg)
    v1 = jnp.max(el, axis=1, keepdims=True)
    i1 = jnp.min(jnp.where(el == v1, lane, big), axis=1, keepdims=True)
    el2 = jnp.where(lane == i1, neg, el)
    v2 = jnp.max(el2, axis=1, keepdims=True)
    i2 = jnp.min(jnp.where(el2 == v2, lane, big), axis=1, keepdims=True)
    t = jnp.exp(v2 - v1)
    w1 = gp / (1.0 + t)
    w2 = gp * t / (1.0 + t)
    ri_ref[...] = jnp.where(lane == 0, i1 - MOE_GROUPS,
                            jnp.where(lane == 1, i2 - MOE_GROUPS, 0))
    rw_ref[...] = jnp.where(lane == 0, w1, jnp.where(lane == 1, w2, 0.0))


def _outproj_router(x2, o_da, o_r, wo_bf, g, wr, br):
    tm = PROJ_TM
    row = lambda w: pl.BlockSpec((tm, w), lambda i: (i, 0))
    full = lambda a, b: pl.BlockSpec((a, b), lambda i: (0, 0))
    return pl.pallas_call(
        _outproj_router_kernel,
        out_shape=(
            jax.ShapeDtypeStruct((SEQ, D_MODEL), F32),
            jax.ShapeDtypeStruct((SEQ * ROW_TILE, LANES), F32),
            jax.ShapeDtypeStruct((SEQ, LANES), I32),
            jax.ShapeDtypeStruct((SEQ, LANES), F32),
        ),
        grid=(SEQ // tm,),
        in_specs=[row(D_MODEL), row(DA_WIDTH), row(RET_WIDTH), full(D_MODEL, D_MODEL),
                  full(1, D_MODEL), full(D_MODEL, 2 * LANES), full(1, LANES)],
        out_specs=(row(D_MODEL), pl.BlockSpec((tm * ROW_TILE, LANES), lambda i: (i, 0)),
                   row(LANES), row(LANES)),
        compiler_params=_params(("arbitrary",)),
        name="outproj_router",
    )(x2, o_da, o_r, wo_bf, g, wr, br)


def _plan_kernel(ri_ref, dest_ref, used_ref):
    TT = PLAN_T
    lane = lax.broadcasted_iota(I32, (TT, LANES), 1)

    def onehots(t):
        r = ri_ref[pl.ds(pl.multiple_of(t * TT, TT), TT), :]
        return lane == r[:, 0:1], lane == r[:, 1:2]

    def count_body(t, acc):
        oh1, oh2 = onehots(t)
        return acc + jnp.sum((oh1 | oh2).astype(F32), axis=0, keepdims=True)

    counts = lax.fori_loop(0, SEQ // TT, count_body, jnp.zeros((1, LANES), F32))
    counts8 = jnp.broadcast_to(counts, (8, LANES)).astype(I32)
    shift = FFN_B.bit_length() - 1
    padded = ((counts8 + (FFN_B - 1)) >> shift) << shift
    lane8 = lax.broadcasted_iota(I32, (8, LANES), 1)
    pad_end = padded
    sh = 1
    while sh < LANES:
        pad_end = pad_end + jnp.where(lane8 >= sh, pltpu.roll(pad_end, sh, axis=1), 0)
        sh *= 2
    pad_start = pad_end - padded

    ltri = (lax.broadcasted_iota(I32, (TT, TT), 0)
            > lax.broadcasted_iota(I32, (TT, TT), 1)).astype(BF16)

    def dest_body(t, carry):
        oh1, oh2 = onehots(t)
        a = (oh1 | oh2).astype(F32)
        base = jnp.dot(ltri, a.astype(BF16), preferred_element_type=F32) + carry
        d1 = jnp.sum(jnp.where(oh1, base, 0.0), axis=1, keepdims=True)
        d2 = jnp.sum(jnp.where(oh2, base, 0.0), axis=1, keepdims=True)
        dest_ref[pl.ds(pl.multiple_of(t * TT, TT), TT), :] = jnp.where(
            lane == 0, d1, jnp.where(lane == 1, d2, 0.0)).astype(I32)
        return carry + jnp.sum(a, axis=0, keepdims=True)

    lax.fori_loop(0, SEQ // TT, dest_body, pad_start[0:1].astype(F32))

    total = jnp.max(pad_end, axis=1, keepdims=True)
    row8 = lax.broadcasted_iota(I32, (8, LANES), 0)
    used_ref[...] = jnp.where(row8 == 0, jnp.broadcast_to(total >> shift, (8, LANES)),
                              jnp.where(row8 == 1, counts8, pad_start))


def _plan(ri):
    return pl.pallas_call(
        _plan_kernel,
        out_shape=(
            jax.ShapeDtypeStruct((SEQ, LANES), I32),
            jax.ShapeDtypeStruct((8, LANES), I32),
        ),
        compiler_params=pltpu.CompilerParams(vmem_limit_bytes=VMEM_LIMIT),
        name="route_plan",
    )(ri)


PAD_BITS = FFN_B.bit_length() - 1


def _pad_fill_copies(e, cnt_ref, pst_ref, zero_sc, xs_hbm, zsem):
    cnt = cnt_ref[e]
    pad = (-cnt) & (FFN_B - 1)
    row = pst_ref[e] + cnt
    out = []
    for bit in reversed(range(PAD_BITS)):
        n = 1 << bit
        start = row + ((pad >> (bit + 1)) << (bit + 1))
        copy = pltpu.make_async_copy(
            zero_sc.at[pl.ds(0, n * ROW_TILE)],
            xs_hbm.at[pl.ds(pl.multiple_of(start * ROW_TILE, ROW_TILE), n * ROW_TILE)], zsem)
        out.append(((pad & n) != 0, copy))
    return out


def _unused_block_copies(b, zero_sc, xs_hbm, zsem):
    half = FFN_B // 2 * ROW_TILE
    return [pltpu.make_async_copy(
        zero_sc, xs_hbm.at[pl.ds(pl.multiple_of((2 * b + k) * half, half), half)], zsem)
        for k in range(2)]


def _dispatch_kernel(dest_ref, cnt_ref, pst_ref, used_ref, xn_ref, xs_hbm, zero_sc, sem, zsem):
    tm = DISP_TM
    i = pl.program_id(0)

    @pl.when(i == 0)
    def _():
        zero_sc[...] = jnp.zeros_like(zero_sc)

        def fill(e, carry):
            for cond, copy in _pad_fill_copies(e, cnt_ref, pst_ref, zero_sc, xs_hbm, zsem):
                pl.when(cond)(copy.start)
            return carry

        lax.fori_loop(0, MOE_EXPERTS, fill, 0)

        def fill_block(b, carry):
            for copy in _unused_block_copies(b, zero_sc, xs_hbm, zsem):
                copy.start()
            return carry

        lax.fori_loop(used_ref[0], N_BLOCKS, fill_block, 0)

    def issue(it, carry):
        for u in range(DMA_UNROLL):
            r = it * DMA_UNROLL + u
            a = (i * tm + r) * 2
            src = xn_ref.at[pl.ds(pl.multiple_of(r * ROW_TILE, ROW_TILE), ROW_TILE)]
            for kk in range(2):
                d = pl.multiple_of(dest_ref[a + kk] * ROW_TILE, ROW_TILE)
                pltpu.make_async_copy(src, xs_hbm.at[pl.ds(d, ROW_TILE)], sem).start(priority=kk)
        return carry

    lax.fori_loop(0, tm // DMA_UNROLL, issue, 0)
    for _ in range(2):
        pltpu.make_async_copy(xn_ref, xs_hbm.at[pl.ds(0, tm * ROW_TILE)], sem).wait()

    @pl.when(i == 0)
    def _():
        def drain(e, carry):
            for cond, copy in _pad_fill_copies(e, cnt_ref, pst_ref, zero_sc, xs_hbm, zsem):
                pl.when(cond)(copy.wait)
            return carry

        lax.fori_loop(0, MOE_EXPERTS, drain, 0)

        def drain_block(b, carry):
            for copy in _unused_block_copies(b, zero_sc, xs_hbm, zsem):
                copy.wait()
            return carry

        lax.fori_loop(used_ref[0], N_BLOCKS, drain_block, 0)


def _dispatch(dest_flat, counts, pad_start, used, xn3):
    tm = DISP_TM
    return pl.pallas_call(
        _dispatch_kernel,
        out_shape=jax.ShapeDtypeStruct((N_BUF * ROW_TILE, LANES), F32),
        grid_spec=pltpu.PrefetchScalarGridSpec(
            num_scalar_prefetch=4,
            grid=(SEQ // tm,),
            in_specs=[pl.BlockSpec((tm * ROW_TILE, LANES), lambda i, d, c, p, u: (i, 0))],
            out_specs=pl.BlockSpec(memory_space=pl.ANY),
            scratch_shapes=[
                pltpu.VMEM((FFN_B // 2 * ROW_TILE, LANES), F32),
                pltpu.SemaphoreType.DMA(()),
                pltpu.SemaphoreType.DMA(()),
            ],
        ),
        compiler_params=_params(("arbitrary",)),
        name="moe_dispatch",
    )(dest_flat, counts, pad_start, used, xn3)


def _ffn_kernel(cnt_ref, pst_ref, used_ref, xs_hbm, wg_ref, wu_ref, wd_ref, y_hbm,
                xbuf, ybuf, zero_sc, wg_bf, wu_bf, wd_bf, sem_in, sem_out, zsem):
    B = FFN_B
    R = B * ROW_TILE
    e = pl.program_id(0)
    n = (cnt_ref[e] + (B - 1)) >> PAD_BITS
    s0 = pst_ref[e] >> PAD_BITS

    def rows(blk):
        return pl.ds(pl.multiple_of(blk * R, R), R)

    def fetch(blk, slot):
        return pltpu.make_async_copy(xs_hbm.at[rows(blk)], xbuf.at[slot], sem_in.at[slot])

    def flush(blk, slot):
        return pltpu.make_async_copy(ybuf.at[slot], y_hbm.at[rows(blk)], sem_out.at[slot])

    used = used_ref[0]

    @pl.when(e == 0)
    def _():
        fetch(0, 0).start()

    @pl.when(n > 0)
    def _():
        wg_bf[...] = wg_ref[0].astype(BF16)
        wu_bf[...] = wu_ref[0].astype(BF16)
        wd_bf[...] = wd_ref[0].astype(BF16)

    def body(j, carry):
        blk = s0 + j
        slot = blk % 2

        @pl.when(blk + 1 < used)
        def _():
            fetch(blk + 1, 1 - slot).start()

        fetch(blk, slot).wait()

        @pl.when(blk >= 2)
        def _():
            flush(blk - 2, slot).wait()

        x = jnp.concatenate([xbuf[slot, pl.ds(c, B, stride=ROW_TILE), :]
                             for c in range(ROW_TILE)], axis=1).astype(BF16)
        hg = jnp.dot(x, wg_bf[...], preferred_element_type=F32)
        hu = jnp.dot(x, wu_bf[...], preferred_element_type=F32)
        hh = ((hg / (1.0 + jnp.exp(-hg))) * hu).astype(BF16)
        y = jnp.dot(hh, wd_bf[...], preferred_element_type=F32)
        for c in range(ROW_TILE):
            ybuf[slot, pl.ds(c, B, stride=ROW_TILE), :] = y[:, c * LANES:(c + 1) * LANES]
        flush(blk, slot).start()
        return carry

    lax.fori_loop(0, n, body, 0)

    @pl.when(e == MOE_EXPERTS - 1)
    def _():
        flush(used - 1, (used + 1) % 2).wait()
        flush(used - 2, used % 2).wait()
        zero_sc[...] = jnp.zeros_like(zero_sc)

        def fill(b, carry):
            pltpu.make_async_copy(zero_sc, y_hbm.at[rows(b)], zsem).start()
            return carry

        def drain(b, carry):
            pltpu.make_async_copy(zero_sc, y_hbm.at[rows(b)], zsem).wait()
            return carry

        lax.fori_loop(used_ref[0], N_BLOCKS, fill, 0)
        lax.fori_loop(used_ref[0], N_BLOCKS, drain, 0)


def _ffn(counts, pad_start, used, xs, w_gate, w_up, w_down):
    B = FFN_B
    wspec = lambda a, c: pl.BlockSpec((1, a, c), lambda e, cnt, pst, used: (e, 0, 0))
    return pl.pallas_call(
        _ffn_kernel,
        out_shape=jax.ShapeDtypeStruct((N_BUF * ROW_TILE, LANES), F32),
        grid_spec=pltpu.PrefetchScalarGridSpec(
            num_scalar_prefetch=3,
            grid=(MOE_EXPERTS,),
            in_specs=[
                pl.BlockSpec(memory_space=pl.ANY),
                wspec(D_MODEL, MOE_HIDDEN),
                wspec(D_MODEL, MOE_HIDDEN),
                wspec(MOE_HIDDEN, D_MODEL),
            ],
            out_specs=pl.BlockSpec(memory_space=pl.ANY),
            scratch_shapes=[
                pltpu.VMEM((2, B * ROW_TILE, LANES), F32),
                pltpu.VMEM((2, B * ROW_TILE, LANES), F32),
                pltpu.VMEM((B * ROW_TILE, LANES), F32),
                pltpu.VMEM((D_MODEL, MOE_HIDDEN), BF16),
                pltpu.VMEM((D_MODEL, MOE_HIDDEN), BF16),
                pltpu.VMEM((MOE_HIDDEN, D_MODEL), BF16),
                pltpu.SemaphoreType.DMA((2,)),
                pltpu.SemaphoreType.DMA((2,)),
                pltpu.SemaphoreType.DMA(()),
            ],
        ),
        compiler_params=_params(("arbitrary",)),
        name="expert_ffn",
    )(counts, pad_start, used, xs, w_gate, w_up, w_down)


def _combine_kernel(dest_ref, h1_ref, rw_ref, g_ref, y_hbm, o_ref, ybuf, sem):
    tm = COMB_TM
    i = pl.program_id(0)

    def gather(tile, slot):
        def issue(it, carry):
            for u in range(DMA_UNROLL):
                r = it * DMA_UNROLL + u
                a = (tile * tm + r) * 2
                for kk in range(2):
                    d = pl.multiple_of(dest_ref[a + kk] * ROW_TILE, ROW_TILE)
                    pltpu.make_async_copy(
                        y_hbm.at[pl.ds(d, ROW_TILE)],
                        ybuf.at[slot, kk, pl.ds(pl.multiple_of(r * ROW_TILE, ROW_TILE), ROW_TILE)],
                        sem.at[slot, kk]).start(priority=kk)
            return carry

        lax.fori_loop(0, tm // DMA_UNROLL, issue, 0)

    @pl.when(i == 0)
    def _():
        gather(0, 0)

    @pl.when(i + 1 < pl.num_programs(0))
    def _():
        gather(i + 1, (i + 1) % 2)

    slot = i % 2
    for kk in range(2):
        pltpu.make_async_copy(y_hbm.at[pl.ds(0, tm * ROW_TILE)], ybuf.at[slot, kk],
                              sem.at[slot, kk]).wait()
    w = rw_ref[...]
    ys = [jnp.concatenate([ybuf[slot, kk, pl.ds(c, tm, stride=ROW_TILE), :]
                           for c in range(ROW_TILE)], axis=1) for kk in range(2)]
    h = h1_ref[...] + w[:, 0:1] * ys[0] + w[:, 1:2] * ys[1]
    var = jnp.mean(h * h, axis=-1, keepdims=True)
    o_ref[...] = h * lax.rsqrt(var + EPS) * g_ref[...]


def _combine(dest_flat, h1, rw, g, y):
    tm = COMB_TM
    return pl.pallas_call(
        _combine_kernel,
        out_shape=jax.ShapeDtypeStruct((SEQ, D_MODEL), F32),
        grid_spec=pltpu.PrefetchScalarGridSpec(
            num_scalar_prefetch=1,
            grid=(SEQ // tm,),
            in_specs=[
                pl.BlockSpec((tm, D_MODEL), lambda i, d: (i, 0)),
                pl.BlockSpec((tm, LANES), lambda i, d: (i, 0)),
                pl.BlockSpec((1, D_MODEL), lambda i, d: (0, 0)),
                pl.BlockSpec(memory_space=pl.ANY),
            ],
            out_specs=pl.BlockSpec((tm, D_MODEL), lambda i, d: (i, 0)),
            scratch_shapes=[
                pltpu.VMEM((2, 2, tm * ROW_TILE, LANES), F32),
                pltpu.SemaphoreType.DMA((2, 2)),
            ],
        ),
        compiler_params=_params(("arbitrary",)),
        name="moe_combine",
    )(dest_flat, h1, rw, g, y)


def _attention_tables():
    T = ATT_T
    slopes = jnp.exp2(-ALIBI_MAX * jnp.arange(1, DA_HEADS + 1, dtype=F32) / DA_HEADS)
    r = jnp.arange(T)
    hi = ((r // CHUNK) * CHUNK).astype(F32)
    lo = (r % CHUNK).astype(F32)
    ones = jnp.ones((T,), F32)
    sl = slopes[:, None]
    one_h = jnp.broadcast_to(ones, (DA_HEADS, T))
    q_rows = jnp.stack([one_h, one_h, -sl * hi[None], -sl * lo[None]], axis=1)
    k_cols = jnp.stack([sl * hi[None], sl * lo[None], one_h, one_h], axis=-1)
    qa = jnp.zeros((DA_HEADS, DA_HEAD_DIM, T), F32).at[:, 0:4, :].set(q_rows)
    ka1 = jnp.zeros((DA_HEADS, T, LANES), F32).at[:, :, DA_HEAD_DIM:DA_HEAD_DIM + 4].set(k_cols)
    ka2 = jnp.zeros((DA_HEADS, T, LANES), F32).at[:, :, 0:4].set(k_cols)
    rel = (r[:, None] - r[None, :]).astype(F32)
    allowed = (r[:, None] // CHUNK) <= (r[None, :] // CHUNK)
    fix = jnp.where(rel > 0, -2.0 * slopes[:, None, None] * rel[None], 0.0)
    dtab = jnp.where(allowed[None], fix, -jnp.inf)
    return slopes, qa.astype(BF16), ka1.astype(BF16), ka2.astype(BF16), dtab


def _retention_tables():
    C = RET_C
    log_gamma = jnp.log1p(-jnp.exp2(-5.0 - jnp.arange(RET_HEADS, dtype=F32)))
    pos = jnp.arange(C, dtype=F32)
    rel = pos[:, None] - pos[None, :]
    dec = jnp.where(rel >= 0, jnp.exp(log_gamma[:, None, None] * jnp.maximum(rel, 0.0)), 0.0)
    qdec = jnp.exp(log_gamma[:, None] * (pos + 1.0)[None, :])[:, :, None]
    kdec = jnp.exp(log_gamma[:, None] * (C - 1 - pos)[None, :])[:, :, None]
    cd = jnp.exp(log_gamma * C)
    return cd, dec, qdec, kdec


def kernel(x, attn_norm_g, w_in, da_lambda_q1, da_lambda_k1, da_lambda_q2, da_lambda_k2,
           da_subln_g, w_out, ffn_norm_g, router_group_w, router_group_b, router_expert_w,
           router_expert_b, expert_w_gate, expert_w_up, expert_w_down, final_norm_g):
    B, S, D = x.shape
    assert (B, S, D) == (1, SEQ, D_MODEL)
    x2 = x.reshape(S, D)

    w = w_in[0]
    w_main = jnp.concatenate([
        w[:, 512:1024],
        w[:, 1536:1792],
        w[:, 1792:2048] * (RET_QK_DIM ** -0.5),
        w[:, 2048:3072]], axis=1).astype(BF16)
    wq_t = (w[:, 0:512] * (DA_HEAD_DIM ** -0.5)).T.astype(BF16)
    wv_t = w[:, 1024:1536].T.astype(BF16)
    cd, dec, qdec, kdec = _retention_tables()
    k_da, qt4, vt4, o_r = _inproj(x2, attn_norm_g[0][None, :], w_main, wq_t, wv_t,
                                  cd, dec, qdec, kdec)

    slopes, qa, ka1, ka2, dtab = _attention_tables()
    o_da = _attention(k_da, qt4, vt4, slopes, qa, ka1, ka2, dtab, da_lambda_q1, da_lambda_k1,
                      da_lambda_q2, da_lambda_k2, da_subln_g)

    wr = jnp.zeros((D, LANES), F32)
    wr = wr.at[:, :MOE_GROUPS].set(router_group_w[0])
    wr = wr.at[:, MOE_GROUPS:MOE_GROUPS + MOE_EXPERTS].set(router_expert_w[0])
    br = jnp.zeros((1, LANES), F32)
    br = br.at[0, :MOE_GROUPS].set(router_group_b[0])
    br = br.at[0, MOE_GROUPS:MOE_GROUPS + MOE_EXPERTS].set(router_expert_b[0])
    wr_hi = wr.astype(BF16)
    wr_lo = (wr - wr_hi.astype(F32)).astype(BF16)
    h1, xn, ri, rw = _outproj_router(x2, o_da, o_r, w_out[0].astype(BF16), ffn_norm_g[0][None, :],
                                     jnp.concatenate([wr_hi, wr_lo], axis=1), br)

    dest, meta = _plan(ri)
    dest_flat = dest[:, :2].reshape(N_ASSIGN)
    used1 = meta[0, :1]
    counts = meta[1, :MOE_EXPERTS]
    pad_start = meta[2, :MOE_EXPERTS]
    xs = _dispatch(dest_flat, counts, pad_start, used1, xn)

    y = _ffn(counts, pad_start, used1, xs, expert_w_gate[0], expert_w_up[0], expert_w_down[0])
    out = _combine(dest_flat, h1, rw, final_norm_g[None, :], y)
    return out.reshape(B, S, D)
```

```python
import functools
import math

import jax
import jax.numpy as jnp
import numpy as np
from jax import lax
from jax.experimental import pallas as pl
from jax.experimental.pallas import tpu as pltpu

F32 = jnp.float32
BF16 = jnp.bfloat16
I32 = jnp.int32

D_MODEL = 1024
SEQ = 16384
CHUNK = 64
EPS = 1e-6

DA_HEADS = 4
DA_HEAD_DIM = 64
DA_V_DIM = 128
DA_WIDTH = 512
ALIBI_MAX = 8.0
RET_HEADS = 4
RET_QK_DIM = 64
RET_V_DIM = 128
RET_WIDTH = 512
T_ROWS = 512
MAIN_COLS = 2048
DK_OFF = 0
RQ_OFF = 512
RK_OFF = 768
RV_OFF = 1024
RG_OFF = 1536

MOE_GROUPS = 4
MOE_EXPERTS_PER_GROUP = 8
MOE_EXPERTS = 32
MOE_HIDDEN = 512
LAMBDA_INIT = 0.8 - 0.6 * math.exp(-0.3 * 0)

LANES = 128
ROW_TILE = 8
VMEM_LIMIT = 56 * 1024 * 1024

PROJ_TM = 512
ATT_T = 512
RET_C = 256
PLAN_T = 512
FFN_B = 256
N_ASSIGN = 2 * SEQ
N_BLOCKS = N_ASSIGN // FFN_B + MOE_EXPERTS
N_BUF = N_BLOCKS * FFN_B
COMB_TM = 256
DISP_TM = 512
DMA_UNROLL = 8


def _params(sem):
    return pltpu.CompilerParams(dimension_semantics=sem, vmem_limit_bytes=VMEM_LIMIT)


def _retention_block(q_all, k_all, v_all, g_all, cd_ref, dec_ref, qdec_ref, kdec_ref, st_sc):
    outs = []
    for h in range(RET_HEADS):
        qk = slice(h * RET_QK_DIM, (h + 1) * RET_QK_DIM)
        vv = slice(h * RET_V_DIM, (h + 1) * RET_V_DIM)
        q = q_all[:, qk]
        k = k_all[:, qk]
        v = v_all[:, vv]
        g = g_all[:, vv]
        s = lax.dot_general(q, k, (((1,), (1,)), ((), ())),
                            preferred_element_type=F32) * dec_ref[h]
        intra = jnp.dot(s.astype(BF16), v, preferred_element_type=F32)
        st = st_sc[h]
        cross = jnp.dot(q, st.astype(BF16), preferred_element_type=F32) * qdec_ref[h]
        kd = (k.astype(F32) * kdec_ref[h]).astype(BF16)
        st_sc[h] = st * cd_ref[h] + lax.dot_general(kd, v, (((0,), (0,)), ((), ())),
                                                    preferred_element_type=F32)
        o = intra + cross
        o = o * lax.rsqrt(jnp.mean(o * o, axis=-1, keepdims=True) + EPS)
        outs.append(((g / (1.0 + jnp.exp(-g))) * o).astype(BF16))
    return outs


def _inproj_kernel(cd_ref, x_ref, g_ref, w_ref, wq_ref, wv_ref, dec_ref, qdec_ref, kdec_ref,
                   k_ref, qt_ref, vt_ref, or_ref, st_sc):
    @pl.when(pl.program_id(0) == 0)
    def _():
        st_sc[...] = jnp.zeros_like(st_sc)

    x = x_ref[...]
    var = jnp.mean(x * x, axis=-1, keepdims=True)
    xn = (x * lax.rsqrt(var + EPS) * g_ref[...]).astype(BF16)

    def proj(lo, hi):
        return jnp.dot(xn, w_ref[:, lo:hi], preferred_element_type=F32)

    k_ref[...] = proj(DK_OFF, DK_OFF + DA_WIDTH).astype(BF16)
    nt = (((1,), (1,)), ((), ()))
    qt = lax.dot_general(wq_ref[...], xn, nt, preferred_element_type=F32)
    qt_ref[...] = qt.astype(BF16).reshape(DA_HEADS, 1, 2 * DA_HEAD_DIM, PROJ_TM)
    vt = lax.dot_general(wv_ref[...], xn, nt, preferred_element_type=F32)
    vt_ref[...] = vt.astype(BF16).reshape(DA_HEADS, 1, DA_V_DIM, PROJ_TM)

    rq = proj(RQ_OFF, RK_OFF).astype(BF16)
    rk = proj(RK_OFF, RV_OFF).astype(BF16)
    rv = proj(RV_OFF, RG_OFF).astype(BF16)
    rg = proj(RG_OFF, MAIN_COLS)
    for blk in range(PROJ_TM // RET_C):
        rows = slice(blk * RET_C, (blk + 1) * RET_C)
        outs = _retention_block(rq[rows], rk[rows], rv[rows], rg[rows],
                                cd_ref, dec_ref, qdec_ref, kdec_ref, st_sc)
        for h in range(RET_HEADS):
            or_ref[rows, h * RET_V_DIM:(h + 1) * RET_V_DIM] = outs[h]


def _inproj(x2, g, w_bf, wq_t, wv_t, cd, dec, qdec, kdec):
    C = RET_C
    t_shape = jax.ShapeDtypeStruct((DA_HEADS, SEQ // PROJ_TM, LANES, PROJ_TM), BF16)
    t_spec = pl.BlockSpec((DA_HEADS, 1, LANES, PROJ_TM), lambda i: (0, i, 0, 0))
    w_t_spec = pl.BlockSpec((T_ROWS, D_MODEL), lambda i: (0, 0))
    return pl.pallas_call(
        _inproj_kernel,
        out_shape=(jax.ShapeDtypeStruct((SEQ, DA_WIDTH), BF16), t_shape, t_shape,
                   jax.ShapeDtypeStruct((SEQ, RET_WIDTH), BF16)),
        grid=(SEQ // PROJ_TM,),
        in_specs=[
            pl.BlockSpec(memory_space=pltpu.SMEM),
            pl.BlockSpec((PROJ_TM, D_MODEL), lambda i: (i, 0)),
            pl.BlockSpec((1, D_MODEL), lambda i: (0, 0)),
            pl.BlockSpec((D_MODEL, MAIN_COLS), lambda i: (0, 0)),
            w_t_spec, w_t_spec,
            pl.BlockSpec((RET_HEADS, C, C), lambda i: (0, 0, 0)),
            pl.BlockSpec((RET_HEADS, C, 1), lambda i: (0, 0, 0)),
            pl.BlockSpec((RET_HEADS, C, 1), lambda i: (0, 0, 0)),
        ],
        out_specs=(pl.BlockSpec((PROJ_TM, DA_WIDTH), lambda i: (i, 0)), t_spec, t_spec,
                   pl.BlockSpec((PROJ_TM, RET_WIDTH), lambda i: (i, 0))),
        scratch_shapes=[pltpu.VMEM((RET_HEADS, RET_QK_DIM, RET_V_DIM), F32)],
        compiler_params=_params(("arbitrary",)),
        name="inproj_retention",
    )(cd, x2, g, w_bf, wq_t, wv_t, dec, qdec, kdec)


ACC_ROWS = DA_V_DIM + 16


N_QT = SEQ // ATT_T
N_OFF = N_QT * (N_QT - 1) // 2


def _pipeline3(n_pos, scores, accumulate):
    scores(0, 0)
    scores(1, 1)
    steady = n_pos - 2

    def triple(k, carry):
        t = 3 * k
        accumulate(t, 0)
        scores(t + 2, 2)
        accumulate(t + 1, 1)
        scores(t + 3, 0)
        accumulate(t + 2, 2)
        scores(t + 4, 1)
        return carry

    lax.fori_loop(0, steady // 3, triple, 0)
    t0 = steady // 3 * 3
    rem = steady - t0
    accumulate(t0, 0)
    if rem >= 1:
        scores(t0 + 2, 2)
    accumulate(t0 + 1, 1)
    if rem == 2:
        scores(t0 + 3, 0)
    if rem >= 1:
        accumulate(t0 + 2, 2)
    if rem == 2:
        accumulate(t0 + 3, 0)


def _attn_kernel(slope_ref, jt_ref, it_ref, qt_ref, k_ref, vt_ref, qa_ref, ka1_ref, ka2_ref,
                 dtab_ref, lq1_ref, lk1_ref, lq2_ref, lk2_ref, g_ref, o_ref,
                 m_sc, acc_sc, s0_sc, s1_sc, s2_sc, mx0_sc, mx1_sc, mx2_sc):
    T = ATT_T
    h = pl.program_id(0)
    slope = slope_ref[h]
    qa = qa_ref[0]
    lane = lax.broadcasted_iota(I32, (T, LANES), 1)
    sums_row = (lax.broadcasted_iota(I32, (16, T), 0) == 0).astype(BF16)
    s_bufs = (s0_sc, s1_sc, s2_sc)
    mx_bufs = (mx0_sc, mx1_sc, mx2_sc)

    def scores(j, i, buf, extra):
        kt = k_ref[pl.ds(pl.multiple_of(j * T, T), T), :]
        ks = (jnp.where(lane < DA_HEAD_DIM, kt, ka1_ref[0]),
              jnp.where(lane >= DA_HEAD_DIM, kt, ka2_ref[0]))
        qt = qt_ref[0, i]
        qw = (jnp.concatenate([qt[0:DA_HEAD_DIM], qa], axis=0),
              jnp.concatenate([qa, qt[DA_HEAD_DIM:]], axis=0))
        for mp in range(2):
            s = jnp.dot(ks[mp], qw[mp], preferred_element_type=F32)
            if extra is not None:
                s = s + extra[0]
            s_bufs[buf][mp] = s
            mx_bufs[buf][mp] = jnp.max(s, axis=0, keepdims=True)

    def accumulate(j, i, buf):
        c = slope * lax.convert_element_type((i - j) * T, F32)
        vte = jnp.concatenate([vt_ref[0, j], sums_row], axis=0)
        for mp in range(2):
            m_prev = m_sc[i, mp]
            m_new = jnp.maximum(m_prev, mx_bufs[buf][mp] - c)
            p = jnp.exp(s_bufs[buf][mp] - (m_new + c)).astype(BF16)
            pv = jnp.dot(vte, p, preferred_element_type=F32)
            acc_sc[i, mp] = jnp.exp(m_prev - m_new) * acc_sc[i, mp] + pv
            m_sc[i, mp] = m_new

    m_sc[...] = jnp.full_like(m_sc, -jnp.inf)
    acc_sc[...] = jnp.zeros_like(acc_sc)
    _pipeline3(N_QT,
               lambda pos, buf: scores(pos, pos, buf, dtab_ref),
               lambda pos, buf: accumulate(pos, pos, buf))
    _pipeline3(N_OFF,
               lambda pos, buf: scores(jt_ref[pos], it_ref[pos], buf, None),
               lambda pos, buf: accumulate(jt_ref[pos], it_ref[pos], buf))

    lam = (jnp.exp(jnp.sum(lq1_ref[...] * lk1_ref[...], axis=1, keepdims=True))
           - jnp.exp(jnp.sum(lq2_ref[...] * lk2_ref[...], axis=1, keepdims=True))
           + LAMBDA_INIT)

    def finish(i, carry):
        a1 = acc_sc[i, 0]
        a2 = acc_sc[i, 1]
        ot = (a1[0:DA_V_DIM] / a1[DA_V_DIM:DA_V_DIM + 1]
              - lam * (a2[0:DA_V_DIM] / a2[DA_V_DIM:DA_V_DIM + 1]))
        o = ot.T
        var = jnp.mean(o * o, axis=-1, keepdims=True)
        o = (o * lax.rsqrt(var + EPS) * g_ref[...]) * (1.0 - LAMBDA_INIT)
        o_ref[pl.ds(pl.multiple_of(i * T, T), T), :] = o.astype(BF16)
        return carry

    lax.fori_loop(0, N_QT, finish, 0)


def _attention(proj, qt4, vt4, slopes, qa, ka1, ka2, dtab, lq1, lk1, lq2, lk2, subln_g):
    T = ATT_T
    vec64 = pl.BlockSpec((1, DA_HEAD_DIM), lambda h: (0, 0))
    per_head = lambda a, b: pl.BlockSpec((1, a, b), lambda h: (h, 0, 0))
    slab = lambda shape, imap: pl.BlockSpec(shape, imap, pipeline_mode=pl.Buffered(1))
    smem = pl.BlockSpec(memory_space=pltpu.SMEM)
    it_tab, jt_tab = np.tril_indices(N_QT, -1)
    return pl.pallas_call(
        _attn_kernel,
        out_shape=jax.ShapeDtypeStruct((SEQ, DA_WIDTH), BF16),
        grid=(DA_HEADS,),
        in_specs=[
            smem, smem, smem,
            slab((1, N_QT, LANES, T), lambda h: (h, 0, 0, 0)),
            slab((SEQ, LANES), lambda h: (0, DK_OFF // LANES + h)),
            slab((1, N_QT, LANES, T), lambda h: (h, 0, 0, 0)),
            per_head(DA_HEAD_DIM, T), per_head(T, LANES), per_head(T, LANES), per_head(T, T),
            vec64, vec64, vec64, vec64,
            pl.BlockSpec((1, DA_V_DIM), lambda h: (0, 0)),
        ],
        out_specs=slab((SEQ, LANES), lambda h: (0, h)),
        scratch_shapes=[
            pltpu.VMEM((N_QT, 2, 1, T), F32),
            pltpu.VMEM((N_QT, 2, ACC_ROWS, T), F32),
            pltpu.VMEM((2, T, T), F32),
            pltpu.VMEM((2, T, T), F32),
            pltpu.VMEM((2, T, T), F32),
            pltpu.VMEM((2, 1, T), F32),
            pltpu.VMEM((2, 1, T), F32),
            pltpu.VMEM((2, 1, T), F32),
        ],
        compiler_params=_params(("arbitrary",)),
        name="diff_attention",
    )(slopes, jnp.asarray(jt_tab, I32), jnp.asarray(it_tab, I32), qt4, proj, vt4, qa, ka1, ka2,
      dtab, lq1, lk1, lq2, lk2, subln_g)


def _outproj_router_kernel(x_ref, oda_ref, or_ref, wo_ref, g_ref, wr_ref, br_ref,
                           h1_ref, xn_ref, ri_ref, rw_ref):
    h1 = (x_ref[...]
          + jnp.dot(oda_ref[...], wo_ref[0:DA_WIDTH, :], preferred_element_type=F32)
          + jnp.dot(or_ref[...], wo_ref[DA_WIDTH:, :], preferred_element_type=F32))
    h1_ref[...] = h1
    var = jnp.mean(h1 * h1, axis=-1, keepdims=True)
    xn = h1 * lax.rsqrt(var + EPS) * g_ref[...]
    for c in range(ROW_TILE):
        xn_ref[pl.ds(c, PROJ_TM, stride=ROW_TILE), :] = xn[:, c * LANES:(c + 1) * LANES]
    x_hi = xn.astype(BF16)
    x_lo = (xn - x_hi.astype(F32)).astype(BF16)
    both = jnp.dot(x_hi, wr_ref[...], preferred_element_type=F32)
    logits = (both[:, :LANES] + both[:, LANES:]
              + jnp.dot(x_lo, wr_ref[:, :LANES], preferred_element_type=F32)) + br_ref[...]
    lane = lax.broadcasted_iota(I32, logits.shape, 1)
    neg = jnp.float32(-jnp.inf)
    big = jnp.int32(1 << 20)
    gl = jnp.where(lane < MOE_GROUPS, logits, neg)
    gmax = jnp.max(gl, axis=1, keepdims=True)
    gidx = jnp.min(jnp.where(gl == gmax, lane, big), axis=1, keepdims=True)
    gsum = jnp.sum(jnp.exp(gl - gmax), axis=1, keepdims=True)
    gp = 1.0 / gsum
    lo = MOE_GROUPS + gidx * MOE_EXPERTS_PER_GROUP
    el = jnp.where((lane >= lo) & (lane < lo + MOE_EXPERTS_PER_GROUP), logits, neg)
    v1 = jnp.max(el, axis=1, keepdims=True)
    i1 = jnp.min(jnp.where(el == v1, lane, big), axis=1, keepdims=True)
    el2 = jnp.where(lane == i1, neg, el)
    v2 = jnp.max(el2, axis=1, keepdims=True)
    i2 = jnp.min(jnp.where(el2 == v2, lane, big), axis=1, keepdims=True)
    t = jnp.exp(v2 - v1)
    w1 = gp / (1.0 + t)
    w2 = gp * t / (1.0 + t)
    ri_ref[...] = jnp.where(lane == 0, i1 - MOE_GROUPS,
                            jnp.where(lane == 1, i2 - MOE_GROUPS, 0))
    rw_ref[...] = jnp.where(lane == 0, w1, jnp.where(lane == 1, w2, 0.0))


def _outproj_router(x2, o_da, o_r, wo_bf, g, wr, br):
    tm = PROJ_TM
    row = lambda w: pl.BlockSpec((tm, w), lambda i: (i, 0))
    full = lambda a, b: pl.BlockSpec((a, b), lambda i: (0, 0))
    return pl.pallas_call(
        _outproj_router_kernel,
        out_shape=(
            jax.ShapeDtypeStruct((SEQ, D_MODEL), F32),
            jax.ShapeDtypeStruct((SEQ * ROW_TILE, LANES), F32),
            jax.ShapeDtypeStruct((SEQ, LANES), I32),
            jax.ShapeDtypeStruct((SEQ, LANES), F32),
        ),
        grid=(SEQ // tm,),
        in_specs=[row(D_MODEL), row(DA_WIDTH), row(RET_WIDTH), full(D_MODEL, D_MODEL),
                  full(1, D_MODEL), full(D_MODEL, 2 * LANES), full(1, LANES)],
        out_specs=(row(D_MODEL), pl.BlockSpec((tm * ROW_TILE, LANES), lambda i: (i, 0)),
                   row(LANES), row(LANES)),
        compiler_params=_params(("arbitrary",)),
        name="outproj_router",
    )(x2, o_da, o_r, wo_bf, g, wr, br)


def _plan_kernel(ri_ref, dest_ref, used_ref):
    TT = PLAN_T
    lane = lax.broadcasted_iota(I32, (TT, LANES), 1)

    def onehots(t):
        r = ri_ref[pl.ds(pl.multiple_of(t * TT, TT), TT), :]
        return lane == r[:, 0:1], lane == r[:, 1:2]

    def count_body(t, acc):
        oh1, oh2 = onehots(t)
        return acc + jnp.sum((oh1 | oh2).astype(F32), axis=0, keepdims=True)

    counts = lax.fori_loop(0, SEQ // TT, count_body, jnp.zeros((1, LANES), F32))
    counts8 = jnp.broadcast_to(counts, (8, LANES)).astype(I32)
    shift = FFN_B.bit_length() - 1
    padded = ((counts8 + (FFN_B - 1)) >> shift) << shift
    lane8 = lax.broadcasted_iota(I32, (8, LANES), 1)
    pad_end = padded
    sh = 1
    while sh < LANES:
        pad_end = pad_end + jnp.where(lane8 >= sh, pltpu.roll(pad_end, sh, axis=1), 0)
        sh *= 2
    pad_start = pad_end - padded

    ltri = (lax.broadcasted_iota(I32, (TT, TT), 0)
            > lax.broadcasted_iota(I32, (TT, TT), 1)).astype(BF16)

    def dest_body(t, carry):
        oh1, oh2 = onehots(t)
        a = (oh1 | oh2).astype(F32)
        base = jnp.dot(ltri, a.astype(BF16), preferred_element_type=F32) + carry
        d1 = jnp.sum(jnp.where(oh1, base, 0.0), axis=1, keepdims=True)
        d2 = jnp.sum(jnp.where(oh2, base, 0.0), axis=1, keepdims=True)
        dest_ref[pl.ds(pl.multiple_of(t * TT, TT), TT), :] = jnp.where(
            lane == 0, d1, jnp.where(lane == 1, d2, 0.0)).astype(I32)
        return carry + jnp.sum(a, axis=0, keepdims=True)

    lax.fori_loop(0, SEQ // TT, dest_body, pad_start[0:1].astype(F32))

    total = jnp.max(pad_end, axis=1, keepdims=True)
    row8 = lax.broadcasted_iota(I32, (8, LANES), 0)
    used_ref[...] = jnp.where(row8 == 0, jnp.broadcast_to(total >> shift, (8, LANES)),
                              jnp.where(row8 == 1, counts8, pad_start))


def _plan(ri):
    return pl.pallas_call(
        _plan_kernel,
        out_shape=(
            jax.ShapeDtypeStruct((SEQ, LANES), I32),
            jax.ShapeDtypeStruct((8, LANES), I32),
        ),
        compiler_params=pltpu.CompilerParams(vmem_limit_bytes=VMEM_LIMIT),
        name="route_plan",
    )(ri)


PAD_BITS = FFN_B.bit_length() - 1


def _pad_fill_copies(e, cnt_ref, pst_ref, zero_sc, xs_hbm, zsem):
    cnt = cnt_ref[e]
    pad = (-cnt) & (FFN_B - 1)
    row = pst_ref[e] + cnt
    out = []
    for bit in reversed(range(PAD_BITS)):
        n = 1 << bit
        start = row + ((pad >> (bit + 1)) << (bit + 1))
        copy = pltpu.make_async_copy(
            zero_sc.at[pl.ds(0, n * ROW_TILE)],
            xs_hbm.at[pl.ds(pl.multiple_of(start * ROW_TILE, ROW_TILE), n * ROW_TILE)], zsem)
        out.append(((pad & n) != 0, copy))
    return out


def _unused_block_copies(b, zero_sc, xs_hbm, zsem):
    half = FFN_B // 2 * ROW_TILE
    return [pltpu.make_async_copy(
        zero_sc, xs_hbm.at[pl.ds(pl.multiple_of((2 * b + k) * half, half), half)], zsem)
        for k in range(2)]


def _dispatch_kernel(dest_ref, cnt_ref, pst_ref, used_ref, xn_ref, xs_hbm, zero_sc, sem, zsem):
    tm = DISP_TM
    i = pl.program_id(0)

    @pl.when(i == 0)
    def _():
        zero_sc[...] = jnp.zeros_like(zero_sc)

        def fill(e, carry):
            for cond, copy in _pad_fill_copies(e, cnt_ref, pst_ref, zero_sc, xs_hbm, zsem):
                pl.when(cond)(copy.start)
            return carry

        lax.fori_loop(0, MOE_EXPERTS, fill, 0)

        def fill_block(b, carry):
            for copy in _unused_block_copies(b, zero_sc, xs_hbm, zsem):
                copy.start()
            return carry

        lax.fori_loop(used_ref[0], N_BLOCKS, fill_block, 0)

    def issue(it, carry):
        for u in range(DMA_UNROLL):
            r = it * DMA_UNROLL + u
            a = (i * tm + r) * 2
            src = xn_ref.at[pl.ds(pl.multiple_of(r * ROW_TILE, ROW_TILE), ROW_TILE)]
            for kk in range(2):
                d = pl.multiple_of(dest_ref[a + kk] * ROW_TILE, ROW_TILE)
                pltpu.make_async_copy(src, xs_hbm.at[pl.ds(d, ROW_TILE)], sem).start(priority=kk)
        return carry

    lax.fori_loop(0, tm // DMA_UNROLL, issue, 0)
    for _ in range(2):
        pltpu.make_async_copy(xn_ref, xs_hbm.at[pl.ds(0, tm * ROW_TILE)], sem).wait()

    @pl.when(i == 0)
    def _():
        def drain(e, carry):
            for cond, copy in _pad_fill_copies(e, cnt_ref, pst_ref, zero_sc, xs_hbm, zsem):
                pl.when(cond)(copy.wait)
            return carry

        lax.fori_loop(0, MOE_EXPERTS, drain, 0)

        def drain_block(b, carry):
            for copy in _unused_block_copies(b, zero_sc, xs_hbm, zsem):
                copy.wait()
            return carry

        lax.fori_loop(used_ref[0], N_BLOCKS, drain_block, 0)


def _dispatch(dest_flat, counts, pad_start, used, xn3):
    tm = DISP_TM
    return pl.pallas_call(
        _dispatch_kernel,
        out_shape=jax.ShapeDtypeStruct((N_BUF * ROW_TILE, LANES), F32),
        grid_spec=pltpu.PrefetchScalarGridSpec(
            num_scalar_prefetch=4,
            grid=(SEQ // tm,),
            in_specs=[pl.BlockSpec((tm * ROW_TILE, LANES), lambda i, d, c, p, u: (i, 0))],
            out_specs=pl.BlockSpec(memory_space=pl.ANY),
            scratch_shapes=[
                pltpu.VMEM((FFN_B // 2 * ROW_TILE, LANES), F32),
                pltpu.SemaphoreType.DMA(()),
                pltpu.SemaphoreType.DMA(()),
            ],
        ),
        compiler_params=_params(("arbitrary",)),
        name="moe_dispatch",
    )(dest_flat, counts, pad_start, used, xn3)


BLOCK_COPY_PRIORITY = 1


def _ffn_kernel(cnt_ref, pst_ref, used_ref, xs_hbm, wg_ref, wu_ref, wd_ref, y_hbm,
                xbuf, ybuf, zero_sc, wg_bf, wu_bf, wd_bf, sem_in, sem_out, zsem):
    B = FFN_B
    R = B * ROW_TILE
    e = pl.program_id(0)
    n = (cnt_ref[e] + (B - 1)) >> PAD_BITS
    s0 = pst_ref[e] >> PAD_BITS

    def rows(blk):
        return pl.ds(pl.multiple_of(blk * R, R), R)

    def fetch(blk, slot):
        return pltpu.make_async_copy(xs_hbm.at[rows(blk)], xbuf.at[slot], sem_in.at[slot])

    def flush(blk, slot):
        return pltpu.make_async_copy(ybuf.at[slot], y_hbm.at[rows(blk)], sem_out.at[slot])

    used = used_ref[0]

    @pl.when(e == 0)
    def _():
        fetch(0, 0).start(priority=BLOCK_COPY_PRIORITY)

    @pl.when(n > 0)
    def _():
        wg_bf[...] = wg_ref[0].astype(BF16)
        wu_bf[...] = wu_ref[0].astype(BF16)
        wd_bf[...] = wd_ref[0].astype(BF16)

    def body(j, carry):
        blk = s0 + j
        slot = blk % 2

        @pl.when(blk + 1 < used)
        def _():
            fetch(blk + 1, 1 - slot).start(priority=BLOCK_COPY_PRIORITY)

        fetch(blk, slot).wait()

        @pl.when(blk >= 2)
        def _():
            flush(blk - 2, slot).wait()

        x = jnp.concatenate([xbuf[slot, pl.ds(c, B, stride=ROW_TILE), :]
                             for c in range(ROW_TILE)], axis=1).astype(BF16)
        hg = jnp.dot(x, wg_bf[...], preferred_element_type=F32)
        hu = jnp.dot(x, wu_bf[...], preferred_element_type=F32)
        hh = ((hg / (1.0 + jnp.exp(-hg))) * hu).astype(BF16)
        y = jnp.dot(hh, wd_bf[...], preferred_element_type=F32)
        for c in range(ROW_TILE):
            ybuf[slot, pl.ds(c, B, stride=ROW_TILE), :] = y[:, c * LANES:(c + 1) * LANES]
        flush(blk, slot).start(priority=BLOCK_COPY_PRIORITY)
        return carry

    lax.fori_loop(0, n, body, 0)

    @pl.when(e == MOE_EXPERTS - 1)
    def _():
        flush(used - 1, (used + 1) % 2).wait()
        flush(used - 2, used % 2).wait()
        zero_sc[...] = jnp.zeros_like(zero_sc)

        def fill(b, carry):
            pltpu.make_async_copy(zero_sc, y_hbm.at[rows(b)], zsem).start()
            return carry

        def drain(b, carry):
            pltpu.make_async_copy(zero_sc, y_hbm.at[rows(b)], zsem).wait()
            return carry

        lax.fori_loop(used_ref[0], N_BLOCKS, fill, 0)
        lax.fori_loop(used_ref[0], N_BLOCKS, drain, 0)


def _ffn(counts, pad_start, used, xs, w_gate, w_up, w_down):
    B = FFN_B
    wspec = lambda a, c: pl.BlockSpec((1, a, c), lambda e, cnt, pst, used: (e, 0, 0))
    return pl.pallas_call(
        _ffn_kernel,
        out_shape=jax.ShapeDtypeStruct((N_BUF * ROW_TILE, LANES), F32),
        grid_spec=pltpu.PrefetchScalarGridSpec(
            num_scalar_prefetch=3,
            grid=(MOE_EXPERTS,),
            in_specs=[
                pl.BlockSpec(memory_space=pl.ANY),
                wspec(D_MODEL, MOE_HIDDEN),
                wspec(D_MODEL, MOE_HIDDEN),
                wspec(MOE_HIDDEN, D_MODEL),
            ],
            out_specs=pl.BlockSpec(memory_space=pl.ANY),
            scratch_shapes=[
                pltpu.VMEM((2, B * ROW_TILE, LANES), F32),
                pltpu.VMEM((2, B * ROW_TILE, LANES), F32),
                pltpu.VMEM((B * ROW_TILE, LANES), F32),
                pltpu.VMEM((D_MODEL, MOE_HIDDEN), BF16),
                pltpu.VMEM((D_MODEL, MOE_HIDDEN), BF16),
                pltpu.VMEM((MOE_HIDDEN, D_MODEL), BF16),
                pltpu.SemaphoreType.DMA((2,)),
                pltpu.SemaphoreType.DMA((2,)),
                pltpu.SemaphoreType.DMA(()),
            ],
        ),
        compiler_params=_params(("arbitrary",)),
        name="expert_ffn",
    )(counts, pad_start, used, xs, w_gate, w_up, w_down)


def _combine_kernel(dest_ref, h1_ref, rw_ref, g_ref, y_hbm, o_ref, ybuf, sem):
    tm = COMB_TM
    i = pl.program_id(0)

    def gather(tile, slot):
        def issue(it, carry):
            for u in range(DMA_UNROLL):
                r = it * DMA_UNROLL + u
                a = (tile * tm + r) * 2
                for kk in range(2):
                    d = pl.multiple_of(dest_ref[a + kk] * ROW_TILE, ROW_TILE)
                    pltpu.make_async_copy(
                        y_hbm.at[pl.ds(d, ROW_TILE)],
                        ybuf.at[slot, kk, pl.ds(pl.multiple_of(r * ROW_TILE, ROW_TILE), ROW_TILE)],
                        sem.at[slot, kk]).start(priority=kk)
            return carry

        lax.fori_loop(0, tm // DMA_UNROLL, issue, 0)

    @pl.when(i == 0)
    def _():
        gather(0, 0)

    @pl.when(i + 1 < pl.num_programs(0))
    def _():
        gather(i + 1, (i + 1) % 2)

    slot = i % 2
    for kk in range(2):
        pltpu.make_async_copy(y_hbm.at[pl.ds(0, tm * ROW_TILE)], ybuf.at[slot, kk],
                              sem.at[slot, kk]).wait()
    w = rw_ref[...]
    ys = [jnp.concatenate([ybuf[slot, kk, pl.ds(c, tm, stride=ROW_TILE), :]
                           for c in range(ROW_TILE)], axis=1) for kk in range(2)]
    h = h1_ref[...] + w[:, 0:1] * ys[0] + w[:, 1:2] * ys[1]
    var = jnp.mean(h * h, axis=-1, keepdims=True)
    o_ref[...] = h * lax.rsqrt(var + EPS) * g_ref[...]


def _combine(dest_flat, h1, rw, g, y):
    tm = COMB_TM
    return pl.pallas_call(
        _combine_kernel,
        out_shape=jax.ShapeDtypeStruct((SEQ, D_MODEL), F32),
        grid_spec=pltpu.PrefetchScalarGridSpec(
            num_scalar_prefetch=1,
            grid=(SEQ // tm,),
            in_specs=[
                pl.BlockSpec((tm, D_MODEL), lambda i, d: (i, 0)),
                pl.BlockSpec((tm, LANES), lambda i, d: (i, 0)),
                pl.BlockSpec((1, D_MODEL), lambda i, d: (0, 0)),
                pl.BlockSpec(memory_space=pl.ANY),
            ],
            out_specs=pl.BlockSpec((tm, D_MODEL), lambda i, d: (i, 0)),
            scratch_shapes=[
                pltpu.VMEM((2, 2, tm * ROW_TILE, LANES), F32),
                pltpu.SemaphoreType.DMA((2, 2)),
            ],
        ),
        compiler_params=_params(("arbitrary",)),
        name="moe_combine",
    )(dest_flat, h1, rw, g, y)


def _attention_tables():
    T = ATT_T
    slopes = jnp.exp2(-ALIBI_MAX * jnp.arange(1, DA_HEADS + 1, dtype=F32) / DA_HEADS)
    r = jnp.arange(T)
    hi = ((r // CHUNK) * CHUNK).astype(F32)
    lo = (r % CHUNK).astype(F32)
    ones = jnp.ones((T,), F32)
    sl = slopes[:, None]
    one_h = jnp.broadcast_to(ones, (DA_HEADS, T))
    q_rows = jnp.stack([one_h, one_h, -sl * hi[None], -sl * lo[None]], axis=1)
    k_cols = jnp.stack([sl * hi[None], sl * lo[None], one_h, one_h], axis=-1)
    qa = jnp.zeros((DA_HEADS, DA_HEAD_DIM, T), F32).at[:, 0:4, :].set(q_rows)
    ka1 = jnp.zeros((DA_HEADS, T, LANES), F32).at[:, :, DA_HEAD_DIM:DA_HEAD_DIM + 4].set(k_cols)
    ka2 = jnp.zeros((DA_HEADS, T, LANES), F32).at[:, :, 0:4].set(k_cols)
    rel = (r[:, None] - r[None, :]).astype(F32)
    allowed = (r[:, None] // CHUNK) <= (r[None, :] // CHUNK)
    fix = jnp.where(rel > 0, -2.0 * slopes[:, None, None] * rel[None], 0.0)
    dtab = jnp.where(allowed[None], fix, -jnp.inf)
    return slopes, qa.astype(BF16), ka1.astype(BF16), ka2.astype(BF16), dtab


def _retention_tables():
    C = RET_C
    log_gamma = jnp.log1p(-jnp.exp2(-5.0 - jnp.arange(RET_HEADS, dtype=F32)))
    pos = jnp.arange(C, dtype=F32)
    rel = pos[:, None] - pos[None, :]
    dec = jnp.where(rel >= 0, jnp.exp(log_gamma[:, None, None] * jnp.maximum(rel, 0.0)), 0.0)
    qdec = jnp.exp(log_gamma[:, None] * (pos + 1.0)[None, :])[:, :, None]
    kdec = jnp.exp(log_gamma[:, None] * (C - 1 - pos)[None, :])[:, :, None]
    cd = jnp.exp(log_gamma * C)
    return cd, dec, qdec, kdec


def kernel(x, attn_norm_g, w_in, da_lambda_q1, da_lambda_k1, da_lambda_q2, da_lambda_k2,
           da_subln_g, w_out, ffn_norm_g, router_group_w, router_group_b, router_expert_w,
           router_expert_b, expert_w_gate, expert_w_up, expert_w_down, final_norm_g):
    B, S, D = x.shape
    assert (B, S, D) == (1, SEQ, D_MODEL)
    x2 = x.reshape(S, D)

    w = w_in[0]
    w_main = jnp.concatenate([
        w[:, 512:1024],
        w[:, 1536:1792],
        w[:, 1792:2048] * (RET_QK_DIM ** -0.5),
        w[:, 2048:3072]], axis=1).astype(BF16)
    wq_t = (w[:, 0:512] * (DA_HEAD_DIM ** -0.5)).T.astype(BF16)
    wv_t = w[:, 1024:1536].T.astype(BF16)
    cd, dec, qdec, kdec = _retention_tables()
    k_da, qt4, vt4, o_r = _inproj(x2, attn_norm_g[0][None, :], w_main, wq_t, wv_t,
                                  cd, dec, qdec, kdec)

    slopes, qa, ka1, ka2, dtab = _attention_tables()
    o_da = _attention(k_da, qt4, vt4, slopes, qa, ka1, ka2, dtab, da_lambda_q1, da_lambda_k1,
                      da_lambda_q2, da_lambda_k2, da_subln_g)

    wr = jnp.zeros((D, LANES), F32)
    wr = wr.at[:, :MOE_GROUPS].set(router_group_w[0])
    wr = wr.at[:, MOE_GROUPS:MOE_GROUPS + MOE_EXPERTS].set(router_expert_w[0])
    br = jnp.zeros((1, LANES), F32)
    br = br.at[0, :MOE_GROUPS].set(router_group_b[0])
    br = br.at[0, MOE_GROUPS:MOE_GROUPS + MOE_EXPERTS].set(router_expert_b[0])
    wr_hi = wr.astype(BF16)
    wr_lo = (wr - wr_hi.astype(F32)).astype(BF16)
    h1, xn, ri, rw = _outproj_router(x2, o_da, o_r, w_out[0].astype(BF16), ffn_norm_g[0][None, :],
                                     jnp.concatenate([wr_hi, wr_lo], axis=1), br)

    dest, meta = _plan(ri)
    dest_flat = dest[:, :2].reshape(N_ASSIGN)
    used1 = meta[0, :1]
    counts = meta[1, :MOE_EXPERTS]
    pad_start = meta[2, :MOE_EXPERTS]
    xs = _dispatch(dest_flat, counts, pad_start, used1, xn)

    y = _ffn(counts, pad_start, used1, xs, expert_w_gate[0], expert_w_up[0], expert_w_down[0])
    out = _combine(dest_flat, h1, rw, final_norm_g[None, :], y)
    return out.reshape(B, S, D)
```

```python
import functools
import math

import jax
import jax.numpy as jnp
import numpy as np
from jax import lax
from jax.experimental import pallas as pl
from jax.experimental.pallas import tpu as pltpu

F32 = jnp.float32
BF16 = jnp.bfloat16
I32 = jnp.int32

D_MODEL = 1024
SEQ = 16384
CHUNK = 64
EPS = 1e-6

DA_HEADS = 4
DA_HEAD_DIM = 64
DA_V_DIM = 128
DA_WIDTH = 512
ALIBI_MAX = 8.0
RET_HEADS = 4
RET_QK_DIM = 64
RET_V_DIM = 128
RET_WIDTH = 512
T_ROWS = 512
MAIN_COLS = 2048
DK_OFF = 0
RQ_OFF = 512
RK_OFF = 768
RV_OFF = 1024
RG_OFF = 1536

MOE_GROUPS = 4
MOE_EXPERTS_PER_GROUP = 8
MOE_EXPERTS = 32
MOE_HIDDEN = 512
LAMBDA_INIT = 0.8 - 0.6 * math.exp(-0.3 * 0)

LANES = 128
ROW_TILE = 8
VMEM_LIMIT = 56 * 1024 * 1024

PROJ_TM = 512
ATT_T = 512
RET_C = 256
PLAN_T = 512
FFN_B = 512
N_ASSIGN = 2 * SEQ
N_BLOCKS = N_ASSIGN // FFN_B + MOE_EXPERTS
N_BUF = N_BLOCKS * FFN_B
COMB_TM = 256
DISP_TM = 512
DMA_UNROLL = 8


def _params(sem):
    return pltpu.CompilerParams(dimension_semantics=sem, vmem_limit_bytes=VMEM_LIMIT)


def _retention_block(q_all, k_all, v_all, g_all, cd_ref, dec_ref, qdec_ref, kdec_ref, st_sc):
    outs = []
    for h in range(RET_HEADS):
        qk = slice(h * RET_QK_DIM, (h + 1) * RET_QK_DIM)
        vv = slice(h * RET_V_DIM, (h + 1) * RET_V_DIM)
        q = q_all[:, qk]
        k = k_all[:, qk]
        v = v_all[:, vv]
        g = g_all[:, vv]
        s = lax.dot_general(q, k, (((1,), (1,)), ((), ())),
                            preferred_element_type=F32) * dec_ref[h]
        intra = jnp.dot(s.astype(BF16), v, preferred_element_type=F32)
        st = st_sc[h]
        cross = jnp.dot(q, st.astype(BF16), preferred_element_type=F32) * qdec_ref[h]
        kd = (k.astype(F32) * kdec_ref[h]).astype(BF16)
        st_sc[h] = st * cd_ref[h] + lax.dot_general(kd, v, (((0,), (0,)), ((), ())),
                                                    preferred_element_type=F32)
        o = intra + cross
        o = o * lax.rsqrt(jnp.mean(o * o, axis=-1, keepdims=True) + EPS)
        outs.append(((g / (1.0 + jnp.exp(-g))) * o).astype(BF16))
    return outs


def _inproj_kernel(cd_ref, x_ref, g_ref, w_ref, wq_ref, wv_ref, dec_ref, qdec_ref, kdec_ref,
                   k_ref, qt_ref, vt_ref, or_ref, st_sc):
    @pl.when(pl.program_id(0) == 0)
    def _():
        st_sc[...] = jnp.zeros_like(st_sc)

    x = x_ref[...]
    var = jnp.mean(x * x, axis=-1, keepdims=True)
    xn = (x * lax.rsqrt(var + EPS) * g_ref[...]).astype(BF16)

    def proj(lo, hi):
        return jnp.dot(xn, w_ref[:, lo:hi], preferred_element_type=F32)

    k_ref[...] = proj(DK_OFF, DK_OFF + DA_WIDTH).astype(BF16)
    nt = (((1,), (1,)), ((), ()))
    qt = lax.dot_general(wq_ref[...], xn, nt, preferred_element_type=F32)
    qt_ref[...] = qt.astype(BF16).reshape(DA_HEADS, 1, 2 * DA_HEAD_DIM, PROJ_TM)
    vt = lax.dot_general(wv_ref[...], xn, nt, preferred_element_type=F32)
    vt_ref[...] = vt.astype(BF16).reshape(DA_HEADS, 1, DA_V_DIM, PROJ_TM)

    rq = proj(RQ_OFF, RK_OFF).astype(BF16)
    rk = proj(RK_OFF, RV_OFF).astype(BF16)
    rv = proj(RV_OFF, RG_OFF).astype(BF16)
    rg = proj(RG_OFF, MAIN_COLS)
    for blk in range(PROJ_TM // RET_C):
        rows = slice(blk * RET_C, (blk + 1) * RET_C)
        outs = _retention_block(rq[rows], rk[rows], rv[rows], rg[rows],
                                cd_ref, dec_ref, qdec_ref, kdec_ref, st_sc)
        for h in range(RET_HEADS):
            or_ref[rows, h * RET_V_DIM:(h + 1) * RET_V_DIM] = outs[h]


def _inproj(x2, g, w_bf, wq_t, wv_t, cd, dec, qdec, kdec):
    C = RET_C
    t_shape = jax.ShapeDtypeStruct((DA_HEADS, SEQ // PROJ_TM, LANES, PROJ_TM), BF16)
    t_spec = pl.BlockSpec((DA_HEADS, 1, LANES, PROJ_TM), lambda i: (0, i, 0, 0))
    w_t_spec = pl.BlockSpec((T_ROWS, D_MODEL), lambda i: (0, 0))
    return pl.pallas_call(
        _inproj_kernel,
        out_shape=(jax.ShapeDtypeStruct((SEQ, DA_WIDTH), BF16), t_shape, t_shape,
                   jax.ShapeDtypeStruct((SEQ, RET_WIDTH), BF16)),
        grid=(SEQ // PROJ_TM,),
        in_specs=[
            pl.BlockSpec(memory_space=pltpu.SMEM),
            pl.BlockSpec((PROJ_TM, D_MODEL), lambda i: (i, 0)),
            pl.BlockSpec((1, D_MODEL), lambda i: (0, 0)),
            pl.BlockSpec((D_MODEL, MAIN_COLS), lambda i: (0, 0)),
            w_t_spec, w_t_spec,
            pl.BlockSpec((RET_HEADS, C, C), lambda i: (0, 0, 0)),
            pl.BlockSpec((RET_HEADS, C, 1), lambda i: (0, 0, 0)),
            pl.BlockSpec((RET_HEADS, C, 1), lambda i: (0, 0, 0)),
        ],
        out_specs=(pl.BlockSpec((PROJ_TM, DA_WIDTH), lambda i: (i, 0)), t_spec, t_spec,
                   pl.BlockSpec((PROJ_TM, RET_WIDTH), lambda i: (i, 0))),
        scratch_shapes=[pltpu.VMEM((RET_HEADS, RET_QK_DIM, RET_V_DIM), F32)],
        compiler_params=_params(("arbitrary",)),
        name="inproj_retention",
    )(cd, x2, g, w_bf, wq_t, wv_t, dec, qdec, kdec)


ACC_ROWS = DA_V_DIM + 16


N_QT = SEQ // ATT_T
N_OFF = N_QT * (N_QT - 1) // 2


def _pipeline3(n_pos, scores, accumulate):
    scores(0, 0)
    scores(1, 1)
    steady = n_pos - 2

    def triple(k, carry):
        t = 3 * k
        accumulate(t, 0)
        scores(t + 2, 2)
        accumulate(t + 1, 1)
        scores(t + 3, 0)
        accumulate(t + 2, 2)
        scores(t + 4, 1)
        return carry

    lax.fori_loop(0, steady // 3, triple, 0)
    t0 = steady // 3 * 3
    rem = steady - t0
    accumulate(t0, 0)
    if rem >= 1:
        scores(t0 + 2, 2)
    accumulate(t0 + 1, 1)
    if rem == 2:
        scores(t0 + 3, 0)
    if rem >= 1:
        accumulate(t0 + 2, 2)
    if rem == 2:
        accumulate(t0 + 3, 0)


def _attn_kernel(slope_ref, jt_ref, it_ref, qt_ref, k_ref, vt_ref, qa_ref, ka1_ref, ka2_ref,
                 dtab_ref, lq1_ref, lk1_ref, lq2_ref, lk2_ref, g_ref, o_ref,
                 m_sc, acc_sc, s0_sc, s1_sc, s2_sc, mx0_sc, mx1_sc, mx2_sc):
    T = ATT_T
    h = pl.program_id(0)
    slope = slope_ref[h]
    qa = qa_ref[0]
    lane = lax.broadcasted_iota(I32, (T, LANES), 1)
    sums_row = (lax.broadcasted_iota(I32, (16, T), 0) == 0).astype(BF16)
    s_bufs = (s0_sc, s1_sc, s2_sc)
    mx_bufs = (mx0_sc, mx1_sc, mx2_sc)

    def scores(j, i, buf, extra):
        kt = k_ref[pl.ds(pl.multiple_of(j * T, T), T), :]
        ks = (jnp.where(lane < DA_HEAD_DIM, kt, ka1_ref[0]),
              jnp.where(lane >= DA_HEAD_DIM, kt, ka2_ref[0]))
        qt = qt_ref[0, i]
        qw = (jnp.concatenate([qt[0:DA_HEAD_DIM], qa], axis=0),
              jnp.concatenate([qa, qt[DA_HEAD_DIM:]], axis=0))
        for mp in range(2):
            s = jnp.dot(ks[mp], qw[mp], preferred_element_type=F32)
            if extra is not None:
                s = s + extra[0]
            s_bufs[buf][mp] = s
            mx_bufs[buf][mp] = jnp.max(s, axis=0, keepdims=True)

    def accumulate(j, i, buf):
        c = slope * lax.convert_element_type((i - j) * T, F32)
        vte = jnp.concatenate([vt_ref[0, j], sums_row], axis=0)
        for mp in range(2):
            m_prev = m_sc[i, mp]
            m_new = jnp.maximum(m_prev, mx_bufs[buf][mp] - c)
            p = jnp.exp(s_bufs[buf][mp] - (m_new + c)).astype(BF16)
            pv = jnp.dot(vte, p, preferred_element_type=F32)
            acc_sc[i, mp] = jnp.exp(m_prev - m_new) * acc_sc[i, mp] + pv
            m_sc[i, mp] = m_new

    m_sc[...] = jnp.full_like(m_sc, -jnp.inf)
    acc_sc[...] = jnp.zeros_like(acc_sc)
    _pipeline3(N_QT,
               lambda pos, buf: scores(pos, pos, buf, dtab_ref),
               lambda pos, buf: accumulate(pos, pos, buf))
    _pipeline3(N_OFF,
               lambda pos, buf: scores(jt_ref[pos], it_ref[pos], buf, None),
               lambda pos, buf: accumulate(jt_ref[pos], it_ref[pos], buf))

    lam = (jnp.exp(jnp.sum(lq1_ref[...] * lk1_ref[...], axis=1, keepdims=True))
           - jnp.exp(jnp.sum(lq2_ref[...] * lk2_ref[...], axis=1, keepdims=True))
           + LAMBDA_INIT)

    def finish(i, carry):
        a1 = acc_sc[i, 0]
        a2 = acc_sc[i, 1]
        ot = (a1[0:DA_V_DIM] / a1[DA_V_DIM:DA_V_DIM + 1]
              - lam * (a2[0:DA_V_DIM] / a2[DA_V_DIM:DA_V_DIM + 1]))
        o = ot.T
        var = jnp.mean(o * o, axis=-1, keepdims=True)
        o = (o * lax.rsqrt(var + EPS) * g_ref[...]) * (1.0 - LAMBDA_INIT)
        o_ref[pl.ds(pl.multiple_of(i * T, T), T), :] = o.astype(BF16)
        return carry

    lax.fori_loop(0, N_QT, finish, 0)


def _attention(proj, qt4, vt4, slopes, qa, ka1, ka2, dtab, lq1, lk1, lq2, lk2, subln_g):
    T = ATT_T
    vec64 = pl.BlockSpec((1, DA_HEAD_DIM), lambda h: (0, 0))
    per_head = lambda a, b: pl.BlockSpec((1, a, b), lambda h: (h, 0, 0))
    slab = lambda shape, imap: pl.BlockSpec(shape, imap, pipeline_mode=pl.Buffered(1))
    smem = pl.BlockSpec(memory_space=pltpu.SMEM)
    it_tab, jt_tab = np.tril_indices(N_QT, -1)
    return pl.pallas_call(
        _attn_kernel,
        out_shape=jax.ShapeDtypeStruct((SEQ, DA_WIDTH), BF16),
        grid=(DA_HEADS,),
        in_specs=[
            smem, smem, smem,
            slab((1, N_QT, LANES, T), lambda h: (h, 0, 0, 0)),
            slab((SEQ, LANES), lambda h: (0, DK_OFF // LANES + h)),
            slab((1, N_QT, LANES, T), lambda h: (h, 0, 0, 0)),
            per_head(DA_HEAD_DIM, T), per_head(T, LANES), per_head(T, LANES), per_head(T, T),
            vec64, vec64, vec64, vec64,
            pl.BlockSpec((1, DA_V_DIM), lambda h: (0, 0)),
        ],
        out_specs=slab((SEQ, LANES), lambda h: (0, h)),
        scratch_shapes=[
            pltpu.VMEM((N_QT, 2, 1, T), F32),
            pltpu.VMEM((N_QT, 2, ACC_ROWS, T), F32),
            pltpu.VMEM((2, T, T), F32),
            pltpu.VMEM((2, T, T), F32),
            pltpu.VMEM((2, T, T), F32),
            pltpu.VMEM((2, 1, T), F32),
            pltpu.VMEM((2, 1, T), F32),
            pltpu.VMEM((2, 1, T), F32),
        ],
        compiler_params=_params(("arbitrary",)),
        name="diff_attention",
    )(slopes, jnp.asarray(jt_tab, I32), jnp.asarray(it_tab, I32), qt4, proj, vt4, qa, ka1, ka2,
      dtab, lq1, lk1, lq2, lk2, subln_g)


def _outproj_router_kernel(x_ref, oda_ref, or_ref, wo_ref, g_ref, wr_ref, br_ref,
                           h1_ref, xn_ref, ri_ref, rw_ref):
    h1 = (x_ref[...]
          + jnp.dot(oda_ref[...], wo_ref[0:DA_WIDTH, :], preferred_element_type=F32)
          + jnp.dot(or_ref[...], wo_ref[DA_WIDTH:, :], preferred_element_type=F32))
    h1_ref[...] = h1
    var = jnp.mean(h1 * h1, axis=-1, keepdims=True)
    xn = h1 * lax.rsqrt(var + EPS) * g_ref[...]
    for c in range(ROW_TILE):
        xn_ref[pl.ds(c, PROJ_TM, stride=ROW_TILE), :] = xn[:, c * LANES:(c + 1) * LANES]
    x_hi = xn.astype(BF16)
    x_lo = (xn - x_hi.astype(F32)).astype(BF16)
    both = jnp.dot(x_hi, wr_ref[...], preferred_element_type=F32)
    logits = (both[:, :LANES] + both[:, LANES:]
              + jnp.dot(x_lo, wr_ref[:, :LANES], preferred_element_type=F32)) + br_ref[...]
    lane = lax.broadcasted_iota(I32, logits.shape, 1)
    neg = jnp.float32(-jnp.inf)
    big = jnp.int32(1 << 20)
    gl = jnp.where(lane < MOE_GROUPS, logits, neg)
    gmax = jnp.max(gl, axis=1, keepdims=True)
    gidx = jnp.min(jnp.where(gl == gmax, lane, big), axis=1, keepdims=True)
    gsum = jnp.sum(jnp.exp(gl - gmax), axis=1, keepdims=True)
    gp = 1.0 / gsum
    lo = MOE_GROUPS + gidx * MOE_EXPERTS_PER_GROUP
    el = jnp.where((lane >= lo) & (lane < lo + MOE_EXPERTS_PER_GROUP), logits, neg)
    v1 = jnp.max(el, axis=1, keepdims=True)
    i1 = jnp.min(jnp.where(el == v1, lane, big), axis=1, keepdims=True)
    el2 = jnp.where(lane == i1, neg, el)
    v2 = jnp.max(el2, axis=1, keepdims=True)
    i2 = jnp.min(jnp.where(el2 == v2, lane, big), axis=1, keepdims=True)
    t = jnp.exp(v2 - v1)
    w1 = gp / (1.0 + t)
    w2 = gp * t / (1.0 + t)
    ri_ref[...] = jnp.where(lane == 0, i1 - MOE_GROUPS,
                            jnp.where(lane == 1, i2 - MOE_GROUPS, 0))
    rw_ref[...] = jnp.where(lane == 0, w1, jnp.where(lane == 1, w2, 0.0))


def _outproj_router(x2, o_da, o_r, wo_bf, g, wr, br):
    tm = PROJ_TM
    row = lambda w: pl.BlockSpec((tm, w), lambda i: (i, 0))
    full = lambda a, b: pl.BlockSpec((a, b), lambda i: (0, 0))
    return pl.pallas_call(
        _outproj_router_kernel,
        out_shape=(
            jax.ShapeDtypeStruct((SEQ, D_MODEL), F32),
            jax.ShapeDtypeStruct((SEQ * ROW_TILE, LANES), F32),
            jax.ShapeDtypeStruct((SEQ, LANES), I32),
            jax.ShapeDtypeStruct((SEQ, LANES), F32),
        ),
        grid=(SEQ // tm,),
        in_specs=[row(D_MODEL), row(DA_WIDTH), row(RET_WIDTH), full(D_MODEL, D_MODEL),
                  full(1, D_MODEL), full(D_MODEL, 2 * LANES), full(1, LANES)],
        out_specs=(row(D_MODEL), pl.BlockSpec((tm * ROW_TILE, LANES), lambda i: (i, 0)),
                   row(LANES), row(LANES)),
        compiler_params=_params(("arbitrary",)),
        name="outproj_router",
    )(x2, o_da, o_r, wo_bf, g, wr, br)


def _plan_kernel(ri_ref, dest_ref, used_ref):
    TT = PLAN_T
    lane = lax.broadcasted_iota(I32, (TT, LANES), 1)

    def onehots(t):
        r = ri_ref[pl.ds(pl.multiple_of(t * TT, TT), TT), :]
        return lane == r[:, 0:1], lane == r[:, 1:2]

    def count_body(t, acc):
        oh1, oh2 = onehots(t)
        return acc + jnp.sum((oh1 | oh2).astype(F32), axis=0, keepdims=True)

    counts = lax.fori_loop(0, SEQ // TT, count_body, jnp.zeros((1, LANES), F32))
    counts8 = jnp.broadcast_to(counts, (8, LANES)).astype(I32)
    shift = FFN_B.bit_length() - 1
    padded = ((counts8 + (FFN_B - 1)) >> shift) << shift
    lane8 = lax.broadcasted_iota(I32, (8, LANES), 1)
    pad_end = padded
    sh = 1
    while sh < LANES:
        pad_end = pad_end + jnp.where(lane8 >= sh, pltpu.roll(pad_end, sh, axis=1), 0)
        sh *= 2
    pad_start = pad_end - padded

    ltri = (lax.broadcasted_iota(I32, (TT, TT), 0)
            > lax.broadcasted_iota(I32, (TT, TT), 1)).astype(BF16)

    def dest_body(t, carry):
        oh1, oh2 = onehots(t)
        a = (oh1 | oh2).astype(F32)
        base = jnp.dot(ltri, a.astype(BF16), preferred_element_type=F32) + carry
        d1 = jnp.sum(jnp.where(oh1, base, 0.0), axis=1, keepdims=True)
        d2 = jnp.sum(jnp.where(oh2, base, 0.0), axis=1, keepdims=True)
        dest_ref[pl.ds(pl.multiple_of(t * TT, TT), TT), :] = jnp.where(
            lane == 0, d1, jnp.where(lane == 1, d2, 0.0)).astype(I32)
        return carry + jnp.sum(a, axis=0, keepdims=True)

    lax.fori_loop(0, SEQ // TT, dest_body, pad_start[0:1].astype(F32))

    total = jnp.max(pad_end, axis=1, keepdims=True)
    row8 = lax.broadcasted_iota(I32, (8, LANES), 0)
    used_ref[...] = jnp.where(row8 == 0, jnp.broadcast_to(total >> shift, (8, LANES)),
                              jnp.where(row8 == 1, counts8, pad_start))


def _plan(ri):
    return pl.pallas_call(
        _plan_kernel,
        out_shape=(
            jax.ShapeDtypeStruct((SEQ, LANES), I32),
            jax.ShapeDtypeStruct((8, LANES), I32),
        ),
        compiler_params=pltpu.CompilerParams(vmem_limit_bytes=VMEM_LIMIT),
        name="route_plan",
    )(ri)


PAD_BITS = FFN_B.bit_length() - 1


def _pad_fill_copies(e, cnt_ref, pst_ref, zero_sc, xs_hbm, zsem):
    cnt = cnt_ref[e]
    pad = (-cnt) & (FFN_B - 1)
    row = pst_ref[e] + cnt
    out = []
    for bit in reversed(range(PAD_BITS)):
        n = 1 << bit
        start = row + ((pad >> (bit + 1)) << (bit + 1))
        copy = pltpu.make_async_copy(
            zero_sc.at[pl.ds(0, n * ROW_TILE)],
            xs_hbm.at[pl.ds(pl.multiple_of(start * ROW_TILE, ROW_TILE), n * ROW_TILE)], zsem)
        out.append(((pad & n) != 0, copy))
    return out


def _unused_block_copies(b, zero_sc, xs_hbm, zsem):
    half = FFN_B // 2 * ROW_TILE
    return [pltpu.make_async_copy(
        zero_sc, xs_hbm.at[pl.ds(pl.multiple_of((2 * b + k) * half, half), half)], zsem)
        for k in range(2)]


def _dispatch_kernel(dest_ref, cnt_ref, pst_ref, used_ref, xn_ref, xs_hbm, zero_sc, sem, zsem):
    tm = DISP_TM
    i = pl.program_id(0)

    @pl.when(i == 0)
    def _():
        zero_sc[...] = jnp.zeros_like(zero_sc)

        def fill(e, carry):
            for cond, copy in _pad_fill_copies(e, cnt_ref, pst_ref, zero_sc, xs_hbm, zsem):
                pl.when(cond)(copy.start)
            return carry

        lax.fori_loop(0, MOE_EXPERTS, fill, 0)

        def fill_block(b, carry):
            for copy in _unused_block_copies(b, zero_sc, xs_hbm, zsem):
                copy.start()
            return carry

        lax.fori_loop(used_ref[0], N_BLOCKS, fill_block, 0)

    def issue(it, carry):
        for u in range(DMA_UNROLL):
            r = it * DMA_UNROLL + u
            a = (i * tm + r) * 2
            src = xn_ref.at[pl.ds(pl.multiple_of(r * ROW_TILE, ROW_TILE), ROW_TILE)]
            for kk in range(2):
                d = pl.multiple_of(dest_ref[a + kk] * ROW_TILE, ROW_TILE)
                pltpu.make_async_copy(src, xs_hbm.at[pl.ds(d, ROW_TILE)], sem).start(priority=kk)
        return carry

    lax.fori_loop(0, tm // DMA_UNROLL, issue, 0)
    for _ in range(2):
        pltpu.make_async_copy(xn_ref, xs_hbm.at[pl.ds(0, tm * ROW_TILE)], sem).wait()

    @pl.when(i == 0)
    def _():
        def drain(e, carry):
            for cond, copy in _pad_fill_copies(e, cnt_ref, pst_ref, zero_sc, xs_hbm, zsem):
                pl.when(cond)(copy.wait)
            return carry

        lax.fori_loop(0, MOE_EXPERTS, drain, 0)

        def drain_block(b, carry):
            for copy in _unused_block_copies(b, zero_sc, xs_hbm, zsem):
                copy.wait()
            return carry

        lax.fori_loop(used_ref[0], N_BLOCKS, drain_block, 0)


def _dispatch(dest_flat, counts, pad_start, used, xn3):
    tm = DISP_TM
    return pl.pallas_call(
        _dispatch_kernel,
        out_shape=jax.ShapeDtypeStruct((N_BUF * ROW_TILE, LANES), F32),
        grid_spec=pltpu.PrefetchScalarGridSpec(
            num_scalar_prefetch=4,
            grid=(SEQ // tm,),
            in_specs=[pl.BlockSpec((tm * ROW_TILE, LANES), lambda i, d, c, p, u: (i, 0))],
            out_specs=pl.BlockSpec(memory_space=pl.ANY),
            scratch_shapes=[
                pltpu.VMEM((FFN_B // 2 * ROW_TILE, LANES), F32),
                pltpu.SemaphoreType.DMA(()),
                pltpu.SemaphoreType.DMA(()),
            ],
        ),
        compiler_params=_params(("arbitrary",)),
        name="moe_dispatch",
    )(dest_flat, counts, pad_start, used, xn3)


BLOCK_COPY_PRIORITY = 1


def _ffn_kernel(cnt_ref, pst_ref, used_ref, xs_hbm, wg_ref, wu_ref, wd_ref, y_hbm,
                xbuf, ybuf, zero_sc, wg_bf, wu_bf, wd_bf, sem_in, sem_out, zsem):
    B = FFN_B
    R = B * ROW_TILE
    e = pl.program_id(0)
    n = (cnt_ref[e] + (B - 1)) >> PAD_BITS
    s0 = pst_ref[e] >> PAD_BITS

    def rows(blk):
        return pl.ds(pl.multiple_of(blk * R, R), R)

    def fetch(blk, slot):
        return pltpu.make_async_copy(xs_hbm.at[rows(blk)], xbuf.at[slot], sem_in.at[slot])

    def flush(blk, slot):
        return pltpu.make_async_copy(ybuf.at[slot], y_hbm.at[rows(blk)], sem_out.at[slot])

    used = used_ref[0]

    @pl.when(e == 0)
    def _():
        fetch(0, 0).start(priority=BLOCK_COPY_PRIORITY)

    @pl.when(n > 0)
    def _():
        wg_bf[...] = wg_ref[0].astype(BF16)
        wu_bf[...] = wu_ref[0].astype(BF16)
        wd_bf[...] = wd_ref[0].astype(BF16)

    def body(j, carry):
        blk = s0 + j
        slot = blk % 2

        @pl.when(blk + 1 < used)
        def _():
            fetch(blk + 1, 1 - slot).start(priority=BLOCK_COPY_PRIORITY)

        fetch(blk, slot).wait()

        @pl.when(blk >= 2)
        def _():
            flush(blk - 2, slot).wait()

        x = jnp.concatenate([xbuf[slot, pl.ds(c, B, stride=ROW_TILE), :]
                             for c in range(ROW_TILE)], axis=1).astype(BF16)
        hg = jnp.dot(x, wg_bf[...], preferred_element_type=F32)
        hu = jnp.dot(x, wu_bf[...], preferred_element_type=F32)
        hh = ((hg / (1.0 + jnp.exp(-hg))) * hu).astype(BF16)
        y = jnp.dot(hh, wd_bf[...], preferred_element_type=F32)
        for c in range(ROW_TILE):
            ybuf[slot, pl.ds(c, B, stride=ROW_TILE), :] = y[:, c * LANES:(c + 1) * LANES]
        flush(blk, slot).start(priority=BLOCK_COPY_PRIORITY)
        return carry

    lax.fori_loop(0, n, body, 0)

    @pl.when(e == MOE_EXPERTS - 1)
    def _():
        flush(used - 1, (used + 1) % 2).wait()
        flush(used - 2, used % 2).wait()
        zero_sc[...] = jnp.zeros_like(zero_sc)

        def fill(b, carry):
            pltpu.make_async_copy(zero_sc, y_hbm.at[rows(b)], zsem).start()
            return carry

        def drain(b, carry):
            pltpu.make_async_copy(zero_sc, y_hbm.at[rows(b)], zsem).wait()
            return carry

        lax.fori_loop(used_ref[0], N_BLOCKS, fill, 0)
        lax.fori_loop(used_ref[0], N_BLOCKS, drain, 0)


def _ffn(counts, pad_start, used, xs, w_gate, w_up, w_down):
    B = FFN_B
    wspec = lambda a, c: pl.BlockSpec((1, a, c), lambda e, cnt, pst, used: (e, 0, 0))
    return pl.pallas_call(
        _ffn_kernel,
        out_shape=jax.ShapeDtypeStruct((N_BUF * ROW_TILE, LANES), F32),
        grid_spec=pltpu.PrefetchScalarGridSpec(
            num_scalar_prefetch=3,
            grid=(MOE_EXPERTS,),
            in_specs=[
                pl.BlockSpec(memory_space=pl.ANY),
                wspec(D_MODEL, MOE_HIDDEN),
                wspec(D_MODEL, MOE_HIDDEN),
                wspec(MOE_HIDDEN, D_MODEL),
            ],
            out_specs=pl.BlockSpec(memory_space=pl.ANY),
            scratch_shapes=[
                pltpu.VMEM((2, B * ROW_TILE, LANES), F32),
                pltpu.VMEM((2, B * ROW_TILE, LANES), F32),
                pltpu.VMEM((B * ROW_TILE, LANES), F32),
                pltpu.VMEM((D_MODEL, MOE_HIDDEN), BF16),
                pltpu.VMEM((D_MODEL, MOE_HIDDEN), BF16),
                pltpu.VMEM((MOE_HIDDEN, D_MODEL), BF16),
                pltpu.SemaphoreType.DMA((2,)),
                pltpu.SemaphoreType.DMA((2,)),
                pltpu.SemaphoreType.DMA(()),
            ],
        ),
        compiler_params=_params(("arbitrary",)),
        name="expert_ffn",
    )(counts, pad_start, used, xs, w_gate, w_up, w_down)


def _combine_kernel(dest_ref, h1_ref, rw_ref, g_ref, y_hbm, o_ref, ybuf, sem):
    tm = COMB_TM
    i = pl.program_id(0)

    def gather(tile, slot):
        def issue(it, carry):
            for u in range(DMA_UNROLL):
                r = it * DMA_UNROLL + u
                a = (tile * tm + r) * 2
                for kk in range(2):
                    d = pl.multiple_of(dest_ref[a + kk] * ROW_TILE, ROW_TILE)
                    pltpu.make_async_copy(
                        y_hbm.at[pl.ds(d, ROW_TILE)],
                        ybuf.at[slot, kk, pl.ds(pl.multiple_of(r * ROW_TILE, ROW_TILE), ROW_TILE)],
                        sem.at[slot, kk]).start(priority=kk)
            return carry

        lax.fori_loop(0, tm // DMA_UNROLL, issue, 0)

    @pl.when(i == 0)
    def _():
        gather(0, 0)

    @pl.when(i + 1 < pl.num_programs(0))
    def _():
        gather(i + 1, (i + 1) % 2)

    slot = i % 2
    for kk in range(2):
        pltpu.make_async_copy(y_hbm.at[pl.ds(0, tm * ROW_TILE)], ybuf.at[slot, kk],
                              sem.at[slot, kk]).wait()
    w = rw_ref[...]
    ys = [jnp.concatenate([ybuf[slot, kk, pl.ds(c, tm, stride=ROW_TILE), :]
                           for c in range(ROW_TILE)], axis=1) for kk in range(2)]
    h = h1_ref[...] + w[:, 0:1] * ys[0] + w[:, 1:2] * ys[1]
    var = jnp.mean(h * h, axis=-1, keepdims=True)
    o_ref[...] = h * lax.rsqrt(var + EPS) * g_ref[...]


def _combine(dest_flat, h1, rw, g, y):
    tm = COMB_TM
    return pl.pallas_call(
        _combine_kernel,
        out_shape=jax.ShapeDtypeStruct((SEQ, D_MODEL), F32),
        grid_spec=pltpu.PrefetchScalarGridSpec(
            num_scalar_prefetch=1,
            grid=(SEQ // tm,),
            in_specs=[
                pl.BlockSpec((tm, D_MODEL), lambda i, d: (i, 0)),
                pl.BlockSpec((tm, LANES), lambda i, d: (i, 0)),
                pl.BlockSpec((1, D_MODEL), lambda i, d: (0, 0)),
                pl.BlockSpec(memory_space=pl.ANY),
            ],
            out_specs=pl.BlockSpec((tm, D_MODEL), lambda i, d: (i, 0)),
            scratch_shapes=[
                pltpu.VMEM((2, 2, tm * ROW_TILE, LANES), F32),
                pltpu.SemaphoreType.DMA((2, 2)),
            ],
        ),
        compiler_params=_params(("arbitrary",)),
        name="moe_combine",
    )(dest_flat, h1, rw, g, y)


def _attention_tables():
    T = ATT_T
    slopes = jnp.exp2(-ALIBI_MAX * jnp.arange(1, DA_HEADS + 1, dtype=F32) / DA_HEADS)
    r = jnp.arange(T)
    hi = ((r // CHUNK) * CHUNK).astype(F32)
    lo = (r % CHUNK).astype(F32)
    ones = jnp.ones((T,), F32)
    sl = slopes[:, None]
    one_h = jnp.broadcast_to(ones, (DA_HEADS, T))
    q_rows = jnp.stack([one_h, one_h, -sl * hi[None], -sl * lo[None]], axis=1)
    k_cols = jnp.stack([sl * hi[None], sl * lo[None], one_h, one_h], axis=-1)
    qa = jnp.zeros((DA_HEADS, DA_HEAD_DIM, T), F32).at[:, 0:4, :].set(q_rows)
    ka1 = jnp.zeros((DA_HEADS, T, LANES), F32).at[:, :, DA_HEAD_DIM:DA_HEAD_DIM + 4].set(k_cols)
    ka2 = jnp.zeros((DA_HEADS, T, LANES), F32).at[:, :, 0:4].set(k_cols)
    rel = (r[:, None] - r[None, :]).astype(F32)
    allowed = (r[:, None] // CHUNK) <= (r[None, :] // CHUNK)
    fix = jnp.where(rel > 0, -2.0 * slopes[:, None, None] * rel[None], 0.0)
    dtab = jnp.where(allowed[None], fix, -jnp.inf)
    return slopes, qa.astype(BF16), ka1.astype(BF16), ka2.astype(BF16), dtab


def _retention_tables():
    C = RET_C
    log_gamma = jnp.log1p(-jnp.exp2(-5.0 - jnp.arange(RET_HEADS, dtype=F32)))
    pos = jnp.arange(C, dtype=F32)
    rel = pos[:, None] - pos[None, :]
    dec = jnp.where(rel >= 0, jnp.exp(log_gamma[:, None, None] * jnp.maximum(rel, 0.0)), 0.0)
    qdec = jnp.exp(log_gamma[:, None] * (pos + 1.0)[None, :])[:, :, None]
    kdec = jnp.exp(log_gamma[:, None] * (C - 1 - pos)[None, :])[:, :, None]
    cd = jnp.exp(log_gamma * C)
    return cd, dec, qdec, kdec


def kernel(x, attn_norm_g, w_in, da_lambda_q1, da_lambda_k1, da_lambda_q2, da_lambda_k2,
           da_subln_g, w_out, ffn_norm_g, router_group_w, router_group_b, router_expert_w,
           router_expert_b, expert_w_gate, expert_w_up, expert_w_down, final_norm_g):
    B, S, D = x.shape
    assert (B, S, D) == (1, SEQ, D_MODEL)
    x2 = x.reshape(S, D)

    w = w_in[0]
    w_main = jnp.concatenate([
        w[:, 512:1024],
        w[:, 1536:1792],
        w[:, 1792:2048] * (RET_QK_DIM ** -0.5),
        w[:, 2048:3072]], axis=1).astype(BF16)
    wq_t = (w[:, 0:512] * (DA_HEAD_DIM ** -0.5)).T.astype(BF16)
    wv_t = w[:, 1024:1536].T.astype(BF16)
    cd, dec, qdec, kdec = _retention_tables()
    k_da, qt4, vt4, o_r = _inproj(x2, attn_norm_g[0][None, :], w_main, wq_t, wv_t,
                                  cd, dec, qdec, kdec)

    slopes, qa, ka1, ka2, dtab = _attention_tables()
    o_da = _attention(k_da, qt4, vt4, slopes, qa, ka1, ka2, dtab, da_lambda_q1, da_lambda_k1,
                      da_lambda_q2, da_lambda_k2, da_subln_g)

    wr = jnp.zeros((D, LANES), F32)
    wr = wr.at[:, :MOE_GROUPS].set(router_group_w[0])
    wr = wr.at[:, MOE_GROUPS:MOE_GROUPS + MOE_EXPERTS].set(router_expert_w[0])
    br = jnp.zeros((1, LANES), F32)
    br = br.at[0, :MOE_GROUPS].set(router_group_b[0])
    br = br.at[0, MOE_GROUPS:MOE_GROUPS + MOE_EXPERTS].set(router_expert_b[0])
    wr_hi = wr.astype(BF16)
    wr_lo = (wr - wr_hi.astype(F32)).astype(BF16)
    h1, xn, ri, rw = _outproj_router(x2, o_da, o_r, w_out[0].astype(BF16), ffn_norm_g[0][None, :],
                                     jnp.concatenate([wr_hi, wr_lo], axis=1), br)

    dest, meta = _plan(ri)
    dest_flat = dest[:, :2].reshape(N_ASSIGN)
    used1 = meta[0, :1]
    counts = meta[1, :MOE_EXPERTS]
    pad_start = meta[2, :MOE_EXPERTS]
    xs = _dispatch(dest_flat, counts, pad_start, used1, xn)

    y = _ffn(counts, pad_start, used1, xs, expert_w_gate[0], expert_w_up[0], expert_w_down[0])
    out = _combine(dest_flat, h1, rw, final_norm_g[None, :], y)
    return out.reshape(B, S, D)
```

```python
import functools
import math

import jax
import jax.numpy as jnp
import numpy as np
from jax import lax
from jax.experimental import pallas as pl
from jax.experimental.pallas import tpu as pltpu

F32 = jnp.float32
BF16 = jnp.bfloat16
I32 = jnp.int32
U32 = jnp.uint32

D_MODEL = 1024
SEQ = 16384
CHUNK = 64
EPS = 1e-6

DA_HEADS = 4
DA_HEAD_DIM = 64
DA_V_DIM = 128
DA_WIDTH = 512
ALIBI_MAX = 8.0
RET_HEADS = 4
RET_QK_DIM = 64
RET_V_DIM = 128
RET_WIDTH = 512
T_ROWS = 512
MAIN_COLS = 2048
DK_OFF = 0
RQ_OFF = 512
RK_OFF = 768
RV_OFF = 1024
RG_OFF = 1536

MOE_GROUPS = 4
MOE_EXPERTS_PER_GROUP = 8
MOE_EXPERTS = 32
MOE_HIDDEN = 512
LAMBDA_INIT = 0.8 - 0.6 * math.exp(-0.3 * 0)

LANES = 128
ROW_TILE = 8
X_ROWS = 4
VMEM_LIMIT = 56 * 1024 * 1024

PROJ_TM = 512
ATT_T = 512
RET_C = 256
PLAN_T = 512
FFN_B = 512
N_ASSIGN = 2 * SEQ
N_BLOCKS = N_ASSIGN // FFN_B + MOE_EXPERTS
N_BUF = N_BLOCKS * FFN_B
COMB_TM = 256
DISP_TM = 512
DMA_UNROLL = 8


def _params(sem):
    return pltpu.CompilerParams(dimension_semantics=sem, vmem_limit_bytes=VMEM_LIMIT)


def _retention_block(q_all, k_all, v_all, g_all, cd_ref, dec_ref, qdec_ref, kdec_ref, st_sc):
    outs = []
    for h in range(RET_HEADS):
        qk = slice(h * RET_QK_DIM, (h + 1) * RET_QK_DIM)
        vv = slice(h * RET_V_DIM, (h + 1) * RET_V_DIM)
        q = q_all[:, qk]
        k = k_all[:, qk]
        v = v_all[:, vv]
        g = g_all[:, vv]
        s = lax.dot_general(q, k, (((1,), (1,)), ((), ())),
                            preferred_element_type=F32) * dec_ref[h]
        intra = jnp.dot(s.astype(BF16), v, preferred_element_type=F32)
        st = st_sc[h]
        cross = jnp.dot(q, st.astype(BF16), preferred_element_type=F32) * qdec_ref[h]
        kd = (k.astype(F32) * kdec_ref[h]).astype(BF16)
        st_sc[h] = st * cd_ref[h] + lax.dot_general(kd, v, (((0,), (0,)), ((), ())),
                                                    preferred_element_type=F32)
        o = intra + cross
        o = o * lax.rsqrt(jnp.mean(o * o, axis=-1, keepdims=True) + EPS)
        outs.append(((g / (1.0 + jnp.exp(-g))) * o).astype(BF16))
    return outs


def _inproj_kernel(cd_ref, x_ref, g_ref, w_ref, wq_ref, wv_ref, dec_ref, qdec_ref, kdec_ref,
                   k_ref, qt_ref, vt_ref, or_ref, st_sc):
    @pl.when(pl.program_id(0) == 0)
    def _():
        st_sc[...] = jnp.zeros_like(st_sc)

    x = x_ref[...]
    var = jnp.mean(x * x, axis=-1, keepdims=True)
    xn = (x * lax.rsqrt(var + EPS) * g_ref[...]).astype(BF16)

    def proj(lo, hi):
        return jnp.dot(xn, w_ref[:, lo:hi], preferred_element_type=F32)

    k_ref[...] = proj(DK_OFF, DK_OFF + DA_WIDTH).astype(BF16)
    nt = (((1,), (1,)), ((), ()))
    qt = lax.dot_general(wq_ref[...], xn, nt, preferred_element_type=F32)
    qt_ref[...] = qt.astype(BF16).reshape(DA_HEADS, 1, 2 * DA_HEAD_DIM, PROJ_TM)
    vt = lax.dot_general(wv_ref[...], xn, nt, preferred_element_type=F32)
    vt_ref[...] = vt.astype(BF16).reshape(DA_HEADS, 1, DA_V_DIM, PROJ_TM)

    rq = proj(RQ_OFF, RK_OFF).astype(BF16)
    rk = proj(RK_OFF, RV_OFF).astype(BF16)
    rv = proj(RV_OFF, RG_OFF).astype(BF16)
    rg = proj(RG_OFF, MAIN_COLS)
    for blk in range(PROJ_TM // RET_C):
        rows = slice(blk * RET_C, (blk + 1) * RET_C)
        outs = _retention_block(rq[rows], rk[rows], rv[rows], rg[rows],
                                cd_ref, dec_ref, qdec_ref, kdec_ref, st_sc)
        for h in range(RET_HEADS):
            or_ref[rows, h * RET_V_DIM:(h + 1) * RET_V_DIM] = outs[h]


def _inproj(x2, g, w_bf, wq_t, wv_t, cd, dec, qdec, kdec):
    C = RET_C
    t_shape = jax.ShapeDtypeStruct((DA_HEADS, SEQ // PROJ_TM, LANES, PROJ_TM), BF16)
    t_spec = pl.BlockSpec((DA_HEADS, 1, LANES, PROJ_TM), lambda i: (0, i, 0, 0))
    w_t_spec = pl.BlockSpec((T_ROWS, D_MODEL), lambda i: (0, 0))
    return pl.pallas_call(
        _inproj_kernel,
        out_shape=(jax.ShapeDtypeStruct((SEQ, DA_WIDTH), BF16), t_shape, t_shape,
                   jax.ShapeDtypeStruct((SEQ, RET_WIDTH), BF16)),
        grid=(SEQ // PROJ_TM,),
        in_specs=[
            pl.BlockSpec(memory_space=pltpu.SMEM),
            pl.BlockSpec((PROJ_TM, D_MODEL), lambda i: (i, 0)),
            pl.BlockSpec((1, D_MODEL), lambda i: (0, 0)),
            pl.BlockSpec((D_MODEL, MAIN_COLS), lambda i: (0, 0)),
            w_t_spec, w_t_spec,
            pl.BlockSpec((RET_HEADS, C, C), lambda i: (0, 0, 0)),
            pl.BlockSpec((RET_HEADS, C, 1), lambda i: (0, 0, 0)),
            pl.BlockSpec((RET_HEADS, C, 1), lambda i: (0, 0, 0)),
        ],
        out_specs=(pl.BlockSpec((PROJ_TM, DA_WIDTH), lambda i: (i, 0)), t_spec, t_spec,
                   pl.BlockSpec((PROJ_TM, RET_WIDTH), lambda i: (i, 0))),
        scratch_shapes=[pltpu.VMEM((RET_HEADS, RET_QK_DIM, RET_V_DIM), F32)],
        compiler_params=_params(("arbitrary",)),
        name="inproj_retention",
    )(cd, x2, g, w_bf, wq_t, wv_t, dec, qdec, kdec)


ACC_ROWS = DA_V_DIM + 16


N_QT = SEQ // ATT_T
N_OFF = N_QT * (N_QT - 1) // 2


def _pipeline3(n_pos, scores, accumulate):
    scores(0, 0)
    scores(1, 1)
    steady = n_pos - 2

    def triple(k, carry):
        t = 3 * k
        accumulate(t, 0)
        scores(t + 2, 2)
        accumulate(t + 1, 1)
        scores(t + 3, 0)
        accumulate(t + 2, 2)
        scores(t + 4, 1)
        return carry

    lax.fori_loop(0, steady // 3, triple, 0)
    t0 = steady // 3 * 3
    rem = steady - t0
    accumulate(t0, 0)
    if rem >= 1:
        scores(t0 + 2, 2)
    accumulate(t0 + 1, 1)
    if rem == 2:
        scores(t0 + 3, 0)
    if rem >= 1:
        accumulate(t0 + 2, 2)
    if rem == 2:
        accumulate(t0 + 3, 0)


def _attn_kernel(slope_ref, jt_ref, it_ref, qt_ref, k_ref, vt_ref, qa_ref, ka1_ref, ka2_ref,
                 dtab_ref, lq1_ref, lk1_ref, lq2_ref, lk2_ref, g_ref, o_ref,
                 m_sc, acc_sc, s0_sc, s1_sc, s2_sc, mx0_sc, mx1_sc, mx2_sc):
    T = ATT_T
    h = pl.program_id(0)
    slope = slope_ref[h]
    qa = qa_ref[0]
    lane = lax.broadcasted_iota(I32, (T, LANES), 1)
    sums_row = (lax.broadcasted_iota(I32, (16, T), 0) == 0).astype(BF16)
    s_bufs = (s0_sc, s1_sc, s2_sc)
    mx_bufs = (mx0_sc, mx1_sc, mx2_sc)

    def scores(j, i, buf, extra):
        kt = k_ref[pl.ds(pl.multiple_of(j * T, T), T), :]
        ks = (jnp.where(lane < DA_HEAD_DIM, kt, ka1_ref[0]),
              jnp.where(lane >= DA_HEAD_DIM, kt, ka2_ref[0]))
        qt = qt_ref[0, i]
        qw = (jnp.concatenate([qt[0:DA_HEAD_DIM], qa], axis=0),
              jnp.concatenate([qa, qt[DA_HEAD_DIM:]], axis=0))
        for mp in range(2):
            s = jnp.dot(ks[mp], qw[mp], preferred_element_type=F32)
            if extra is not None:
                s = s + extra[0]
            s_bufs[buf][mp] = s
            mx_bufs[buf][mp] = jnp.max(s, axis=0, keepdims=True)

    def accumulate(j, i, buf):
        c = slope * lax.convert_element_type((i - j) * T, F32)
        vte = jnp.concatenate([vt_ref[0, j], sums_row], axis=0)
        for mp in range(2):
            m_prev = m_sc[i, mp]
            m_new = jnp.maximum(m_prev, mx_bufs[buf][mp] - c)
            p = jnp.exp(s_bufs[buf][mp] - (m_new + c)).astype(BF16)
            pv = jnp.dot(vte, p, preferred_element_type=F32)
            acc_sc[i, mp] = jnp.exp(m_prev - m_new) * acc_sc[i, mp] + pv
            m_sc[i, mp] = m_new

    m_sc[...] = jnp.full_like(m_sc, -jnp.inf)
    acc_sc[...] = jnp.zeros_like(acc_sc)
    _pipeline3(N_QT,
               lambda pos, buf: scores(pos, pos, buf, dtab_ref),
               lambda pos, buf: accumulate(pos, pos, buf))
    _pipeline3(N_OFF,
               lambda pos, buf: scores(jt_ref[pos], it_ref[pos], buf, None),
               lambda pos, buf: accumulate(jt_ref[pos], it_ref[pos], buf))

    lam = (jnp.exp(jnp.sum(lq1_ref[...] * lk1_ref[...], axis=1, keepdims=True))
           - jnp.exp(jnp.sum(lq2_ref[...] * lk2_ref[...], axis=1, keepdims=True))
           + LAMBDA_INIT)

    def finish(i, carry):
        a1 = acc_sc[i, 0]
        a2 = acc_sc[i, 1]
        ot = (a1[0:DA_V_DIM] / a1[DA_V_DIM:DA_V_DIM + 1]
              - lam * (a2[0:DA_V_DIM] / a2[DA_V_DIM:DA_V_DIM + 1]))
        o = ot.T
        var = jnp.mean(o * o, axis=-1, keepdims=True)
        o = (o * lax.rsqrt(var + EPS) * g_ref[...]) * (1.0 - LAMBDA_INIT)
        o_ref[pl.ds(pl.multiple_of(i * T, T), T), :] = o.astype(BF16)
        return carry

    lax.fori_loop(0, N_QT, finish, 0)


def _attention(proj, qt4, vt4, slopes, qa, ka1, ka2, dtab, lq1, lk1, lq2, lk2, subln_g):
    T = ATT_T
    vec64 = pl.BlockSpec((1, DA_HEAD_DIM), lambda h: (0, 0))
    per_head = lambda a, b: pl.BlockSpec((1, a, b), lambda h: (h, 0, 0))
    slab = lambda shape, imap: pl.BlockSpec(shape, imap, pipeline_mode=pl.Buffered(1))
    smem = pl.BlockSpec(memory_space=pltpu.SMEM)
    it_tab, jt_tab = np.tril_indices(N_QT, -1)
    return pl.pallas_call(
        _attn_kernel,
        out_shape=jax.ShapeDtypeStruct((SEQ, DA_WIDTH), BF16),
        grid=(DA_HEADS,),
        in_specs=[
            smem, smem, smem,
            slab((1, N_QT, LANES, T), lambda h: (h, 0, 0, 0)),
            slab((SEQ, LANES), lambda h: (0, DK_OFF // LANES + h)),
            slab((1, N_QT, LANES, T), lambda h: (h, 0, 0, 0)),
            per_head(DA_HEAD_DIM, T), per_head(T, LANES), per_head(T, LANES), per_head(T, T),
            vec64, vec64, vec64, vec64,
            pl.BlockSpec((1, DA_V_DIM), lambda h: (0, 0)),
        ],
        out_specs=slab((SEQ, LANES), lambda h: (0, h)),
        scratch_shapes=[
            pltpu.VMEM((N_QT, 2, 1, T), F32),
            pltpu.VMEM((N_QT, 2, ACC_ROWS, T), F32),
            pltpu.VMEM((2, T, T), F32),
            pltpu.VMEM((2, T, T), F32),
            pltpu.VMEM((2, T, T), F32),
            pltpu.VMEM((2, 1, T), F32),
            pltpu.VMEM((2, 1, T), F32),
            pltpu.VMEM((2, 1, T), F32),
        ],
        compiler_params=_params(("arbitrary",)),
        name="diff_attention",
    )(slopes, jnp.asarray(jt_tab, I32), jnp.asarray(it_tab, I32), qt4, proj, vt4, qa, ka1, ka2,
      dtab, lq1, lk1, lq2, lk2, subln_g)


def _outproj_router_kernel(x_ref, oda_ref, or_ref, wo_ref, g_ref, wr_ref, br_ref,
                           h1_ref, xn_ref, ri_ref, rw_ref):
    h1 = (x_ref[...]
          + jnp.dot(oda_ref[...], wo_ref[0:DA_WIDTH, :], preferred_element_type=F32)
          + jnp.dot(or_ref[...], wo_ref[DA_WIDTH:, :], preferred_element_type=F32))
    h1_ref[...] = h1
    var = jnp.mean(h1 * h1, axis=-1, keepdims=True)
    xn = h1 * lax.rsqrt(var + EPS) * g_ref[...]
    bits = lax.bitcast_convert_type(xn.astype(BF16).astype(F32), U32)
    for c in range(X_ROWS):
        lo = bits[:, c * LANES:(c + 1) * LANES] >> 16
        hi = bits[:, (c + X_ROWS) * LANES:(c + X_ROWS + 1) * LANES] & jnp.uint32(0xFFFF0000)
        xn_ref[pl.ds(c, PROJ_TM, stride=X_ROWS), :] = lo | hi
    x_hi = xn.astype(BF16)
    x_lo = (xn - x_hi.astype(F32)).astype(BF16)
    both = jnp.dot(x_hi, wr_ref[...], preferred_element_type=F32)
    logits = (both[:, :LANES] + both[:, LANES:]
              + jnp.dot(x_lo, wr_ref[:, :LANES], preferred_element_type=F32)) + br_ref[...]
    lane = lax.broadcasted_iota(I32, logits.shape, 1)
    neg = jnp.float32(-jnp.inf)
    big = jnp.int32(1 << 20)
    gl = jnp.where(lane < MOE_GROUPS, logits, neg)
    gmax = jnp.max(gl, axis=1, keepdims=True)
    gidx = jnp.min(jnp.where(gl == gmax, lane, big), axis=1, keepdims=True)
    gsum = jnp.sum(jnp.exp(gl - gmax), axis=1, keepdims=True)
    gp = 1.0 / gsum
    lo = MOE_GROUPS + gidx * MOE_EXPERTS_PER_GROUP
    el = jnp.where((lane >= lo) & (lane < lo + MOE_EXPERTS_PER_GROUP), logits, neg)
    v1 = jnp.max(el, axis=1, keepdims=True)
    i1 = jnp.min(jnp.where(el == v1, lane, big), axis=1, keepdims=True)
    el2 = jnp.where(lane == i1, neg, el)
    v2 = jnp.max(el2, axis=1, keepdims=True)
    i2 = jnp.min(jnp.where(el2 == v2, lane, big), axis=1, keepdims=True)
    t = jnp.exp(v2 - v1)
    w1 = gp / (1.0 + t)
    w2 = gp * t / (1.0 + t)
    ri_ref[...] = jnp.where(lane == 0, i1 - MOE_GROUPS,
                            jnp.where(lane == 1, i2 - MOE_GROUPS, 0))
    rw_ref[...] = jnp.where(lane == 0, w1, jnp.where(lane == 1, w2, 0.0))


def _outproj_router(x2, o_da, o_r, wo_bf, g, wr, br):
    tm = PROJ_TM
    row = lambda w: pl.BlockSpec((tm, w), lambda i: (i, 0))
    full = lambda a, b: pl.BlockSpec((a, b), lambda i: (0, 0))
    return pl.pallas_call(
        _outproj_router_kernel,
        out_shape=(
            jax.ShapeDtypeStruct((SEQ, D_MODEL), F32),
            jax.ShapeDtypeStruct((SEQ * X_ROWS, LANES), U32),
            jax.ShapeDtypeStruct((SEQ, LANES), I32),
            jax.ShapeDtypeStruct((SEQ, LANES), F32),
        ),
        grid=(SEQ // tm,),
        in_specs=[row(D_MODEL), row(DA_WIDTH), row(RET_WIDTH), full(D_MODEL, D_MODEL),
                  full(1, D_MODEL), full(D_MODEL, 2 * LANES), full(1, LANES)],
        out_specs=(row(D_MODEL), pl.BlockSpec((tm * X_ROWS, LANES), lambda i: (i, 0)),
                   row(LANES), row(LANES)),
        compiler_params=_params(("arbitrary",)),
        name="outproj_router",
    )(x2, o_da, o_r, wo_bf, g, wr, br)


def _plan_kernel(ri_ref, dest_ref, used_ref):
    TT = PLAN_T
    lane = lax.broadcasted_iota(I32, (TT, LANES), 1)

    def onehots(t):
        r = ri_ref[pl.ds(pl.multiple_of(t * TT, TT), TT), :]
        return lane == r[:, 0:1], lane == r[:, 1:2]

    def count_body(t, acc):
        oh1, oh2 = onehots(t)
        return acc + jnp.sum((oh1 | oh2).astype(F32), axis=0, keepdims=True)

    counts = lax.fori_loop(0, SEQ // TT, count_body, jnp.zeros((1, LANES), F32))
    counts8 = jnp.broadcast_to(counts, (8, LANES)).astype(I32)
    shift = FFN_B.bit_length() - 1
    padded = ((counts8 + (FFN_B - 1)) >> shift) << shift
    lane8 = lax.broadcasted_iota(I32, (8, LANES), 1)
    pad_end = padded
    sh = 1
    while sh < LANES:
        pad_end = pad_end + jnp.where(lane8 >= sh, pltpu.roll(pad_end, sh, axis=1), 0)
        sh *= 2
    pad_start = pad_end - padded

    ltri = (lax.broadcasted_iota(I32, (TT, TT), 0)
            > lax.broadcasted_iota(I32, (TT, TT), 1)).astype(BF16)

    def dest_body(t, carry):
        oh1, oh2 = onehots(t)
        a = (oh1 | oh2).astype(F32)
        base = jnp.dot(ltri, a.astype(BF16), preferred_element_type=F32) + carry
        d1 = jnp.sum(jnp.where(oh1, base, 0.0), axis=1, keepdims=True)
        d2 = jnp.sum(jnp.where(oh2, base, 0.0), axis=1, keepdims=True)
        dest_ref[pl.ds(pl.multiple_of(t * TT, TT), TT), :] = jnp.where(
            lane == 0, d1, jnp.where(lane == 1, d2, 0.0)).astype(I32)
        return carry + jnp.sum(a, axis=0, keepdims=True)

    lax.fori_loop(0, SEQ // TT, dest_body, pad_start[0:1].astype(F32))

    total = jnp.max(pad_end, axis=1, keepdims=True)
    row8 = lax.broadcasted_iota(I32, (8, LANES), 0)
    used_ref[...] = jnp.where(row8 == 0, jnp.broadcast_to(total >> shift, (8, LANES)),
                              jnp.where(row8 == 1, counts8, pad_start))


def _plan(ri):
    return pl.pallas_call(
        _plan_kernel,
        out_shape=(
            jax.ShapeDtypeStruct((SEQ, LANES), I32),
            jax.ShapeDtypeStruct((8, LANES), I32),
        ),
        compiler_params=pltpu.CompilerParams(vmem_limit_bytes=VMEM_LIMIT),
        name="route_plan",
    )(ri)


PAD_BITS = FFN_B.bit_length() - 1


def _pad_fill_copies(e, cnt_ref, pst_ref, zero_sc, xs_hbm, zsem):
    cnt = cnt_ref[e]
    pad = (-cnt) & (FFN_B - 1)
    row = pst_ref[e] + cnt
    out = []
    for bit in reversed(range(PAD_BITS)):
        n = 1 << bit
        start = row + ((pad >> (bit + 1)) << (bit + 1))
        copy = pltpu.make_async_copy(
            zero_sc.at[pl.ds(0, n * X_ROWS)],
            xs_hbm.at[pl.ds(pl.multiple_of(start * X_ROWS, X_ROWS), n * X_ROWS)], zsem)
        out.append(((pad & n) != 0, copy))
    return out


def _unused_block_copies(b, zero_sc, xs_hbm, zsem):
    half = FFN_B // 2 * X_ROWS
    return [pltpu.make_async_copy(
        zero_sc, xs_hbm.at[pl.ds(pl.multiple_of((2 * b + k) * half, half), half)], zsem)
        for k in range(2)]


def _dispatch_kernel(dest_ref, cnt_ref, pst_ref, used_ref, xn_ref, xs_hbm, zero_sc, sem, zsem):
    tm = DISP_TM
    i = pl.program_id(0)

    @pl.when(i == 0)
    def _():
        zero_sc[...] = jnp.zeros_like(zero_sc)

        def fill(e, carry):
            for cond, copy in _pad_fill_copies(e, cnt_ref, pst_ref, zero_sc, xs_hbm, zsem):
                pl.when(cond)(copy.start)
            return carry

        lax.fori_loop(0, MOE_EXPERTS, fill, 0)

        def fill_block(b, carry):
            for copy in _unused_block_copies(b, zero_sc, xs_hbm, zsem):
                copy.start()
            return carry

        lax.fori_loop(used_ref[0], N_BLOCKS, fill_block, 0)

    def issue(it, carry):
        for u in range(DMA_UNROLL):
            r = it * DMA_UNROLL + u
            a = (i * tm + r) * 2
            src = xn_ref.at[pl.ds(pl.multiple_of(r * X_ROWS, X_ROWS), X_ROWS)]
            for kk in range(2):
                d = pl.multiple_of(dest_ref[a + kk] * X_ROWS, X_ROWS)
                pltpu.make_async_copy(src, xs_hbm.at[pl.ds(d, X_ROWS)], sem).start(priority=kk)
        return carry

    lax.fori_loop(0, tm // DMA_UNROLL, issue, 0)
    for _ in range(2):
        pltpu.make_async_copy(xn_ref, xs_hbm.at[pl.ds(0, tm * X_ROWS)], sem).wait()

    @pl.when(i == 0)
    def _():
        def drain(e, carry):
            for cond, copy in _pad_fill_copies(e, cnt_ref, pst_ref, zero_sc, xs_hbm, zsem):
                pl.when(cond)(copy.wait)
            return carry

        lax.fori_loop(0, MOE_EXPERTS, drain, 0)

        def drain_block(b, carry):
            for copy in _unused_block_copies(b, zero_sc, xs_hbm, zsem):
                copy.wait()
            return carry

        lax.fori_loop(used_ref[0], N_BLOCKS, drain_block, 0)


def _dispatch(dest_flat, counts, pad_start, used, xn3):
    tm = DISP_TM
    return pl.pallas_call(
        _dispatch_kernel,
        out_shape=jax.ShapeDtypeStruct((N_BUF * X_ROWS, LANES), U32),
        grid_spec=pltpu.PrefetchScalarGridSpec(
            num_scalar_prefetch=4,
            grid=(SEQ // tm,),
            in_specs=[pl.BlockSpec((tm * X_ROWS, LANES), lambda i, d, c, p, u: (i, 0))],
            out_specs=pl.BlockSpec(memory_space=pl.ANY),
            scratch_shapes=[
                pltpu.VMEM((FFN_B // 2 * X_ROWS, LANES), U32),
                pltpu.SemaphoreType.DMA(()),
                pltpu.SemaphoreType.DMA(()),
            ],
        ),
        compiler_params=_params(("arbitrary",)),
        name="moe_dispatch",
    )(dest_flat, counts, pad_start, used, xn3)


BLOCK_COPY_PRIORITY = 1


def _ffn_kernel(cnt_ref, pst_ref, used_ref, xs_hbm, wg_ref, wu_ref, wd_ref, y_hbm,
                xbuf, ybuf, zero_sc, wg_bf, wu_bf, wd_bf, sem_in, sem_out, zsem):
    B = FFN_B
    e = pl.program_id(0)
    n = (cnt_ref[e] + (B - 1)) >> PAD_BITS
    s0 = pst_ref[e] >> PAD_BITS

    def rows(blk, per_token):
        size = B * per_token
        return pl.ds(pl.multiple_of(blk * size, size), size)

    def fetch(blk, slot):
        return pltpu.make_async_copy(xs_hbm.at[rows(blk, X_ROWS)], xbuf.at[slot],
                                     sem_in.at[slot])

    def flush(blk, slot):
        return pltpu.make_async_copy(ybuf.at[slot], y_hbm.at[rows(blk, ROW_TILE)],
                                     sem_out.at[slot])

    used = used_ref[0]

    @pl.when(e == 0)
    def _():
        fetch(0, 0).start(priority=BLOCK_COPY_PRIORITY)

    @pl.when(n > 0)
    def _():
        wg_bf[...] = wg_ref[0].astype(BF16)
        wu_bf[...] = wu_ref[0].astype(BF16)
        wd_bf[...] = wd_ref[0].astype(BF16)

    def body(j, carry):
        blk = s0 + j
        slot = blk % 2

        @pl.when(blk + 1 < used)
        def _():
            fetch(blk + 1, 1 - slot).start(priority=BLOCK_COPY_PRIORITY)

        fetch(blk, slot).wait()

        @pl.when(blk >= 2)
        def _():
            flush(blk - 2, slot).wait()

        words = [xbuf[slot, pl.ds(c, B, stride=X_ROWS), :] for c in range(X_ROWS)]
        lo = [lax.bitcast_convert_type(w << 16, F32) for w in words]
        hi = [lax.bitcast_convert_type(w & jnp.uint32(0xFFFF0000), F32) for w in words]
        x = jnp.concatenate(lo + hi, axis=1).astype(BF16)
        hg = jnp.dot(x, wg_bf[...], preferred_element_type=F32)
        hu = jnp.dot(x, wu_bf[...], preferred_element_type=F32)
        hh = ((hg / (1.0 + jnp.exp(-hg))) * hu).astype(BF16)
        y = jnp.dot(hh, wd_bf[...], preferred_element_type=F32)
        for c in range(ROW_TILE):
            ybuf[slot, pl.ds(c, B, stride=ROW_TILE), :] = y[:, c * LANES:(c + 1) * LANES]
        flush(blk, slot).start(priority=BLOCK_COPY_PRIORITY)
        return carry

    lax.fori_loop(0, n, body, 0)

    @pl.when(e == MOE_EXPERTS - 1)
    def _():
        flush(used - 1, (used + 1) % 2).wait()
        flush(used - 2, used % 2).wait()
        zero_sc[...] = jnp.zeros_like(zero_sc)

        def fill(b, carry):
            pltpu.make_async_copy(zero_sc, y_hbm.at[rows(b, ROW_TILE)], zsem).start()
            return carry

        def drain(b, carry):
            pltpu.make_async_copy(zero_sc, y_hbm.at[rows(b, ROW_TILE)], zsem).wait()
            return carry

        lax.fori_loop(used_ref[0], N_BLOCKS, fill, 0)
        lax.fori_loop(used_ref[0], N_BLOCKS, drain, 0)


def _ffn(counts, pad_start, used, xs, w_gate, w_up, w_down):
    B = FFN_B
    wspec = lambda a, c: pl.BlockSpec((1, a, c), lambda e, cnt, pst, used: (e, 0, 0))
    return pl.pallas_call(
        _ffn_kernel,
        out_shape=jax.ShapeDtypeStruct((N_BUF * ROW_TILE, LANES), F32),
        grid_spec=pltpu.PrefetchScalarGridSpec(
            num_scalar_prefetch=3,
            grid=(MOE_EXPERTS,),
            in_specs=[
                pl.BlockSpec(memory_space=pl.ANY),
                wspec(D_MODEL, MOE_HIDDEN),
                wspec(D_MODEL, MOE_HIDDEN),
                wspec(MOE_HIDDEN, D_MODEL),
            ],
            out_specs=pl.BlockSpec(memory_space=pl.ANY),
            scratch_shapes=[
                pltpu.VMEM((2, B * X_ROWS, LANES), U32),
                pltpu.VMEM((2, B * ROW_TILE, LANES), F32),
                pltpu.VMEM((B * ROW_TILE, LANES), F32),
                pltpu.VMEM((D_MODEL, MOE_HIDDEN), BF16),
                pltpu.VMEM((D_MODEL, MOE_HIDDEN), BF16),
                pltpu.VMEM((MOE_HIDDEN, D_MODEL), BF16),
                pltpu.SemaphoreType.DMA((2,)),
                pltpu.SemaphoreType.DMA((2,)),
                pltpu.SemaphoreType.DMA(()),
            ],
        ),
        compiler_params=_params(("arbitrary",)),
        name="expert_ffn",
    )(counts, pad_start, used, xs, w_gate, w_up, w_down)


def _combine_kernel(dest_ref, h1_ref, rw_ref, g_ref, y_hbm, o_ref, ybuf, sem):
    tm = COMB_TM
    i = pl.program_id(0)

    def gather(tile, slot):
        def issue(it, carry):
            for u in range(DMA_UNROLL):
                r = it * DMA_UNROLL + u
                a = (tile * tm + r) * 2
                for kk in range(2):
                    d = pl.multiple_of(dest_ref[a + kk] * ROW_TILE, ROW_TILE)
                    pltpu.make_async_copy(
                        y_hbm.at[pl.ds(d, ROW_TILE)],
                        ybuf.at[slot, kk, pl.ds(pl.multiple_of(r * ROW_TILE, ROW_TILE), ROW_TILE)],
                        sem.at[slot, kk]).start(priority=kk)
            return carry

        lax.fori_loop(0, tm // DMA_UNROLL, issue, 0)

    @pl.when(i == 0)
    def _():
        gather(0, 0)

    @pl.when(i + 1 < pl.num_programs(0))
    def _():
        gather(i + 1, (i + 1) % 2)

    slot = i % 2
    for kk in range(2):
        pltpu.make_async_copy(y_hbm.at[pl.ds(0, tm * ROW_TILE)], ybuf.at[slot, kk],
                              sem.at[slot, kk]).wait()
    w = rw_ref[...]
    ys = [jnp.concatenate([ybuf[slot, kk, pl.ds(c, tm, stride=ROW_TILE), :]
                           for c in range(ROW_TILE)], axis=1) for kk in range(2)]
    h = h1_ref[...] + w[:, 0:1] * ys[0] + w[:, 1:2] * ys[1]
    var = jnp.mean(h * h, axis=-1, keepdims=True)
    o_ref[...] = h * lax.rsqrt(var + EPS) * g_ref[...]


def _combine(dest_flat, h1, rw, g, y):
    tm = COMB_TM
    return pl.pallas_call(
        _combine_kernel,
        out_shape=jax.ShapeDtypeStruct((SEQ, D_MODEL), F32),
        grid_spec=pltpu.PrefetchScalarGridSpec(
            num_scalar_prefetch=1,
            grid=(SEQ // tm,),
            in_specs=[
                pl.BlockSpec((tm, D_MODEL), lambda i, d: (i, 0)),
                pl.BlockSpec((tm, LANES), lambda i, d: (i, 0)),
                pl.BlockSpec((1, D_MODEL), lambda i, d: (0, 0)),
                pl.BlockSpec(memory_space=pl.ANY),
            ],
            out_specs=pl.BlockSpec((tm, D_MODEL), lambda i, d: (i, 0)),
            scratch_shapes=[
                pltpu.VMEM((2, 2, tm * ROW_TILE, LANES), F32),
                pltpu.SemaphoreType.DMA((2, 2)),
            ],
        ),
        compiler_params=_params(("arbitrary",)),
        name="moe_combine",
    )(dest_flat, h1, rw, g, y)


def _attention_tables():
    T = ATT_T
    slopes = jnp.exp2(-ALIBI_MAX * jnp.arange(1, DA_HEADS + 1, dtype=F32) / DA_HEADS)
    r = jnp.arange(T)
    hi = ((r // CHUNK) * CHUNK).astype(F32)
    lo = (r % CHUNK).astype(F32)
    ones = jnp.ones((T,), F32)
    sl = slopes[:, None]
    one_h = jnp.broadcast_to(ones, (DA_HEADS, T))
    q_rows = jnp.stack([one_h, one_h, -sl * hi[None], -sl * lo[None]], axis=1)
    k_cols = jnp.stack([sl * hi[None], sl * lo[None], one_h, one_h], axis=-1)
    qa = jnp.zeros((DA_HEADS, DA_HEAD_DIM, T), F32).at[:, 0:4, :].set(q_rows)
    ka1 = jnp.zeros((DA_HEADS, T, LANES), F32).at[:, :, DA_HEAD_DIM:DA_HEAD_DIM + 4].set(k_cols)
    ka2 = jnp.zeros((DA_HEADS, T, LANES), F32).at[:, :, 0:4].set(k_cols)
    rel = (r[:, None] - r[None, :]).astype(F32)
    allowed = (r[:, None] // CHUNK) <= (r[None, :] // CHUNK)
    fix = jnp.where(rel > 0, -2.0 * slopes[:, None, None] * rel[None], 0.0)
    dtab = jnp.where(allowed[None], fix, -jnp.inf)
    return slopes, qa.astype(BF16), ka1.astype(BF16), ka2.astype(BF16), dtab


def _retention_tables():
    C = RET_C
    log_gamma = jnp.log1p(-jnp.exp2(-5.0 - jnp.arange(RET_HEADS, dtype=F32)))
    pos = jnp.arange(C, dtype=F32)
    rel = pos[:, None] - pos[None, :]
    dec = jnp.where(rel >= 0, jnp.exp(log_gamma[:, None, None] * jnp.maximum(rel, 0.0)), 0.0)
    qdec = jnp.exp(log_gamma[:, None] * (pos + 1.0)[None, :])[:, :, None]
    kdec = jnp.exp(log_gamma[:, None] * (C - 1 - pos)[None, :])[:, :, None]
    cd = jnp.exp(log_gamma * C)
    return cd, dec, qdec, kdec


def kernel(x, attn_norm_g, w_in, da_lambda_q1, da_lambda_k1, da_lambda_q2, da_lambda_k2,
           da_subln_g, w_out, ffn_norm_g, router_group_w, router_group_b, router_expert_w,
           router_expert_b, expert_w_gate, expert_w_up, expert_w_down, final_norm_g):
    B, S, D = x.shape
    assert (B, S, D) == (1, SEQ, D_MODEL)
    x2 = x.reshape(S, D)

    w = w_in[0]
    w_main = jnp.concatenate([
        w[:, 512:1024],
        w[:, 1536:1792],
        w[:, 1792:2048] * (RET_QK_DIM ** -0.5),
        w[:, 2048:3072]], axis=1).astype(BF16)
    wq_t = (w[:, 0:512] * (DA_HEAD_DIM ** -0.5)).T.astype(BF16)
    wv_t = w[:, 1024:1536].T.astype(BF16)
    cd, dec, qdec, kdec = _retention_tables()
    k_da, qt4, vt4, o_r = _inproj(x2, attn_norm_g[0][None, :], w_main, wq_t, wv_t,
                                  cd, dec, qdec, kdec)

    slopes, qa, ka1, ka2, dtab = _attention_tables()
    o_da = _attention(k_da, qt4, vt4, slopes, qa, ka1, ka2, dtab, da_lambda_q1, da_lambda_k1,
                      da_lambda_q2, da_lambda_k2, da_subln_g)

    wr = jnp.zeros((D, LANES), F32)
    wr = wr.at[:, :MOE_GROUPS].set(router_group_w[0])
    wr = wr.at[:, MOE_GROUPS:MOE_GROUPS + MOE_EXPERTS].set(router_expert_w[0])
    br = jnp.zeros((1, LANES), F32)
    br = br.at[0, :MOE_GROUPS].set(router_group_b[0])
    br = br.at[0, MOE_GROUPS:MOE_GROUPS + MOE_EXPERTS].set(router_expert_b[0])
    wr_hi = wr.astype(BF16)
    wr_lo = (wr - wr_hi.astype(F32)).astype(BF16)
    h1, xn, ri, rw = _outproj_router(x2, o_da, o_r, w_out[0].astype(BF16), ffn_norm_g[0][None, :],
                                     jnp.concatenate([wr_hi, wr_lo], axis=1), br)

    dest, meta = _plan(ri)
    dest_flat = dest[:, :2].reshape(N_ASSIGN)
    used1 = meta[0, :1]
    counts = meta[1, :MOE_EXPERTS]
    pad_start = meta[2, :MOE_EXPERTS]
    xs = _dispatch(dest_flat, counts, pad_start, used1, xn)

    y = _ffn(counts, pad_start, used1, xs, expert_w_gate[0], expert_w_up[0], expert_w_down[0])
    out = _combine(dest_flat, h1, rw, final_norm_g[None, :], y)
    return out.reshape(B, S, D)
```

```python
import functools
import math

import jax
import jax.numpy as jnp
import numpy as np
from jax import lax
from jax.experimental import pallas as pl
from jax.experimental.pallas import tpu as pltpu

F32 = jnp.float32
BF16 = jnp.bfloat16
I32 = jnp.int32
U32 = jnp.uint32

D_MODEL = 1024
SEQ = 16384
CHUNK = 64
EPS = 1e-6

DA_HEADS = 4
DA_HEAD_DIM = 64
DA_V_DIM = 128
DA_WIDTH = 512
ALIBI_MAX = 8.0
RET_HEADS = 4
RET_QK_DIM = 64
RET_V_DIM = 128
RET_WIDTH = 512
T_ROWS = 512
MAIN_COLS = 2048
DK_OFF = 0
RQ_OFF = 512
RK_OFF = 768
RV_OFF = 1024
RG_OFF = 1536

MOE_GROUPS = 4
MOE_EXPERTS_PER_GROUP = 8
MOE_EXPERTS = 32
MOE_HIDDEN = 512
LAMBDA_INIT = 0.8 - 0.6 * math.exp(-0.3 * 0)

LANES = 128
ROW_TILE = 8
X_ROWS = 4
VMEM_LIMIT = 56 * 1024 * 1024

PROJ_TM = 512
ATT_T = 512
RET_C = 256
PLAN_T = 512
FFN_B = 512
N_ASSIGN = 2 * SEQ
N_BLOCKS = N_ASSIGN // FFN_B + MOE_EXPERTS
N_BUF = N_BLOCKS * FFN_B
COMB_TM = 256
DISP_TM = 512
DMA_UNROLL = 8


def _params(sem):
    return pltpu.CompilerParams(dimension_semantics=sem, vmem_limit_bytes=VMEM_LIMIT)


def _retention_block(q_all, k_all, v_all, g_all, cd_ref, dec_ref, qdec_ref, kdec_ref, st_sc):
    outs = []
    for h in range(RET_HEADS):
        qk = slice(h * RET_QK_DIM, (h + 1) * RET_QK_DIM)
        vv = slice(h * RET_V_DIM, (h + 1) * RET_V_DIM)
        q = q_all[:, qk]
        k = k_all[:, qk]
        v = v_all[:, vv]
        g = g_all[:, vv]
        s = lax.dot_general(q, k, (((1,), (1,)), ((), ())),
                            preferred_element_type=F32) * dec_ref[h]
        intra = jnp.dot(s.astype(BF16), v, preferred_element_type=F32)
        st = st_sc[h]
        cross = jnp.dot(q, st.astype(BF16), preferred_element_type=F32) * qdec_ref[h]
        kd = (k.astype(F32) * kdec_ref[h]).astype(BF16)
        st_sc[h] = st * cd_ref[h] + lax.dot_general(kd, v, (((0,), (0,)), ((), ())),
                                                    preferred_element_type=F32)
        o = intra + cross
        o = o * lax.rsqrt(jnp.mean(o * o, axis=-1, keepdims=True) + EPS)
        outs.append(((g / (1.0 + jnp.exp(-g))) * o).astype(BF16))
    return outs


def _inproj_kernel(cd_ref, x_ref, g_ref, w_ref, wq_ref, wv_ref, dec_ref, qdec_ref, kdec_ref,
                   k_ref, qt_ref, vt_ref, or_ref, st_sc):
    @pl.when(pl.program_id(0) == 0)
    def _():
        st_sc[...] = jnp.zeros_like(st_sc)

    x = x_ref[...]
    var = jnp.mean(x * x, axis=-1, keepdims=True)
    xn = (x * lax.rsqrt(var + EPS) * g_ref[...]).astype(BF16)

    def proj(lo, hi):
        return jnp.dot(xn, w_ref[:, lo:hi], preferred_element_type=F32)

    k_ref[...] = proj(DK_OFF, DK_OFF + DA_WIDTH).astype(BF16)
    nt = (((1,), (1,)), ((), ()))
    qt = lax.dot_general(wq_ref[...], xn, nt, preferred_element_type=F32)
    qt_ref[...] = qt.astype(BF16).reshape(DA_HEADS, 1, 2 * DA_HEAD_DIM, PROJ_TM)
    vt = lax.dot_general(wv_ref[...], xn, nt, preferred_element_type=F32)
    vt_ref[...] = vt.astype(BF16).reshape(DA_HEADS, 1, DA_V_DIM, PROJ_TM)

    rq = proj(RQ_OFF, RK_OFF).astype(BF16)
    rk = proj(RK_OFF, RV_OFF).astype(BF16)
    rv = proj(RV_OFF, RG_OFF).astype(BF16)
    rg = proj(RG_OFF, MAIN_COLS)
    for blk in range(PROJ_TM // RET_C):
        rows = slice(blk * RET_C, (blk + 1) * RET_C)
        outs = _retention_block(rq[rows], rk[rows], rv[rows], rg[rows],
                                cd_ref, dec_ref, qdec_ref, kdec_ref, st_sc)
        for h in range(RET_HEADS):
            or_ref[rows, h * RET_V_DIM:(h + 1) * RET_V_DIM] = outs[h]


def _inproj(x2, g, w_bf, wq_t, wv_t, cd, dec, qdec, kdec):
    C = RET_C
    t_shape = jax.ShapeDtypeStruct((DA_HEADS, SEQ // PROJ_TM, LANES, PROJ_TM), BF16)
    t_spec = pl.BlockSpec((DA_HEADS, 1, LANES, PROJ_TM), lambda i: (0, i, 0, 0))
    w_t_spec = pl.BlockSpec((T_ROWS, D_MODEL), lambda i: (0, 0))
    return pl.pallas_call(
        _inproj_kernel,
        out_shape=(jax.ShapeDtypeStruct((SEQ, DA_WIDTH), BF16), t_shape, t_shape,
                   jax.ShapeDtypeStruct((SEQ, RET_WIDTH), BF16)),
        grid=(SEQ // PROJ_TM,),
        in_specs=[
            pl.BlockSpec(memory_space=pltpu.SMEM),
            pl.BlockSpec((PROJ_TM, D_MODEL), lambda i: (i, 0)),
            pl.BlockSpec((1, D_MODEL), lambda i: (0, 0)),
            pl.BlockSpec((D_MODEL, MAIN_COLS), lambda i: (0, 0)),
            w_t_spec, w_t_spec,
            pl.BlockSpec((RET_HEADS, C, C), lambda i: (0, 0, 0)),
            pl.BlockSpec((RET_HEADS, C, 1), lambda i: (0, 0, 0)),
            pl.BlockSpec((RET_HEADS, C, 1), lambda i: (0, 0, 0)),
        ],
        out_specs=(pl.BlockSpec((PROJ_TM, DA_WIDTH), lambda i: (i, 0)), t_spec, t_spec,
                   pl.BlockSpec((PROJ_TM, RET_WIDTH), lambda i: (i, 0))),
        scratch_shapes=[pltpu.VMEM((RET_HEADS, RET_QK_DIM, RET_V_DIM), F32)],
        compiler_params=_params(("arbitrary",)),
        name="inproj_retention",
    )(cd, x2, g, w_bf, wq_t, wv_t, dec, qdec, kdec)


ACC_ROWS = DA_V_DIM + 16


N_QT = SEQ // ATT_T
N_OFF = N_QT * (N_QT - 1) // 2


def _pipeline3(n_pos, scores, accumulate):
    scores(0, 0)
    scores(1, 1)
    steady = n_pos - 2

    def triple(k, carry):
        t = 3 * k
        accumulate(t, 0)
        scores(t + 2, 2)
        accumulate(t + 1, 1)
        scores(t + 3, 0)
        accumulate(t + 2, 2)
        scores(t + 4, 1)
        return carry

    lax.fori_loop(0, steady // 3, triple, 0)
    t0 = steady // 3 * 3
    rem = steady - t0
    accumulate(t0, 0)
    if rem >= 1:
        scores(t0 + 2, 2)
    accumulate(t0 + 1, 1)
    if rem == 2:
        scores(t0 + 3, 0)
    if rem >= 1:
        accumulate(t0 + 2, 2)
    if rem == 2:
        accumulate(t0 + 3, 0)


def _attn_kernel(slope_ref, jt_ref, it_ref, qt_ref, k_ref, vt_ref, qa_ref, ka1_ref, ka2_ref,
                 dtab_ref, lq1_ref, lk1_ref, lq2_ref, lk2_ref, g_ref, o_ref,
                 m_sc, acc_sc, s0_sc, s1_sc, s2_sc, mx0_sc, mx1_sc, mx2_sc):
    T = ATT_T
    h = pl.program_id(0)
    slope = slope_ref[h]
    qa = qa_ref[0]
    lane = lax.broadcasted_iota(I32, (T, LANES), 1)
    sums_row = (lax.broadcasted_iota(I32, (16, T), 0) == 0).astype(BF16)
    s_bufs = (s0_sc, s1_sc, s2_sc)
    mx_bufs = (mx0_sc, mx1_sc, mx2_sc)

    def scores(j, i, buf, extra):
        kt = k_ref[pl.ds(pl.multiple_of(j * T, T), T), :]
        ks = (jnp.where(lane < DA_HEAD_DIM, kt, ka1_ref[0]),
              jnp.where(lane >= DA_HEAD_DIM, kt, ka2_ref[0]))
        qt = qt_ref[0, i]
        qw = (jnp.concatenate([qt[0:DA_HEAD_DIM], qa], axis=0),
              jnp.concatenate([qa, qt[DA_HEAD_DIM:]], axis=0))
        for mp in range(2):
            s = jnp.dot(ks[mp], qw[mp], preferred_element_type=F32)
            if extra is not None:
                s = s + extra[0]
            s_bufs[buf][mp] = s
            mx_bufs[buf][mp] = jnp.max(s, axis=0, keepdims=True)

    def accumulate(j, i, buf):
        c = slope * lax.convert_element_type((i - j) * T, F32)
        vte = jnp.concatenate([vt_ref[0, j], sums_row], axis=0)
        for mp in range(2):
            m_prev = m_sc[i, mp]
            m_new = jnp.maximum(m_prev, mx_bufs[buf][mp] - c)
            p = jnp.exp(s_bufs[buf][mp] - (m_new + c)).astype(BF16)
            pv = jnp.dot(vte, p, preferred_element_type=F32)
            acc_sc[i, mp] = jnp.exp(m_prev - m_new) * acc_sc[i, mp] + pv
            m_sc[i, mp] = m_new

    m_sc[...] = jnp.full_like(m_sc, -jnp.inf)
    acc_sc[...] = jnp.zeros_like(acc_sc)
    _pipeline3(N_QT,
               lambda pos, buf: scores(pos, pos, buf, dtab_ref),
               lambda pos, buf: accumulate(pos, pos, buf))
    _pipeline3(N_OFF,
               lambda pos, buf: scores(jt_ref[pos], it_ref[pos], buf, None),
               lambda pos, buf: accumulate(jt_ref[pos], it_ref[pos], buf))

    lam = (jnp.exp(jnp.sum(lq1_ref[...] * lk1_ref[...], axis=1, keepdims=True))
           - jnp.exp(jnp.sum(lq2_ref[...] * lk2_ref[...], axis=1, keepdims=True))
           + LAMBDA_INIT)

    def finish(i, carry):
        a1 = acc_sc[i, 0]
        a2 = acc_sc[i, 1]
        ot = (a1[0:DA_V_DIM] / a1[DA_V_DIM:DA_V_DIM + 1]
              - lam * (a2[0:DA_V_DIM] / a2[DA_V_DIM:DA_V_DIM + 1]))
        o = ot.T
        var = jnp.mean(o * o, axis=-1, keepdims=True)
        o = (o * lax.rsqrt(var + EPS) * g_ref[...]) * (1.0 - LAMBDA_INIT)
        o_ref[pl.ds(pl.multiple_of(i * T, T), T), :] = o.astype(BF16)
        return carry

    lax.fori_loop(0, N_QT, finish, 0)


def _attention(proj, qt4, vt4, slopes, qa, ka1, ka2, dtab, lq1, lk1, lq2, lk2, subln_g):
    T = ATT_T
    vec64 = pl.BlockSpec((1, DA_HEAD_DIM), lambda h: (0, 0))
    per_head = lambda a, b: pl.BlockSpec((1, a, b), lambda h: (h, 0, 0))
    slab = lambda shape, imap: pl.BlockSpec(shape, imap, pipeline_mode=pl.Buffered(1))
    smem = pl.BlockSpec(memory_space=pltpu.SMEM)
    it_tab, jt_tab = np.tril_indices(N_QT, -1)
    return pl.pallas_call(
        _attn_kernel,
        out_shape=jax.ShapeDtypeStruct((SEQ, DA_WIDTH), BF16),
        grid=(DA_HEADS,),
        in_specs=[
            smem, smem, smem,
            slab((1, N_QT, LANES, T), lambda h: (h, 0, 0, 0)),
            slab((SEQ, LANES), lambda h: (0, DK_OFF // LANES + h)),
            slab((1, N_QT, LANES, T), lambda h: (h, 0, 0, 0)),
            per_head(DA_HEAD_DIM, T), per_head(T, LANES), per_head(T, LANES), per_head(T, T),
            vec64, vec64, vec64, vec64,
            pl.BlockSpec((1, DA_V_DIM), lambda h: (0, 0)),
        ],
        out_specs=slab((SEQ, LANES), lambda h: (0, h)),
        scratch_shapes=[
            pltpu.VMEM((N_QT, 2, 1, T), F32),
            pltpu.VMEM((N_QT, 2, ACC_ROWS, T), F32),
            pltpu.VMEM((2, T, T), F32),
            pltpu.VMEM((2, T, T), F32),
            pltpu.VMEM((2, T, T), F32),
            pltpu.VMEM((2, 1, T), F32),
            pltpu.VMEM((2, 1, T), F32),
            pltpu.VMEM((2, 1, T), F32),
        ],
        compiler_params=_params(("arbitrary",)),
        name="diff_attention",
    )(slopes, jnp.asarray(jt_tab, I32), jnp.asarray(it_tab, I32), qt4, proj, vt4, qa, ka1, ka2,
      dtab, lq1, lk1, lq2, lk2, subln_g)


def _outproj_router_kernel(x_ref, oda_ref, or_ref, wo_ref, g_ref, wr_ref, br_ref,
                           h1_ref, xn_ref, ri_ref, rw_ref, cnt_ref):
    h1 = (x_ref[...]
          + jnp.dot(oda_ref[...], wo_ref[0:DA_WIDTH, :], preferred_element_type=F32)
          + jnp.dot(or_ref[...], wo_ref[DA_WIDTH:, :], preferred_element_type=F32))
    h1_ref[...] = h1
    var = jnp.mean(h1 * h1, axis=-1, keepdims=True)
    xn = h1 * lax.rsqrt(var + EPS) * g_ref[...]
    bits = lax.bitcast_convert_type(xn.astype(BF16).astype(F32), U32)
    for c in range(X_ROWS):
        lo = bits[:, c * LANES:(c + 1) * LANES] >> 16
        hi = bits[:, (c + X_ROWS) * LANES:(c + X_ROWS + 1) * LANES] & jnp.uint32(0xFFFF0000)
        xn_ref[pl.ds(c, PROJ_TM, stride=X_ROWS), :] = lo | hi
    x_hi = xn.astype(BF16)
    x_lo = (xn - x_hi.astype(F32)).astype(BF16)
    both = jnp.dot(x_hi, wr_ref[...], preferred_element_type=F32)
    logits = (both[:, :LANES] + both[:, LANES:]
              + jnp.dot(x_lo, wr_ref[:, :LANES], preferred_element_type=F32)) + br_ref[...]
    lane = lax.broadcasted_iota(I32, logits.shape, 1)
    neg = jnp.float32(-jnp.inf)
    big = jnp.int32(1 << 20)
    gl = jnp.where(lane < MOE_GROUPS, logits, neg)
    gmax = jnp.max(gl, axis=1, keepdims=True)
    gidx = jnp.min(jnp.where(gl == gmax, lane, big), axis=1, keepdims=True)
    gsum = jnp.sum(jnp.exp(gl - gmax), axis=1, keepdims=True)
    gp = 1.0 / gsum
    lo = MOE_GROUPS + gidx * MOE_EXPERTS_PER_GROUP
    el = jnp.where((lane >= lo) & (lane < lo + MOE_EXPERTS_PER_GROUP), logits, neg)
    v1 = jnp.max(el, axis=1, keepdims=True)
    i1 = jnp.min(jnp.where(el == v1, lane, big), axis=1, keepdims=True)
    el2 = jnp.where(lane == i1, neg, el)
    v2 = jnp.max(el2, axis=1, keepdims=True)
    i2 = jnp.min(jnp.where(el2 == v2, lane, big), axis=1, keepdims=True)
    t = jnp.exp(v2 - v1)
    w1 = gp / (1.0 + t)
    w2 = gp * t / (1.0 + t)
    ri_ref[...] = jnp.where(lane == 0, i1 - MOE_GROUPS,
                            jnp.where(lane == 1, i2 - MOE_GROUPS, 0))
    rw_ref[...] = jnp.where(lane == 0, w1, jnp.where(lane == 1, w2, 0.0))

    @pl.when(pl.program_id(0) == 0)
    def _():
        cnt_ref[...] = jnp.zeros_like(cnt_ref)

    chosen = (lane == i1 - MOE_GROUPS) | (lane == i2 - MOE_GROUPS)
    cnt_ref[...] += jnp.sum(chosen.astype(F32), axis=0, keepdims=True)


def _outproj_router(x2, o_da, o_r, wo_bf, g, wr, br):
    tm = PROJ_TM
    row = lambda w: pl.BlockSpec((tm, w), lambda i: (i, 0))
    full = lambda a, b: pl.BlockSpec((a, b), lambda i: (0, 0))
    return pl.pallas_call(
        _outproj_router_kernel,
        out_shape=(
            jax.ShapeDtypeStruct((SEQ, D_MODEL), F32),
            jax.ShapeDtypeStruct((SEQ * X_ROWS, LANES), U32),
            jax.ShapeDtypeStruct((SEQ, LANES), I32),
            jax.ShapeDtypeStruct((SEQ, LANES), F32),
            jax.ShapeDtypeStruct((8, LANES), F32),
        ),
        grid=(SEQ // tm,),
        in_specs=[row(D_MODEL), row(DA_WIDTH), row(RET_WIDTH), full(D_MODEL, D_MODEL),
                  full(1, D_MODEL), full(D_MODEL, 2 * LANES), full(1, LANES)],
        out_specs=(row(D_MODEL), pl.BlockSpec((tm * X_ROWS, LANES), lambda i: (i, 0)),
                   row(LANES), row(LANES), full(8, LANES)),
        compiler_params=_params(("arbitrary",)),
        name="outproj_router",
    )(x2, o_da, o_r, wo_bf, g, wr, br)


def _plan_kernel(ri_ref, cnt_ref, dest_ref, used_ref):
    TT = PLAN_T
    lane = lax.broadcasted_iota(I32, (TT, LANES), 1)

    def onehots(t):
        r = ri_ref[pl.ds(pl.multiple_of(t * TT, TT), TT), :]
        return lane == r[:, 0:1], lane == r[:, 1:2]

    counts8 = cnt_ref[...].astype(I32)
    shift = FFN_B.bit_length() - 1
    padded = ((counts8 + (FFN_B - 1)) >> shift) << shift
    lane8 = lax.broadcasted_iota(I32, (8, LANES), 1)
    pad_end = padded
    sh = 1
    while sh < LANES:
        pad_end = pad_end + jnp.where(lane8 >= sh, pltpu.roll(pad_end, sh, axis=1), 0)
        sh *= 2
    pad_start = pad_end - padded

    ltri = (lax.broadcasted_iota(I32, (TT, TT), 0)
            > lax.broadcasted_iota(I32, (TT, TT), 1)).astype(BF16)

    def dest_body(t, carry):
        oh1, oh2 = onehots(t)
        a = (oh1 | oh2).astype(F32)
        base = jnp.dot(ltri, a.astype(BF16), preferred_element_type=F32) + carry
        d1 = jnp.sum(jnp.where(oh1, base, 0.0), axis=1, keepdims=True)
        d2 = jnp.sum(jnp.where(oh2, base, 0.0), axis=1, keepdims=True)
        both = jnp.where(lane == 0, d1, jnp.where(lane == 1, d2, 0.0))
        dest_ref[t] = both.T[0:8, :].astype(I32)
        return carry + jnp.sum(a, axis=0, keepdims=True)

    lax.fori_loop(0, SEQ // TT, dest_body, pad_start[0:1].astype(F32))

    total = jnp.max(pad_end, axis=1, keepdims=True)
    row8 = lax.broadcasted_iota(I32, (8, LANES), 0)
    used_ref[...] = jnp.where(row8 == 0, jnp.broadcast_to(total >> shift, (8, LANES)),
                              jnp.where(row8 == 1, counts8, pad_start))


def _plan(ri, cnt):
    return pl.pallas_call(
        _plan_kernel,
        out_shape=(
            jax.ShapeDtypeStruct((SEQ // PLAN_T, 8, PLAN_T), I32),
            jax.ShapeDtypeStruct((8, LANES), I32),
        ),
        compiler_params=pltpu.CompilerParams(vmem_limit_bytes=VMEM_LIMIT),
        name="route_plan",
    )(ri, cnt)


PAD_BITS = FFN_B.bit_length() - 1


def _pad_fill_copies(e, cnt_ref, pst_ref, zero_sc, xs_hbm, zsem):
    cnt = cnt_ref[e]
    pad = (-cnt) & (FFN_B - 1)
    row = pst_ref[e] + cnt
    out = []
    for bit in reversed(range(PAD_BITS)):
        n = 1 << bit
        start = row + ((pad >> (bit + 1)) << (bit + 1))
        copy = pltpu.make_async_copy(
            zero_sc.at[pl.ds(0, n * X_ROWS)],
            xs_hbm.at[pl.ds(pl.multiple_of(start * X_ROWS, X_ROWS), n * X_ROWS)], zsem)
        out.append(((pad & n) != 0, copy))
    return out


def _unused_block_copies(b, zero_sc, xs_hbm, zsem):
    half = FFN_B // 2 * X_ROWS
    return [pltpu.make_async_copy(
        zero_sc, xs_hbm.at[pl.ds(pl.multiple_of((2 * b + k) * half, half), half)], zsem)
        for k in range(2)]


def _dispatch_kernel(dest_ref, cnt_ref, pst_ref, used_ref, xn_ref, xs_hbm, zero_sc, sem, zsem):
    tm = DISP_TM
    i = pl.program_id(0)

    @pl.when(i == 0)
    def _():
        zero_sc[...] = jnp.zeros_like(zero_sc)

        def fill(e, carry):
            for cond, copy in _pad_fill_copies(e, cnt_ref, pst_ref, zero_sc, xs_hbm, zsem):
                pl.when(cond)(copy.start)
            return carry

        lax.fori_loop(0, MOE_EXPERTS, fill, 0)

        def fill_block(b, carry):
            for copy in _unused_block_copies(b, zero_sc, xs_hbm, zsem):
                copy.start()
            return carry

        lax.fori_loop(used_ref[0], N_BLOCKS, fill_block, 0)

    def issue(it, carry):
        for u in range(DMA_UNROLL):
            r = it * DMA_UNROLL + u
            t = i * tm + r
            src = xn_ref.at[pl.ds(pl.multiple_of(r * X_ROWS, X_ROWS), X_ROWS)]
            for kk in range(2):
                d = pl.multiple_of(dest_ref[kk * SEQ + t] * X_ROWS, X_ROWS)
                pltpu.make_async_copy(src, xs_hbm.at[pl.ds(d, X_ROWS)], sem).start(priority=kk)
        return carry

    lax.fori_loop(0, tm // DMA_UNROLL, issue, 0)
    for _ in range(2):
        pltpu.make_async_copy(xn_ref, xs_hbm.at[pl.ds(0, tm * X_ROWS)], sem).wait()

    @pl.when(i == 0)
    def _():
        def drain(e, carry):
            for cond, copy in _pad_fill_copies(e, cnt_ref, pst_ref, zero_sc, xs_hbm, zsem):
                pl.when(cond)(copy.wait)
            return carry

        lax.fori_loop(0, MOE_EXPERTS, drain, 0)

        def drain_block(b, carry):
            for copy in _unused_block_copies(b, zero_sc, xs_hbm, zsem):
                copy.wait()
            return carry

        lax.fori_loop(used_ref[0], N_BLOCKS, drain_block, 0)


def _dispatch(dest_flat, counts, pad_start, used, xn3):
    tm = DISP_TM
    return pl.pallas_call(
        _dispatch_kernel,
        out_shape=jax.ShapeDtypeStruct((N_BUF * X_ROWS, LANES), U32),
        grid_spec=pltpu.PrefetchScalarGridSpec(
            num_scalar_prefetch=4,
            grid=(SEQ // tm,),
            in_specs=[pl.BlockSpec((tm * X_ROWS, LANES), lambda i, d, c, p, u: (i, 0))],
            out_specs=pl.BlockSpec(memory_space=pl.ANY),
            scratch_shapes=[
                pltpu.VMEM((FFN_B // 2 * X_ROWS, LANES), U32),
                pltpu.SemaphoreType.DMA(()),
                pltpu.SemaphoreType.DMA(()),
            ],
        ),
        compiler_params=_params(("arbitrary",)),
        name="moe_dispatch",
    )(dest_flat, counts, pad_start, used, xn3)


BLOCK_COPY_PRIORITY = 1


def _ffn_kernel(cnt_ref, pst_ref, used_ref, xs_hbm, wg_ref, wu_ref, wd_ref, y_hbm,
                xbuf, ybuf, zero_sc, wg_bf, wu_bf, wd_bf, sem_in, sem_out, zsem):
    B = FFN_B
    e = pl.program_id(0)
    n = (cnt_ref[e] + (B - 1)) >> PAD_BITS
    s0 = pst_ref[e] >> PAD_BITS

    def rows(blk, per_token):
        size = B * per_token
        return pl.ds(pl.multiple_of(blk * size, size), size)

    def fetch(blk, slot):
        return pltpu.make_async_copy(xs_hbm.at[rows(blk, X_ROWS)], xbuf.at[slot],
                                     sem_in.at[slot])

    def flush(blk, slot):
        return pltpu.make_async_copy(ybuf.at[slot], y_hbm.at[rows(blk, ROW_TILE)],
                                     sem_out.at[slot])

    used = used_ref[0]

    @pl.when(e == 0)
    def _():
        fetch(0, 0).start(priority=BLOCK_COPY_PRIORITY)

    @pl.when(n > 0)
    def _():
        wg_bf[...] = wg_ref[0].astype(BF16)
        wu_bf[...] = wu_ref[0].astype(BF16)
        wd_bf[...] = wd_ref[0].astype(BF16)

    def body(j, carry):
        blk = s0 + j
        slot = blk % 2

        @pl.when(blk + 1 < used)
        def _():
            fetch(blk + 1, 1 - slot).start(priority=BLOCK_COPY_PRIORITY)

        fetch(blk, slot).wait()

        @pl.when(blk >= 2)
        def _():
            flush(blk - 2, slot).wait()

        words = [xbuf[slot, pl.ds(c, B, stride=X_ROWS), :] for c in range(X_ROWS)]
        lo = [lax.bitcast_convert_type(w << 16, F32) for w in words]
        hi = [lax.bitcast_convert_type(w & jnp.uint32(0xFFFF0000), F32) for w in words]
        x = jnp.concatenate(lo + hi, axis=1).astype(BF16)
        hg = jnp.dot(x, wg_bf[...], preferred_element_type=F32)
        hu = jnp.dot(x, wu_bf[...], preferred_element_type=F32)
        hh = ((hg / (1.0 + jnp.exp(-hg))) * hu).astype(BF16)
        y = jnp.dot(hh, wd_bf[...], preferred_element_type=F32)
        for c in range(ROW_TILE):
            ybuf[slot, pl.ds(c, B, stride=ROW_TILE), :] = y[:, c * LANES:(c + 1) * LANES]
        flush(blk, slot).start(priority=BLOCK_COPY_PRIORITY)
        return carry

    lax.fori_loop(0, n, body, 0)

    @pl.when(e == MOE_EXPERTS - 1)
    def _():
        flush(used - 1, (used + 1) % 2).wait()
        flush(used - 2, used % 2).wait()
        zero_sc[...] = jnp.zeros_like(zero_sc)

        def fill(b, carry):
            pltpu.make_async_copy(zero_sc, y_hbm.at[rows(b, ROW_TILE)], zsem).start()
            return carry

        def drain(b, carry):
            pltpu.make_async_copy(zero_sc, y_hbm.at[rows(b, ROW_TILE)], zsem).wait()
            return carry

        lax.fori_loop(used_ref[0], N_BLOCKS, fill, 0)
        lax.fori_loop(used_ref[0], N_BLOCKS, drain, 0)


def _ffn(counts, pad_start, used, xs, w_gate, w_up, w_down):
    B = FFN_B
    wspec = lambda a, c: pl.BlockSpec((1, a, c), lambda e, cnt, pst, used: (e, 0, 0))
    return pl.pallas_call(
        _ffn_kernel,
        out_shape=jax.ShapeDtypeStruct((N_BUF * ROW_TILE, LANES), F32),
        grid_spec=pltpu.PrefetchScalarGridSpec(
            num_scalar_prefetch=3,
            grid=(MOE_EXPERTS,),
            in_specs=[
                pl.BlockSpec(memory_space=pl.ANY),
                wspec(D_MODEL, MOE_HIDDEN),
                wspec(D_MODEL, MOE_HIDDEN),
                wspec(MOE_HIDDEN, D_MODEL),
            ],
            out_specs=pl.BlockSpec(memory_space=pl.ANY),
            scratch_shapes=[
                pltpu.VMEM((2, B * X_ROWS, LANES), U32),
                pltpu.VMEM((2, B * ROW_TILE, LANES), F32),
                pltpu.VMEM((B * ROW_TILE, LANES), F32),
                pltpu.VMEM((D_MODEL, MOE_HIDDEN), BF16),
                pltpu.VMEM((D_MODEL, MOE_HIDDEN), BF16),
                pltpu.VMEM((MOE_HIDDEN, D_MODEL), BF16),
                pltpu.SemaphoreType.DMA((2,)),
                pltpu.SemaphoreType.DMA((2,)),
                pltpu.SemaphoreType.DMA(()),
            ],
        ),
        compiler_params=_params(("arbitrary",)),
        name="expert_ffn",
    )(counts, pad_start, used, xs, w_gate, w_up, w_down)


def _combine_kernel(dest_ref, h1_ref, rw_ref, g_ref, y_hbm, o_ref, ybuf, sem):
    tm = COMB_TM
    i = pl.program_id(0)

    def gather(tile, slot):
        def issue(it, carry):
            for u in range(DMA_UNROLL):
                r = it * DMA_UNROLL + u
                t = tile * tm + r
                for kk in range(2):
                    d = pl.multiple_of(dest_ref[kk * SEQ + t] * ROW_TILE, ROW_TILE)
                    pltpu.make_async_copy(
                        y_hbm.at[pl.ds(d, ROW_TILE)],
                        ybuf.at[slot, kk, pl.ds(pl.multiple_of(r * ROW_TILE, ROW_TILE), ROW_TILE)],
                        sem.at[slot, kk]).start(priority=kk)
            return carry

        lax.fori_loop(0, tm // DMA_UNROLL, issue, 0)

    @pl.when(i == 0)
    def _():
        gather(0, 0)

    @pl.when(i + 1 < pl.num_programs(0))
    def _():
        gather(i + 1, (i + 1) % 2)

    slot = i % 2
    for kk in range(2):
        pltpu.make_async_copy(y_hbm.at[pl.ds(0, tm * ROW_TILE)], ybuf.at[slot, kk],
                              sem.at[slot, kk]).wait()
    w = rw_ref[...]
    ys = [jnp.concatenate([ybuf[slot, kk, pl.ds(c, tm, stride=ROW_TILE), :]
                           for c in range(ROW_TILE)], axis=1) for kk in range(2)]
    h = h1_ref[...] + w[:, 0:1] * ys[0] + w[:, 1:2] * ys[1]
    var = jnp.mean(h * h, axis=-1, keepdims=True)
    o_ref[...] = h * lax.rsqrt(var + EPS) * g_ref[...]


def _combine(dest_flat, h1, rw, g, y):
    tm = COMB_TM
    return pl.pallas_call(
        _combine_kernel,
        out_shape=jax.ShapeDtypeStruct((SEQ, D_MODEL), F32),
        grid_spec=pltpu.PrefetchScalarGridSpec(
            num_scalar_prefetch=1,
            grid=(SEQ // tm,),
            in_specs=[
                pl.BlockSpec((tm, D_MODEL), lambda i, d: (i, 0)),
                pl.BlockSpec((tm, LANES), lambda i, d: (i, 0)),
                pl.BlockSpec((1, D_MODEL), lambda i, d: (0, 0)),
                pl.BlockSpec(memory_space=pl.ANY),
            ],
            out_specs=pl.BlockSpec((tm, D_MODEL), lambda i, d: (i, 0)),
            scratch_shapes=[
                pltpu.VMEM((2, 2, tm * ROW_TILE, LANES), F32),
                pltpu.SemaphoreType.DMA((2, 2)),
            ],
        ),
        compiler_params=_params(("arbitrary",)),
        name="moe_combine",
    )(dest_flat, h1, rw, g, y)


def _attention_tables():
    T = ATT_T
    f32 = np.float32
    slopes = np.exp2(-ALIBI_MAX * np.arange(1, DA_HEADS + 1, dtype=f32) / DA_HEADS).astype(f32)
    r = np.arange(T)
    hi = ((r // CHUNK) * CHUNK).astype(f32)
    lo = (r % CHUNK).astype(f32)
    sl = slopes[:, None]
    one_h = np.ones((DA_HEADS, T), f32)
    q_rows = np.stack([one_h, one_h, -sl * hi[None], -sl * lo[None]], axis=1)
    k_cols = np.stack([sl * hi[None], sl * lo[None], one_h, one_h], axis=-1)
    qa = np.zeros((DA_HEADS, DA_HEAD_DIM, T), f32)
    qa[:, 0:4, :] = q_rows
    ka1 = np.zeros((DA_HEADS, T, LANES), f32)
    ka1[:, :, DA_HEAD_DIM:DA_HEAD_DIM + 4] = k_cols
    ka2 = np.zeros((DA_HEADS, T, LANES), f32)
    ka2[:, :, 0:4] = k_cols
    rel = (r[:, None] - r[None, :]).astype(f32)
    allowed = (r[:, None] // CHUNK) <= (r[None, :] // CHUNK)
    fix = np.where(rel > 0, -2.0 * slopes[:, None, None] * rel[None], 0.0).astype(f32)
    dtab = np.where(allowed[None], fix, -np.inf).astype(f32)
    return (jnp.asarray(slopes), jnp.asarray(qa, BF16), jnp.asarray(ka1, BF16),
            jnp.asarray(ka2, BF16), jnp.asarray(dtab))


def _retention_tables():
    C = RET_C
    f32 = np.float32
    log_gamma = np.log1p(-np.exp2(-5.0 - np.arange(RET_HEADS, dtype=f32))).astype(f32)
    pos = np.arange(C, dtype=f32)
    rel = pos[:, None] - pos[None, :]
    dec = np.where(rel >= 0, np.exp(log_gamma[:, None, None] * np.maximum(rel, 0.0)), 0.0)
    qdec = np.exp(log_gamma[:, None] * (pos + 1.0)[None, :])[:, :, None]
    kdec = np.exp(log_gamma[:, None] * (C - 1 - pos)[None, :])[:, :, None]
    cd = np.exp(log_gamma * C)
    return tuple(jnp.asarray(t, F32) for t in (cd, dec, qdec, kdec))


def kernel(x, attn_norm_g, w_in, da_lambda_q1, da_lambda_k1, da_lambda_q2, da_lambda_k2,
           da_subln_g, w_out, ffn_norm_g, router_group_w, router_group_b, router_expert_w,
           router_expert_b, expert_w_gate, expert_w_up, expert_w_down, final_norm_g):
    B, S, D = x.shape
    assert (B, S, D) == (1, SEQ, D_MODEL)
    x2 = x.reshape(S, D)

    w = w_in[0]
    w_main = jnp.concatenate([
        w[:, 512:1024],
        w[:, 1536:1792],
        w[:, 1792:2048] * (RET_QK_DIM ** -0.5),
        w[:, 2048:3072]], axis=1).astype(BF16)
    wq_t = (w[:, 0:512] * (DA_HEAD_DIM ** -0.5)).T.astype(BF16)
    wv_t = w[:, 1024:1536].T.astype(BF16)
    cd, dec, qdec, kdec = _retention_tables()
    slopes, qa, ka1, ka2, dtab = _attention_tables()
    k_da, qt4, vt4, o_r = _inproj(x2, attn_norm_g[0][None, :], w_main, wq_t, wv_t,
                                  cd, dec, qdec, kdec)

    o_da = _attention(k_da, qt4, vt4, slopes, qa, ka1, ka2, dtab, da_lambda_q1, da_lambda_k1,
                      da_lambda_q2, da_lambda_k2, da_subln_g)

    wr = jnp.zeros((D, LANES), F32)
    wr = wr.at[:, :MOE_GROUPS].set(router_group_w[0])
    wr = wr.at[:, MOE_GROUPS:MOE_GROUPS + MOE_EXPERTS].set(router_expert_w[0])
    br = jnp.zeros((1, LANES), F32)
    br = br.at[0, :MOE_GROUPS].set(router_group_b[0])
    br = br.at[0, MOE_GROUPS:MOE_GROUPS + MOE_EXPERTS].set(router_expert_b[0])
    wr_hi = wr.astype(BF16)
    wr_lo = (wr - wr_hi.astype(F32)).astype(BF16)
    h1, xn, ri, rw, cnt = _outproj_router(x2, o_da, o_r, w_out[0].astype(BF16),
                                          ffn_norm_g[0][None, :],
                                          jnp.concatenate([wr_hi, wr_lo], axis=1), br)

    dest, meta = _plan(ri, cnt)
    dest_flat = dest[:, 0:2, :].transpose(1, 0, 2).reshape(N_ASSIGN)
    used1 = meta[0, :1]
    counts = meta[1, :MOE_EXPERTS]
    pad_start = meta[2, :MOE_EXPERTS]
    xs = _dispatch(dest_flat, counts, pad_start, used1, xn)

    y = _ffn(counts, pad_start, used1, xs, expert_w_gate[0], expert_w_up[0], expert_w_down[0])
    out = _combine(dest_flat, h1, rw, final_norm_g[None, :], y)
    return out.reshape(B, S, D)
```

```python
import functools
import math

import jax
import jax.numpy as jnp
import numpy as np
from jax import lax
from jax.experimental import pallas as pl
from jax.experimental.pallas import tpu as pltpu

F32 = jnp.float32
BF16 = jnp.bfloat16
I32 = jnp.int32
U32 = jnp.uint32

D_MODEL = 1024
SEQ = 16384
CHUNK = 64
EPS = 1e-6

DA_HEADS = 4
DA_HEAD_DIM = 64
DA_V_DIM = 128
DA_WIDTH = 512
ALIBI_MAX = 8.0
RET_HEADS = 4
RET_QK_DIM = 64
RET_V_DIM = 128
RET_WIDTH = 512
W_IN_COLS = 3072
T_ROWS = 512
MAIN_COLS = 2048
DK_OFF = 0
RQ_OFF = 512
RK_OFF = 768
RV_OFF = 1024
RG_OFF = 1536

MOE_GROUPS = 4
MOE_EXPERTS_PER_GROUP = 8
MOE_EXPERTS = 32
MOE_HIDDEN = 512
LAMBDA_INIT = 0.8 - 0.6 * math.exp(-0.3 * 0)

LANES = 128
ROW_TILE = 8
X_ROWS = 4
VMEM_LIMIT = 56 * 1024 * 1024

PROJ_TM = 512
ATT_T = 512
RET_C = 256
PLAN_T = 512
FFN_B = 512
N_ASSIGN = 2 * SEQ
N_BLOCKS = N_ASSIGN // FFN_B + MOE_EXPERTS
N_BUF = N_BLOCKS * FFN_B
COMB_TM = 256
DISP_TM = 512
DMA_UNROLL = 8


def _params(sem):
    return pltpu.CompilerParams(dimension_semantics=sem, vmem_limit_bytes=VMEM_LIMIT)


def _retention_block(q_all, k_all, v_all, g_all, cd_ref, dec_ref, qdec_ref, kdec_ref, st_sc):
    outs = []
    for h in range(RET_HEADS):
        qk = slice(h * RET_QK_DIM, (h + 1) * RET_QK_DIM)
        vv = slice(h * RET_V_DIM, (h + 1) * RET_V_DIM)
        q = q_all[:, qk]
        k = k_all[:, qk]
        v = v_all[:, vv]
        g = g_all[:, vv]
        s = lax.dot_general(q, k, (((1,), (1,)), ((), ())),
                            preferred_element_type=F32) * dec_ref[h]
        intra = jnp.dot(s.astype(BF16), v, preferred_element_type=F32)
        st = st_sc[h]
        cross = jnp.dot(q, st.astype(BF16), preferred_element_type=F32) * qdec_ref[h]
        kd = (k.astype(F32) * kdec_ref[h]).astype(BF16)
        st_sc[h] = st * cd_ref[h] + lax.dot_general(kd, v, (((0,), (0,)), ((), ())),
                                                    preferred_element_type=F32)
        o = intra + cross
        o = o * lax.rsqrt(jnp.mean(o * o, axis=-1, keepdims=True) + EPS)
        outs.append(((g / (1.0 + jnp.exp(-g))) * o).astype(BF16))
    return outs


def _inproj_kernel(cd_ref, x_ref, g_ref, win_ref, dec_ref, qdec_ref, kdec_ref,
                   k_ref, qt_ref, vt_ref, or_ref, st_sc, w_ref, wq_ref, wv_ref):
    @pl.when(pl.program_id(0) == 0)
    def _():
        st_sc[...] = jnp.zeros_like(st_sc)
        cols = DA_WIDTH
        w_ref[:, DK_OFF:RQ_OFF] = win_ref[:, cols:2 * cols].astype(BF16)
        w_ref[:, RQ_OFF:RK_OFF] = win_ref[:, 3 * cols:3 * cols + 256].astype(BF16)
        w_ref[:, RK_OFF:RV_OFF] = (win_ref[:, 3 * cols + 256:4 * cols]
                                   * (RET_QK_DIM ** -0.5)).astype(BF16)
        w_ref[:, RV_OFF:MAIN_COLS] = win_ref[:, 4 * cols:6 * cols].astype(BF16)
        step = 256
        for r in range(D_MODEL // step):
            rows = slice(r * step, (r + 1) * step)
            wq_ref[:, rows] = (win_ref[rows, 0:cols] * (DA_HEAD_DIM ** -0.5)).T.astype(BF16)
            wv_ref[:, rows] = win_ref[rows, 2 * cols:3 * cols].T.astype(BF16)

    x = x_ref[...]
    var = jnp.mean(x * x, axis=-1, keepdims=True)
    xn = (x * lax.rsqrt(var + EPS) * g_ref[...]).astype(BF16)

    def proj(lo, hi):
        return jnp.dot(xn, w_ref[:, lo:hi], preferred_element_type=F32)

    k_ref[...] = proj(DK_OFF, DK_OFF + DA_WIDTH).astype(BF16)
    nt = (((1,), (1,)), ((), ()))
    qt = lax.dot_general(wq_ref[...], xn, nt, preferred_element_type=F32)
    qt_ref[...] = qt.astype(BF16).reshape(DA_HEADS, 1, 2 * DA_HEAD_DIM, PROJ_TM)
    vt = lax.dot_general(wv_ref[...], xn, nt, preferred_element_type=F32)
    vt_ref[...] = vt.astype(BF16).reshape(DA_HEADS, 1, DA_V_DIM, PROJ_TM)

    rq = proj(RQ_OFF, RK_OFF).astype(BF16)
    rk = proj(RK_OFF, RV_OFF).astype(BF16)
    rv = proj(RV_OFF, RG_OFF).astype(BF16)
    rg = proj(RG_OFF, MAIN_COLS)
    for blk in range(PROJ_TM // RET_C):
        rows = slice(blk * RET_C, (blk + 1) * RET_C)
        outs = _retention_block(rq[rows], rk[rows], rv[rows], rg[rows],
                                cd_ref, dec_ref, qdec_ref, kdec_ref, st_sc)
        for h in range(RET_HEADS):
            or_ref[rows, h * RET_V_DIM:(h + 1) * RET_V_DIM] = outs[h]


def _inproj(x2, g, w_in, cd, dec, qdec, kdec):
    C = RET_C
    t_shape = jax.ShapeDtypeStruct((DA_HEADS, SEQ // PROJ_TM, LANES, PROJ_TM), BF16)
    t_spec = pl.BlockSpec((DA_HEADS, 1, LANES, PROJ_TM), lambda i: (0, i, 0, 0))
    return pl.pallas_call(
        _inproj_kernel,
        out_shape=(jax.ShapeDtypeStruct((SEQ, DA_WIDTH), BF16), t_shape, t_shape,
                   jax.ShapeDtypeStruct((SEQ, RET_WIDTH), BF16)),
        grid=(SEQ // PROJ_TM,),
        in_specs=[
            pl.BlockSpec(memory_space=pltpu.SMEM),
            pl.BlockSpec((PROJ_TM, D_MODEL), lambda i: (i, 0)),
            pl.BlockSpec((1, D_MODEL), lambda i: (0, 0)),
            pl.BlockSpec((D_MODEL, W_IN_COLS), lambda i: (0, 0), pipeline_mode=pl.Buffered(1)),
            pl.BlockSpec((RET_HEADS, C, C), lambda i: (0, 0, 0)),
            pl.BlockSpec((RET_HEADS, C, 1), lambda i: (0, 0, 0)),
            pl.BlockSpec((RET_HEADS, C, 1), lambda i: (0, 0, 0)),
        ],
        out_specs=(pl.BlockSpec((PROJ_TM, DA_WIDTH), lambda i: (i, 0)), t_spec, t_spec,
                   pl.BlockSpec((PROJ_TM, RET_WIDTH), lambda i: (i, 0))),
        scratch_shapes=[
            pltpu.VMEM((RET_HEADS, RET_QK_DIM, RET_V_DIM), F32),
            pltpu.VMEM((D_MODEL, MAIN_COLS), BF16),
            pltpu.VMEM((T_ROWS, D_MODEL), BF16),
            pltpu.VMEM((T_ROWS, D_MODEL), BF16),
        ],
        compiler_params=_params(("arbitrary",)),
        name="inproj_retention",
    )(cd, x2, g, w_in, dec, qdec, kdec)


ACC_ROWS = DA_V_DIM + 16


N_QT = SEQ // ATT_T
N_OFF = N_QT * (N_QT - 1) // 2


def _pipeline3(n_pos, scores, accumulate):
    scores(0, 0)
    scores(1, 1)
    steady = n_pos - 2

    def triple(k, carry):
        t = 3 * k
        accumulate(t, 0)
        scores(t + 2, 2)
        accumulate(t + 1, 1)
        scores(t + 3, 0)
        accumulate(t + 2, 2)
        scores(t + 4, 1)
        return carry

    lax.fori_loop(0, steady // 3, triple, 0)
    t0 = steady // 3 * 3
    rem = steady - t0
    accumulate(t0, 0)
    if rem >= 1:
        scores(t0 + 2, 2)
    accumulate(t0 + 1, 1)
    if rem == 2:
        scores(t0 + 3, 0)
    if rem >= 1:
        accumulate(t0 + 2, 2)
    if rem == 2:
        accumulate(t0 + 3, 0)


def _attn_kernel(slope_ref, jt_ref, it_ref, qt_ref, k_ref, vt_ref, qa_ref, ka1_ref, ka2_ref,
                 dtab_ref, lq1_ref, lk1_ref, lq2_ref, lk2_ref, g_ref, o_ref,
                 m_sc, acc_sc, s0_sc, s1_sc, s2_sc, mx0_sc, mx1_sc, mx2_sc):
    T = ATT_T
    h = pl.program_id(0)
    slope = slope_ref[h]
    qa = qa_ref[0]
    lane = lax.broadcasted_iota(I32, (T, LANES), 1)
    sums_row = (lax.broadcasted_iota(I32, (16, T), 0) == 0).astype(BF16)
    s_bufs = (s0_sc, s1_sc, s2_sc)
    mx_bufs = (mx0_sc, mx1_sc, mx2_sc)

    def scores(j, i, buf, extra):
        kt = k_ref[pl.ds(pl.multiple_of(j * T, T), T), :]
        ks = (jnp.where(lane < DA_HEAD_DIM, kt, ka1_ref[0]),
              jnp.where(lane >= DA_HEAD_DIM, kt, ka2_ref[0]))
        qt = qt_ref[0, i]
        qw = (jnp.concatenate([qt[0:DA_HEAD_DIM], qa], axis=0),
              jnp.concatenate([qa, qt[DA_HEAD_DIM:]], axis=0))
        for mp in range(2):
            s = jnp.dot(ks[mp], qw[mp], preferred_element_type=F32)
            if extra is not None:
                s = s + extra[0]
            s_bufs[buf][mp] = s
            mx_bufs[buf][mp] = jnp.max(s, axis=0, keepdims=True)

    def accumulate(j, i, buf):
        c = slope * lax.convert_element_type((i - j) * T, F32)
        vte = jnp.concatenate([vt_ref[0, j], sums_row], axis=0)
        for mp in range(2):
            m_prev = m_sc[i, mp]
            m_new = jnp.maximum(m_prev, mx_bufs[buf][mp] - c)
            p = jnp.exp(s_bufs[buf][mp] - (m_new + c)).astype(BF16)
            pv = jnp.dot(vte, p, preferred_element_type=F32)
            acc_sc[i, mp] = jnp.exp(m_prev - m_new) * acc_sc[i, mp] + pv
            m_sc[i, mp] = m_new

    m_sc[...] = jnp.full_like(m_sc, -jnp.inf)
    acc_sc[...] = jnp.zeros_like(acc_sc)
    _pipeline3(N_QT,
               lambda pos, buf: scores(pos, pos, buf, dtab_ref),
               lambda pos, buf: accumulate(pos, pos, buf))
    _pipeline3(N_OFF,
               lambda pos, buf: scores(jt_ref[pos], it_ref[pos], buf, None),
               lambda pos, buf: accumulate(jt_ref[pos], it_ref[pos], buf))

    lam = (jnp.exp(jnp.sum(lq1_ref[...] * lk1_ref[...], axis=1, keepdims=True))
           - jnp.exp(jnp.sum(lq2_ref[...] * lk2_ref[...], axis=1, keepdims=True))
           + LAMBDA_INIT)

    def finish(i, carry):
        a1 = acc_sc[i, 0]
        a2 = acc_sc[i, 1]
        ot = (a1[0:DA_V_DIM] / a1[DA_V_DIM:DA_V_DIM + 1]
              - lam * (a2[0:DA_V_DIM] / a2[DA_V_DIM:DA_V_DIM + 1]))
        o = ot.T
        var = jnp.mean(o * o, axis=-1, keepdims=True)
        o = (o * lax.rsqrt(var + EPS) * g_ref[...]) * (1.0 - LAMBDA_INIT)
        o_ref[pl.ds(pl.multiple_of(i * T, T), T), :] = o.astype(BF16)
        return carry

    lax.fori_loop(0, N_QT, finish, 0)


def _attention(proj, qt4, vt4, slopes, qa, ka1, ka2, dtab, lq1, lk1, lq2, lk2, subln_g):
    T = ATT_T
    vec64 = pl.BlockSpec((1, DA_HEAD_DIM), lambda h: (0, 0))
    per_head = lambda a, b: pl.BlockSpec((1, a, b), lambda h: (h, 0, 0))
    slab = lambda shape, imap: pl.BlockSpec(shape, imap, pipeline_mode=pl.Buffered(1))
    smem = pl.BlockSpec(memory_space=pltpu.SMEM)
    it_tab, jt_tab = np.tril_indices(N_QT, -1)
    return pl.pallas_call(
        _attn_kernel,
        out_shape=jax.ShapeDtypeStruct((SEQ, DA_WIDTH), BF16),
        grid=(DA_HEADS,),
        in_specs=[
            smem, smem, smem,
            slab((1, N_QT, LANES, T), lambda h: (h, 0, 0, 0)),
            slab((SEQ, LANES), lambda h: (0, DK_OFF // LANES + h)),
            slab((1, N_QT, LANES, T), lambda h: (h, 0, 0, 0)),
            per_head(DA_HEAD_DIM, T), per_head(T, LANES), per_head(T, LANES), per_head(T, T),
            vec64, vec64, vec64, vec64,
            pl.BlockSpec((1, DA_V_DIM), lambda h: (0, 0)),
        ],
        out_specs=slab((SEQ, LANES), lambda h: (0, h)),
        scratch_shapes=[
            pltpu.VMEM((N_QT, 2, 1, T), F32),
            pltpu.VMEM((N_QT, 2, ACC_ROWS, T), F32),
            pltpu.VMEM((2, T, T), F32),
            pltpu.VMEM((2, T, T), F32),
            pltpu.VMEM((2, T, T), F32),
            pltpu.VMEM((2, 1, T), F32),
            pltpu.VMEM((2, 1, T), F32),
            pltpu.VMEM((2, 1, T), F32),
        ],
        compiler_params=_params(("arbitrary",)),
        name="diff_attention",
    )(slopes, jnp.asarray(jt_tab, I32), jnp.asarray(it_tab, I32), qt4, proj, vt4, qa, ka1, ka2,
      dtab, lq1, lk1, lq2, lk2, subln_g)


def _outproj_router_kernel(x_ref, oda_ref, or_ref, wo_ref, g_ref, wr_ref, br_ref,
                           h1_ref, xn_ref, ri_ref, rw_ref, cnt_ref):
    h1 = (x_ref[...]
          + jnp.dot(oda_ref[...], wo_ref[0:DA_WIDTH, :], preferred_element_type=F32)
          + jnp.dot(or_ref[...], wo_ref[DA_WIDTH:, :], preferred_element_type=F32))
    h1_ref[...] = h1
    var = jnp.mean(h1 * h1, axis=-1, keepdims=True)
    xn = h1 * lax.rsqrt(var + EPS) * g_ref[...]
    bits = lax.bitcast_convert_type(xn.astype(BF16).astype(F32), U32)
    for c in range(X_ROWS):
        lo = bits[:, c * LANES:(c + 1) * LANES] >> 16
        hi = bits[:, (c + X_ROWS) * LANES:(c + X_ROWS + 1) * LANES] & jnp.uint32(0xFFFF0000)
        xn_ref[pl.ds(c, PROJ_TM, stride=X_ROWS), :] = lo | hi
    x_hi = xn.astype(BF16)
    x_lo = (xn - x_hi.astype(F32)).astype(BF16)
    both = jnp.dot(x_hi, wr_ref[...], preferred_element_type=F32)
    logits = (both[:, :LANES] + both[:, LANES:]
              + jnp.dot(x_lo, wr_ref[:, :LANES], preferred_element_type=F32)) + br_ref[...]
    lane = lax.broadcasted_iota(I32, logits.shape, 1)
    neg = jnp.float32(-jnp.inf)
    big = jnp.int32(1 << 20)
    gl = jnp.where(lane < MOE_GROUPS, logits, neg)
    gmax = jnp.max(gl, axis=1, keepdims=True)
    gidx = jnp.min(jnp.where(gl == gmax, lane, big), axis=1, keepdims=True)
    gsum = jnp.sum(jnp.exp(gl - gmax), axis=1, keepdims=True)
    gp = 1.0 / gsum
    lo = MOE_GROUPS + gidx * MOE_EXPERTS_PER_GROUP
    el = jnp.where((lane >= lo) & (lane < lo + MOE_EXPERTS_PER_GROUP), logits, neg)
    v1 = jnp.max(el, axis=1, keepdims=True)
    i1 = jnp.min(jnp.where(el == v1, lane, big), axis=1, keepdims=True)
    el2 = jnp.where(lane == i1, neg, el)
    v2 = jnp.max(el2, axis=1, keepdims=True)
    i2 = jnp.min(jnp.where(el2 == v2, lane, big), axis=1, keepdims=True)
    t = jnp.exp(v2 - v1)
    w1 = gp / (1.0 + t)
    w2 = gp * t / (1.0 + t)
    ri_ref[...] = jnp.where(lane == 0, i1 - MOE_GROUPS,
                            jnp.where(lane == 1, i2 - MOE_GROUPS, 0))
    rw_ref[...] = jnp.where(lane == 0, w1, jnp.where(lane == 1, w2, 0.0))

    @pl.when(pl.program_id(0) == 0)
    def _():
        cnt_ref[...] = jnp.zeros_like(cnt_ref)

    chosen = (lane == i1 - MOE_GROUPS) | (lane == i2 - MOE_GROUPS)
    cnt_ref[...] += jnp.sum(chosen.astype(F32), axis=0, keepdims=True)


def _outproj_router(x2, o_da, o_r, wo_bf, g, wr, br):
    tm = PROJ_TM
    row = lambda w: pl.BlockSpec((tm, w), lambda i: (i, 0))
    full = lambda a, b: pl.BlockSpec((a, b), lambda i: (0, 0))
    return pl.pallas_call(
        _outproj_router_kernel,
        out_shape=(
            jax.ShapeDtypeStruct((SEQ, D_MODEL), F32),
            jax.ShapeDtypeStruct((SEQ * X_ROWS, LANES), U32),
            jax.ShapeDtypeStruct((SEQ, LANES), I32),
            jax.ShapeDtypeStruct((SEQ, LANES), F32),
            jax.ShapeDtypeStruct((8, LANES), F32),
        ),
        grid=(SEQ // tm,),
        in_specs=[row(D_MODEL), row(DA_WIDTH), row(RET_WIDTH), full(D_MODEL, D_MODEL),
                  full(1, D_MODEL), full(D_MODEL, 2 * LANES), full(1, LANES)],
        out_specs=(row(D_MODEL), pl.BlockSpec((tm * X_ROWS, LANES), lambda i: (i, 0)),
                   row(LANES), row(LANES), full(8, LANES)),
        compiler_params=_params(("arbitrary",)),
        name="outproj_router",
    )(x2, o_da, o_r, wo_bf, g, wr, br)


def _plan_kernel(ri_ref, cnt_ref, dest_ref, used_ref):
    TT = PLAN_T
    lane = lax.broadcasted_iota(I32, (TT, LANES), 1)

    def onehots(t):
        r = ri_ref[pl.ds(pl.multiple_of(t * TT, TT), TT), :]
        return lane == r[:, 0:1], lane == r[:, 1:2]

    counts8 = cnt_ref[...].astype(I32)
    shift = FFN_B.bit_length() - 1
    padded = ((counts8 + (FFN_B - 1)) >> shift) << shift
    lane8 = lax.broadcasted_iota(I32, (8, LANES), 1)
    pad_end = padded
    sh = 1
    while sh < LANES:
        pad_end = pad_end + jnp.where(lane8 >= sh, pltpu.roll(pad_end, sh, axis=1), 0)
        sh *= 2
    pad_start = pad_end - padded

    ltri = (lax.broadcasted_iota(I32, (TT, TT), 0)
            > lax.broadcasted_iota(I32, (TT, TT), 1)).astype(BF16)

    def dest_body(t, carry):
        oh1, oh2 = onehots(t)
        a = (oh1 | oh2).astype(F32)
        base = jnp.dot(ltri, a.astype(BF16), preferred_element_type=F32) + carry
        d1 = jnp.sum(jnp.where(oh1, base, 0.0), axis=1, keepdims=True)
        d2 = jnp.sum(jnp.where(oh2, base, 0.0), axis=1, keepdims=True)
        both = jnp.where(lane == 0, d1, jnp.where(lane == 1, d2, 0.0))
        dest_ref[t] = both.T[0:8, :].astype(I32)
        return carry + jnp.sum(a, axis=0, keepdims=True)

    lax.fori_loop(0, SEQ // TT, dest_body, pad_start[0:1].astype(F32))

    total = jnp.max(pad_end, axis=1, keepdims=True)
    row8 = lax.broadcasted_iota(I32, (8, LANES), 0)
    used_ref[...] = jnp.where(row8 == 0, jnp.broadcast_to(total >> shift, (8, LANES)),
                              jnp.where(row8 == 1, counts8, pad_start))


def _plan(ri, cnt):
    return pl.pallas_call(
        _plan_kernel,
        out_shape=(
            jax.ShapeDtypeStruct((SEQ // PLAN_T, 8, PLAN_T), I32),
            jax.ShapeDtypeStruct((8, LANES), I32),
        ),
        compiler_params=pltpu.CompilerParams(vmem_limit_bytes=VMEM_LIMIT),
        name="route_plan",
    )(ri, cnt)


PAD_BITS = FFN_B.bit_length() - 1


def _pad_fill_copies(e, cnt_ref, pst_ref, zero_sc, xs_hbm, zsem):
    cnt = cnt_ref[e]
    pad = (-cnt) & (FFN_B - 1)
    row = pst_ref[e] + cnt
    out = []
    for bit in reversed(range(PAD_BITS)):
        n = 1 << bit
        start = row + ((pad >> (bit + 1)) << (bit + 1))
        copy = pltpu.make_async_copy(
            zero_sc.at[pl.ds(0, n * X_ROWS)],
            xs_hbm.at[pl.ds(pl.multiple_of(start * X_ROWS, X_ROWS), n * X_ROWS)], zsem)
        out.append(((pad & n) != 0, copy))
    return out


def _unused_block_copies(b, zero_sc, xs_hbm, zsem):
    half = FFN_B // 2 * X_ROWS
    return [pltpu.make_async_copy(
        zero_sc, xs_hbm.at[pl.ds(pl.multiple_of((2 * b + k) * half, half), half)], zsem)
        for k in range(2)]


def _dispatch_kernel(dest_ref, cnt_ref, pst_ref, used_ref, xn_ref, xs_hbm, zero_sc, sem, zsem):
    tm = DISP_TM
    i = pl.program_id(0)

    @pl.when(i == 0)
    def _():
        zero_sc[...] = jnp.zeros_like(zero_sc)

        def fill(e, carry):
            for cond, copy in _pad_fill_copies(e, cnt_ref, pst_ref, zero_sc, xs_hbm, zsem):
                pl.when(cond)(copy.start)
            return carry

        lax.fori_loop(0, MOE_EXPERTS, fill, 0)

        def fill_block(b, carry):
            for copy in _unused_block_copies(b, zero_sc, xs_hbm, zsem):
                copy.start()
            return carry

        lax.fori_loop(used_ref[0], N_BLOCKS, fill_block, 0)

    def issue(it, carry):
        for u in range(DMA_UNROLL):
            r = it * DMA_UNROLL + u
            t = i * tm + r
            src = xn_ref.at[pl.ds(pl.multiple_of(r * X_ROWS, X_ROWS), X_ROWS)]
            for kk in range(2):
                d = pl.multiple_of(dest_ref[kk * SEQ + t] * X_ROWS, X_ROWS)
                pltpu.make_async_copy(src, xs_hbm.at[pl.ds(d, X_ROWS)], sem).start(priority=kk)
        return carry

    lax.fori_loop(0, tm // DMA_UNROLL, issue, 0)
    for _ in range(2):
        pltpu.make_async_copy(xn_ref, xs_hbm.at[pl.ds(0, tm * X_ROWS)], sem).wait()

    @pl.when(i == 0)
    def _():
        def drain(e, carry):
            for cond, copy in _pad_fill_copies(e, cnt_ref, pst_ref, zero_sc, xs_hbm, zsem):
                pl.when(cond)(copy.wait)
            return carry

        lax.fori_loop(0, MOE_EXPERTS, drain, 0)

        def drain_block(b, carry):
            for copy in _unused_block_copies(b, zero_sc, xs_hbm, zsem):
                copy.wait()
            return carry

        lax.fori_loop(used_ref[0], N_BLOCKS, drain_block, 0)


def _dispatch(dest_flat, counts, pad_start, used, xn3):
    tm = DISP_TM
    return pl.pallas_call(
        _dispatch_kernel,
        out_shape=jax.ShapeDtypeStruct((N_BUF * X_ROWS, LANES), U32),
        grid_spec=pltpu.PrefetchScalarGridSpec(
            num_scalar_prefetch=4,
            grid=(SEQ // tm,),
            in_specs=[pl.BlockSpec((tm * X_ROWS, LANES), lambda i, d, c, p, u: (i, 0))],
            out_specs=pl.BlockSpec(memory_space=pl.ANY),
            scratch_shapes=[
                pltpu.VMEM((FFN_B // 2 * X_ROWS, LANES), U32),
                pltpu.SemaphoreType.DMA(()),
                pltpu.SemaphoreType.DMA(()),
            ],
        ),
        compiler_params=_params(("arbitrary",)),
        name="moe_dispatch",
    )(dest_flat, counts, pad_start, used, xn3)


BLOCK_COPY_PRIORITY = 1


def _ffn_kernel(cnt_ref, pst_ref, used_ref, xs_hbm, wg_ref, wu_ref, wd_ref, y_hbm,
                xbuf, ybuf, zero_sc, wg_bf, wu_bf, wd_bf, sem_in, sem_out, zsem):
    B = FFN_B
    e = pl.program_id(0)
    n = (cnt_ref[e] + (B - 1)) >> PAD_BITS
    s0 = pst_ref[e] >> PAD_BITS

    def rows(blk, per_token):
        size = B * per_token
        return pl.ds(pl.multiple_of(blk * size, size), size)

    def fetch(blk, slot):
        return pltpu.make_async_copy(xs_hbm.at[rows(blk, X_ROWS)], xbuf.at[slot],
                                     sem_in.at[slot])

    def flush(blk, slot):
        return pltpu.make_async_copy(ybuf.at[slot], y_hbm.at[rows(blk, ROW_TILE)],
                                     sem_out.at[slot])

    used = used_ref[0]

    @pl.when(e == 0)
    def _():
        fetch(0, 0).start(priority=BLOCK_COPY_PRIORITY)

    @pl.when(n > 0)
    def _():
        wg_bf[...] = wg_ref[0].astype(BF16)
        wu_bf[...] = wu_ref[0].astype(BF16)
        wd_bf[...] = wd_ref[0].astype(BF16)

    def body(j, carry):
        blk = s0 + j
        slot = blk % 2

        @pl.when(blk + 1 < used)
        def _():
            fetch(blk + 1, 1 - slot).start(priority=BLOCK_COPY_PRIORITY)

        fetch(blk, slot).wait()

        @pl.when(blk >= 2)
        def _():
            flush(blk - 2, slot).wait()

        words = [xbuf[slot, pl.ds(c, B, stride=X_ROWS), :] for c in range(X_ROWS)]
        lo = [lax.bitcast_convert_type(w << 16, F32) for w in words]
        hi = [lax.bitcast_convert_type(w & jnp.uint32(0xFFFF0000), F32) for w in words]
        x = jnp.concatenate(lo + hi, axis=1).astype(BF16)
        hg = jnp.dot(x, wg_bf[...], preferred_element_type=F32)
        hu = jnp.dot(x, wu_bf[...], preferred_element_type=F32)
        hh = ((hg / (1.0 + jnp.exp(-hg))) * hu).astype(BF16)
        y = jnp.dot(hh, wd_bf[...], preferred_element_type=F32)
        for c in range(ROW_TILE):
            ybuf[slot, pl.ds(c, B, stride=ROW_TILE), :] = y[:, c * LANES:(c + 1) * LANES]
        flush(blk, slot).start(priority=BLOCK_COPY_PRIORITY)
        return carry

    lax.fori_loop(0, n, body, 0)

    @pl.when(e == MOE_EXPERTS - 1)
    def _():
        flush(used - 1, (used + 1) % 2).wait()
        flush(used - 2, used % 2).wait()
        zero_sc[...] = jnp.zeros_like(zero_sc)

        def fill(b, carry):
            pltpu.make_async_copy(zero_sc, y_hbm.at[rows(b, ROW_TILE)], zsem).start()
            return carry

        def drain(b, carry):
            pltpu.make_async_copy(zero_sc, y_hbm.at[rows(b, ROW_TILE)], zsem).wait()
            return carry

        lax.fori_loop(used_ref[0], N_BLOCKS, fill, 0)
        lax.fori_loop(used_ref[0], N_BLOCKS, drain, 0)


def _ffn(counts, pad_start, used, xs, w_gate, w_up, w_down):
    B = FFN_B
    wspec = lambda a, c: pl.BlockSpec((1, a, c), lambda e, cnt, pst, used: (e, 0, 0))
    return pl.pallas_call(
        _ffn_kernel,
        out_shape=jax.ShapeDtypeStruct((N_BUF * ROW_TILE, LANES), F32),
        grid_spec=pltpu.PrefetchScalarGridSpec(
            num_scalar_prefetch=3,
            grid=(MOE_EXPERTS,),
            in_specs=[
                pl.BlockSpec(memory_space=pl.ANY),
                wspec(D_MODEL, MOE_HIDDEN),
                wspec(D_MODEL, MOE_HIDDEN),
                wspec(MOE_HIDDEN, D_MODEL),
            ],
            out_specs=pl.BlockSpec(memory_space=pl.ANY),
            scratch_shapes=[
                pltpu.VMEM((2, B * X_ROWS, LANES), U32),
                pltpu.VMEM((2, B * ROW_TILE, LANES), F32),
                pltpu.VMEM((B * ROW_TILE, LANES), F32),
                pltpu.VMEM((D_MODEL, MOE_HIDDEN), BF16),
                pltpu.VMEM((D_MODEL, MOE_HIDDEN), BF16),
                pltpu.VMEM((MOE_HIDDEN, D_MODEL), BF16),
                pltpu.SemaphoreType.DMA((2,)),
                pltpu.SemaphoreType.DMA((2,)),
                pltpu.SemaphoreType.DMA(()),
            ],
        ),
        compiler_params=_params(("arbitrary",)),
        name="expert_ffn",
    )(counts, pad_start, used, xs, w_gate, w_up, w_down)


def _combine_kernel(dest_ref, h1_ref, rw_ref, g_ref, y_hbm, o_ref, ybuf, sem):
    tm = COMB_TM
    i = pl.program_id(0)

    def gather(tile, slot):
        def issue(it, carry):
            for u in range(DMA_UNROLL):
                r = it * DMA_UNROLL + u
                t = tile * tm + r
                for kk in range(2):
                    d = pl.multiple_of(dest_ref[kk * SEQ + t] * ROW_TILE, ROW_TILE)
                    pltpu.make_async_copy(
                        y_hbm.at[pl.ds(d, ROW_TILE)],
                        ybuf.at[slot, kk, pl.ds(pl.multiple_of(r * ROW_TILE, ROW_TILE), ROW_TILE)],
                        sem.at[slot, kk]).start(priority=kk)
            return carry

        lax.fori_loop(0, tm // DMA_UNROLL, issue, 0)

    @pl.when(i == 0)
    def _():
        gather(0, 0)

    @pl.when(i + 1 < pl.num_programs(0))
    def _():
        gather(i + 1, (i + 1) % 2)

    slot = i % 2
    for kk in range(2):
        pltpu.make_async_copy(y_hbm.at[pl.ds(0, tm * ROW_TILE)], ybuf.at[slot, kk],
                              sem.at[slot, kk]).wait()
    w = rw_ref[...]
    ys = [jnp.concatenate([ybuf[slot, kk, pl.ds(c, tm, stride=ROW_TILE), :]
                           for c in range(ROW_TILE)], axis=1) for kk in range(2)]
    h = h1_ref[...] + w[:, 0:1] * ys[0] + w[:, 1:2] * ys[1]
    var = jnp.mean(h * h, axis=-1, keepdims=True)
    o_ref[...] = h * lax.rsqrt(var + EPS) * g_ref[...]


def _combine(dest_flat, h1, rw, g, y):
    tm = COMB_TM
    return pl.pallas_call(
        _combine_kernel,
        out_shape=jax.ShapeDtypeStruct((SEQ, D_MODEL), F32),
        grid_spec=pltpu.PrefetchScalarGridSpec(
            num_scalar_prefetch=1,
            grid=(SEQ // tm,),
            in_specs=[
                pl.BlockSpec((tm, D_MODEL), lambda i, d: (i, 0)),
                pl.BlockSpec((tm, LANES), lambda i, d: (i, 0)),
                pl.BlockSpec((1, D_MODEL), lambda i, d: (0, 0)),
                pl.BlockSpec(memory_space=pl.ANY),
            ],
            out_specs=pl.BlockSpec((tm, D_MODEL), lambda i, d: (i, 0)),
            scratch_shapes=[
                pltpu.VMEM((2, 2, tm * ROW_TILE, LANES), F32),
                pltpu.SemaphoreType.DMA((2, 2)),
            ],
        ),
        compiler_params=_params(("arbitrary",)),
        name="moe_combine",
    )(dest_flat, h1, rw, g, y)


def _attention_tables():
    T = ATT_T
    f32 = np.float32
    slopes = np.exp2(-ALIBI_MAX * np.arange(1, DA_HEADS + 1, dtype=f32) / DA_HEADS).astype(f32)
    r = np.arange(T)
    hi = ((r // CHUNK) * CHUNK).astype(f32)
    lo = (r % CHUNK).astype(f32)
    sl = slopes[:, None]
    one_h = np.ones((DA_HEADS, T), f32)
    q_rows = np.stack([one_h, one_h, -sl * hi[None], -sl * lo[None]], axis=1)
    k_cols = np.stack([sl * hi[None], sl * lo[None], one_h, one_h], axis=-1)
    qa = np.zeros((DA_HEADS, DA_HEAD_DIM, T), f32)
    qa[:, 0:4, :] = q_rows
    ka1 = np.zeros((DA_HEADS, T, LANES), f32)
    ka1[:, :, DA_HEAD_DIM:DA_HEAD_DIM + 4] = k_cols
    ka2 = np.zeros((DA_HEADS, T, LANES), f32)
    ka2[:, :, 0:4] = k_cols
    rel = (r[:, None] - r[None, :]).astype(f32)
    allowed = (r[:, None] // CHUNK) <= (r[None, :] // CHUNK)
    fix = np.where(rel > 0, -2.0 * slopes[:, None, None] * rel[None], 0.0).astype(f32)
    dtab = np.where(allowed[None], fix, -np.inf).astype(f32)
    return (jnp.asarray(slopes), jnp.asarray(qa, BF16), jnp.asarray(ka1, BF16),
            jnp.asarray(ka2, BF16), jnp.asarray(dtab))


def _retention_tables():
    C = RET_C
    f32 = np.float32
    log_gamma = np.log1p(-np.exp2(-5.0 - np.arange(RET_HEADS, dtype=f32))).astype(f32)
    pos = np.arange(C, dtype=f32)
    rel = pos[:, None] - pos[None, :]
    dec = np.where(rel >= 0, np.exp(log_gamma[:, None, None] * np.maximum(rel, 0.0)), 0.0)
    qdec = np.exp(log_gamma[:, None] * (pos + 1.0)[None, :])[:, :, None]
    kdec = np.exp(log_gamma[:, None] * (C - 1 - pos)[None, :])[:, :, None]
    cd = np.exp(log_gamma * C)
    return tuple(jnp.asarray(t, F32) for t in (cd, dec, qdec, kdec))


def kernel(x, attn_norm_g, w_in, da_lambda_q1, da_lambda_k1, da_lambda_q2, da_lambda_k2,
           da_subln_g, w_out, ffn_norm_g, router_group_w, router_group_b, router_expert_w,
           router_expert_b, expert_w_gate, expert_w_up, expert_w_down, final_norm_g):
    B, S, D = x.shape
    assert (B, S, D) == (1, SEQ, D_MODEL)
    x2 = x.reshape(S, D)

    cd, dec, qdec, kdec = _retention_tables()
    slopes, qa, ka1, ka2, dtab = _attention_tables()
    k_da, qt4, vt4, o_r = _inproj(x2, attn_norm_g[0][None, :], w_in[0], cd, dec, qdec, kdec)

    o_da = _attention(k_da, qt4, vt4, slopes, qa, ka1, ka2, dtab, da_lambda_q1, da_lambda_k1,
                      da_lambda_q2, da_lambda_k2, da_subln_g)

    wr = jnp.zeros((D, LANES), F32)
    wr = wr.at[:, :MOE_GROUPS].set(router_group_w[0])
    wr = wr.at[:, MOE_GROUPS:MOE_GROUPS + MOE_EXPERTS].set(router_expert_w[0])
    br = jnp.zeros((1, LANES), F32)
    br = br.at[0, :MOE_GROUPS].set(router_group_b[0])
    br = br.at[0, MOE_GROUPS:MOE_GROUPS + MOE_EXPERTS].set(router_expert_b[0])
    wr_hi = wr.astype(BF16)
    wr_lo = (wr - wr_hi.astype(F32)).astype(BF16)
    h1, xn, ri, rw, cnt = _outproj_router(x2, o_da, o_r, w_out[0].astype(BF16),
                                          ffn_norm_g[0][None, :],
                                          jnp.concatenate([wr_hi, wr_lo], axis=1), br)

    dest, meta = _plan(ri, cnt)
    dest_flat = dest[:, 0:2, :].transpose(1, 0, 2).reshape(N_ASSIGN)
    used1 = meta[0, :1]
    counts = meta[1, :MOE_EXPERTS]
    pad_start = meta[2, :MOE_EXPERTS]
    xs = _dispatch(dest_flat, counts, pad_start, used1, xn)

    y = _ffn(counts, pad_start, used1, xs, expert_w_gate[0], expert_w_up[0], expert_w_down[0])
    out = _combine(dest_flat, h1, rw, final_norm_g[None, :], y)
    return out.reshape(B, S, D)
```

```python
import math

import jax
import jax.numpy as jnp
import numpy as np
from jax import lax
from jax.experimental import pallas as pl
from jax.experimental.pallas import tpu as pltpu

F32 = jnp.float32
BF16 = jnp.bfloat16
I32 = jnp.int32
U32 = jnp.uint32

D_MODEL = 1024
SEQ = 16384
CHUNK = 64
EPS = 1e-6

DA_HEADS = 4
DA_HEAD_DIM = 64
DA_V_DIM = 128
DA_WIDTH = 512
ALIBI_MAX = 8.0
RET_HEADS = 4
RET_QK_DIM = 64
RET_V_DIM = 128
RET_WIDTH = 512
W_IN_COLS = 3072
T_ROWS = 512
MAIN_COLS = 2048
DK_OFF = 0
RQ_OFF = 512
RK_OFF = 768
RV_OFF = 1024
RG_OFF = 1536

MOE_GROUPS = 4
MOE_EXPERTS_PER_GROUP = 8
MOE_EXPERTS = 32
MOE_HIDDEN = 512
LAMBDA_INIT = 0.8 - 0.6 * math.exp(-0.3 * 0)

LANES = 128
ROW_TILE = 8
X_ROWS = 4
VMEM_LIMIT = 56 * 1024 * 1024

PROJ_TM = 512
ROUTER_TM = 1024
ATT_T = 512
RET_C = 256
PLAN_T = 512
FFN_B = 512
N_ASSIGN = 2 * SEQ
N_BLOCKS = N_ASSIGN // FFN_B + MOE_EXPERTS
N_BUF = N_BLOCKS * FFN_B
COMB_TM = 512
DISP_TM = 512
DMA_UNROLL = 8


def _params(sem):
    return pltpu.CompilerParams(dimension_semantics=sem, vmem_limit_bytes=VMEM_LIMIT)


def _retention_block(q_all, k_all, v_all, g_all, cd_ref, dec_ref, qdec_ref, kdec_ref, st_sc):
    outs = []
    for h in range(RET_HEADS):
        qk = slice(h * RET_QK_DIM, (h + 1) * RET_QK_DIM)
        vv = slice(h * RET_V_DIM, (h + 1) * RET_V_DIM)
        q = q_all[:, qk]
        k = k_all[:, qk]
        v = v_all[:, vv]
        g = g_all[:, vv]
        s = lax.dot_general(q, k, (((1,), (1,)), ((), ())),
                            preferred_element_type=F32) * dec_ref[h]
        intra = jnp.dot(s.astype(BF16), v, preferred_element_type=F32)
        st = st_sc[h]
        cross = jnp.dot(q, st.astype(BF16), preferred_element_type=F32) * qdec_ref[h]
        kd = (k.astype(F32) * kdec_ref[h]).astype(BF16)
        st_sc[h] = st * cd_ref[h] + lax.dot_general(kd, v, (((0,), (0,)), ((), ())),
                                                    preferred_element_type=F32)
        o = intra + cross
        o = o * lax.rsqrt(jnp.mean(o * o, axis=-1, keepdims=True) + EPS)
        outs.append(((g / (1.0 + jnp.exp(-g))) * o).astype(BF16))
    return outs


def _inproj_kernel(cd_ref, x_ref, g_ref, win_ref, dec_ref, qdec_ref, kdec_ref,
                   k_ref, qt_ref, vt_ref, or_ref, st_sc, w_ref, wq_ref, wv_ref):
    @pl.when(pl.program_id(0) == 0)
    def _():
        st_sc[...] = jnp.zeros_like(st_sc)
        cols = DA_WIDTH
        w_ref[:, DK_OFF:RQ_OFF] = win_ref[:, cols:2 * cols].astype(BF16)
        w_ref[:, RQ_OFF:RK_OFF] = win_ref[:, 3 * cols:3 * cols + 256].astype(BF16)
        w_ref[:, RK_OFF:RV_OFF] = (win_ref[:, 3 * cols + 256:4 * cols]
                                   * (RET_QK_DIM ** -0.5)).astype(BF16)
        w_ref[:, RV_OFF:MAIN_COLS] = win_ref[:, 4 * cols:6 * cols].astype(BF16)
        step = 256
        for r in range(D_MODEL // step):
            rows = slice(r * step, (r + 1) * step)
            wq_ref[:, rows] = (win_ref[rows, 0:cols] * (DA_HEAD_DIM ** -0.5)).T.astype(BF16)
            wv_ref[:, rows] = win_ref[rows, 2 * cols:3 * cols].T.astype(BF16)

    x = x_ref[...]
    var = jnp.mean(x * x, axis=-1, keepdims=True)
    xn = (x * lax.rsqrt(var + EPS) * g_ref[...]).astype(BF16)

    def proj(lo, hi):
        return jnp.dot(xn, w_ref[:, lo:hi], preferred_element_type=F32)

    k_ref[...] = proj(DK_OFF, DK_OFF + DA_WIDTH).astype(BF16)
    nt = (((1,), (1,)), ((), ()))
    qt = lax.dot_general(wq_ref[...], xn, nt, preferred_element_type=F32)
    qt_ref[...] = qt.astype(BF16).reshape(DA_HEADS, 1, 2 * DA_HEAD_DIM, PROJ_TM)
    vt = lax.dot_general(wv_ref[...], xn, nt, preferred_element_type=F32)
    vt_ref[...] = vt.astype(BF16).reshape(DA_HEADS, 1, DA_V_DIM, PROJ_TM)

    rq = proj(RQ_OFF, RK_OFF).astype(BF16)
    rk = proj(RK_OFF, RV_OFF).astype(BF16)
    rv = proj(RV_OFF, RG_OFF).astype(BF16)
    rg = proj(RG_OFF, MAIN_COLS)
    for blk in range(PROJ_TM // RET_C):
        rows = slice(blk * RET_C, (blk + 1) * RET_C)
        outs = _retention_block(rq[rows], rk[rows], rv[rows], rg[rows],
                                cd_ref, dec_ref, qdec_ref, kdec_ref, st_sc)
        for h in range(RET_HEADS):
            or_ref[rows, h * RET_V_DIM:(h + 1) * RET_V_DIM] = outs[h]


def _inproj(x2, g, w_in, cd, dec, qdec, kdec):
    C = RET_C
    t_shape = jax.ShapeDtypeStruct((DA_HEADS, SEQ // PROJ_TM, LANES, PROJ_TM), BF16)
    t_spec = pl.BlockSpec((DA_HEADS, 1, LANES, PROJ_TM), lambda i: (0, i, 0, 0))
    return pl.pallas_call(
        _inproj_kernel,
        out_shape=(jax.ShapeDtypeStruct((SEQ, DA_WIDTH), BF16), t_shape, t_shape,
                   jax.ShapeDtypeStruct((SEQ, RET_WIDTH), BF16)),
        grid=(SEQ // PROJ_TM,),
        in_specs=[
            pl.BlockSpec(memory_space=pltpu.SMEM),
            pl.BlockSpec((PROJ_TM, D_MODEL), lambda i: (i, 0)),
            pl.BlockSpec((1, D_MODEL), lambda i: (0, 0)),
            pl.BlockSpec((D_MODEL, W_IN_COLS), lambda i: (0, 0), pipeline_mode=pl.Buffered(1)),
            pl.BlockSpec((RET_HEADS, C, C), lambda i: (0, 0, 0)),
            pl.BlockSpec((RET_HEADS, C, 1), lambda i: (0, 0, 0)),
            pl.BlockSpec((RET_HEADS, C, 1), lambda i: (0, 0, 0)),
        ],
        out_specs=(pl.BlockSpec((PROJ_TM, DA_WIDTH), lambda i: (i, 0)), t_spec, t_spec,
                   pl.BlockSpec((PROJ_TM, RET_WIDTH), lambda i: (i, 0))),
        scratch_shapes=[
            pltpu.VMEM((RET_HEADS, RET_QK_DIM, RET_V_DIM), F32),
            pltpu.VMEM((D_MODEL, MAIN_COLS), BF16),
            pltpu.VMEM((T_ROWS, D_MODEL), BF16),
            pltpu.VMEM((T_ROWS, D_MODEL), BF16),
        ],
        compiler_params=_params(("arbitrary",)),
        name="inproj_retention",
    )(cd, x2, g, w_in, dec, qdec, kdec)


ACC_ROWS = DA_V_DIM + 16


N_QT = SEQ // ATT_T
N_OFF = N_QT * (N_QT - 1) // 2


def _pipeline3(n_pos, scores, accumulate):
    scores(0, 0)
    scores(1, 1)
    steady = n_pos - 2

    def triple(k, carry):
        t = 3 * k
        accumulate(t, 0)
        scores(t + 2, 2)
        accumulate(t + 1, 1)
        scores(t + 3, 0)
        accumulate(t + 2, 2)
        scores(t + 4, 1)
        return carry

    lax.fori_loop(0, steady // 3, triple, 0)
    t0 = steady // 3 * 3
    rem = steady - t0
    accumulate(t0, 0)
    if rem >= 1:
        scores(t0 + 2, 2)
    accumulate(t0 + 1, 1)
    if rem == 2:
        scores(t0 + 3, 0)
    if rem >= 1:
        accumulate(t0 + 2, 2)
    if rem == 2:
        accumulate(t0 + 3, 0)


def _attn_kernel(slope_ref, jt_ref, it_ref, qt_ref, k_ref, vt_ref, qa_ref, ka1_ref, ka2_ref,
                 dtab_ref, lq1_ref, lk1_ref, lq2_ref, lk2_ref, g_ref, o_ref,
                 m_sc, acc_sc, s0_sc, s1_sc, s2_sc, mx0_sc, mx1_sc, mx2_sc):
    T = ATT_T
    h = pl.program_id(0)
    slope = slope_ref[h]
    qa = qa_ref[0]
    lane = lax.broadcasted_iota(I32, (T, LANES), 1)
    sums_row = (lax.broadcasted_iota(I32, (16, T), 0) == 0).astype(BF16)
    s_bufs = (s0_sc, s1_sc, s2_sc)
    mx_bufs = (mx0_sc, mx1_sc, mx2_sc)

    def scores(j, i, buf, extra):
        kt = k_ref[pl.ds(pl.multiple_of(j * T, T), T), :]
        ks = (jnp.where(lane < DA_HEAD_DIM, kt, ka1_ref[0]),
              jnp.where(lane >= DA_HEAD_DIM, kt, ka2_ref[0]))
        qt = qt_ref[0, i]
        qw = (jnp.concatenate([qt[0:DA_HEAD_DIM], qa], axis=0),
              jnp.concatenate([qa, qt[DA_HEAD_DIM:]], axis=0))
        for mp in range(2):
            s = jnp.dot(ks[mp], qw[mp], preferred_element_type=F32)
            if extra is not None:
                s = s + extra[0]
            s_bufs[buf][mp] = s
            mx_bufs[buf][mp] = jnp.max(s, axis=0, keepdims=True)

    def accumulate(j, i, buf):
        c = slope * lax.convert_element_type((i - j) * T, F32)
        vte = jnp.concatenate([vt_ref[0, j], sums_row], axis=0)
        for mp in range(2):
            m_prev = m_sc[i, mp]
            m_new = jnp.maximum(m_prev, mx_bufs[buf][mp] - c)
            p = jnp.exp(s_bufs[buf][mp] - (m_new + c)).astype(BF16)
            pv = jnp.dot(vte, p, preferred_element_type=F32)
            acc_sc[i, mp] = jnp.exp(m_prev - m_new) * acc_sc[i, mp] + pv
            m_sc[i, mp] = m_new

    m_sc[...] = jnp.full_like(m_sc, -jnp.inf)
    acc_sc[...] = jnp.zeros_like(acc_sc)
    _pipeline3(N_QT,
               lambda pos, buf: scores(pos, pos, buf, dtab_ref),
               lambda pos, buf: accumulate(pos, pos, buf))
    _pipeline3(N_OFF,
               lambda pos, buf: scores(jt_ref[pos], it_ref[pos], buf, None),
               lambda pos, buf: accumulate(jt_ref[pos], it_ref[pos], buf))

    lam = (jnp.exp(jnp.sum(lq1_ref[...] * lk1_ref[...], axis=1, keepdims=True))
           - jnp.exp(jnp.sum(lq2_ref[...] * lk2_ref[...], axis=1, keepdims=True))
           + LAMBDA_INIT)

    def finish(i, carry):
        a1 = acc_sc[i, 0]
        a2 = acc_sc[i, 1]
        ot = (a1[0:DA_V_DIM] / a1[DA_V_DIM:DA_V_DIM + 1]
              - lam * (a2[0:DA_V_DIM] / a2[DA_V_DIM:DA_V_DIM + 1]))
        o = ot.T
        var = jnp.mean(o * o, axis=-1, keepdims=True)
        o = (o * lax.rsqrt(var + EPS) * g_ref[...]) * (1.0 - LAMBDA_INIT)
        o_ref[pl.ds(pl.multiple_of(i * T, T), T), :] = o.astype(BF16)
        return carry

    lax.fori_loop(0, N_QT, finish, 0)


def _attention(proj, qt4, vt4, slopes, qa, ka1, ka2, dtab, lq1, lk1, lq2, lk2, subln_g):
    T = ATT_T
    vec64 = pl.BlockSpec((1, DA_HEAD_DIM), lambda h: (0, 0))
    per_head = lambda a, b: pl.BlockSpec((1, a, b), lambda h: (h, 0, 0))
    slab = lambda shape, imap: pl.BlockSpec(shape, imap, pipeline_mode=pl.Buffered(1))
    smem = pl.BlockSpec(memory_space=pltpu.SMEM)
    it_tab, jt_tab = np.tril_indices(N_QT, -1)
    return pl.pallas_call(
        _attn_kernel,
        out_shape=jax.ShapeDtypeStruct((SEQ, DA_WIDTH), BF16),
        grid=(DA_HEADS,),
        in_specs=[
            smem, smem, smem,
            slab((1, N_QT, LANES, T), lambda h: (h, 0, 0, 0)),
            slab((SEQ, LANES), lambda h: (0, DK_OFF // LANES + h)),
            slab((1, N_QT, LANES, T), lambda h: (h, 0, 0, 0)),
            per_head(DA_HEAD_DIM, T), per_head(T, LANES), per_head(T, LANES), per_head(T, T),
            vec64, vec64, vec64, vec64,
            pl.BlockSpec((1, DA_V_DIM), lambda h: (0, 0)),
        ],
        out_specs=slab((SEQ, LANES), lambda h: (0, h)),
        scratch_shapes=[
            pltpu.VMEM((N_QT, 2, 1, T), F32),
            pltpu.VMEM((N_QT, 2, ACC_ROWS, T), F32),
            pltpu.VMEM((2, T, T), F32),
            pltpu.VMEM((2, T, T), F32),
            pltpu.VMEM((2, T, T), F32),
            pltpu.VMEM((2, 1, T), F32),
            pltpu.VMEM((2, 1, T), F32),
            pltpu.VMEM((2, 1, T), F32),
        ],
        compiler_params=_params(("arbitrary",)),
        name="diff_attention",
    )(slopes, jnp.asarray(jt_tab, I32), jnp.asarray(it_tab, I32), qt4, proj, vt4, qa, ka1, ka2,
      dtab, lq1, lk1, lq2, lk2, subln_g)


def _outproj_router_kernel(x_ref, oda_ref, or_ref, wo32_ref, g_ref, wr_ref, br_ref,
                           h1_ref, xn_ref, ri_ref, rw_ref, cnt_ref, wo_ref):
    @pl.when(pl.program_id(0) == 0)
    def _():
        wo_ref[...] = wo32_ref[...].astype(BF16)

    h1 = (x_ref[...]
          + jnp.dot(oda_ref[...], wo_ref[0:DA_WIDTH, :], preferred_element_type=F32)
          + jnp.dot(or_ref[...], wo_ref[DA_WIDTH:, :], preferred_element_type=F32))
    h1_ref[...] = h1
    var = jnp.mean(h1 * h1, axis=-1, keepdims=True)
    xn = h1 * lax.rsqrt(var + EPS) * g_ref[...]
    bits = lax.bitcast_convert_type(xn.astype(BF16).astype(F32), U32)
    for c in range(X_ROWS):
        lo = bits[:, c * LANES:(c + 1) * LANES] >> 16
        hi = bits[:, (c + X_ROWS) * LANES:(c + X_ROWS + 1) * LANES] & jnp.uint32(0xFFFF0000)
        xn_ref[pl.ds(c, ROUTER_TM, stride=X_ROWS), :] = lo | hi
    x_hi = xn.astype(BF16)
    x_lo = (xn - x_hi.astype(F32)).astype(BF16)
    both = jnp.dot(x_hi, wr_ref[...], preferred_element_type=F32)
    logits = (both[:, :LANES] + both[:, LANES:]
              + jnp.dot(x_lo, wr_ref[:, :LANES], preferred_element_type=F32)) + br_ref[...]
    lane = lax.broadcasted_iota(I32, logits.shape, 1)
    neg = jnp.float32(-jnp.inf)
    big = jnp.int32(1 << 20)
    gl = jnp.where(lane < MOE_GROUPS, logits, neg)
    gmax = jnp.max(gl, axis=1, keepdims=True)
    gidx = jnp.min(jnp.where(gl == gmax, lane, big), axis=1, keepdims=True)
    gsum = jnp.sum(jnp.exp(gl - gmax), axis=1, keepdims=True)
    gp = 1.0 / gsum
    lo = MOE_GROUPS + gidx * MOE_EXPERTS_PER_GROUP
    el = jnp.where((lane >= lo) & (lane < lo + MOE_EXPERTS_PER_GROUP), logits, neg)
    v1 = jnp.max(el, axis=1, keepdims=True)
    i1 = jnp.min(jnp.where(el == v1, lane, big), axis=1, keepdims=True)
    el2 = jnp.where(lane == i1, neg, el)
    v2 = jnp.max(el2, axis=1, keepdims=True)
    i2 = jnp.min(jnp.where(el2 == v2, lane, big), axis=1, keepdims=True)
    t = jnp.exp(v2 - v1)
    w1 = gp / (1.0 + t)
    w2 = gp * t / (1.0 + t)
    ri_ref[...] = jnp.where(lane == 0, i1 - MOE_GROUPS,
                            jnp.where(lane == 1, i2 - MOE_GROUPS, 0))
    rw_ref[...] = jnp.where(lane == 0, w1, jnp.where(lane == 1, w2, 0.0))

    @pl.when(pl.program_id(0) == 0)
    def _():
        cnt_ref[...] = jnp.zeros_like(cnt_ref)

    chosen = (lane == i1 - MOE_GROUPS) | (lane == i2 - MOE_GROUPS)
    cnt_ref[...] += jnp.sum(chosen.astype(F32), axis=0, keepdims=True)


def _outproj_router(x2, o_da, o_r, w_out, g, wr, br):
    tm = ROUTER_TM
    row = lambda w: pl.BlockSpec((tm, w), lambda i: (i, 0))
    full = lambda a, b: pl.BlockSpec((a, b), lambda i: (0, 0))
    return pl.pallas_call(
        _outproj_router_kernel,
        out_shape=(
            jax.ShapeDtypeStruct((SEQ, D_MODEL), F32),
            jax.ShapeDtypeStruct((SEQ * X_ROWS, LANES), U32),
            jax.ShapeDtypeStruct((SEQ, LANES), I32),
            jax.ShapeDtypeStruct((SEQ, LANES), F32),
            jax.ShapeDtypeStruct((8, LANES), F32),
        ),
        grid=(SEQ // tm,),
        in_specs=[row(D_MODEL), row(DA_WIDTH), row(RET_WIDTH),
                  pl.BlockSpec((D_MODEL, D_MODEL), lambda i: (0, 0),
                               pipeline_mode=pl.Buffered(1)),
                  full(1, D_MODEL), full(D_MODEL, 2 * LANES), full(1, LANES)],
        out_specs=(row(D_MODEL), pl.BlockSpec((tm * X_ROWS, LANES), lambda i: (i, 0)),
                   row(LANES), row(LANES), full(8, LANES)),
        scratch_shapes=[pltpu.VMEM((D_MODEL, D_MODEL), BF16)],
        compiler_params=_params(("arbitrary",)),
        name="outproj_router",
    )(x2, o_da, o_r, w_out, g, wr, br)


def _plan_kernel(ri_ref, cnt_ref, dest_ref, used_ref):
    TT = PLAN_T
    lane = lax.broadcasted_iota(I32, (TT, LANES), 1)

    def onehots(t):
        r = ri_ref[pl.ds(pl.multiple_of(t * TT, TT), TT), :]
        return lane == r[:, 0:1], lane == r[:, 1:2]

    counts8 = cnt_ref[...].astype(I32)
    shift = FFN_B.bit_length() - 1
    padded = ((counts8 + (FFN_B - 1)) >> shift) << shift
    lane8 = lax.broadcasted_iota(I32, (8, LANES), 1)
    pad_end = padded
    sh = 1
    while sh < LANES:
        pad_end = pad_end + jnp.where(lane8 >= sh, pltpu.roll(pad_end, sh, axis=1), 0)
        sh *= 2
    pad_start = pad_end - padded

    ltri = (lax.broadcasted_iota(I32, (TT, TT), 0)
            > lax.broadcasted_iota(I32, (TT, TT), 1)).astype(BF16)

    def dest_body(t, carry):
        oh1, oh2 = onehots(t)
        a = (oh1 | oh2).astype(F32)
        base = jnp.dot(ltri, a.astype(BF16), preferred_element_type=F32) + carry
        d1 = jnp.sum(jnp.where(oh1, base, 0.0), axis=1, keepdims=True)
        d2 = jnp.sum(jnp.where(oh2, base, 0.0), axis=1, keepdims=True)
        both = jnp.where(lane == 0, d1, jnp.where(lane == 1, d2, 0.0))
        dest_ref[t] = both.T[0:8, :].astype(I32)
        return carry + jnp.sum(a, axis=0, keepdims=True)

    lax.fori_loop(0, SEQ // TT, dest_body, pad_start[0:1].astype(F32))

    total = jnp.max(pad_end, axis=1, keepdims=True)
    row8 = lax.broadcasted_iota(I32, (8, LANES), 0)
    used_ref[...] = jnp.where(row8 == 0, jnp.broadcast_to(total >> shift, (8, LANES)),
                              jnp.where(row8 == 1, counts8, pad_start))


def _plan(ri, cnt):
    return pl.pallas_call(
        _plan_kernel,
        out_shape=(
            jax.ShapeDtypeStruct((SEQ // PLAN_T, 8, PLAN_T), I32),
            jax.ShapeDtypeStruct((8, LANES), I32),
        ),
        compiler_params=pltpu.CompilerParams(vmem_limit_bytes=VMEM_LIMIT),
        name="route_plan",
    )(ri, cnt)


PAD_BITS = FFN_B.bit_length() - 1


def _pad_fill_copies(e, cnt_ref, pst_ref, zero_sc, xs_hbm, zsem):
    cnt = cnt_ref[e]
    pad = (-cnt) & (FFN_B - 1)
    row = pst_ref[e] + cnt
    out = []
    for bit in reversed(range(PAD_BITS)):
        n = 1 << bit
        start = row + ((pad >> (bit + 1)) << (bit + 1))
        copy = pltpu.make_async_copy(
            zero_sc.at[pl.ds(0, n * X_ROWS)],
            xs_hbm.at[pl.ds(pl.multiple_of(start * X_ROWS, X_ROWS), n * X_ROWS)], zsem)
        out.append(((pad & n) != 0, copy))
    return out


def _unused_block_copies(b, zero_sc, xs_hbm, zsem):
    half = FFN_B // 2 * X_ROWS
    return [pltpu.make_async_copy(
        zero_sc, xs_hbm.at[pl.ds(pl.multiple_of((2 * b + k) * half, half), half)], zsem)
        for k in range(2)]


def _dispatch_kernel(dest_ref, cnt_ref, pst_ref, used_ref, xn_ref, xs_hbm, zero_sc, sem, zsem):
    tm = DISP_TM
    i = pl.program_id(0)

    @pl.when(i == 0)
    def _():
        zero_sc[...] = jnp.zeros_like(zero_sc)

        def fill(e, carry):
            for cond, copy in _pad_fill_copies(e, cnt_ref, pst_ref, zero_sc, xs_hbm, zsem):
                pl.when(cond)(copy.start)
            return carry

        lax.fori_loop(0, MOE_EXPERTS, fill, 0)

        def fill_block(b, carry):
            for copy in _unused_block_copies(b, zero_sc, xs_hbm, zsem):
                copy.start()
            return carry

        lax.fori_loop(used_ref[0], N_BLOCKS, fill_block, 0)

    def issue(it, carry):
        for u in range(DMA_UNROLL):
            r = it * DMA_UNROLL + u
            t = i * tm + r
            src = xn_ref.at[pl.ds(pl.multiple_of(r * X_ROWS, X_ROWS), X_ROWS)]
            for kk in range(2):
                d = pl.multiple_of(dest_ref[kk * SEQ + t] * X_ROWS, X_ROWS)
                pltpu.make_async_copy(src, xs_hbm.at[pl.ds(d, X_ROWS)], sem).start(priority=kk)
        return carry

    lax.fori_loop(0, tm // DMA_UNROLL, issue, 0)
    for _ in range(2):
        pltpu.make_async_copy(xn_ref, xs_hbm.at[pl.ds(0, tm * X_ROWS)], sem).wait()

    @pl.when(i == 0)
    def _():
        def drain(e, carry):
            for cond, copy in _pad_fill_copies(e, cnt_ref, pst_ref, zero_sc, xs_hbm, zsem):
                pl.when(cond)(copy.wait)
            return carry

        lax.fori_loop(0, MOE_EXPERTS, drain, 0)

        def drain_block(b, carry):
            for copy in _unused_block_copies(b, zero_sc, xs_hbm, zsem):
                copy.wait()
            return carry

        lax.fori_loop(used_ref[0], N_BLOCKS, drain_block, 0)


def _dispatch(dest_flat, counts, pad_start, used, xn3):
    tm = DISP_TM
    return pl.pallas_call(
        _dispatch_kernel,
        out_shape=jax.ShapeDtypeStruct((N_BUF * X_ROWS, LANES), U32),
        grid_spec=pltpu.PrefetchScalarGridSpec(
            num_scalar_prefetch=4,
            grid=(SEQ // tm,),
            in_specs=[pl.BlockSpec((tm * X_ROWS, LANES), lambda i, d, c, p, u: (i, 0))],
            out_specs=pl.BlockSpec(memory_space=pl.ANY),
            scratch_shapes=[
                pltpu.VMEM((FFN_B // 2 * X_ROWS, LANES), U32),
                pltpu.SemaphoreType.DMA(()),
                pltpu.SemaphoreType.DMA(()),
            ],
        ),
        compiler_params=_params(("arbitrary",)),
        name="moe_dispatch",
    )(dest_flat, counts, pad_start, used, xn3)


BLOCK_COPY_PRIORITY = 1


def _ffn_kernel(cnt_ref, pst_ref, used_ref, xs_hbm, wg_ref, wu_ref, wd_ref, y_hbm,
                xbuf, ybuf, zero_sc, wg_bf, wu_bf, wd_bf, sem_in, sem_out, zsem):
    B = FFN_B
    e = pl.program_id(0)
    n = (cnt_ref[e] + (B - 1)) >> PAD_BITS
    s0 = pst_ref[e] >> PAD_BITS

    def rows(blk, per_token):
        size = B * per_token
        return pl.ds(pl.multiple_of(blk * size, size), size)

    def fetch(blk, slot):
        return pltpu.make_async_copy(xs_hbm.at[rows(blk, X_ROWS)], xbuf.at[slot],
                                     sem_in.at[slot])

    def flush(blk, slot):
        return pltpu.make_async_copy(ybuf.at[slot], y_hbm.at[rows(blk, ROW_TILE)],
                                     sem_out.at[slot])

    used = used_ref[0]

    @pl.when(e == 0)
    def _():
        fetch(0, 0).start(priority=BLOCK_COPY_PRIORITY)

    @pl.when(n > 0)
    def _():
        wg_bf[...] = wg_ref[0].astype(BF16)
        wu_bf[...] = wu_ref[0].astype(BF16)
        wd_bf[...] = wd_ref[0].astype(BF16)

    def body(j, carry):
        blk = s0 + j
        slot = blk % 2

        @pl.when(blk + 1 < used)
        def _():
            fetch(blk + 1, 1 - slot).start(priority=BLOCK_COPY_PRIORITY)

        fetch(blk, slot).wait()

        @pl.when(blk >= 2)
        def _():
            flush(blk - 2, slot).wait()

        words = [xbuf[slot, pl.ds(c, B, stride=X_ROWS), :] for c in range(X_ROWS)]
        lo = [lax.bitcast_convert_type(w << 16, F32) for w in words]
        hi = [lax.bitcast_convert_type(w & jnp.uint32(0xFFFF0000), F32) for w in words]
        x = jnp.concatenate(lo + hi, axis=1).astype(BF16)
        hg = jnp.dot(x, wg_bf[...], preferred_element_type=F32)
        hu = jnp.dot(x, wu_bf[...], preferred_element_type=F32)
        hh = ((hg / (1.0 + jnp.exp(-hg))) * hu).astype(BF16)
        y = jnp.dot(hh, wd_bf[...], preferred_element_type=F32)
        for c in range(ROW_TILE):
            ybuf[slot, pl.ds(c, B, stride=ROW_TILE), :] = y[:, c * LANES:(c + 1) * LANES]
        flush(blk, slot).start(priority=BLOCK_COPY_PRIORITY)
        return carry

    lax.fori_loop(0, n, body, 0)

    @pl.when(e == MOE_EXPERTS - 1)
    def _():
        flush(used - 1, (used + 1) % 2).wait()
        flush(used - 2, used % 2).wait()
        zero_sc[...] = jnp.zeros_like(zero_sc)

        def fill(b, carry):
            pltpu.make_async_copy(zero_sc, y_hbm.at[rows(b, ROW_TILE)], zsem).start()
            return carry

        def drain(b, carry):
            pltpu.make_async_copy(zero_sc, y_hbm.at[rows(b, ROW_TILE)], zsem).wait()
            return carry

        lax.fori_loop(used_ref[0], N_BLOCKS, fill, 0)
        lax.fori_loop(used_ref[0], N_BLOCKS, drain, 0)


def _ffn(counts, pad_start, used, xs, w_gate, w_up, w_down):
    B = FFN_B
    wspec = lambda a, c: pl.BlockSpec((1, a, c), lambda e, cnt, pst, used: (e, 0, 0))
    return pl.pallas_call(
        _ffn_kernel,
        out_shape=jax.ShapeDtypeStruct((N_BUF * ROW_TILE, LANES), F32),
        grid_spec=pltpu.PrefetchScalarGridSpec(
            num_scalar_prefetch=3,
            grid=(MOE_EXPERTS,),
            in_specs=[
                pl.BlockSpec(memory_space=pl.ANY),
                wspec(D_MODEL, MOE_HIDDEN),
                wspec(D_MODEL, MOE_HIDDEN),
                wspec(MOE_HIDDEN, D_MODEL),
            ],
            out_specs=pl.BlockSpec(memory_space=pl.ANY),
            scratch_shapes=[
                pltpu.VMEM((2, B * X_ROWS, LANES), U32),
                pltpu.VMEM((2, B * ROW_TILE, LANES), F32),
                pltpu.VMEM((B * ROW_TILE, LANES), F32),
                pltpu.VMEM((D_MODEL, MOE_HIDDEN), BF16),
                pltpu.VMEM((D_MODEL, MOE_HIDDEN), BF16),
                pltpu.VMEM((MOE_HIDDEN, D_MODEL), BF16),
                pltpu.SemaphoreType.DMA((2,)),
                pltpu.SemaphoreType.DMA((2,)),
                pltpu.SemaphoreType.DMA(()),
            ],
        ),
        compiler_params=_params(("arbitrary",)),
        name="expert_ffn",
    )(counts, pad_start, used, xs, w_gate, w_up, w_down)


def _combine_kernel(dest_ref, h1_ref, rw_ref, g_ref, y_hbm, o_ref, ybuf, sem):
    tm = COMB_TM
    i = pl.program_id(0)

    def gather(tile, slot):
        def issue(it, carry):
            for u in range(DMA_UNROLL):
                r = it * DMA_UNROLL + u
                t = tile * tm + r
                for kk in range(2):
                    d = pl.multiple_of(dest_ref[kk * SEQ + t] * ROW_TILE, ROW_TILE)
                    pltpu.make_async_copy(
                        y_hbm.at[pl.ds(d, ROW_TILE)],
                        ybuf.at[slot, kk, pl.ds(pl.multiple_of(r * ROW_TILE, ROW_TILE), ROW_TILE)],
                        sem.at[slot, kk]).start(priority=kk)
            return carry

        lax.fori_loop(0, tm // DMA_UNROLL, issue, 0)

    @pl.when(i == 0)
    def _():
        gather(0, 0)

    @pl.when(i + 1 < pl.num_programs(0))
    def _():
        gather(i + 1, (i + 1) % 2)

    slot = i % 2
    for kk in range(2):
        pltpu.make_async_copy(y_hbm.at[pl.ds(0, tm * ROW_TILE)], ybuf.at[slot, kk],
                              sem.at[slot, kk]).wait()
    w = rw_ref[...]
    ys = [jnp.concatenate([ybuf[slot, kk, pl.ds(c, tm, stride=ROW_TILE), :]
                           for c in range(ROW_TILE)], axis=1) for kk in range(2)]
    h = h1_ref[...] + w[:, 0:1] * ys[0] + w[:, 1:2] * ys[1]
    var = jnp.mean(h * h, axis=-1, keepdims=True)
    o_ref[...] = h * lax.rsqrt(var + EPS) * g_ref[...]


def _combine(dest_flat, h1, rw, g, y):
    tm = COMB_TM
    return pl.pallas_call(
        _combine_kernel,
        out_shape=jax.ShapeDtypeStruct((SEQ, D_MODEL), F32),
        grid_spec=pltpu.PrefetchScalarGridSpec(
            num_scalar_prefetch=1,
            grid=(SEQ // tm,),
            in_specs=[
                pl.BlockSpec((tm, D_MODEL), lambda i, d: (i, 0)),
                pl.BlockSpec((tm, LANES), lambda i, d: (i, 0)),
                pl.BlockSpec((1, D_MODEL), lambda i, d: (0, 0)),
                pl.BlockSpec(memory_space=pl.ANY),
            ],
            out_specs=pl.BlockSpec((tm, D_MODEL), lambda i, d: (i, 0)),
            scratch_shapes=[
                pltpu.VMEM((2, 2, tm * ROW_TILE, LANES), F32),
                pltpu.SemaphoreType.DMA((2, 2)),
            ],
        ),
        compiler_params=_params(("arbitrary",)),
        name="moe_combine",
    )(dest_flat, h1, rw, g, y)


def _attention_tables():
    T = ATT_T
    f32 = np.float32
    slopes = np.exp2(-ALIBI_MAX * np.arange(1, DA_HEADS + 1, dtype=f32) / DA_HEADS).astype(f32)
    r = np.arange(T)
    hi = ((r // CHUNK) * CHUNK).astype(f32)
    lo = (r % CHUNK).astype(f32)
    sl = slopes[:, None]
    one_h = np.ones((DA_HEADS, T), f32)
    q_rows = np.stack([one_h, one_h, -sl * hi[None], -sl * lo[None]], axis=1)
    k_cols = np.stack([sl * hi[None], sl * lo[None], one_h, one_h], axis=-1)
    qa = np.zeros((DA_HEADS, DA_HEAD_DIM, T), f32)
    qa[:, 0:4, :] = q_rows
    ka1 = np.zeros((DA_HEADS, T, LANES), f32)
    ka1[:, :, DA_HEAD_DIM:DA_HEAD_DIM + 4] = k_cols
    ka2 = np.zeros((DA_HEADS, T, LANES), f32)
    ka2[:, :, 0:4] = k_cols
    rel = (r[:, None] - r[None, :]).astype(f32)
    allowed = (r[:, None] // CHUNK) <= (r[None, :] // CHUNK)
    fix = np.where(rel > 0, -2.0 * slopes[:, None, None] * rel[None], 0.0).astype(f32)
    dtab = np.where(allowed[None], fix, -np.inf).astype(f32)
    return (jnp.asarray(slopes), jnp.asarray(qa, BF16), jnp.asarray(ka1, BF16),
            jnp.asarray(ka2, BF16), jnp.asarray(dtab))


def _retention_tables():
    C = RET_C
    f32 = np.float32
    log_gamma = np.log1p(-np.exp2(-5.0 - np.arange(RET_HEADS, dtype=f32))).astype(f32)
    pos = np.arange(C, dtype=f32)
    rel = pos[:, None] - pos[None, :]
    dec = np.where(rel >= 0, np.exp(log_gamma[:, None, None] * np.maximum(rel, 0.0)), 0.0)
    qdec = np.exp(log_gamma[:, None] * (pos + 1.0)[None, :])[:, :, None]
    kdec = np.exp(log_gamma[:, None] * (C - 1 - pos)[None, :])[:, :, None]
    cd = np.exp(log_gamma * C)
    return tuple(jnp.asarray(t, F32) for t in (cd, dec, qdec, kdec))


def kernel(x, attn_norm_g, w_in, da_lambda_q1, da_lambda_k1, da_lambda_q2, da_lambda_k2,
           da_subln_g, w_out, ffn_norm_g, router_group_w, router_group_b, router_expert_w,
           router_expert_b, expert_w_gate, expert_w_up, expert_w_down, final_norm_g):
    B, S, D = x.shape
    assert (B, S, D) == (1, SEQ, D_MODEL)
    x2 = x.reshape(S, D)

    cd, dec, qdec, kdec = _retention_tables()
    slopes, qa, ka1, ka2, dtab = _attention_tables()
    k_da, qt4, vt4, o_r = _inproj(x2, attn_norm_g[0][None, :], w_in[0], cd, dec, qdec, kdec)

    o_da = _attention(k_da, qt4, vt4, slopes, qa, ka1, ka2, dtab, da_lambda_q1, da_lambda_k1,
                      da_lambda_q2, da_lambda_k2, da_subln_g)

    wr = jnp.zeros((D, LANES), F32)
    wr = wr.at[:, :MOE_GROUPS].set(router_group_w[0])
    wr = wr.at[:, MOE_GROUPS:MOE_GROUPS + MOE_EXPERTS].set(router_expert_w[0])
    br = jnp.zeros((1, LANES), F32)
    br = br.at[0, :MOE_GROUPS].set(router_group_b[0])
    br = br.at[0, MOE_GROUPS:MOE_GROUPS + MOE_EXPERTS].set(router_expert_b[0])
    wr_hi = wr.astype(BF16)
    wr_lo = (wr - wr_hi.astype(F32)).astype(BF16)
    h1, xn, ri, rw, cnt = _outproj_router(x2, o_da, o_r, w_out[0],
                                          ffn_norm_g[0][None, :],
                                          jnp.concatenate([wr_hi, wr_lo], axis=1), br)

    dest, meta = _plan(ri, cnt)
    dest_flat = dest[:, 0:2, :].transpose(1, 0, 2).reshape(N_ASSIGN)
    used1 = meta[0, :1]
    counts = meta[1, :MOE_EXPERTS]
    pad_start = meta[2, :MOE_EXPERTS]
    xs = _dispatch(dest_flat, counts, pad_start, used1, xn)

    y = _ffn(counts, pad_start, used1, xs, expert_w_gate[0], expert_w_up[0], expert_w_down[0])
    out = _combine(dest_flat, h1, rw, final_norm_g[None, :], y)
    return out.reshape(B, S, D)
```

```python
import math

import jax
import jax.numpy as jnp
import numpy as np
from jax import lax
from jax.experimental import pallas as pl
from jax.experimental.pallas import tpu as pltpu

F32 = jnp.float32
BF16 = jnp.bfloat16
I32 = jnp.int32
U32 = jnp.uint32

D_MODEL = 1024
SEQ = 16384
CHUNK = 64
EPS = 1e-6

DA_HEADS = 4
DA_HEAD_DIM = 64
DA_V_DIM = 128
DA_WIDTH = 512
ALIBI_MAX = 8.0
RET_HEADS = 4
RET_QK_DIM = 64
RET_V_DIM = 128
RET_WIDTH = 512
W_IN_COLS = 3072
T_ROWS = 512
MAIN_COLS = 2048
DK_OFF = 0
RQ_OFF = 512
RK_OFF = 768
RV_OFF = 1024
RG_OFF = 1536

MOE_GROUPS = 4
MOE_EXPERTS_PER_GROUP = 8
MOE_EXPERTS = 32
MOE_HIDDEN = 512
LAMBDA_INIT = 0.8 - 0.6 * math.exp(-0.3 * 0)

LANES = 128
ROW_TILE = 8
X_ROWS = 4
VMEM_LIMIT = 56 * 1024 * 1024

PROJ_TM = 512
ROUTER_TM = 1024
ATT_T = 512
RET_C = 256
PLAN_T = 512
FFN_B = 256
N_ASSIGN = 2 * SEQ
N_BLOCKS = N_ASSIGN // FFN_B + MOE_EXPERTS
N_BUF = N_BLOCKS * FFN_B
COMB_TM = 512
DISP_TM = 512
DMA_UNROLL = 16


def _params(sem):
    return pltpu.CompilerParams(dimension_semantics=sem, vmem_limit_bytes=VMEM_LIMIT)


def _retention_block(q_all, k_all, v_all, g_all, cd_ref, dec_ref, qdec_ref, kdec_ref, st_sc):
    outs = []
    for h in range(RET_HEADS):
        qk = slice(h * RET_QK_DIM, (h + 1) * RET_QK_DIM)
        vv = slice(h * RET_V_DIM, (h + 1) * RET_V_DIM)
        q = q_all[:, qk]
        k = k_all[:, qk]
        v = v_all[:, vv]
        g = g_all[:, vv]
        s = lax.dot_general(q, k, (((1,), (1,)), ((), ())),
                            preferred_element_type=F32) * dec_ref[h]
        intra = jnp.dot(s.astype(BF16), v, preferred_element_type=F32)
        st = st_sc[h]
        cross = jnp.dot(q, st.astype(BF16), preferred_element_type=F32) * qdec_ref[h]
        kd = (k.astype(F32) * kdec_ref[h]).astype(BF16)
        st_sc[h] = st * cd_ref[h] + lax.dot_general(kd, v, (((0,), (0,)), ((), ())),
                                                    preferred_element_type=F32)
        o = intra + cross
        o = o * lax.rsqrt(jnp.mean(o * o, axis=-1, keepdims=True) + EPS)
        outs.append(((g / (1.0 + jnp.exp(-g))) * o).astype(BF16))
    return outs


def _inproj_kernel(cd_ref, x_ref, g_ref, win_ref, dec_ref, qdec_ref, kdec_ref,
                   k_ref, qt_ref, vt_ref, or_ref, st_sc, w_ref, wq_ref, wv_ref):
    @pl.when(pl.program_id(0) == 0)
    def _():
        st_sc[...] = jnp.zeros_like(st_sc)
        cols = DA_WIDTH
        w_ref[:, DK_OFF:RQ_OFF] = win_ref[:, cols:2 * cols].astype(BF16)
        w_ref[:, RQ_OFF:RK_OFF] = win_ref[:, 3 * cols:3 * cols + 256].astype(BF16)
        w_ref[:, RK_OFF:RV_OFF] = (win_ref[:, 3 * cols + 256:4 * cols]
                                   * (RET_QK_DIM ** -0.5)).astype(BF16)
        w_ref[:, RV_OFF:MAIN_COLS] = win_ref[:, 4 * cols:6 * cols].astype(BF16)
        step = 256
        for r in range(D_MODEL // step):
            rows = slice(r * step, (r + 1) * step)
            wq_ref[:, rows] = (win_ref[rows, 0:cols] * (DA_HEAD_DIM ** -0.5)).T.astype(BF16)
            wv_ref[:, rows] = win_ref[rows, 2 * cols:3 * cols].T.astype(BF16)

    x = x_ref[...]
    var = jnp.mean(x * x, axis=-1, keepdims=True)
    xn = (x * lax.rsqrt(var + EPS) * g_ref[...]).astype(BF16)

    def proj(lo, hi):
        return jnp.dot(xn, w_ref[:, lo:hi], preferred_element_type=F32)

    k_ref[...] = proj(DK_OFF, DK_OFF + DA_WIDTH).astype(BF16)
    nt = (((1,), (1,)), ((), ()))
    qt = lax.dot_general(wq_ref[...], xn, nt, preferred_element_type=F32)
    qt_ref[...] = qt.astype(BF16).reshape(DA_HEADS, 1, 2 * DA_HEAD_DIM, PROJ_TM)
    vt = lax.dot_general(wv_ref[...], xn, nt, preferred_element_type=F32)
    vt_ref[...] = vt.astype(BF16).reshape(DA_HEADS, 1, DA_V_DIM, PROJ_TM)

    rq = proj(RQ_OFF, RK_OFF).astype(BF16)
    rk = proj(RK_OFF, RV_OFF).astype(BF16)
    rv = proj(RV_OFF, RG_OFF).astype(BF16)
    rg = proj(RG_OFF, MAIN_COLS)
    for blk in range(PROJ_TM // RET_C):
        rows = slice(blk * RET_C, (blk + 1) * RET_C)
        outs = _retention_block(rq[rows], rk[rows], rv[rows], rg[rows],
                                cd_ref, dec_ref, qdec_ref, kdec_ref, st_sc)
        for h in range(RET_HEADS):
            or_ref[rows, h * RET_V_DIM:(h + 1) * RET_V_DIM] = outs[h]


def _inproj(x2, g, w_in, cd, dec, qdec, kdec):
    C = RET_C
    t_shape = jax.ShapeDtypeStruct((DA_HEADS, SEQ // PROJ_TM, LANES, PROJ_TM), BF16)
    t_spec = pl.BlockSpec((DA_HEADS, 1, LANES, PROJ_TM), lambda i: (0, i, 0, 0))
    return pl.pallas_call(
        _inproj_kernel,
        out_shape=(jax.ShapeDtypeStruct((SEQ, DA_WIDTH), BF16), t_shape, t_shape,
                   jax.ShapeDtypeStruct((SEQ, RET_WIDTH), BF16)),
        grid=(SEQ // PROJ_TM,),
        in_specs=[
            pl.BlockSpec(memory_space=pltpu.SMEM),
            pl.BlockSpec((PROJ_TM, D_MODEL), lambda i: (i, 0)),
            pl.BlockSpec((1, D_MODEL), lambda i: (0, 0)),
            pl.BlockSpec((D_MODEL, W_IN_COLS), lambda i: (0, 0), pipeline_mode=pl.Buffered(1)),
            pl.BlockSpec((RET_HEADS, C, C), lambda i: (0, 0, 0)),
            pl.BlockSpec((RET_HEADS, C, 1), lambda i: (0, 0, 0)),
            pl.BlockSpec((RET_HEADS, C, 1), lambda i: (0, 0, 0)),
        ],
        out_specs=(pl.BlockSpec((PROJ_TM, DA_WIDTH), lambda i: (i, 0)), t_spec, t_spec,
                   pl.BlockSpec((PROJ_TM, RET_WIDTH), lambda i: (i, 0))),
        scratch_shapes=[
            pltpu.VMEM((RET_HEADS, RET_QK_DIM, RET_V_DIM), F32),
            pltpu.VMEM((D_MODEL, MAIN_COLS), BF16),
            pltpu.VMEM((T_ROWS, D_MODEL), BF16),
            pltpu.VMEM((T_ROWS, D_MODEL), BF16),
        ],
        compiler_params=_params(("arbitrary",)),
        name="inproj_retention",
    )(cd, x2, g, w_in, dec, qdec, kdec)


ACC_ROWS = DA_V_DIM + 16


N_QT = SEQ // ATT_T
N_OFF = N_QT * (N_QT - 1) // 2


def _pipeline3(n_pos, scores, accumulate):
    scores(0, 0)
    scores(1, 1)
    steady = n_pos - 2

    def triple(k, carry):
        t = 3 * k
        accumulate(t, 0)
        scores(t + 2, 2)
        accumulate(t + 1, 1)
        scores(t + 3, 0)
        accumulate(t + 2, 2)
        scores(t + 4, 1)
        return carry

    lax.fori_loop(0, steady // 3, triple, 0)
    t0 = steady // 3 * 3
    rem = steady - t0
    accumulate(t0, 0)
    if rem >= 1:
        scores(t0 + 2, 2)
    accumulate(t0 + 1, 1)
    if rem == 2:
        scores(t0 + 3, 0)
    if rem >= 1:
        accumulate(t0 + 2, 2)
    if rem == 2:
        accumulate(t0 + 3, 0)


def _attn_kernel(slope_ref, jt_ref, it_ref, qt_ref, k_ref, vt_ref, qa_ref, ka1_ref, ka2_ref,
                 dtab_ref, lq1_ref, lk1_ref, lq2_ref, lk2_ref, g_ref, o_ref,
                 m_sc, acc_sc, s0_sc, s1_sc, s2_sc, mx0_sc, mx1_sc, mx2_sc):
    T = ATT_T
    h = pl.program_id(0)
    slope = slope_ref[h]
    qa = qa_ref[0]
    lane = lax.broadcasted_iota(I32, (T, LANES), 1)
    sums_row = (lax.broadcasted_iota(I32, (16, T), 0) == 0).astype(BF16)
    s_bufs = (s0_sc, s1_sc, s2_sc)
    mx_bufs = (mx0_sc, mx1_sc, mx2_sc)

    def scores(j, i, buf, extra):
        kt = k_ref[pl.ds(pl.multiple_of(j * T, T), T), :]
        ks = (jnp.where(lane < DA_HEAD_DIM, kt, ka1_ref[0]),
              jnp.where(lane >= DA_HEAD_DIM, kt, ka2_ref[0]))
        qt = qt_ref[0, i]
        qw = (jnp.concatenate([qt[0:DA_HEAD_DIM], qa], axis=0),
              jnp.concatenate([qa, qt[DA_HEAD_DIM:]], axis=0))
        for mp in range(2):
            s = jnp.dot(ks[mp], qw[mp], preferred_element_type=F32)
            if extra is not None:
                s = s + extra[0]
            s_bufs[buf][mp] = s
            mx_bufs[buf][mp] = jnp.max(s, axis=0, keepdims=True)

    def accumulate(j, i, buf):
        c = slope * lax.convert_element_type((i - j) * T, F32)
        vte = jnp.concatenate([vt_ref[0, j], sums_row], axis=0)
        for mp in range(2):
            m_prev = m_sc[i, mp]
            m_new = jnp.maximum(m_prev, mx_bufs[buf][mp] - c)
            p = jnp.exp(s_bufs[buf][mp] - (m_new + c)).astype(BF16)
            pv = jnp.dot(vte, p, preferred_element_type=F32)
            acc_sc[i, mp] = jnp.exp(m_prev - m_new) * acc_sc[i, mp] + pv
            m_sc[i, mp] = m_new

    m_sc[...] = jnp.full_like(m_sc, -jnp.inf)
    acc_sc[...] = jnp.zeros_like(acc_sc)
    _pipeline3(N_QT,
               lambda pos, buf: scores(pos, pos, buf, dtab_ref),
               lambda pos, buf: accumulate(pos, pos, buf))
    _pipeline3(N_OFF,
               lambda pos, buf: scores(jt_ref[pos], it_ref[pos], buf, None),
               lambda pos, buf: accumulate(jt_ref[pos], it_ref[pos], buf))

    lam = (jnp.exp(jnp.sum(lq1_ref[...] * lk1_ref[...], axis=1, keepdims=True))
           - jnp.exp(jnp.sum(lq2_ref[...] * lk2_ref[...], axis=1, keepdims=True))
           + LAMBDA_INIT)

    def finish(i, carry):
        a1 = acc_sc[i, 0]
        a2 = acc_sc[i, 1]
        ot = (a1[0:DA_V_DIM] / a1[DA_V_DIM:DA_V_DIM + 1]
              - lam * (a2[0:DA_V_DIM] / a2[DA_V_DIM:DA_V_DIM + 1]))
        o = ot.T
        var = jnp.mean(o * o, axis=-1, keepdims=True)
        o = (o * lax.rsqrt(var + EPS) * g_ref[...]) * (1.0 - LAMBDA_INIT)
        o_ref[pl.ds(pl.multiple_of(i * T, T), T), :] = o.astype(BF16)
        return carry

    lax.fori_loop(0, N_QT, finish, 0)


def _attention(proj, qt4, vt4, slopes, qa, ka1, ka2, dtab, lq1, lk1, lq2, lk2, subln_g):
    T = ATT_T
    vec64 = pl.BlockSpec((1, DA_HEAD_DIM), lambda h: (0, 0))
    per_head = lambda a, b: pl.BlockSpec((1, a, b), lambda h: (h, 0, 0))
    slab = lambda shape, imap: pl.BlockSpec(shape, imap, pipeline_mode=pl.Buffered(1))
    smem = pl.BlockSpec(memory_space=pltpu.SMEM)
    it_tab, jt_tab = np.tril_indices(N_QT, -1)
    return pl.pallas_call(
        _attn_kernel,
        out_shape=jax.ShapeDtypeStruct((SEQ, DA_WIDTH), BF16),
        grid=(DA_HEADS,),
        in_specs=[
            smem, smem, smem,
            slab((1, N_QT, LANES, T), lambda h: (h, 0, 0, 0)),
            slab((SEQ, LANES), lambda h: (0, DK_OFF // LANES + h)),
            slab((1, N_QT, LANES, T), lambda h: (h, 0, 0, 0)),
            per_head(DA_HEAD_DIM, T), per_head(T, LANES), per_head(T, LANES), per_head(T, T),
            vec64, vec64, vec64, vec64,
            pl.BlockSpec((1, DA_V_DIM), lambda h: (0, 0)),
        ],
        out_specs=slab((SEQ, LANES), lambda h: (0, h)),
        scratch_shapes=[
            pltpu.VMEM((N_QT, 2, 1, T), F32),
            pltpu.VMEM((N_QT, 2, ACC_ROWS, T), F32),
            pltpu.VMEM((2, T, T), F32),
            pltpu.VMEM((2, T, T), F32),
            pltpu.VMEM((2, T, T), F32),
            pltpu.VMEM((2, 1, T), F32),
            pltpu.VMEM((2, 1, T), F32),
            pltpu.VMEM((2, 1, T), F32),
        ],
        compiler_params=_params(("arbitrary",)),
        name="diff_attention",
    )(slopes, jnp.asarray(jt_tab, I32), jnp.asarray(it_tab, I32), qt4, proj, vt4, qa, ka1, ka2,
      dtab, lq1, lk1, lq2, lk2, subln_g)


def _outproj_router_kernel(x_ref, oda_ref, or_ref, wo32_ref, g_ref, wr_ref, br_ref,
                           h1_ref, xn_ref, ri_ref, rw_ref, cnt_ref, wo_ref):
    @pl.when(pl.program_id(0) == 0)
    def _():
        wo_ref[...] = wo32_ref[...].astype(BF16)

    h1 = (x_ref[...]
          + jnp.dot(oda_ref[...], wo_ref[0:DA_WIDTH, :], preferred_element_type=F32)
          + jnp.dot(or_ref[...], wo_ref[DA_WIDTH:, :], preferred_element_type=F32))
    h1_ref[...] = h1
    var = jnp.mean(h1 * h1, axis=-1, keepdims=True)
    xn = h1 * lax.rsqrt(var + EPS) * g_ref[...]
    bits = lax.bitcast_convert_type(xn.astype(BF16).astype(F32), U32)
    for c in range(X_ROWS):
        lo = bits[:, c * LANES:(c + 1) * LANES] >> 16
        hi = bits[:, (c + X_ROWS) * LANES:(c + X_ROWS + 1) * LANES] & jnp.uint32(0xFFFF0000)
        xn_ref[pl.ds(c, ROUTER_TM, stride=X_ROWS), :] = lo | hi
    x_hi = xn.astype(BF16)
    x_lo = (xn - x_hi.astype(F32)).astype(BF16)
    both = jnp.dot(x_hi, wr_ref[...], preferred_element_type=F32)
    logits = (both[:, :LANES] + both[:, LANES:]
              + jnp.dot(x_lo, wr_ref[:, :LANES], preferred_element_type=F32)) + br_ref[...]
    lane = lax.broadcasted_iota(I32, logits.shape, 1)
    neg = jnp.float32(-jnp.inf)
    big = jnp.int32(1 << 20)
    gl = jnp.where(lane < MOE_GROUPS, logits, neg)
    gmax = jnp.max(gl, axis=1, keepdims=True)
    gidx = jnp.min(jnp.where(gl == gmax, lane, big), axis=1, keepdims=True)
    gsum = jnp.sum(jnp.exp(gl - gmax), axis=1, keepdims=True)
    gp = 1.0 / gsum
    lo = MOE_GROUPS + gidx * MOE_EXPERTS_PER_GROUP
    el = jnp.where((lane >= lo) & (lane < lo + MOE_EXPERTS_PER_GROUP), logits, neg)
    v1 = jnp.max(el, axis=1, keepdims=True)
    i1 = jnp.min(jnp.where(el == v1, lane, big), axis=1, keepdims=True)
    el2 = jnp.where(lane == i1, neg, el)
    v2 = jnp.max(el2, axis=1, keepdims=True)
    i2 = jnp.min(jnp.where(el2 == v2, lane, big), axis=1, keepdims=True)
    t = jnp.exp(v2 - v1)
    w1 = gp / (1.0 + t)
    w2 = gp * t / (1.0 + t)
    ri_ref[...] = jnp.where(lane == 0, i1 - MOE_GROUPS,
                            jnp.where(lane == 1, i2 - MOE_GROUPS, 0))
    rw_ref[...] = jnp.where(lane == 0, w1, jnp.where(lane == 1, w2, 0.0))

    @pl.when(pl.program_id(0) == 0)
    def _():
        cnt_ref[...] = jnp.zeros_like(cnt_ref)

    chosen = (lane == i1 - MOE_GROUPS) | (lane == i2 - MOE_GROUPS)
    cnt_ref[...] += jnp.sum(chosen.astype(F32), axis=0, keepdims=True)


def _outproj_router(x2, o_da, o_r, w_out, g, wr, br):
    tm = ROUTER_TM
    row = lambda w: pl.BlockSpec((tm, w), lambda i: (i, 0))
    full = lambda a, b: pl.BlockSpec((a, b), lambda i: (0, 0))
    return pl.pallas_call(
        _outproj_router_kernel,
        out_shape=(
            jax.ShapeDtypeStruct((SEQ, D_MODEL), F32),
            jax.ShapeDtypeStruct((SEQ * X_ROWS, LANES), U32),
            jax.ShapeDtypeStruct((SEQ, LANES), I32),
            jax.ShapeDtypeStruct((SEQ, LANES), F32),
            jax.ShapeDtypeStruct((8, LANES), F32),
        ),
        grid=(SEQ // tm,),
        in_specs=[row(D_MODEL), row(DA_WIDTH), row(RET_WIDTH),
                  pl.BlockSpec((D_MODEL, D_MODEL), lambda i: (0, 0),
                               pipeline_mode=pl.Buffered(1)),
                  full(1, D_MODEL), full(D_MODEL, 2 * LANES), full(1, LANES)],
        out_specs=(row(D_MODEL), pl.BlockSpec((tm * X_ROWS, LANES), lambda i: (i, 0)),
                   row(LANES), row(LANES), full(8, LANES)),
        scratch_shapes=[pltpu.VMEM((D_MODEL, D_MODEL), BF16)],
        compiler_params=_params(("arbitrary",)),
        name="outproj_router",
    )(x2, o_da, o_r, w_out, g, wr, br)


def _plan_kernel(ri_ref, cnt_ref, dest_ref, used_ref):
    TT = PLAN_T
    lane = lax.broadcasted_iota(I32, (TT, LANES), 1)

    def onehots(t):
        r = ri_ref[pl.ds(pl.multiple_of(t * TT, TT), TT), :]
        return lane == r[:, 0:1], lane == r[:, 1:2]

    counts8 = cnt_ref[...].astype(I32)
    shift = FFN_B.bit_length() - 1
    padded = ((counts8 + (FFN_B - 1)) >> shift) << shift
    lane8 = lax.broadcasted_iota(I32, (8, LANES), 1)
    pad_end = padded
    sh = 1
    while sh < LANES:
        pad_end = pad_end + jnp.where(lane8 >= sh, pltpu.roll(pad_end, sh, axis=1), 0)
        sh *= 2
    pad_start = pad_end - padded

    ltri = (lax.broadcasted_iota(I32, (TT, TT), 0)
            > lax.broadcasted_iota(I32, (TT, TT), 1)).astype(BF16)

    def dest_body(t, carry):
        oh1, oh2 = onehots(t)
        a = (oh1 | oh2).astype(F32)
        base = jnp.dot(ltri, a.astype(BF16), preferred_element_type=F32) + carry
        d1 = jnp.sum(jnp.where(oh1, base, 0.0), axis=1, keepdims=True)
        d2 = jnp.sum(jnp.where(oh2, base, 0.0), axis=1, keepdims=True)
        both = jnp.where(lane == 0, d1, jnp.where(lane == 1, d2, 0.0))
        dest_ref[t] = both.T[0:8, :].astype(I32)
        return carry + jnp.sum(a, axis=0, keepdims=True)

    lax.fori_loop(0, SEQ // TT, dest_body, pad_start[0:1].astype(F32))

    total = jnp.max(pad_end, axis=1, keepdims=True)
    row8 = lax.broadcasted_iota(I32, (8, LANES), 0)
    used_ref[...] = jnp.where(row8 == 0, jnp.broadcast_to(total >> shift, (8, LANES)),
                              jnp.where(row8 == 1, counts8, pad_start))


def _plan(ri, cnt):
    return pl.pallas_call(
        _plan_kernel,
        out_shape=(
            jax.ShapeDtypeStruct((SEQ // PLAN_T, 8, PLAN_T), I32),
            jax.ShapeDtypeStruct((8, LANES), I32),
        ),
        compiler_params=pltpu.CompilerParams(vmem_limit_bytes=VMEM_LIMIT),
        name="route_plan",
    )(ri, cnt)


PAD_BITS = FFN_B.bit_length() - 1


def _pad_fill_copies(e, cnt_ref, pst_ref, zero_sc, xs_hbm, zsem):
    cnt = cnt_ref[e]
    pad = (-cnt) & (FFN_B - 1)
    row = pst_ref[e] + cnt
    out = []
    for bit in reversed(range(PAD_BITS)):
        n = 1 << bit
        start = row + ((pad >> (bit + 1)) << (bit + 1))
        copy = pltpu.make_async_copy(
            zero_sc.at[pl.ds(0, n * X_ROWS)],
            xs_hbm.at[pl.ds(pl.multiple_of(start * X_ROWS, X_ROWS), n * X_ROWS)], zsem)
        out.append(((pad & n) != 0, copy))
    return out


def _unused_block_copies(b, zero_sc, xs_hbm, zsem):
    half = FFN_B // 2 * X_ROWS
    return [pltpu.make_async_copy(
        zero_sc, xs_hbm.at[pl.ds(pl.multiple_of((2 * b + k) * half, half), half)], zsem)
        for k in range(2)]


def _dispatch_kernel(dest_ref, cnt_ref, pst_ref, used_ref, xn_ref, xs_hbm, zero_sc, sem, zsem):
    tm = DISP_TM
    i = pl.program_id(0)

    @pl.when(i == 0)
    def _():
        zero_sc[...] = jnp.zeros_like(zero_sc)

        def fill(e, carry):
            for cond, copy in _pad_fill_copies(e, cnt_ref, pst_ref, zero_sc, xs_hbm, zsem):
                pl.when(cond)(copy.start)
            return carry

        lax.fori_loop(0, MOE_EXPERTS, fill, 0)

        def fill_block(b, carry):
            for copy in _unused_block_copies(b, zero_sc, xs_hbm, zsem):
                copy.start()
            return carry

        lax.fori_loop(used_ref[0], N_BLOCKS, fill_block, 0)

    def issue(it, carry):
        for u in range(DMA_UNROLL):
            r = it * DMA_UNROLL + u
            t = i * tm + r
            src = xn_ref.at[pl.ds(pl.multiple_of(r * X_ROWS, X_ROWS), X_ROWS)]
            for kk in range(2):
                d = pl.multiple_of(dest_ref[kk * SEQ + t] * X_ROWS, X_ROWS)
                pltpu.make_async_copy(src, xs_hbm.at[pl.ds(d, X_ROWS)], sem).start(priority=kk)
        return carry

    lax.fori_loop(0, tm // DMA_UNROLL, issue, 0)
    for _ in range(2):
        pltpu.make_async_copy(xn_ref, xs_hbm.at[pl.ds(0, tm * X_ROWS)], sem).wait()

    @pl.when(i == 0)
    def _():
        def drain(e, carry):
            for cond, copy in _pad_fill_copies(e, cnt_ref, pst_ref, zero_sc, xs_hbm, zsem):
                pl.when(cond)(copy.wait)
            return carry

        lax.fori_loop(0, MOE_EXPERTS, drain, 0)

        def drain_block(b, carry):
            for copy in _unused_block_copies(b, zero_sc, xs_hbm, zsem):
                copy.wait()
            return carry

        lax.fori_loop(used_ref[0], N_BLOCKS, drain_block, 0)


def _dispatch(dest_flat, counts, pad_start, used, xn3):
    tm = DISP_TM
    return pl.pallas_call(
        _dispatch_kernel,
        out_shape=jax.ShapeDtypeStruct((N_BUF * X_ROWS, LANES), U32),
        grid_spec=pltpu.PrefetchScalarGridSpec(
            num_scalar_prefetch=4,
            grid=(SEQ // tm,),
            in_specs=[pl.BlockSpec((tm * X_ROWS, LANES), lambda i, d, c, p, u: (i, 0))],
            out_specs=pl.BlockSpec(memory_space=pl.ANY),
            scratch_shapes=[
                pltpu.VMEM((FFN_B // 2 * X_ROWS, LANES), U32),
                pltpu.SemaphoreType.DMA(()),
                pltpu.SemaphoreType.DMA(()),
            ],
        ),
        compiler_params=_params(("arbitrary",)),
        name="moe_dispatch",
    )(dest_flat, counts, pad_start, used, xn3)


BLOCK_COPY_PRIORITY = 1


def _ffn_kernel(cnt_ref, pst_ref, used_ref, xs_hbm, wg_ref, wu_ref, wd_ref, y_hbm,
                xbuf, ybuf, zero_sc, wg_bf, wu_bf, wd_bf, sem_in, sem_out, zsem):
    B = FFN_B
    e = pl.program_id(0)
    n = (cnt_ref[e] + (B - 1)) >> PAD_BITS
    s0 = pst_ref[e] >> PAD_BITS

    def rows(blk, per_token):
        size = B * per_token
        return pl.ds(pl.multiple_of(blk * size, size), size)

    def fetch(blk, slot):
        return pltpu.make_async_copy(xs_hbm.at[rows(blk, X_ROWS)], xbuf.at[slot],
                                     sem_in.at[slot])

    def flush(blk, slot):
        return pltpu.make_async_copy(ybuf.at[slot], y_hbm.at[rows(blk, ROW_TILE)],
                                     sem_out.at[slot])

    used = used_ref[0]

    @pl.when(e == 0)
    def _():
        fetch(0, 0).start(priority=BLOCK_COPY_PRIORITY)

    @pl.when(n > 0)
    def _():
        wg_bf[...] = wg_ref[0].astype(BF16)
        wu_bf[...] = wu_ref[0].astype(BF16)
        wd_bf[...] = wd_ref[0].astype(BF16)

    def body(j, carry):
        blk = s0 + j
        slot = blk % 2

        @pl.when(blk + 1 < used)
        def _():
            fetch(blk + 1, 1 - slot).start(priority=BLOCK_COPY_PRIORITY)

        fetch(blk, slot).wait()

        @pl.when(blk >= 2)
        def _():
            flush(blk - 2, slot).wait()

        words = [xbuf[slot, pl.ds(c, B, stride=X_ROWS), :] for c in range(X_ROWS)]
        lo = [lax.bitcast_convert_type(w << 16, F32) for w in words]
        hi = [lax.bitcast_convert_type(w & jnp.uint32(0xFFFF0000), F32) for w in words]
        x = jnp.concatenate(lo + hi, axis=1).astype(BF16)
        hg = jnp.dot(x, wg_bf[...], preferred_element_type=F32)
        hu = jnp.dot(x, wu_bf[...], preferred_element_type=F32)
        hh = ((hg / (1.0 + jnp.exp(-hg))) * hu).astype(BF16)
        y = jnp.dot(hh, wd_bf[...], preferred_element_type=F32)
        for c in range(ROW_TILE):
            ybuf[slot, pl.ds(c, B, stride=ROW_TILE), :] = y[:, c * LANES:(c + 1) * LANES]
        flush(blk, slot).start(priority=BLOCK_COPY_PRIORITY)
        return carry

    lax.fori_loop(0, n, body, 0)

    @pl.when(e == MOE_EXPERTS - 1)
    def _():
        flush(used - 1, (used + 1) % 2).wait()
        flush(used - 2, used % 2).wait()
        zero_sc[...] = jnp.zeros_like(zero_sc)

        def fill(b, carry):
            pltpu.make_async_copy(zero_sc, y_hbm.at[rows(b, ROW_TILE)], zsem).start()
            return carry

        def drain(b, carry):
            pltpu.make_async_copy(zero_sc, y_hbm.at[rows(b, ROW_TILE)], zsem).wait()
            return carry

        lax.fori_loop(used_ref[0], N_BLOCKS, fill, 0)
        lax.fori_loop(used_ref[0], N_BLOCKS, drain, 0)


def _ffn(counts, pad_start, used, xs, w_gate, w_up, w_down):
    B = FFN_B
    wspec = lambda a, c: pl.BlockSpec((1, a, c), lambda e, cnt, pst, used: (e, 0, 0))
    return pl.pallas_call(
        _ffn_kernel,
        out_shape=jax.ShapeDtypeStruct((N_BUF * ROW_TILE, LANES), F32),
        grid_spec=pltpu.PrefetchScalarGridSpec(
            num_scalar_prefetch=3,
            grid=(MOE_EXPERTS,),
            in_specs=[
                pl.BlockSpec(memory_space=pl.ANY),
                wspec(D_MODEL, MOE_HIDDEN),
                wspec(D_MODEL, MOE_HIDDEN),
                wspec(MOE_HIDDEN, D_MODEL),
            ],
            out_specs=pl.BlockSpec(memory_space=pl.ANY),
            scratch_shapes=[
                pltpu.VMEM((2, B * X_ROWS, LANES), U32),
                pltpu.VMEM((2, B * ROW_TILE, LANES), F32),
                pltpu.VMEM((B * ROW_TILE, LANES), F32),
                pltpu.VMEM((D_MODEL, MOE_HIDDEN), BF16),
                pltpu.VMEM((D_MODEL, MOE_HIDDEN), BF16),
                pltpu.VMEM((MOE_HIDDEN, D_MODEL), BF16),
                pltpu.SemaphoreType.DMA((2,)),
                pltpu.SemaphoreType.DMA((2,)),
                pltpu.SemaphoreType.DMA(()),
            ],
        ),
        compiler_params=_params(("arbitrary",)),
        name="expert_ffn",
    )(counts, pad_start, used, xs, w_gate, w_up, w_down)


def _combine_kernel(dest_ref, h1_ref, rw_ref, g_ref, y_hbm, o_ref, ybuf, sem):
    tm = COMB_TM
    i = pl.program_id(0)

    def gather(tile, slot):
        def issue(it, carry):
            for u in range(DMA_UNROLL):
                r = it * DMA_UNROLL + u
                t = tile * tm + r
                for kk in range(2):
                    d = pl.multiple_of(dest_ref[kk * SEQ + t] * ROW_TILE, ROW_TILE)
                    pltpu.make_async_copy(
                        y_hbm.at[pl.ds(d, ROW_TILE)],
                        ybuf.at[slot, kk, pl.ds(pl.multiple_of(r * ROW_TILE, ROW_TILE), ROW_TILE)],
                        sem.at[slot, kk]).start(priority=kk)
            return carry

        lax.fori_loop(0, tm // DMA_UNROLL, issue, 0)

    @pl.when(i == 0)
    def _():
        gather(0, 0)

    @pl.when(i + 1 < pl.num_programs(0))
    def _():
        gather(i + 1, (i + 1) % 2)

    slot = i % 2
    for kk in range(2):
        pltpu.make_async_copy(y_hbm.at[pl.ds(0, tm * ROW_TILE)], ybuf.at[slot, kk],
                              sem.at[slot, kk]).wait()
    w = rw_ref[...]
    ys = [jnp.concatenate([ybuf[slot, kk, pl.ds(c, tm, stride=ROW_TILE), :]
                           for c in range(ROW_TILE)], axis=1) for kk in range(2)]
    h = h1_ref[...] + w[:, 0:1] * ys[0] + w[:, 1:2] * ys[1]
    var = jnp.mean(h * h, axis=-1, keepdims=True)
    o_ref[...] = h * lax.rsqrt(var + EPS) * g_ref[...]


def _combine(dest_flat, h1, rw, g, y):
    tm = COMB_TM
    return pl.pallas_call(
        _combine_kernel,
        out_shape=jax.ShapeDtypeStruct((SEQ, D_MODEL), F32),
        grid_spec=pltpu.PrefetchScalarGridSpec(
            num_scalar_prefetch=1,
            grid=(SEQ // tm,),
            in_specs=[
                pl.BlockSpec((tm, D_MODEL), lambda i, d: (i, 0)),
                pl.BlockSpec((tm, LANES), lambda i, d: (i, 0)),
                pl.BlockSpec((1, D_MODEL), lambda i, d: (0, 0)),
                pl.BlockSpec(memory_space=pl.ANY),
            ],
            out_specs=pl.BlockSpec((tm, D_MODEL), lambda i, d: (i, 0)),
            scratch_shapes=[
                pltpu.VMEM((2, 2, tm * ROW_TILE, LANES), F32),
                pltpu.SemaphoreType.DMA((2, 2)),
            ],
        ),
        compiler_params=_params(("arbitrary",)),
        name="moe_combine",
    )(dest_flat, h1, rw, g, y)


def _attention_tables():
    T = ATT_T
    f32 = np.float32
    slopes = np.exp2(-ALIBI_MAX * np.arange(1, DA_HEADS + 1, dtype=f32) / DA_HEADS).astype(f32)
    r = np.arange(T)
    hi = ((r // CHUNK) * CHUNK).astype(f32)
    lo = (r % CHUNK).astype(f32)
    sl = slopes[:, None]
    one_h = np.ones((DA_HEADS, T), f32)
    q_rows = np.stack([one_h, one_h, -sl * hi[None], -sl * lo[None]], axis=1)
    k_cols = np.stack([sl * hi[None], sl * lo[None], one_h, one_h], axis=-1)
    qa = np.zeros((DA_HEADS, DA_HEAD_DIM, T), f32)
    qa[:, 0:4, :] = q_rows
    ka1 = np.zeros((DA_HEADS, T, LANES), f32)
    ka1[:, :, DA_HEAD_DIM:DA_HEAD_DIM + 4] = k_cols
    ka2 = np.zeros((DA_HEADS, T, LANES), f32)
    ka2[:, :, 0:4] = k_cols
    rel = (r[:, None] - r[None, :]).astype(f32)
    allowed = (r[:, None] // CHUNK) <= (r[None, :] // CHUNK)
    fix = np.where(rel > 0, -2.0 * slopes[:, None, None] * rel[None], 0.0).astype(f32)
    dtab = np.where(allowed[None], fix, -np.inf).astype(f32)
    return (jnp.asarray(slopes), jnp.asarray(qa, BF16), jnp.asarray(ka1, BF16),
            jnp.asarray(ka2, BF16), jnp.asarray(dtab))


def _retention_tables():
    C = RET_C
    f32 = np.float32
    log_gamma = np.log1p(-np.exp2(-5.0 - np.arange(RET_HEADS, dtype=f32))).astype(f32)
    pos = np.arange(C, dtype=f32)
    rel = pos[:, None] - pos[None, :]
    dec = np.where(rel >= 0, np.exp(log_gamma[:, None, None] * np.maximum(rel, 0.0)), 0.0)
    qdec = np.exp(log_gamma[:, None] * (pos + 1.0)[None, :])[:, :, None]
    kdec = np.exp(log_gamma[:, None] * (C - 1 - pos)[None, :])[:, :, None]
    cd = np.exp(log_gamma * C)
    return tuple(jnp.asarray(t, F32) for t in (cd, dec, qdec, kdec))


def kernel(x, attn_norm_g, w_in, da_lambda_q1, da_lambda_k1, da_lambda_q2, da_lambda_k2,
           da_subln_g, w_out, ffn_norm_g, router_group_w, router_group_b, router_expert_w,
           router_expert_b, expert_w_gate, expert_w_up, expert_w_down, final_norm_g):
    B, S, D = x.shape
    assert (B, S, D) == (1, SEQ, D_MODEL)
    x2 = x.reshape(S, D)

    cd, dec, qdec, kdec = _retention_tables()
    slopes, qa, ka1, ka2, dtab = _attention_tables()
    k_da, qt4, vt4, o_r = _inproj(x2, attn_norm_g[0][None, :], w_in[0], cd, dec, qdec, kdec)

    o_da = _attention(k_da, qt4, vt4, slopes, qa, ka1, ka2, dtab, da_lambda_q1, da_lambda_k1,
                      da_lambda_q2, da_lambda_k2, da_subln_g)

    wr = jnp.zeros((D, LANES), F32)
    wr = wr.at[:, :MOE_GROUPS].set(router_group_w[0])
    wr = wr.at[:, MOE_GROUPS:MOE_GROUPS + MOE_EXPERTS].set(router_expert_w[0])
    br = jnp.zeros((1, LANES), F32)
    br = br.at[0, :MOE_GROUPS].set(router_group_b[0])
    br = br.at[0, MOE_GROUPS:MOE_GROUPS + MOE_EXPERTS].set(router_expert_b[0])
    wr_hi = wr.astype(BF16)
    wr_lo = (wr - wr_hi.astype(F32)).astype(BF16)
    h1, xn, ri, rw, cnt = _outproj_router(x2, o_da, o_r, w_out[0],
                                          ffn_norm_g[0][None, :],
                                          jnp.concatenate([wr_hi, wr_lo], axis=1), br)

    dest, meta = _plan(ri, cnt)
    dest_flat = dest[:, 0:2, :].transpose(1, 0, 2).reshape(N_ASSIGN)
    used1 = meta[0, :1]
    counts = meta[1, :MOE_EXPERTS]
    pad_start = meta[2, :MOE_EXPERTS]
    xs = _dispatch(dest_flat, counts, pad_start, used1, xn)

    y = _ffn(counts, pad_start, used1, xs, expert_w_gate[0], expert_w_up[0], expert_w_down[0])
    out = _combine(dest_flat, h1, rw, final_norm_g[None, :], y)
    return out.reshape(B, S, D)
```

```python
import math

import jax
import jax.numpy as jnp
import numpy as np
from jax import lax
from jax.experimental import pallas as pl
from jax.experimental.pallas import tpu as pltpu

F32 = jnp.float32
BF16 = jnp.bfloat16
I32 = jnp.int32
U32 = jnp.uint32

D_MODEL = 1024
SEQ = 16384
CHUNK = 64
EPS = 1e-6

DA_HEADS = 4
DA_HEAD_DIM = 64
DA_V_DIM = 128
DA_WIDTH = 512
ALIBI_MAX = 8.0
RET_HEADS = 4
RET_QK_DIM = 64
RET_V_DIM = 128
RET_WIDTH = 512
W_IN_COLS = 3072
T_ROWS = 512
MAIN_COLS = 2048
DK_OFF = 0
RQ_OFF = 512
RK_OFF = 768
RV_OFF = 1024
RG_OFF = 1536

MOE_GROUPS = 4
MOE_EXPERTS_PER_GROUP = 8
MOE_EXPERTS = 32
MOE_HIDDEN = 512
LAMBDA_INIT = 0.8 - 0.6 * math.exp(-0.3 * 0)

LANES = 128
ROW_TILE = 8
X_ROWS = 4
VMEM_LIMIT = 56 * 1024 * 1024

PROJ_TM = 512
ROUTER_TM = 1024
ATT_T = 512
RET_C = 256
PLAN_T = 512
FFN_B = 512
N_ASSIGN = 2 * SEQ
N_BLOCKS = N_ASSIGN // FFN_B + MOE_EXPERTS
N_BUF = N_BLOCKS * FFN_B
COMB_TM = 512
DISP_TM = 1024
DMA_UNROLL = 8


def _params(sem):
    return pltpu.CompilerParams(dimension_semantics=sem, vmem_limit_bytes=VMEM_LIMIT)


def _retention_block(q_all, k_all, v_all, g_all, cd_ref, dec_ref, qdec_ref, kdec_ref, st_sc):
    outs = []
    for h in range(RET_HEADS):
        qk = slice(h * RET_QK_DIM, (h + 1) * RET_QK_DIM)
        vv = slice(h * RET_V_DIM, (h + 1) * RET_V_DIM)
        q = q_all[:, qk]
        k = k_all[:, qk]
        v = v_all[:, vv]
        g = g_all[:, vv]
        s = lax.dot_general(q, k, (((1,), (1,)), ((), ())),
                            preferred_element_type=F32) * dec_ref[h]
        intra = jnp.dot(s.astype(BF16), v, preferred_element_type=F32)
        st = st_sc[h]
        cross = jnp.dot(q, st.astype(BF16), preferred_element_type=F32) * qdec_ref[h]
        kd = (k.astype(F32) * kdec_ref[h]).astype(BF16)
        st_sc[h] = st * cd_ref[h] + lax.dot_general(kd, v, (((0,), (0,)), ((), ())),
                                                    preferred_element_type=F32)
        o = intra + cross
        o = o * lax.rsqrt(jnp.mean(o * o, axis=-1, keepdims=True) + EPS)
        outs.append(((g / (1.0 + jnp.exp(-g))) * o).astype(BF16))
    return outs


def _inproj_kernel(cd_ref, x_ref, g_ref, win_ref, dec_ref, qdec_ref, kdec_ref,
                   k_ref, qt_ref, vt_ref, or_ref, st_sc, w_ref, wq_ref, wv_ref):
    @pl.when(pl.program_id(0) == 0)
    def _():
        st_sc[...] = jnp.zeros_like(st_sc)
        cols = DA_WIDTH
        w_ref[:, DK_OFF:RQ_OFF] = win_ref[:, cols:2 * cols].astype(BF16)
        w_ref[:, RQ_OFF:RK_OFF] = win_ref[:, 3 * cols:3 * cols + 256].astype(BF16)
        w_ref[:, RK_OFF:RV_OFF] = (win_ref[:, 3 * cols + 256:4 * cols]
                                   * (RET_QK_DIM ** -0.5)).astype(BF16)
        w_ref[:, RV_OFF:MAIN_COLS] = win_ref[:, 4 * cols:6 * cols].astype(BF16)
        step = 256
        for r in range(D_MODEL // step):
            rows = slice(r * step, (r + 1) * step)
            wq_ref[:, rows] = (win_ref[rows, 0:cols] * (DA_HEAD_DIM ** -0.5)).T.astype(BF16)
            wv_ref[:, rows] = win_ref[rows, 2 * cols:3 * cols].T.astype(BF16)

    x = x_ref[...]
    var = jnp.mean(x * x, axis=-1, keepdims=True)
    xn = (x * lax.rsqrt(var + EPS) * g_ref[...]).astype(BF16)

    def proj(lo, hi):
        return jnp.dot(xn, w_ref[:, lo:hi], preferred_element_type=F32)

    k_ref[...] = proj(DK_OFF, DK_OFF + DA_WIDTH).astype(BF16)
    nt = (((1,), (1,)), ((), ()))
    qt = lax.dot_general(wq_ref[...], xn, nt, preferred_element_type=F32)
    qt_ref[...] = qt.astype(BF16).reshape(DA_HEADS, 1, 2 * DA_HEAD_DIM, PROJ_TM)
    vt = lax.dot_general(wv_ref[...], xn, nt, preferred_element_type=F32)
    vt_ref[...] = vt.astype(BF16).reshape(DA_HEADS, 1, DA_V_DIM, PROJ_TM)

    rq = proj(RQ_OFF, RK_OFF).astype(BF16)
    rk = proj(RK_OFF, RV_OFF).astype(BF16)
    rv = proj(RV_OFF, RG_OFF).astype(BF16)
    rg = proj(RG_OFF, MAIN_COLS)
    for blk in range(PROJ_TM // RET_C):
        rows = slice(blk * RET_C, (blk + 1) * RET_C)
        outs = _retention_block(rq[rows], rk[rows], rv[rows], rg[rows],
                                cd_ref, dec_ref, qdec_ref, kdec_ref, st_sc)
        for h in range(RET_HEADS):
            or_ref[rows, h * RET_V_DIM:(h + 1) * RET_V_DIM] = outs[h]


def _inproj(x2, g, w_in, cd, dec, qdec, kdec):
    C = RET_C
    t_shape = jax.ShapeDtypeStruct((DA_HEADS, SEQ // PROJ_TM, LANES, PROJ_TM), BF16)
    t_spec = pl.BlockSpec((DA_HEADS, 1, LANES, PROJ_TM), lambda i: (0, i, 0, 0))
    return pl.pallas_call(
        _inproj_kernel,
        out_shape=(jax.ShapeDtypeStruct((SEQ, DA_WIDTH), BF16), t_shape, t_shape,
                   jax.ShapeDtypeStruct((SEQ, RET_WIDTH), BF16)),
        grid=(SEQ // PROJ_TM,),
        in_specs=[
            pl.BlockSpec(memory_space=pltpu.SMEM),
            pl.BlockSpec((PROJ_TM, D_MODEL), lambda i: (i, 0)),
            pl.BlockSpec((1, D_MODEL), lambda i: (0, 0)),
            pl.BlockSpec((D_MODEL, W_IN_COLS), lambda i: (0, 0), pipeline_mode=pl.Buffered(1)),
            pl.BlockSpec((RET_HEADS, C, C), lambda i: (0, 0, 0)),
            pl.BlockSpec((RET_HEADS, C, 1), lambda i: (0, 0, 0)),
            pl.BlockSpec((RET_HEADS, C, 1), lambda i: (0, 0, 0)),
        ],
        out_specs=(pl.BlockSpec((PROJ_TM, DA_WIDTH), lambda i: (i, 0)), t_spec, t_spec,
                   pl.BlockSpec((PROJ_TM, RET_WIDTH), lambda i: (i, 0))),
        scratch_shapes=[
            pltpu.VMEM((RET_HEADS, RET_QK_DIM, RET_V_DIM), F32),
            pltpu.VMEM((D_MODEL, MAIN_COLS), BF16),
            pltpu.VMEM((T_ROWS, D_MODEL), BF16),
            pltpu.VMEM((T_ROWS, D_MODEL), BF16),
        ],
        compiler_params=_params(("arbitrary",)),
        name="inproj_retention",
    )(cd, x2, g, w_in, dec, qdec, kdec)


ACC_ROWS = DA_V_DIM + 16


N_QT = SEQ // ATT_T
N_OFF = N_QT * (N_QT - 1) // 2


def _pipeline3(n_pos, scores, accumulate):
    scores(0, 0)
    scores(1, 1)
    steady = n_pos - 2

    def triple(k, carry):
        t = 3 * k
        accumulate(t, 0)
        scores(t + 2, 2)
        accumulate(t + 1, 1)
        scores(t + 3, 0)
        accumulate(t + 2, 2)
        scores(t + 4, 1)
        return carry

    lax.fori_loop(0, steady // 3, triple, 0)
    t0 = steady // 3 * 3
    rem = steady - t0
    accumulate(t0, 0)
    if rem >= 1:
        scores(t0 + 2, 2)
    accumulate(t0 + 1, 1)
    if rem == 2:
        scores(t0 + 3, 0)
    if rem >= 1:
        accumulate(t0 + 2, 2)
    if rem == 2:
        accumulate(t0 + 3, 0)


def _attn_kernel(slope_ref, jt_ref, it_ref, qt_ref, k_ref, vt_ref, qa_ref, ka1_ref, ka2_ref,
                 dtab_ref, lq1_ref, lk1_ref, lq2_ref, lk2_ref, g_ref, o_ref,
                 m_sc, acc_sc, s0_sc, s1_sc, s2_sc, mx0_sc, mx1_sc, mx2_sc):
    T = ATT_T
    h = pl.program_id(0)
    slope = slope_ref[h]
    qa = qa_ref[0]
    lane = lax.broadcasted_iota(I32, (T, LANES), 1)
    sums_row = (lax.broadcasted_iota(I32, (16, T), 0) == 0).astype(BF16)
    s_bufs = (s0_sc, s1_sc, s2_sc)
    mx_bufs = (mx0_sc, mx1_sc, mx2_sc)

    def scores(j, i, buf, extra):
        kt = k_ref[pl.ds(pl.multiple_of(j * T, T), T), :]
        ks = (jnp.where(lane < DA_HEAD_DIM, kt, ka1_ref[0]),
              jnp.where(lane >= DA_HEAD_DIM, kt, ka2_ref[0]))
        qt = qt_ref[0, i]
        qw = (jnp.concatenate([qt[0:DA_HEAD_DIM], qa], axis=0),
              jnp.concatenate([qa, qt[DA_HEAD_DIM:]], axis=0))
        for mp in range(2):
            s = jnp.dot(ks[mp], qw[mp], preferred_element_type=F32)
            if extra is not None:
                s = s + extra[0]
            s_bufs[buf][mp] = s
            mx_bufs[buf][mp] = jnp.max(s, axis=0, keepdims=True)

    def accumulate(j, i, buf):
        c = slope * lax.convert_element_type((i - j) * T, F32)
        vte = jnp.concatenate([vt_ref[0, j], sums_row], axis=0)
        for mp in range(2):
            m_prev = m_sc[i, mp]
            m_new = jnp.maximum(m_prev, mx_bufs[buf][mp] - c)
            p = jnp.exp(s_bufs[buf][mp] - (m_new + c)).astype(BF16)
            pv = jnp.dot(vte, p, preferred_element_type=F32)
            acc_sc[i, mp] = jnp.exp(m_prev - m_new) * acc_sc[i, mp] + pv
            m_sc[i, mp] = m_new

    m_sc[...] = jnp.full_like(m_sc, -jnp.inf)
    acc_sc[...] = jnp.zeros_like(acc_sc)
    _pipeline3(N_QT,
               lambda pos, buf: scores(pos, pos, buf, dtab_ref),
               lambda pos, buf: accumulate(pos, pos, buf))
    _pipeline3(N_OFF,
               lambda pos, buf: scores(jt_ref[pos], it_ref[pos], buf, None),
               lambda pos, buf: accumulate(jt_ref[pos], it_ref[pos], buf))

    lam = (jnp.exp(jnp.sum(lq1_ref[...] * lk1_ref[...], axis=1, keepdims=True))
           - jnp.exp(jnp.sum(lq2_ref[...] * lk2_ref[...], axis=1, keepdims=True))
           + LAMBDA_INIT)

    def finish(i):
        a1 = acc_sc[i, 0]
        a2 = acc_sc[i, 1]
        ot = (a1[0:DA_V_DIM] / a1[DA_V_DIM:DA_V_DIM + 1]
              - lam * (a2[0:DA_V_DIM] / a2[DA_V_DIM:DA_V_DIM + 1]))
        o = ot.T
        var = jnp.mean(o * o, axis=-1, keepdims=True)
        o = (o * lax.rsqrt(var + EPS) * g_ref[...]) * (1.0 - LAMBDA_INIT)
        o_ref[pl.ds(pl.multiple_of(i * T, T), T), :] = o.astype(BF16)

    def finish_pair(k, carry):
        finish(2 * k)
        finish(2 * k + 1)
        return carry

    lax.fori_loop(0, N_QT // 2, finish_pair, 0)


def _attention(proj, qt4, vt4, slopes, qa, ka1, ka2, dtab, lq1, lk1, lq2, lk2, subln_g):
    T = ATT_T
    vec64 = pl.BlockSpec((1, DA_HEAD_DIM), lambda h: (0, 0))
    per_head = lambda a, b: pl.BlockSpec((1, a, b), lambda h: (h, 0, 0))
    slab = lambda shape, imap: pl.BlockSpec(shape, imap, pipeline_mode=pl.Buffered(1))
    smem = pl.BlockSpec(memory_space=pltpu.SMEM)
    it_tab, jt_tab = np.tril_indices(N_QT, -1)
    return pl.pallas_call(
        _attn_kernel,
        out_shape=jax.ShapeDtypeStruct((SEQ, DA_WIDTH), BF16),
        grid=(DA_HEADS,),
        in_specs=[
            smem, smem, smem,
            slab((1, N_QT, LANES, T), lambda h: (h, 0, 0, 0)),
            slab((SEQ, LANES), lambda h: (0, DK_OFF // LANES + h)),
            slab((1, N_QT, LANES, T), lambda h: (h, 0, 0, 0)),
            per_head(DA_HEAD_DIM, T), per_head(T, LANES), per_head(T, LANES), per_head(T, T),
            vec64, vec64, vec64, vec64,
            pl.BlockSpec((1, DA_V_DIM), lambda h: (0, 0)),
        ],
        out_specs=slab((SEQ, LANES), lambda h: (0, h)),
        scratch_shapes=[
            pltpu.VMEM((N_QT, 2, 1, T), F32),
            pltpu.VMEM((N_QT, 2, ACC_ROWS, T), F32),
            pltpu.VMEM((2, T, T), F32),
            pltpu.VMEM((2, T, T), F32),
            pltpu.VMEM((2, T, T), F32),
            pltpu.VMEM((2, 1, T), F32),
            pltpu.VMEM((2, 1, T), F32),
            pltpu.VMEM((2, 1, T), F32),
        ],
        compiler_params=_params(("arbitrary",)),
        name="diff_attention",
    )(slopes, jnp.asarray(jt_tab, I32), jnp.asarray(it_tab, I32), qt4, proj, vt4, qa, ka1, ka2,
      dtab, lq1, lk1, lq2, lk2, subln_g)


def _outproj_router_kernel(x_ref, oda_ref, or_ref, wo32_ref, g_ref, wr_ref, br_ref,
                           h1_ref, xn_ref, ri_ref, rw_ref, cnt_ref, wo_ref):
    @pl.when(pl.program_id(0) == 0)
    def _():
        wo_ref[...] = wo32_ref[...].astype(BF16)

    h1 = (x_ref[...]
          + jnp.dot(oda_ref[...], wo_ref[0:DA_WIDTH, :], preferred_element_type=F32)
          + jnp.dot(or_ref[...], wo_ref[DA_WIDTH:, :], preferred_element_type=F32))
    h1_ref[...] = h1
    var = jnp.mean(h1 * h1, axis=-1, keepdims=True)
    xn = h1 * lax.rsqrt(var + EPS) * g_ref[...]
    bits = lax.bitcast_convert_type(xn.astype(BF16).astype(F32), U32)
    for c in range(X_ROWS):
        lo = bits[:, c * LANES:(c + 1) * LANES] >> 16
        hi = bits[:, (c + X_ROWS) * LANES:(c + X_ROWS + 1) * LANES] & jnp.uint32(0xFFFF0000)
        xn_ref[pl.ds(c, ROUTER_TM, stride=X_ROWS), :] = lo | hi
    x_hi = xn.astype(BF16)
    x_lo = (xn - x_hi.astype(F32)).astype(BF16)
    both = jnp.dot(x_hi, wr_ref[...], preferred_element_type=F32)
    logits = (both[:, :LANES] + both[:, LANES:]
              + jnp.dot(x_lo, wr_ref[:, :LANES], preferred_element_type=F32)) + br_ref[...]
    lane = lax.broadcasted_iota(I32, logits.shape, 1)
    neg = jnp.float32(-jnp.inf)
    big = jnp.int32(1 << 20)
    gl = jnp.where(lane < MOE_GROUPS, logits, neg)
    gmax = jnp.max(gl, axis=1, keepdims=True)
    gidx = jnp.min(jnp.where(gl == gmax, lane, big), axis=1, keepdims=True)
    gsum = jnp.sum(jnp.exp(gl - gmax), axis=1, keepdims=True)
    gp = 1.0 / gsum
    lo = MOE_GROUPS + gidx * MOE_EXPERTS_PER_GROUP
    el = jnp.where((lane >= lo) & (lane < lo + MOE_EXPERTS_PER_GROUP), logits, neg)
    v1 = jnp.max(el, axis=1, keepdims=True)
    i1 = jnp.min(jnp.where(el == v1, lane, big), axis=1, keepdims=True)
    el2 = jnp.where(lane == i1, neg, el)
    v2 = jnp.max(el2, axis=1, keepdims=True)
    i2 = jnp.min(jnp.where(el2 == v2, lane, big), axis=1, keepdims=True)
    t = jnp.exp(v2 - v1)
    w1 = gp / (1.0 + t)
    w2 = gp * t / (1.0 + t)
    ri_ref[...] = jnp.where(lane == 0, i1 - MOE_GROUPS,
                            jnp.where(lane == 1, i2 - MOE_GROUPS, 0))
    rw_ref[...] = jnp.where(lane == 0, w1, jnp.where(lane == 1, w2, 0.0))

    @pl.when(pl.program_id(0) == 0)
    def _():
        cnt_ref[...] = jnp.zeros_like(cnt_ref)

    chosen = (lane == i1 - MOE_GROUPS) | (lane == i2 - MOE_GROUPS)
    cnt_ref[...] += jnp.sum(chosen.astype(F32), axis=0, keepdims=True)


def _outproj_router(x2, o_da, o_r, w_out, g, wr, br):
    tm = ROUTER_TM
    row = lambda w: pl.BlockSpec((tm, w), lambda i: (i, 0))
    full = lambda a, b: pl.BlockSpec((a, b), lambda i: (0, 0))
    return pl.pallas_call(
        _outproj_router_kernel,
        out_shape=(
            jax.ShapeDtypeStruct((SEQ, D_MODEL), F32),
            jax.ShapeDtypeStruct((SEQ * X_ROWS, LANES), U32),
            jax.ShapeDtypeStruct((SEQ, LANES), I32),
            jax.ShapeDtypeStruct((SEQ, LANES), F32),
            jax.ShapeDtypeStruct((8, LANES), F32),
        ),
        grid=(SEQ // tm,),
        in_specs=[row(D_MODEL), row(DA_WIDTH), row(RET_WIDTH),
                  pl.BlockSpec((D_MODEL, D_MODEL), lambda i: (0, 0),
                               pipeline_mode=pl.Buffered(1)),
                  full(1, D_MODEL), full(D_MODEL, 2 * LANES), full(1, LANES)],
        out_specs=(row(D_MODEL), pl.BlockSpec((tm * X_ROWS, LANES), lambda i: (i, 0)),
                   row(LANES), row(LANES), full(8, LANES)),
        scratch_shapes=[pltpu.VMEM((D_MODEL, D_MODEL), BF16)],
        compiler_params=_params(("arbitrary",)),
        name="outproj_router",
    )(x2, o_da, o_r, w_out, g, wr, br)


def _plan_kernel(ri_ref, cnt_ref, dest_ref, used_ref):
    TT = PLAN_T
    lane = lax.broadcasted_iota(I32, (TT, LANES), 1)

    def onehots(t):
        r = ri_ref[pl.ds(pl.multiple_of(t * TT, TT), TT), :]
        return lane == r[:, 0:1], lane == r[:, 1:2]

    counts8 = cnt_ref[...].astype(I32)
    shift = FFN_B.bit_length() - 1
    padded = ((counts8 + (FFN_B - 1)) >> shift) << shift
    lane8 = lax.broadcasted_iota(I32, (8, LANES), 1)
    pad_end = padded
    sh = 1
    while sh < LANES:
        pad_end = pad_end + jnp.where(lane8 >= sh, pltpu.roll(pad_end, sh, axis=1), 0)
        sh *= 2
    pad_start = pad_end - padded

    ltri = (lax.broadcasted_iota(I32, (TT, TT), 0)
            > lax.broadcasted_iota(I32, (TT, TT), 1)).astype(BF16)

    def dest_body(t, carry):
        oh1, oh2 = onehots(t)
        a = (oh1 | oh2).astype(F32)
        base = jnp.dot(ltri, a.astype(BF16), preferred_element_type=F32) + carry
        d1 = jnp.sum(jnp.where(oh1, base, 0.0), axis=1, keepdims=True)
        d2 = jnp.sum(jnp.where(oh2, base, 0.0), axis=1, keepdims=True)
        both = jnp.where(lane == 0, d1, jnp.where(lane == 1, d2, 0.0))
        dest_ref[t] = both.T[0:8, :].astype(I32)
        return carry + jnp.sum(a, axis=0, keepdims=True)

    lax.fori_loop(0, SEQ // TT, dest_body, pad_start[0:1].astype(F32))

    total = jnp.max(pad_end, axis=1, keepdims=True)
    row8 = lax.broadcasted_iota(I32, (8, LANES), 0)
    used_ref[...] = jnp.where(row8 == 0, jnp.broadcast_to(total >> shift, (8, LANES)),
                              jnp.where(row8 == 1, counts8, pad_start))


def _plan(ri, cnt):
    return pl.pallas_call(
        _plan_kernel,
        out_shape=(
            jax.ShapeDtypeStruct((SEQ // PLAN_T, 8, PLAN_T), I32),
            jax.ShapeDtypeStruct((8, LANES), I32),
        ),
        compiler_params=pltpu.CompilerParams(vmem_limit_bytes=VMEM_LIMIT),
        name="route_plan",
    )(ri, cnt)


PAD_BITS = FFN_B.bit_length() - 1


def _pad_fill_copies(e, cnt_ref, pst_ref, zero_sc, xs_hbm, zsem):
    cnt = cnt_ref[e]
    pad = (-cnt) & (FFN_B - 1)
    row = pst_ref[e] + cnt
    out = []
    for bit in reversed(range(PAD_BITS)):
        n = 1 << bit
        start = row + ((pad >> (bit + 1)) << (bit + 1))
        copy = pltpu.make_async_copy(
            zero_sc.at[pl.ds(0, n * X_ROWS)],
            xs_hbm.at[pl.ds(pl.multiple_of(start * X_ROWS, X_ROWS), n * X_ROWS)], zsem)
        out.append(((pad & n) != 0, copy))
    return out


def _unused_block_copies(b, zero_sc, xs_hbm, zsem):
    half = FFN_B // 2 * X_ROWS
    return [pltpu.make_async_copy(
        zero_sc, xs_hbm.at[pl.ds(pl.multiple_of((2 * b + k) * half, half), half)], zsem)
        for k in range(2)]


def _dispatch_kernel(dest_ref, cnt_ref, pst_ref, used_ref, xn_ref, xs_hbm, zero_sc, sem, zsem):
    tm = DISP_TM
    i = pl.program_id(0)

    @pl.when(i == 0)
    def _():
        zero_sc[...] = jnp.zeros_like(zero_sc)

        def fill(e, carry):
            for cond, copy in _pad_fill_copies(e, cnt_ref, pst_ref, zero_sc, xs_hbm, zsem):
                pl.when(cond)(copy.start)
            return carry

        lax.fori_loop(0, MOE_EXPERTS, fill, 0)

        def fill_block(b, carry):
            for copy in _unused_block_copies(b, zero_sc, xs_hbm, zsem):
                copy.start()
            return carry

        lax.fori_loop(used_ref[0], N_BLOCKS, fill_block, 0)

    def issue(it, carry):
        for u in range(DMA_UNROLL):
            r = it * DMA_UNROLL + u
            t = i * tm + r
            src = xn_ref.at[pl.ds(pl.multiple_of(r * X_ROWS, X_ROWS), X_ROWS)]
            for kk in range(2):
                d = pl.multiple_of(dest_ref[kk * SEQ + t] * X_ROWS, X_ROWS)
                pltpu.make_async_copy(src, xs_hbm.at[pl.ds(d, X_ROWS)], sem).start(priority=kk)
        return carry

    lax.fori_loop(0, tm // DMA_UNROLL, issue, 0)
    for _ in range(2):
        pltpu.make_async_copy(xn_ref, xs_hbm.at[pl.ds(0, tm * X_ROWS)], sem).wait()

    @pl.when(i == 0)
    def _():
        def drain(e, carry):
            for cond, copy in _pad_fill_copies(e, cnt_ref, pst_ref, zero_sc, xs_hbm, zsem):
                pl.when(cond)(copy.wait)
            return carry

        lax.fori_loop(0, MOE_EXPERTS, drain, 0)

        def drain_block(b, carry):
            for copy in _unused_block_copies(b, zero_sc, xs_hbm, zsem):
                copy.wait()
            return carry

        lax.fori_loop(used_ref[0], N_BLOCKS, drain_block, 0)


def _dispatch(dest_flat, counts, pad_start, used, xn3):
    tm = DISP_TM
    return pl.pallas_call(
        _dispatch_kernel,
        out_shape=jax.ShapeDtypeStruct((N_BUF * X_ROWS, LANES), U32),
        grid_spec=pltpu.PrefetchScalarGridSpec(
            num_scalar_prefetch=4,
            grid=(SEQ // tm,),
            in_specs=[pl.BlockSpec((tm * X_ROWS, LANES), lambda i, d, c, p, u: (i, 0))],
            out_specs=pl.BlockSpec(memory_space=pl.ANY),
            scratch_shapes=[
                pltpu.VMEM((FFN_B // 2 * X_ROWS, LANES), U32),
                pltpu.SemaphoreType.DMA(()),
                pltpu.SemaphoreType.DMA(()),
            ],
        ),
        compiler_params=_params(("arbitrary",)),
        name="moe_dispatch",
    )(dest_flat, counts, pad_start, used, xn3)


BLOCK_COPY_PRIORITY = 1


def _ffn_kernel(cnt_ref, pst_ref, used_ref, xs_hbm, wg_ref, wu_ref, wd_ref, y_hbm,
                xbuf, ybuf, zero_sc, wg_bf, wu_bf, wd_bf, sem_in, sem_out, zsem):
    B = FFN_B
    e = pl.program_id(0)
    n = (cnt_ref[e] + (B - 1)) >> PAD_BITS
    s0 = pst_ref[e] >> PAD_BITS

    def rows(blk, per_token):
        size = B * per_token
        return pl.ds(pl.multiple_of(blk * size, size), size)

    def fetch(blk, slot):
        return pltpu.make_async_copy(xs_hbm.at[rows(blk, X_ROWS)], xbuf.at[slot],
                                     sem_in.at[slot])

    def flush(blk, slot):
        return pltpu.make_async_copy(ybuf.at[slot], y_hbm.at[rows(blk, ROW_TILE)],
                                     sem_out.at[slot])

    used = used_ref[0]

    @pl.when(e == 0)
    def _():
        fetch(0, 0).start(priority=BLOCK_COPY_PRIORITY)

    @pl.when(n > 0)
    def _():
        wg_bf[...] = wg_ref[0].astype(BF16)
        wu_bf[...] = wu_ref[0].astype(BF16)
        wd_bf[...] = wd_ref[0].astype(BF16)

    def body(j, carry):
        blk = s0 + j
        slot = blk % 2

        @pl.when(blk + 1 < used)
        def _():
            fetch(blk + 1, 1 - slot).start(priority=BLOCK_COPY_PRIORITY)

        fetch(blk, slot).wait()

        @pl.when(blk >= 2)
        def _():
            flush(blk - 2, slot).wait()

        words = [xbuf[slot, pl.ds(c, B, stride=X_ROWS), :] for c in range(X_ROWS)]
        lo = [lax.bitcast_convert_type(w << 16, F32) for w in words]
        hi = [lax.bitcast_convert_type(w & jnp.uint32(0xFFFF0000), F32) for w in words]
        x = jnp.concatenate(lo + hi, axis=1).astype(BF16)
        hg = jnp.dot(x, wg_bf[...], preferred_element_type=F32)
        hu = jnp.dot(x, wu_bf[...], preferred_element_type=F32)
        hh = ((hg / (1.0 + jnp.exp(-hg))) * hu).astype(BF16)
        y = jnp.dot(hh, wd_bf[...], preferred_element_type=F32)
        for c in range(ROW_TILE):
            ybuf[slot, pl.ds(c, B, stride=ROW_TILE), :] = y[:, c * LANES:(c + 1) * LANES]
        flush(blk, slot).start(priority=BLOCK_COPY_PRIORITY)
        return carry

    lax.fori_loop(0, n, body, 0)

    @pl.when(e == MOE_EXPERTS - 1)
    def _():
        flush(used - 1, (used + 1) % 2).wait()
        flush(used - 2, used % 2).wait()
        zero_sc[...] = jnp.zeros_like(zero_sc)

        def fill(b, carry):
            pltpu.make_async_copy(zero_sc, y_hbm.at[rows(b, ROW_TILE)], zsem).start()
            return carry

        def drain(b, carry):
            pltpu.make_async_copy(zero_sc, y_hbm.at[rows(b, ROW_TILE)], zsem).wait()
            return carry

        lax.fori_loop(used_ref[0], N_BLOCKS, fill, 0)
        lax.fori_loop(used_ref[0], N_BLOCKS, drain, 0)


def _ffn(counts, pad_start, used, xs, w_gate, w_up, w_down):
    B = FFN_B
    wspec = lambda a, c: pl.BlockSpec((1, a, c), lambda e, cnt, pst, used: (e, 0, 0))
    return pl.pallas_call(
        _ffn_kernel,
        out_shape=jax.ShapeDtypeStruct((N_BUF * ROW_TILE, LANES), F32),
        grid_spec=pltpu.PrefetchScalarGridSpec(
            num_scalar_prefetch=3,
            grid=(MOE_EXPERTS,),
            in_specs=[
                pl.BlockSpec(memory_space=pl.ANY),
                wspec(D_MODEL, MOE_HIDDEN),
                wspec(D_MODEL, MOE_HIDDEN),
                wspec(MOE_HIDDEN, D_MODEL),
            ],
            out_specs=pl.BlockSpec(memory_space=pl.ANY),
            scratch_shapes=[
                pltpu.VMEM((2, B * X_ROWS, LANES), U32),
                pltpu.VMEM((2, B * ROW_TILE, LANES), F32),
                pltpu.VMEM((B * ROW_TILE, LANES), F32),
                pltpu.VMEM((D_MODEL, MOE_HIDDEN), BF16),
                pltpu.VMEM((D_MODEL, MOE_HIDDEN), BF16),
                pltpu.VMEM((MOE_HIDDEN, D_MODEL), BF16),
                pltpu.SemaphoreType.DMA((2,)),
                pltpu.SemaphoreType.DMA((2,)),
                pltpu.SemaphoreType.DMA(()),
            ],
        ),
        compiler_params=_params(("arbitrary",)),
        name="expert_ffn",
    )(counts, pad_start, used, xs, w_gate, w_up, w_down)


def _combine_kernel(dest_ref, h1_ref, rw_ref, g_ref, y_hbm, o_ref, ybuf, sem):
    tm = COMB_TM
    i = pl.program_id(0)

    def gather(tile, slot):
        def issue(it, carry):
            for u in range(DMA_UNROLL):
                r = it * DMA_UNROLL + u
                t = tile * tm + r
                for kk in range(2):
                    d = pl.multiple_of(dest_ref[kk * SEQ + t] * ROW_TILE, ROW_TILE)
                    pltpu.make_async_copy(
                        y_hbm.at[pl.ds(d, ROW_TILE)],
                        ybuf.at[slot, kk, pl.ds(pl.multiple_of(r * ROW_TILE, ROW_TILE), ROW_TILE)],
                        sem.at[slot, kk]).start(priority=kk)
            return carry

        lax.fori_loop(0, tm // DMA_UNROLL, issue, 0)

    @pl.when(i == 0)
    def _():
        gather(0, 0)

    @pl.when(i + 1 < pl.num_programs(0))
    def _():
        gather(i + 1, (i + 1) % 2)

    slot = i % 2
    for kk in range(2):
        pltpu.make_async_copy(y_hbm.at[pl.ds(0, tm * ROW_TILE)], ybuf.at[slot, kk],
                              sem.at[slot, kk]).wait()
    w = rw_ref[...]
    ys = [jnp.concatenate([ybuf[slot, kk, pl.ds(c, tm, stride=ROW_TILE), :]
                           for c in range(ROW_TILE)], axis=1) for kk in range(2)]
    h = h1_ref[...] + w[:, 0:1] * ys[0] + w[:, 1:2] * ys[1]
    var = jnp.mean(h * h, axis=-1, keepdims=True)
    o_ref[...] = h * lax.rsqrt(var + EPS) * g_ref[...]


def _combine(dest_flat, h1, rw, g, y):
    tm = COMB_TM
    return pl.pallas_call(
        _combine_kernel,
        out_shape=jax.ShapeDtypeStruct((SEQ, D_MODEL), F32),
        grid_spec=pltpu.PrefetchScalarGridSpec(
            num_scalar_prefetch=1,
            grid=(SEQ // tm,),
            in_specs=[
                pl.BlockSpec((tm, D_MODEL), lambda i, d: (i, 0)),
                pl.BlockSpec((tm, LANES), lambda i, d: (i, 0)),
                pl.BlockSpec((1, D_MODEL), lambda i, d: (0, 0)),
                pl.BlockSpec(memory_space=pl.ANY),
            ],
            out_specs=pl.BlockSpec((tm, D_MODEL), lambda i, d: (i, 0)),
            scratch_shapes=[
                pltpu.VMEM((2, 2, tm * ROW_TILE, LANES), F32),
                pltpu.SemaphoreType.DMA((2, 2)),
            ],
        ),
        compiler_params=_params(("arbitrary",)),
        name="moe_combine",
    )(dest_flat, h1, rw, g, y)


def _attention_tables():
    T = ATT_T
    f32 = np.float32
    slopes = np.exp2(-ALIBI_MAX * np.arange(1, DA_HEADS + 1, dtype=f32) / DA_HEADS).astype(f32)
    r = np.arange(T)
    hi = ((r // CHUNK) * CHUNK).astype(f32)
    lo = (r % CHUNK).astype(f32)
    sl = slopes[:, None]
    one_h = np.ones((DA_HEADS, T), f32)
    q_rows = np.stack([one_h, one_h, -sl * hi[None], -sl * lo[None]], axis=1)
    k_cols = np.stack([sl * hi[None], sl * lo[None], one_h, one_h], axis=-1)
    qa = np.zeros((DA_HEADS, DA_HEAD_DIM, T), f32)
    qa[:, 0:4, :] = q_rows
    ka1 = np.zeros((DA_HEADS, T, LANES), f32)
    ka1[:, :, DA_HEAD_DIM:DA_HEAD_DIM + 4] = k_cols
    ka2 = np.zeros((DA_HEADS, T, LANES), f32)
    ka2[:, :, 0:4] = k_cols
    rel = (r[:, None] - r[None, :]).astype(f32)
    allowed = (r[:, None] // CHUNK) <= (r[None, :] // CHUNK)
    fix = np.where(rel > 0, -2.0 * slopes[:, None, None] * rel[None], 0.0).astype(f32)
    dtab = np.where(allowed[None], fix, -np.inf).astype(f32)
    return (jnp.asarray(slopes), jnp.asarray(qa, BF16), jnp.asarray(ka1, BF16),
            jnp.asarray(ka2, BF16), jnp.asarray(dtab))


def _retention_tables():
    C = RET_C
    f32 = np.float32
    log_gamma = np.log1p(-np.exp2(-5.0 - np.arange(RET_HEADS, dtype=f32))).astype(f32)
    pos = np.arange(C, dtype=f32)
    rel = pos[:, None] - pos[None, :]
    dec = np.where(rel >= 0, np.exp(log_gamma[:, None, None] * np.maximum(rel, 0.0)), 0.0)
    qdec = np.exp(log_gamma[:, None] * (pos + 1.0)[None, :])[:, :, None]
    kdec = np.exp(log_gamma[:, None] * (C - 1 - pos)[None, :])[:, :, None]
    cd = np.exp(log_gamma * C)
    return tuple(jnp.asarray(t, F32) for t in (cd, dec, qdec, kdec))


def kernel(x, attn_norm_g, w_in, da_lambda_q1, da_lambda_k1, da_lambda_q2, da_lambda_k2,
           da_subln_g, w_out, ffn_norm_g, router_group_w, router_group_b, router_expert_w,
           router_expert_b, expert_w_gate, expert_w_up, expert_w_down, final_norm_g):
    B, S, D = x.shape
    assert (B, S, D) == (1, SEQ, D_MODEL)
    x2 = x.reshape(S, D)

    cd, dec, qdec, kdec = _retention_tables()
    slopes, qa, ka1, ka2, dtab = _attention_tables()
    k_da, qt4, vt4, o_r = _inproj(x2, attn_norm_g[0][None, :], w_in[0], cd, dec, qdec, kdec)

    o_da = _attention(k_da, qt4, vt4, slopes, qa, ka1, ka2, dtab, da_lambda_q1, da_lambda_k1,
                      da_lambda_q2, da_lambda_k2, da_subln_g)

    wr = jnp.zeros((D, LANES), F32)
    wr = wr.at[:, :MOE_GROUPS].set(router_group_w[0])
    wr = wr.at[:, MOE_GROUPS:MOE_GROUPS + MOE_EXPERTS].set(router_expert_w[0])
    br = jnp.zeros((1, LANES), F32)
    br = br.at[0, :MOE_GROUPS].set(router_group_b[0])
    br = br.at[0, MOE_GROUPS:MOE_GROUPS + MOE_EXPERTS].set(router_expert_b[0])
    wr_hi = wr.astype(BF16)
    wr_lo = (wr - wr_hi.astype(F32)).astype(BF16)
    h1, xn, ri, rw, cnt = _outproj_router(x2, o_da, o_r, w_out[0],
                                          ffn_norm_g[0][None, :],
                                          jnp.concatenate([wr_hi, wr_lo], axis=1), br)

    dest, meta = _plan(ri, cnt)
    dest_flat = dest[:, 0:2, :].transpose(1, 0, 2).reshape(N_ASSIGN)
    used1 = meta[0, :1]
    counts = meta[1, :MOE_EXPERTS]
    pad_start = meta[2, :MOE_EXPERTS]
    xs = _dispatch(dest_flat, counts, pad_start, used1, xn)

    y = _ffn(counts, pad_start, used1, xs, expert_w_gate[0], expert_w_up[0], expert_w_down[0])
    out = _combine(dest_flat, h1, rw, final_norm_g[None, :], y)
    return out.reshape(B, S, D)
```

```python
import math

import jax
import jax.numpy as jnp
import numpy as np
from jax import lax
from jax.experimental import pallas as pl
from jax.experimental.pallas import tpu as pltpu

F32 = jnp.float32
BF16 = jnp.bfloat16
I32 = jnp.int32
U32 = jnp.uint32

D_MODEL = 1024
SEQ = 16384
CHUNK = 64
EPS = 1e-6

DA_HEADS = 4
DA_HEAD_DIM = 64
DA_V_DIM = 128
DA_WIDTH = 512
ALIBI_MAX = 8.0
RET_HEADS = 4
RET_QK_DIM = 64
RET_V_DIM = 128
RET_WIDTH = 512
W_IN_COLS = 3072
T_ROWS = 512
MAIN_COLS = 2048
DK_OFF = 0
RQ_OFF = 512
RK_OFF = 768
RV_OFF = 1024
RG_OFF = 1536

MOE_GROUPS = 4
MOE_EXPERTS_PER_GROUP = 8
MOE_EXPERTS = 32
MOE_HIDDEN = 512
LAMBDA_INIT = 0.8 - 0.6 * math.exp(-0.3 * 0)

LANES = 128
ROW_TILE = 8
X_ROWS = 4
VMEM_LIMIT = 56 * 1024 * 1024

PROJ_TM = 512
ROUTER_TM = 1024
ATT_T = 512
RET_C = 256
PLAN_T = 512
FFN_B = 512
N_ASSIGN = 2 * SEQ
N_BLOCKS = N_ASSIGN // FFN_B + MOE_EXPERTS
N_BUF = N_BLOCKS * FFN_B
COMB_TM = 512
DISP_TM = 1024
DMA_UNROLL = 8


def _params(sem):
    return pltpu.CompilerParams(dimension_semantics=sem, vmem_limit_bytes=VMEM_LIMIT)


def _retention_block(q_all, k_all, v_all, g_all, cd_ref, dec_ref, qdec_ref, kdec_ref, st_sc):
    outs = []
    for h in range(RET_HEADS):
        qk = slice(h * RET_QK_DIM, (h + 1) * RET_QK_DIM)
        vv = slice(h * RET_V_DIM, (h + 1) * RET_V_DIM)
        q = q_all[:, qk]
        k = k_all[:, qk]
        v = v_all[:, vv]
        g = g_all[:, vv]
        s = lax.dot_general(q, k, (((1,), (1,)), ((), ())),
                            preferred_element_type=F32) * dec_ref[h]
        intra = jnp.dot(s.astype(BF16), v, preferred_element_type=F32)
        st = st_sc[h]
        cross = jnp.dot(q, st.astype(BF16), preferred_element_type=F32) * qdec_ref[h]
        kd = (k.astype(F32) * kdec_ref[h]).astype(BF16)
        st_sc[h] = st * cd_ref[h] + lax.dot_general(kd, v, (((0,), (0,)), ((), ())),
                                                    preferred_element_type=F32)
        o = intra + cross
        o = o * lax.rsqrt(jnp.mean(o * o, axis=-1, keepdims=True) + EPS)
        outs.append(((g / (1.0 + jnp.exp(-g))) * o).astype(BF16))
    return outs


def _inproj_kernel(cd_ref, x_ref, g_ref, win_ref, dec_ref, qdec_ref, kdec_ref,
                   k_ref, qt_ref, vt_ref, or_ref, st_sc, w_ref, wq_ref, wv_ref):
    @pl.when(pl.program_id(0) == 0)
    def _():
        st_sc[...] = jnp.zeros_like(st_sc)
        cols = DA_WIDTH
        w_ref[:, DK_OFF:RQ_OFF] = win_ref[:, cols:2 * cols].astype(BF16)
        w_ref[:, RQ_OFF:RK_OFF] = win_ref[:, 3 * cols:3 * cols + 256].astype(BF16)
        w_ref[:, RK_OFF:RV_OFF] = (win_ref[:, 3 * cols + 256:4 * cols]
                                   * (RET_QK_DIM ** -0.5)).astype(BF16)
        w_ref[:, RV_OFF:MAIN_COLS] = win_ref[:, 4 * cols:6 * cols].astype(BF16)
        step = 256
        for r in range(D_MODEL // step):
            rows = slice(r * step, (r + 1) * step)
            wq_ref[:, rows] = (win_ref[rows, 0:cols] * (DA_HEAD_DIM ** -0.5)).T.astype(BF16)
            wv_ref[:, rows] = win_ref[rows, 2 * cols:3 * cols].T.astype(BF16)

    x = x_ref[...]
    var = jnp.mean(x * x, axis=-1, keepdims=True)
    xn = (x * lax.rsqrt(var + EPS) * g_ref[...]).astype(BF16)

    def proj(lo, hi):
        return jnp.dot(xn, w_ref[:, lo:hi], preferred_element_type=F32)

    k_ref[...] = proj(DK_OFF, DK_OFF + DA_WIDTH).astype(BF16)
    nt = (((1,), (1,)), ((), ()))
    qt = lax.dot_general(wq_ref[...], xn, nt, preferred_element_type=F32)
    qt_ref[...] = qt.astype(BF16).reshape(DA_HEADS, 1, 2 * DA_HEAD_DIM, PROJ_TM)
    vt = lax.dot_general(wv_ref[...], xn, nt, preferred_element_type=F32)
    vt_ref[...] = vt.astype(BF16).reshape(DA_HEADS, 1, DA_V_DIM, PROJ_TM)

    rq = proj(RQ_OFF, RK_OFF).astype(BF16)
    rk = proj(RK_OFF, RV_OFF).astype(BF16)
    rv = proj(RV_OFF, RG_OFF).astype(BF16)
    rg = proj(RG_OFF, MAIN_COLS)
    for blk in range(PROJ_TM // RET_C):
        rows = slice(blk * RET_C, (blk + 1) * RET_C)
        outs = _retention_block(rq[rows], rk[rows], rv[rows], rg[rows],
                                cd_ref, dec_ref, qdec_ref, kdec_ref, st_sc)
        for h in range(RET_HEADS):
            or_ref[rows, h * RET_V_DIM:(h + 1) * RET_V_DIM] = outs[h]


def _inproj(x2, g, w_in, cd, dec, qdec, kdec):
    C = RET_C
    t_shape = jax.ShapeDtypeStruct((DA_HEADS, SEQ // PROJ_TM, LANES, PROJ_TM), BF16)
    t_spec = pl.BlockSpec((DA_HEADS, 1, LANES, PROJ_TM), lambda i: (0, i, 0, 0))
    return pl.pallas_call(
        _inproj_kernel,
        out_shape=(jax.ShapeDtypeStruct((SEQ, DA_WIDTH), BF16), t_shape, t_shape,
                   jax.ShapeDtypeStruct((SEQ, RET_WIDTH), BF16)),
        grid=(SEQ // PROJ_TM,),
        in_specs=[
            pl.BlockSpec(memory_space=pltpu.SMEM),
            pl.BlockSpec((PROJ_TM, D_MODEL), lambda i: (i, 0)),
            pl.BlockSpec((1, D_MODEL), lambda i: (0, 0)),
            pl.BlockSpec((D_MODEL, W_IN_COLS), lambda i: (0, 0), pipeline_mode=pl.Buffered(1)),
            pl.BlockSpec((RET_HEADS, C, C), lambda i: (0, 0, 0)),
            pl.BlockSpec((RET_HEADS, C, 1), lambda i: (0, 0, 0)),
            pl.BlockSpec((RET_HEADS, C, 1), lambda i: (0, 0, 0)),
        ],
        out_specs=(pl.BlockSpec((PROJ_TM, DA_WIDTH), lambda i: (i, 0)), t_spec, t_spec,
                   pl.BlockSpec((PROJ_TM, RET_WIDTH), lambda i: (i, 0))),
        scratch_shapes=[
            pltpu.VMEM((RET_HEADS, RET_QK_DIM, RET_V_DIM), F32),
            pltpu.VMEM((D_MODEL, MAIN_COLS), BF16),
            pltpu.VMEM((T_ROWS, D_MODEL), BF16),
            pltpu.VMEM((T_ROWS, D_MODEL), BF16),
        ],
        compiler_params=_params(("arbitrary",)),
        name="inproj_retention",
    )(cd, x2, g, w_in, dec, qdec, kdec)


ACC_ROWS = DA_V_DIM + 16


N_QT = SEQ // ATT_T
N_OFF = N_QT * (N_QT - 1) // 2


def _pipeline3(n_pos, scores, accumulate):
    scores(0, 0)
    scores(1, 1)
    steady = n_pos - 2

    def triple(k, carry):
        t = 3 * k
        accumulate(t, 0)
        scores(t + 2, 2)
        accumulate(t + 1, 1)
        scores(t + 3, 0)
        accumulate(t + 2, 2)
        scores(t + 4, 1)
        return carry

    lax.fori_loop(0, steady // 3, triple, 0)
    t0 = steady // 3 * 3
    rem = steady - t0
    accumulate(t0, 0)
    if rem >= 1:
        scores(t0 + 2, 2)
    accumulate(t0 + 1, 1)
    if rem == 2:
        scores(t0 + 3, 0)
    if rem >= 1:
        accumulate(t0 + 2, 2)
    if rem == 2:
        accumulate(t0 + 3, 0)


def _attn_kernel(slope_ref, jt_ref, it_ref, qt_ref, k_ref, vt_ref, qa_ref, ka1_ref, ka2_ref,
                 dtab_ref, lq1_ref, lk1_ref, lq2_ref, lk2_ref, g_ref, o_ref,
                 m_sc, acc_sc, s0_sc, s1_sc, s2_sc, mx0_sc, mx1_sc, mx2_sc):
    T = ATT_T
    h = pl.program_id(0)
    slope = slope_ref[h]
    qa = qa_ref[0]
    lane = lax.broadcasted_iota(I32, (T, LANES), 1)
    sums_row = (lax.broadcasted_iota(I32, (16, T), 0) == 0).astype(BF16)
    s_bufs = (s0_sc, s1_sc, s2_sc)
    mx_bufs = (mx0_sc, mx1_sc, mx2_sc)

    def scores(j, i, buf, extra):
        kt = k_ref[pl.ds(pl.multiple_of(j * T, T), T), :]
        ks = (jnp.where(lane < DA_HEAD_DIM, kt, ka1_ref[0]),
              jnp.where(lane >= DA_HEAD_DIM, kt, ka2_ref[0]))
        qt = qt_ref[0, i]
        qw = (jnp.concatenate([qt[0:DA_HEAD_DIM], qa], axis=0),
              jnp.concatenate([qa, qt[DA_HEAD_DIM:]], axis=0))
        for mp in range(2):
            s = jnp.dot(ks[mp], qw[mp], preferred_element_type=F32)
            if extra is not None:
                s = s + extra[0]
            s_bufs[buf][mp] = s
            mx_bufs[buf][mp] = jnp.max(s, axis=0, keepdims=True)

    def accumulate(j, i, buf):
        c = slope * lax.convert_element_type((i - j) * T, F32)
        vte = jnp.concatenate([vt_ref[0, j], sums_row], axis=0)
        for mp in range(2):
            m_prev = m_sc[i, mp]
            m_new = jnp.maximum(m_prev, mx_bufs[buf][mp] - c)
            p = jnp.exp(s_bufs[buf][mp] - (m_new + c)).astype(BF16)
            pv = jnp.dot(vte, p, preferred_element_type=F32)
            acc_sc[i, mp] = jnp.exp(m_prev - m_new) * acc_sc[i, mp] + pv
            m_sc[i, mp] = m_new

    m_sc[...] = jnp.full_like(m_sc, -jnp.inf)
    acc_sc[...] = jnp.zeros_like(acc_sc)
    _pipeline3(N_QT,
               lambda pos, buf: scores(pos, pos, buf, dtab_ref),
               lambda pos, buf: accumulate(pos, pos, buf))
    _pipeline3(N_OFF,
               lambda pos, buf: scores(jt_ref[pos], it_ref[pos], buf, None),
               lambda pos, buf: accumulate(jt_ref[pos], it_ref[pos], buf))

    lam = (jnp.exp(jnp.sum(lq1_ref[...] * lk1_ref[...], axis=1, keepdims=True))
           - jnp.exp(jnp.sum(lq2_ref[...] * lk2_ref[...], axis=1, keepdims=True))
           + LAMBDA_INIT)

    def finish(i):
        a1 = acc_sc[i, 0]
        a2 = acc_sc[i, 1]
        ot = (a1[0:DA_V_DIM] / a1[DA_V_DIM:DA_V_DIM + 1]
              - lam * (a2[0:DA_V_DIM] / a2[DA_V_DIM:DA_V_DIM + 1]))
        o = ot.T
        var = jnp.mean(o * o, axis=-1, keepdims=True)
        o = (o * lax.rsqrt(var + EPS) * g_ref[...]) * (1.0 - LAMBDA_INIT)
        o_ref[pl.ds(pl.multiple_of(i * T, T), T), :] = o.astype(BF16)

    def finish_pair(k, carry):
        finish(2 * k)
        finish(2 * k + 1)
        return carry

    lax.fori_loop(0, N_QT // 2, finish_pair, 0)


def _attention(proj, qt4, vt4, slopes, qa, ka1, ka2, dtab, lq1, lk1, lq2, lk2, subln_g):
    T = ATT_T
    vec64 = pl.BlockSpec((1, DA_HEAD_DIM), lambda h: (0, 0))
    per_head = lambda a, b: pl.BlockSpec((1, a, b), lambda h: (h, 0, 0))
    slab = pl.BlockSpec
    single = lambda shape, imap: pl.BlockSpec(shape, imap, pipeline_mode=pl.Buffered(1))
    smem = pl.BlockSpec(memory_space=pltpu.SMEM)
    it_tab, jt_tab = np.tril_indices(N_QT, -1)
    return pl.pallas_call(
        _attn_kernel,
        out_shape=jax.ShapeDtypeStruct((SEQ, DA_WIDTH), BF16),
        grid=(DA_HEADS,),
        in_specs=[
            smem, smem, smem,
            slab((1, N_QT, LANES, T), lambda h: (h, 0, 0, 0)),
            slab((SEQ, LANES), lambda h: (0, DK_OFF // LANES + h)),
            slab((1, N_QT, LANES, T), lambda h: (h, 0, 0, 0)),
            per_head(DA_HEAD_DIM, T), per_head(T, LANES), per_head(T, LANES),
            single((1, T, T), lambda h: (h, 0, 0)),
            vec64, vec64, vec64, vec64,
            pl.BlockSpec((1, DA_V_DIM), lambda h: (0, 0)),
        ],
        out_specs=single((SEQ, LANES), lambda h: (0, h)),
        scratch_shapes=[
            pltpu.VMEM((N_QT, 2, 1, T), F32),
            pltpu.VMEM((N_QT, 2, ACC_ROWS, T), F32),
            pltpu.VMEM((2, T, T), F32),
            pltpu.VMEM((2, T, T), F32),
            pltpu.VMEM((2, T, T), F32),
            pltpu.VMEM((2, 1, T), F32),
            pltpu.VMEM((2, 1, T), F32),
            pltpu.VMEM((2, 1, T), F32),
        ],
        compiler_params=_params(("arbitrary",)),
        name="diff_attention",
    )(slopes, jnp.asarray(jt_tab, I32), jnp.asarray(it_tab, I32), qt4, proj, vt4, qa, ka1, ka2,
      dtab, lq1, lk1, lq2, lk2, subln_g)


def _outproj_router_kernel(x_ref, oda_ref, or_ref, wo32_ref, g_ref, wr_ref, br_ref,
                           h1_ref, xn_ref, ri_ref, rw_ref, cnt_ref, wo_ref):
    @pl.when(pl.program_id(0) == 0)
    def _():
        wo_ref[...] = wo32_ref[...].astype(BF16)

    h1 = (x_ref[...]
          + jnp.dot(oda_ref[...], wo_ref[0:DA_WIDTH, :], preferred_element_type=F32)
          + jnp.dot(or_ref[...], wo_ref[DA_WIDTH:, :], preferred_element_type=F32))
    h1_ref[...] = h1
    var = jnp.mean(h1 * h1, axis=-1, keepdims=True)
    xn = h1 * lax.rsqrt(var + EPS) * g_ref[...]
    bits = lax.bitcast_convert_type(xn.astype(BF16).astype(F32), U32)
    for c in range(X_ROWS):
        lo = bits[:, c * LANES:(c + 1) * LANES] >> 16
        hi = bits[:, (c + X_ROWS) * LANES:(c + X_ROWS + 1) * LANES] & jnp.uint32(0xFFFF0000)
        xn_ref[pl.ds(c, ROUTER_TM, stride=X_ROWS), :] = lo | hi
    x_hi = xn.astype(BF16)
    x_lo = (xn - x_hi.astype(F32)).astype(BF16)
    both = jnp.dot(x_hi, wr_ref[...], preferred_element_type=F32)
    logits = (both[:, :LANES] + both[:, LANES:]
              + jnp.dot(x_lo, wr_ref[:, :LANES], preferred_element_type=F32)) + br_ref[...]
    lane = lax.broadcasted_iota(I32, logits.shape, 1)
    neg = jnp.float32(-jnp.inf)
    big = jnp.int32(1 << 20)
    gl = jnp.where(lane < MOE_GROUPS, logits, neg)
    gmax = jnp.max(gl, axis=1, keepdims=True)
    gidx = jnp.min(jnp.where(gl == gmax, lane, big), axis=1, keepdims=True)
    gsum = jnp.sum(jnp.exp(gl - gmax), axis=1, keepdims=True)
    gp = 1.0 / gsum
    lo = MOE_GROUPS + gidx * MOE_EXPERTS_PER_GROUP
    el = jnp.where((lane >= lo) & (lane < lo + MOE_EXPERTS_PER_GROUP), logits, neg)
    v1 = jnp.max(el, axis=1, keepdims=True)
    i1 = jnp.min(jnp.where(el == v1, lane, big), axis=1, keepdims=True)
    el2 = jnp.where(lane == i1, neg, el)
    v2 = jnp.max(el2, axis=1, keepdims=True)
    i2 = jnp.min(jnp.where(el2 == v2, lane, big), axis=1, keepdims=True)
    t = jnp.exp(v2 - v1)
    w1 = gp / (1.0 + t)
    w2 = gp * t / (1.0 + t)
    ri_ref[...] = jnp.where(lane == 0, i1 - MOE_GROUPS,
                            jnp.where(lane == 1, i2 - MOE_GROUPS, 0))
    rw_ref[...] = jnp.where(lane == 0, w1, jnp.where(lane == 1, w2, 0.0))

    @pl.when(pl.program_id(0) == 0)
    def _():
        cnt_ref[...] = jnp.zeros_like(cnt_ref)

    chosen = (lane == i1 - MOE_GROUPS) | (lane == i2 - MOE_GROUPS)
    cnt_ref[...] += jnp.sum(chosen.astype(F32), axis=0, keepdims=True)


def _outproj_router(x2, o_da, o_r, w_out, g, wr, br):
    tm = ROUTER_TM
    row = lambda w: pl.BlockSpec((tm, w), lambda i: (i, 0))
    full = lambda a, b: pl.BlockSpec((a, b), lambda i: (0, 0))
    return pl.pallas_call(
        _outproj_router_kernel,
        out_shape=(
            jax.ShapeDtypeStruct((SEQ, D_MODEL), F32),
            jax.ShapeDtypeStruct((SEQ * X_ROWS, LANES), U32),
            jax.ShapeDtypeStruct((SEQ, LANES), I32),
            jax.ShapeDtypeStruct((SEQ, LANES), F32),
            jax.ShapeDtypeStruct((8, LANES), F32),
        ),
        grid=(SEQ // tm,),
        in_specs=[row(D_MODEL), row(DA_WIDTH), row(RET_WIDTH),
                  pl.BlockSpec((D_MODEL, D_MODEL), lambda i: (0, 0),
                               pipeline_mode=pl.Buffered(1)),
                  full(1, D_MODEL), full(D_MODEL, 2 * LANES), full(1, LANES)],
        out_specs=(row(D_MODEL), pl.BlockSpec((tm * X_ROWS, LANES), lambda i: (i, 0)),
                   row(LANES), row(LANES), full(8, LANES)),
        scratch_shapes=[pltpu.VMEM((D_MODEL, D_MODEL), BF16)],
        compiler_params=_params(("arbitrary",)),
        name="outproj_router",
    )(x2, o_da, o_r, w_out, g, wr, br)


def _plan_kernel(ri_ref, cnt_ref, dest_ref, used_ref):
    TT = PLAN_T
    lane = lax.broadcasted_iota(I32, (TT, LANES), 1)

    def onehots(t):
        r = ri_ref[pl.ds(pl.multiple_of(t * TT, TT), TT), :]
        return lane == r[:, 0:1], lane == r[:, 1:2]

    counts8 = cnt_ref[...].astype(I32)
    shift = FFN_B.bit_length() - 1
    padded = ((counts8 + (FFN_B - 1)) >> shift) << shift
    lane8 = lax.broadcasted_iota(I32, (8, LANES), 1)
    pad_end = padded
    sh = 1
    while sh < LANES:
        pad_end = pad_end + jnp.where(lane8 >= sh, pltpu.roll(pad_end, sh, axis=1), 0)
        sh *= 2
    pad_start = pad_end - padded

    ltri = (lax.broadcasted_iota(I32, (TT, TT), 0)
            > lax.broadcasted_iota(I32, (TT, TT), 1)).astype(BF16)

    def dest_body(t, carry):
        oh1, oh2 = onehots(t)
        a = (oh1 | oh2).astype(F32)
        base = jnp.dot(ltri, a.astype(BF16), preferred_element_type=F32) + carry
        d1 = jnp.sum(jnp.where(oh1, base, 0.0), axis=1, keepdims=True)
        d2 = jnp.sum(jnp.where(oh2, base, 0.0), axis=1, keepdims=True)
        both = jnp.where(lane == 0, d1, jnp.where(lane == 1, d2, 0.0))
        dest_ref[t] = both.T[0:8, :].astype(I32)
        return carry + jnp.sum(a, axis=0, keepdims=True)

    lax.fori_loop(0, SEQ // TT, dest_body, pad_start[0:1].astype(F32))

    total = jnp.max(pad_end, axis=1, keepdims=True)
    row8 = lax.broadcasted_iota(I32, (8, LANES), 0)
    used_ref[...] = jnp.where(row8 == 0, jnp.broadcast_to(total >> shift, (8, LANES)),
                              jnp.where(row8 == 1, counts8, pad_start))


def _plan(ri, cnt):
    return pl.pallas_call(
        _plan_kernel,
        out_shape=(
            jax.ShapeDtypeStruct((SEQ // PLAN_T, 8, PLAN_T), I32),
            jax.ShapeDtypeStruct((8, LANES), I32),
        ),
        compiler_params=pltpu.CompilerParams(vmem_limit_bytes=VMEM_LIMIT),
        name="route_plan",
    )(ri, cnt)


PAD_BITS = FFN_B.bit_length() - 1


def _pad_fill_copies(e, cnt_ref, pst_ref, zero_sc, xs_hbm, zsem):
    cnt = cnt_ref[e]
    pad = (-cnt) & (FFN_B - 1)
    row = pst_ref[e] + cnt
    out = []
    for bit in reversed(range(PAD_BITS)):
        n = 1 << bit
        start = row + ((pad >> (bit + 1)) << (bit + 1))
        copy = pltpu.make_async_copy(
            zero_sc.at[pl.ds(0, n * X_ROWS)],
            xs_hbm.at[pl.ds(pl.multiple_of(start * X_ROWS, X_ROWS), n * X_ROWS)], zsem)
        out.append(((pad & n) != 0, copy))
    return out


def _unused_block_copies(b, zero_sc, xs_hbm, zsem):
    half = FFN_B // 2 * X_ROWS
    return [pltpu.make_async_copy(
        zero_sc, xs_hbm.at[pl.ds(pl.multiple_of((2 * b + k) * half, half), half)], zsem)
        for k in range(2)]


def _dispatch_kernel(dest_ref, cnt_ref, pst_ref, used_ref, xn_ref, xs_hbm, zero_sc, sem, zsem):
    tm = DISP_TM
    i = pl.program_id(0)

    @pl.when(i == 0)
    def _():
        zero_sc[...] = jnp.zeros_like(zero_sc)

        def fill(e, carry):
            for cond, copy in _pad_fill_copies(e, cnt_ref, pst_ref, zero_sc, xs_hbm, zsem):
                pl.when(cond)(copy.start)
            return carry

        lax.fori_loop(0, MOE_EXPERTS, fill, 0)

        def fill_block(b, carry):
            for copy in _unused_block_copies(b, zero_sc, xs_hbm, zsem):
                copy.start()
            return carry

        lax.fori_loop(used_ref[0], N_BLOCKS, fill_block, 0)

    def issue(it, carry):
        for u in range(DMA_UNROLL):
            r = it * DMA_UNROLL + u
            t = i * tm + r
            src = xn_ref.at[pl.ds(pl.multiple_of(r * X_ROWS, X_ROWS), X_ROWS)]
            for kk in range(2):
                d = pl.multiple_of(dest_ref[kk * SEQ + t] * X_ROWS, X_ROWS)
                pltpu.make_async_copy(src, xs_hbm.at[pl.ds(d, X_ROWS)], sem).start(priority=kk)
        return carry

    lax.fori_loop(0, tm // DMA_UNROLL, issue, 0)
    for _ in range(2):
        pltpu.make_async_copy(xn_ref, xs_hbm.at[pl.ds(0, tm * X_ROWS)], sem).wait()

    @pl.when(i == 0)
    def _():
        def drain(e, carry):
            for cond, copy in _pad_fill_copies(e, cnt_ref, pst_ref, zero_sc, xs_hbm, zsem):
                pl.when(cond)(copy.wait)
            return carry

        lax.fori_loop(0, MOE_EXPERTS, drain, 0)

        def drain_block(b, carry):
            for copy in _unused_block_copies(b, zero_sc, xs_hbm, zsem):
                copy.wait()
            return carry

        lax.fori_loop(used_ref[0], N_BLOCKS, drain_block, 0)


def _dispatch(dest_flat, counts, pad_start, used, xn3):
    tm = DISP_TM
    return pl.pallas_call(
        _dispatch_kernel,
        out_shape=jax.ShapeDtypeStruct((N_BUF * X_ROWS, LANES), U32),
        grid_spec=pltpu.PrefetchScalarGridSpec(
            num_scalar_prefetch=4,
            grid=(SEQ // tm,),
            in_specs=[pl.BlockSpec((tm * X_ROWS, LANES), lambda i, d, c, p, u: (i, 0))],
            out_specs=pl.BlockSpec(memory_space=pl.ANY),
            scratch_shapes=[
                pltpu.VMEM((FFN_B // 2 * X_ROWS, LANES), U32),
                pltpu.SemaphoreType.DMA(()),
                pltpu.SemaphoreType.DMA(()),
            ],
        ),
        compiler_params=_params(("arbitrary",)),
        name="moe_dispatch",
    )(dest_flat, counts, pad_start, used, xn3)


BLOCK_COPY_PRIORITY = 1


def _ffn_kernel(cnt_ref, pst_ref, used_ref, xs_hbm, wg_ref, wu_ref, wd_ref, y_hbm,
                xbuf, ybuf, zero_sc, wg_bf, wu_bf, wd_bf, sem_in, sem_out, zsem):
    B = FFN_B
    e = pl.program_id(0)
    n = (cnt_ref[e] + (B - 1)) >> PAD_BITS
    s0 = pst_ref[e] >> PAD_BITS

    def rows(blk, per_token):
        size = B * per_token
        return pl.ds(pl.multiple_of(blk * size, size), size)

    def fetch(blk, slot):
        return pltpu.make_async_copy(xs_hbm.at[rows(blk, X_ROWS)], xbuf.at[slot],
                                     sem_in.at[slot])

    def flush(blk, slot):
        return pltpu.make_async_copy(ybuf.at[slot], y_hbm.at[rows(blk, ROW_TILE)],
                                     sem_out.at[slot])

    used = used_ref[0]

    @pl.when(e == 0)
    def _():
        fetch(0, 0).start(priority=BLOCK_COPY_PRIORITY)

    @pl.when(n > 0)
    def _():
        wg_bf[...] = wg_ref[0].astype(BF16)
        wu_bf[...] = wu_ref[0].astype(BF16)
        wd_bf[...] = wd_ref[0].astype(BF16)

    def body(j, carry):
        blk = s0 + j
        slot = blk % 2

        @pl.when(blk + 1 < used)
        def _():
            fetch(blk + 1, 1 - slot).start(priority=BLOCK_COPY_PRIORITY)

        fetch(blk, slot).wait()

        @pl.when(blk >= 2)
        def _():
            flush(blk - 2, slot).wait()

        words = [xbuf[slot, pl.ds(c, B, stride=X_ROWS), :] for c in range(X_ROWS)]
        lo = [lax.bitcast_convert_type(w << 16, F32) for w in words]
        hi = [lax.bitcast_convert_type(w & jnp.uint32(0xFFFF0000), F32) for w in words]
        x = jnp.concatenate(lo + hi, axis=1).astype(BF16)
        hg = jnp.dot(x, wg_bf[...], preferred_element_type=F32)
        hu = jnp.dot(x, wu_bf[...], preferred_element_type=F32)
        hh = ((hg / (1.0 + jnp.exp(-hg))) * hu).astype(BF16)
        y = jnp.dot(hh, wd_bf[...], preferred_element_type=F32)
        for c in range(ROW_TILE):
            ybuf[slot, pl.ds(c, B, stride=ROW_TILE), :] = y[:, c * LANES:(c + 1) * LANES]
        flush(blk, slot).start(priority=BLOCK_COPY_PRIORITY)
        return carry

    lax.fori_loop(0, n, body, 0)

    @pl.when(e == MOE_EXPERTS - 1)
    def _():
        flush(used - 1, (used + 1) % 2).wait()
        flush(used - 2, used % 2).wait()
        zero_sc[...] = jnp.zeros_like(zero_sc)

        def fill(b, carry):
            pltpu.make_async_copy(zero_sc, y_hbm.at[rows(b, ROW_TILE)], zsem).start()
            return carry

        def drain(b, carry):
            pltpu.make_async_copy(zero_sc, y_hbm.at[rows(b, ROW_TILE)], zsem).wait()
            return carry

        lax.fori_loop(used_ref[0], N_BLOCKS, fill, 0)
        lax.fori_loop(used_ref[0], N_BLOCKS, drain, 0)


def _ffn(counts, pad_start, used, xs, w_gate, w_up, w_down):
    B = FFN_B
    wspec = lambda a, c: pl.BlockSpec((1, a, c), lambda e, cnt, pst, used: (e, 0, 0))
    return pl.pallas_call(
        _ffn_kernel,
        out_shape=jax.ShapeDtypeStruct((N_BUF * ROW_TILE, LANES), F32),
        grid_spec=pltpu.PrefetchScalarGridSpec(
            num_scalar_prefetch=3,
            grid=(MOE_EXPERTS,),
            in_specs=[
                pl.BlockSpec(memory_space=pl.ANY),
                wspec(D_MODEL, MOE_HIDDEN),
                wspec(D_MODEL, MOE_HIDDEN),
                wspec(MOE_HIDDEN, D_MODEL),
            ],
            out_specs=pl.BlockSpec(memory_space=pl.ANY),
            scratch_shapes=[
                pltpu.VMEM((2, B * X_ROWS, LANES), U32),
                pltpu.VMEM((2, B * ROW_TILE, LANES), F32),
                pltpu.VMEM((B * ROW_TILE, LANES), F32),
                pltpu.VMEM((D_MODEL, MOE_HIDDEN), BF16),
                pltpu.VMEM((D_MODEL, MOE_HIDDEN), BF16),
                pltpu.VMEM((MOE_HIDDEN, D_MODEL), BF16),
                pltpu.SemaphoreType.DMA((2,)),
                pltpu.SemaphoreType.DMA((2,)),
                pltpu.SemaphoreType.DMA(()),
            ],
        ),
        compiler_params=_params(("arbitrary",)),
        name="expert_ffn",
    )(counts, pad_start, used, xs, w_gate, w_up, w_down)


def _combine_kernel(dest_ref, h1_ref, rw_ref, g_ref, y_hbm, o_ref, ybuf, sem):
    tm = COMB_TM
    i = pl.program_id(0)

    def gather(tile, slot):
        def issue(it, carry):
            for u in range(DMA_UNROLL):
                r = it * DMA_UNROLL + u
                t = tile * tm + r
                for kk in range(2):
                    d = pl.multiple_of(dest_ref[kk * SEQ + t] * ROW_TILE, ROW_TILE)
                    pltpu.make_async_copy(
                        y_hbm.at[pl.ds(d, ROW_TILE)],
                        ybuf.at[slot, kk, pl.ds(pl.multiple_of(r * ROW_TILE, ROW_TILE), ROW_TILE)],
                        sem.at[slot, kk]).start(priority=kk)
            return carry

        lax.fori_loop(0, tm // DMA_UNROLL, issue, 0)

    @pl.when(i == 0)
    def _():
        gather(0, 0)

    @pl.when(i + 1 < pl.num_programs(0))
    def _():
        gather(i + 1, (i + 1) % 2)

    slot = i % 2
    for kk in range(2):
        pltpu.make_async_copy(y_hbm.at[pl.ds(0, tm * ROW_TILE)], ybuf.at[slot, kk],
                              sem.at[slot, kk]).wait()
    w = rw_ref[...]
    ys = [jnp.concatenate([ybuf[slot, kk, pl.ds(c, tm, stride=ROW_TILE), :]
                           for c in range(ROW_TILE)], axis=1) for kk in range(2)]
    h = h1_ref[...] + w[:, 0:1] * ys[0] + w[:, 1:2] * ys[1]
    var = jnp.mean(h * h, axis=-1, keepdims=True)
    o_ref[...] = h * lax.rsqrt(var + EPS) * g_ref[...]


def _combine(dest_flat, h1, rw, g, y):
    tm = COMB_TM
    return pl.pallas_call(
        _combine_kernel,
        out_shape=jax.ShapeDtypeStruct((SEQ, D_MODEL), F32),
        grid_spec=pltpu.PrefetchScalarGridSpec(
            num_scalar_prefetch=1,
            grid=(SEQ // tm,),
            in_specs=[
                pl.BlockSpec((tm, D_MODEL), lambda i, d: (i, 0)),
                pl.BlockSpec((tm, LANES), lambda i, d: (i, 0)),
                pl.BlockSpec((1, D_MODEL), lambda i, d: (0, 0)),
                pl.BlockSpec(memory_space=pl.ANY),
            ],
            out_specs=pl.BlockSpec((tm, D_MODEL), lambda i, d: (i, 0)),
            scratch_shapes=[
                pltpu.VMEM((2, 2, tm * ROW_TILE, LANES), F32),
                pltpu.SemaphoreType.DMA((2, 2)),
            ],
        ),
        compiler_params=_params(("arbitrary",)),
        name="moe_combine",
    )(dest_flat, h1, rw, g, y)


def _attention_tables():
    T = ATT_T
    f32 = np.float32
    slopes = np.exp2(-ALIBI_MAX * np.arange(1, DA_HEADS + 1, dtype=f32) / DA_HEADS).astype(f32)
    r = np.arange(T)
    hi = ((r // CHUNK) * CHUNK).astype(f32)
    lo = (r % CHUNK).astype(f32)
    sl = slopes[:, None]
    one_h = np.ones((DA_HEADS, T), f32)
    q_rows = np.stack([one_h, one_h, -sl * hi[None], -sl * lo[None]], axis=1)
    k_cols = np.stack([sl * hi[None], sl * lo[None], one_h, one_h], axis=-1)
    qa = np.zeros((DA_HEADS, DA_HEAD_DIM, T), f32)
    qa[:, 0:4, :] = q_rows
    ka1 = np.zeros((DA_HEADS, T, LANES), f32)
    ka1[:, :, DA_HEAD_DIM:DA_HEAD_DIM + 4] = k_cols
    ka2 = np.zeros((DA_HEADS, T, LANES), f32)
    ka2[:, :, 0:4] = k_cols
    rel = (r[:, None] - r[None, :]).astype(f32)
    allowed = (r[:, None] // CHUNK) <= (r[None, :] // CHUNK)
    fix = np.where(rel > 0, -2.0 * slopes[:, None, None] * rel[None], 0.0).astype(f32)
    dtab = np.where(allowed[None], fix, -np.inf).astype(f32)
    return (jnp.asarray(slopes), jnp.asarray(qa, BF16), jnp.asarray(ka1, BF16),
            jnp.asarray(ka2, BF16), jnp.asarray(dtab))


def _retention_tables():
    C = RET_C
    f32 = np.float32
    log_gamma = np.log1p(-np.exp2(-5.0 - np.arange(RET_HEADS, dtype=f32))).astype(f32)
    pos = np.arange(C, dtype=f32)
    rel = pos[:, None] - pos[None, :]
    dec = np.where(rel >= 0, np.exp(log_gamma[:, None, None] * np.maximum(rel, 0.0)), 0.0)
    qdec = np.exp(log_gamma[:, None] * (pos + 1.0)[None, :])[:, :, None]
    kdec = np.exp(log_gamma[:, None] * (C - 1 - pos)[None, :])[:, :, None]
    cd = np.exp(log_gamma * C)
    return tuple(jnp.asarray(t, F32) for t in (cd, dec, qdec, kdec))


def kernel(x, attn_norm_g, w_in, da_lambda_q1, da_lambda_k1, da_lambda_q2, da_lambda_k2,
           da_subln_g, w_out, ffn_norm_g, router_group_w, router_group_b, router_expert_w,
           router_expert_b, expert_w_gate, expert_w_up, expert_w_down, final_norm_g):
    B, S, D = x.shape
    assert (B, S, D) == (1, SEQ, D_MODEL)
    x2 = x.reshape(S, D)

    cd, dec, qdec, kdec = _retention_tables()
    slopes, qa, ka1, ka2, dtab = _attention_tables()
    k_da, qt4, vt4, o_r = _inproj(x2, attn_norm_g[0][None, :], w_in[0], cd, dec, qdec, kdec)

    o_da = _attention(k_da, qt4, vt4, slopes, qa, ka1, ka2, dtab, da_lambda_q1, da_lambda_k1,
                      da_lambda_q2, da_lambda_k2, da_subln_g)

    wr = jnp.zeros((D, LANES), F32)
    wr = wr.at[:, :MOE_GROUPS].set(router_group_w[0])
    wr = wr.at[:, MOE_GROUPS:MOE_GROUPS + MOE_EXPERTS].set(router_expert_w[0])
    br = jnp.zeros((1, LANES), F32)
    br = br.at[0, :MOE_GROUPS].set(router_group_b[0])
    br = br.at[0, MOE_GROUPS:MOE_GROUPS + MOE_EXPERTS].set(router_expert_b[0])
    wr_hi = wr.astype(BF16)
    wr_lo = (wr - wr_hi.astype(F32)).astype(BF16)
    h1, xn, ri, rw, cnt = _outproj_router(x2, o_da, o_r, w_out[0],
                                          ffn_norm_g[0][None, :],
                                          jnp.concatenate([wr_hi, wr_lo], axis=1), br)

    dest, meta = _plan(ri, cnt)
    dest_flat = dest[:, 0:2, :].transpose(1, 0, 2).reshape(N_ASSIGN)
    used1 = meta[0, :1]
    counts = meta[1, :MOE_EXPERTS]
    pad_start = meta[2, :MOE_EXPERTS]
    xs = _dispatch(dest_flat, counts, pad_start, used1, xn)

    y = _ffn(counts, pad_start, used1, xs, expert_w_gate[0], expert_w_up[0], expert_w_down[0])
    out = _combine(dest_flat, h1, rw, final_norm_g[None, :], y)
    return out.reshape(B, S, D)
```

```python
import math

import jax
import jax.numpy as jnp
import numpy as np
from jax import lax
from jax.experimental import pallas as pl
from jax.experimental.pallas import tpu as pltpu

F32 = jnp.float32
BF16 = jnp.bfloat16
I32 = jnp.int32
U32 = jnp.uint32

D_MODEL = 1024
SEQ = 16384
CHUNK = 64
EPS = 1e-6

DA_HEADS = 4
DA_HEAD_DIM = 64
DA_V_DIM = 128
DA_WIDTH = 512
ALIBI_MAX = 8.0
RET_HEADS = 4
RET_QK_DIM = 64
RET_V_DIM = 128
RET_WIDTH = 512
W_IN_COLS = 3072
T_ROWS = 512
MAIN_COLS = 2048
DK_OFF = 0
RQ_OFF = 512
RK_OFF = 768
RV_OFF = 1024
RG_OFF = 1536

MOE_GROUPS = 4
MOE_EXPERTS_PER_GROUP = 8
MOE_EXPERTS = 32
MOE_HIDDEN = 512
LAMBDA_INIT = 0.8 - 0.6 * math.exp(-0.3 * 0)

LANES = 128
X_ROWS = 4
VMEM_LIMIT = 56 * 1024 * 1024

PROJ_TM = 512
ROUTER_TM = 1024
ATT_T = 512
RET_C = 256
PLAN_T = 512
FFN_B = 512
N_ASSIGN = 2 * SEQ
N_BLOCKS = N_ASSIGN // FFN_B + MOE_EXPERTS
N_BUF = N_BLOCKS * FFN_B
COMB_TM = 512
DISP_TM = 1024
DMA_UNROLL = 8


def _params(sem):
    return pltpu.CompilerParams(dimension_semantics=sem, vmem_limit_bytes=VMEM_LIMIT)


def _pack_rows(v):
    bits = lax.bitcast_convert_type(v.astype(BF16).astype(F32), U32)
    return [(bits[:, c * LANES:(c + 1) * LANES] >> 16)
            | (bits[:, (c + X_ROWS) * LANES:(c + X_ROWS + 1) * LANES] & jnp.uint32(0xFFFF0000))
            for c in range(X_ROWS)]


def _unpack_rows(words):
    lo = [lax.bitcast_convert_type(w << 16, F32) for w in words]
    hi = [lax.bitcast_convert_type(w & jnp.uint32(0xFFFF0000), F32) for w in words]
    return jnp.concatenate(lo + hi, axis=1)


def _retention_block(q_all, k_all, v_all, g_all, cd_ref, dec_ref, qdec_ref, kdec_ref, st_sc):
    outs = []
    for h in range(RET_HEADS):
        qk = slice(h * RET_QK_DIM, (h + 1) * RET_QK_DIM)
        vv = slice(h * RET_V_DIM, (h + 1) * RET_V_DIM)
        q = q_all[:, qk]
        k = k_all[:, qk]
        v = v_all[:, vv]
        g = g_all[:, vv]
        s = lax.dot_general(q, k, (((1,), (1,)), ((), ())),
                            preferred_element_type=F32) * dec_ref[h]
        intra = jnp.dot(s.astype(BF16), v, preferred_element_type=F32)
        st = st_sc[h]
        cross = jnp.dot(q, st.astype(BF16), preferred_element_type=F32) * qdec_ref[h]
        kd = (k.astype(F32) * kdec_ref[h]).astype(BF16)
        st_sc[h] = st * cd_ref[h] + lax.dot_general(kd, v, (((0,), (0,)), ((), ())),
                                                    preferred_element_type=F32)
        o = intra + cross
        o = o * lax.rsqrt(jnp.mean(o * o, axis=-1, keepdims=True) + EPS)
        outs.append(((g / (1.0 + jnp.exp(-g))) * o).astype(BF16))
    return outs


def _inproj_kernel(cd_ref, x_ref, g_ref, win_ref, dec_ref, qdec_ref, kdec_ref,
                   k_ref, qt_ref, vt_ref, or_ref, st_sc, w_ref, wq_ref, wv_ref):
    @pl.when(pl.program_id(0) == 0)
    def _():
        st_sc[...] = jnp.zeros_like(st_sc)
        cols = DA_WIDTH
        w_ref[:, DK_OFF:RQ_OFF] = win_ref[:, cols:2 * cols].astype(BF16)
        w_ref[:, RQ_OFF:RK_OFF] = win_ref[:, 3 * cols:3 * cols + 256].astype(BF16)
        w_ref[:, RK_OFF:RV_OFF] = (win_ref[:, 3 * cols + 256:4 * cols]
                                   * (RET_QK_DIM ** -0.5)).astype(BF16)
        w_ref[:, RV_OFF:MAIN_COLS] = win_ref[:, 4 * cols:6 * cols].astype(BF16)
        step = 256
        for r in range(D_MODEL // step):
            rows = slice(r * step, (r + 1) * step)
            wq_ref[:, rows] = (win_ref[rows, 0:cols] * (DA_HEAD_DIM ** -0.5)).T.astype(BF16)
            wv_ref[:, rows] = win_ref[rows, 2 * cols:3 * cols].T.astype(BF16)

    x = x_ref[...]
    var = jnp.mean(x * x, axis=-1, keepdims=True)
    xn = (x * lax.rsqrt(var + EPS) * g_ref[...]).astype(BF16)

    def proj(lo, hi):
        return jnp.dot(xn, w_ref[:, lo:hi], preferred_element_type=F32)

    k_ref[...] = proj(DK_OFF, DK_OFF + DA_WIDTH).astype(BF16)
    nt = (((1,), (1,)), ((), ()))
    qt = lax.dot_general(wq_ref[...], xn, nt, preferred_element_type=F32)
    qt_ref[...] = qt.astype(BF16).reshape(DA_HEADS, 1, 2 * DA_HEAD_DIM, PROJ_TM)
    vt = lax.dot_general(wv_ref[...], xn, nt, preferred_element_type=F32)
    vt_ref[...] = vt.astype(BF16).reshape(DA_HEADS, 1, DA_V_DIM, PROJ_TM)

    rq = proj(RQ_OFF, RK_OFF).astype(BF16)
    rk = proj(RK_OFF, RV_OFF).astype(BF16)
    rv = proj(RV_OFF, RG_OFF).astype(BF16)
    rg = proj(RG_OFF, MAIN_COLS)
    for blk in range(PROJ_TM // RET_C):
        rows = slice(blk * RET_C, (blk + 1) * RET_C)
        outs = _retention_block(rq[rows], rk[rows], rv[rows], rg[rows],
                                cd_ref, dec_ref, qdec_ref, kdec_ref, st_sc)
        for h in range(RET_HEADS):
            or_ref[rows, h * RET_V_DIM:(h + 1) * RET_V_DIM] = outs[h]


def _inproj(x2, g, w_in, cd, dec, qdec, kdec):
    C = RET_C
    t_shape = jax.ShapeDtypeStruct((DA_HEADS, SEQ // PROJ_TM, LANES, PROJ_TM), BF16)
    t_spec = pl.BlockSpec((DA_HEADS, 1, LANES, PROJ_TM), lambda i: (0, i, 0, 0))
    return pl.pallas_call(
        _inproj_kernel,
        out_shape=(jax.ShapeDtypeStruct((SEQ, DA_WIDTH), BF16), t_shape, t_shape,
                   jax.ShapeDtypeStruct((SEQ, RET_WIDTH), BF16)),
        grid=(SEQ // PROJ_TM,),
        in_specs=[
            pl.BlockSpec(memory_space=pltpu.SMEM),
            pl.BlockSpec((PROJ_TM, D_MODEL), lambda i: (i, 0)),
            pl.BlockSpec((1, D_MODEL), lambda i: (0, 0)),
            pl.BlockSpec((D_MODEL, W_IN_COLS), lambda i: (0, 0), pipeline_mode=pl.Buffered(1)),
            pl.BlockSpec((RET_HEADS, C, C), lambda i: (0, 0, 0)),
            pl.BlockSpec((RET_HEADS, C, 1), lambda i: (0, 0, 0)),
            pl.BlockSpec((RET_HEADS, C, 1), lambda i: (0, 0, 0)),
        ],
        out_specs=(pl.BlockSpec((PROJ_TM, DA_WIDTH), lambda i: (i, 0)), t_spec, t_spec,
                   pl.BlockSpec((PROJ_TM, RET_WIDTH), lambda i: (i, 0))),
        scratch_shapes=[
            pltpu.VMEM((RET_HEADS, RET_QK_DIM, RET_V_DIM), F32),
            pltpu.VMEM((D_MODEL, MAIN_COLS), BF16),
            pltpu.VMEM((T_ROWS, D_MODEL), BF16),
            pltpu.VMEM((T_ROWS, D_MODEL), BF16),
        ],
        compiler_params=_params(("arbitrary",)),
        name="inproj_retention",
    )(cd, x2, g, w_in, dec, qdec, kdec)


ACC_ROWS = DA_V_DIM + 16


N_QT = SEQ // ATT_T
N_OFF = N_QT * (N_QT - 1) // 2


def _pipeline3(n_pos, scores, accumulate):
    scores(0, 0)
    scores(1, 1)
    steady = n_pos - 2

    def triple(k, carry):
        t = 3 * k
        accumulate(t, 0)
        scores(t + 2, 2)
        accumulate(t + 1, 1)
        scores(t + 3, 0)
        accumulate(t + 2, 2)
        scores(t + 4, 1)
        return carry

    lax.fori_loop(0, steady // 3, triple, 0)
    t0 = steady // 3 * 3
    rem = steady - t0
    accumulate(t0, 0)
    if rem >= 1:
        scores(t0 + 2, 2)
    accumulate(t0 + 1, 1)
    if rem == 2:
        scores(t0 + 3, 0)
    if rem >= 1:
        accumulate(t0 + 2, 2)
    if rem == 2:
        accumulate(t0 + 3, 0)


def _attn_kernel(slope_ref, jt_ref, it_ref, qt_ref, k_ref, vt_ref, qa_ref, ka1_ref, ka2_ref,
                 dtab_ref, lq1_ref, lk1_ref, lq2_ref, lk2_ref, g_ref, o_ref,
                 m_sc, acc_sc, s0_sc, s1_sc, s2_sc, mx0_sc, mx1_sc, mx2_sc):
    T = ATT_T
    h = pl.program_id(0)
    slope = slope_ref[h]
    qa = qa_ref[0]
    lane = lax.broadcasted_iota(I32, (T, LANES), 1)
    sums_row = (lax.broadcasted_iota(I32, (16, T), 0) == 0).astype(BF16)
    s_bufs = (s0_sc, s1_sc, s2_sc)
    mx_bufs = (mx0_sc, mx1_sc, mx2_sc)

    def scores(j, i, buf, extra):
        kt = k_ref[pl.ds(pl.multiple_of(j * T, T), T), :]
        ks = (jnp.where(lane < DA_HEAD_DIM, kt, ka1_ref[0]),
              jnp.where(lane >= DA_HEAD_DIM, kt, ka2_ref[0]))
        qt = qt_ref[0, i]
        qw = (jnp.concatenate([qt[0:DA_HEAD_DIM], qa], axis=0),
              jnp.concatenate([qa, qt[DA_HEAD_DIM:]], axis=0))
        for mp in range(2):
            s = jnp.dot(ks[mp], qw[mp], preferred_element_type=F32)
            if extra is not None:
                s = s + extra[0]
            s_bufs[buf][mp] = s
            mx_bufs[buf][mp] = jnp.max(s, axis=0, keepdims=True)

    def accumulate(j, i, buf):
        c = slope * lax.convert_element_type((i - j) * T, F32)
        vte = jnp.concatenate([vt_ref[0, j], sums_row], axis=0)
        for mp in range(2):
            m_prev = m_sc[i, mp]
            m_new = jnp.maximum(m_prev, mx_bufs[buf][mp] - c)
            p = jnp.exp(s_bufs[buf][mp] - (m_new + c)).astype(BF16)
            pv = jnp.dot(vte, p, preferred_element_type=F32)
            acc_sc[i, mp] = jnp.exp(m_prev - m_new) * acc_sc[i, mp] + pv
            m_sc[i, mp] = m_new

    m_sc[...] = jnp.full_like(m_sc, -jnp.inf)
    acc_sc[...] = jnp.zeros_like(acc_sc)
    _pipeline3(N_QT,
               lambda pos, buf: scores(pos, pos, buf, dtab_ref),
               lambda pos, buf: accumulate(pos, pos, buf))
    _pipeline3(N_OFF,
               lambda pos, buf: scores(jt_ref[pos], it_ref[pos], buf, None),
               lambda pos, buf: accumulate(jt_ref[pos], it_ref[pos], buf))

    lam = (jnp.exp(jnp.sum(lq1_ref[...] * lk1_ref[...], axis=1, keepdims=True))
           - jnp.exp(jnp.sum(lq2_ref[...] * lk2_ref[...], axis=1, keepdims=True))
           + LAMBDA_INIT)

    def finish(i):
        a1 = acc_sc[i, 0]
        a2 = acc_sc[i, 1]
        ot = (a1[0:DA_V_DIM] / a1[DA_V_DIM:DA_V_DIM + 1]
              - lam * (a2[0:DA_V_DIM] / a2[DA_V_DIM:DA_V_DIM + 1]))
        o = ot.T
        var = jnp.mean(o * o, axis=-1, keepdims=True)
        o = (o * lax.rsqrt(var + EPS) * g_ref[...]) * (1.0 - LAMBDA_INIT)
        o_ref[pl.ds(pl.multiple_of(i * T, T), T), :] = o.astype(BF16)

    def finish_pair(k, carry):
        finish(2 * k)
        finish(2 * k + 1)
        return carry

    lax.fori_loop(0, N_QT // 2, finish_pair, 0)


def _attention(proj, qt4, vt4, slopes, qa, ka1, ka2, dtab, lq1, lk1, lq2, lk2, subln_g):
    T = ATT_T
    vec64 = pl.BlockSpec((1, DA_HEAD_DIM), lambda h: (0, 0))
    per_head = lambda a, b: pl.BlockSpec((1, a, b), lambda h: (h, 0, 0))
    slab = pl.BlockSpec
    single = lambda shape, imap: pl.BlockSpec(shape, imap, pipeline_mode=pl.Buffered(1))
    smem = pl.BlockSpec(memory_space=pltpu.SMEM)
    it_tab, jt_tab = np.tril_indices(N_QT, -1)
    return pl.pallas_call(
        _attn_kernel,
        out_shape=jax.ShapeDtypeStruct((SEQ, DA_WIDTH), BF16),
        grid=(DA_HEADS,),
        in_specs=[
            smem, smem, smem,
            slab((1, N_QT, LANES, T), lambda h: (h, 0, 0, 0)),
            slab((SEQ, LANES), lambda h: (0, DK_OFF // LANES + h)),
            slab((1, N_QT, LANES, T), lambda h: (h, 0, 0, 0)),
            per_head(DA_HEAD_DIM, T), per_head(T, LANES), per_head(T, LANES),
            single((1, T, T), lambda h: (h, 0, 0)),
            vec64, vec64, vec64, vec64,
            pl.BlockSpec((1, DA_V_DIM), lambda h: (0, 0)),
        ],
        out_specs=single((SEQ, LANES), lambda h: (0, h)),
        scratch_shapes=[
            pltpu.VMEM((N_QT, 2, 1, T), F32),
            pltpu.VMEM((N_QT, 2, ACC_ROWS, T), F32),
            pltpu.VMEM((2, T, T), F32),
            pltpu.VMEM((2, T, T), F32),
            pltpu.VMEM((2, T, T), F32),
            pltpu.VMEM((2, 1, T), F32),
            pltpu.VMEM((2, 1, T), F32),
            pltpu.VMEM((2, 1, T), F32),
        ],
        compiler_params=_params(("arbitrary",)),
        name="diff_attention",
    )(slopes, jnp.asarray(jt_tab, I32), jnp.asarray(it_tab, I32), qt4, proj, vt4, qa, ka1, ka2,
      dtab, lq1, lk1, lq2, lk2, subln_g)


def _outproj_router_kernel(x_ref, oda_ref, or_ref, wo32_ref, g_ref, wr_ref, br_ref,
                           h1_ref, xn_ref, ri_ref, rw_ref, cnt_ref, wo_ref):
    @pl.when(pl.program_id(0) == 0)
    def _():
        wo_ref[...] = wo32_ref[...].astype(BF16)

    h1 = (x_ref[...]
          + jnp.dot(oda_ref[...], wo_ref[0:DA_WIDTH, :], preferred_element_type=F32)
          + jnp.dot(or_ref[...], wo_ref[DA_WIDTH:, :], preferred_element_type=F32))
    h1_ref[...] = h1
    var = jnp.mean(h1 * h1, axis=-1, keepdims=True)
    xn = h1 * lax.rsqrt(var + EPS) * g_ref[...]
    for c, words in enumerate(_pack_rows(xn)):
        xn_ref[pl.ds(c, ROUTER_TM, stride=X_ROWS), :] = words
    x_hi = xn.astype(BF16)
    x_lo = (xn - x_hi.astype(F32)).astype(BF16)
    both = jnp.dot(x_hi, wr_ref[...], preferred_element_type=F32)
    logits = (both[:, :LANES] + both[:, LANES:]
              + jnp.dot(x_lo, wr_ref[:, :LANES], preferred_element_type=F32)) + br_ref[...]
    lane = lax.broadcasted_iota(I32, logits.shape, 1)
    neg = jnp.float32(-jnp.inf)
    big = jnp.int32(1 << 20)
    gl = jnp.where(lane < MOE_GROUPS, logits, neg)
    gmax = jnp.max(gl, axis=1, keepdims=True)
    gidx = jnp.min(jnp.where(gl == gmax, lane, big), axis=1, keepdims=True)
    gsum = jnp.sum(jnp.exp(gl - gmax), axis=1, keepdims=True)
    gp = 1.0 / gsum
    lo = MOE_GROUPS + gidx * MOE_EXPERTS_PER_GROUP
    el = jnp.where((lane >= lo) & (lane < lo + MOE_EXPERTS_PER_GROUP), logits, neg)
    v1 = jnp.max(el, axis=1, keepdims=True)
    i1 = jnp.min(jnp.where(el == v1, lane, big), axis=1, keepdims=True)
    el2 = jnp.where(lane == i1, neg, el)
    v2 = jnp.max(el2, axis=1, keepdims=True)
    i2 = jnp.min(jnp.where(el2 == v2, lane, big), axis=1, keepdims=True)
    t = jnp.exp(v2 - v1)
    w1 = gp / (1.0 + t)
    w2 = gp * t / (1.0 + t)
    ri_ref[...] = jnp.where(lane == 0, i1 - MOE_GROUPS,
                            jnp.where(lane == 1, i2 - MOE_GROUPS, 0))
    rw_ref[...] = jnp.where(lane == 0, w1, jnp.where(lane == 1, w2, 0.0))

    @pl.when(pl.program_id(0) == 0)
    def _():
        cnt_ref[...] = jnp.zeros_like(cnt_ref)

    chosen = (lane == i1 - MOE_GROUPS) | (lane == i2 - MOE_GROUPS)
    cnt_ref[...] += jnp.sum(chosen.astype(F32), axis=0, keepdims=True)


def _outproj_router(x2, o_da, o_r, w_out, g, wr, br):
    tm = ROUTER_TM
    row = lambda w: pl.BlockSpec((tm, w), lambda i: (i, 0))
    full = lambda a, b: pl.BlockSpec((a, b), lambda i: (0, 0))
    return pl.pallas_call(
        _outproj_router_kernel,
        out_shape=(
            jax.ShapeDtypeStruct((SEQ, D_MODEL), F32),
            jax.ShapeDtypeStruct((SEQ * X_ROWS, LANES), U32),
            jax.ShapeDtypeStruct((SEQ, LANES), I32),
            jax.ShapeDtypeStruct((SEQ, LANES), F32),
            jax.ShapeDtypeStruct((8, LANES), F32),
        ),
        grid=(SEQ // tm,),
        in_specs=[row(D_MODEL), row(DA_WIDTH), row(RET_WIDTH),
                  pl.BlockSpec((D_MODEL, D_MODEL), lambda i: (0, 0),
                               pipeline_mode=pl.Buffered(1)),
                  full(1, D_MODEL), full(D_MODEL, 2 * LANES), full(1, LANES)],
        out_specs=(row(D_MODEL), pl.BlockSpec((tm * X_ROWS, LANES), lambda i: (i, 0)),
                   row(LANES), row(LANES), full(8, LANES)),
        scratch_shapes=[pltpu.VMEM((D_MODEL, D_MODEL), BF16)],
        compiler_params=_params(("arbitrary",)),
        name="outproj_router",
    )(x2, o_da, o_r, w_out, g, wr, br)


def _plan_kernel(ri_ref, cnt_ref, dest_ref, used_ref):
    TT = PLAN_T
    lane = lax.broadcasted_iota(I32, (TT, LANES), 1)

    def onehots(t):
        r = ri_ref[pl.ds(pl.multiple_of(t * TT, TT), TT), :]
        return lane == r[:, 0:1], lane == r[:, 1:2]

    counts8 = cnt_ref[...].astype(I32)
    shift = FFN_B.bit_length() - 1
    padded = ((counts8 + (FFN_B - 1)) >> shift) << shift
    lane8 = lax.broadcasted_iota(I32, (8, LANES), 1)
    pad_end = padded
    sh = 1
    while sh < LANES:
        pad_end = pad_end + jnp.where(lane8 >= sh, pltpu.roll(pad_end, sh, axis=1), 0)
        sh *= 2
    pad_start = pad_end - padded

    ltri = (lax.broadcasted_iota(I32, (TT, TT), 0)
            > lax.broadcasted_iota(I32, (TT, TT), 1)).astype(BF16)

    def dest_body(t, carry):
        oh1, oh2 = onehots(t)
        a = (oh1 | oh2).astype(F32)
        base = jnp.dot(ltri, a.astype(BF16), preferred_element_type=F32) + carry
        d1 = jnp.sum(jnp.where(oh1, base, 0.0), axis=1, keepdims=True)
        d2 = jnp.sum(jnp.where(oh2, base, 0.0), axis=1, keepdims=True)
        both = jnp.where(lane == 0, d1, jnp.where(lane == 1, d2, 0.0))
        dest_ref[t] = both.T[0:8, :].astype(I32)
        return carry + jnp.sum(a, axis=0, keepdims=True)

    lax.fori_loop(0, SEQ // TT, dest_body, pad_start[0:1].astype(F32))

    total = jnp.max(pad_end, axis=1, keepdims=True)
    row8 = lax.broadcasted_iota(I32, (8, LANES), 0)
    used_ref[...] = jnp.where(row8 == 0, jnp.broadcast_to(total >> shift, (8, LANES)),
                              jnp.where(row8 == 1, counts8, pad_start))


def _plan(ri, cnt):
    return pl.pallas_call(
        _plan_kernel,
        out_shape=(
            jax.ShapeDtypeStruct((SEQ // PLAN_T, 8, PLAN_T), I32),
            jax.ShapeDtypeStruct((8, LANES), I32),
        ),
        compiler_params=pltpu.CompilerParams(vmem_limit_bytes=VMEM_LIMIT),
        name="route_plan",
    )(ri, cnt)


PAD_BITS = FFN_B.bit_length() - 1


def _pad_fill_copies(e, cnt_ref, pst_ref, zero_sc, xs_hbm, zsem):
    cnt = cnt_ref[e]
    pad = (-cnt) & (FFN_B - 1)
    row = pst_ref[e] + cnt
    out = []
    for bit in reversed(range(PAD_BITS)):
        n = 1 << bit
        start = row + ((pad >> (bit + 1)) << (bit + 1))
        copy = pltpu.make_async_copy(
            zero_sc.at[pl.ds(0, n * X_ROWS)],
            xs_hbm.at[pl.ds(pl.multiple_of(start * X_ROWS, X_ROWS), n * X_ROWS)], zsem)
        out.append(((pad & n) != 0, copy))
    return out


def _unused_block_copies(b, zero_sc, xs_hbm, zsem):
    half = FFN_B // 2 * X_ROWS
    return [pltpu.make_async_copy(
        zero_sc, xs_hbm.at[pl.ds(pl.multiple_of((2 * b + k) * half, half), half)], zsem)
        for k in range(2)]


def _dispatch_kernel(dest_ref, cnt_ref, pst_ref, used_ref, xn_ref, xs_hbm, zero_sc, sem, zsem):
    tm = DISP_TM
    i = pl.program_id(0)

    @pl.when(i == 0)
    def _():
        zero_sc[...] = jnp.zeros_like(zero_sc)

        def fill(e, carry):
            for cond, copy in _pad_fill_copies(e, cnt_ref, pst_ref, zero_sc, xs_hbm, zsem):
                pl.when(cond)(copy.start)
            return carry

        lax.fori_loop(0, MOE_EXPERTS, fill, 0)

        def fill_block(b, carry):
            for copy in _unused_block_copies(b, zero_sc, xs_hbm, zsem):
                copy.start()
            return carry

        lax.fori_loop(used_ref[0], N_BLOCKS, fill_block, 0)

    def issue(it, carry):
        for u in range(DMA_UNROLL):
            r = it * DMA_UNROLL + u
            t = i * tm + r
            src = xn_ref.at[pl.ds(pl.multiple_of(r * X_ROWS, X_ROWS), X_ROWS)]
            for kk in range(2):
                d = pl.multiple_of(dest_ref[kk * SEQ + t] * X_ROWS, X_ROWS)
                pltpu.make_async_copy(src, xs_hbm.at[pl.ds(d, X_ROWS)], sem).start(priority=kk)
        return carry

    lax.fori_loop(0, tm // DMA_UNROLL, issue, 0)
    for _ in range(2):
        pltpu.make_async_copy(xn_ref, xs_hbm.at[pl.ds(0, tm * X_ROWS)], sem).wait()

    @pl.when(i == 0)
    def _():
        def drain(e, carry):
            for cond, copy in _pad_fill_copies(e, cnt_ref, pst_ref, zero_sc, xs_hbm, zsem):
                pl.when(cond)(copy.wait)
            return carry

        lax.fori_loop(0, MOE_EXPERTS, drain, 0)

        def drain_block(b, carry):
            for copy in _unused_block_copies(b, zero_sc, xs_hbm, zsem):
                copy.wait()
            return carry

        lax.fori_loop(used_ref[0], N_BLOCKS, drain_block, 0)


def _dispatch(dest_flat, counts, pad_start, used, xn3):
    tm = DISP_TM
    return pl.pallas_call(
        _dispatch_kernel,
        out_shape=jax.ShapeDtypeStruct((N_BUF * X_ROWS, LANES), U32),
        grid_spec=pltpu.PrefetchScalarGridSpec(
            num_scalar_prefetch=4,
            grid=(SEQ // tm,),
            in_specs=[pl.BlockSpec((tm * X_ROWS, LANES), lambda i, d, c, p, u: (i, 0))],
            out_specs=pl.BlockSpec(memory_space=pl.ANY),
            scratch_shapes=[
                pltpu.VMEM((FFN_B // 2 * X_ROWS, LANES), U32),
                pltpu.SemaphoreType.DMA(()),
                pltpu.SemaphoreType.DMA(()),
            ],
        ),
        compiler_params=_params(("arbitrary",)),
        name="moe_dispatch",
    )(dest_flat, counts, pad_start, used, xn3)


BLOCK_COPY_PRIORITY = 1


def _ffn_kernel(cnt_ref, pst_ref, used_ref, xs_hbm, wg_ref, wu_ref, wd_ref, y_hbm,
                xbuf, ybuf, zero_sc, wg_bf, wu_bf, wd_bf, sem_in, sem_out, zsem):
    B = FFN_B
    e = pl.program_id(0)
    n = (cnt_ref[e] + (B - 1)) >> PAD_BITS
    s0 = pst_ref[e] >> PAD_BITS

    def rows(blk):
        size = B * X_ROWS
        return pl.ds(pl.multiple_of(blk * size, size), size)

    def fetch(blk, slot):
        return pltpu.make_async_copy(xs_hbm.at[rows(blk)], xbuf.at[slot], sem_in.at[slot])

    def flush(blk, slot):
        return pltpu.make_async_copy(ybuf.at[slot], y_hbm.at[rows(blk)], sem_out.at[slot])

    used = used_ref[0]

    @pl.when(e == 0)
    def _():
        fetch(0, 0).start(priority=BLOCK_COPY_PRIORITY)

    @pl.when(n > 0)
    def _():
        wg_bf[...] = wg_ref[0].astype(BF16)
        wu_bf[...] = wu_ref[0].astype(BF16)
        wd_bf[...] = wd_ref[0].astype(BF16)

    def body(j, carry):
        blk = s0 + j
        slot = blk % 2

        @pl.when(blk + 1 < used)
        def _():
            fetch(blk + 1, 1 - slot).start(priority=BLOCK_COPY_PRIORITY)

        fetch(blk, slot).wait()

        @pl.when(blk >= 2)
        def _():
            flush(blk - 2, slot).wait()

        x = _unpack_rows([xbuf[slot, pl.ds(c, B, stride=X_ROWS), :]
                          for c in range(X_ROWS)]).astype(BF16)
        hg = jnp.dot(x, wg_bf[...], preferred_element_type=F32)
        hu = jnp.dot(x, wu_bf[...], preferred_element_type=F32)
        hh = ((hg / (1.0 + jnp.exp(-hg))) * hu).astype(BF16)
        y = jnp.dot(hh, wd_bf[...], preferred_element_type=F32)
        for c, words in enumerate(_pack_rows(y)):
            ybuf[slot, pl.ds(c, B, stride=X_ROWS), :] = words
        flush(blk, slot).start(priority=BLOCK_COPY_PRIORITY)
        return carry

    lax.fori_loop(0, n, body, 0)

    @pl.when(e == MOE_EXPERTS - 1)
    def _():
        flush(used - 1, (used + 1) % 2).wait()
        flush(used - 2, used % 2).wait()
        zero_sc[...] = jnp.zeros_like(zero_sc)

        def fill(b, carry):
            pltpu.make_async_copy(zero_sc, y_hbm.at[rows(b)], zsem).start()
            return carry

        def drain(b, carry):
            pltpu.make_async_copy(zero_sc, y_hbm.at[rows(b)], zsem).wait()
            return carry

        lax.fori_loop(used_ref[0], N_BLOCKS, fill, 0)
        lax.fori_loop(used_ref[0], N_BLOCKS, drain, 0)


def _ffn(counts, pad_start, used, xs, w_gate, w_up, w_down):
    B = FFN_B
    wspec = lambda a, c: pl.BlockSpec((1, a, c), lambda e, cnt, pst, used: (e, 0, 0))
    return pl.pallas_call(
        _ffn_kernel,
        out_shape=jax.ShapeDtypeStruct((N_BUF * X_ROWS, LANES), U32),
        grid_spec=pltpu.PrefetchScalarGridSpec(
            num_scalar_prefetch=3,
            grid=(MOE_EXPERTS,),
            in_specs=[
                pl.BlockSpec(memory_space=pl.ANY),
                wspec(D_MODEL, MOE_HIDDEN),
                wspec(D_MODEL, MOE_HIDDEN),
                wspec(MOE_HIDDEN, D_MODEL),
            ],
            out_specs=pl.BlockSpec(memory_space=pl.ANY),
            scratch_shapes=[
                pltpu.VMEM((2, B * X_ROWS, LANES), U32),
                pltpu.VMEM((2, B * X_ROWS, LANES), U32),
                pltpu.VMEM((B * X_ROWS, LANES), U32),
                pltpu.VMEM((D_MODEL, MOE_HIDDEN), BF16),
                pltpu.VMEM((D_MODEL, MOE_HIDDEN), BF16),
                pltpu.VMEM((MOE_HIDDEN, D_MODEL), BF16),
                pltpu.SemaphoreType.DMA((2,)),
                pltpu.SemaphoreType.DMA((2,)),
                pltpu.SemaphoreType.DMA(()),
            ],
        ),
        compiler_params=_params(("arbitrary",)),
        name="expert_ffn",
    )(counts, pad_start, used, xs, w_gate, w_up, w_down)


def _combine_kernel(dest_ref, h1_ref, rw_ref, g_ref, y_hbm, o_ref, ybuf, sem):
    tm = COMB_TM
    i = pl.program_id(0)

    def gather(tile, slot):
        def issue(it, carry):
            for u in range(DMA_UNROLL):
                r = it * DMA_UNROLL + u
                t = tile * tm + r
                for kk in range(2):
                    d = pl.multiple_of(dest_ref[kk * SEQ + t] * X_ROWS, X_ROWS)
                    pltpu.make_async_copy(
                        y_hbm.at[pl.ds(d, X_ROWS)],
                        ybuf.at[slot, kk, pl.ds(pl.multiple_of(r * X_ROWS, X_ROWS), X_ROWS)],
                        sem.at[slot, kk]).start(priority=kk)
            return carry

        lax.fori_loop(0, tm // DMA_UNROLL, issue, 0)

    @pl.when(i == 0)
    def _():
        gather(0, 0)

    @pl.when(i + 1 < pl.num_programs(0))
    def _():
        gather(i + 1, (i + 1) % 2)

    slot = i % 2
    for kk in range(2):
        pltpu.make_async_copy(y_hbm.at[pl.ds(0, tm * X_ROWS)], ybuf.at[slot, kk],
                              sem.at[slot, kk]).wait()
    w = rw_ref[...]
    ys = [_unpack_rows([ybuf[slot, kk, pl.ds(c, tm, stride=X_ROWS), :] for c in range(X_ROWS)])
          for kk in range(2)]
    h = h1_ref[...] + w[:, 0:1] * ys[0] + w[:, 1:2] * ys[1]
    var = jnp.mean(h * h, axis=-1, keepdims=True)
    o_ref[...] = h * lax.rsqrt(var + EPS) * g_ref[...]


def _combine(dest_flat, h1, rw, g, y):
    tm = COMB_TM
    return pl.pallas_call(
        _combine_kernel,
        out_shape=jax.ShapeDtypeStruct((SEQ, D_MODEL), F32),
        grid_spec=pltpu.PrefetchScalarGridSpec(
            num_scalar_prefetch=1,
            grid=(SEQ // tm,),
            in_specs=[
                pl.BlockSpec((tm, D_MODEL), lambda i, d: (i, 0)),
                pl.BlockSpec((tm, LANES), lambda i, d: (i, 0)),
                pl.BlockSpec((1, D_MODEL), lambda i, d: (0, 0)),
                pl.BlockSpec(memory_space=pl.ANY),
            ],
            out_specs=pl.BlockSpec((tm, D_MODEL), lambda i, d: (i, 0)),
            scratch_shapes=[
                pltpu.VMEM((2, 2, tm * X_ROWS, LANES), U32),
                pltpu.SemaphoreType.DMA((2, 2)),
            ],
        ),
        compiler_params=_params(("arbitrary",)),
        name="moe_combine",
    )(dest_flat, h1, rw, g, y)


def _attention_tables():
    T = ATT_T
    f32 = np.float32
    slopes = np.exp2(-ALIBI_MAX * np.arange(1, DA_HEADS + 1, dtype=f32) / DA_HEADS).astype(f32)
    r = np.arange(T)
    hi = ((r // CHUNK) * CHUNK).astype(f32)
    lo = (r % CHUNK).astype(f32)
    sl = slopes[:, None]
    one_h = np.ones((DA_HEADS, T), f32)
    q_rows = np.stack([one_h, one_h, -sl * hi[None], -sl * lo[None]], axis=1)
    k_cols = np.stack([sl * hi[None], sl * lo[None], one_h, one_h], axis=-1)
    qa = np.zeros((DA_HEADS, DA_HEAD_DIM, T), f32)
    qa[:, 0:4, :] = q_rows
    ka1 = np.zeros((DA_HEADS, T, LANES), f32)
    ka1[:, :, DA_HEAD_DIM:DA_HEAD_DIM + 4] = k_cols
    ka2 = np.zeros((DA_HEADS, T, LANES), f32)
    ka2[:, :, 0:4] = k_cols
    rel = (r[:, None] - r[None, :]).astype(f32)
    allowed = (r[:, None] // CHUNK) <= (r[None, :] // CHUNK)
    fix = np.where(rel > 0, -2.0 * slopes[:, None, None] * rel[None], 0.0).astype(f32)
    dtab = np.where(allowed[None], fix, -np.inf).astype(f32)
    return (jnp.asarray(slopes), jnp.asarray(qa, BF16), jnp.asarray(ka1, BF16),
            jnp.asarray(ka2, BF16), jnp.asarray(dtab))


def _retention_tables():
    C = RET_C
    f32 = np.float32
    log_gamma = np.log1p(-np.exp2(-5.0 - np.arange(RET_HEADS, dtype=f32))).astype(f32)
    pos = np.arange(C, dtype=f32)
    rel = pos[:, None] - pos[None, :]
    dec = np.where(rel >= 0, np.exp(log_gamma[:, None, None] * np.maximum(rel, 0.0)), 0.0)
    qdec = np.exp(log_gamma[:, None] * (pos + 1.0)[None, :])[:, :, None]
    kdec = np.exp(log_gamma[:, None] * (C - 1 - pos)[None, :])[:, :, None]
    cd = np.exp(log_gamma * C)
    return tuple(jnp.asarray(t, F32) for t in (cd, dec, qdec, kdec))


def kernel(x, attn_norm_g, w_in, da_lambda_q1, da_lambda_k1, da_lambda_q2, da_lambda_k2,
           da_subln_g, w_out, ffn_norm_g, router_group_w, router_group_b, router_expert_w,
           router_expert_b, expert_w_gate, expert_w_up, expert_w_down, final_norm_g):
    B, S, D = x.shape
    assert (B, S, D) == (1, SEQ, D_MODEL)
    x2 = x.reshape(S, D)

    cd, dec, qdec, kdec = _retention_tables()
    slopes, qa, ka1, ka2, dtab = _attention_tables()
    k_da, qt4, vt4, o_r = _inproj(x2, attn_norm_g[0][None, :], w_in[0], cd, dec, qdec, kdec)

    o_da = _attention(k_da, qt4, vt4, slopes, qa, ka1, ka2, dtab, da_lambda_q1, da_lambda_k1,
                      da_lambda_q2, da_lambda_k2, da_subln_g)

    wr = jnp.zeros((D, LANES), F32)
    wr = wr.at[:, :MOE_GROUPS].set(router_group_w[0])
    wr = wr.at[:, MOE_GROUPS:MOE_GROUPS + MOE_EXPERTS].set(router_expert_w[0])
    br = jnp.zeros((1, LANES), F32)
    br = br.at[0, :MOE_GROUPS].set(router_group_b[0])
    br = br.at[0, MOE_GROUPS:MOE_GROUPS + MOE_EXPERTS].set(router_expert_b[0])
    wr_hi = wr.astype(BF16)
    wr_lo = (wr - wr_hi.astype(F32)).astype(BF16)
    h1, xn, ri, rw, cnt = _outproj_router(x2, o_da, o_r, w_out[0],
                                          ffn_norm_g[0][None, :],
                                          jnp.concatenate([wr_hi, wr_lo], axis=1), br)

    dest, meta = _plan(ri, cnt)
    dest_flat = dest[:, 0:2, :].transpose(1, 0, 2).reshape(N_ASSIGN)
    used1 = meta[0, :1]
    counts = meta[1, :MOE_EXPERTS]
    pad_start = meta[2, :MOE_EXPERTS]
    xs = _dispatch(dest_flat, counts, pad_start, used1, xn)

    y = _ffn(counts, pad_start, used1, xs, expert_w_gate[0], expert_w_up[0], expert_w_down[0])
    out = _combine(dest_flat, h1, rw, final_norm_g[None, :], y)
    return out.reshape(B, S, D)
```

```python
import math

import jax
import jax.numpy as jnp
import numpy as np
from jax import lax
from jax.experimental import pallas as pl
from jax.experimental.pallas import tpu as pltpu

F32 = jnp.float32
BF16 = jnp.bfloat16
I32 = jnp.int32
U32 = jnp.uint32

D_MODEL = 1024
SEQ = 16384
CHUNK = 64
EPS = 1e-6

DA_HEADS = 4
DA_HEAD_DIM = 64
DA_V_DIM = 128
DA_WIDTH = 512
ALIBI_MAX = 8.0
RET_HEADS = 4
RET_QK_DIM = 64
RET_V_DIM = 128
RET_WIDTH = 512
W_IN_COLS = 3072
T_ROWS = 512
MAIN_COLS = 2048
DK_OFF = 0
RQ_OFF = 512
RK_OFF = 768
RV_OFF = 1024
RG_OFF = 1536

MOE_GROUPS = 4
MOE_EXPERTS_PER_GROUP = 8
MOE_EXPERTS = 32
MOE_HIDDEN = 512
LAMBDA_INIT = 0.8 - 0.6 * math.exp(-0.3 * 0)

LANES = 128
X_ROWS = 4
VMEM_LIMIT = 56 * 1024 * 1024

PROJ_TM = 512
ROUTER_TM = 1024
ATT_T = 512
RET_C = 256
PLAN_T = 512
FFN_B = 512
N_ASSIGN = 2 * SEQ
N_BLOCKS = N_ASSIGN // FFN_B + MOE_EXPERTS
N_BUF = N_BLOCKS * FFN_B
COMB_TM = 512
DISP_TM = 1024
DMA_UNROLL = 8


def _params(sem):
    return pltpu.CompilerParams(dimension_semantics=sem, vmem_limit_bytes=VMEM_LIMIT)


def _pack_rows(v):
    bits = lax.bitcast_convert_type(v.astype(BF16).astype(F32), U32)
    return [(bits[:, c * LANES:(c + 1) * LANES] >> 16)
            | (bits[:, (c + X_ROWS) * LANES:(c + X_ROWS + 1) * LANES] & jnp.uint32(0xFFFF0000))
            for c in range(X_ROWS)]


def _unpack_rows(words):
    lo = [lax.bitcast_convert_type(w << 16, F32) for w in words]
    hi = [lax.bitcast_convert_type(w & jnp.uint32(0xFFFF0000), F32) for w in words]
    return jnp.concatenate(lo + hi, axis=1)


def _retention_block(q_all, k_all, v_all, g_all, cd_ref, dec_ref, qdec_ref, kdec_ref, st_sc):
    outs = []
    for h in range(RET_HEADS):
        qk = slice(h * RET_QK_DIM, (h + 1) * RET_QK_DIM)
        vv = slice(h * RET_V_DIM, (h + 1) * RET_V_DIM)
        q = q_all[:, qk]
        k = k_all[:, qk]
        v = v_all[:, vv]
        g = g_all[:, vv]
        s = lax.dot_general(q, k, (((1,), (1,)), ((), ())),
                            preferred_element_type=F32) * dec_ref[h]
        intra = jnp.dot(s.astype(BF16), v, preferred_element_type=F32)
        st = st_sc[h]
        cross = jnp.dot(q, st.astype(BF16), preferred_element_type=F32) * qdec_ref[h]
        kd = (k.astype(F32) * kdec_ref[h]).astype(BF16)
        st_sc[h] = st * cd_ref[h] + lax.dot_general(kd, v, (((0,), (0,)), ((), ())),
                                                    preferred_element_type=F32)
        o = intra + cross
        o = o * lax.rsqrt(jnp.mean(o * o, axis=-1, keepdims=True) + EPS)
        outs.append(((g / (1.0 + jnp.exp(-g))) * o).astype(BF16))
    return outs


def _inproj_kernel(cd_ref, x_ref, g_ref, win_ref, dec_ref, qdec_ref, kdec_ref,
                   k_ref, qt_ref, vt_ref, or_ref, st_sc, w_ref, wq_ref, wv_ref):
    @pl.when(pl.program_id(0) == 0)
    def _():
        st_sc[...] = jnp.zeros_like(st_sc)
        cols = DA_WIDTH
        w_ref[:, DK_OFF:RQ_OFF] = win_ref[:, cols:2 * cols].astype(BF16)
        w_ref[:, RQ_OFF:RK_OFF] = win_ref[:, 3 * cols:3 * cols + 256].astype(BF16)
        w_ref[:, RK_OFF:RV_OFF] = (win_ref[:, 3 * cols + 256:4 * cols]
                                   * (RET_QK_DIM ** -0.5)).astype(BF16)
        w_ref[:, RV_OFF:MAIN_COLS] = win_ref[:, 4 * cols:6 * cols].astype(BF16)
        step = 256
        for r in range(D_MODEL // step):
            rows = slice(r * step, (r + 1) * step)
            wq_ref[:, rows] = (win_ref[rows, 0:cols] * (DA_HEAD_DIM ** -0.5)).T.astype(BF16)
            wv_ref[:, rows] = win_ref[rows, 2 * cols:3 * cols].T.astype(BF16)

    x = x_ref[...]
    var = jnp.mean(x * x, axis=-1, keepdims=True)
    xn = (x * lax.rsqrt(var + EPS) * g_ref[...]).astype(BF16)

    def proj(lo, hi):
        return jnp.dot(xn, w_ref[:, lo:hi], preferred_element_type=F32)

    k_ref[...] = proj(DK_OFF, DK_OFF + DA_WIDTH).astype(BF16)
    nt = (((1,), (1,)), ((), ()))
    qt = lax.dot_general(wq_ref[...], xn, nt, preferred_element_type=F32)
    qt_ref[...] = qt.astype(BF16).reshape(DA_HEADS, 1, 2 * DA_HEAD_DIM, PROJ_TM)
    vt = lax.dot_general(wv_ref[...], xn, nt, preferred_element_type=F32)
    vt_ref[...] = vt.astype(BF16).reshape(DA_HEADS, 1, DA_V_DIM, PROJ_TM)

    rq = proj(RQ_OFF, RK_OFF).astype(BF16)
    rk = proj(RK_OFF, RV_OFF).astype(BF16)
    rv = proj(RV_OFF, RG_OFF).astype(BF16)
    rg = proj(RG_OFF, MAIN_COLS)
    for blk in range(PROJ_TM // RET_C):
        rows = slice(blk * RET_C, (blk + 1) * RET_C)
        outs = _retention_block(rq[rows], rk[rows], rv[rows], rg[rows],
                                cd_ref, dec_ref, qdec_ref, kdec_ref, st_sc)
        for h in range(RET_HEADS):
            or_ref[rows, h * RET_V_DIM:(h + 1) * RET_V_DIM] = outs[h]


def _inproj(x2, g, w_in, cd, dec, qdec, kdec):
    C = RET_C
    t_shape = jax.ShapeDtypeStruct((DA_HEADS, SEQ // PROJ_TM, LANES, PROJ_TM), BF16)
    t_spec = pl.BlockSpec((DA_HEADS, 1, LANES, PROJ_TM), lambda i: (0, i, 0, 0))
    return pl.pallas_call(
        _inproj_kernel,
        out_shape=(jax.ShapeDtypeStruct((SEQ, DA_WIDTH), BF16), t_shape, t_shape,
                   jax.ShapeDtypeStruct((SEQ, RET_WIDTH), BF16)),
        grid=(SEQ // PROJ_TM,),
        in_specs=[
            pl.BlockSpec(memory_space=pltpu.SMEM),
            pl.BlockSpec((PROJ_TM, D_MODEL), lambda i: (i, 0)),
            pl.BlockSpec((1, D_MODEL), lambda i: (0, 0)),
            pl.BlockSpec((D_MODEL, W_IN_COLS), lambda i: (0, 0), pipeline_mode=pl.Buffered(1)),
            pl.BlockSpec((RET_HEADS, C, C), lambda i: (0, 0, 0)),
            pl.BlockSpec((RET_HEADS, C, 1), lambda i: (0, 0, 0)),
            pl.BlockSpec((RET_HEADS, C, 1), lambda i: (0, 0, 0)),
        ],
        out_specs=(pl.BlockSpec((PROJ_TM, DA_WIDTH), lambda i: (i, 0)), t_spec, t_spec,
                   pl.BlockSpec((PROJ_TM, RET_WIDTH), lambda i: (i, 0))),
        scratch_shapes=[
            pltpu.VMEM((RET_HEADS, RET_QK_DIM, RET_V_DIM), F32),
            pltpu.VMEM((D_MODEL, MAIN_COLS), BF16),
            pltpu.VMEM((T_ROWS, D_MODEL), BF16),
            pltpu.VMEM((T_ROWS, D_MODEL), BF16),
        ],
        compiler_params=_params(("arbitrary",)),
        name="inproj_retention",
    )(cd, x2, g, w_in, dec, qdec, kdec)


ACC_ROWS = DA_V_DIM + 16


N_QT = SEQ // ATT_T
N_OFF = N_QT * (N_QT - 1) // 2


def _pipeline3(n_pos, scores, accumulate):
    scores(0, 0)
    scores(1, 1)
    steady = n_pos - 2

    def triple(k, carry):
        t = 3 * k
        accumulate(t, 0)
        scores(t + 2, 2)
        accumulate(t + 1, 1)
        scores(t + 3, 0)
        accumulate(t + 2, 2)
        scores(t + 4, 1)
        return carry

    lax.fori_loop(0, steady // 3, triple, 0)
    t0 = steady // 3 * 3
    rem = steady - t0
    accumulate(t0, 0)
    if rem >= 1:
        scores(t0 + 2, 2)
    accumulate(t0 + 1, 1)
    if rem == 2:
        scores(t0 + 3, 0)
    if rem >= 1:
        accumulate(t0 + 2, 2)
    if rem == 2:
        accumulate(t0 + 3, 0)


def _attn_kernel(slope_ref, jt_ref, it_ref, qt_ref, k_ref, vt_ref, qa_ref, ka1_ref, ka2_ref,
                 dtab_ref, lq1_ref, lk1_ref, lq2_ref, lk2_ref, g_ref, o_ref,
                 m_sc, acc_sc, s0_sc, s1_sc, s2_sc, mx0_sc, mx1_sc, mx2_sc):
    T = ATT_T
    h = pl.program_id(0)
    slope = slope_ref[h]
    qa = qa_ref[0]
    lane = lax.broadcasted_iota(I32, (T, LANES), 1)
    sums_row = (lax.broadcasted_iota(I32, (16, T), 0) == 0).astype(BF16)
    s_bufs = (s0_sc, s1_sc, s2_sc)
    mx_bufs = (mx0_sc, mx1_sc, mx2_sc)

    HALF = T // 2

    def operands(j, i):
        kt = k_ref[pl.ds(pl.multiple_of(j * T, T), T), :]
        ks = (jnp.where(lane < DA_HEAD_DIM, kt, ka1_ref[0]),
              jnp.where(lane >= DA_HEAD_DIM, kt, ka2_ref[0]))
        qt = qt_ref[0, i]
        qw = (jnp.concatenate([qt[0:DA_HEAD_DIM], qa], axis=0),
              jnp.concatenate([qa, qt[DA_HEAD_DIM:]], axis=0))
        return ks, qw

    def scores(j, i, buf):
        ks, qw = operands(j, i)
        for mp in range(2):
            s = jnp.dot(ks[mp], qw[mp], preferred_element_type=F32)
            s_bufs[buf][mp] = s
            mx_bufs[buf][mp] = jnp.max(s, axis=0, keepdims=True)

    def scores_diag(i, buf):
        ks, qw = operands(i, i)
        for mp in range(2):
            left = (jnp.dot(ks[mp][0:HALF], qw[mp][:, 0:HALF], preferred_element_type=F32)
                    + dtab_ref[0, 0:HALF, 0:HALF])
            right = (jnp.dot(ks[mp], qw[mp][:, HALF:], preferred_element_type=F32)
                     + dtab_ref[0, :, HALF:])
            s_bufs[buf][mp, 0:HALF, 0:HALF] = left
            s_bufs[buf][mp, :, HALF:] = right
            mx_bufs[buf][mp] = jnp.concatenate(
                [jnp.max(left, axis=0, keepdims=True), jnp.max(right, axis=0, keepdims=True)],
                axis=1)

    def update(i, mp, m_prev, m_new, pv):
        acc_sc[i, mp] = jnp.exp(m_prev - m_new) * acc_sc[i, mp] + pv
        m_sc[i, mp] = m_new

    def accumulate(j, i, buf):
        c = slope * lax.convert_element_type((i - j) * T, F32)
        vte = jnp.concatenate([vt_ref[0, j], sums_row], axis=0)
        for mp in range(2):
            m_prev = m_sc[i, mp]
            m_new = jnp.maximum(m_prev, mx_bufs[buf][mp] - c)
            p = jnp.exp(s_bufs[buf][mp] - (m_new + c)).astype(BF16)
            update(i, mp, m_prev, m_new, jnp.dot(vte, p, preferred_element_type=F32))

    def accumulate_diag(i, buf):
        vte = jnp.concatenate([vt_ref[0, i], sums_row], axis=0)
        for mp in range(2):
            m_prev = m_sc[i, mp]
            m_new = jnp.maximum(m_prev, mx_bufs[buf][mp])
            p_left = jnp.exp(s_bufs[buf][mp, 0:HALF, 0:HALF] - m_new[:, 0:HALF]).astype(BF16)
            p_right = jnp.exp(s_bufs[buf][mp, :, HALF:] - m_new[:, HALF:]).astype(BF16)
            pv = jnp.concatenate(
                [jnp.dot(vte[:, 0:HALF], p_left, preferred_element_type=F32),
                 jnp.dot(vte, p_right, preferred_element_type=F32)], axis=1)
            update(i, mp, m_prev, m_new, pv)

    m_sc[...] = jnp.full_like(m_sc, -jnp.inf)
    acc_sc[...] = jnp.zeros_like(acc_sc)
    _pipeline3(N_QT, scores_diag, accumulate_diag)
    _pipeline3(N_OFF,
               lambda pos, buf: scores(jt_ref[pos], it_ref[pos], buf),
               lambda pos, buf: accumulate(jt_ref[pos], it_ref[pos], buf))

    lam = (jnp.exp(jnp.sum(lq1_ref[...] * lk1_ref[...], axis=1, keepdims=True))
           - jnp.exp(jnp.sum(lq2_ref[...] * lk2_ref[...], axis=1, keepdims=True))
           + LAMBDA_INIT)

    def finish(i):
        a1 = acc_sc[i, 0]
        a2 = acc_sc[i, 1]
        ot = (a1[0:DA_V_DIM] / a1[DA_V_DIM:DA_V_DIM + 1]
              - lam * (a2[0:DA_V_DIM] / a2[DA_V_DIM:DA_V_DIM + 1]))
        o = ot.T
        var = jnp.mean(o * o, axis=-1, keepdims=True)
        o = (o * lax.rsqrt(var + EPS) * g_ref[...]) * (1.0 - LAMBDA_INIT)
        o_ref[pl.ds(pl.multiple_of(i * T, T), T), :] = o.astype(BF16)

    def finish_pair(k, carry):
        finish(2 * k)
        finish(2 * k + 1)
        return carry

    lax.fori_loop(0, N_QT // 2, finish_pair, 0)


def _attention(proj, qt4, vt4, slopes, qa, ka1, ka2, dtab, lq1, lk1, lq2, lk2, subln_g):
    T = ATT_T
    vec64 = pl.BlockSpec((1, DA_HEAD_DIM), lambda h: (0, 0))
    per_head = lambda a, b: pl.BlockSpec((1, a, b), lambda h: (h, 0, 0))
    slab = pl.BlockSpec
    single = lambda shape, imap: pl.BlockSpec(shape, imap, pipeline_mode=pl.Buffered(1))
    smem = pl.BlockSpec(memory_space=pltpu.SMEM)
    it_tab, jt_tab = np.tril_indices(N_QT, -1)
    return pl.pallas_call(
        _attn_kernel,
        out_shape=jax.ShapeDtypeStruct((SEQ, DA_WIDTH), BF16),
        grid=(DA_HEADS,),
        in_specs=[
            smem, smem, smem,
            slab((1, N_QT, LANES, T), lambda h: (h, 0, 0, 0)),
            slab((SEQ, LANES), lambda h: (0, DK_OFF // LANES + h)),
            slab((1, N_QT, LANES, T), lambda h: (h, 0, 0, 0)),
            per_head(DA_HEAD_DIM, T), per_head(T, LANES), per_head(T, LANES),
            single((1, T, T), lambda h: (h, 0, 0)),
            vec64, vec64, vec64, vec64,
            pl.BlockSpec((1, DA_V_DIM), lambda h: (0, 0)),
        ],
        out_specs=single((SEQ, LANES), lambda h: (0, h)),
        scratch_shapes=[
            pltpu.VMEM((N_QT, 2, 1, T), F32),
            pltpu.VMEM((N_QT, 2, ACC_ROWS, T), F32),
            pltpu.VMEM((2, T, T), F32),
            pltpu.VMEM((2, T, T), F32),
            pltpu.VMEM((2, T, T), F32),
            pltpu.VMEM((2, 1, T), F32),
            pltpu.VMEM((2, 1, T), F32),
            pltpu.VMEM((2, 1, T), F32),
        ],
        compiler_params=_params(("arbitrary",)),
        name="diff_attention",
    )(slopes, jnp.asarray(jt_tab, I32), jnp.asarray(it_tab, I32), qt4, proj, vt4, qa, ka1, ka2,
      dtab, lq1, lk1, lq2, lk2, subln_g)


def _outproj_router_kernel(x_ref, oda_ref, or_ref, wo32_ref, g_ref, wr_ref, br_ref,
                           h1_ref, xn_ref, ri_ref, rw_ref, cnt_ref, wo_ref):
    @pl.when(pl.program_id(0) == 0)
    def _():
        wo_ref[...] = wo32_ref[...].astype(BF16)

    h1 = (x_ref[...]
          + jnp.dot(oda_ref[...], wo_ref[0:DA_WIDTH, :], preferred_element_type=F32)
          + jnp.dot(or_ref[...], wo_ref[DA_WIDTH:, :], preferred_element_type=F32))
    h1_ref[...] = h1
    var = jnp.mean(h1 * h1, axis=-1, keepdims=True)
    xn = h1 * lax.rsqrt(var + EPS) * g_ref[...]
    for c, words in enumerate(_pack_rows(xn)):
        xn_ref[pl.ds(c, ROUTER_TM, stride=X_ROWS), :] = words
    x_hi = xn.astype(BF16)
    x_lo = (xn - x_hi.astype(F32)).astype(BF16)
    both = jnp.dot(x_hi, wr_ref[...], preferred_element_type=F32)
    logits = (both[:, :LANES] + both[:, LANES:]
              + jnp.dot(x_lo, wr_ref[:, :LANES], preferred_element_type=F32)) + br_ref[...]
    lane = lax.broadcasted_iota(I32, logits.shape, 1)
    neg = jnp.float32(-jnp.inf)
    big = jnp.int32(1 << 20)
    gl = jnp.where(lane < MOE_GROUPS, logits, neg)
    gmax = jnp.max(gl, axis=1, keepdims=True)
    gidx = jnp.min(jnp.where(gl == gmax, lane, big), axis=1, keepdims=True)
    gsum = jnp.sum(jnp.exp(gl - gmax), axis=1, keepdims=True)
    gp = 1.0 / gsum
    lo = MOE_GROUPS + gidx * MOE_EXPERTS_PER_GROUP
    el = jnp.where((lane >= lo) & (lane < lo + MOE_EXPERTS_PER_GROUP), logits, neg)
    v1 = jnp.max(el, axis=1, keepdims=True)
    i1 = jnp.min(jnp.where(el == v1, lane, big), axis=1, keepdims=True)
    el2 = jnp.where(lane == i1, neg, el)
    v2 = jnp.max(el2, axis=1, keepdims=True)
    i2 = jnp.min(jnp.where(el2 == v2, lane, big), axis=1, keepdims=True)
    t = jnp.exp(v2 - v1)
    w1 = gp / (1.0 + t)
    w2 = gp * t / (1.0 + t)
    ri_ref[...] = jnp.where(lane == 0, i1 - MOE_GROUPS,
                            jnp.where(lane == 1, i2 - MOE_GROUPS, 0))
    rw_ref[...] = jnp.where(lane == 0, w1, jnp.where(lane == 1, w2, 0.0))

    @pl.when(pl.program_id(0) == 0)
    def _():
        cnt_ref[...] = jnp.zeros_like(cnt_ref)

    chosen = (lane == i1 - MOE_GROUPS) | (lane == i2 - MOE_GROUPS)
    cnt_ref[...] += jnp.sum(chosen.astype(F32), axis=0, keepdims=True)


def _outproj_router(x2, o_da, o_r, w_out, g, wr, br):
    tm = ROUTER_TM
    row = lambda w: pl.BlockSpec((tm, w), lambda i: (i, 0))
    full = lambda a, b: pl.BlockSpec((a, b), lambda i: (0, 0))
    return pl.pallas_call(
        _outproj_router_kernel,
        out_shape=(
            jax.ShapeDtypeStruct((SEQ, D_MODEL), F32),
            jax.ShapeDtypeStruct((SEQ * X_ROWS, LANES), U32),
            jax.ShapeDtypeStruct((SEQ, LANES), I32),
            jax.ShapeDtypeStruct((SEQ, LANES), F32),
            jax.ShapeDtypeStruct((8, LANES), F32),
        ),
        grid=(SEQ // tm,),
        in_specs=[row(D_MODEL), row(DA_WIDTH), row(RET_WIDTH),
                  pl.BlockSpec((D_MODEL, D_MODEL), lambda i: (0, 0),
                               pipeline_mode=pl.Buffered(1)),
                  full(1, D_MODEL), full(D_MODEL, 2 * LANES), full(1, LANES)],
        out_specs=(row(D_MODEL), pl.BlockSpec((tm * X_ROWS, LANES), lambda i: (i, 0)),
                   row(LANES), row(LANES), full(8, LANES)),
        scratch_shapes=[pltpu.VMEM((D_MODEL, D_MODEL), BF16)],
        compiler_params=_params(("arbitrary",)),
        name="outproj_router",
    )(x2, o_da, o_r, w_out, g, wr, br)


def _plan_kernel(ri_ref, cnt_ref, dest_ref, used_ref):
    TT = PLAN_T
    lane = lax.broadcasted_iota(I32, (TT, LANES), 1)

    def onehots(t):
        r = ri_ref[pl.ds(pl.multiple_of(t * TT, TT), TT), :]
        return lane == r[:, 0:1], lane == r[:, 1:2]

    counts8 = cnt_ref[...].astype(I32)
    shift = FFN_B.bit_length() - 1
    padded = ((counts8 + (FFN_B - 1)) >> shift) << shift
    lane8 = lax.broadcasted_iota(I32, (8, LANES), 1)
    pad_end = padded
    sh = 1
    while sh < LANES:
        pad_end = pad_end + jnp.where(lane8 >= sh, pltpu.roll(pad_end, sh, axis=1), 0)
        sh *= 2
    pad_start = pad_end - padded

    ltri = (lax.broadcasted_iota(I32, (TT, TT), 0)
            > lax.broadcasted_iota(I32, (TT, TT), 1)).astype(BF16)

    def dest_body(t, carry):
        oh1, oh2 = onehots(t)
        a = (oh1 | oh2).astype(F32)
        base = jnp.dot(ltri, a.astype(BF16), preferred_element_type=F32) + carry
        d1 = jnp.sum(jnp.where(oh1, base, 0.0), axis=1, keepdims=True)
        d2 = jnp.sum(jnp.where(oh2, base, 0.0), axis=1, keepdims=True)
        both = jnp.where(lane == 0, d1, jnp.where(lane == 1, d2, 0.0))
        dest_ref[t] = both.T[0:8, :].astype(I32)
        return carry + jnp.sum(a, axis=0, keepdims=True)

    lax.fori_loop(0, SEQ // TT, dest_body, pad_start[0:1].astype(F32))

    total = jnp.max(pad_end, axis=1, keepdims=True)
    row8 = lax.broadcasted_iota(I32, (8, LANES), 0)
    used_ref[...] = jnp.where(row8 == 0, jnp.broadcast_to(total >> shift, (8, LANES)),
                              jnp.where(row8 == 1, counts8, pad_start))


def _plan(ri, cnt):
    return pl.pallas_call(
        _plan_kernel,
        out_shape=(
            jax.ShapeDtypeStruct((SEQ // PLAN_T, 8, PLAN_T), I32),
            jax.ShapeDtypeStruct((8, LANES), I32),
        ),
        compiler_params=pltpu.CompilerParams(vmem_limit_bytes=VMEM_LIMIT),
        name="route_plan",
    )(ri, cnt)


PAD_BITS = FFN_B.bit_length() - 1


def _pad_fill_copies(e, cnt_ref, pst_ref, zero_sc, xs_hbm, zsem):
    cnt = cnt_ref[e]
    pad = (-cnt) & (FFN_B - 1)
    row = pst_ref[e] + cnt
    out = []
    for bit in reversed(range(PAD_BITS)):
        n = 1 << bit
        start = row + ((pad >> (bit + 1)) << (bit + 1))
        copy = pltpu.make_async_copy(
            zero_sc.at[pl.ds(0, n * X_ROWS)],
            xs_hbm.at[pl.ds(pl.multiple_of(start * X_ROWS, X_ROWS), n * X_ROWS)], zsem)
        out.append(((pad & n) != 0, copy))
    return out


def _unused_block_copies(b, zero_sc, xs_hbm, zsem):
    half = FFN_B // 2 * X_ROWS
    return [pltpu.make_async_copy(
        zero_sc, xs_hbm.at[pl.ds(pl.multiple_of((2 * b + k) * half, half), half)], zsem)
        for k in range(2)]


def _dispatch_kernel(dest_ref, cnt_ref, pst_ref, used_ref, xn_ref, xs_hbm, zero_sc, sem, zsem):
    tm = DISP_TM
    i = pl.program_id(0)

    @pl.when(i == 0)
    def _():
        zero_sc[...] = jnp.zeros_like(zero_sc)

        def fill(e, carry):
            for cond, copy in _pad_fill_copies(e, cnt_ref, pst_ref, zero_sc, xs_hbm, zsem):
                pl.when(cond)(copy.start)
            return carry

        lax.fori_loop(0, MOE_EXPERTS, fill, 0)

        def fill_block(b, carry):
            for copy in _unused_block_copies(b, zero_sc, xs_hbm, zsem):
                copy.start()
            return carry

        lax.fori_loop(used_ref[0], N_BLOCKS, fill_block, 0)

    def issue(it, carry):
        for u in range(DMA_UNROLL):
            r = it * DMA_UNROLL + u
            t = i * tm + r
            src = xn_ref.at[pl.ds(pl.multiple_of(r * X_ROWS, X_ROWS), X_ROWS)]
            for kk in range(2):
                d = pl.multiple_of(dest_ref[kk * SEQ + t] * X_ROWS, X_ROWS)
                pltpu.make_async_copy(src, xs_hbm.at[pl.ds(d, X_ROWS)], sem).start(priority=kk)
        return carry

    lax.fori_loop(0, tm // DMA_UNROLL, issue, 0)
    for _ in range(2):
        pltpu.make_async_copy(xn_ref, xs_hbm.at[pl.ds(0, tm * X_ROWS)], sem).wait()

    @pl.when(i == 0)
    def _():
        def drain(e, carry):
            for cond, copy in _pad_fill_copies(e, cnt_ref, pst_ref, zero_sc, xs_hbm, zsem):
                pl.when(cond)(copy.wait)
            return carry

        lax.fori_loop(0, MOE_EXPERTS, drain, 0)

        def drain_block(b, carry):
            for copy in _unused_block_copies(b, zero_sc, xs_hbm, zsem):
                copy.wait()
            return carry

        lax.fori_loop(used_ref[0], N_BLOCKS, drain_block, 0)


def _dispatch(dest_flat, counts, pad_start, used, xn3):
    tm = DISP_TM
    return pl.pallas_call(
        _dispatch_kernel,
        out_shape=jax.ShapeDtypeStruct((N_BUF * X_ROWS, LANES), U32),
        grid_spec=pltpu.PrefetchScalarGridSpec(
            num_scalar_prefetch=4,
            grid=(SEQ // tm,),
            in_specs=[pl.BlockSpec((tm * X_ROWS, LANES), lambda i, d, c, p, u: (i, 0))],
            out_specs=pl.BlockSpec(memory_space=pl.ANY),
            scratch_shapes=[
                pltpu.VMEM((FFN_B // 2 * X_ROWS, LANES), U32),
                pltpu.SemaphoreType.DMA(()),
                pltpu.SemaphoreType.DMA(()),
            ],
        ),
        compiler_params=_params(("arbitrary",)),
        name="moe_dispatch",
    )(dest_flat, counts, pad_start, used, xn3)


BLOCK_COPY_PRIORITY = 1


def _ffn_kernel(cnt_ref, pst_ref, used_ref, xs_hbm, wg_ref, wu_ref, wd_ref, y_hbm,
                xbuf, ybuf, zero_sc, wg_bf, wu_bf, wd_bf, sem_in, sem_out, zsem):
    B = FFN_B
    e = pl.program_id(0)
    n = (cnt_ref[e] + (B - 1)) >> PAD_BITS
    s0 = pst_ref[e] >> PAD_BITS

    def rows(blk):
        size = B * X_ROWS
        return pl.ds(pl.multiple_of(blk * size, size), size)

    def fetch(blk, slot):
        return pltpu.make_async_copy(xs_hbm.at[rows(blk)], xbuf.at[slot], sem_in.at[slot])

    def flush(blk, slot):
        return pltpu.make_async_copy(ybuf.at[slot], y_hbm.at[rows(blk)], sem_out.at[slot])

    used = used_ref[0]

    @pl.when(e == 0)
    def _():
        fetch(0, 0).start(priority=BLOCK_COPY_PRIORITY)

    @pl.when(n > 0)
    def _():
        wg_bf[...] = wg_ref[0].astype(BF16)
        wu_bf[...] = wu_ref[0].astype(BF16)
        wd_bf[...] = wd_ref[0].astype(BF16)

    def body(j, carry):
        blk = s0 + j
        slot = blk % 2

        @pl.when(blk + 1 < used)
        def _():
            fetch(blk + 1, 1 - slot).start(priority=BLOCK_COPY_PRIORITY)

        fetch(blk, slot).wait()

        @pl.when(blk >= 2)
        def _():
            flush(blk - 2, slot).wait()

        x = _unpack_rows([xbuf[slot, pl.ds(c, B, stride=X_ROWS), :]
                          for c in range(X_ROWS)]).astype(BF16)
        hg = jnp.dot(x, wg_bf[...], preferred_element_type=F32)
        hu = jnp.dot(x, wu_bf[...], preferred_element_type=F32)
        hh = ((hg / (1.0 + jnp.exp(-hg))) * hu).astype(BF16)
        y = jnp.dot(hh, wd_bf[...], preferred_element_type=F32)
        for c, words in enumerate(_pack_rows(y)):
            ybuf[slot, pl.ds(c, B, stride=X_ROWS), :] = words
        flush(blk, slot).start(priority=BLOCK_COPY_PRIORITY)
        return carry

    lax.fori_loop(0, n, body, 0)

    @pl.when(e == MOE_EXPERTS - 1)
    def _():
        flush(used - 1, (used + 1) % 2).wait()
        flush(used - 2, used % 2).wait()
        zero_sc[...] = jnp.zeros_like(zero_sc)

        def fill(b, carry):
            pltpu.make_async_copy(zero_sc, y_hbm.at[rows(b)], zsem).start()
            return carry

        def drain(b, carry):
            pltpu.make_async_copy(zero_sc, y_hbm.at[rows(b)], zsem).wait()
            return carry

        lax.fori_loop(used_ref[0], N_BLOCKS, fill, 0)
        lax.fori_loop(used_ref[0], N_BLOCKS, drain, 0)


def _ffn(counts, pad_start, used, xs, w_gate, w_up, w_down):
    B = FFN_B
    wspec = lambda a, c: pl.BlockSpec((1, a, c), lambda e, cnt, pst, used: (e, 0, 0))
    return pl.pallas_call(
        _ffn_kernel,
        out_shape=jax.ShapeDtypeStruct((N_BUF * X_ROWS, LANES), U32),
        grid_spec=pltpu.PrefetchScalarGridSpec(
            num_scalar_prefetch=3,
            grid=(MOE_EXPERTS,),
            in_specs=[
                pl.BlockSpec(memory_space=pl.ANY),
                wspec(D_MODEL, MOE_HIDDEN),
                wspec(D_MODEL, MOE_HIDDEN),
                wspec(MOE_HIDDEN, D_MODEL),
            ],
            out_specs=pl.BlockSpec(memory_space=pl.ANY),
            scratch_shapes=[
                pltpu.VMEM((2, B * X_ROWS, LANES), U32),
                pltpu.VMEM((2, B * X_ROWS, LANES), U32),
                pltpu.VMEM((B * X_ROWS, LANES), U32),
                pltpu.VMEM((D_MODEL, MOE_HIDDEN), BF16),
                pltpu.VMEM((D_MODEL, MOE_HIDDEN), BF16),
                pltpu.VMEM((MOE_HIDDEN, D_MODEL), BF16),
                pltpu.SemaphoreType.DMA((2,)),
                pltpu.SemaphoreType.DMA((2,)),
                pltpu.SemaphoreType.DMA(()),
            ],
        ),
        compiler_params=_params(("arbitrary",)),
        name="expert_ffn",
    )(counts, pad_start, used, xs, w_gate, w_up, w_down)


def _combine_kernel(dest_ref, h1_ref, rw_ref, g_ref, y_hbm, o_ref, ybuf, sem):
    tm = COMB_TM
    i = pl.program_id(0)

    def gather(tile, slot):
        def issue(it, carry):
            for u in range(DMA_UNROLL):
                r = it * DMA_UNROLL + u
                t = tile * tm + r
                for kk in range(2):
                    d = pl.multiple_of(dest_ref[kk * SEQ + t] * X_ROWS, X_ROWS)
                    pltpu.make_async_copy(
                        y_hbm.at[pl.ds(d, X_ROWS)],
                        ybuf.at[slot, kk, pl.ds(pl.multiple_of(r * X_ROWS, X_ROWS), X_ROWS)],
                        sem.at[slot, kk]).start(priority=kk)
            return carry

        lax.fori_loop(0, tm // DMA_UNROLL, issue, 0)

    @pl.when(i == 0)
    def _():
        gather(0, 0)

    @pl.when(i + 1 < pl.num_programs(0))
    def _():
        gather(i + 1, (i + 1) % 2)

    slot = i % 2
    for kk in range(2):
        pltpu.make_async_copy(y_hbm.at[pl.ds(0, tm * X_ROWS)], ybuf.at[slot, kk],
                              sem.at[slot, kk]).wait()
    w = rw_ref[...]
    ys = [_unpack_rows([ybuf[slot, kk, pl.ds(c, tm, stride=X_ROWS), :] for c in range(X_ROWS)])
          for kk in range(2)]
    h = h1_ref[...] + w[:, 0:1] * ys[0] + w[:, 1:2] * ys[1]
    var = jnp.mean(h * h, axis=-1, keepdims=True)
    o_ref[...] = h * lax.rsqrt(var + EPS) * g_ref[...]


def _combine(dest_flat, h1, rw, g, y):
    tm = COMB_TM
    return pl.pallas_call(
        _combine_kernel,
        out_shape=jax.ShapeDtypeStruct((SEQ, D_MODEL), F32),
        grid_spec=pltpu.PrefetchScalarGridSpec(
            num_scalar_prefetch=1,
            grid=(SEQ // tm,),
            in_specs=[
                pl.BlockSpec((tm, D_MODEL), lambda i, d: (i, 0)),
                pl.BlockSpec((tm, LANES), lambda i, d: (i, 0)),
                pl.BlockSpec((1, D_MODEL), lambda i, d: (0, 0)),
                pl.BlockSpec(memory_space=pl.ANY),
            ],
            out_specs=pl.BlockSpec((tm, D_MODEL), lambda i, d: (i, 0)),
            scratch_shapes=[
                pltpu.VMEM((2, 2, tm * X_ROWS, LANES), U32),
                pltpu.SemaphoreType.DMA((2, 2)),
            ],
        ),
        compiler_params=_params(("arbitrary",)),
        name="moe_combine",
    )(dest_flat, h1, rw, g, y)


def _attention_tables():
    T = ATT_T
    f32 = np.float32
    slopes = np.exp2(-ALIBI_MAX * np.arange(1, DA_HEADS + 1, dtype=f32) / DA_HEADS).astype(f32)
    r = np.arange(T)
    hi = ((r // CHUNK) * CHUNK).astype(f32)
    lo = (r % CHUNK).astype(f32)
    sl = slopes[:, None]
    one_h = np.ones((DA_HEADS, T), f32)
    q_rows = np.stack([one_h, one_h, -sl * hi[None], -sl * lo[None]], axis=1)
    k_cols = np.stack([sl * hi[None], sl * lo[None], one_h, one_h], axis=-1)
    qa = np.zeros((DA_HEADS, DA_HEAD_DIM, T), f32)
    qa[:, 0:4, :] = q_rows
    ka1 = np.zeros((DA_HEADS, T, LANES), f32)
    ka1[:, :, DA_HEAD_DIM:DA_HEAD_DIM + 4] = k_cols
    ka2 = np.zeros((DA_HEADS, T, LANES), f32)
    ka2[:, :, 0:4] = k_cols
    rel = (r[:, None] - r[None, :]).astype(f32)
    allowed = (r[:, None] // CHUNK) <= (r[None, :] // CHUNK)
    fix = np.where(rel > 0, -2.0 * slopes[:, None, None] * rel[None], 0.0).astype(f32)
    dtab = np.where(allowed[None], fix, -np.inf).astype(f32)
    return (jnp.asarray(slopes), jnp.asarray(qa, BF16), jnp.asarray(ka1, BF16),
            jnp.asarray(ka2, BF16), jnp.asarray(dtab))


def _retention_tables():
    C = RET_C
    f32 = np.float32
    log_gamma = np.log1p(-np.exp2(-5.0 - np.arange(RET_HEADS, dtype=f32))).astype(f32)
    pos = np.arange(C, dtype=f32)
    rel = pos[:, None] - pos[None, :]
    dec = np.where(rel >= 0, np.exp(log_gamma[:, None, None] * np.maximum(rel, 0.0)), 0.0)
    qdec = np.exp(log_gamma[:, None] * (pos + 1.0)[None, :])[:, :, None]
    kdec = np.exp(log_gamma[:, None] * (C - 1 - pos)[None, :])[:, :, None]
    cd = np.exp(log_gamma * C)
    return tuple(jnp.asarray(t, F32) for t in (cd, dec, qdec, kdec))


def kernel(x, attn_norm_g, w_in, da_lambda_q1, da_lambda_k1, da_lambda_q2, da_lambda_k2,
           da_subln_g, w_out, ffn_norm_g, router_group_w, router_group_b, router_expert_w,
           router_expert_b, expert_w_gate, expert_w_up, expert_w_down, final_norm_g):
    B, S, D = x.shape
    assert (B, S, D) == (1, SEQ, D_MODEL)
    x2 = x.reshape(S, D)

    cd, dec, qdec, kdec = _retention_tables()
    slopes, qa, ka1, ka2, dtab = _attention_tables()
    k_da, qt4, vt4, o_r = _inproj(x2, attn_norm_g[0][None, :], w_in[0], cd, dec, qdec, kdec)

    o_da = _attention(k_da, qt4, vt4, slopes, qa, ka1, ka2, dtab, da_lambda_q1, da_lambda_k1,
                      da_lambda_q2, da_lambda_k2, da_subln_g)

    wr = jnp.zeros((D, LANES), F32)
    wr = wr.at[:, :MOE_GROUPS].set(router_group_w[0])
    wr = wr.at[:, MOE_GROUPS:MOE_GROUPS + MOE_EXPERTS].set(router_expert_w[0])
    br = jnp.zeros((1, LANES), F32)
    br = br.at[0, :MOE_GROUPS].set(router_group_b[0])
    br = br.at[0, MOE_GROUPS:MOE_GROUPS + MOE_EXPERTS].set(router_expert_b[0])
    wr_hi = wr.astype(BF16)
    wr_lo = (wr - wr_hi.astype(F32)).astype(BF16)
    h1, xn, ri, rw, cnt = _outproj_router(x2, o_da, o_r, w_out[0],
                                          ffn_norm_g[0][None, :],
                                          jnp.concatenate([wr_hi, wr_lo], axis=1), br)

    dest, meta = _plan(ri, cnt)
    dest_flat = dest[:, 0:2, :].transpose(1, 0, 2).reshape(N_ASSIGN)
    used1 = meta[0, :1]
    counts = meta[1, :MOE_EXPERTS]
    pad_start = meta[2, :MOE_EXPERTS]
    xs = _dispatch(dest_flat, counts, pad_start, used1, xn)

    y = _ffn(counts, pad_start, used1, xs, expert_w_gate[0], expert_w_up[0], expert_w_down[0])
    out = _combine(dest_flat, h1, rw, final_norm_g[None, :], y)
    return out.reshape(B, S, D)
```

```python
import math

import jax
import jax.numpy as jnp
import numpy as np
from jax import lax
from jax.experimental import pallas as pl
from jax.experimental.pallas import tpu as pltpu

F32 = jnp.float32
BF16 = jnp.bfloat16
I32 = jnp.int32
U32 = jnp.uint32

D_MODEL = 1024
SEQ = 16384
CHUNK = 64
EPS = 1e-6

DA_HEADS = 4
DA_HEAD_DIM = 64
DA_V_DIM = 128
DA_WIDTH = 512
ALIBI_MAX = 8.0
RET_HEADS = 4
RET_QK_DIM = 64
RET_V_DIM = 128
RET_WIDTH = 512
W_IN_COLS = 3072
T_ROWS = 512
MAIN_COLS = 2048
DK_OFF = 0
RQ_OFF = 512
RK_OFF = 768
RV_OFF = 1024
RG_OFF = 1536

MOE_GROUPS = 4
MOE_EXPERTS_PER_GROUP = 8
MOE_EXPERTS = 32
MOE_HIDDEN = 512
LAMBDA_INIT = 0.8 - 0.6 * math.exp(-0.3 * 0)

LANES = 128
X_ROWS = 4
VMEM_LIMIT = 56 * 1024 * 1024

PROJ_TM = 512
ROUTER_TM = 1024
ATT_T = 512
RET_C = 256
PLAN_T = 512
FFN_B = 512
N_ASSIGN = 2 * SEQ
N_BLOCKS = N_ASSIGN // FFN_B + MOE_EXPERTS
N_BUF = N_BLOCKS * FFN_B
COMB_TM = 512
DISP_TM = 1024
DMA_UNROLL = 8


def _params(sem):
    return pltpu.CompilerParams(dimension_semantics=sem, vmem_limit_bytes=VMEM_LIMIT)


def _pack_rows(v):
    bits = lax.bitcast_convert_type(v.astype(BF16).astype(F32), U32)
    return [(bits[:, c * LANES:(c + 1) * LANES] >> 16)
            | (bits[:, (c + X_ROWS) * LANES:(c + X_ROWS + 1) * LANES] & jnp.uint32(0xFFFF0000))
            for c in range(X_ROWS)]


def _unpack_rows(words):
    lo = [lax.bitcast_convert_type(w << 16, F32) for w in words]
    hi = [lax.bitcast_convert_type(w & jnp.uint32(0xFFFF0000), F32) for w in words]
    return jnp.concatenate(lo + hi, axis=1)


def _retention_block(q_all, k_all, v_all, g_all, cd_ref, dec_ref, qdec_ref, kdec_ref, st_sc):
    outs = []
    for h in range(RET_HEADS):
        qk = slice(h * RET_QK_DIM, (h + 1) * RET_QK_DIM)
        vv = slice(h * RET_V_DIM, (h + 1) * RET_V_DIM)
        q = q_all[:, qk]
        k = k_all[:, qk]
        v = v_all[:, vv]
        g = g_all[:, vv]
        s = lax.dot_general(q, k, (((1,), (1,)), ((), ())),
                            preferred_element_type=F32) * dec_ref[h]
        intra = jnp.dot(s.astype(BF16), v, preferred_element_type=F32)
        st = st_sc[h]
        cross = jnp.dot(q, st.astype(BF16), preferred_element_type=F32) * qdec_ref[h]
        kd = (k.astype(F32) * kdec_ref[h]).astype(BF16)
        st_sc[h] = st * cd_ref[h] + lax.dot_general(kd, v, (((0,), (0,)), ((), ())),
                                                    preferred_element_type=F32)
        o = intra + cross
        o = o * lax.rsqrt(jnp.mean(o * o, axis=-1, keepdims=True) + EPS)
        outs.append(((g / (1.0 + jnp.exp(-g))) * o).astype(BF16))
    return outs


def _inproj_kernel(cd_ref, x_ref, g_ref, win_ref, dec_ref, qdec_ref, kdec_ref,
                   k_ref, qt_ref, vt_ref, or_ref, st_sc, w_ref, wq_ref, wv_ref):
    @pl.when(pl.program_id(0) == 0)
    def _():
        st_sc[...] = jnp.zeros_like(st_sc)
        cols = DA_WIDTH
        w_ref[:, DK_OFF:RQ_OFF] = win_ref[:, cols:2 * cols].astype(BF16)
        w_ref[:, RQ_OFF:RK_OFF] = win_ref[:, 3 * cols:3 * cols + 256].astype(BF16)
        w_ref[:, RK_OFF:RV_OFF] = (win_ref[:, 3 * cols + 256:4 * cols]
                                   * (RET_QK_DIM ** -0.5)).astype(BF16)
        w_ref[:, RV_OFF:MAIN_COLS] = win_ref[:, 4 * cols:6 * cols].astype(BF16)
        step = 256
        for r in range(D_MODEL // step):
            rows = slice(r * step, (r + 1) * step)
            wq_ref[:, rows] = (win_ref[rows, 0:cols] * (DA_HEAD_DIM ** -0.5)).T.astype(BF16)
            wv_ref[:, rows] = win_ref[rows, 2 * cols:3 * cols].T.astype(BF16)

    x = x_ref[...]
    var = jnp.mean(x * x, axis=-1, keepdims=True)
    xn = (x * lax.rsqrt(var + EPS) * g_ref[...]).astype(BF16)

    def proj(lo, hi):
        return jnp.dot(xn, w_ref[:, lo:hi], preferred_element_type=F32)

    k_ref[...] = proj(DK_OFF, DK_OFF + DA_WIDTH).astype(BF16)
    nt = (((1,), (1,)), ((), ()))
    qt = lax.dot_general(wq_ref[...], xn, nt, preferred_element_type=F32)
    qt_ref[...] = qt.astype(BF16).reshape(DA_HEADS, 1, 2 * DA_HEAD_DIM, PROJ_TM)
    vt = lax.dot_general(wv_ref[...], xn, nt, preferred_element_type=F32)
    vt_ref[...] = vt.astype(BF16).reshape(DA_HEADS, 1, DA_V_DIM, PROJ_TM)

    rq = proj(RQ_OFF, RK_OFF).astype(BF16)
    rk = proj(RK_OFF, RV_OFF).astype(BF16)
    rv = proj(RV_OFF, RG_OFF).astype(BF16)
    rg = proj(RG_OFF, MAIN_COLS)
    for blk in range(PROJ_TM // RET_C):
        rows = slice(blk * RET_C, (blk + 1) * RET_C)
        outs = _retention_block(rq[rows], rk[rows], rv[rows], rg[rows],
                                cd_ref, dec_ref, qdec_ref, kdec_ref, st_sc)
        for h in range(RET_HEADS):
            or_ref[rows, h * RET_V_DIM:(h + 1) * RET_V_DIM] = outs[h]


def _inproj(x2, g, w_in, cd, dec, qdec, kdec):
    C = RET_C
    t_shape = jax.ShapeDtypeStruct((DA_HEADS, SEQ // PROJ_TM, LANES, PROJ_TM), BF16)
    t_spec = pl.BlockSpec((DA_HEADS, 1, LANES, PROJ_TM), lambda i: (0, i, 0, 0))
    return pl.pallas_call(
        _inproj_kernel,
        out_shape=(jax.ShapeDtypeStruct((SEQ, DA_WIDTH), BF16), t_shape, t_shape,
                   jax.ShapeDtypeStruct((SEQ, RET_WIDTH), BF16)),
        grid=(SEQ // PROJ_TM,),
        in_specs=[
            pl.BlockSpec(memory_space=pltpu.SMEM),
            pl.BlockSpec((PROJ_TM, D_MODEL), lambda i: (i, 0)),
            pl.BlockSpec((1, D_MODEL), lambda i: (0, 0)),
            pl.BlockSpec((D_MODEL, W_IN_COLS), lambda i: (0, 0), pipeline_mode=pl.Buffered(1)),
            pl.BlockSpec((RET_HEADS, C, C), lambda i: (0, 0, 0)),
            pl.BlockSpec((RET_HEADS, C, 1), lambda i: (0, 0, 0)),
            pl.BlockSpec((RET_HEADS, C, 1), lambda i: (0, 0, 0)),
        ],
        out_specs=(pl.BlockSpec((PROJ_TM, DA_WIDTH), lambda i: (i, 0)), t_spec, t_spec,
                   pl.BlockSpec((PROJ_TM, RET_WIDTH), lambda i: (i, 0))),
        scratch_shapes=[
            pltpu.VMEM((RET_HEADS, RET_QK_DIM, RET_V_DIM), F32),
            pltpu.VMEM((D_MODEL, MAIN_COLS), BF16),
            pltpu.VMEM((T_ROWS, D_MODEL), BF16),
            pltpu.VMEM((T_ROWS, D_MODEL), BF16),
        ],
        compiler_params=_params(("arbitrary",)),
        name="inproj_retention",
    )(cd, x2, g, w_in, dec, qdec, kdec)


ACC_ROWS = DA_V_DIM + 16


N_QT = SEQ // ATT_T
N_OFF = N_QT * (N_QT - 1) // 2


def _pipeline3(n_pos, scores, accumulate):
    scores(0, 0)
    scores(1, 1)
    steady = n_pos - 2

    def triple(k, carry):
        t = 3 * k
        accumulate(t, 0)
        scores(t + 2, 2)
        accumulate(t + 1, 1)
        scores(t + 3, 0)
        accumulate(t + 2, 2)
        scores(t + 4, 1)
        return carry

    lax.fori_loop(0, steady // 3, triple, 0)
    t0 = steady // 3 * 3
    rem = steady - t0
    accumulate(t0, 0)
    if rem >= 1:
        scores(t0 + 2, 2)
    accumulate(t0 + 1, 1)
    if rem == 2:
        scores(t0 + 3, 0)
    if rem >= 1:
        accumulate(t0 + 2, 2)
    if rem == 2:
        accumulate(t0 + 3, 0)


def _attn_kernel(slope_ref, jt_ref, it_ref, qt_ref, k_ref, vt_ref, qa_ref, ka1_ref, ka2_ref,
                 dtab_ref, lq1_ref, lk1_ref, lq2_ref, lk2_ref, g_ref, o_ref,
                 m_sc, acc_sc, s0_sc, s1_sc, s2_sc, mx0_sc, mx1_sc, mx2_sc):
    T = ATT_T
    h = pl.program_id(0)
    slope = slope_ref[h]
    qa = qa_ref[0]
    lane = lax.broadcasted_iota(I32, (T, LANES), 1)
    sums_row = (lax.broadcasted_iota(I32, (16, T), 0) == 0).astype(BF16)
    s_bufs = (s0_sc, s1_sc, s2_sc)
    mx_bufs = (mx0_sc, mx1_sc, mx2_sc)

    HALF = T // 2

    def operands(j, i):
        kt = k_ref[pl.ds(pl.multiple_of(j * T, T), T), :]
        ks = (jnp.where(lane < DA_HEAD_DIM, kt, ka1_ref[0]),
              jnp.where(lane >= DA_HEAD_DIM, kt, ka2_ref[0]))
        qt = qt_ref[0, i]
        qw = (jnp.concatenate([qt[0:DA_HEAD_DIM], qa], axis=0),
              jnp.concatenate([qa, qt[DA_HEAD_DIM:]], axis=0))
        return ks, qw

    def scores(j, i, buf):
        ks, qw = operands(j, i)
        for mp in range(2):
            s = jnp.dot(ks[mp], qw[mp], preferred_element_type=F32)
            s_bufs[buf][mp] = s
            mx_bufs[buf][mp] = jnp.max(s, axis=0, keepdims=True)

    def scores_diag(i, buf):
        ks, qw = operands(i, i)
        for mp in range(2):
            left = (jnp.dot(ks[mp][0:HALF], qw[mp][:, 0:HALF], preferred_element_type=F32)
                    + dtab_ref[0, 0:HALF, 0:HALF])
            right = (jnp.dot(ks[mp], qw[mp][:, HALF:], preferred_element_type=F32)
                     + dtab_ref[0, :, HALF:])
            s_bufs[buf][mp, 0:HALF, 0:HALF] = left
            s_bufs[buf][mp, :, HALF:] = right
            mx_bufs[buf][mp] = jnp.concatenate(
                [jnp.max(left, axis=0, keepdims=True), jnp.max(right, axis=0, keepdims=True)],
                axis=1)

    def update(i, mp, m_prev, m_new, pv):
        acc_sc[i, mp] = jnp.exp(m_prev - m_new) * acc_sc[i, mp] + pv
        m_sc[i, mp] = m_new

    def accumulate(j, i, buf):
        c = slope * lax.convert_element_type((i - j) * T, F32)
        vte = jnp.concatenate([vt_ref[0, j], sums_row], axis=0)
        for mp in range(2):
            m_prev = m_sc[i, mp]
            m_new = jnp.maximum(m_prev, mx_bufs[buf][mp] - c)
            p = jnp.exp(s_bufs[buf][mp] - (m_new + c)).astype(BF16)
            update(i, mp, m_prev, m_new, jnp.dot(vte, p, preferred_element_type=F32))

    def accumulate_diag(i, buf):
        vte = jnp.concatenate([vt_ref[0, i], sums_row], axis=0)
        for mp in range(2):
            m_prev = m_sc[i, mp]
            m_new = jnp.maximum(m_prev, mx_bufs[buf][mp])
            p_left = jnp.exp(s_bufs[buf][mp, 0:HALF, 0:HALF] - m_new[:, 0:HALF]).astype(BF16)
            p_right = jnp.exp(s_bufs[buf][mp, :, HALF:] - m_new[:, HALF:]).astype(BF16)
            pv = jnp.concatenate(
                [jnp.dot(vte[:, 0:HALF], p_left, preferred_element_type=F32),
                 jnp.dot(vte, p_right, preferred_element_type=F32)], axis=1)
            update(i, mp, m_prev, m_new, pv)

    m_sc[...] = jnp.full_like(m_sc, -jnp.inf)
    acc_sc[...] = jnp.zeros_like(acc_sc)
    _pipeline3(N_QT, scores_diag, accumulate_diag)
    _pipeline3(N_OFF,
               lambda pos, buf: scores(jt_ref[pos], it_ref[pos], buf),
               lambda pos, buf: accumulate(jt_ref[pos], it_ref[pos], buf))

    lam = (jnp.exp(jnp.sum(lq1_ref[...] * lk1_ref[...], axis=1, keepdims=True))
           - jnp.exp(jnp.sum(lq2_ref[...] * lk2_ref[...], axis=1, keepdims=True))
           + LAMBDA_INIT)

    def finish(i):
        a1 = acc_sc[i, 0]
        a2 = acc_sc[i, 1]
        ot = (a1[0:DA_V_DIM] / a1[DA_V_DIM:DA_V_DIM + 1]
              - lam * (a2[0:DA_V_DIM] / a2[DA_V_DIM:DA_V_DIM + 1]))
        o = ot.T
        var = jnp.mean(o * o, axis=-1, keepdims=True)
        o = (o * lax.rsqrt(var + EPS) * g_ref[...]) * (1.0 - LAMBDA_INIT)
        o_ref[pl.ds(pl.multiple_of(i * T, T), T), :] = o.astype(BF16)

    def finish_pair(k, carry):
        finish(2 * k)
        finish(2 * k + 1)
        return carry

    lax.fori_loop(0, N_QT // 2, finish_pair, 0)


def _attention(proj, qt4, vt4, slopes, qa, ka1, ka2, dtab, lq1, lk1, lq2, lk2, subln_g):
    T = ATT_T
    vec64 = pl.BlockSpec((1, DA_HEAD_DIM), lambda h: (0, 0))
    per_head = lambda a, b: pl.BlockSpec((1, a, b), lambda h: (h, 0, 0))
    slab = pl.BlockSpec
    single = lambda shape, imap: pl.BlockSpec(shape, imap, pipeline_mode=pl.Buffered(1))
    smem = pl.BlockSpec(memory_space=pltpu.SMEM)
    it_tab, jt_tab = np.tril_indices(N_QT, -1)
    return pl.pallas_call(
        _attn_kernel,
        out_shape=jax.ShapeDtypeStruct((SEQ, DA_WIDTH), BF16),
        grid=(DA_HEADS,),
        in_specs=[
            smem, smem, smem,
            slab((1, N_QT, LANES, T), lambda h: (h, 0, 0, 0)),
            slab((SEQ, LANES), lambda h: (0, DK_OFF // LANES + h)),
            slab((1, N_QT, LANES, T), lambda h: (h, 0, 0, 0)),
            per_head(DA_HEAD_DIM, T), per_head(T, LANES), per_head(T, LANES),
            single((1, T, T), lambda h: (h, 0, 0)),
            vec64, vec64, vec64, vec64,
            pl.BlockSpec((1, DA_V_DIM), lambda h: (0, 0)),
        ],
        out_specs=single((SEQ, LANES), lambda h: (0, h)),
        scratch_shapes=[
            pltpu.VMEM((N_QT, 2, 1, T), F32),
            pltpu.VMEM((N_QT, 2, ACC_ROWS, T), F32),
            pltpu.VMEM((2, T, T), F32),
            pltpu.VMEM((2, T, T), F32),
            pltpu.VMEM((2, T, T), F32),
            pltpu.VMEM((2, 1, T), F32),
            pltpu.VMEM((2, 1, T), F32),
            pltpu.VMEM((2, 1, T), F32),
        ],
        compiler_params=_params(("arbitrary",)),
        name="diff_attention",
    )(slopes, jnp.asarray(jt_tab, I32), jnp.asarray(it_tab, I32), qt4, proj, vt4, qa, ka1, ka2,
      dtab, lq1, lk1, lq2, lk2, subln_g)


def _outproj_router_kernel(x_ref, oda_ref, or_ref, wo32_ref, g_ref, wr_ref, br_ref,
                           h1_ref, xn_ref, ri_ref, rw_ref, cnt_ref, wo_ref):
    @pl.when(pl.program_id(0) == 0)
    def _():
        wo_ref[...] = wo32_ref[...].astype(BF16)

    h1 = (x_ref[...]
          + jnp.dot(oda_ref[...], wo_ref[0:DA_WIDTH, :], preferred_element_type=F32)
          + jnp.dot(or_ref[...], wo_ref[DA_WIDTH:, :], preferred_element_type=F32))
    h1_ref[...] = h1
    var = jnp.mean(h1 * h1, axis=-1, keepdims=True)
    xn = h1 * lax.rsqrt(var + EPS) * g_ref[...]
    for c, words in enumerate(_pack_rows(xn)):
        xn_ref[pl.ds(c, ROUTER_TM, stride=X_ROWS), :] = words
    x_hi = xn.astype(BF16)
    x_lo = (xn - x_hi.astype(F32)).astype(BF16)
    both = jnp.dot(x_hi, wr_ref[...], preferred_element_type=F32)
    logits = (both[:, :LANES] + both[:, LANES:]
              + jnp.dot(x_lo, wr_ref[:, :LANES], preferred_element_type=F32)) + br_ref[...]
    lane = lax.broadcasted_iota(I32, logits.shape, 1)
    neg = jnp.float32(-jnp.inf)
    big = jnp.int32(1 << 20)
    gl = jnp.where(lane < MOE_GROUPS, logits, neg)
    gmax = jnp.max(gl, axis=1, keepdims=True)
    gidx = jnp.min(jnp.where(gl == gmax, lane, big), axis=1, keepdims=True)
    gsum = jnp.sum(jnp.exp(gl - gmax), axis=1, keepdims=True)
    gp = 1.0 / gsum
    lo = MOE_GROUPS + gidx * MOE_EXPERTS_PER_GROUP
    el = jnp.where((lane >= lo) & (lane < lo + MOE_EXPERTS_PER_GROUP), logits, neg)
    v1 = jnp.max(el, axis=1, keepdims=True)
    i1 = jnp.min(jnp.where(el == v1, lane, big), axis=1, keepdims=True)
    el2 = jnp.where(lane == i1, neg, el)
    v2 = jnp.max(el2, axis=1, keepdims=True)
    i2 = jnp.min(jnp.where(el2 == v2, lane, big), axis=1, keepdims=True)
    t = jnp.exp(v2 - v1)
    w1 = gp / (1.0 + t)
    w2 = gp * t / (1.0 + t)
    ri_ref[...] = jnp.where(lane == 0, i1 - MOE_GROUPS,
                            jnp.where(lane == 1, i2 - MOE_GROUPS, 0))
    rw_ref[...] = jnp.where(lane == 0, w1, jnp.where(lane == 1, w2, 0.0))

    @pl.when(pl.program_id(0) == 0)
    def _():
        cnt_ref[...] = jnp.zeros_like(cnt_ref)

    chosen = (lane == i1 - MOE_GROUPS) | (lane == i2 - MOE_GROUPS)
    cnt_ref[...] += jnp.sum(chosen.astype(F32), axis=0, keepdims=True)


def _outproj_router(x2, o_da, o_r, w_out, g, wr, br):
    tm = ROUTER_TM
    row = lambda w: pl.BlockSpec((tm, w), lambda i: (i, 0))
    full = lambda a, b: pl.BlockSpec((a, b), lambda i: (0, 0))
    return pl.pallas_call(
        _outproj_router_kernel,
        out_shape=(
            jax.ShapeDtypeStruct((SEQ, D_MODEL), F32),
            jax.ShapeDtypeStruct((SEQ * X_ROWS, LANES), U32),
            jax.ShapeDtypeStruct((SEQ, LANES), I32),
            jax.ShapeDtypeStruct((SEQ, LANES), F32),
            jax.ShapeDtypeStruct((8, LANES), F32),
        ),
        grid=(SEQ // tm,),
        in_specs=[row(D_MODEL), row(DA_WIDTH), row(RET_WIDTH),
                  pl.BlockSpec((D_MODEL, D_MODEL), lambda i: (0, 0),
                               pipeline_mode=pl.Buffered(1)),
                  full(1, D_MODEL), full(D_MODEL, 2 * LANES), full(1, LANES)],
        out_specs=(row(D_MODEL), pl.BlockSpec((tm * X_ROWS, LANES), lambda i: (i, 0)),
                   row(LANES), row(LANES), full(8, LANES)),
        scratch_shapes=[pltpu.VMEM((D_MODEL, D_MODEL), BF16)],
        compiler_params=_params(("arbitrary",)),
        name="outproj_router",
    )(x2, o_da, o_r, w_out, g, wr, br)


def _plan_kernel(ri_ref, cnt_ref, dest_ref, used_ref):
    TT = PLAN_T
    lane = lax.broadcasted_iota(I32, (TT, LANES), 1)

    def onehots(t):
        r = ri_ref[pl.ds(pl.multiple_of(t * TT, TT), TT), :]
        return lane == r[:, 0:1], lane == r[:, 1:2]

    counts8 = cnt_ref[...].astype(I32)
    shift = FFN_B.bit_length() - 1
    padded = ((counts8 + (FFN_B - 1)) >> shift) << shift
    lane8 = lax.broadcasted_iota(I32, (8, LANES), 1)
    pad_end = padded
    sh = 1
    while sh < LANES:
        pad_end = pad_end + jnp.where(lane8 >= sh, pltpu.roll(pad_end, sh, axis=1), 0)
        sh *= 2
    pad_start = pad_end - padded

    ltri = (lax.broadcasted_iota(I32, (TT, TT), 0)
            > lax.broadcasted_iota(I32, (TT, TT), 1)).astype(BF16)

    def dest_body(t, carry):
        oh1, oh2 = onehots(t)
        a = (oh1 | oh2).astype(F32)
        base = jnp.dot(ltri, a.astype(BF16), preferred_element_type=F32) + carry
        d1 = jnp.sum(jnp.where(oh1, base, 0.0), axis=1, keepdims=True)
        d2 = jnp.sum(jnp.where(oh2, base, 0.0), axis=1, keepdims=True)
        both = jnp.where(lane == 0, d1, jnp.where(lane == 1, d2, 0.0))
        dest_ref[t] = both.T[0:8, :].astype(I32)
        return carry + jnp.sum(a, axis=0, keepdims=True)

    lax.fori_loop(0, SEQ // TT, dest_body, pad_start[0:1].astype(F32))

    total = jnp.max(pad_end, axis=1, keepdims=True)
    row8 = lax.broadcasted_iota(I32, (8, LANES), 0)
    used_ref[...] = jnp.where(row8 == 0, jnp.broadcast_to(total >> shift, (8, LANES)),
                              jnp.where(row8 == 1, counts8, pad_start))


def _plan(ri, cnt):
    return pl.pallas_call(
        _plan_kernel,
        out_shape=(
            jax.ShapeDtypeStruct((SEQ // PLAN_T, 8, PLAN_T), I32),
            jax.ShapeDtypeStruct((8, LANES), I32),
        ),
        compiler_params=pltpu.CompilerParams(vmem_limit_bytes=VMEM_LIMIT),
        name="route_plan",
    )(ri, cnt)


PAD_BITS = FFN_B.bit_length() - 1


def _pad_fill_copies(e, cnt_ref, pst_ref, zero_sc, xs_hbm, zsem):
    cnt = cnt_ref[e]
    pad = (-cnt) & (FFN_B - 1)
    row = pst_ref[e] + cnt
    out = []
    for bit in reversed(range(PAD_BITS)):
        n = 1 << bit
        start = row + ((pad >> (bit + 1)) << (bit + 1))
        copy = pltpu.make_async_copy(
            zero_sc.at[pl.ds(0, n * X_ROWS)],
            xs_hbm.at[pl.ds(pl.multiple_of(start * X_ROWS, X_ROWS), n * X_ROWS)], zsem)
        out.append(((pad & n) != 0, copy))
    return out


def _unused_block_copies(b, zero_sc, xs_hbm, zsem):
    half = FFN_B // 2 * X_ROWS
    return [pltpu.make_async_copy(
        zero_sc, xs_hbm.at[pl.ds(pl.multiple_of((2 * b + k) * half, half), half)], zsem)
        for k in range(2)]


def _dispatch_kernel(dest_ref, cnt_ref, pst_ref, used_ref, xn_ref, xs_hbm, zero_sc, sem, zsem):
    tm = DISP_TM
    i = pl.program_id(0)

    @pl.when(i == 0)
    def _():
        zero_sc[...] = jnp.zeros_like(zero_sc)

        def fill(e, carry):
            for cond, copy in _pad_fill_copies(e, cnt_ref, pst_ref, zero_sc, xs_hbm, zsem):
                pl.when(cond)(copy.start)
            return carry

        lax.fori_loop(0, MOE_EXPERTS, fill, 0)

        def fill_block(b, carry):
            for copy in _unused_block_copies(b, zero_sc, xs_hbm, zsem):
                copy.start()
            return carry

        lax.fori_loop(used_ref[0], N_BLOCKS, fill_block, 0)

    def issue(it, carry):
        for u in range(DMA_UNROLL):
            r = it * DMA_UNROLL + u
            t = i * tm + r
            src = xn_ref.at[pl.ds(pl.multiple_of(r * X_ROWS, X_ROWS), X_ROWS)]
            for kk in range(2):
                d = pl.multiple_of(dest_ref[kk * SEQ + t] * X_ROWS, X_ROWS)
                pltpu.make_async_copy(src, xs_hbm.at[pl.ds(d, X_ROWS)], sem).start(priority=kk)
        return carry

    lax.fori_loop(0, tm // DMA_UNROLL, issue, 0)
    for _ in range(2):
        pltpu.make_async_copy(xn_ref, xs_hbm.at[pl.ds(0, tm * X_ROWS)], sem).wait()

    @pl.when(i == 0)
    def _():
        def drain(e, carry):
            for cond, copy in _pad_fill_copies(e, cnt_ref, pst_ref, zero_sc, xs_hbm, zsem):
                pl.when(cond)(copy.wait)
            return carry

        lax.fori_loop(0, MOE_EXPERTS, drain, 0)

        def drain_block(b, carry):
            for copy in _unused_block_copies(b, zero_sc, xs_hbm, zsem):
                copy.wait()
            return carry

        lax.fori_loop(used_ref[0], N_BLOCKS, drain_block, 0)


def _dispatch(dest_flat, counts, pad_start, used, xn3):
    tm = DISP_TM
    return pl.pallas_call(
        _dispatch_kernel,
        out_shape=jax.ShapeDtypeStruct((N_BUF * X_ROWS, LANES), U32),
        grid_spec=pltpu.PrefetchScalarGridSpec(
            num_scalar_prefetch=4,
            grid=(SEQ // tm,),
            in_specs=[pl.BlockSpec((tm * X_ROWS, LANES), lambda i, d, c, p, u: (i, 0))],
            out_specs=pl.BlockSpec(memory_space=pl.ANY),
            scratch_shapes=[
                pltpu.VMEM((FFN_B // 2 * X_ROWS, LANES), U32),
                pltpu.SemaphoreType.DMA(()),
                pltpu.SemaphoreType.DMA(()),
            ],
        ),
        compiler_params=_params(("arbitrary",)),
        name="moe_dispatch",
    )(dest_flat, counts, pad_start, used, xn3)


BLOCK_COPY_PRIORITY = 1


def _ffn_kernel(cnt_ref, pst_ref, used_ref, xs_hbm, wg_ref, wu_ref, wd_ref, y_hbm,
                xbuf, ybuf, zero_sc, wg_bf, wu_bf, wd_bf, sem_in, sem_out, zsem):
    B = FFN_B
    e = pl.program_id(0)
    n = (cnt_ref[e] + (B - 1)) >> PAD_BITS
    s0 = pst_ref[e] >> PAD_BITS

    def rows(blk):
        size = B * X_ROWS
        return pl.ds(pl.multiple_of(blk * size, size), size)

    def fetch(blk, slot):
        return pltpu.make_async_copy(xs_hbm.at[rows(blk)], xbuf.at[slot], sem_in.at[slot])

    def flush(blk, slot):
        return pltpu.make_async_copy(ybuf.at[slot], y_hbm.at[rows(blk)], sem_out.at[slot])

    used = used_ref[0]

    @pl.when(e == 0)
    def _():
        fetch(0, 0).start(priority=BLOCK_COPY_PRIORITY)

    @pl.when(n > 0)
    def _():
        wg_bf[...] = wg_ref[0].astype(BF16)
        wu_bf[...] = wu_ref[0].astype(BF16)
        wd_bf[...] = wd_ref[0].astype(BF16)

    def body(j, carry):
        blk = s0 + j
        slot = blk % 2

        @pl.when(blk + 1 < used)
        def _():
            fetch(blk + 1, 1 - slot).start(priority=BLOCK_COPY_PRIORITY)

        fetch(blk, slot).wait()

        @pl.when(blk >= 2)
        def _():
            flush(blk - 2, slot).wait()

        def swiglu(rows):
            x = _unpack_rows([xbuf[slot, pl.ds(c, rows, stride=X_ROWS), :]
                              for c in range(X_ROWS)]).astype(BF16)
            hg = jnp.dot(x, wg_bf[...], preferred_element_type=F32)
            hu = jnp.dot(x, wu_bf[...], preferred_element_type=F32)
            hh = ((hg / (1.0 + jnp.exp(-hg))) * hu).astype(BF16)
            y = jnp.dot(hh, wd_bf[...], preferred_element_type=F32)
            for c, words in enumerate(_pack_rows(y)):
                ybuf[slot, pl.ds(c, rows, stride=X_ROWS), :] = words

        valid = cnt_ref[e] - j * B

        @pl.when(valid > B // 2)
        def _():
            swiglu(B)

        @pl.when(valid <= B // 2)
        def _():
            swiglu(B // 2)
            tail = pl.ds(B // 2 * X_ROWS, B // 2 * X_ROWS)
            ybuf[slot, tail, :] = jnp.zeros((B // 2 * X_ROWS, LANES), U32)

        flush(blk, slot).start(priority=BLOCK_COPY_PRIORITY)
        return carry

    lax.fori_loop(0, n, body, 0)

    @pl.when(e == MOE_EXPERTS - 1)
    def _():
        flush(used - 1, (used + 1) % 2).wait()
        flush(used - 2, used % 2).wait()
        zero_sc[...] = jnp.zeros_like(zero_sc)

        def fill(b, carry):
            pltpu.make_async_copy(zero_sc, y_hbm.at[rows(b)], zsem).start()
            return carry

        def drain(b, carry):
            pltpu.make_async_copy(zero_sc, y_hbm.at[rows(b)], zsem).wait()
            return carry

        lax.fori_loop(used_ref[0], N_BLOCKS, fill, 0)
        lax.fori_loop(used_ref[0], N_BLOCKS, drain, 0)


def _ffn(counts, pad_start, used, xs, w_gate, w_up, w_down):
    B = FFN_B
    wspec = lambda a, c: pl.BlockSpec((1, a, c), lambda e, cnt, pst, used: (e, 0, 0))
    return pl.pallas_call(
        _ffn_kernel,
        out_shape=jax.ShapeDtypeStruct((N_BUF * X_ROWS, LANES), U32),
        grid_spec=pltpu.PrefetchScalarGridSpec(
            num_scalar_prefetch=3,
            grid=(MOE_EXPERTS,),
            in_specs=[
                pl.BlockSpec(memory_space=pl.ANY),
                wspec(D_MODEL, MOE_HIDDEN),
                wspec(D_MODEL, MOE_HIDDEN),
                wspec(MOE_HIDDEN, D_MODEL),
            ],
            out_specs=pl.BlockSpec(memory_space=pl.ANY),
            scratch_shapes=[
                pltpu.VMEM((2, B * X_ROWS, LANES), U32),
                pltpu.VMEM((2, B * X_ROWS, LANES), U32),
                pltpu.VMEM((B * X_ROWS, LANES), U32),
                pltpu.VMEM((D_MODEL, MOE_HIDDEN), BF16),
                pltpu.VMEM((D_MODEL, MOE_HIDDEN), BF16),
                pltpu.VMEM((MOE_HIDDEN, D_MODEL), BF16),
                pltpu.SemaphoreType.DMA((2,)),
                pltpu.SemaphoreType.DMA((2,)),
                pltpu.SemaphoreType.DMA(()),
            ],
        ),
        compiler_params=_params(("arbitrary",)),
        name="expert_ffn",
    )(counts, pad_start, used, xs, w_gate, w_up, w_down)


def _combine_kernel(dest_ref, h1_ref, rw_ref, g_ref, y_hbm, o_ref, ybuf, sem):
    tm = COMB_TM
    i = pl.program_id(0)

    def gather(tile, slot):
        def issue(it, carry):
            for u in range(DMA_UNROLL):
                r = it * DMA_UNROLL + u
                t = tile * tm + r
                for kk in range(2):
                    d = pl.multiple_of(dest_ref[kk * SEQ + t] * X_ROWS, X_ROWS)
                    pltpu.make_async_copy(
                        y_hbm.at[pl.ds(d, X_ROWS)],
                        ybuf.at[slot, kk, pl.ds(pl.multiple_of(r * X_ROWS, X_ROWS), X_ROWS)],
                        sem.at[slot, kk]).start(priority=kk)
            return carry

        lax.fori_loop(0, tm // DMA_UNROLL, issue, 0)

    @pl.when(i == 0)
    def _():
        gather(0, 0)

    @pl.when(i + 1 < pl.num_programs(0))
    def _():
        gather(i + 1, (i + 1) % 2)

    slot = i % 2
    for kk in range(2):
        pltpu.make_async_copy(y_hbm.at[pl.ds(0, tm * X_ROWS)], ybuf.at[slot, kk],
                              sem.at[slot, kk]).wait()
    w = rw_ref[...]
    ys = [_unpack_rows([ybuf[slot, kk, pl.ds(c, tm, stride=X_ROWS), :] for c in range(X_ROWS)])
          for kk in range(2)]
    h = h1_ref[...] + w[:, 0:1] * ys[0] + w[:, 1:2] * ys[1]
    var = jnp.mean(h * h, axis=-1, keepdims=True)
    o_ref[...] = h * lax.rsqrt(var + EPS) * g_ref[...]


def _combine(dest_flat, h1, rw, g, y):
    tm = COMB_TM
    return pl.pallas_call(
        _combine_kernel,
        out_shape=jax.ShapeDtypeStruct((SEQ, D_MODEL), F32),
        grid_spec=pltpu.PrefetchScalarGridSpec(
            num_scalar_prefetch=1,
            grid=(SEQ // tm,),
            in_specs=[
                pl.BlockSpec((tm, D_MODEL), lambda i, d: (i, 0)),
                pl.BlockSpec((tm, LANES), lambda i, d: (i, 0)),
                pl.BlockSpec((1, D_MODEL), lambda i, d: (0, 0)),
                pl.BlockSpec(memory_space=pl.ANY),
            ],
            out_specs=pl.BlockSpec((tm, D_MODEL), lambda i, d: (i, 0)),
            scratch_shapes=[
                pltpu.VMEM((2, 2, tm * X_ROWS, LANES), U32),
                pltpu.SemaphoreType.DMA((2, 2)),
            ],
        ),
        compiler_params=_params(("arbitrary",)),
        name="moe_combine",
    )(dest_flat, h1, rw, g, y)


def _attention_tables():
    T = ATT_T
    f32 = np.float32
    slopes = np.exp2(-ALIBI_MAX * np.arange(1, DA_HEADS + 1, dtype=f32) / DA_HEADS).astype(f32)
    r = np.arange(T)
    hi = ((r // CHUNK) * CHUNK).astype(f32)
    lo = (r % CHUNK).astype(f32)
    sl = slopes[:, None]
    one_h = np.ones((DA_HEADS, T), f32)
    q_rows = np.stack([one_h, one_h, -sl * hi[None], -sl * lo[None]], axis=1)
    k_cols = np.stack([sl * hi[None], sl * lo[None], one_h, one_h], axis=-1)
    qa = np.zeros((DA_HEADS, DA_HEAD_DIM, T), f32)
    qa[:, 0:4, :] = q_rows
    ka1 = np.zeros((DA_HEADS, T, LANES), f32)
    ka1[:, :, DA_HEAD_DIM:DA_HEAD_DIM + 4] = k_cols
    ka2 = np.zeros((DA_HEADS, T, LANES), f32)
    ka2[:, :, 0:4] = k_cols
    rel = (r[:, None] - r[None, :]).astype(f32)
    allowed = (r[:, None] // CHUNK) <= (r[None, :] // CHUNK)
    fix = np.where(rel > 0, -2.0 * slopes[:, None, None] * rel[None], 0.0).astype(f32)
    dtab = np.where(allowed[None], fix, -np.inf).astype(f32)
    return (jnp.asarray(slopes), jnp.asarray(qa, BF16), jnp.asarray(ka1, BF16),
            jnp.asarray(ka2, BF16), jnp.asarray(dtab))


def _retention_tables():
    C = RET_C
    f32 = np.float32
    log_gamma = np.log1p(-np.exp2(-5.0 - np.arange(RET_HEADS, dtype=f32))).astype(f32)
    pos = np.arange(C, dtype=f32)
    rel = pos[:, None] - pos[None, :]
    dec = np.where(rel >= 0, np.exp(log_gamma[:, None, None] * np.maximum(rel, 0.0)), 0.0)
    qdec = np.exp(log_gamma[:, None] * (pos + 1.0)[None, :])[:, :, None]
    kdec = np.exp(log_gamma[:, None] * (C - 1 - pos)[None, :])[:, :, None]
    cd = np.exp(log_gamma * C)
    return tuple(jnp.asarray(t, F32) for t in (cd, dec, qdec, kdec))


def kernel(x, attn_norm_g, w_in, da_lambda_q1, da_lambda_k1, da_lambda_q2, da_lambda_k2,
           da_subln_g, w_out, ffn_norm_g, router_group_w, router_group_b, router_expert_w,
           router_expert_b, expert_w_gate, expert_w_up, expert_w_down, final_norm_g):
    B, S, D = x.shape
    assert (B, S, D) == (1, SEQ, D_MODEL)
    x2 = x.reshape(S, D)

    cd, dec, qdec, kdec = _retention_tables()
    slopes, qa, ka1, ka2, dtab = _attention_tables()
    k_da, qt4, vt4, o_r = _inproj(x2, attn_norm_g[0][None, :], w_in[0], cd, dec, qdec, kdec)

    o_da = _attention(k_da, qt4, vt4, slopes, qa, ka1, ka2, dtab, da_lambda_q1, da_lambda_k1,
                      da_lambda_q2, da_lambda_k2, da_subln_g)

    wr = jnp.zeros((D, LANES), F32)
    wr = wr.at[:, :MOE_GROUPS].set(router_group_w[0])
    wr = wr.at[:, MOE_GROUPS:MOE_GROUPS + MOE_EXPERTS].set(router_expert_w[0])
    br = jnp.zeros((1, LANES), F32)
    br = br.at[0, :MOE_GROUPS].set(router_group_b[0])
    br = br.at[0, MOE_GROUPS:MOE_GROUPS + MOE_EXPERTS].set(router_expert_b[0])
    wr_hi = wr.astype(BF16)
    wr_lo = (wr - wr_hi.astype(F32)).astype(BF16)
    h1, xn, ri, rw, cnt = _outproj_router(x2, o_da, o_r, w_out[0],
                                          ffn_norm_g[0][None, :],
                                          jnp.concatenate([wr_hi, wr_lo], axis=1), br)

    dest, meta = _plan(ri, cnt)
    dest_flat = dest[:, 0:2, :].transpose(1, 0, 2).reshape(N_ASSIGN)
    used1 = meta[0, :1]
    counts = meta[1, :MOE_EXPERTS]
    pad_start = meta[2, :MOE_EXPERTS]
    xs = _dispatch(dest_flat, counts, pad_start, used1, xn)

    y = _ffn(counts, pad_start, used1, xs, expert_w_gate[0], expert_w_up[0], expert_w_down[0])
    out = _combine(dest_flat, h1, rw, final_norm_g[None, :], y)
    return out.reshape(B, S, D)
```

```python
import math

import jax
import jax.numpy as jnp
import numpy as np
from jax import lax
from jax.experimental import pallas as pl
from jax.experimental.pallas import tpu as pltpu

F32 = jnp.float32
BF16 = jnp.bfloat16
I32 = jnp.int32
U32 = jnp.uint32

D_MODEL = 1024
SEQ = 16384
CHUNK = 64
EPS = 1e-6

DA_HEADS = 4
DA_HEAD_DIM = 64
DA_V_DIM = 128
DA_WIDTH = 512
ALIBI_MAX = 8.0
RET_HEADS = 4
RET_QK_DIM = 64
RET_V_DIM = 128
RET_WIDTH = 512
W_IN_COLS = 3072
T_ROWS = 512
MAIN_COLS = 2048
DK_OFF = 0
RQ_OFF = 512
RK_OFF = 768
RV_OFF = 1024
RG_OFF = 1536

MOE_GROUPS = 4
MOE_EXPERTS_PER_GROUP = 8
MOE_EXPERTS = 32
MOE_HIDDEN = 512
LAMBDA_INIT = 0.8 - 0.6 * math.exp(-0.3 * 0)

LANES = 128
X_ROWS = 4
VMEM_LIMIT = 56 * 1024 * 1024

PROJ_TM = 512
ROUTER_TM = 1024
ATT_T = 512
RET_C = 256
PLAN_T = 512
FFN_B = 512
N_ASSIGN = 2 * SEQ
N_BLOCKS = N_ASSIGN // FFN_B + MOE_EXPERTS
N_BUF = N_BLOCKS * FFN_B
COMB_TM = 1024
DISP_TM = 2048
DMA_UNROLL = 8


def _params(sem):
    return pltpu.CompilerParams(dimension_semantics=sem, vmem_limit_bytes=VMEM_LIMIT)


def _pack_rows(v):
    bits = lax.bitcast_convert_type(v.astype(BF16).astype(F32), U32)
    return [(bits[:, c * LANES:(c + 1) * LANES] >> 16)
            | (bits[:, (c + X_ROWS) * LANES:(c + X_ROWS + 1) * LANES] & jnp.uint32(0xFFFF0000))
            for c in range(X_ROWS)]


def _unpack_rows(words):
    lo = [lax.bitcast_convert_type(w << 16, F32) for w in words]
    hi = [lax.bitcast_convert_type(w & jnp.uint32(0xFFFF0000), F32) for w in words]
    return jnp.concatenate(lo + hi, axis=1)


def _retention_block(q_all, k_all, v_all, g_all, cd_ref, dec_ref, qdec_ref, kdec_ref, st_sc):
    outs = []
    for h in range(RET_HEADS):
        qk = slice(h * RET_QK_DIM, (h + 1) * RET_QK_DIM)
        vv = slice(h * RET_V_DIM, (h + 1) * RET_V_DIM)
        q = q_all[:, qk]
        k = k_all[:, qk]
        v = v_all[:, vv]
        g = g_all[:, vv]
        s = lax.dot_general(q, k, (((1,), (1,)), ((), ())),
                            preferred_element_type=F32) * dec_ref[h]
        intra = jnp.dot(s.astype(BF16), v, preferred_element_type=F32)
        st = st_sc[h]
        cross = jnp.dot(q, st.astype(BF16), preferred_element_type=F32) * qdec_ref[h]
        kd = (k.astype(F32) * kdec_ref[h]).astype(BF16)
        st_sc[h] = st * cd_ref[h] + lax.dot_general(kd, v, (((0,), (0,)), ((), ())),
                                                    preferred_element_type=F32)
        o = intra + cross
        o = o * lax.rsqrt(jnp.mean(o * o, axis=-1, keepdims=True) + EPS)
        outs.append(((g / (1.0 + jnp.exp(-g))) * o).astype(BF16))
    return outs


def _inproj_kernel(cd_ref, x_ref, g_ref, win_ref, dec_ref, qdec_ref, kdec_ref,
                   k_ref, qt_ref, vt_ref, or_ref, st_sc, w_ref, wq_ref, wv_ref):
    @pl.when(pl.program_id(0) == 0)
    def _():
        st_sc[...] = jnp.zeros_like(st_sc)
        cols = DA_WIDTH
        w_ref[:, DK_OFF:RQ_OFF] = win_ref[:, cols:2 * cols].astype(BF16)
        w_ref[:, RQ_OFF:RK_OFF] = win_ref[:, 3 * cols:3 * cols + 256].astype(BF16)
        w_ref[:, RK_OFF:RV_OFF] = (win_ref[:, 3 * cols + 256:4 * cols]
                                   * (RET_QK_DIM ** -0.5)).astype(BF16)
        w_ref[:, RV_OFF:MAIN_COLS] = win_ref[:, 4 * cols:6 * cols].astype(BF16)
        step = 256
        for r in range(D_MODEL // step):
            rows = slice(r * step, (r + 1) * step)
            wq_ref[:, rows] = (win_ref[rows, 0:cols] * (DA_HEAD_DIM ** -0.5)).T.astype(BF16)
            wv_ref[:, rows] = win_ref[rows, 2 * cols:3 * cols].T.astype(BF16)

    x = x_ref[...]
    var = jnp.mean(x * x, axis=-1, keepdims=True)
    xn = (x * lax.rsqrt(var + EPS) * g_ref[...]).astype(BF16)

    def proj(lo, hi):
        return jnp.dot(xn, w_ref[:, lo:hi], preferred_element_type=F32)

    k_ref[...] = proj(DK_OFF, DK_OFF + DA_WIDTH).astype(BF16)
    nt = (((1,), (1,)), ((), ()))
    qt = lax.dot_general(wq_ref[...], xn, nt, preferred_element_type=F32)
    qt_ref[...] = qt.astype(BF16).reshape(DA_HEADS, 1, 2 * DA_HEAD_DIM, PROJ_TM)
    vt = lax.dot_general(wv_ref[...], xn, nt, preferred_element_type=F32)
    vt_ref[...] = vt.astype(BF16).reshape(DA_HEADS, 1, DA_V_DIM, PROJ_TM)

    rq = proj(RQ_OFF, RK_OFF).astype(BF16)
    rk = proj(RK_OFF, RV_OFF).astype(BF16)
    rv = proj(RV_OFF, RG_OFF).astype(BF16)
    rg = proj(RG_OFF, MAIN_COLS)
    for blk in range(PROJ_TM // RET_C):
        rows = slice(blk * RET_C, (blk + 1) * RET_C)
        outs = _retention_block(rq[rows], rk[rows], rv[rows], rg[rows],
                                cd_ref, dec_ref, qdec_ref, kdec_ref, st_sc)
        for h in range(RET_HEADS):
            or_ref[rows, h * RET_V_DIM:(h + 1) * RET_V_DIM] = outs[h]


def _inproj(x2, g, w_in, cd, dec, qdec, kdec):
    C = RET_C
    t_shape = jax.ShapeDtypeStruct((DA_HEADS, SEQ // PROJ_TM, LANES, PROJ_TM), BF16)
    t_spec = pl.BlockSpec((DA_HEADS, 1, LANES, PROJ_TM), lambda i: (0, i, 0, 0))
    return pl.pallas_call(
        _inproj_kernel,
        out_shape=(jax.ShapeDtypeStruct((SEQ, DA_WIDTH), BF16), t_shape, t_shape,
                   jax.ShapeDtypeStruct((SEQ, RET_WIDTH), BF16)),
        grid=(SEQ // PROJ_TM,),
        in_specs=[
            pl.BlockSpec(memory_space=pltpu.SMEM),
            pl.BlockSpec((PROJ_TM, D_MODEL), lambda i: (i, 0)),
            pl.BlockSpec((1, D_MODEL), lambda i: (0, 0)),
            pl.BlockSpec((D_MODEL, W_IN_COLS), lambda i: (0, 0), pipeline_mode=pl.Buffered(1)),
            pl.BlockSpec((RET_HEADS, C, C), lambda i: (0, 0, 0)),
            pl.BlockSpec((RET_HEADS, C, 1), lambda i: (0, 0, 0)),
            pl.BlockSpec((RET_HEADS, C, 1), lambda i: (0, 0, 0)),
        ],
        out_specs=(pl.BlockSpec((PROJ_TM, DA_WIDTH), lambda i: (i, 0)), t_spec, t_spec,
                   pl.BlockSpec((PROJ_TM, RET_WIDTH), lambda i: (i, 0))),
        scratch_shapes=[
            pltpu.VMEM((RET_HEADS, RET_QK_DIM, RET_V_DIM), F32),
            pltpu.VMEM((D_MODEL, MAIN_COLS), BF16),
            pltpu.VMEM((T_ROWS, D_MODEL), BF16),
            pltpu.VMEM((T_ROWS, D_MODEL), BF16),
        ],
        compiler_params=_params(("arbitrary",)),
        name="inproj_retention",
    )(cd, x2, g, w_in, dec, qdec, kdec)


ACC_ROWS = DA_V_DIM + 16


N_QT = SEQ // ATT_T
N_OFF = N_QT * (N_QT - 1) // 2


def _pipeline3(n_pos, scores, accumulate):
    scores(0, 0)
    scores(1, 1)
    steady = n_pos - 2

    def triple(k, carry):
        t = 3 * k
        accumulate(t, 0)
        scores(t + 2, 2)
        accumulate(t + 1, 1)
        scores(t + 3, 0)
        accumulate(t + 2, 2)
        scores(t + 4, 1)
        return carry

    lax.fori_loop(0, steady // 3, triple, 0)
    t0 = steady // 3 * 3
    rem = steady - t0
    accumulate(t0, 0)
    if rem >= 1:
        scores(t0 + 2, 2)
    accumulate(t0 + 1, 1)
    if rem == 2:
        scores(t0 + 3, 0)
    if rem >= 1:
        accumulate(t0 + 2, 2)
    if rem == 2:
        accumulate(t0 + 3, 0)


def _attn_kernel(slope_ref, jt_ref, it_ref, qt_ref, k_ref, vt_ref, qa_ref, ka1_ref, ka2_ref,
                 dtab_ref, lq1_ref, lk1_ref, lq2_ref, lk2_ref, g_ref, o_ref,
                 m_sc, acc_sc, s0_sc, s1_sc, s2_sc, mx0_sc, mx1_sc, mx2_sc):
    T = ATT_T
    h = pl.program_id(0)
    slope = slope_ref[h]
    qa = qa_ref[0]
    lane = lax.broadcasted_iota(I32, (T, LANES), 1)
    sums_row = (lax.broadcasted_iota(I32, (16, T), 0) == 0).astype(BF16)
    s_bufs = (s0_sc, s1_sc, s2_sc)
    mx_bufs = (mx0_sc, mx1_sc, mx2_sc)

    HALF = T // 2

    def operands(j, i):
        kt = k_ref[pl.ds(pl.multiple_of(j * T, T), T), :]
        ks = (jnp.where(lane < DA_HEAD_DIM, kt, ka1_ref[0]),
              jnp.where(lane >= DA_HEAD_DIM, kt, ka2_ref[0]))
        qt = qt_ref[0, i]
        qw = (jnp.concatenate([qt[0:DA_HEAD_DIM], qa], axis=0),
              jnp.concatenate([qa, qt[DA_HEAD_DIM:]], axis=0))
        return ks, qw

    def scores(j, i, buf):
        ks, qw = operands(j, i)
        for mp in range(2):
            s = jnp.dot(ks[mp], qw[mp], preferred_element_type=F32)
            s_bufs[buf][mp] = s
            mx_bufs[buf][mp] = jnp.max(s, axis=0, keepdims=True)

    def scores_diag(i, buf):
        ks, qw = operands(i, i)
        for mp in range(2):
            left = (jnp.dot(ks[mp][0:HALF], qw[mp][:, 0:HALF], preferred_element_type=F32)
                    + dtab_ref[0, 0:HALF, 0:HALF])
            right = (jnp.dot(ks[mp], qw[mp][:, HALF:], preferred_element_type=F32)
                     + dtab_ref[0, :, HALF:])
            s_bufs[buf][mp, 0:HALF, 0:HALF] = left
            s_bufs[buf][mp, :, HALF:] = right
            mx_bufs[buf][mp] = jnp.concatenate(
                [jnp.max(left, axis=0, keepdims=True), jnp.max(right, axis=0, keepdims=True)],
                axis=1)

    def update(i, mp, m_prev, m_new, pv):
        acc_sc[i, mp] = jnp.exp(m_prev - m_new) * acc_sc[i, mp] + pv
        m_sc[i, mp] = m_new

    def accumulate(j, i, buf):
        c = slope * lax.convert_element_type((i - j) * T, F32)
        vte = jnp.concatenate([vt_ref[0, j], sums_row], axis=0)
        for mp in range(2):
            m_prev = m_sc[i, mp]
            m_new = jnp.maximum(m_prev, mx_bufs[buf][mp] - c)
            p = jnp.exp(s_bufs[buf][mp] - (m_new + c)).astype(BF16)
            update(i, mp, m_prev, m_new, jnp.dot(vte, p, preferred_element_type=F32))

    def accumulate_diag(i, buf):
        vte = jnp.concatenate([vt_ref[0, i], sums_row], axis=0)
        for mp in range(2):
            m_prev = m_sc[i, mp]
            m_new = jnp.maximum(m_prev, mx_bufs[buf][mp])
            p_left = jnp.exp(s_bufs[buf][mp, 0:HALF, 0:HALF] - m_new[:, 0:HALF]).astype(BF16)
            p_right = jnp.exp(s_bufs[buf][mp, :, HALF:] - m_new[:, HALF:]).astype(BF16)
            pv = jnp.concatenate(
                [jnp.dot(vte[:, 0:HALF], p_left, preferred_element_type=F32),
                 jnp.dot(vte, p_right, preferred_element_type=F32)], axis=1)
            update(i, mp, m_prev, m_new, pv)

    m_sc[...] = jnp.full_like(m_sc, -jnp.inf)
    acc_sc[...] = jnp.zeros_like(acc_sc)
    _pipeline3(N_QT, scores_diag, accumulate_diag)
    _pipeline3(N_OFF,
               lambda pos, buf: scores(jt_ref[pos], it_ref[pos], buf),
               lambda pos, buf: accumulate(jt_ref[pos], it_ref[pos], buf))

    lam = (jnp.exp(jnp.sum(lq1_ref[...] * lk1_ref[...], axis=1, keepdims=True))
           - jnp.exp(jnp.sum(lq2_ref[...] * lk2_ref[...], axis=1, keepdims=True))
           + LAMBDA_INIT)

    def finish(i):
        a1 = acc_sc[i, 0]
        a2 = acc_sc[i, 1]
        ot = (a1[0:DA_V_DIM] / a1[DA_V_DIM:DA_V_DIM + 1]
              - lam * (a2[0:DA_V_DIM] / a2[DA_V_DIM:DA_V_DIM + 1]))
        o = ot.T
        var = jnp.mean(o * o, axis=-1, keepdims=True)
        o = (o * lax.rsqrt(var + EPS) * g_ref[...]) * (1.0 - LAMBDA_INIT)
        o_ref[pl.ds(pl.multiple_of(i * T, T), T), :] = o.astype(BF16)

    def finish_pair(k, carry):
        finish(2 * k)
        finish(2 * k + 1)
        return carry

    lax.fori_loop(0, N_QT // 2, finish_pair, 0)


def _attention(proj, qt4, vt4, slopes, qa, ka1, ka2, dtab, lq1, lk1, lq2, lk2, subln_g):
    T = ATT_T
    vec64 = pl.BlockSpec((1, DA_HEAD_DIM), lambda h: (0, 0))
    per_head = lambda a, b: pl.BlockSpec((1, a, b), lambda h: (h, 0, 0))
    slab = pl.BlockSpec
    single = lambda shape, imap: pl.BlockSpec(shape, imap, pipeline_mode=pl.Buffered(1))
    smem = pl.BlockSpec(memory_space=pltpu.SMEM)
    it_tab, jt_tab = np.tril_indices(N_QT, -1)
    return pl.pallas_call(
        _attn_kernel,
        out_shape=jax.ShapeDtypeStruct((SEQ, DA_WIDTH), BF16),
        grid=(DA_HEADS,),
        in_specs=[
            smem, smem, smem,
            slab((1, N_QT, LANES, T), lambda h: (h, 0, 0, 0)),
            slab((SEQ, LANES), lambda h: (0, DK_OFF // LANES + h)),
            slab((1, N_QT, LANES, T), lambda h: (h, 0, 0, 0)),
            per_head(DA_HEAD_DIM, T), per_head(T, LANES), per_head(T, LANES),
            single((1, T, T), lambda h: (h, 0, 0)),
            vec64, vec64, vec64, vec64,
            pl.BlockSpec((1, DA_V_DIM), lambda h: (0, 0)),
        ],
        out_specs=single((SEQ, LANES), lambda h: (0, h)),
        scratch_shapes=[
            pltpu.VMEM((N_QT, 2, 1, T), F32),
            pltpu.VMEM((N_QT, 2, ACC_ROWS, T), F32),
            pltpu.VMEM((2, T, T), F32),
            pltpu.VMEM((2, T, T), F32),
            pltpu.VMEM((2, T, T), F32),
            pltpu.VMEM((2, 1, T), F32),
            pltpu.VMEM((2, 1, T), F32),
            pltpu.VMEM((2, 1, T), F32),
        ],
        compiler_params=_params(("arbitrary",)),
        name="diff_attention",
    )(slopes, jnp.asarray(jt_tab, I32), jnp.asarray(it_tab, I32), qt4, proj, vt4, qa, ka1, ka2,
      dtab, lq1, lk1, lq2, lk2, subln_g)


def _outproj_router_kernel(x_ref, oda_ref, or_ref, wo32_ref, g_ref, wr_ref, br_ref,
                           h1_ref, xn_ref, ri_ref, rw_ref, cnt_ref, wo_ref):
    @pl.when(pl.program_id(0) == 0)
    def _():
        wo_ref[...] = wo32_ref[...].astype(BF16)

    h1 = (x_ref[...]
          + jnp.dot(oda_ref[...], wo_ref[0:DA_WIDTH, :], preferred_element_type=F32)
          + jnp.dot(or_ref[...], wo_ref[DA_WIDTH:, :], preferred_element_type=F32))
    h1_ref[...] = h1
    var = jnp.mean(h1 * h1, axis=-1, keepdims=True)
    xn = h1 * lax.rsqrt(var + EPS) * g_ref[...]
    for c, words in enumerate(_pack_rows(xn)):
        xn_ref[pl.ds(c, ROUTER_TM, stride=X_ROWS), :] = words
    x_hi = xn.astype(BF16)
    x_lo = (xn - x_hi.astype(F32)).astype(BF16)
    both = jnp.dot(x_hi, wr_ref[...], preferred_element_type=F32)
    logits = (both[:, :LANES] + both[:, LANES:]
              + jnp.dot(x_lo, wr_ref[:, :LANES], preferred_element_type=F32)) + br_ref[...]
    lane = lax.broadcasted_iota(I32, logits.shape, 1)
    neg = jnp.float32(-jnp.inf)
    big = jnp.int32(1 << 20)
    gl = jnp.where(lane < MOE_GROUPS, logits, neg)
    gmax = jnp.max(gl, axis=1, keepdims=True)
    gidx = jnp.min(jnp.where(gl == gmax, lane, big), axis=1, keepdims=True)
    gsum = jnp.sum(jnp.exp(gl - gmax), axis=1, keepdims=True)
    gp = 1.0 / gsum
    lo = MOE_GROUPS + gidx * MOE_EXPERTS_PER_GROUP
    el = jnp.where((lane >= lo) & (lane < lo + MOE_EXPERTS_PER_GROUP), logits, neg)
    v1 = jnp.max(el, axis=1, keepdims=True)
    i1 = jnp.min(jnp.where(el == v1, lane, big), axis=1, keepdims=True)
    el2 = jnp.where(lane == i1, neg, el)
    v2 = jnp.max(el2, axis=1, keepdims=True)
    i2 = jnp.min(jnp.where(el2 == v2, lane, big), axis=1, keepdims=True)
    t = jnp.exp(v2 - v1)
    w1 = gp / (1.0 + t)
    w2 = gp * t / (1.0 + t)
    ri_ref[...] = jnp.where(lane == 0, i1 - MOE_GROUPS,
                            jnp.where(lane == 1, i2 - MOE_GROUPS, 0))
    rw_ref[...] = jnp.where(lane == 0, w1, jnp.where(lane == 1, w2, 0.0))

    @pl.when(pl.program_id(0) == 0)
    def _():
        cnt_ref[...] = jnp.zeros_like(cnt_ref)

    chosen = (lane == i1 - MOE_GROUPS) | (lane == i2 - MOE_GROUPS)
    cnt_ref[...] += jnp.sum(chosen.astype(F32), axis=0, keepdims=True)


def _outproj_router(x2, o_da, o_r, w_out, g, wr, br):
    tm = ROUTER_TM
    row = lambda w: pl.BlockSpec((tm, w), lambda i: (i, 0))
    full = lambda a, b: pl.BlockSpec((a, b), lambda i: (0, 0))
    return pl.pallas_call(
        _outproj_router_kernel,
        out_shape=(
            jax.ShapeDtypeStruct((SEQ, D_MODEL), F32),
            jax.ShapeDtypeStruct((SEQ * X_ROWS, LANES), U32),
            jax.ShapeDtypeStruct((SEQ, LANES), I32),
            jax.ShapeDtypeStruct((SEQ, LANES), F32),
            jax.ShapeDtypeStruct((8, LANES), F32),
        ),
        grid=(SEQ // tm,),
        in_specs=[row(D_MODEL), row(DA_WIDTH), row(RET_WIDTH),
                  pl.BlockSpec((D_MODEL, D_MODEL), lambda i: (0, 0),
                               pipeline_mode=pl.Buffered(1)),
                  full(1, D_MODEL), full(D_MODEL, 2 * LANES), full(1, LANES)],
        out_specs=(row(D_MODEL), pl.BlockSpec((tm * X_ROWS, LANES), lambda i: (i, 0)),
                   row(LANES), row(LANES), full(8, LANES)),
        scratch_shapes=[pltpu.VMEM((D_MODEL, D_MODEL), BF16)],
        compiler_params=_params(("arbitrary",)),
        name="outproj_router",
    )(x2, o_da, o_r, w_out, g, wr, br)


def _plan_kernel(ri_ref, cnt_ref, dest_ref, used_ref):
    TT = PLAN_T
    lane = lax.broadcasted_iota(I32, (TT, LANES), 1)

    def onehots(t):
        r = ri_ref[pl.ds(pl.multiple_of(t * TT, TT), TT), :]
        return lane == r[:, 0:1], lane == r[:, 1:2]

    counts8 = cnt_ref[...].astype(I32)
    shift = FFN_B.bit_length() - 1
    padded = ((counts8 + (FFN_B - 1)) >> shift) << shift
    lane8 = lax.broadcasted_iota(I32, (8, LANES), 1)
    pad_end = padded
    sh = 1
    while sh < LANES:
        pad_end = pad_end + jnp.where(lane8 >= sh, pltpu.roll(pad_end, sh, axis=1), 0)
        sh *= 2
    pad_start = pad_end - padded

    ltri = (lax.broadcasted_iota(I32, (TT, TT), 0)
            > lax.broadcasted_iota(I32, (TT, TT), 1)).astype(BF16)

    def dest_body(t, carry):
        oh1, oh2 = onehots(t)
        a = (oh1 | oh2).astype(F32)
        base = jnp.dot(ltri, a.astype(BF16), preferred_element_type=F32) + carry
        d1 = jnp.sum(jnp.where(oh1, base, 0.0), axis=1, keepdims=True)
        d2 = jnp.sum(jnp.where(oh2, base, 0.0), axis=1, keepdims=True)
        both = jnp.where(lane == 0, d1, jnp.where(lane == 1, d2, 0.0))
        dest_ref[t] = both.T[0:8, :].astype(I32)
        return carry + jnp.sum(a, axis=0, keepdims=True)

    lax.fori_loop(0, SEQ // TT, dest_body, pad_start[0:1].astype(F32))

    total = jnp.max(pad_end, axis=1, keepdims=True)
    row8 = lax.broadcasted_iota(I32, (8, LANES), 0)
    used_ref[...] = jnp.where(row8 == 0, jnp.broadcast_to(total >> shift, (8, LANES)),
                              jnp.where(row8 == 1, counts8, pad_start))


def _plan(ri, cnt):
    return pl.pallas_call(
        _plan_kernel,
        out_shape=(
            jax.ShapeDtypeStruct((SEQ // PLAN_T, 8, PLAN_T), I32),
            jax.ShapeDtypeStruct((8, LANES), I32),
        ),
        compiler_params=pltpu.CompilerParams(vmem_limit_bytes=VMEM_LIMIT),
        name="route_plan",
    )(ri, cnt)


PAD_BITS = FFN_B.bit_length() - 1


def _pad_fill_copies(e, cnt_ref, pst_ref, zero_sc, xs_hbm, zsem):
    cnt = cnt_ref[e]
    pad = (-cnt) & (FFN_B - 1)
    row = pst_ref[e] + cnt
    out = []
    for bit in reversed(range(PAD_BITS)):
        n = 1 << bit
        start = row + ((pad >> (bit + 1)) << (bit + 1))
        copy = pltpu.make_async_copy(
            zero_sc.at[pl.ds(0, n * X_ROWS)],
            xs_hbm.at[pl.ds(pl.multiple_of(start * X_ROWS, X_ROWS), n * X_ROWS)], zsem)
        out.append(((pad & n) != 0, copy))
    return out


def _unused_block_copies(b, zero_sc, xs_hbm, zsem):
    half = FFN_B // 2 * X_ROWS
    return [pltpu.make_async_copy(
        zero_sc, xs_hbm.at[pl.ds(pl.multiple_of((2 * b + k) * half, half), half)], zsem)
        for k in range(2)]


def _dispatch_kernel(dest_ref, cnt_ref, pst_ref, used_ref, xn_ref, xs_hbm, zero_sc, sem, zsem):
    tm = DISP_TM
    i = pl.program_id(0)

    @pl.when(i == 0)
    def _():
        zero_sc[...] = jnp.zeros_like(zero_sc)

        def fill(e, carry):
            for cond, copy in _pad_fill_copies(e, cnt_ref, pst_ref, zero_sc, xs_hbm, zsem):
                pl.when(cond)(copy.start)
            return carry

        lax.fori_loop(0, MOE_EXPERTS, fill, 0)

        def fill_block(b, carry):
            for copy in _unused_block_copies(b, zero_sc, xs_hbm, zsem):
                copy.start()
            return carry

        lax.fori_loop(used_ref[0], N_BLOCKS, fill_block, 0)

    def issue(it, carry):
        for u in range(DMA_UNROLL):
            r = it * DMA_UNROLL + u
            t = i * tm + r
            src = xn_ref.at[pl.ds(pl.multiple_of(r * X_ROWS, X_ROWS), X_ROWS)]
            for kk in range(2):
                d = pl.multiple_of(dest_ref[kk * SEQ + t] * X_ROWS, X_ROWS)
                pltpu.make_async_copy(src, xs_hbm.at[pl.ds(d, X_ROWS)], sem).start(priority=kk)
        return carry

    lax.fori_loop(0, tm // DMA_UNROLL, issue, 0)
    for _ in range(2):
        pltpu.make_async_copy(xn_ref, xs_hbm.at[pl.ds(0, tm * X_ROWS)], sem).wait()

    @pl.when(i == 0)
    def _():
        def drain(e, carry):
            for cond, copy in _pad_fill_copies(e, cnt_ref, pst_ref, zero_sc, xs_hbm, zsem):
                pl.when(cond)(copy.wait)
            return carry

        lax.fori_loop(0, MOE_EXPERTS, drain, 0)

        def drain_block(b, carry):
            for copy in _unused_block_copies(b, zero_sc, xs_hbm, zsem):
                copy.wait()
            return carry

        lax.fori_loop(used_ref[0], N_BLOCKS, drain_block, 0)


def _dispatch(dest_flat, counts, pad_start, used, xn3):
    tm = DISP_TM
    return pl.pallas_call(
        _dispatch_kernel,
        out_shape=jax.ShapeDtypeStruct((N_BUF * X_ROWS, LANES), U32),
        grid_spec=pltpu.PrefetchScalarGridSpec(
            num_scalar_prefetch=4,
            grid=(SEQ // tm,),
            in_specs=[pl.BlockSpec((tm * X_ROWS, LANES), lambda i, d, c, p, u: (i, 0))],
            out_specs=pl.BlockSpec(memory_space=pl.ANY),
            scratch_shapes=[
                pltpu.VMEM((FFN_B // 2 * X_ROWS, LANES), U32),
                pltpu.SemaphoreType.DMA(()),
                pltpu.SemaphoreType.DMA(()),
            ],
        ),
        compiler_params=_params(("arbitrary",)),
        name="moe_dispatch",
    )(dest_flat, counts, pad_start, used, xn3)


BLOCK_COPY_PRIORITY = 1


def _ffn_kernel(cnt_ref, pst_ref, used_ref, xs_hbm, wg_ref, wu_ref, wd_ref, y_hbm,
                xbuf, ybuf, zero_sc, wg_bf, wu_bf, wd_bf, sem_in, sem_out, zsem):
    B = FFN_B
    e = pl.program_id(0)
    n = (cnt_ref[e] + (B - 1)) >> PAD_BITS
    s0 = pst_ref[e] >> PAD_BITS

    def rows(blk):
        size = B * X_ROWS
        return pl.ds(pl.multiple_of(blk * size, size), size)

    def fetch(blk, slot):
        return pltpu.make_async_copy(xs_hbm.at[rows(blk)], xbuf.at[slot], sem_in.at[slot])

    def flush(blk, slot):
        return pltpu.make_async_copy(ybuf.at[slot], y_hbm.at[rows(blk)], sem_out.at[slot])

    used = used_ref[0]

    @pl.when(e == 0)
    def _():
        fetch(0, 0).start(priority=BLOCK_COPY_PRIORITY)

    @pl.when(n > 0)
    def _():
        wg_bf[...] = wg_ref[0].astype(BF16)
        wu_bf[...] = wu_ref[0].astype(BF16)
        wd_bf[...] = wd_ref[0].astype(BF16)

    def body(j, carry):
        blk = s0 + j
        slot = blk % 2

        @pl.when(blk + 1 < used)
        def _():
            fetch(blk + 1, 1 - slot).start(priority=BLOCK_COPY_PRIORITY)

        fetch(blk, slot).wait()

        @pl.when(blk >= 2)
        def _():
            flush(blk - 2, slot).wait()

        def swiglu(rows):
            x = _unpack_rows([xbuf[slot, pl.ds(c, rows, stride=X_ROWS), :]
                              for c in range(X_ROWS)]).astype(BF16)
            hg = jnp.dot(x, wg_bf[...], preferred_element_type=F32)
            hu = jnp.dot(x, wu_bf[...], preferred_element_type=F32)
            hh = ((hg / (1.0 + jnp.exp(-hg))) * hu).astype(BF16)
            y = jnp.dot(hh, wd_bf[...], preferred_element_type=F32)
            for c, words in enumerate(_pack_rows(y)):
                ybuf[slot, pl.ds(c, rows, stride=X_ROWS), :] = words

        valid = cnt_ref[e] - j * B

        @pl.when(valid > B // 2)
        def _():
            swiglu(B)

        @pl.when(valid <= B // 2)
        def _():
            swiglu(B // 2)
            tail = pl.ds(B // 2 * X_ROWS, B // 2 * X_ROWS)
            ybuf[slot, tail, :] = jnp.zeros((B // 2 * X_ROWS, LANES), U32)

        flush(blk, slot).start(priority=BLOCK_COPY_PRIORITY)
        return carry

    lax.fori_loop(0, n, body, 0)

    @pl.when(e == MOE_EXPERTS - 1)
    def _():
        flush(used - 1, (used + 1) % 2).wait()
        flush(used - 2, used % 2).wait()
        zero_sc[...] = jnp.zeros_like(zero_sc)

        def fill(b, carry):
            pltpu.make_async_copy(zero_sc, y_hbm.at[rows(b)], zsem).start()
            return carry

        def drain(b, carry):
            pltpu.make_async_copy(zero_sc, y_hbm.at[rows(b)], zsem).wait()
            return carry

        lax.fori_loop(used_ref[0], N_BLOCKS, fill, 0)
        lax.fori_loop(used_ref[0], N_BLOCKS, drain, 0)


def _ffn(counts, pad_start, used, xs, w_gate, w_up, w_down):
    B = FFN_B
    wspec = lambda a, c: pl.BlockSpec((1, a, c), lambda e, cnt, pst, used: (e, 0, 0))
    return pl.pallas_call(
        _ffn_kernel,
        out_shape=jax.ShapeDtypeStruct((N_BUF * X_ROWS, LANES), U32),
        grid_spec=pltpu.PrefetchScalarGridSpec(
            num_scalar_prefetch=3,
            grid=(MOE_EXPERTS,),
            in_specs=[
                pl.BlockSpec(memory_space=pl.ANY),
                wspec(D_MODEL, MOE_HIDDEN),
                wspec(D_MODEL, MOE_HIDDEN),
                wspec(MOE_HIDDEN, D_MODEL),
            ],
            out_specs=pl.BlockSpec(memory_space=pl.ANY),
            scratch_shapes=[
                pltpu.VMEM((2, B * X_ROWS, LANES), U32),
                pltpu.VMEM((2, B * X_ROWS, LANES), U32),
                pltpu.VMEM((B * X_ROWS, LANES), U32),
                pltpu.VMEM((D_MODEL, MOE_HIDDEN), BF16),
                pltpu.VMEM((D_MODEL, MOE_HIDDEN), BF16),
                pltpu.VMEM((MOE_HIDDEN, D_MODEL), BF16),
                pltpu.SemaphoreType.DMA((2,)),
                pltpu.SemaphoreType.DMA((2,)),
                pltpu.SemaphoreType.DMA(()),
            ],
        ),
        compiler_params=_params(("arbitrary",)),
        name="expert_ffn",
    )(counts, pad_start, used, xs, w_gate, w_up, w_down)


def _combine_kernel(dest_ref, h1_ref, rw_ref, g_ref, y_hbm, o_ref, ybuf, sem):
    tm = COMB_TM
    i = pl.program_id(0)

    def gather(tile, slot):
        def issue(it, carry):
            for u in range(DMA_UNROLL):
                r = it * DMA_UNROLL + u
                t = tile * tm + r
                for kk in range(2):
                    d = pl.multiple_of(dest_ref[kk * SEQ + t] * X_ROWS, X_ROWS)
                    pltpu.make_async_copy(
                        y_hbm.at[pl.ds(d, X_ROWS)],
                        ybuf.at[slot, kk, pl.ds(pl.multiple_of(r * X_ROWS, X_ROWS), X_ROWS)],
                        sem.at[slot, kk]).start(priority=kk)
            return carry

        lax.fori_loop(0, tm // DMA_UNROLL, issue, 0)

    @pl.when(i == 0)
    def _():
        gather(0, 0)

    @pl.when(i + 1 < pl.num_programs(0))
    def _():
        gather(i + 1, (i + 1) % 2)

    slot = i % 2
    for kk in range(2):
        pltpu.make_async_copy(y_hbm.at[pl.ds(0, tm * X_ROWS)], ybuf.at[slot, kk],
                              sem.at[slot, kk]).wait()
    w = rw_ref[...]
    ys = [_unpack_rows([ybuf[slot, kk, pl.ds(c, tm, stride=X_ROWS), :] for c in range(X_ROWS)])
          for kk in range(2)]
    h = h1_ref[...] + w[:, 0:1] * ys[0] + w[:, 1:2] * ys[1]
    var = jnp.mean(h * h, axis=-1, keepdims=True)
    o_ref[...] = h * lax.rsqrt(var + EPS) * g_ref[...]


def _combine(dest_flat, h1, rw, g, y):
    tm = COMB_TM
    return pl.pallas_call(
        _combine_kernel,
        out_shape=jax.ShapeDtypeStruct((SEQ, D_MODEL), F32),
        grid_spec=pltpu.PrefetchScalarGridSpec(
            num_scalar_prefetch=1,
            grid=(SEQ // tm,),
            in_specs=[
                pl.BlockSpec((tm, D_MODEL), lambda i, d: (i, 0)),
                pl.BlockSpec((tm, LANES), lambda i, d: (i, 0)),
                pl.BlockSpec((1, D_MODEL), lambda i, d: (0, 0)),
                pl.BlockSpec(memory_space=pl.ANY),
            ],
            out_specs=pl.BlockSpec((tm, D_MODEL), lambda i, d: (i, 0)),
            scratch_shapes=[
                pltpu.VMEM((2, 2, tm * X_ROWS, LANES), U32),
                pltpu.SemaphoreType.DMA((2, 2)),
            ],
        ),
        compiler_params=_params(("arbitrary",)),
        name="moe_combine",
    )(dest_flat, h1, rw, g, y)


def _attention_tables():
    T = ATT_T
    f32 = np.float32
    slopes = np.exp2(-ALIBI_MAX * np.arange(1, DA_HEADS + 1, dtype=f32) / DA_HEADS).astype(f32)
    r = np.arange(T)
    hi = ((r // CHUNK) * CHUNK).astype(f32)
    lo = (r % CHUNK).astype(f32)
    sl = slopes[:, None]
    one_h = np.ones((DA_HEADS, T), f32)
    q_rows = np.stack([one_h, one_h, -sl * hi[None], -sl * lo[None]], axis=1)
    k_cols = np.stack([sl * hi[None], sl * lo[None], one_h, one_h], axis=-1)
    qa = np.zeros((DA_HEADS, DA_HEAD_DIM, T), f32)
    qa[:, 0:4, :] = q_rows
    ka1 = np.zeros((DA_HEADS, T, LANES), f32)
    ka1[:, :, DA_HEAD_DIM:DA_HEAD_DIM + 4] = k_cols
    ka2 = np.zeros((DA_HEADS, T, LANES), f32)
    ka2[:, :, 0:4] = k_cols
    rel = (r[:, None] - r[None, :]).astype(f32)
    allowed = (r[:, None] // CHUNK) <= (r[None, :] // CHUNK)
    fix = np.where(rel > 0, -2.0 * slopes[:, None, None] * rel[None], 0.0).astype(f32)
    dtab = np.where(allowed[None], fix, -np.inf).astype(f32)
    return (jnp.asarray(slopes), jnp.asarray(qa, BF16), jnp.asarray(ka1, BF16),
            jnp.asarray(ka2, BF16), jnp.asarray(dtab))


def _retention_tables():
    C = RET_C
    f32 = np.float32
    log_gamma = np.log1p(-np.exp2(-5.0 - np.arange(RET_HEADS, dtype=f32))).astype(f32)
    pos = np.arange(C, dtype=f32)
    rel = pos[:, None] - pos[None, :]
    dec = np.where(rel >= 0, np.exp(log_gamma[:, None, None] * np.maximum(rel, 0.0)), 0.0)
    qdec = np.exp(log_gamma[:, None] * (pos + 1.0)[None, :])[:, :, None]
    kdec = np.exp(log_gamma[:, None] * (C - 1 - pos)[None, :])[:, :, None]
    cd = np.exp(log_gamma * C)
    return tuple(jnp.asarray(t, F32) for t in (cd, dec, qdec, kdec))


def kernel(x, attn_norm_g, w_in, da_lambda_q1, da_lambda_k1, da_lambda_q2, da_lambda_k2,
           da_subln_g, w_out, ffn_norm_g, router_group_w, router_group_b, router_expert_w,
           router_expert_b, expert_w_gate, expert_w_up, expert_w_down, final_norm_g):
    B, S, D = x.shape
    assert (B, S, D) == (1, SEQ, D_MODEL)
    x2 = x.reshape(S, D)

    cd, dec, qdec, kdec = _retention_tables()
    slopes, qa, ka1, ka2, dtab = _attention_tables()
    k_da, qt4, vt4, o_r = _inproj(x2, attn_norm_g[0][None, :], w_in[0], cd, dec, qdec, kdec)

    o_da = _attention(k_da, qt4, vt4, slopes, qa, ka1, ka2, dtab, da_lambda_q1, da_lambda_k1,
                      da_lambda_q2, da_lambda_k2, da_subln_g)

    wr = jnp.zeros((D, LANES), F32)
    wr = wr.at[:, :MOE_GROUPS].set(router_group_w[0])
    wr = wr.at[:, MOE_GROUPS:MOE_GROUPS + MOE_EXPERTS].set(router_expert_w[0])
    br = jnp.zeros((1, LANES), F32)
    br = br.at[0, :MOE_GROUPS].set(router_group_b[0])
    br = br.at[0, MOE_GROUPS:MOE_GROUPS + MOE_EXPERTS].set(router_expert_b[0])
    wr_hi = wr.astype(BF16)
    wr_lo = (wr - wr_hi.astype(F32)).astype(BF16)
    h1, xn, ri, rw, cnt = _outproj_router(x2, o_da, o_r, w_out[0],
                                          ffn_norm_g[0][None, :],
                                          jnp.concatenate([wr_hi, wr_lo], axis=1), br)

    dest, meta = _plan(ri, cnt)
    dest_flat = dest[:, 0:2, :].transpose(1, 0, 2).reshape(N_ASSIGN)
    used1 = meta[0, :1]
    counts = meta[1, :MOE_EXPERTS]
    pad_start = meta[2, :MOE_EXPERTS]
    xs = _dispatch(dest_flat, counts, pad_start, used1, xn)

    y = _ffn(counts, pad_start, used1, xs, expert_w_gate[0], expert_w_up[0], expert_w_down[0])
    out = _combine(dest_flat, h1, rw, final_norm_g[None, :], y)
    return out.reshape(B, S, D)
```

```python
import math

import jax
import jax.numpy as jnp
import numpy as np
from jax import lax
from jax.experimental import pallas as pl
from jax.experimental.pallas import tpu as pltpu

F32 = jnp.float32
BF16 = jnp.bfloat16
I32 = jnp.int32
U32 = jnp.uint32

D_MODEL = 1024
SEQ = 16384
CHUNK = 64
EPS = 1e-6

DA_HEADS = 4
DA_HEAD_DIM = 64
DA_V_DIM = 128
DA_WIDTH = 512
ALIBI_MAX = 8.0
RET_HEADS = 4
RET_QK_DIM = 64
RET_V_DIM = 128
RET_WIDTH = 512
W_IN_COLS = 3072
T_ROWS = 512
MAIN_COLS = 2048
DK_OFF = 0
RQ_OFF = 512
RK_OFF = 768
RV_OFF = 1024
RG_OFF = 1536

MOE_GROUPS = 4
MOE_EXPERTS_PER_GROUP = 8
MOE_EXPERTS = 32
MOE_HIDDEN = 512
LAMBDA_INIT = 0.8 - 0.6 * math.exp(-0.3 * 0)

LANES = 128
X_ROWS = 4
VMEM_LIMIT = 56 * 1024 * 1024

PROJ_TM = 512
ROUTER_TM = 1024
ATT_T = 512
RET_C = 256
PLAN_T = 512
FFN_B = 512
N_ASSIGN = 2 * SEQ
N_BLOCKS = N_ASSIGN // FFN_B + MOE_EXPERTS
N_BUF = N_BLOCKS * FFN_B
COMB_TM = 512
DISP_TM = 1024
DMA_UNROLL = 8


def _params(sem):
    return pltpu.CompilerParams(dimension_semantics=sem, vmem_limit_bytes=VMEM_LIMIT)


def _pack_rows(v):
    bits = lax.bitcast_convert_type(v.astype(BF16).astype(F32), U32)
    return [(bits[:, c * LANES:(c + 1) * LANES] >> 16)
            | (bits[:, (c + X_ROWS) * LANES:(c + X_ROWS + 1) * LANES] & jnp.uint32(0xFFFF0000))
            for c in range(X_ROWS)]


def _unpack_rows(words):
    lo = [lax.bitcast_convert_type(w << 16, F32) for w in words]
    hi = [lax.bitcast_convert_type(w & jnp.uint32(0xFFFF0000), F32) for w in words]
    return jnp.concatenate(lo + hi, axis=1)


def _retention_block(q_all, k_all, v_all, g_all, cd_ref, dec_ref, qdec_ref, kdec_ref, st_sc):
    outs = []
    for h in range(RET_HEADS):
        qk = slice(h * RET_QK_DIM, (h + 1) * RET_QK_DIM)
        vv = slice(h * RET_V_DIM, (h + 1) * RET_V_DIM)
        q = q_all[:, qk]
        k = k_all[:, qk]
        v = v_all[:, vv]
        g = g_all[:, vv]
        s = lax.dot_general(q, k, (((1,), (1,)), ((), ())),
                            preferred_element_type=F32) * dec_ref[h]
        intra = jnp.dot(s.astype(BF16), v, preferred_element_type=F32)
        st = st_sc[h]
        cross = jnp.dot(q, st.astype(BF16), preferred_element_type=F32) * qdec_ref[h]
        kd = (k.astype(F32) * kdec_ref[h]).astype(BF16)
        st_sc[h] = st * cd_ref[h] + lax.dot_general(kd, v, (((0,), (0,)), ((), ())),
                                                    preferred_element_type=F32)
        o = intra + cross
        o = o * lax.rsqrt(jnp.mean(o * o, axis=-1, keepdims=True) + EPS)
        outs.append(((g / (1.0 + jnp.exp(-g))) * o).astype(BF16))
    return outs


def _inproj_kernel(cd_ref, x_ref, g_ref, win_ref, dec_ref, qdec_ref, kdec_ref,
                   k_ref, qt_ref, vt_ref, or_ref, st_sc, w_ref, wq_ref, wv_ref):
    @pl.when(pl.program_id(0) == 0)
    def _():
        st_sc[...] = jnp.zeros_like(st_sc)
        cols = DA_WIDTH
        w_ref[:, DK_OFF:RQ_OFF] = win_ref[:, cols:2 * cols].astype(BF16)
        w_ref[:, RQ_OFF:RK_OFF] = win_ref[:, 3 * cols:3 * cols + 256].astype(BF16)
        w_ref[:, RK_OFF:RV_OFF] = (win_ref[:, 3 * cols + 256:4 * cols]
                                   * (RET_QK_DIM ** -0.5)).astype(BF16)
        w_ref[:, RV_OFF:MAIN_COLS] = win_ref[:, 4 * cols:6 * cols].astype(BF16)
        step = 256
        for r in range(D_MODEL // step):
            rows = slice(r * step, (r + 1) * step)
            wq_ref[:, rows] = (win_ref[rows, 0:cols] * (DA_HEAD_DIM ** -0.5)).T.astype(BF16)
            wv_ref[:, rows] = win_ref[rows, 2 * cols:3 * cols].T.astype(BF16)

    x = x_ref[...]
    var = jnp.mean(x * x, axis=-1, keepdims=True)
    xn = (x * lax.rsqrt(var + EPS) * g_ref[...]).astype(BF16)

    def proj(lo, hi):
        return jnp.dot(xn, w_ref[:, lo:hi], preferred_element_type=F32)

    k_ref[...] = proj(DK_OFF, DK_OFF + DA_WIDTH).astype(BF16)
    nt = (((1,), (1,)), ((), ()))
    qt = lax.dot_general(wq_ref[...], xn, nt, preferred_element_type=F32)
    qt_ref[...] = qt.astype(BF16).reshape(DA_HEADS, 1, 2 * DA_HEAD_DIM, PROJ_TM)
    vt = lax.dot_general(wv_ref[...], xn, nt, preferred_element_type=F32)
    vt_ref[...] = vt.astype(BF16).reshape(DA_HEADS, 1, DA_V_DIM, PROJ_TM)

    rq = proj(RQ_OFF, RK_OFF).astype(BF16)
    rk = proj(RK_OFF, RV_OFF).astype(BF16)
    rv = proj(RV_OFF, RG_OFF).astype(BF16)
    rg = proj(RG_OFF, MAIN_COLS)
    for blk in range(PROJ_TM // RET_C):
        rows = slice(blk * RET_C, (blk + 1) * RET_C)
        outs = _retention_block(rq[rows], rk[rows], rv[rows], rg[rows],
                                cd_ref, dec_ref, qdec_ref, kdec_ref, st_sc)
        for h in range(RET_HEADS):
            or_ref[rows, h * RET_V_DIM:(h + 1) * RET_V_DIM] = outs[h]


def _inproj(x2, g, w_in, cd, dec, qdec, kdec):
    C = RET_C
    t_shape = jax.ShapeDtypeStruct((DA_HEADS, SEQ // PROJ_TM, LANES, PROJ_TM), BF16)
    t_spec = pl.BlockSpec((DA_HEADS, 1, LANES, PROJ_TM), lambda i: (0, i, 0, 0))
    return pl.pallas_call(
        _inproj_kernel,
        out_shape=(jax.ShapeDtypeStruct((SEQ, DA_WIDTH), BF16), t_shape, t_shape,
                   jax.ShapeDtypeStruct((SEQ, RET_WIDTH), BF16)),
        grid=(SEQ // PROJ_TM,),
        in_specs=[
            pl.BlockSpec(memory_space=pltpu.SMEM),
            pl.BlockSpec((PROJ_TM, D_MODEL), lambda i: (i, 0)),
            pl.BlockSpec((1, D_MODEL), lambda i: (0, 0)),
            pl.BlockSpec((D_MODEL, W_IN_COLS), lambda i: (0, 0), pipeline_mode=pl.Buffered(1)),
            pl.BlockSpec((RET_HEADS, C, C), lambda i: (0, 0, 0)),
            pl.BlockSpec((RET_HEADS, C, 1), lambda i: (0, 0, 0)),
            pl.BlockSpec((RET_HEADS, C, 1), lambda i: (0, 0, 0)),
        ],
        out_specs=(pl.BlockSpec((PROJ_TM, DA_WIDTH), lambda i: (i, 0)), t_spec, t_spec,
                   pl.BlockSpec((PROJ_TM, RET_WIDTH), lambda i: (i, 0))),
        scratch_shapes=[
            pltpu.VMEM((RET_HEADS, RET_QK_DIM, RET_V_DIM), F32),
            pltpu.VMEM((D_MODEL, MAIN_COLS), BF16),
            pltpu.VMEM((T_ROWS, D_MODEL), BF16),
            pltpu.VMEM((T_ROWS, D_MODEL), BF16),
        ],
        compiler_params=_params(("arbitrary",)),
        name="inproj_retention",
    )(cd, x2, g, w_in, dec, qdec, kdec)


ACC_ROWS = DA_V_DIM + 16


N_QT = SEQ // ATT_T
N_OFF = N_QT * (N_QT - 1) // 2


def _pipeline3(n_pos, scores, accumulate):
    scores(0, 0)
    scores(1, 1)
    steady = n_pos - 2

    def triple(k, carry):
        t = 3 * k
        accumulate(t, 0)
        scores(t + 2, 2)
        accumulate(t + 1, 1)
        scores(t + 3, 0)
        accumulate(t + 2, 2)
        scores(t + 4, 1)
        return carry

    lax.fori_loop(0, steady // 3, triple, 0)
    t0 = steady // 3 * 3
    rem = steady - t0
    accumulate(t0, 0)
    if rem >= 1:
        scores(t0 + 2, 2)
    accumulate(t0 + 1, 1)
    if rem == 2:
        scores(t0 + 3, 0)
    if rem >= 1:
        accumulate(t0 + 2, 2)
    if rem == 2:
        accumulate(t0 + 3, 0)


def _attn_kernel(slope_ref, jt_ref, it_ref, qt_ref, k_ref, vt_ref, qa_ref, ka1_ref, ka2_ref,
                 dtab_ref, lq1_ref, lk1_ref, lq2_ref, lk2_ref, g_ref, o_ref,
                 m_sc, acc_sc, s0_sc, s1_sc, s2_sc, mx0_sc, mx1_sc, mx2_sc):
    T = ATT_T
    h = pl.program_id(0)
    slope = slope_ref[h]
    qa = qa_ref[0]
    lane = lax.broadcasted_iota(I32, (T, LANES), 1)
    sums_row = (lax.broadcasted_iota(I32, (16, T), 0) == 0).astype(BF16)
    s_bufs = (s0_sc, s1_sc, s2_sc)
    mx_bufs = (mx0_sc, mx1_sc, mx2_sc)

    HALF = T // 2

    def operands(j, i):
        kt = k_ref[pl.ds(pl.multiple_of(j * T, T), T), :]
        ks = (jnp.where(lane < DA_HEAD_DIM, kt, ka1_ref[0]),
              jnp.where(lane >= DA_HEAD_DIM, kt, ka2_ref[0]))
        qt = qt_ref[0, i]
        qw = (jnp.concatenate([qt[0:DA_HEAD_DIM], qa], axis=0),
              jnp.concatenate([qa, qt[DA_HEAD_DIM:]], axis=0))
        return ks, qw

    def scores(j, i, buf):
        ks, qw = operands(j, i)
        for mp in range(2):
            s = jnp.dot(ks[mp], qw[mp], preferred_element_type=F32)
            s_bufs[buf][mp] = s
            mx_bufs[buf][mp] = jnp.max(s, axis=0, keepdims=True)

    def scores_diag(i, buf):
        ks, qw = operands(i, i)
        for mp in range(2):
            left = (jnp.dot(ks[mp][0:HALF], qw[mp][:, 0:HALF], preferred_element_type=F32)
                    + dtab_ref[0, 0:HALF, 0:HALF])
            right = (jnp.dot(ks[mp], qw[mp][:, HALF:], preferred_element_type=F32)
                     + dtab_ref[0, :, HALF:])
            s_bufs[buf][mp, 0:HALF, 0:HALF] = left
            s_bufs[buf][mp, :, HALF:] = right
            mx_bufs[buf][mp] = jnp.concatenate(
                [jnp.max(left, axis=0, keepdims=True), jnp.max(right, axis=0, keepdims=True)],
                axis=1)

    def update(i, mp, m_prev, m_new, pv):
        acc_sc[i, mp] = jnp.exp(m_prev - m_new) * acc_sc[i, mp] + pv
        m_sc[i, mp] = m_new

    def accumulate(j, i, buf):
        c = slope * lax.convert_element_type((i - j) * T, F32)
        vte = jnp.concatenate([vt_ref[0, j], sums_row], axis=0)
        for mp in range(2):
            m_prev = m_sc[i, mp]
            m_new = jnp.maximum(m_prev, mx_bufs[buf][mp] - c)
            p = jnp.exp(s_bufs[buf][mp] - (m_new + c)).astype(BF16)
            update(i, mp, m_prev, m_new, jnp.dot(vte, p, preferred_element_type=F32))

    def accumulate_diag(i, buf):
        vte = jnp.concatenate([vt_ref[0, i], sums_row], axis=0)
        for mp in range(2):
            m_prev = m_sc[i, mp]
            m_new = jnp.maximum(m_prev, mx_bufs[buf][mp])
            p_left = jnp.exp(s_bufs[buf][mp, 0:HALF, 0:HALF] - m_new[:, 0:HALF]).astype(BF16)
            p_right = jnp.exp(s_bufs[buf][mp, :, HALF:] - m_new[:, HALF:]).astype(BF16)
            pv = jnp.concatenate(
                [jnp.dot(vte[:, 0:HALF], p_left, preferred_element_type=F32),
                 jnp.dot(vte, p_right, preferred_element_type=F32)], axis=1)
            update(i, mp, m_prev, m_new, pv)

    m_sc[...] = jnp.full_like(m_sc, -jnp.inf)
    acc_sc[...] = jnp.zeros_like(acc_sc)
    _pipeline3(N_QT, scores_diag, accumulate_diag)
    _pipeline3(N_OFF,
               lambda pos, buf: scores(jt_ref[pos], it_ref[pos], buf),
               lambda pos, buf: accumulate(jt_ref[pos], it_ref[pos], buf))

    lam = (jnp.exp(jnp.sum(lq1_ref[...] * lk1_ref[...], axis=1, keepdims=True))
           - jnp.exp(jnp.sum(lq2_ref[...] * lk2_ref[...], axis=1, keepdims=True))
           + LAMBDA_INIT)

    def finish(i):
        a1 = acc_sc[i, 0]
        a2 = acc_sc[i, 1]
        ot = (a1[0:DA_V_DIM] / a1[DA_V_DIM:DA_V_DIM + 1]
              - lam * (a2[0:DA_V_DIM] / a2[DA_V_DIM:DA_V_DIM + 1]))
        o = ot.T
        var = jnp.mean(o * o, axis=-1, keepdims=True)
        o = (o * lax.rsqrt(var + EPS) * g_ref[...]) * (1.0 - LAMBDA_INIT)
        o_ref[pl.ds(pl.multiple_of(i * T, T), T), :] = o.astype(BF16)

    def finish_pair(k, carry):
        finish(2 * k)
        finish(2 * k + 1)
        return carry

    lax.fori_loop(0, N_QT // 2, finish_pair, 0)


def _attention(proj, qt4, vt4, slopes, qa, ka1, ka2, dtab, lq1, lk1, lq2, lk2, subln_g):
    T = ATT_T
    vec64 = pl.BlockSpec((1, DA_HEAD_DIM), lambda h: (0, 0))
    per_head = lambda a, b: pl.BlockSpec((1, a, b), lambda h: (h, 0, 0))
    slab = pl.BlockSpec
    single = lambda shape, imap: pl.BlockSpec(shape, imap, pipeline_mode=pl.Buffered(1))
    smem = pl.BlockSpec(memory_space=pltpu.SMEM)
    it_tab, jt_tab = np.tril_indices(N_QT, -1)
    return pl.pallas_call(
        _attn_kernel,
        out_shape=jax.ShapeDtypeStruct((SEQ, DA_WIDTH), BF16),
        grid=(DA_HEADS,),
        in_specs=[
            smem, smem, smem,
            slab((1, N_QT, LANES, T), lambda h: (h, 0, 0, 0)),
            slab((SEQ, LANES), lambda h: (0, DK_OFF // LANES + h)),
            slab((1, N_QT, LANES, T), lambda h: (h, 0, 0, 0)),
            per_head(DA_HEAD_DIM, T), per_head(T, LANES), per_head(T, LANES),
            single((1, T, T), lambda h: (h, 0, 0)),
            vec64, vec64, vec64, vec64,
            pl.BlockSpec((1, DA_V_DIM), lambda h: (0, 0)),
        ],
        out_specs=single((SEQ, LANES), lambda h: (0, h)),
        scratch_shapes=[
            pltpu.VMEM((N_QT, 2, 1, T), F32),
            pltpu.VMEM((N_QT, 2, ACC_ROWS, T), F32),
            pltpu.VMEM((2, T, T), F32),
            pltpu.VMEM((2, T, T), F32),
            pltpu.VMEM((2, T, T), F32),
            pltpu.VMEM((2, 1, T), F32),
            pltpu.VMEM((2, 1, T), F32),
            pltpu.VMEM((2, 1, T), F32),
        ],
        compiler_params=_params(("arbitrary",)),
        name="diff_attention",
    )(slopes, jnp.asarray(jt_tab, I32), jnp.asarray(it_tab, I32), qt4, proj, vt4, qa, ka1, ka2,
      dtab, lq1, lk1, lq2, lk2, subln_g)


X_RING = 3


def _outproj_router_kernel(x_hbm, oda_ref, or_ref, wo32_ref, g_ref, wr_ref, br_ref,
                           h1_ref, xn_ref, ri_ref, rw_ref, cnt_ref, wo_ref, xbuf, xsem):
    i = pl.program_id(0)
    n = pl.num_programs(0)

    def fetch(step):
        slot = step % X_RING
        rows = pl.ds(pl.multiple_of(step * ROUTER_TM, ROUTER_TM), ROUTER_TM)
        return pltpu.make_async_copy(x_hbm.at[rows], xbuf.at[slot], xsem.at[slot])

    @pl.when(i == 0)
    def _():
        wo_ref[...] = wo32_ref[...].astype(BF16)
        for s in range(X_RING - 1):
            fetch(s).start()

    @pl.when(i + X_RING - 1 < n)
    def _():
        fetch(i + X_RING - 1).start()

    fetch(i).wait()
    h1 = (xbuf[i % X_RING]
          + jnp.dot(oda_ref[...], wo_ref[0:DA_WIDTH, :], preferred_element_type=F32)
          + jnp.dot(or_ref[...], wo_ref[DA_WIDTH:, :], preferred_element_type=F32))
    h1_ref[...] = h1
    var = jnp.mean(h1 * h1, axis=-1, keepdims=True)
    xn = h1 * lax.rsqrt(var + EPS) * g_ref[...]
    for c, words in enumerate(_pack_rows(xn)):
        xn_ref[pl.ds(c, ROUTER_TM, stride=X_ROWS), :] = words
    x_hi = xn.astype(BF16)
    x_lo = (xn - x_hi.astype(F32)).astype(BF16)
    both = jnp.dot(x_hi, wr_ref[...], preferred_element_type=F32)
    logits = (both[:, :LANES] + both[:, LANES:]
              + jnp.dot(x_lo, wr_ref[:, :LANES], preferred_element_type=F32)) + br_ref[...]
    lane = lax.broadcasted_iota(I32, logits.shape, 1)
    neg = jnp.float32(-jnp.inf)
    big = jnp.int32(1 << 20)
    gl = jnp.where(lane < MOE_GROUPS, logits, neg)
    gmax = jnp.max(gl, axis=1, keepdims=True)
    gidx = jnp.min(jnp.where(gl == gmax, lane, big), axis=1, keepdims=True)
    gsum = jnp.sum(jnp.exp(gl - gmax), axis=1, keepdims=True)
    gp = 1.0 / gsum
    lo = MOE_GROUPS + gidx * MOE_EXPERTS_PER_GROUP
    el = jnp.where((lane >= lo) & (lane < lo + MOE_EXPERTS_PER_GROUP), logits, neg)
    v1 = jnp.max(el, axis=1, keepdims=True)
    i1 = jnp.min(jnp.where(el == v1, lane, big), axis=1, keepdims=True)
    el2 = jnp.where(lane == i1, neg, el)
    v2 = jnp.max(el2, axis=1, keepdims=True)
    i2 = jnp.min(jnp.where(el2 == v2, lane, big), axis=1, keepdims=True)
    t = jnp.exp(v2 - v1)
    w1 = gp / (1.0 + t)
    w2 = gp * t / (1.0 + t)
    ri_ref[...] = jnp.where(lane == 0, i1 - MOE_GROUPS,
                            jnp.where(lane == 1, i2 - MOE_GROUPS, 0))
    rw_ref[...] = jnp.where(lane == 0, w1, jnp.where(lane == 1, w2, 0.0))

    @pl.when(pl.program_id(0) == 0)
    def _():
        cnt_ref[...] = jnp.zeros_like(cnt_ref)

    chosen = (lane == i1 - MOE_GROUPS) | (lane == i2 - MOE_GROUPS)
    cnt_ref[...] += jnp.sum(chosen.astype(F32), axis=0, keepdims=True)


def _outproj_router(x2, o_da, o_r, w_out, g, wr, br):
    tm = ROUTER_TM
    row = lambda w: pl.BlockSpec((tm, w), lambda i: (i, 0))
    full = lambda a, b: pl.BlockSpec((a, b), lambda i: (0, 0))
    return pl.pallas_call(
        _outproj_router_kernel,
        out_shape=(
            jax.ShapeDtypeStruct((SEQ, D_MODEL), F32),
            jax.ShapeDtypeStruct((SEQ * X_ROWS, LANES), U32),
            jax.ShapeDtypeStruct((SEQ, LANES), I32),
            jax.ShapeDtypeStruct((SEQ, LANES), F32),
            jax.ShapeDtypeStruct((8, LANES), F32),
        ),
        grid=(SEQ // tm,),
        in_specs=[pl.BlockSpec(memory_space=pl.ANY), row(DA_WIDTH), row(RET_WIDTH),
                  pl.BlockSpec((D_MODEL, D_MODEL), lambda i: (0, 0),
                               pipeline_mode=pl.Buffered(1)),
                  full(1, D_MODEL), full(D_MODEL, 2 * LANES), full(1, LANES)],
        out_specs=(row(D_MODEL), pl.BlockSpec((tm * X_ROWS, LANES), lambda i: (i, 0)),
                   row(LANES), row(LANES), full(8, LANES)),
        scratch_shapes=[pltpu.VMEM((D_MODEL, D_MODEL), BF16),
                        pltpu.VMEM((X_RING, tm, D_MODEL), F32),
                        pltpu.SemaphoreType.DMA((X_RING,))],
        compiler_params=_params(("arbitrary",)),
        name="outproj_router",
    )(x2, o_da, o_r, w_out, g, wr, br)


def _plan_kernel(ri_ref, cnt_ref, dest_ref, used_ref):
    TT = PLAN_T
    lane = lax.broadcasted_iota(I32, (TT, LANES), 1)

    def onehots(t):
        r = ri_ref[pl.ds(pl.multiple_of(t * TT, TT), TT), :]
        return lane == r[:, 0:1], lane == r[:, 1:2]

    counts8 = cnt_ref[...].astype(I32)
    shift = FFN_B.bit_length() - 1
    padded = ((counts8 + (FFN_B - 1)) >> shift) << shift
    lane8 = lax.broadcasted_iota(I32, (8, LANES), 1)
    pad_end = padded
    sh = 1
    while sh < LANES:
        pad_end = pad_end + jnp.where(lane8 >= sh, pltpu.roll(pad_end, sh, axis=1), 0)
        sh *= 2
    pad_start = pad_end - padded

    ltri = (lax.broadcasted_iota(I32, (TT, TT), 0)
            > lax.broadcasted_iota(I32, (TT, TT), 1)).astype(BF16)

    def dest_body(t, carry):
        oh1, oh2 = onehots(t)
        a = (oh1 | oh2).astype(F32)
        base = jnp.dot(ltri, a.astype(BF16), preferred_element_type=F32) + carry
        d1 = jnp.sum(jnp.where(oh1, base, 0.0), axis=1, keepdims=True)
        d2 = jnp.sum(jnp.where(oh2, base, 0.0), axis=1, keepdims=True)
        both = jnp.where(lane == 0, d1, jnp.where(lane == 1, d2, 0.0))
        dest_ref[t] = both.T[0:8, :].astype(I32)
        return carry + jnp.sum(a, axis=0, keepdims=True)

    lax.fori_loop(0, SEQ // TT, dest_body, pad_start[0:1].astype(F32))

    total = jnp.max(pad_end, axis=1, keepdims=True)
    row8 = lax.broadcasted_iota(I32, (8, LANES), 0)
    used_ref[...] = jnp.where(row8 == 0, jnp.broadcast_to(total >> shift, (8, LANES)),
                              jnp.where(row8 == 1, counts8, pad_start))


def _plan(ri, cnt):
    return pl.pallas_call(
        _plan_kernel,
        out_shape=(
            jax.ShapeDtypeStruct((SEQ // PLAN_T, 8, PLAN_T), I32),
            jax.ShapeDtypeStruct((8, LANES), I32),
        ),
        compiler_params=pltpu.CompilerParams(vmem_limit_bytes=VMEM_LIMIT),
        name="route_plan",
    )(ri, cnt)


PAD_BITS = FFN_B.bit_length() - 1


def _pad_fill_copies(e, cnt_ref, pst_ref, zero_sc, xs_hbm, zsem):
    cnt = cnt_ref[e]
    pad = (-cnt) & (FFN_B - 1)
    row = pst_ref[e] + cnt
    out = []
    for bit in reversed(range(PAD_BITS)):
        n = 1 << bit
        start = row + ((pad >> (bit + 1)) << (bit + 1))
        copy = pltpu.make_async_copy(
            zero_sc.at[pl.ds(0, n * X_ROWS)],
            xs_hbm.at[pl.ds(pl.multiple_of(start * X_ROWS, X_ROWS), n * X_ROWS)], zsem)
        out.append(((pad & n) != 0, copy))
    return out


def _unused_block_copies(b, zero_sc, xs_hbm, zsem):
    half = FFN_B // 2 * X_ROWS
    return [pltpu.make_async_copy(
        zero_sc, xs_hbm.at[pl.ds(pl.multiple_of((2 * b + k) * half, half), half)], zsem)
        for k in range(2)]


def _dispatch_kernel(dest_ref, cnt_ref, pst_ref, used_ref, xn_ref, xs_hbm, zero_sc, sem, zsem):
    tm = DISP_TM
    i = pl.program_id(0)

    @pl.when(i == 0)
    def _():
        zero_sc[...] = jnp.zeros_like(zero_sc)

        def fill(e, carry):
            for cond, copy in _pad_fill_copies(e, cnt_ref, pst_ref, zero_sc, xs_hbm, zsem):
                pl.when(cond)(copy.start)
            return carry

        lax.fori_loop(0, MOE_EXPERTS, fill, 0)

        def fill_block(b, carry):
            for copy in _unused_block_copies(b, zero_sc, xs_hbm, zsem):
                copy.start()
            return carry

        lax.fori_loop(used_ref[0], N_BLOCKS, fill_block, 0)

    def issue(it, carry):
        for u in range(DMA_UNROLL):
            r = it * DMA_UNROLL + u
            t = i * tm + r
            src = xn_ref.at[pl.ds(pl.multiple_of(r * X_ROWS, X_ROWS), X_ROWS)]
            for kk in range(2):
                d = pl.multiple_of(dest_ref[kk * SEQ + t] * X_ROWS, X_ROWS)
                pltpu.make_async_copy(src, xs_hbm.at[pl.ds(d, X_ROWS)], sem).start(priority=kk)
        return carry

    lax.fori_loop(0, tm // DMA_UNROLL, issue, 0)
    for _ in range(2):
        pltpu.make_async_copy(xn_ref, xs_hbm.at[pl.ds(0, tm * X_ROWS)], sem).wait()

    @pl.when(i == 0)
    def _():
        def drain(e, carry):
            for cond, copy in _pad_fill_copies(e, cnt_ref, pst_ref, zero_sc, xs_hbm, zsem):
                pl.when(cond)(copy.wait)
            return carry

        lax.fori_loop(0, MOE_EXPERTS, drain, 0)

        def drain_block(b, carry):
            for copy in _unused_block_copies(b, zero_sc, xs_hbm, zsem):
                copy.wait()
            return carry

        lax.fori_loop(used_ref[0], N_BLOCKS, drain_block, 0)


def _dispatch(dest_flat, counts, pad_start, used, xn3):
    tm = DISP_TM
    return pl.pallas_call(
        _dispatch_kernel,
        out_shape=jax.ShapeDtypeStruct((N_BUF * X_ROWS, LANES), U32),
        grid_spec=pltpu.PrefetchScalarGridSpec(
            num_scalar_prefetch=4,
            grid=(SEQ // tm,),
            in_specs=[pl.BlockSpec((tm * X_ROWS, LANES), lambda i, d, c, p, u: (i, 0))],
            out_specs=pl.BlockSpec(memory_space=pl.ANY),
            scratch_shapes=[
                pltpu.VMEM((FFN_B // 2 * X_ROWS, LANES), U32),
                pltpu.SemaphoreType.DMA(()),
                pltpu.SemaphoreType.DMA(()),
            ],
        ),
        compiler_params=_params(("arbitrary",)),
        name="moe_dispatch",
    )(dest_flat, counts, pad_start, used, xn3)


BLOCK_COPY_PRIORITY = 1


def _ffn_kernel(cnt_ref, pst_ref, used_ref, xs_hbm, wg_ref, wu_ref, wd_ref, y_hbm,
                xbuf, ybuf, zero_sc, wg_bf, wu_bf, wd_bf, sem_in, sem_out, zsem):
    B = FFN_B
    e = pl.program_id(0)
    n = (cnt_ref[e] + (B - 1)) >> PAD_BITS
    s0 = pst_ref[e] >> PAD_BITS

    def rows(blk):
        size = B * X_ROWS
        return pl.ds(pl.multiple_of(blk * size, size), size)

    def fetch(blk, slot):
        return pltpu.make_async_copy(xs_hbm.at[rows(blk)], xbuf.at[slot], sem_in.at[slot])

    def flush(blk, slot):
        return pltpu.make_async_copy(ybuf.at[slot], y_hbm.at[rows(blk)], sem_out.at[slot])

    used = used_ref[0]

    @pl.when(e == 0)
    def _():
        fetch(0, 0).start(priority=BLOCK_COPY_PRIORITY)

    @pl.when(n > 0)
    def _():
        wg_bf[...] = wg_ref[0].astype(BF16)
        wu_bf[...] = wu_ref[0].astype(BF16)
        wd_bf[...] = wd_ref[0].astype(BF16)

    def body(j, carry):
        blk = s0 + j
        slot = blk % 2

        @pl.when(blk + 1 < used)
        def _():
            fetch(blk + 1, 1 - slot).start(priority=BLOCK_COPY_PRIORITY)

        fetch(blk, slot).wait()

        @pl.when(blk >= 2)
        def _():
            flush(blk - 2, slot).wait()

        def swiglu(rows):
            x = _unpack_rows([xbuf[slot, pl.ds(c, rows, stride=X_ROWS), :]
                              for c in range(X_ROWS)]).astype(BF16)
            hg = jnp.dot(x, wg_bf[...], preferred_element_type=F32)
            hu = jnp.dot(x, wu_bf[...], preferred_element_type=F32)
            hh = ((hg / (1.0 + jnp.exp(-hg))) * hu).astype(BF16)
            y = jnp.dot(hh, wd_bf[...], preferred_element_type=F32)
            for c, words in enumerate(_pack_rows(y)):
                ybuf[slot, pl.ds(c, rows, stride=X_ROWS), :] = words

        valid = cnt_ref[e] - j * B

        @pl.when(valid > B // 2)
        def _():
            swiglu(B)

        @pl.when(valid <= B // 2)
        def _():
            swiglu(B // 2)
            tail = pl.ds(B // 2 * X_ROWS, B // 2 * X_ROWS)
            ybuf[slot, tail, :] = jnp.zeros((B // 2 * X_ROWS, LANES), U32)

        flush(blk, slot).start(priority=BLOCK_COPY_PRIORITY)
        return carry

    lax.fori_loop(0, n, body, 0)

    @pl.when(e == MOE_EXPERTS - 1)
    def _():
        flush(used - 1, (used + 1) % 2).wait()
        flush(used - 2, used % 2).wait()
        zero_sc[...] = jnp.zeros_like(zero_sc)

        def fill(b, carry):
            pltpu.make_async_copy(zero_sc, y_hbm.at[rows(b)], zsem).start()
            return carry

        def drain(b, carry):
            pltpu.make_async_copy(zero_sc, y_hbm.at[rows(b)], zsem).wait()
            return carry

        lax.fori_loop(used_ref[0], N_BLOCKS, fill, 0)
        lax.fori_loop(used_ref[0], N_BLOCKS, drain, 0)


def _ffn(counts, pad_start, used, xs, w_gate, w_up, w_down):
    B = FFN_B
    wspec = lambda a, c: pl.BlockSpec((1, a, c), lambda e, cnt, pst, used: (e, 0, 0))
    return pl.pallas_call(
        _ffn_kernel,
        out_shape=jax.ShapeDtypeStruct((N_BUF * X_ROWS, LANES), U32),
        grid_spec=pltpu.PrefetchScalarGridSpec(
            num_scalar_prefetch=3,
            grid=(MOE_EXPERTS,),
            in_specs=[
                pl.BlockSpec(memory_space=pl.ANY),
                wspec(D_MODEL, MOE_HIDDEN),
                wspec(D_MODEL, MOE_HIDDEN),
                wspec(MOE_HIDDEN, D_MODEL),
            ],
            out_specs=pl.BlockSpec(memory_space=pl.ANY),
            scratch_shapes=[
                pltpu.VMEM((2, B * X_ROWS, LANES), U32),
                pltpu.VMEM((2, B * X_ROWS, LANES), U32),
                pltpu.VMEM((B * X_ROWS, LANES), U32),
                pltpu.VMEM((D_MODEL, MOE_HIDDEN), BF16),
                pltpu.VMEM((D_MODEL, MOE_HIDDEN), BF16),
                pltpu.VMEM((MOE_HIDDEN, D_MODEL), BF16),
                pltpu.SemaphoreType.DMA((2,)),
                pltpu.SemaphoreType.DMA((2,)),
                pltpu.SemaphoreType.DMA(()),
            ],
        ),
        compiler_params=_params(("arbitrary",)),
        name="expert_ffn",
    )(counts, pad_start, used, xs, w_gate, w_up, w_down)


def _combine_kernel(dest_ref, h1_ref, rw_ref, g_ref, y_hbm, o_ref, ybuf, sem):
    tm = COMB_TM
    i = pl.program_id(0)

    def gather(tile, slot):
        def issue(it, carry):
            for u in range(DMA_UNROLL):
                r = it * DMA_UNROLL + u
                t = tile * tm + r
                for kk in range(2):
                    d = pl.multiple_of(dest_ref[kk * SEQ + t] * X_ROWS, X_ROWS)
                    pltpu.make_async_copy(
                        y_hbm.at[pl.ds(d, X_ROWS)],
                        ybuf.at[slot, kk, pl.ds(pl.multiple_of(r * X_ROWS, X_ROWS), X_ROWS)],
                        sem.at[slot, kk]).start(priority=kk)
            return carry

        lax.fori_loop(0, tm // DMA_UNROLL, issue, 0)

    @pl.when(i == 0)
    def _():
        gather(0, 0)

    @pl.when(i + 1 < pl.num_programs(0))
    def _():
        gather(i + 1, (i + 1) % 2)

    slot = i % 2
    for kk in range(2):
        pltpu.make_async_copy(y_hbm.at[pl.ds(0, tm * X_ROWS)], ybuf.at[slot, kk],
                              sem.at[slot, kk]).wait()
    w = rw_ref[...]
    ys = [_unpack_rows([ybuf[slot, kk, pl.ds(c, tm, stride=X_ROWS), :] for c in range(X_ROWS)])
          for kk in range(2)]
    h = h1_ref[...] + w[:, 0:1] * ys[0] + w[:, 1:2] * ys[1]
    var = jnp.mean(h * h, axis=-1, keepdims=True)
    o_ref[...] = h * lax.rsqrt(var + EPS) * g_ref[...]


def _combine(dest_flat, h1, rw, g, y):
    tm = COMB_TM
    return pl.pallas_call(
        _combine_kernel,
        out_shape=jax.ShapeDtypeStruct((SEQ, D_MODEL), F32),
        grid_spec=pltpu.PrefetchScalarGridSpec(
            num_scalar_prefetch=1,
            grid=(SEQ // tm,),
            in_specs=[
                pl.BlockSpec((tm, D_MODEL), lambda i, d: (i, 0)),
                pl.BlockSpec((tm, LANES), lambda i, d: (i, 0)),
                pl.BlockSpec((1, D_MODEL), lambda i, d: (0, 0)),
                pl.BlockSpec(memory_space=pl.ANY),
            ],
            out_specs=pl.BlockSpec((tm, D_MODEL), lambda i, d: (i, 0)),
            scratch_shapes=[
                pltpu.VMEM((2, 2, tm * X_ROWS, LANES), U32),
                pltpu.SemaphoreType.DMA((2, 2)),
            ],
        ),
        compiler_params=_params(("arbitrary",)),
        name="moe_combine",
    )(dest_flat, h1, rw, g, y)


def _attention_tables():
    T = ATT_T
    f32 = np.float32
    slopes = np.exp2(-ALIBI_MAX * np.arange(1, DA_HEADS + 1, dtype=f32) / DA_HEADS).astype(f32)
    r = np.arange(T)
    hi = ((r // CHUNK) * CHUNK).astype(f32)
    lo = (r % CHUNK).astype(f32)
    sl = slopes[:, None]
    one_h = np.ones((DA_HEADS, T), f32)
    q_rows = np.stack([one_h, one_h, -sl * hi[None], -sl * lo[None]], axis=1)
    k_cols = np.stack([sl * hi[None], sl * lo[None], one_h, one_h], axis=-1)
    qa = np.zeros((DA_HEADS, DA_HEAD_DIM, T), f32)
    qa[:, 0:4, :] = q_rows
    ka1 = np.zeros((DA_HEADS, T, LANES), f32)
    ka1[:, :, DA_HEAD_DIM:DA_HEAD_DIM + 4] = k_cols
    ka2 = np.zeros((DA_HEADS, T, LANES), f32)
    ka2[:, :, 0:4] = k_cols
    rel = (r[:, None] - r[None, :]).astype(f32)
    allowed = (r[:, None] // CHUNK) <= (r[None, :] // CHUNK)
    fix = np.where(rel > 0, -2.0 * slopes[:, None, None] * rel[None], 0.0).astype(f32)
    dtab = np.where(allowed[None], fix, -np.inf).astype(f32)
    return (jnp.asarray(slopes), jnp.asarray(qa, BF16), jnp.asarray(ka1, BF16),
            jnp.asarray(ka2, BF16), jnp.asarray(dtab))


def _retention_tables():
    C = RET_C
    f32 = np.float32
    log_gamma = np.log1p(-np.exp2(-5.0 - np.arange(RET_HEADS, dtype=f32))).astype(f32)
    pos = np.arange(C, dtype=f32)
    rel = pos[:, None] - pos[None, :]
    dec = np.where(rel >= 0, np.exp(log_gamma[:, None, None] * np.maximum(rel, 0.0)), 0.0)
    qdec = np.exp(log_gamma[:, None] * (pos + 1.0)[None, :])[:, :, None]
    kdec = np.exp(log_gamma[:, None] * (C - 1 - pos)[None, :])[:, :, None]
    cd = np.exp(log_gamma * C)
    return tuple(jnp.asarray(t, F32) for t in (cd, dec, qdec, kdec))


def kernel(x, attn_norm_g, w_in, da_lambda_q1, da_lambda_k1, da_lambda_q2, da_lambda_k2,
           da_subln_g, w_out, ffn_norm_g, router_group_w, router_group_b, router_expert_w,
           router_expert_b, expert_w_gate, expert_w_up, expert_w_down, final_norm_g):
    B, S, D = x.shape
    assert (B, S, D) == (1, SEQ, D_MODEL)
    x2 = x.reshape(S, D)

    cd, dec, qdec, kdec = _retention_tables()
    slopes, qa, ka1, ka2, dtab = _attention_tables()
    k_da, qt4, vt4, o_r = _inproj(x2, attn_norm_g[0][None, :], w_in[0], cd, dec, qdec, kdec)

    o_da = _attention(k_da, qt4, vt4, slopes, qa, ka1, ka2, dtab, da_lambda_q1, da_lambda_k1,
                      da_lambda_q2, da_lambda_k2, da_subln_g)

    wr = jnp.zeros((D, LANES), F32)
    wr = wr.at[:, :MOE_GROUPS].set(router_group_w[0])
    wr = wr.at[:, MOE_GROUPS:MOE_GROUPS + MOE_EXPERTS].set(router_expert_w[0])
    br = jnp.zeros((1, LANES), F32)
    br = br.at[0, :MOE_GROUPS].set(router_group_b[0])
    br = br.at[0, MOE_GROUPS:MOE_GROUPS + MOE_EXPERTS].set(router_expert_b[0])
    wr_hi = wr.astype(BF16)
    wr_lo = (wr - wr_hi.astype(F32)).astype(BF16)
    h1, xn, ri, rw, cnt = _outproj_router(x2, o_da, o_r, w_out[0],
                                          ffn_norm_g[0][None, :],
                                          jnp.concatenate([wr_hi, wr_lo], axis=1), br)

    dest, meta = _plan(ri, cnt)
    dest_flat = dest[:, 0:2, :].transpose(1, 0, 2).reshape(N_ASSIGN)
    used1 = meta[0, :1]
    counts = meta[1, :MOE_EXPERTS]
    pad_start = meta[2, :MOE_EXPERTS]
    xs = _dispatch(dest_flat, counts, pad_start, used1, xn)

    y = _ffn(counts, pad_start, used1, xs, expert_w_gate[0], expert_w_up[0], expert_w_down[0])
    out = _combine(dest_flat, h1, rw, final_norm_g[None, :], y)
    return out.reshape(B, S, D)
```

```python
import math

import jax
import jax.numpy as jnp
import numpy as np
from jax import lax
from jax.experimental import pallas as pl
from jax.experimental.pallas import tpu as pltpu

F32 = jnp.float32
BF16 = jnp.bfloat16
I32 = jnp.int32
U32 = jnp.uint32

D_MODEL = 1024
SEQ = 16384
CHUNK = 64
EPS = 1e-6

DA_HEADS = 4
DA_HEAD_DIM = 64
DA_V_DIM = 128
DA_WIDTH = 512
ALIBI_MAX = 8.0
RET_HEADS = 4
RET_QK_DIM = 64
RET_V_DIM = 128
RET_WIDTH = 512
W_IN_COLS = 3072
T_ROWS = 512
MAIN_COLS = 2048
DK_OFF = 0
RQ_OFF = 512
RK_OFF = 768
RV_OFF = 1024
RG_OFF = 1536

MOE_GROUPS = 4
MOE_EXPERTS_PER_GROUP = 8
MOE_EXPERTS = 32
MOE_HIDDEN = 512
LAMBDA_INIT = 0.8 - 0.6 * math.exp(-0.3 * 0)

LANES = 128
X_ROWS = 4
VMEM_LIMIT = 56 * 1024 * 1024

PROJ_TM = 512
ROUTER_TM = 1024
ATT_T = 512
RET_C = 256
PLAN_T = 512
FFN_B = 512
N_ASSIGN = 2 * SEQ
N_BLOCKS = N_ASSIGN // FFN_B + MOE_EXPERTS
N_BUF = N_BLOCKS * FFN_B
COMB_TM = 512
DISP_TM = 1024
DMA_UNROLL = 8


def _params(sem):
    return pltpu.CompilerParams(dimension_semantics=sem, vmem_limit_bytes=VMEM_LIMIT)


def _pack_rows(v):
    bits = lax.bitcast_convert_type(v.astype(BF16).astype(F32), U32)
    return [(bits[:, c * LANES:(c + 1) * LANES] >> 16)
            | (bits[:, (c + X_ROWS) * LANES:(c + X_ROWS + 1) * LANES] & jnp.uint32(0xFFFF0000))
            for c in range(X_ROWS)]


def _unpack_rows(words):
    lo = [lax.bitcast_convert_type(w << 16, F32) for w in words]
    hi = [lax.bitcast_convert_type(w & jnp.uint32(0xFFFF0000), F32) for w in words]
    return jnp.concatenate(lo + hi, axis=1)


def _retention_block(q_all, k_all, v_all, g_all, cd_ref, dec_ref, qdec_ref, kdec_ref, st_sc):
    outs = []
    for h in range(RET_HEADS):
        qk = slice(h * RET_QK_DIM, (h + 1) * RET_QK_DIM)
        vv = slice(h * RET_V_DIM, (h + 1) * RET_V_DIM)
        q = q_all[:, qk]
        k = k_all[:, qk]
        v = v_all[:, vv]
        g = g_all[:, vv]
        s = lax.dot_general(q, k, (((1,), (1,)), ((), ())),
                            preferred_element_type=F32) * dec_ref[h]
        intra = jnp.dot(s.astype(BF16), v, preferred_element_type=F32)
        st = st_sc[h]
        cross = jnp.dot(q, st.astype(BF16), preferred_element_type=F32) * qdec_ref[h]
        kd = (k.astype(F32) * kdec_ref[h]).astype(BF16)
        st_sc[h] = st * cd_ref[h] + lax.dot_general(kd, v, (((0,), (0,)), ((), ())),
                                                    preferred_element_type=F32)
        o = intra + cross
        o = o * lax.rsqrt(jnp.mean(o * o, axis=-1, keepdims=True) + EPS)
        outs.append(((g / (1.0 + jnp.exp(-g))) * o).astype(BF16))
    return outs


def _inproj_kernel(cd_ref, x_ref, g_ref, win_ref, dec_ref, qdec_ref, kdec_ref,
                   k_ref, qt_ref, vt_ref, or_ref, st_sc, w_ref, wq_ref, wv_ref):
    @pl.when(pl.program_id(0) == 0)
    def _():
        st_sc[...] = jnp.zeros_like(st_sc)
        cols = DA_WIDTH
        w_ref[:, DK_OFF:RQ_OFF] = win_ref[:, cols:2 * cols].astype(BF16)
        w_ref[:, RQ_OFF:RK_OFF] = win_ref[:, 3 * cols:3 * cols + 256].astype(BF16)
        w_ref[:, RK_OFF:RV_OFF] = (win_ref[:, 3 * cols + 256:4 * cols]
                                   * (RET_QK_DIM ** -0.5)).astype(BF16)
        w_ref[:, RV_OFF:MAIN_COLS] = win_ref[:, 4 * cols:6 * cols].astype(BF16)
        step = 256
        for r in range(D_MODEL // step):
            rows = slice(r * step, (r + 1) * step)
            wq_ref[:, rows] = (win_ref[rows, 0:cols] * (DA_HEAD_DIM ** -0.5)).T.astype(BF16)
            wv_ref[:, rows] = win_ref[rows, 2 * cols:3 * cols].T.astype(BF16)

    x = x_ref[...]
    var = jnp.mean(x * x, axis=-1, keepdims=True)
    xn = (x * lax.rsqrt(var + EPS) * g_ref[...]).astype(BF16)

    def proj(lo, hi):
        return jnp.dot(xn, w_ref[:, lo:hi], preferred_element_type=F32)

    k_ref[...] = proj(DK_OFF, DK_OFF + DA_WIDTH).astype(BF16)
    nt = (((1,), (1,)), ((), ()))
    qt = lax.dot_general(wq_ref[...], xn, nt, preferred_element_type=F32)
    qt_ref[...] = qt.astype(BF16).reshape(DA_HEADS, 1, 2 * DA_HEAD_DIM, PROJ_TM)
    vt = lax.dot_general(wv_ref[...], xn, nt, preferred_element_type=F32)
    vt_ref[...] = vt.astype(BF16).reshape(DA_HEADS, 1, DA_V_DIM, PROJ_TM)

    rq = proj(RQ_OFF, RK_OFF).astype(BF16)
    rk = proj(RK_OFF, RV_OFF).astype(BF16)
    rv = proj(RV_OFF, RG_OFF).astype(BF16)
    rg = proj(RG_OFF, MAIN_COLS)
    for blk in range(PROJ_TM // RET_C):
        rows = slice(blk * RET_C, (blk + 1) * RET_C)
        outs = _retention_block(rq[rows], rk[rows], rv[rows], rg[rows],
                                cd_ref, dec_ref, qdec_ref, kdec_ref, st_sc)
        for h in range(RET_HEADS):
            or_ref[rows, h * RET_V_DIM:(h + 1) * RET_V_DIM] = outs[h]


def _inproj(x2, g, w_in, cd, dec, qdec, kdec):
    C = RET_C
    t_shape = jax.ShapeDtypeStruct((DA_HEADS, SEQ // PROJ_TM, LANES, PROJ_TM), BF16)
    t_spec = pl.BlockSpec((DA_HEADS, 1, LANES, PROJ_TM), lambda i: (0, i, 0, 0))
    return pl.pallas_call(
        _inproj_kernel,
        out_shape=(jax.ShapeDtypeStruct((SEQ, DA_WIDTH), BF16), t_shape, t_shape,
                   jax.ShapeDtypeStruct((SEQ, RET_WIDTH), BF16)),
        grid=(SEQ // PROJ_TM,),
        in_specs=[
            pl.BlockSpec(memory_space=pltpu.SMEM),
            pl.BlockSpec((PROJ_TM, D_MODEL), lambda i: (i, 0)),
            pl.BlockSpec((1, D_MODEL), lambda i: (0, 0)),
            pl.BlockSpec((D_MODEL, W_IN_COLS), lambda i: (0, 0), pipeline_mode=pl.Buffered(1)),
            pl.BlockSpec((RET_HEADS, C, C), lambda i: (0, 0, 0)),
            pl.BlockSpec((RET_HEADS, C, 1), lambda i: (0, 0, 0)),
            pl.BlockSpec((RET_HEADS, C, 1), lambda i: (0, 0, 0)),
        ],
        out_specs=(pl.BlockSpec((PROJ_TM, DA_WIDTH), lambda i: (i, 0)), t_spec, t_spec,
                   pl.BlockSpec((PROJ_TM, RET_WIDTH), lambda i: (i, 0))),
        scratch_shapes=[
            pltpu.VMEM((RET_HEADS, RET_QK_DIM, RET_V_DIM), F32),
            pltpu.VMEM((D_MODEL, MAIN_COLS), BF16),
            pltpu.VMEM((T_ROWS, D_MODEL), BF16),
            pltpu.VMEM((T_ROWS, D_MODEL), BF16),
        ],
        compiler_params=_params(("arbitrary",)),
        name="inproj_retention",
    )(cd, x2, g, w_in, dec, qdec, kdec)


ACC_ROWS = DA_V_DIM + 16


N_QT = SEQ // ATT_T
N_OFF = N_QT * (N_QT - 1) // 2


def _pipeline3(n_pos, scores, accumulate, rot=0, prologue=True, overlap_next=()):
    def buf(k):
        return (k + rot) % 3

    if prologue:
        scores(0, buf(0))
        scores(1, buf(1))
    steady = n_pos - 2

    def triple(k, carry):
        t = 3 * k
        accumulate(t, buf(0))
        scores(t + 2, buf(2))
        accumulate(t + 1, buf(1))
        scores(t + 3, buf(0))
        accumulate(t + 2, buf(2))
        scores(t + 4, buf(1))
        return carry

    lax.fori_loop(0, steady // 3, triple, 0)
    t0 = steady // 3 * 3
    for idx, pos in enumerate(range(t0, n_pos)):
        accumulate(pos, buf(idx))
        if pos + 2 < n_pos:
            scores(pos + 2, buf(idx + 2))
        elif pos + 2 - n_pos < len(overlap_next):
            overlap_next[pos + 2 - n_pos]()


def _attn_kernel(slope_ref, jt_ref, it_ref, qt_ref, k_ref, vt_ref, qa_ref, ka1_ref, ka2_ref,
                 dtab_ref, lq1_ref, lk1_ref, lq2_ref, lk2_ref, g_ref, o_ref,
                 m_sc, acc_sc, s0_sc, s1_sc, s2_sc, mx0_sc, mx1_sc, mx2_sc):
    T = ATT_T
    h = pl.program_id(0)
    slope = slope_ref[h]
    qa = qa_ref[0]
    lane = lax.broadcasted_iota(I32, (T, LANES), 1)
    sums_row = (lax.broadcasted_iota(I32, (16, T), 0) == 0).astype(BF16)
    s_bufs = (s0_sc, s1_sc, s2_sc)
    mx_bufs = (mx0_sc, mx1_sc, mx2_sc)

    HALF = T // 2

    def operands(j, i):
        kt = k_ref[pl.ds(pl.multiple_of(j * T, T), T), :]
        ks = (jnp.where(lane < DA_HEAD_DIM, kt, ka1_ref[0]),
              jnp.where(lane >= DA_HEAD_DIM, kt, ka2_ref[0]))
        qt = qt_ref[0, i]
        qw = (jnp.concatenate([qt[0:DA_HEAD_DIM], qa], axis=0),
              jnp.concatenate([qa, qt[DA_HEAD_DIM:]], axis=0))
        return ks, qw

    def scores(j, i, buf):
        ks, qw = operands(j, i)
        for mp in range(2):
            s = jnp.dot(ks[mp], qw[mp], preferred_element_type=F32)
            s_bufs[buf][mp] = s
            mx_bufs[buf][mp] = jnp.max(s, axis=0, keepdims=True)

    def scores_diag(i, buf):
        ks, qw = operands(i, i)
        for mp in range(2):
            left = (jnp.dot(ks[mp][0:HALF], qw[mp][:, 0:HALF], preferred_element_type=F32)
                    + dtab_ref[0, 0:HALF, 0:HALF])
            right = (jnp.dot(ks[mp], qw[mp][:, HALF:], preferred_element_type=F32)
                     + dtab_ref[0, :, HALF:])
            s_bufs[buf][mp, 0:HALF, 0:HALF] = left
            s_bufs[buf][mp, :, HALF:] = right
            mx_bufs[buf][mp] = jnp.concatenate(
                [jnp.max(left, axis=0, keepdims=True), jnp.max(right, axis=0, keepdims=True)],
                axis=1)

    def update(i, mp, m_prev, m_new, pv):
        acc_sc[i, mp] = jnp.exp(m_prev - m_new) * acc_sc[i, mp] + pv
        m_sc[i, mp] = m_new

    def accumulate(j, i, buf):
        c = slope * lax.convert_element_type((i - j) * T, F32)
        vte = jnp.concatenate([vt_ref[0, j], sums_row], axis=0)
        for mp in range(2):
            m_prev = m_sc[i, mp]
            m_new = jnp.maximum(m_prev, mx_bufs[buf][mp] - c)
            p = jnp.exp(s_bufs[buf][mp] - (m_new + c)).astype(BF16)
            update(i, mp, m_prev, m_new, jnp.dot(vte, p, preferred_element_type=F32))

    def accumulate_diag(i, buf):
        vte = jnp.concatenate([vt_ref[0, i], sums_row], axis=0)
        for mp in range(2):
            m_prev = m_sc[i, mp]
            m_new = jnp.maximum(m_prev, mx_bufs[buf][mp])
            p_left = jnp.exp(s_bufs[buf][mp, 0:HALF, 0:HALF] - m_new[:, 0:HALF]).astype(BF16)
            p_right = jnp.exp(s_bufs[buf][mp, :, HALF:] - m_new[:, HALF:]).astype(BF16)
            pv = jnp.concatenate(
                [jnp.dot(vte[:, 0:HALF], p_left, preferred_element_type=F32),
                 jnp.dot(vte, p_right, preferred_element_type=F32)], axis=1)
            update(i, mp, m_prev, m_new, pv)

    m_sc[...] = jnp.full_like(m_sc, -jnp.inf)
    acc_sc[...] = jnp.zeros_like(acc_sc)
    def off_scores(pos, buf):
        scores(jt_ref[pos], it_ref[pos], buf)

    def off_accumulate(pos, buf):
        accumulate(jt_ref[pos], it_ref[pos], buf)

    off_rot = (N_QT - 2) % 3 + 2
    _pipeline3(N_QT, scores_diag, accumulate_diag,
               overlap_next=(lambda: off_scores(0, off_rot % 3),
                             lambda: off_scores(1, (off_rot + 1) % 3)))
    _pipeline3(N_OFF, off_scores, off_accumulate, rot=off_rot, prologue=False)

    lam = (jnp.exp(jnp.sum(lq1_ref[...] * lk1_ref[...], axis=1, keepdims=True))
           - jnp.exp(jnp.sum(lq2_ref[...] * lk2_ref[...], axis=1, keepdims=True))
           + LAMBDA_INIT)

    def finish(i):
        a1 = acc_sc[i, 0]
        a2 = acc_sc[i, 1]
        ot = (a1[0:DA_V_DIM] / a1[DA_V_DIM:DA_V_DIM + 1]
              - lam * (a2[0:DA_V_DIM] / a2[DA_V_DIM:DA_V_DIM + 1]))
        o = ot.T
        var = jnp.mean(o * o, axis=-1, keepdims=True)
        o = (o * lax.rsqrt(var + EPS) * g_ref[...]) * (1.0 - LAMBDA_INIT)
        o_ref[pl.ds(pl.multiple_of(i * T, T), T), :] = o.astype(BF16)

    def finish_pair(k, carry):
        finish(2 * k)
        finish(2 * k + 1)
        return carry

    lax.fori_loop(0, N_QT // 2, finish_pair, 0)


def _attention(proj, qt4, vt4, slopes, qa, ka1, ka2, dtab, lq1, lk1, lq2, lk2, subln_g):
    T = ATT_T
    vec64 = pl.BlockSpec((1, DA_HEAD_DIM), lambda h: (0, 0))
    per_head = lambda a, b: pl.BlockSpec((1, a, b), lambda h: (h, 0, 0))
    slab = pl.BlockSpec
    single = lambda shape, imap: pl.BlockSpec(shape, imap, pipeline_mode=pl.Buffered(1))
    smem = pl.BlockSpec(memory_space=pltpu.SMEM)
    it_tab, jt_tab = np.tril_indices(N_QT, -1)
    return pl.pallas_call(
        _attn_kernel,
        out_shape=jax.ShapeDtypeStruct((SEQ, DA_WIDTH), BF16),
        grid=(DA_HEADS,),
        in_specs=[
            smem, smem, smem,
            slab((1, N_QT, LANES, T), lambda h: (h, 0, 0, 0)),
            slab((SEQ, LANES), lambda h: (0, DK_OFF // LANES + h)),
            slab((1, N_QT, LANES, T), lambda h: (h, 0, 0, 0)),
            per_head(DA_HEAD_DIM, T), per_head(T, LANES), per_head(T, LANES),
            single((1, T, T), lambda h: (h, 0, 0)),
            vec64, vec64, vec64, vec64,
            pl.BlockSpec((1, DA_V_DIM), lambda h: (0, 0)),
        ],
        out_specs=single((SEQ, LANES), lambda h: (0, h)),
        scratch_shapes=[
            pltpu.VMEM((N_QT, 2, 1, T), F32),
            pltpu.VMEM((N_QT, 2, ACC_ROWS, T), F32),
            pltpu.VMEM((2, T, T), F32),
            pltpu.VMEM((2, T, T), F32),
            pltpu.VMEM((2, T, T), F32),
            pltpu.VMEM((2, 1, T), F32),
            pltpu.VMEM((2, 1, T), F32),
            pltpu.VMEM((2, 1, T), F32),
        ],
        compiler_params=_params(("arbitrary",)),
        name="diff_attention",
    )(slopes, jnp.asarray(jt_tab, I32), jnp.asarray(it_tab, I32), qt4, proj, vt4, qa, ka1, ka2,
      dtab, lq1, lk1, lq2, lk2, subln_g)


def _outproj_router_kernel(x_ref, oda_ref, or_ref, wo32_ref, g_ref, wr_ref, br_ref,
                           h1_ref, xn_ref, ri_ref, rw_ref, cnt_ref, wo_ref):
    @pl.when(pl.program_id(0) == 0)
    def _():
        wo_ref[...] = wo32_ref[...].astype(BF16)

    h1 = (x_ref[...]
          + jnp.dot(oda_ref[...], wo_ref[0:DA_WIDTH, :], preferred_element_type=F32)
          + jnp.dot(or_ref[...], wo_ref[DA_WIDTH:, :], preferred_element_type=F32))
    h1_ref[...] = h1
    var = jnp.mean(h1 * h1, axis=-1, keepdims=True)
    xn = h1 * lax.rsqrt(var + EPS) * g_ref[...]
    for c, words in enumerate(_pack_rows(xn)):
        xn_ref[pl.ds(c, ROUTER_TM, stride=X_ROWS), :] = words
    x_hi = xn.astype(BF16)
    x_lo = (xn - x_hi.astype(F32)).astype(BF16)
    both = jnp.dot(x_hi, wr_ref[...], preferred_element_type=F32)
    logits = (both[:, :LANES] + both[:, LANES:]
              + jnp.dot(x_lo, wr_ref[:, :LANES], preferred_element_type=F32)) + br_ref[...]
    lane = lax.broadcasted_iota(I32, logits.shape, 1)
    neg = jnp.float32(-jnp.inf)
    big = jnp.int32(1 << 20)
    gl = jnp.where(lane < MOE_GROUPS, logits, neg)
    gmax = jnp.max(gl, axis=1, keepdims=True)
    gidx = jnp.min(jnp.where(gl == gmax, lane, big), axis=1, keepdims=True)
    gsum = jnp.sum(jnp.exp(gl - gmax), axis=1, keepdims=True)
    gp = 1.0 / gsum
    lo = MOE_GROUPS + gidx * MOE_EXPERTS_PER_GROUP
    el = jnp.where((lane >= lo) & (lane < lo + MOE_EXPERTS_PER_GROUP), logits, neg)
    v1 = jnp.max(el, axis=1, keepdims=True)
    i1 = jnp.min(jnp.where(el == v1, lane, big), axis=1, keepdims=True)
    el2 = jnp.where(lane == i1, neg, el)
    v2 = jnp.max(el2, axis=1, keepdims=True)
    i2 = jnp.min(jnp.where(el2 == v2, lane, big), axis=1, keepdims=True)
    t = jnp.exp(v2 - v1)
    w1 = gp / (1.0 + t)
    w2 = gp * t / (1.0 + t)
    ri_ref[...] = jnp.where(lane == 0, i1 - MOE_GROUPS,
                            jnp.where(lane == 1, i2 - MOE_GROUPS, 0))
    rw_ref[...] = jnp.where(lane == 0, w1, jnp.where(lane == 1, w2, 0.0))

    @pl.when(pl.program_id(0) == 0)
    def _():
        cnt_ref[...] = jnp.zeros_like(cnt_ref)

    chosen = (lane == i1 - MOE_GROUPS) | (lane == i2 - MOE_GROUPS)
    cnt_ref[...] += jnp.sum(chosen.astype(F32), axis=0, keepdims=True)


def _outproj_router(x2, o_da, o_r, w_out, g, wr, br):
    tm = ROUTER_TM
    row = lambda w: pl.BlockSpec((tm, w), lambda i: (i, 0))
    full = lambda a, b: pl.BlockSpec((a, b), lambda i: (0, 0))
    return pl.pallas_call(
        _outproj_router_kernel,
        out_shape=(
            jax.ShapeDtypeStruct((SEQ, D_MODEL), F32),
            jax.ShapeDtypeStruct((SEQ * X_ROWS, LANES), U32),
            jax.ShapeDtypeStruct((SEQ, LANES), I32),
            jax.ShapeDtypeStruct((SEQ, LANES), F32),
            jax.ShapeDtypeStruct((8, LANES), F32),
        ),
        grid=(SEQ // tm,),
        in_specs=[row(D_MODEL), row(DA_WIDTH), row(RET_WIDTH),
                  pl.BlockSpec((D_MODEL, D_MODEL), lambda i: (0, 0),
                               pipeline_mode=pl.Buffered(1)),
                  full(1, D_MODEL), full(D_MODEL, 2 * LANES), full(1, LANES)],
        out_specs=(row(D_MODEL), pl.BlockSpec((tm * X_ROWS, LANES), lambda i: (i, 0)),
                   row(LANES), row(LANES), full(8, LANES)),
        scratch_shapes=[pltpu.VMEM((D_MODEL, D_MODEL), BF16)],
        compiler_params=_params(("arbitrary",)),
        name="outproj_router",
    )(x2, o_da, o_r, w_out, g, wr, br)


def _plan_kernel(ri_ref, cnt_ref, dest_ref, used_ref):
    TT = PLAN_T
    lane = lax.broadcasted_iota(I32, (TT, LANES), 1)

    def onehots(t):
        r = ri_ref[pl.ds(pl.multiple_of(t * TT, TT), TT), :]
        return lane == r[:, 0:1], lane == r[:, 1:2]

    counts8 = cnt_ref[...].astype(I32)
    shift = FFN_B.bit_length() - 1
    padded = ((counts8 + (FFN_B - 1)) >> shift) << shift
    lane8 = lax.broadcasted_iota(I32, (8, LANES), 1)
    pad_end = padded
    sh = 1
    while sh < LANES:
        pad_end = pad_end + jnp.where(lane8 >= sh, pltpu.roll(pad_end, sh, axis=1), 0)
        sh *= 2
    pad_start = pad_end - padded

    ltri = (lax.broadcasted_iota(I32, (TT, TT), 0)
            > lax.broadcasted_iota(I32, (TT, TT), 1)).astype(BF16)

    def dest_body(t, carry):
        oh1, oh2 = onehots(t)
        a = (oh1 | oh2).astype(F32)
        base = jnp.dot(ltri, a.astype(BF16), preferred_element_type=F32) + carry
        d1 = jnp.sum(jnp.where(oh1, base, 0.0), axis=1, keepdims=True)
        d2 = jnp.sum(jnp.where(oh2, base, 0.0), axis=1, keepdims=True)
        both = jnp.where(lane == 0, d1, jnp.where(lane == 1, d2, 0.0))
        dest_ref[t] = both.T[0:8, :].astype(I32)
        return carry + jnp.sum(a, axis=0, keepdims=True)

    lax.fori_loop(0, SEQ // TT, dest_body, pad_start[0:1].astype(F32))

    total = jnp.max(pad_end, axis=1, keepdims=True)
    row8 = lax.broadcasted_iota(I32, (8, LANES), 0)
    used_ref[...] = jnp.where(row8 == 0, jnp.broadcast_to(total >> shift, (8, LANES)),
                              jnp.where(row8 == 1, counts8, pad_start))


def _plan(ri, cnt):
    return pl.pallas_call(
        _plan_kernel,
        out_shape=(
            jax.ShapeDtypeStruct((SEQ // PLAN_T, 8, PLAN_T), I32),
            jax.ShapeDtypeStruct((8, LANES), I32),
        ),
        compiler_params=pltpu.CompilerParams(vmem_limit_bytes=VMEM_LIMIT),
        name="route_plan",
    )(ri, cnt)


PAD_BITS = FFN_B.bit_length() - 1


def _pad_fill_copies(e, cnt_ref, pst_ref, zero_sc, xs_hbm, zsem):
    cnt = cnt_ref[e]
    pad = (-cnt) & (FFN_B - 1)
    row = pst_ref[e] + cnt
    out = []
    for bit in reversed(range(PAD_BITS)):
        n = 1 << bit
        start = row + ((pad >> (bit + 1)) << (bit + 1))
        copy = pltpu.make_async_copy(
            zero_sc.at[pl.ds(0, n * X_ROWS)],
            xs_hbm.at[pl.ds(pl.multiple_of(start * X_ROWS, X_ROWS), n * X_ROWS)], zsem)
        out.append(((pad & n) != 0, copy))
    return out


def _unused_block_copies(b, zero_sc, xs_hbm, zsem):
    half = FFN_B // 2 * X_ROWS
    return [pltpu.make_async_copy(
        zero_sc, xs_hbm.at[pl.ds(pl.multiple_of((2 * b + k) * half, half), half)], zsem)
        for k in range(2)]


def _dispatch_kernel(dest_ref, cnt_ref, pst_ref, used_ref, xn_ref, xs_hbm, zero_sc, sem, zsem):
    tm = DISP_TM
    i = pl.program_id(0)

    @pl.when(i == 0)
    def _():
        zero_sc[...] = jnp.zeros_like(zero_sc)

        def fill(e, carry):
            for cond, copy in _pad_fill_copies(e, cnt_ref, pst_ref, zero_sc, xs_hbm, zsem):
                pl.when(cond)(copy.start)
            return carry

        lax.fori_loop(0, MOE_EXPERTS, fill, 0)

        def fill_block(b, carry):
            for copy in _unused_block_copies(b, zero_sc, xs_hbm, zsem):
                copy.start()
            return carry

        lax.fori_loop(used_ref[0], N_BLOCKS, fill_block, 0)

    def issue(it, carry):
        for u in range(DMA_UNROLL):
            r = it * DMA_UNROLL + u
            t = i * tm + r
            src = xn_ref.at[pl.ds(pl.multiple_of(r * X_ROWS, X_ROWS), X_ROWS)]
            for kk in range(2):
                d = pl.multiple_of(dest_ref[kk * SEQ + t] * X_ROWS, X_ROWS)
                pltpu.make_async_copy(src, xs_hbm.at[pl.ds(d, X_ROWS)], sem).start(priority=kk)
        return carry

    lax.fori_loop(0, tm // DMA_UNROLL, issue, 0)
    for _ in range(2):
        pltpu.make_async_copy(xn_ref, xs_hbm.at[pl.ds(0, tm * X_ROWS)], sem).wait()

    @pl.when(i == 0)
    def _():
        def drain(e, carry):
            for cond, copy in _pad_fill_copies(e, cnt_ref, pst_ref, zero_sc, xs_hbm, zsem):
                pl.when(cond)(copy.wait)
            return carry

        lax.fori_loop(0, MOE_EXPERTS, drain, 0)

        def drain_block(b, carry):
            for copy in _unused_block_copies(b, zero_sc, xs_hbm, zsem):
                copy.wait()
            return carry

        lax.fori_loop(used_ref[0], N_BLOCKS, drain_block, 0)


def _dispatch(dest_flat, counts, pad_start, used, xn3):
    tm = DISP_TM
    return pl.pallas_call(
        _dispatch_kernel,
        out_shape=jax.ShapeDtypeStruct((N_BUF * X_ROWS, LANES), U32),
        grid_spec=pltpu.PrefetchScalarGridSpec(
            num_scalar_prefetch=4,
            grid=(SEQ // tm,),
            in_specs=[pl.BlockSpec((tm * X_ROWS, LANES), lambda i, d, c, p, u: (i, 0))],
            out_specs=pl.BlockSpec(memory_space=pl.ANY),
            scratch_shapes=[
                pltpu.VMEM((FFN_B // 2 * X_ROWS, LANES), U32),
                pltpu.SemaphoreType.DMA(()),
                pltpu.SemaphoreType.DMA(()),
            ],
        ),
        compiler_params=_params(("arbitrary",)),
        name="moe_dispatch",
    )(dest_flat, counts, pad_start, used, xn3)


BLOCK_COPY_PRIORITY = 1


def _ffn_kernel(cnt_ref, pst_ref, used_ref, xs_hbm, wg_ref, wu_ref, wd_ref, y_hbm,
                xbuf, ybuf, zero_sc, wg_bf, wu_bf, wd_bf, sem_in, sem_out, zsem):
    B = FFN_B
    e = pl.program_id(0)
    n = (cnt_ref[e] + (B - 1)) >> PAD_BITS
    s0 = pst_ref[e] >> PAD_BITS

    def rows(blk):
        size = B * X_ROWS
        return pl.ds(pl.multiple_of(blk * size, size), size)

    def fetch(blk, slot):
        return pltpu.make_async_copy(xs_hbm.at[rows(blk)], xbuf.at[slot], sem_in.at[slot])

    def flush(blk, slot):
        return pltpu.make_async_copy(ybuf.at[slot], y_hbm.at[rows(blk)], sem_out.at[slot])

    used = used_ref[0]

    @pl.when(e == 0)
    def _():
        fetch(0, 0).start(priority=BLOCK_COPY_PRIORITY)

    @pl.when(n > 0)
    def _():
        wg_bf[...] = wg_ref[0].astype(BF16)
        wu_bf[...] = wu_ref[0].astype(BF16)
        wd_bf[...] = wd_ref[0].astype(BF16)

    def body(j, carry):
        blk = s0 + j
        slot = blk % 2

        @pl.when(blk + 1 < used)
        def _():
            fetch(blk + 1, 1 - slot).start(priority=BLOCK_COPY_PRIORITY)

        fetch(blk, slot).wait()

        @pl.when(blk >= 2)
        def _():
            flush(blk - 2, slot).wait()

        def swiglu(rows):
            x = _unpack_rows([xbuf[slot, pl.ds(c, rows, stride=X_ROWS), :]
                              for c in range(X_ROWS)]).astype(BF16)
            hg = jnp.dot(x, wg_bf[...], preferred_element_type=F32)
            hu = jnp.dot(x, wu_bf[...], preferred_element_type=F32)
            hh = ((hg / (1.0 + jnp.exp(-hg))) * hu).astype(BF16)
            y = jnp.dot(hh, wd_bf[...], preferred_element_type=F32)
            for c, words in enumerate(_pack_rows(y)):
                ybuf[slot, pl.ds(c, rows, stride=X_ROWS), :] = words

        valid = cnt_ref[e] - j * B

        @pl.when(valid > B // 2)
        def _():
            swiglu(B)

        @pl.when(valid <= B // 2)
        def _():
            swiglu(B // 2)
            tail = pl.ds(B // 2 * X_ROWS, B // 2 * X_ROWS)
            ybuf[slot, tail, :] = jnp.zeros((B // 2 * X_ROWS, LANES), U32)

        flush(blk, slot).start(priority=BLOCK_COPY_PRIORITY)
        return carry

    lax.fori_loop(0, n, body, 0)

    @pl.when(e == MOE_EXPERTS - 1)
    def _():
        flush(used - 1, (used + 1) % 2).wait()
        flush(used - 2, used % 2).wait()
        zero_sc[...] = jnp.zeros_like(zero_sc)

        def fill(b, carry):
            pltpu.make_async_copy(zero_sc, y_hbm.at[rows(b)], zsem).start()
            return carry

        def drain(b, carry):
            pltpu.make_async_copy(zero_sc, y_hbm.at[rows(b)], zsem).wait()
            return carry

        lax.fori_loop(used_ref[0], N_BLOCKS, fill, 0)
        lax.fori_loop(used_ref[0], N_BLOCKS, drain, 0)


def _ffn(counts, pad_start, used, xs, w_gate, w_up, w_down):
    B = FFN_B
    wspec = lambda a, c: pl.BlockSpec((1, a, c), lambda e, cnt, pst, used: (e, 0, 0))
    return pl.pallas_call(
        _ffn_kernel,
        out_shape=jax.ShapeDtypeStruct((N_BUF * X_ROWS, LANES), U32),
        grid_spec=pltpu.PrefetchScalarGridSpec(
            num_scalar_prefetch=3,
            grid=(MOE_EXPERTS,),
            in_specs=[
                pl.BlockSpec(memory_space=pl.ANY),
                wspec(D_MODEL, MOE_HIDDEN),
                wspec(D_MODEL, MOE_HIDDEN),
                wspec(MOE_HIDDEN, D_MODEL),
            ],
            out_specs=pl.BlockSpec(memory_space=pl.ANY),
            scratch_shapes=[
                pltpu.VMEM((2, B * X_ROWS, LANES), U32),
                pltpu.VMEM((2, B * X_ROWS, LANES), U32),
                pltpu.VMEM((B * X_ROWS, LANES), U32),
                pltpu.VMEM((D_MODEL, MOE_HIDDEN), BF16),
                pltpu.VMEM((D_MODEL, MOE_HIDDEN), BF16),
                pltpu.VMEM((MOE_HIDDEN, D_MODEL), BF16),
                pltpu.SemaphoreType.DMA((2,)),
                pltpu.SemaphoreType.DMA((2,)),
                pltpu.SemaphoreType.DMA(()),
            ],
        ),
        compiler_params=_params(("arbitrary",)),
        name="expert_ffn",
    )(counts, pad_start, used, xs, w_gate, w_up, w_down)


def _combine_kernel(dest_ref, h1_ref, rw_ref, g_ref, y_hbm, o_ref, ybuf, sem):
    tm = COMB_TM
    i = pl.program_id(0)

    def gather(tile, slot):
        def issue(it, carry):
            for u in range(DMA_UNROLL):
                r = it * DMA_UNROLL + u
                t = tile * tm + r
                for kk in range(2):
                    d = pl.multiple_of(dest_ref[kk * SEQ + t] * X_ROWS, X_ROWS)
                    pltpu.make_async_copy(
                        y_hbm.at[pl.ds(d, X_ROWS)],
                        ybuf.at[slot, kk, pl.ds(pl.multiple_of(r * X_ROWS, X_ROWS), X_ROWS)],
                        sem.at[slot, kk]).start(priority=kk)
            return carry

        lax.fori_loop(0, tm // DMA_UNROLL, issue, 0)

    @pl.when(i == 0)
    def _():
        gather(0, 0)

    @pl.when(i + 1 < pl.num_programs(0))
    def _():
        gather(i + 1, (i + 1) % 2)

    slot = i % 2
    for kk in range(2):
        pltpu.make_async_copy(y_hbm.at[pl.ds(0, tm * X_ROWS)], ybuf.at[slot, kk],
                              sem.at[slot, kk]).wait()
    w = rw_ref[...]
    ys = [_unpack_rows([ybuf[slot, kk, pl.ds(c, tm, stride=X_ROWS), :] for c in range(X_ROWS)])
          for kk in range(2)]
    h = h1_ref[...] + w[:, 0:1] * ys[0] + w[:, 1:2] * ys[1]
    var = jnp.mean(h * h, axis=-1, keepdims=True)
    o_ref[...] = h * lax.rsqrt(var + EPS) * g_ref[...]


def _combine(dest_flat, h1, rw, g, y):
    tm = COMB_TM
    return pl.pallas_call(
        _combine_kernel,
        out_shape=jax.ShapeDtypeStruct((SEQ, D_MODEL), F32),
        grid_spec=pltpu.PrefetchScalarGridSpec(
            num_scalar_prefetch=1,
            grid=(SEQ // tm,),
            in_specs=[
                pl.BlockSpec((tm, D_MODEL), lambda i, d: (i, 0)),
                pl.BlockSpec((tm, LANES), lambda i, d: (i, 0)),
                pl.BlockSpec((1, D_MODEL), lambda i, d: (0, 0)),
                pl.BlockSpec(memory_space=pl.ANY),
            ],
            out_specs=pl.BlockSpec((tm, D_MODEL), lambda i, d: (i, 0)),
            scratch_shapes=[
                pltpu.VMEM((2, 2, tm * X_ROWS, LANES), U32),
                pltpu.SemaphoreType.DMA((2, 2)),
            ],
        ),
        compiler_params=_params(("arbitrary",)),
        name="moe_combine",
    )(dest_flat, h1, rw, g, y)


def _attention_tables():
    T = ATT_T
    f32 = np.float32
    slopes = np.exp2(-ALIBI_MAX * np.arange(1, DA_HEADS + 1, dtype=f32) / DA_HEADS).astype(f32)
    r = np.arange(T)
    hi = ((r // CHUNK) * CHUNK).astype(f32)
    lo = (r % CHUNK).astype(f32)
    sl = slopes[:, None]
    one_h = np.ones((DA_HEADS, T), f32)
    q_rows = np.stack([one_h, one_h, -sl * hi[None], -sl * lo[None]], axis=1)
    k_cols = np.stack([sl * hi[None], sl * lo[None], one_h, one_h], axis=-1)
    qa = np.zeros((DA_HEADS, DA_HEAD_DIM, T), f32)
    qa[:, 0:4, :] = q_rows
    ka1 = np.zeros((DA_HEADS, T, LANES), f32)
    ka1[:, :, DA_HEAD_DIM:DA_HEAD_DIM + 4] = k_cols
    ka2 = np.zeros((DA_HEADS, T, LANES), f32)
    ka2[:, :, 0:4] = k_cols
    rel = (r[:, None] - r[None, :]).astype(f32)
    allowed = (r[:, None] // CHUNK) <= (r[None, :] // CHUNK)
    fix = np.where(rel > 0, -2.0 * slopes[:, None, None] * rel[None], 0.0).astype(f32)
    dtab = np.where(allowed[None], fix, -np.inf).astype(f32)
    return (jnp.asarray(slopes), jnp.asarray(qa, BF16), jnp.asarray(ka1, BF16),
            jnp.asarray(ka2, BF16), jnp.asarray(dtab))


def _retention_tables():
    C = RET_C
    f32 = np.float32
    log_gamma = np.log1p(-np.exp2(-5.0 - np.arange(RET_HEADS, dtype=f32))).astype(f32)
    pos = np.arange(C, dtype=f32)
    rel = pos[:, None] - pos[None, :]
    dec = np.where(rel >= 0, np.exp(log_gamma[:, None, None] * np.maximum(rel, 0.0)), 0.0)
    qdec = np.exp(log_gamma[:, None] * (pos + 1.0)[None, :])[:, :, None]
    kdec = np.exp(log_gamma[:, None] * (C - 1 - pos)[None, :])[:, :, None]
    cd = np.exp(log_gamma * C)
    return tuple(jnp.asarray(t, F32) for t in (cd, dec, qdec, kdec))


def kernel(x, attn_norm_g, w_in, da_lambda_q1, da_lambda_k1, da_lambda_q2, da_lambda_k2,
           da_subln_g, w_out, ffn_norm_g, router_group_w, router_group_b, router_expert_w,
           router_expert_b, expert_w_gate, expert_w_up, expert_w_down, final_norm_g):
    B, S, D = x.shape
    assert (B, S, D) == (1, SEQ, D_MODEL)
    x2 = x.reshape(S, D)

    cd, dec, qdec, kdec = _retention_tables()
    slopes, qa, ka1, ka2, dtab = _attention_tables()
    k_da, qt4, vt4, o_r = _inproj(x2, attn_norm_g[0][None, :], w_in[0], cd, dec, qdec, kdec)

    o_da = _attention(k_da, qt4, vt4, slopes, qa, ka1, ka2, dtab, da_lambda_q1, da_lambda_k1,
                      da_lambda_q2, da_lambda_k2, da_subln_g)

    wr = jnp.zeros((D, LANES), F32)
    wr = wr.at[:, :MOE_GROUPS].set(router_group_w[0])
    wr = wr.at[:, MOE_GROUPS:MOE_GROUPS + MOE_EXPERTS].set(router_expert_w[0])
    br = jnp.zeros((1, LANES), F32)
    br = br.at[0, :MOE_GROUPS].set(router_group_b[0])
    br = br.at[0, MOE_GROUPS:MOE_GROUPS + MOE_EXPERTS].set(router_expert_b[0])
    wr_hi = wr.astype(BF16)
    wr_lo = (wr - wr_hi.astype(F32)).astype(BF16)
    h1, xn, ri, rw, cnt = _outproj_router(x2, o_da, o_r, w_out[0],
                                          ffn_norm_g[0][None, :],
                                          jnp.concatenate([wr_hi, wr_lo], axis=1), br)

    dest, meta = _plan(ri, cnt)
    dest_flat = dest[:, 0:2, :].transpose(1, 0, 2).reshape(N_ASSIGN)
    used1 = meta[0, :1]
    counts = meta[1, :MOE_EXPERTS]
    pad_start = meta[2, :MOE_EXPERTS]
    xs = _dispatch(dest_flat, counts, pad_start, used1, xn)

    y = _ffn(counts, pad_start, used1, xs, expert_w_gate[0], expert_w_up[0], expert_w_down[0])
    out = _combine(dest_flat, h1, rw, final_norm_g[None, :], y)
    return out.reshape(B, S, D)
```

```python
import math

import jax
import jax.numpy as jnp
import numpy as np
from jax import lax
from jax.experimental import pallas as pl
from jax.experimental.pallas import tpu as pltpu

F32 = jnp.float32
BF16 = jnp.bfloat16
I32 = jnp.int32
U32 = jnp.uint32

D_MODEL = 1024
SEQ = 16384
CHUNK = 64
EPS = 1e-6

DA_HEADS = 4
DA_HEAD_DIM = 64
DA_V_DIM = 128
DA_WIDTH = 512
ALIBI_MAX = 8.0
RET_HEADS = 4
RET_QK_DIM = 64
RET_V_DIM = 128
RET_WIDTH = 512
W_IN_COLS = 3072
T_ROWS = 512
MAIN_COLS = 2048
DK_OFF = 0
RQ_OFF = 512
RK_OFF = 768
RV_OFF = 1024
RG_OFF = 1536

MOE_GROUPS = 4
MOE_EXPERTS_PER_GROUP = 8
MOE_EXPERTS = 32
MOE_HIDDEN = 512
LAMBDA_INIT = 0.8 - 0.6 * math.exp(-0.3 * 0)

LANES = 128
X_ROWS = 4
VMEM_LIMIT = 56 * 1024 * 1024

PROJ_TM = 512
ROUTER_TM = 1024
ATT_T = 512
RET_C = 256
PLAN_T = 512
FFN_B = 512
N_ASSIGN = 2 * SEQ
N_BLOCKS = N_ASSIGN // FFN_B + MOE_EXPERTS
N_BUF = N_BLOCKS * FFN_B
COMB_TM = 512
DISP_TM = 1024
DMA_UNROLL = 8


def _params(sem):
    return pltpu.CompilerParams(dimension_semantics=sem, vmem_limit_bytes=VMEM_LIMIT)


def _pack_rows(v):
    bits = lax.bitcast_convert_type(v.astype(BF16).astype(F32), U32)
    return [(bits[:, c * LANES:(c + 1) * LANES] >> 16)
            | (bits[:, (c + X_ROWS) * LANES:(c + X_ROWS + 1) * LANES] & jnp.uint32(0xFFFF0000))
            for c in range(X_ROWS)]


def _unpack_rows(words):
    lo = [lax.bitcast_convert_type(w << 16, F32) for w in words]
    hi = [lax.bitcast_convert_type(w & jnp.uint32(0xFFFF0000), F32) for w in words]
    return jnp.concatenate(lo + hi, axis=1)


def _retention_block(q_all, k_all, v_all, g_all, cd_ref, dec_ref, qdec_ref, kdec_ref, st_sc):
    outs = []
    for h in range(RET_HEADS):
        qk = slice(h * RET_QK_DIM, (h + 1) * RET_QK_DIM)
        vv = slice(h * RET_V_DIM, (h + 1) * RET_V_DIM)
        q = q_all[:, qk]
        k = k_all[:, qk]
        v = v_all[:, vv]
        g = g_all[:, vv]
        s = lax.dot_general(q, k, (((1,), (1,)), ((), ())),
                            preferred_element_type=F32) * dec_ref[h]
        intra = jnp.dot(s.astype(BF16), v, preferred_element_type=F32)
        st = st_sc[h]
        cross = jnp.dot(q, st.astype(BF16), preferred_element_type=F32) * qdec_ref[h]
        kd = (k.astype(F32) * kdec_ref[h]).astype(BF16)
        st_sc[h] = st * cd_ref[h] + lax.dot_general(kd, v, (((0,), (0,)), ((), ())),
                                                    preferred_element_type=F32)
        o = intra + cross
        o = o * lax.rsqrt(jnp.mean(o * o, axis=-1, keepdims=True) + EPS)
        outs.append(((g / (1.0 + jnp.exp(-g))) * o).astype(BF16))
    return outs


def _inproj_kernel(cd_ref, x_ref, g_ref, win_ref, dec_ref, qdec_ref, kdec_ref,
                   k_ref, qt_ref, vt_ref, or_ref, st_sc, w_ref, wq_ref, wv_ref):
    @pl.when(pl.program_id(0) == 0)
    def _():
        st_sc[...] = jnp.zeros_like(st_sc)
        cols = DA_WIDTH
        w_ref[:, DK_OFF:RQ_OFF] = win_ref[:, cols:2 * cols].astype(BF16)
        w_ref[:, RQ_OFF:RK_OFF] = win_ref[:, 3 * cols:3 * cols + 256].astype(BF16)
        w_ref[:, RK_OFF:RV_OFF] = (win_ref[:, 3 * cols + 256:4 * cols]
                                   * (RET_QK_DIM ** -0.5)).astype(BF16)
        w_ref[:, RV_OFF:MAIN_COLS] = win_ref[:, 4 * cols:6 * cols].astype(BF16)
        step = 256
        for r in range(D_MODEL // step):
            rows = slice(r * step, (r + 1) * step)
            wq_ref[:, rows] = (win_ref[rows, 0:cols] * (DA_HEAD_DIM ** -0.5)).T.astype(BF16)
            wv_ref[:, rows] = win_ref[rows, 2 * cols:3 * cols].T.astype(BF16)

    x = x_ref[...]
    var = jnp.mean(x * x, axis=-1, keepdims=True)
    xn = (x * lax.rsqrt(var + EPS) * g_ref[...]).astype(BF16)

    def proj(lo, hi):
        return jnp.dot(xn, w_ref[:, lo:hi], preferred_element_type=F32)

    k_ref[...] = proj(DK_OFF, DK_OFF + DA_WIDTH).astype(BF16)
    nt = (((1,), (1,)), ((), ()))
    qt = lax.dot_general(wq_ref[...], xn, nt, preferred_element_type=F32)
    qt_ref[...] = qt.astype(BF16).reshape(DA_HEADS, 1, 2 * DA_HEAD_DIM, PROJ_TM)
    vt = lax.dot_general(wv_ref[...], xn, nt, preferred_element_type=F32)
    vt_ref[...] = vt.astype(BF16).reshape(DA_HEADS, 1, DA_V_DIM, PROJ_TM)

    rq = proj(RQ_OFF, RK_OFF).astype(BF16)
    rk = proj(RK_OFF, RV_OFF).astype(BF16)
    rv = proj(RV_OFF, RG_OFF).astype(BF16)
    rg = proj(RG_OFF, MAIN_COLS)
    for blk in range(PROJ_TM // RET_C):
        rows = slice(blk * RET_C, (blk + 1) * RET_C)
        outs = _retention_block(rq[rows], rk[rows], rv[rows], rg[rows],
                                cd_ref, dec_ref, qdec_ref, kdec_ref, st_sc)
        for h in range(RET_HEADS):
            or_ref[rows, h * RET_V_DIM:(h + 1) * RET_V_DIM] = outs[h]


def _inproj(x2, g, w_in, cd, dec, qdec, kdec):
    C = RET_C
    t_shape = jax.ShapeDtypeStruct((DA_HEADS, SEQ // PROJ_TM, LANES, PROJ_TM), BF16)
    t_spec = pl.BlockSpec((DA_HEADS, 1, LANES, PROJ_TM), lambda i: (0, i, 0, 0))
    return pl.pallas_call(
        _inproj_kernel,
        out_shape=(jax.ShapeDtypeStruct((SEQ, DA_WIDTH), BF16), t_shape, t_shape,
                   jax.ShapeDtypeStruct((SEQ, RET_WIDTH), BF16)),
        grid=(SEQ // PROJ_TM,),
        in_specs=[
            pl.BlockSpec(memory_space=pltpu.SMEM),
            pl.BlockSpec((PROJ_TM, D_MODEL), lambda i: (i, 0)),
            pl.BlockSpec((1, D_MODEL), lambda i: (0, 0)),
            pl.BlockSpec((D_MODEL, W_IN_COLS), lambda i: (0, 0), pipeline_mode=pl.Buffered(1)),
            pl.BlockSpec((RET_HEADS, C, C), lambda i: (0, 0, 0)),
            pl.BlockSpec((RET_HEADS, C, 1), lambda i: (0, 0, 0)),
            pl.BlockSpec((RET_HEADS, C, 1), lambda i: (0, 0, 0)),
        ],
        out_specs=(pl.BlockSpec((PROJ_TM, DA_WIDTH), lambda i: (i, 0)), t_spec, t_spec,
                   pl.BlockSpec((PROJ_TM, RET_WIDTH), lambda i: (i, 0))),
        scratch_shapes=[
            pltpu.VMEM((RET_HEADS, RET_QK_DIM, RET_V_DIM), F32),
            pltpu.VMEM((D_MODEL, MAIN_COLS), BF16),
            pltpu.VMEM((T_ROWS, D_MODEL), BF16),
            pltpu.VMEM((T_ROWS, D_MODEL), BF16),
        ],
        compiler_params=_params(("arbitrary",)),
        name="inproj_retention",
    )(cd, x2, g, w_in, dec, qdec, kdec)


ACC_ROWS = DA_V_DIM + 16


N_QT = SEQ // ATT_T
N_OFF = N_QT * (N_QT - 1) // 2


def _pipeline3(n_pos, scores, accumulate, rot=0, prologue=True, overlap_next=()):
    def buf(k):
        return (k + rot) % 3

    if prologue:
        scores(0, buf(0))
        scores(1, buf(1))
    steady = n_pos - 2

    def triple(k, carry):
        t = 3 * k
        accumulate(t, buf(0))
        scores(t + 2, buf(2))
        accumulate(t + 1, buf(1))
        scores(t + 3, buf(0))
        accumulate(t + 2, buf(2))
        scores(t + 4, buf(1))
        return carry

    lax.fori_loop(0, steady // 3, triple, 0)
    t0 = steady // 3 * 3
    for idx, pos in enumerate(range(t0, n_pos)):
        accumulate(pos, buf(idx))
        if pos + 2 < n_pos:
            scores(pos + 2, buf(idx + 2))
        elif pos + 2 - n_pos < len(overlap_next):
            overlap_next[pos + 2 - n_pos]()


def _attn_kernel(slope_ref, jt_ref, it_ref, qt_ref, k_ref, vt_ref, qa_ref, ka1_ref, ka2_ref,
                 dtab_ref, lq1_ref, lk1_ref, lq2_ref, lk2_ref, g_ref, o_ref,
                 m_sc, acc_sc, s0_sc, s1_sc, s2_sc, mx0_sc, mx1_sc, mx2_sc):
    T = ATT_T
    h = pl.program_id(0)
    slope = slope_ref[h]
    qa = qa_ref[0]
    lane = lax.broadcasted_iota(I32, (T, LANES), 1)
    sums_row = (lax.broadcasted_iota(I32, (16, T), 0) == 0).astype(BF16)
    s_bufs = (s0_sc, s1_sc, s2_sc)
    mx_bufs = (mx0_sc, mx1_sc, mx2_sc)

    HALF = T // 2

    def operands(j, i):
        kt = k_ref[pl.ds(pl.multiple_of(j * T, T), T), :]
        ks = (jnp.where(lane < DA_HEAD_DIM, kt, ka1_ref[0]),
              jnp.where(lane >= DA_HEAD_DIM, kt, ka2_ref[0]))
        qt = qt_ref[0, i]
        qw = (jnp.concatenate([qt[0:DA_HEAD_DIM], qa], axis=0),
              jnp.concatenate([qa, qt[DA_HEAD_DIM:]], axis=0))
        return ks, qw

    def scores(j, i, buf):
        ks, qw = operands(j, i)
        for mp in range(2):
            s = jnp.dot(ks[mp], qw[mp], preferred_element_type=F32)
            s_bufs[buf][mp] = s
            mx_bufs[buf][mp] = jnp.max(s, axis=0, keepdims=True)

    def scores_diag(i, buf):
        ks, qw = operands(i, i)
        for mp in range(2):
            left = (jnp.dot(ks[mp][0:HALF], qw[mp][:, 0:HALF], preferred_element_type=F32)
                    + dtab_ref[0, 0:HALF, 0:HALF])
            right = (jnp.dot(ks[mp], qw[mp][:, HALF:], preferred_element_type=F32)
                     + dtab_ref[0, :, HALF:])
            s_bufs[buf][mp, 0:HALF, 0:HALF] = left
            s_bufs[buf][mp, :, HALF:] = right
            mx_bufs[buf][mp] = jnp.concatenate(
                [jnp.max(left, axis=0, keepdims=True), jnp.max(right, axis=0, keepdims=True)],
                axis=1)

    def update(i, mp, m_prev, m_new, pv):
        acc_sc[i, mp] = jnp.exp(m_prev - m_new) * acc_sc[i, mp] + pv
        m_sc[i, mp] = m_new

    def accumulate(j, i, buf):
        c = slope * lax.convert_element_type((i - j) * T, F32)
        vte = jnp.concatenate([vt_ref[0, j], sums_row], axis=0)
        for mp in range(2):
            m_prev = m_sc[i, mp]
            m_new = jnp.maximum(m_prev, mx_bufs[buf][mp] - c)
            p = jnp.exp(s_bufs[buf][mp] - (m_new + c)).astype(BF16)
            update(i, mp, m_prev, m_new, jnp.dot(vte, p, preferred_element_type=F32))

    def accumulate_diag(i, buf):
        vte = jnp.concatenate([vt_ref[0, i], sums_row], axis=0)
        for mp in range(2):
            m_new = mx_bufs[buf][mp]
            p_left = jnp.exp(s_bufs[buf][mp, 0:HALF, 0:HALF] - m_new[:, 0:HALF]).astype(BF16)
            p_right = jnp.exp(s_bufs[buf][mp, :, HALF:] - m_new[:, HALF:]).astype(BF16)
            pv = jnp.concatenate(
                [jnp.dot(vte[:, 0:HALF], p_left, preferred_element_type=F32),
                 jnp.dot(vte, p_right, preferred_element_type=F32)], axis=1)
            acc_sc[i, mp] = pv
            m_sc[i, mp] = m_new

    def off_scores(pos, buf):
        scores(jt_ref[pos], it_ref[pos], buf)

    def off_accumulate(pos, buf):
        accumulate(jt_ref[pos], it_ref[pos], buf)

    off_rot = (N_QT - 2) % 3 + 2
    _pipeline3(N_QT, scores_diag, accumulate_diag,
               overlap_next=(lambda: off_scores(0, off_rot % 3),
                             lambda: off_scores(1, (off_rot + 1) % 3)))
    _pipeline3(N_OFF, off_scores, off_accumulate, rot=off_rot, prologue=False)

    lam = (jnp.exp(jnp.sum(lq1_ref[...] * lk1_ref[...], axis=1, keepdims=True))
           - jnp.exp(jnp.sum(lq2_ref[...] * lk2_ref[...], axis=1, keepdims=True))
           + LAMBDA_INIT)

    def finish(i):
        a1 = acc_sc[i, 0]
        a2 = acc_sc[i, 1]
        ot = (a1[0:DA_V_DIM] / a1[DA_V_DIM:DA_V_DIM + 1]
              - lam * (a2[0:DA_V_DIM] / a2[DA_V_DIM:DA_V_DIM + 1]))
        o = ot.T
        var = jnp.mean(o * o, axis=-1, keepdims=True)
        o = (o * lax.rsqrt(var + EPS) * g_ref[...]) * (1.0 - LAMBDA_INIT)
        o_ref[pl.ds(pl.multiple_of(i * T, T), T), :] = o.astype(BF16)

    def finish_pair(k, carry):
        finish(2 * k)
        finish(2 * k + 1)
        return carry

    lax.fori_loop(0, N_QT // 2, finish_pair, 0)


def _attention(proj, qt4, vt4, slopes, qa, ka1, ka2, dtab, lq1, lk1, lq2, lk2, subln_g):
    T = ATT_T
    vec64 = pl.BlockSpec((1, DA_HEAD_DIM), lambda h: (0, 0))
    per_head = lambda a, b: pl.BlockSpec((1, a, b), lambda h: (h, 0, 0))
    slab = pl.BlockSpec
    single = lambda shape, imap: pl.BlockSpec(shape, imap, pipeline_mode=pl.Buffered(1))
    smem = pl.BlockSpec(memory_space=pltpu.SMEM)
    it_tab, jt_tab = np.tril_indices(N_QT, -1)
    return pl.pallas_call(
        _attn_kernel,
        out_shape=jax.ShapeDtypeStruct((SEQ, DA_WIDTH), BF16),
        grid=(DA_HEADS,),
        in_specs=[
            smem, smem, smem,
            slab((1, N_QT, LANES, T), lambda h: (h, 0, 0, 0)),
            slab((SEQ, LANES), lambda h: (0, DK_OFF // LANES + h)),
            slab((1, N_QT, LANES, T), lambda h: (h, 0, 0, 0)),
            per_head(DA_HEAD_DIM, T), per_head(T, LANES), per_head(T, LANES),
            single((1, T, T), lambda h: (h, 0, 0)),
            vec64, vec64, vec64, vec64,
            pl.BlockSpec((1, DA_V_DIM), lambda h: (0, 0)),
        ],
        out_specs=single((SEQ, LANES), lambda h: (0, h)),
        scratch_shapes=[
            pltpu.VMEM((N_QT, 2, 1, T), F32),
            pltpu.VMEM((N_QT, 2, ACC_ROWS, T), F32),
            pltpu.VMEM((2, T, T), F32),
            pltpu.VMEM((2, T, T), F32),
            pltpu.VMEM((2, T, T), F32),
            pltpu.VMEM((2, 1, T), F32),
            pltpu.VMEM((2, 1, T), F32),
            pltpu.VMEM((2, 1, T), F32),
        ],
        compiler_params=_params(("arbitrary",)),
        name="diff_attention",
    )(slopes, jnp.asarray(jt_tab, I32), jnp.asarray(it_tab, I32), qt4, proj, vt4, qa, ka1, ka2,
      dtab, lq1, lk1, lq2, lk2, subln_g)


def _outproj_router_kernel(x_ref, oda_ref, or_ref, wo32_ref, g_ref, wr_ref, br_ref,
                           h1_ref, xn_ref, ri_ref, rw_ref, cnt_ref, wo_ref):
    @pl.when(pl.program_id(0) == 0)
    def _():
        wo_ref[...] = wo32_ref[...].astype(BF16)

    h1 = (x_ref[...]
          + jnp.dot(oda_ref[...], wo_ref[0:DA_WIDTH, :], preferred_element_type=F32)
          + jnp.dot(or_ref[...], wo_ref[DA_WIDTH:, :], preferred_element_type=F32))
    h1_ref[...] = h1
    var = jnp.mean(h1 * h1, axis=-1, keepdims=True)
    xn = h1 * lax.rsqrt(var + EPS) * g_ref[...]
    for c, words in enumerate(_pack_rows(xn)):
        xn_ref[pl.ds(c, ROUTER_TM, stride=X_ROWS), :] = words
    x_hi = xn.astype(BF16)
    x_lo = (xn - x_hi.astype(F32)).astype(BF16)
    both = jnp.dot(x_hi, wr_ref[...], preferred_element_type=F32)
    logits = (both[:, :LANES] + both[:, LANES:]
              + jnp.dot(x_lo, wr_ref[:, :LANES], preferred_element_type=F32)) + br_ref[...]
    lane = lax.broadcasted_iota(I32, logits.shape, 1)
    neg = jnp.float32(-jnp.inf)
    big = jnp.int32(1 << 20)
    gl = jnp.where(lane < MOE_GROUPS, logits, neg)
    gmax = jnp.max(gl, axis=1, keepdims=True)
    gidx = jnp.min(jnp.where(gl == gmax, lane, big), axis=1, keepdims=True)
    gsum = jnp.sum(jnp.exp(gl - gmax), axis=1, keepdims=True)
    gp = 1.0 / gsum
    lo = MOE_GROUPS + gidx * MOE_EXPERTS_PER_GROUP
    el = jnp.where((lane >= lo) & (lane < lo + MOE_EXPERTS_PER_GROUP), logits, neg)
    v1 = jnp.max(el, axis=1, keepdims=True)
    i1 = jnp.min(jnp.where(el == v1, lane, big), axis=1, keepdims=True)
    el2 = jnp.where(lane == i1, neg, el)
    v2 = jnp.max(el2, axis=1, keepdims=True)
    i2 = jnp.min(jnp.where(el2 == v2, lane, big), axis=1, keepdims=True)
    t = jnp.exp(v2 - v1)
    w1 = gp / (1.0 + t)
    w2 = gp * t / (1.0 + t)
    ri_ref[...] = jnp.where(lane == 0, i1 - MOE_GROUPS,
                            jnp.where(lane == 1, i2 - MOE_GROUPS, 0))
    rw_ref[...] = jnp.where(lane == 0, w1, jnp.where(lane == 1, w2, 0.0))

    @pl.when(pl.program_id(0) == 0)
    def _():
        cnt_ref[...] = jnp.zeros_like(cnt_ref)

    chosen = (lane == i1 - MOE_GROUPS) | (lane == i2 - MOE_GROUPS)
    cnt_ref[...] += jnp.sum(chosen.astype(F32), axis=0, keepdims=True)


def _outproj_router(x2, o_da, o_r, w_out, g, wr, br):
    tm = ROUTER_TM
    row = lambda w: pl.BlockSpec((tm, w), lambda i: (i, 0))
    full = lambda a, b: pl.BlockSpec((a, b), lambda i: (0, 0))
    return pl.pallas_call(
        _outproj_router_kernel,
        out_shape=(
            jax.ShapeDtypeStruct((SEQ, D_MODEL), F32),
            jax.ShapeDtypeStruct((SEQ * X_ROWS, LANES), U32),
            jax.ShapeDtypeStruct((SEQ, LANES), I32),
            jax.ShapeDtypeStruct((SEQ, LANES), F32),
            jax.ShapeDtypeStruct((8, LANES), F32),
        ),
        grid=(SEQ // tm,),
        in_specs=[row(D_MODEL), row(DA_WIDTH), row(RET_WIDTH),
                  pl.BlockSpec((D_MODEL, D_MODEL), lambda i: (0, 0),
                               pipeline_mode=pl.Buffered(1)),
                  full(1, D_MODEL), full(D_MODEL, 2 * LANES), full(1, LANES)],
        out_specs=(row(D_MODEL), pl.BlockSpec((tm * X_ROWS, LANES), lambda i: (i, 0)),
                   row(LANES), row(LANES), full(8, LANES)),
        scratch_shapes=[pltpu.VMEM((D_MODEL, D_MODEL), BF16)],
        compiler_params=_params(("arbitrary",)),
        name="outproj_router",
    )(x2, o_da, o_r, w_out, g, wr, br)


def _plan_kernel(ri_ref, cnt_ref, dest_ref, used_ref):
    TT = PLAN_T
    lane = lax.broadcasted_iota(I32, (TT, LANES), 1)

    def onehots(t):
        r = ri_ref[pl.ds(pl.multiple_of(t * TT, TT), TT), :]
        return lane == r[:, 0:1], lane == r[:, 1:2]

    counts8 = cnt_ref[...].astype(I32)
    shift = FFN_B.bit_length() - 1
    padded = ((counts8 + (FFN_B - 1)) >> shift) << shift
    lane8 = lax.broadcasted_iota(I32, (8, LANES), 1)
    pad_end = padded
    sh = 1
    while sh < LANES:
        pad_end = pad_end + jnp.where(lane8 >= sh, pltpu.roll(pad_end, sh, axis=1), 0)
        sh *= 2
    pad_start = pad_end - padded

    ltri = (lax.broadcasted_iota(I32, (TT, TT), 0)
            > lax.broadcasted_iota(I32, (TT, TT), 1)).astype(BF16)

    def dest_body(t, carry):
        oh1, oh2 = onehots(t)
        a = (oh1 | oh2).astype(F32)
        base = jnp.dot(ltri, a.astype(BF16), preferred_element_type=F32) + carry
        d1 = jnp.sum(jnp.where(oh1, base, 0.0), axis=1, keepdims=True)
        d2 = jnp.sum(jnp.where(oh2, base, 0.0), axis=1, keepdims=True)
        both = jnp.where(lane == 0, d1, jnp.where(lane == 1, d2, 0.0))
        dest_ref[t] = both.T[0:8, :].astype(I32)
        return carry + jnp.sum(a, axis=0, keepdims=True)

    lax.fori_loop(0, SEQ // TT, dest_body, pad_start[0:1].astype(F32))

    total = jnp.max(pad_end, axis=1, keepdims=True)
    row8 = lax.broadcasted_iota(I32, (8, LANES), 0)
    used_ref[...] = jnp.where(row8 == 0, jnp.broadcast_to(total >> shift, (8, LANES)),
                              jnp.where(row8 == 1, counts8, pad_start))


def _plan(ri, cnt):
    return pl.pallas_call(
        _plan_kernel,
        out_shape=(
            jax.ShapeDtypeStruct((SEQ // PLAN_T, 8, PLAN_T), I32),
            jax.ShapeDtypeStruct((8, LANES), I32),
        ),
        compiler_params=pltpu.CompilerParams(vmem_limit_bytes=VMEM_LIMIT),
        name="route_plan",
    )(ri, cnt)


PAD_BITS = FFN_B.bit_length() - 1


def _pad_fill_copies(e, cnt_ref, pst_ref, zero_sc, xs_hbm, zsem):
    cnt = cnt_ref[e]
    pad = (-cnt) & (FFN_B - 1)
    row = pst_ref[e] + cnt
    out = []
    for bit in reversed(range(PAD_BITS)):
        n = 1 << bit
        start = row + ((pad >> (bit + 1)) << (bit + 1))
        copy = pltpu.make_async_copy(
            zero_sc.at[pl.ds(0, n * X_ROWS)],
            xs_hbm.at[pl.ds(pl.multiple_of(start * X_ROWS, X_ROWS), n * X_ROWS)], zsem)
        out.append(((pad & n) != 0, copy))
    return out


def _unused_block_copies(b, zero_sc, xs_hbm, zsem):
    half = FFN_B // 2 * X_ROWS
    return [pltpu.make_async_copy(
        zero_sc, xs_hbm.at[pl.ds(pl.multiple_of((2 * b + k) * half, half), half)], zsem)
        for k in range(2)]


def _dispatch_kernel(dest_ref, cnt_ref, pst_ref, used_ref, xn_ref, xs_hbm, zero_sc, sem, zsem):
    tm = DISP_TM
    i = pl.program_id(0)

    @pl.when(i == 0)
    def _():
        zero_sc[...] = jnp.zeros_like(zero_sc)

        def fill(e, carry):
            for cond, copy in _pad_fill_copies(e, cnt_ref, pst_ref, zero_sc, xs_hbm, zsem):
                pl.when(cond)(copy.start)
            return carry

        lax.fori_loop(0, MOE_EXPERTS, fill, 0)

        def fill_block(b, carry):
            for copy in _unused_block_copies(b, zero_sc, xs_hbm, zsem):
                copy.start()
            return carry

        lax.fori_loop(used_ref[0], N_BLOCKS, fill_block, 0)

    def issue(it, carry):
        for u in range(DMA_UNROLL):
            r = it * DMA_UNROLL + u
            t = i * tm + r
            src = xn_ref.at[pl.ds(pl.multiple_of(r * X_ROWS, X_ROWS), X_ROWS)]
            for kk in range(2):
                d = pl.multiple_of(dest_ref[kk * SEQ + t] * X_ROWS, X_ROWS)
                pltpu.make_async_copy(src, xs_hbm.at[pl.ds(d, X_ROWS)], sem).start(priority=kk)
        return carry

    lax.fori_loop(0, tm // DMA_UNROLL, issue, 0)
    for _ in range(2):
        pltpu.make_async_copy(xn_ref, xs_hbm.at[pl.ds(0, tm * X_ROWS)], sem).wait()

    @pl.when(i == 0)
    def _():
        def drain(e, carry):
            for cond, copy in _pad_fill_copies(e, cnt_ref, pst_ref, zero_sc, xs_hbm, zsem):
                pl.when(cond)(copy.wait)
            return carry

        lax.fori_loop(0, MOE_EXPERTS, drain, 0)

        def drain_block(b, carry):
            for copy in _unused_block_copies(b, zero_sc, xs_hbm, zsem):
                copy.wait()
            return carry

        lax.fori_loop(used_ref[0], N_BLOCKS, drain_block, 0)


def _dispatch(dest_flat, counts, pad_start, used, xn3):
    tm = DISP_TM
    return pl.pallas_call(
        _dispatch_kernel,
        out_shape=jax.ShapeDtypeStruct((N_BUF * X_ROWS, LANES), U32),
        grid_spec=pltpu.PrefetchScalarGridSpec(
            num_scalar_prefetch=4,
            grid=(SEQ // tm,),
            in_specs=[pl.BlockSpec((tm * X_ROWS, LANES), lambda i, d, c, p, u: (i, 0))],
            out_specs=pl.BlockSpec(memory_space=pl.ANY),
            scratch_shapes=[
                pltpu.VMEM((FFN_B // 2 * X_ROWS, LANES), U32),
                pltpu.SemaphoreType.DMA(()),
                pltpu.SemaphoreType.DMA(()),
            ],
        ),
        compiler_params=_params(("arbitrary",)),
        name="moe_dispatch",
    )(dest_flat, counts, pad_start, used, xn3)


BLOCK_COPY_PRIORITY = 1


def _ffn_kernel(cnt_ref, pst_ref, used_ref, xs_hbm, wg_ref, wu_ref, wd_ref, y_hbm,
                xbuf, ybuf, zero_sc, wg_bf, wu_bf, wd_bf, sem_in, sem_out, zsem):
    B = FFN_B
    e = pl.program_id(0)
    n = (cnt_ref[e] + (B - 1)) >> PAD_BITS
    s0 = pst_ref[e] >> PAD_BITS

    def rows(blk):
        size = B * X_ROWS
        return pl.ds(pl.multiple_of(blk * size, size), size)

    def fetch(blk, slot):
        return pltpu.make_async_copy(xs_hbm.at[rows(blk)], xbuf.at[slot], sem_in.at[slot])

    def flush(blk, slot):
        return pltpu.make_async_copy(ybuf.at[slot], y_hbm.at[rows(blk)], sem_out.at[slot])

    used = used_ref[0]

    @pl.when(e == 0)
    def _():
        fetch(0, 0).start(priority=BLOCK_COPY_PRIORITY)

    @pl.when(n > 0)
    def _():
        wg_bf[...] = wg_ref[0].astype(BF16)
        wu_bf[...] = wu_ref[0].astype(BF16)
        wd_bf[...] = wd_ref[0].astype(BF16)

    def body(j, carry):
        blk = s0 + j
        slot = blk % 2

        @pl.when(blk + 1 < used)
        def _():
            fetch(blk + 1, 1 - slot).start(priority=BLOCK_COPY_PRIORITY)

        fetch(blk, slot).wait()

        @pl.when(blk >= 2)
        def _():
            flush(blk - 2, slot).wait()

        def swiglu(rows):
            x = _unpack_rows([xbuf[slot, pl.ds(c, rows, stride=X_ROWS), :]
                              for c in range(X_ROWS)]).astype(BF16)
            hg = jnp.dot(x, wg_bf[...], preferred_element_type=F32)
            hu = jnp.dot(x, wu_bf[...], preferred_element_type=F32)
            hh = ((hg / (1.0 + jnp.exp(-hg))) * hu).astype(BF16)
            y = jnp.dot(hh, wd_bf[...], preferred_element_type=F32)
            for c, words in enumerate(_pack_rows(y)):
                ybuf[slot, pl.ds(c, rows, stride=X_ROWS), :] = words

        valid = cnt_ref[e] - j * B

        @pl.when(valid > B // 2)
        def _():
            swiglu(B)

        @pl.when(valid <= B // 2)
        def _():
            swiglu(B // 2)
            tail = pl.ds(B // 2 * X_ROWS, B // 2 * X_ROWS)
            ybuf[slot, tail, :] = jnp.zeros((B // 2 * X_ROWS, LANES), U32)

        flush(blk, slot).start(priority=BLOCK_COPY_PRIORITY)
        return carry

    lax.fori_loop(0, n, body, 0)

    @pl.when(e == MOE_EXPERTS - 1)
    def _():
        flush(used - 1, (used + 1) % 2).wait()
        flush(used - 2, used % 2).wait()
        zero_sc[...] = jnp.zeros_like(zero_sc)

        def fill(b, carry):
            pltpu.make_async_copy(zero_sc, y_hbm.at[rows(b)], zsem).start()
            return carry

        def drain(b, carry):
            pltpu.make_async_copy(zero_sc, y_hbm.at[rows(b)], zsem).wait()
            return carry

        lax.fori_loop(used_ref[0], N_BLOCKS, fill, 0)
        lax.fori_loop(used_ref[0], N_BLOCKS, drain, 0)


def _ffn(counts, pad_start, used, xs, w_gate, w_up, w_down):
    B = FFN_B
    wspec = lambda a, c: pl.BlockSpec((1, a, c), lambda e, cnt, pst, used: (e, 0, 0))
    return pl.pallas_call(
        _ffn_kernel,
        out_shape=jax.ShapeDtypeStruct((N_BUF * X_ROWS, LANES), U32),
        grid_spec=pltpu.PrefetchScalarGridSpec(
            num_scalar_prefetch=3,
            grid=(MOE_EXPERTS,),
            in_specs=[
                pl.BlockSpec(memory_space=pl.ANY),
                wspec(D_MODEL, MOE_HIDDEN),
                wspec(D_MODEL, MOE_HIDDEN),
                wspec(MOE_HIDDEN, D_MODEL),
            ],
            out_specs=pl.BlockSpec(memory_space=pl.ANY),
            scratch_shapes=[
                pltpu.VMEM((2, B * X_ROWS, LANES), U32),
                pltpu.VMEM((2, B * X_ROWS, LANES), U32),
                pltpu.VMEM((B * X_ROWS, LANES), U32),
                pltpu.VMEM((D_MODEL, MOE_HIDDEN), BF16),
                pltpu.VMEM((D_MODEL, MOE_HIDDEN), BF16),
                pltpu.VMEM((MOE_HIDDEN, D_MODEL), BF16),
                pltpu.SemaphoreType.DMA((2,)),
                pltpu.SemaphoreType.DMA((2,)),
                pltpu.SemaphoreType.DMA(()),
            ],
        ),
        compiler_params=_params(("arbitrary",)),
        name="expert_ffn",
    )(counts, pad_start, used, xs, w_gate, w_up, w_down)


def _combine_kernel(dest_ref, h1_ref, rw_ref, g_ref, y_hbm, o_ref, ybuf, sem):
    tm = COMB_TM
    i = pl.program_id(0)

    def gather(tile, slot):
        def issue(it, carry):
            for u in range(DMA_UNROLL):
                r = it * DMA_UNROLL + u
                t = tile * tm + r
                for kk in range(2):
                    d = pl.multiple_of(dest_ref[kk * SEQ + t] * X_ROWS, X_ROWS)
                    pltpu.make_async_copy(
                        y_hbm.at[pl.ds(d, X_ROWS)],
                        ybuf.at[slot, kk, pl.ds(pl.multiple_of(r * X_ROWS, X_ROWS), X_ROWS)],
                        sem.at[slot, kk]).start(priority=kk)
            return carry

        lax.fori_loop(0, tm // DMA_UNROLL, issue, 0)

    @pl.when(i == 0)
    def _():
        gather(0, 0)

    @pl.when(i + 1 < pl.num_programs(0))
    def _():
        gather(i + 1, (i + 1) % 2)

    slot = i % 2
    for kk in range(2):
        pltpu.make_async_copy(y_hbm.at[pl.ds(0, tm * X_ROWS)], ybuf.at[slot, kk],
                              sem.at[slot, kk]).wait()
    w = rw_ref[...]
    ys = [_unpack_rows([ybuf[slot, kk, pl.ds(c, tm, stride=X_ROWS), :] for c in range(X_ROWS)])
          for kk in range(2)]
    h = h1_ref[...] + w[:, 0:1] * ys[0] + w[:, 1:2] * ys[1]
    var = jnp.mean(h * h, axis=-1, keepdims=True)
    o_ref[...] = h * lax.rsqrt(var + EPS) * g_ref[...]


def _combine(dest_flat, h1, rw, g, y):
    tm = COMB_TM
    return pl.pallas_call(
        _combine_kernel,
        out_shape=jax.ShapeDtypeStruct((SEQ, D_MODEL), F32),
        grid_spec=pltpu.PrefetchScalarGridSpec(
            num_scalar_prefetch=1,
            grid=(SEQ // tm,),
            in_specs=[
                pl.BlockSpec((tm, D_MODEL), lambda i, d: (i, 0)),
                pl.BlockSpec((tm, LANES), lambda i, d: (i, 0)),
                pl.BlockSpec((1, D_MODEL), lambda i, d: (0, 0)),
                pl.BlockSpec(memory_space=pl.ANY),
            ],
            out_specs=pl.BlockSpec((tm, D_MODEL), lambda i, d: (i, 0)),
            scratch_shapes=[
                pltpu.VMEM((2, 2, tm * X_ROWS, LANES), U32),
                pltpu.SemaphoreType.DMA((2, 2)),
            ],
        ),
        compiler_params=_params(("arbitrary",)),
        name="moe_combine",
    )(dest_flat, h1, rw, g, y)


def _attention_tables():
    T = ATT_T
    f32 = np.float32
    slopes = np.exp2(-ALIBI_MAX * np.arange(1, DA_HEADS + 1, dtype=f32) / DA_HEADS).astype(f32)
    r = np.arange(T)
    hi = ((r // CHUNK) * CHUNK).astype(f32)
    lo = (r % CHUNK).astype(f32)
    sl = slopes[:, None]
    one_h = np.ones((DA_HEADS, T), f32)
    q_rows = np.stack([one_h, one_h, -sl * hi[None], -sl * lo[None]], axis=1)
    k_cols = np.stack([sl * hi[None], sl * lo[None], one_h, one_h], axis=-1)
    qa = np.zeros((DA_HEADS, DA_HEAD_DIM, T), f32)
    qa[:, 0:4, :] = q_rows
    ka1 = np.zeros((DA_HEADS, T, LANES), f32)
    ka1[:, :, DA_HEAD_DIM:DA_HEAD_DIM + 4] = k_cols
    ka2 = np.zeros((DA_HEADS, T, LANES), f32)
    ka2[:, :, 0:4] = k_cols
    rel = (r[:, None] - r[None, :]).astype(f32)
    allowed = (r[:, None] // CHUNK) <= (r[None, :] // CHUNK)
    fix = np.where(rel > 0, -2.0 * slopes[:, None, None] * rel[None], 0.0).astype(f32)
    dtab = np.where(allowed[None], fix, -np.inf).astype(f32)
    return (jnp.asarray(slopes), jnp.asarray(qa, BF16), jnp.asarray(ka1, BF16),
            jnp.asarray(ka2, BF16), jnp.asarray(dtab))


def _retention_tables():
    C = RET_C
    f32 = np.float32
    log_gamma = np.log1p(-np.exp2(-5.0 - np.arange(RET_HEADS, dtype=f32))).astype(f32)
    pos = np.arange(C, dtype=f32)
    rel = pos[:, None] - pos[None, :]
    dec = np.where(rel >= 0, np.exp(log_gamma[:, None, None] * np.maximum(rel, 0.0)), 0.0)
    qdec = np.exp(log_gamma[:, None] * (pos + 1.0)[None, :])[:, :, None]
    kdec = np.exp(log_gamma[:, None] * (C - 1 - pos)[None, :])[:, :, None]
    cd = np.exp(log_gamma * C)
    return tuple(jnp.asarray(t, F32) for t in (cd, dec, qdec, kdec))


def kernel(x, attn_norm_g, w_in, da_lambda_q1, da_lambda_k1, da_lambda_q2, da_lambda_k2,
           da_subln_g, w_out, ffn_norm_g, router_group_w, router_group_b, router_expert_w,
           router_expert_b, expert_w_gate, expert_w_up, expert_w_down, final_norm_g):
    B, S, D = x.shape
    assert (B, S, D) == (1, SEQ, D_MODEL)
    x2 = x.reshape(S, D)

    cd, dec, qdec, kdec = _retention_tables()
    slopes, qa, ka1, ka2, dtab = _attention_tables()
    k_da, qt4, vt4, o_r = _inproj(x2, attn_norm_g[0][None, :], w_in[0], cd, dec, qdec, kdec)

    o_da = _attention(k_da, qt4, vt4, slopes, qa, ka1, ka2, dtab, da_lambda_q1, da_lambda_k1,
                      da_lambda_q2, da_lambda_k2, da_subln_g)

    wr = jnp.zeros((D, LANES), F32)
    wr = wr.at[:, :MOE_GROUPS].set(router_group_w[0])
    wr = wr.at[:, MOE_GROUPS:MOE_GROUPS + MOE_EXPERTS].set(router_expert_w[0])
    br = jnp.zeros((1, LANES), F32)
    br = br.at[0, :MOE_GROUPS].set(router_group_b[0])
    br = br.at[0, MOE_GROUPS:MOE_GROUPS + MOE_EXPERTS].set(router_expert_b[0])
    wr_hi = wr.astype(BF16)
    wr_lo = (wr - wr_hi.astype(F32)).astype(BF16)
    h1, xn, ri, rw, cnt = _outproj_router(x2, o_da, o_r, w_out[0],
                                          ffn_norm_g[0][None, :],
                                          jnp.concatenate([wr_hi, wr_lo], axis=1), br)

    dest, meta = _plan(ri, cnt)
    dest_flat = dest[:, 0:2, :].transpose(1, 0, 2).reshape(N_ASSIGN)
    used1 = meta[0, :1]
    counts = meta[1, :MOE_EXPERTS]
    pad_start = meta[2, :MOE_EXPERTS]
    xs = _dispatch(dest_flat, counts, pad_start, used1, xn)

    y = _ffn(counts, pad_start, used1, xs, expert_w_gate[0], expert_w_up[0], expert_w_down[0])
    out = _combine(dest_flat, h1, rw, final_norm_g[None, :], y)
    return out.reshape(B, S, D)
```
